```python
import jax, jax.numpy as jnp
from jax import lax
import numpy as np

D_MODEL = 2048
BATCH = 8
SEQ = 2048
DEPTH = 2

HEAD_DIM = 128
SB_WIDTH = D_MODEL // 2
N_SB_HEADS = SB_WIDTH // HEAD_DIM
SGU_WIDTH = D_MODEL - SB_WIDTH
N_SGU_GROUPS = 8
SGU_GROUP_DIM = SGU_WIDTH // N_SGU_GROUPS
MIX_WIDTH = SB_WIDTH + SGU_WIDTH
IN_WIDTH = 3 * SB_WIDTH + 2 * SGU_WIDTH
SB_BLOCK = 128
CHUNK = 128
D_FF = 5632
CONV_WIDTH = 3
EPS = 1e-6

kernel_name = "hybrid_stickbreaking_sgu_convffn"


def rms_norm(x, g):
    xf = x.astype(jnp.float32)
    y = xf * lax.rsqrt(jnp.mean(xf * xf, axis=-1, keepdims=True) + EPS)
    return (y * g.astype(jnp.float32)).astype(x.dtype)


def stick_breaking_attention(q, k, v):
    B, S, H, Dh = q.shape
    scale = Dh ** -0.5
    outs = []
    for i in range(S // SB_BLOCK):
        q0 = i * SB_BLOCK
        kv_len = q0 + SB_BLOCK
        q_blk = q[:, q0:kv_len]
        k_blk = k[:, :kv_len]
        v_blk = v[:, :kv_len]
        z = jnp.einsum('bthd,bshd->bhts', q_blk, k_blk,
                       preferred_element_type=jnp.float32) * scale
        t_idx = q0 + jnp.arange(SB_BLOCK)[:, None]
        s_idx = jnp.arange(kv_len)[None, :]
        mask = s_idx < t_idx
        log_beta = jax.nn.log_sigmoid(z)
        log_1m_beta = jnp.where(mask, jax.nn.log_sigmoid(-z), 0.0)
        tail = lax.cumsum(log_1m_beta, axis=3, reverse=True) - log_1m_beta
        a = jnp.where(mask, jnp.exp(log_beta + tail), 0.0)
        o = jnp.einsum('bhts,bshd->bthd', a.astype(v.dtype), v_blk)
        outs.append(o)
    return jnp.concatenate(outs, axis=1)


def chunked_spatial_gating(u, v, w_s, b_s):
    B, S, G, dg = v.shape
    n_chunks = S // CHUNK
    causal = jnp.tril(jnp.ones((CHUNK, CHUNK), dtype=bool))
    w = jnp.where(causal[None], w_s, 0.0).astype(v.dtype)
    vc = v.reshape(B, n_chunks, CHUNK, G, dg)
    mixed = jnp.einsum('gts,bcsgd->bctgd', w, vc) + b_s.T[None, None, :, :, None]
    return u * mixed.reshape(B, S, G, dg)


def causal_depthwise_conv(h, w, b):
    S = h.shape[1]
    hp = jnp.pad(h, ((0, 0), (CONV_WIDTH - 1, 0), (0, 0)))
    out = b
    for j in range(CONV_WIDTH):
        out = out + hp[:, j:j + S] * w[j]
    return out


def _fwd_setup_inputs(seed: int = 0) -> dict:
    key = jax.random.key(seed)
    ks = jax.random.split(key, 20)
    f32 = jnp.float32
    nrm = lambda k, shape, s: jax.random.normal(k, shape, f32) * s
    gain = lambda k, shape: 1.0 + 0.02 * jax.random.normal(k, shape, f32)
    return {
        "x": jax.random.normal(ks[0], (BATCH, SEQ, D_MODEL), f32),
        "attn_norm_g": gain(ks[1], (DEPTH, D_MODEL)),
        "w_in": nrm(ks[2], (DEPTH, D_MODEL, IN_WIDTH), D_MODEL ** -0.5),
        "q_norm_g": gain(ks[3], (DEPTH, HEAD_DIM)),
        "k_norm_g": gain(ks[4], (DEPTH, HEAD_DIM)),
        "sgu_norm_g": gain(ks[5], (DEPTH, N_SGU_GROUPS, SGU_GROUP_DIM)),
        "sgu_w": nrm(ks[6], (DEPTH, N_SGU_GROUPS, CHUNK, CHUNK), CHUNK ** -0.5),
        "sgu_b": 1.0 + 0.02 * jax.random.normal(ks[7], (DEPTH, N_SGU_GROUPS, CHUNK), f32),
        "out_norm_a_g": gain(ks[8], (DEPTH, N_SB_HEADS, HEAD_DIM)),
        "out_norm_b_g": gain(ks[9], (DEPTH, N_SGU_GROUPS, SGU_GROUP_DIM)),
        "w_out": nrm(ks[10], (DEPTH, MIX_WIDTH, D_MODEL), (2 * DEPTH * MIX_WIDTH) ** -0.5),
        "ffn_norm_g": gain(ks[11], (DEPTH, D_MODEL)),
        "w_up": nrm(ks[12], (DEPTH, D_MODEL, 2 * D_FF), D_MODEL ** -0.5),
        "conv_w": nrm(ks[13], (DEPTH, CONV_WIDTH, 2 * D_FF), CONV_WIDTH ** -0.5),
        "conv_b": nrm(ks[14], (DEPTH, 2 * D_FF), 0.02),
        "w_down": nrm(ks[15], (DEPTH, D_FF, D_MODEL), (2 * DEPTH * D_FF) ** -0.5),
    }


def _fwd_reference(x, attn_norm_g, w_in, q_norm_g, k_norm_g, sgu_norm_g, sgu_w, sgu_b,
              out_norm_a_g, out_norm_b_g, w_out, ffn_norm_g, w_up, conv_w, conv_b, w_down):
    B, S, _ = x.shape
    for l in range(DEPTH):
        h = rms_norm(x, attn_norm_g[l])
        p = h @ w_in[l]
        q, k, va, u_pre, v_pre = jnp.split(
            p, [SB_WIDTH, 2 * SB_WIDTH, 3 * SB_WIDTH, 3 * SB_WIDTH + SGU_WIDTH], axis=-1)
        q = rms_norm(q.reshape(B, S, N_SB_HEADS, HEAD_DIM), q_norm_g[l])
        k = rms_norm(k.reshape(B, S, N_SB_HEADS, HEAD_DIM), k_norm_g[l])
        va = va.reshape(B, S, N_SB_HEADS, HEAD_DIM)
        att = stick_breaking_attention(q, k, va)
        att = rms_norm(att, out_norm_a_g[l])

        u = jax.nn.gelu(u_pre, approximate=False).reshape(B, S, N_SGU_GROUPS, SGU_GROUP_DIM)
        vs = jax.nn.gelu(v_pre, approximate=False).reshape(B, S, N_SGU_GROUPS, SGU_GROUP_DIM)
        vs = rms_norm(vs, sgu_norm_g[l])
        sg = chunked_spatial_gating(u, vs, sgu_w[l], sgu_b[l])
        sg = rms_norm(sg, out_norm_b_g[l])

        mix = jnp.concatenate([att.reshape(B, S, SB_WIDTH),
                               sg.reshape(B, S, SGU_WIDTH)], axis=-1)
        x = x + mix @ w_out[l]

        h = rms_norm(x, ffn_norm_g[l])
        up = causal_depthwise_conv(h @ w_up[l], conv_w[l], conv_b[l])
        gate, val = jnp.split(up, 2, axis=-1)
        x = x + (jax.nn.silu(gate) * val) @ w_down[l]
    return x


import jax as _jax
import jax.numpy as _jnp

TWIN_FORMAT = 'train_step'
FWD_PARAMS = ['x', 'attn_norm_g', 'w_in', 'q_norm_g', 'k_norm_g', 'sgu_norm_g', 'sgu_w', 'sgu_b', 'out_norm_a_g', 'out_norm_b_g', 'w_out', 'ffn_norm_g', 'w_up', 'conv_w', 'conv_b', 'w_down']
TWIN_WEIGHTS = ['attn_norm_g', 'w_in', 'q_norm_g', 'k_norm_g', 'sgu_norm_g', 'sgu_w', 'sgu_b', 'out_norm_a_g', 'out_norm_b_g', 'w_out', 'ffn_norm_g', 'w_up', 'conv_w', 'conv_b', 'w_down']
TWIN_DIFF_INPUT = 'x'
TWIN_INPUTS = ['x', 'attn_norm_g', 'w_in', 'q_norm_g', 'k_norm_g', 'sgu_norm_g', 'sgu_w', 'sgu_b', 'out_norm_a_g', 'out_norm_b_g', 'w_out', 'ffn_norm_g', 'w_up', 'conv_w', 'conv_b', 'w_down', 'loss_target', 'm_attn_norm_g', 'm_w_in', 'm_q_norm_g', 'm_k_norm_g', 'm_sgu_norm_g', 'm_sgu_w', 'm_sgu_b', 'm_out_norm_a_g', 'm_out_norm_b_g', 'm_w_out', 'm_ffn_norm_g', 'm_w_up', 'm_conv_w', 'm_conv_b', 'm_w_down', 'v_attn_norm_g', 'v_w_in', 'v_q_norm_g', 'v_k_norm_g', 'v_sgu_norm_g', 'v_sgu_w', 'v_sgu_b', 'v_out_norm_a_g', 'v_out_norm_b_g', 'v_w_out', 'v_ffn_norm_g', 'v_w_up', 'v_conv_w', 'v_conv_b', 'v_w_down']
TWIN_OUTPUTS = ['loss', 'grad_x', 'grad_attn_norm_g', 'grad_w_in', 'grad_q_norm_g', 'grad_k_norm_g', 'grad_sgu_norm_g', 'grad_sgu_w', 'grad_sgu_b', 'grad_out_norm_a_g', 'grad_out_norm_b_g', 'grad_w_out', 'grad_ffn_norm_g', 'grad_w_up', 'grad_conv_w', 'grad_conv_b', 'grad_w_down', 'delta_attn_norm_g', 'delta_w_in', 'delta_q_norm_g', 'delta_k_norm_g', 'delta_sgu_norm_g', 'delta_sgu_w', 'delta_sgu_b', 'delta_out_norm_a_g', 'delta_out_norm_b_g', 'delta_w_out', 'delta_ffn_norm_g', 'delta_w_up', 'delta_conv_w', 'delta_conv_b', 'delta_w_down', 'new_m_attn_norm_g', 'new_m_w_in', 'new_m_q_norm_g', 'new_m_k_norm_g', 'new_m_sgu_norm_g', 'new_m_sgu_w', 'new_m_sgu_b', 'new_m_out_norm_a_g', 'new_m_out_norm_b_g', 'new_m_w_out', 'new_m_ffn_norm_g', 'new_m_w_up', 'new_m_conv_w', 'new_m_conv_b', 'new_m_w_down', 'new_v_attn_norm_g', 'new_v_w_in', 'new_v_q_norm_g', 'new_v_k_norm_g', 'new_v_sgu_norm_g', 'new_v_sgu_w', 'new_v_sgu_b', 'new_v_out_norm_a_g', 'new_v_out_norm_b_g', 'new_v_w_out', 'new_v_ffn_norm_g', 'new_v_w_up', 'new_v_conv_w', 'new_v_conv_b', 'new_v_w_down']
TWIN_LEAF_KINDS = {'loss': 'loss', 'grad_x': 'grad_x', 'grad_attn_norm_g': 'grad_w', 'grad_w_in': 'grad_w', 'grad_q_norm_g': 'grad_w', 'grad_k_norm_g': 'grad_w', 'grad_sgu_norm_g': 'grad_w', 'grad_sgu_w': 'grad_w', 'grad_sgu_b': 'grad_w', 'grad_out_norm_a_g': 'grad_w', 'grad_out_norm_b_g': 'grad_w', 'grad_w_out': 'grad_w', 'grad_ffn_norm_g': 'grad_w', 'grad_w_up': 'grad_w', 'grad_conv_w': 'grad_w', 'grad_conv_b': 'grad_w', 'grad_w_down': 'grad_w', 'delta_attn_norm_g': 'delta_w', 'delta_w_in': 'delta_w', 'delta_q_norm_g': 'delta_w', 'delta_k_norm_g': 'delta_w', 'delta_sgu_norm_g': 'delta_w', 'delta_sgu_w': 'delta_w', 'delta_sgu_b': 'delta_w', 'delta_out_norm_a_g': 'delta_w', 'delta_out_norm_b_g': 'delta_w', 'delta_w_out': 'delta_w', 'delta_ffn_norm_g': 'delta_w', 'delta_w_up': 'delta_w', 'delta_conv_w': 'delta_w', 'delta_conv_b': 'delta_w', 'delta_w_down': 'delta_w', 'new_m_attn_norm_g': 'new_m', 'new_m_w_in': 'new_m', 'new_m_q_norm_g': 'new_m', 'new_m_k_norm_g': 'new_m', 'new_m_sgu_norm_g': 'new_m', 'new_m_sgu_w': 'new_m', 'new_m_sgu_b': 'new_m', 'new_m_out_norm_a_g': 'new_m', 'new_m_out_norm_b_g': 'new_m', 'new_m_w_out': 'new_m', 'new_m_ffn_norm_g': 'new_m', 'new_m_w_up': 'new_m', 'new_m_conv_w': 'new_m', 'new_m_conv_b': 'new_m', 'new_m_w_down': 'new_m', 'new_v_attn_norm_g': 'new_v', 'new_v_w_in': 'new_v', 'new_v_q_norm_g': 'new_v', 'new_v_k_norm_g': 'new_v', 'new_v_sgu_norm_g': 'new_v', 'new_v_sgu_w': 'new_v', 'new_v_sgu_b': 'new_v', 'new_v_out_norm_a_g': 'new_v', 'new_v_out_norm_b_g': 'new_v', 'new_v_w_out': 'new_v', 'new_v_ffn_norm_g': 'new_v', 'new_v_w_up': 'new_v', 'new_v_conv_w': 'new_v', 'new_v_conv_b': 'new_v', 'new_v_w_down': 'new_v'}


def _forward(args):
    return _fwd_reference(*[args[k] for k in FWD_PARAMS])


def _output_shape():
    out = _jax.eval_shape(lambda: _forward(_fwd_setup_inputs(0)))
    return out.shape, out.dtype

N_MICROBATCH = 1
ADAM_LR = 0.001
ADAM_B1 = 0.9
ADAM_B2 = 0.999
ADAM_EPS = 1e-08
ADAM_WD = 0.01
ADAM_STEP = 10
PER_EXAMPLE_BATCH_AXIS = {'x': 0, 'loss_target': 0}
SHARED_INPUTS = []
_WEIGHT_DTYPES = {'attn_norm_g': _jnp.float32, 'w_in': _jnp.float32, 'q_norm_g': _jnp.float32, 'k_norm_g': _jnp.float32, 'sgu_norm_g': _jnp.float32, 'sgu_w': _jnp.float32, 'sgu_b': _jnp.float32, 'out_norm_a_g': _jnp.float32, 'out_norm_b_g': _jnp.float32, 'w_out': _jnp.float32, 'ffn_norm_g': _jnp.float32, 'w_up': _jnp.float32, 'conv_w': _jnp.float32, 'conv_b': _jnp.float32, 'w_down': _jnp.float32}
MOMENT_SCALE = {'attn_norm_g': 1.011258e-01, 'w_in': 6.418776e-02, 'q_norm_g': 8.518756e-02, 'k_norm_g': 8.527731e-02, 'sgu_norm_g': 4.617190e-02, 'sgu_w': 3.608463e-02, 'sgu_b': 3.387744e-02, 'out_norm_a_g': 1.994095e+00, 'out_norm_b_g': 2.017897e+00, 'w_out': 4.382300e-01, 'ffn_norm_g': 1.659039e+00, 'w_up': 4.107872e-02, 'conv_w': 2.321114e-01, 'conv_b': 2.219239e-01, 'w_down': 8.774648e-02}


def _to_microbatches(a, axis):
    t = _jnp.moveaxis(a, axis, 0)
    t = t.reshape((N_MICROBATCH, t.shape[0] // N_MICROBATCH) + t.shape[1:])
    return _jnp.moveaxis(t, 1, axis + 1)


def setup_inputs(seed: int = 0) -> dict:
    inp = _fwd_setup_inputs(seed)
    key = _jax.random.fold_in(_jax.random.key(seed), 7919)
    shape, _ = _output_shape()
    out = dict(inp)
    out["loss_target"] = _jax.random.normal(_jax.random.fold_in(key, 0), shape, _jnp.float32)
    for i, name in enumerate(TWIN_WEIGHTS):
        w = inp[name].astype(_jnp.float32)
        if MOMENT_SCALE is None:
            s = _jnp.sqrt(_jnp.mean(_jnp.square(w)) + 1e-30)
        else:
            s = MOMENT_SCALE[name]
        km, kv = _jax.random.split(_jax.random.fold_in(key, i + 1))
        out[name] = w
        out["m_" + name] = s * _jax.random.normal(km, w.shape, _jnp.float32)
        out["v_" + name] = (s * s) * _jax.random.uniform(kv, w.shape, _jnp.float32, 0.5, 1.5)
    if N_MICROBATCH > 1:
        for name, axis in PER_EXAMPLE_BATCH_AXIS.items():
            out[name] = _to_microbatches(out[name], axis)
    return {'x': out['x'], 'attn_norm_g': out['attn_norm_g'], 'w_in': out['w_in'], 'q_norm_g': out['q_norm_g'], 'k_norm_g': out['k_norm_g'], 'sgu_norm_g': out['sgu_norm_g'], 'sgu_w': out['sgu_w'], 'sgu_b': out['sgu_b'], 'out_norm_a_g': out['out_norm_a_g'], 'out_norm_b_g': out['out_norm_b_g'], 'w_out': out['w_out'], 'ffn_norm_g': out['ffn_norm_g'], 'w_up': out['w_up'], 'conv_w': out['conv_w'], 'conv_b': out['conv_b'], 'w_down': out['w_down'], 'loss_target': out['loss_target'], 'm_attn_norm_g': out['m_attn_norm_g'], 'm_w_in': out['m_w_in'], 'm_q_norm_g': out['m_q_norm_g'], 'm_k_norm_g': out['m_k_norm_g'], 'm_sgu_norm_g': out['m_sgu_norm_g'], 'm_sgu_w': out['m_sgu_w'], 'm_sgu_b': out['m_sgu_b'], 'm_out_norm_a_g': out['m_out_norm_a_g'], 'm_out_norm_b_g': out['m_out_norm_b_g'], 'm_w_out': out['m_w_out'], 'm_ffn_norm_g': out['m_ffn_norm_g'], 'm_w_up': out['m_w_up'], 'm_conv_w': out['m_conv_w'], 'm_conv_b': out['m_conv_b'], 'm_w_down': out['m_w_down'], 'v_attn_norm_g': out['v_attn_norm_g'], 'v_w_in': out['v_w_in'], 'v_q_norm_g': out['v_q_norm_g'], 'v_k_norm_g': out['v_k_norm_g'], 'v_sgu_norm_g': out['v_sgu_norm_g'], 'v_sgu_w': out['v_sgu_w'], 'v_sgu_b': out['v_sgu_b'], 'v_out_norm_a_g': out['v_out_norm_a_g'], 'v_out_norm_b_g': out['v_out_norm_b_g'], 'v_w_out': out['v_w_out'], 'v_ffn_norm_g': out['v_ffn_norm_g'], 'v_w_up': out['v_w_up'], 'v_conv_w': out['v_conv_w'], 'v_conv_b': out['v_conv_b'], 'v_w_down': out['v_w_down']}


def _loss(weights, diff, rest, loss_target):
    with _jax.named_scope("forward"):
        args = {**rest, TWIN_DIFF_INPUT: diff, **{k: w.astype(_WEIGHT_DTYPES[k]) for k, w in weights.items()}}
        y = _forward(args)
    with _jax.named_scope("loss_head"):
        err = _jnp.square(y.astype(_jnp.float32) - loss_target)
        return 0.5 * _jnp.sum(_jnp.mean(err, axis=-1)) if err.ndim else 0.5 * err


def _adamw(w, g, m, v):
    m = ADAM_B1 * m + (1.0 - ADAM_B1) * g
    v = ADAM_B2 * v + (1.0 - ADAM_B2) * _jnp.square(g)
    m_hat = m / (1.0 - ADAM_B1 ** ADAM_STEP)
    v_hat = v / (1.0 - ADAM_B2 ** ADAM_STEP)
    delta = -ADAM_LR * (m_hat / (_jnp.sqrt(v_hat) + ADAM_EPS) + ADAM_WD * w)
    return delta, m, v


def reference(x, attn_norm_g, w_in, q_norm_g, k_norm_g, sgu_norm_g, sgu_w, sgu_b, out_norm_a_g, out_norm_b_g, w_out, ffn_norm_g, w_up, conv_w, conv_b, w_down, loss_target, m_attn_norm_g, m_w_in, m_q_norm_g, m_k_norm_g, m_sgu_norm_g, m_sgu_w, m_sgu_b, m_out_norm_a_g, m_out_norm_b_g, m_w_out, m_ffn_norm_g, m_w_up, m_conv_w, m_conv_b, m_w_down, v_attn_norm_g, v_w_in, v_q_norm_g, v_k_norm_g, v_sgu_norm_g, v_sgu_w, v_sgu_b, v_out_norm_a_g, v_out_norm_b_g, v_w_out, v_ffn_norm_g, v_w_up, v_conv_w, v_conv_b, v_w_down):
    given = dict(x=x, attn_norm_g=attn_norm_g, w_in=w_in, q_norm_g=q_norm_g, k_norm_g=k_norm_g, sgu_norm_g=sgu_norm_g, sgu_w=sgu_w, sgu_b=sgu_b, out_norm_a_g=out_norm_a_g, out_norm_b_g=out_norm_b_g, w_out=w_out, ffn_norm_g=ffn_norm_g, w_up=w_up, conv_w=conv_w, conv_b=conv_b, w_down=w_down, loss_target=loss_target, m_attn_norm_g=m_attn_norm_g, m_w_in=m_w_in, m_q_norm_g=m_q_norm_g, m_k_norm_g=m_k_norm_g, m_sgu_norm_g=m_sgu_norm_g, m_sgu_w=m_sgu_w, m_sgu_b=m_sgu_b, m_out_norm_a_g=m_out_norm_a_g, m_out_norm_b_g=m_out_norm_b_g, m_w_out=m_w_out, m_ffn_norm_g=m_ffn_norm_g, m_w_up=m_w_up, m_conv_w=m_conv_w, m_conv_b=m_conv_b, m_w_down=m_w_down, v_attn_norm_g=v_attn_norm_g, v_w_in=v_w_in, v_q_norm_g=v_q_norm_g, v_k_norm_g=v_k_norm_g, v_sgu_norm_g=v_sgu_norm_g, v_sgu_w=v_sgu_w, v_sgu_b=v_sgu_b, v_out_norm_a_g=v_out_norm_a_g, v_out_norm_b_g=v_out_norm_b_g, v_w_out=v_w_out, v_ffn_norm_g=v_ffn_norm_g, v_w_up=v_w_up, v_conv_w=v_conv_w, v_conv_b=v_conv_b, v_w_down=v_w_down)
    weights = {n: given[n] for n in TWIN_WEIGHTS}
    shared = {n: given[n] for n in SHARED_INPUTS}
    per_example = {n: given[n] for n in ['x']}
    grad_fn = _jax.value_and_grad(_loss, argnums=(0, 1))

    def one_microbatch(ex, loss_target):
        ex = dict(ex)
        diff = ex.pop(TWIN_DIFF_INPUT)
        return grad_fn(weights, diff, {**shared, **ex}, loss_target)

    if N_MICROBATCH == 1:
        loss, (grad_w, grad_x) = one_microbatch(per_example, given["loss_target"])
    else:
        def body(carry, xs):
            loss_sum, grad_sum = carry
            l_k, (gw_k, gx_k) = one_microbatch(xs[0], xs[1])
            with _jax.named_scope("update"):
                return (loss_sum + l_k, _jax.tree.map(_jnp.add, grad_sum, gw_k)), gx_k

        init = (_jnp.zeros((), _jnp.float32), _jax.tree.map(_jnp.zeros_like, weights))
        (loss, grad_w), grad_x = _jax.lax.scan(body, init, (per_example, given["loss_target"]))
    with _jax.named_scope("update"):
        delta_w, new_m, new_v = {}, {}, {}
        for n in TWIN_WEIGHTS:
            delta_w[n], new_m[n], new_v[n] = _adamw(weights[n], grad_w[n], given["m_" + n], given["v_" + n])
    return (loss, grad_x, *[grad_w[n] for n in TWIN_WEIGHTS], *[delta_w[n] for n in TWIN_WEIGHTS],
            *[new_m[n] for n in TWIN_WEIGHTS], *[new_v[n] for n in TWIN_WEIGHTS])
```

```python
import functools
import math

import jax
import jax.numpy as jnp
from jax import lax
from jax.experimental import pallas as pl
from jax.experimental.pallas import tpu as pltpu

F32 = jnp.float32
BF16 = jnp.bfloat16
EPS = 1e-6
BLK = 128
N_CHIPS = 4
N_DEV = 8
ADAM_LR, ADAM_B1, ADAM_B2, ADAM_EPS, ADAM_WD, ADAM_STEP = 0.001, 0.9, 0.999, 1e-08, 0.01, 10
VMEM_BIG = 48 * 1024 * 1024
MESH = pl.DeviceIdType.MESH
ANY = pl.BlockSpec(memory_space=pl.ANY)


def _pick(dim, prefs):
    for t in prefs:
        if dim % t == 0:
            return t
    raise ValueError(f"no tile in {prefs} divides {dim}")


def _params(sem=None, vmem=None):
    return pltpu.CompilerParams(dimension_semantics=sem, vmem_limit_bytes=vmem)


def _ldims(arr, split):
    if split == 1:
        return arr.shape
    p, r, cs = arr.shape
    assert p == split
    return (r, p * cs)


def _spec(tr, tc, split, cols, rc):
    if split == 1:
        return pl.BlockSpec((tr, tc), lambda i, j, k: rc(i, j, k))
    per = (cols // split) // tc

    def imap(i, j, k):
        r, c = rc(i, j, k)
        return (c // per, r, c % per)

    return pl.BlockSpec((None, tr, tc), imap)


def _mm(a, b, mode, *, name, a_split=1, b_split=1, o_split=1, out_dtype=F32, res=None):
    ar, ac = _ldims(a, a_split)
    br, bc = _ldims(b, b_split)
    if mode == "nn":
        m, k, n = ar, ac, bc
        assert br == k
        ku, nu, mu = math.gcd(k // a_split, k), math.gcd(n // b_split, n // o_split), m
    elif mode == "nt":
        m, k, n = ar, ac, br
        assert bc == k
        ku, nu, mu = math.gcd(k // a_split, k // b_split), n // o_split, m
    else:
        k, m, n = ar, ac, bc
        assert br == k
        ku, nu, mu = k, math.gcd(n // b_split, n // o_split), m // a_split
    tm, tn, tk = _pick(mu, (512, 256, 128)), _pick(nu, (512, 256, 128)), _pick(ku, (512, 256, 128))
    nk = k // tk
    if mode == "nn":
        a_spec = _spec(tm, tk, a_split, k, lambda i, j, kk: (i, kk))
        b_spec = _spec(tk, tn, b_split, n, lambda i, j, kk: (kk, j))
    elif mode == "nt":
        a_spec = _spec(tm, tk, a_split, k, lambda i, j, kk: (i, kk))
        b_spec = _spec(tn, tk, b_split, k, lambda i, j, kk: (j, kk))
    else:
        a_spec = _spec(tk, tm, a_split, m, lambda i, j, kk: (kk, i))
        b_spec = _spec(tk, tn, b_split, n, lambda i, j, kk: (kk, j))
    o_spec = _spec(tm, tn, o_split, n, lambda i, j, kk: (i, j))
    dims = {"nn": (((1,), (0,)), ((), ())), "nt": (((1,), (1,)), ((), ())), "tn": (((0,), (0,)), ((), ()))}[mode]

    def body(a_ref, b_ref, *rest):
        if res is None:
            o_ref, acc = rest
        else:
            r_ref, o_ref, acc = rest
        kk = pl.program_id(2)

        @pl.when(kk == 0)
        def _():
            acc[...] = jnp.zeros_like(acc)

        acc[...] += lax.dot_general(a_ref[...].astype(BF16), b_ref[...].astype(BF16), dims, preferred_element_type=F32)

        @pl.when(kk == nk - 1)
        def _():
            out = acc[...]
            if res is not None:
                out = out + r_ref[...]
            o_ref[...] = out.astype(o_ref.dtype)

    in_specs, args = [a_spec, b_spec], [a, b]
    if res is not None:
        in_specs.append(pl.BlockSpec((tm, tn), lambda i, j, kk: (i, j)))
        args.append(res)
    out_shape = (m, n) if o_split == 1 else (o_split, m, n // o_split)
    return pl.pallas_call(
        body, name=name, grid=(m // tm, n // tn, nk), in_specs=in_specs, out_specs=o_spec,
        out_shape=jax.ShapeDtypeStruct(out_shape, out_dtype), scratch_shapes=[pltpu.VMEM((tm, tn), F32)],
        compiler_params=_params(("parallel", "parallel", "arbitrary")),
    )(*args)


def _rstd(v):
    return lax.rsqrt(jnp.mean(v * v, axis=-1, keepdims=True) + EPS)


def _norm_bwd(v, r, gain, dout):
    a = dout * gain
    dv = r * (a - v * (r * r * jnp.mean(a * v, axis=-1, keepdims=True)))
    return dv, dout * v * r


def _rmsnorm_fwd(x, g, *, name):
    s, d = x.shape
    tr = _pick(s, (256, 128))

    def body(x_ref, g_ref, o_ref):
        v = x_ref[...]
        o_ref[...] = (v * _rstd(v) * g_ref[...]).astype(o_ref.dtype)

    return pl.pallas_call(
        body, name=name, grid=(s // tr,),
        in_specs=[pl.BlockSpec((tr, d), lambda i: (i, 0)), pl.BlockSpec((1, d), lambda i: (0, 0))],
        out_specs=pl.BlockSpec((tr, d), lambda i: (i, 0)), out_shape=jax.ShapeDtypeStruct((s, d), BF16),
        compiler_params=_params(("parallel",)),
    )(x, g.reshape(1, d))


def _rmsnorm_bwd(x, g, dh, dres, *, name):
    s, d = x.shape
    tr = _pick(s, (256, 128))

    def body(x_ref, g_ref, dh_ref, dres_ref, dx_ref, dg_ref):
        v = x_ref[...]
        dv, dgr = _norm_bwd(v, _rstd(v), g_ref[...], dh_ref[...])
        dx_ref[...] = dres_ref[...] + dv
        part = jnp.sum(dgr, axis=0, keepdims=True)

        @pl.when(pl.program_id(0) == 0)
        def _():
            dg_ref[...] = part

        @pl.when(pl.program_id(0) > 0)
        def _():
            dg_ref[...] += part

    row = pl.BlockSpec((tr, d), lambda i: (i, 0))
    one = pl.BlockSpec((1, d), lambda i: (0, 0))
    return pl.pallas_call(
        body, name=name, grid=(s // tr,), in_specs=[row, one, row, row], out_specs=[row, one],
        out_shape=[jax.ShapeDtypeStruct((s, d), F32), jax.ShapeDtypeStruct((1, d), F32)],
        compiler_params=_params(("arbitrary",)),
    )(x, g.reshape(1, d), dh, dres)


def _loss_head(y, target, *, name):
    s, d = y.shape
    tr = _pick(s, (256, 128))

    def body(y_ref, t_ref, dy_ref, ls_ref):
        e = y_ref[...] - t_ref[...]
        dy_ref[...] = e * (1.0 / d)
        part = jnp.full(ls_ref.shape, jnp.sum(e * e), F32)

        @pl.when(pl.program_id(0) == 0)
        def _():
            ls_ref[...] = part

        @pl.when(pl.program_id(0) > 0)
        def _():
            ls_ref[...] += part

    row = pl.BlockSpec((tr, d), lambda i: (i, 0))
    return pl.pallas_call(
        body, name=name, grid=(s // tr,), in_specs=[row, row], out_specs=[row, pl.BlockSpec((8, 128), lambda i: (0, 0))],
        out_shape=[jax.ShapeDtypeStruct((s, d), F32), jax.ShapeDtypeStruct((8, 128), F32)],
        compiler_params=_params(("arbitrary",)),
    )(y, target)


def _iota2(axis):
    return lax.broadcasted_iota(jnp.int32, (BLK, BLK), axis)


def _tri_sum(v, tri):
    hi = v.astype(BF16)
    lo = (v - hi.astype(F32)).astype(BF16)
    return jnp.dot(hi, tri, preferred_element_type=F32) + jnp.dot(lo, tri, preferred_element_type=F32)


def _dot_nt(a, b):
    return lax.dot_general(a, b, (((1,), (1,)), ((), ())), preferred_element_type=F32)


def _dot_tn(a, b):
    return lax.dot_general(a, b, (((0,), (0,)), ((), ())), preferred_element_type=F32)


def _sb_logits(qi, kj, i, j, scale):
    z = _dot_nt(qi, kj) * scale
    mask = (j * BLK + _iota2(1)) < (i * BLK + _iota2(0))
    lb = jnp.minimum(z, 0.0) - jnp.log(1.0 + jnp.exp(-jnp.abs(z)))
    l1m = jnp.where(mask, lb - z, 0.0)
    return z, lb, l1m, mask


def _attn_fwd(p, gq, gk, go, n_heads, *, name):
    s = p.shape[0]
    nb = s // BLK
    scale = BLK ** -0.5

    def body(q_ref, k_ref, v_ref, gq_ref, gk_ref, go_ref, att_ref, o_ref, l_ref, qn, kn, vb):
        q = q_ref[...]
        k = k_ref[...]
        qn[...] = (q * _rstd(q) * gq_ref[...]).astype(BF16)
        kn[...] = (k * _rstd(k) * gk_ref[...]).astype(BF16)
        vb[...] = v_ref[...].astype(BF16)
        tri_gt = (_iota2(0) > _iota2(1)).astype(BF16)

        def q_block(i, _):
            rows = pl.ds(pl.multiple_of(i * BLK, BLK), BLK)
            qi = qn[rows, :]

            def k_block(jj, carry):
                later, acc = carry
                j = i - jj
                cols = pl.ds(pl.multiple_of(j * BLK, BLK), BLK)
                _, lb, l1m, mask = _sb_logits(qi, kn[cols, :], i, j, scale)
                tail = _tri_sum(l1m, tri_gt) + later
                a = jnp.where(mask, jnp.exp(lb + tail), 0.0)
                acc = acc + jnp.dot(a.astype(BF16), vb[cols, :], preferred_element_type=F32)
                return later + jnp.sum(l1m, axis=1, keepdims=True), acc

            total, acc = lax.fori_loop(0, i + 1, k_block, (jnp.zeros((BLK, 1), F32), jnp.zeros((BLK, BLK), F32)))
            o_ref[rows, :] = acc
            l_ref[rows, :] = total
            att_ref[rows, :] = (acc * _rstd(acc) * go_ref[...]).astype(att_ref.dtype)
            return 0

        lax.fori_loop(0, nb, q_block, 0)

    def col(off):
        return pl.BlockSpec((s, BLK), lambda h: (0, off + h))

    gain = pl.BlockSpec((1, BLK), lambda h: (0, 0))
    per_head = pl.BlockSpec((None, 1, BLK), lambda h: (h, 0, 0))
    return pl.pallas_call(
        body, name=name, grid=(n_heads,),
        in_specs=[col(0), col(n_heads), col(2 * n_heads), gain, gain, per_head],
        out_specs=[col(0), col(0), pl.BlockSpec((None, s, 1), lambda h: (h, 0, 0))],
        out_shape=[jax.ShapeDtypeStruct((s, n_heads * BLK), BF16), jax.ShapeDtypeStruct((s, n_heads * BLK), F32),
                   jax.ShapeDtypeStruct((n_heads, s, 1), F32)],
        scratch_shapes=[pltpu.VMEM((s, BLK), BF16)] * 3,
        compiler_params=_params(("parallel",), VMEM_BIG),
    )(p, p, p, gq.reshape(1, BLK), gk.reshape(1, BLK), go.reshape(n_heads, 1, BLK))


def _attn_bwd(p, o_raw, lsum, dmix, gq, gk, go, n_heads, *, name):
    s = p.shape[0]
    nb = s // BLK
    scale = BLK ** -0.5

    def body(q_ref, k_ref, v_ref, o_ref, l_ref, da_ref, gq_ref, gk_ref, go_ref,
             dqkv_ref, dgq_ref, dgk_ref, dgo_ref, qn, kn, vb, dob, dqn, dkn, dvv):
        q = q_ref[...]
        k = k_ref[...]
        rq = _rstd(q)
        rk = _rstd(k)
        qn[...] = (q * rq * gq_ref[...]).astype(BF16)
        kn[...] = (k * rk * gk_ref[...]).astype(BF16)
        vb[...] = v_ref[...].astype(BF16)
        o = o_ref[...]
        do, dgo_rows = _norm_bwd(o, _rstd(o), go_ref[...], da_ref[...])
        dob[...] = do.astype(BF16)
        dgo_ref[...] = jnp.sum(dgo_rows, axis=0, keepdims=True)
        dkn[...] = jnp.zeros_like(dkn)
        dvv[...] = jnp.zeros_like(dvv)
        tri_gt = (_iota2(0) > _iota2(1)).astype(BF16)
        tri_lt = (_iota2(0) < _iota2(1)).astype(BF16)

        def q_block(i, _):
            rows = pl.ds(pl.multiple_of(i * BLK, BLK), BLK)
            qi = qn[rows, :]
            doi = dob[rows, :]
            total = l_ref[rows, :]

            def k_block(j, carry):
                seen, gsum, dq = carry
                cols = pl.ds(pl.multiple_of(j * BLK, BLK), BLK)
                kj = kn[cols, :]
                _, lb, l1m, mask = _sb_logits(qi, kj, i, j, scale)
                seen = seen + jnp.sum(l1m, axis=1, keepdims=True)
                tail = _tri_sum(l1m, tri_gt) + (total - seen)
                a = jnp.where(mask, jnp.exp(lb + tail), 0.0)
                g = _dot_nt(doi, vb[cols, :]) * a
                before = _tri_sum(g, tri_lt) + gsum
                beta = jnp.exp(lb)
                dz = jnp.where(mask, g * (1.0 - beta) - beta * before, 0.0)
                dzs = (dz * scale).astype(BF16)
                dkn[cols, :] += _dot_tn(dzs, qi)
                dvv[cols, :] += _dot_tn(a.astype(BF16), doi)
                dq = dq + jnp.dot(dzs, kj, preferred_element_type=F32)
                return seen, gsum + jnp.sum(g, axis=1, keepdims=True), dq

            zero = jnp.zeros((BLK, 1), F32)
            _, _, dq = lax.fori_loop(0, i + 1, k_block, (zero, zero, jnp.zeros((BLK, BLK), F32)))
            dqn[rows, :] = dq
            return 0

        lax.fori_loop(0, nb, q_block, 0)
        dq_raw, dgq_rows = _norm_bwd(q, rq, gq_ref[...], dqn[...])
        dk_raw, dgk_rows = _norm_bwd(k, rk, gk_ref[...], dkn[...])
        dqkv_ref[0] = dq_raw
        dqkv_ref[1] = dk_raw
        dqkv_ref[2] = dvv[...]
        dgq_ref[...] = jnp.sum(dgq_rows, axis=0, keepdims=True)
        dgk_ref[...] = jnp.sum(dgk_rows, axis=0, keepdims=True)

    def col(off):
        return pl.BlockSpec((s, BLK), lambda h: (0, off + h))

    gain = pl.BlockSpec((1, BLK), lambda h: (0, 0))
    per_head = pl.BlockSpec((None, 1, BLK), lambda h: (h, 0, 0))
    head_gain = jax.ShapeDtypeStruct((n_heads, 1, BLK), F32)
    return pl.pallas_call(
        body, name=name, grid=(n_heads,),
        in_specs=[col(0), col(n_heads), col(2 * n_heads), col(0), pl.BlockSpec((None, s, 1), lambda h: (h, 0, 0)), col(0),
                  gain, gain, per_head],
        out_specs=[pl.BlockSpec((3, s, BLK), lambda h: (0, 0, h)), per_head, per_head, per_head],
        out_shape=[jax.ShapeDtypeStruct((3, s, n_heads * BLK), F32), head_gain, head_gain, head_gain],
        scratch_shapes=[pltpu.VMEM((s, BLK), BF16)] * 4 + [pltpu.VMEM((s, BLK), F32)] * 3,
        compiler_params=_params(("parallel",), VMEM_BIG),
    )(p, p, p, o_raw, lsum, dmix, gq.reshape(1, BLK), gk.reshape(1, BLK), go.reshape(n_heads, 1, BLK))


_INV_SQRT2 = 0.7071067811865476
_INV_SQRT2PI = 0.3989422804014327


def _gelu(x):
    return 0.5 * x * (1.0 + lax.erf(x * _INV_SQRT2))


def _gelu_grad(x):
    return 0.5 * (1.0 + lax.erf(x * _INV_SQRT2)) + x * jnp.exp(-0.5 * x * x) * _INV_SQRT2PI


def _sgu_fwd(p, w, b, gv, gout, n_heads, *, name):
    s = p.shape[0]
    n_groups = w.shape[0]
    nb = s // BLK

    def body(u_ref, v_ref, w_ref, b_ref, gv_ref, go_ref, out_ref):
        wt = jnp.where(_iota2(0) >= _iota2(1), w_ref[...], 0.0).astype(BF16)
        bias = b_ref[...]

        def chunk(c, _):
            rows = pl.ds(pl.multiple_of(c * BLK, BLK), BLK)
            u = _gelu(u_ref[rows, :])
            vv = _gelu(v_ref[rows, :])
            vs = vv * _rstd(vv) * gv_ref[...]
            gated = u * (jnp.dot(wt, vs.astype(BF16), preferred_element_type=F32) + bias)
            out_ref[rows, :] = (gated * _rstd(gated) * go_ref[...]).astype(out_ref.dtype)
            return 0

        lax.fori_loop(0, nb, chunk, 0)

    def col(off):
        return pl.BlockSpec((s, BLK), lambda g: (0, off + g))

    per_group = pl.BlockSpec((None, 1, BLK), lambda g: (g, 0, 0))
    return pl.pallas_call(
        body, name=name, grid=(n_groups,),
        in_specs=[col(3 * n_heads), col(3 * n_heads + n_groups), pl.BlockSpec((None, BLK, BLK), lambda g: (g, 0, 0)),
                  pl.BlockSpec((None, BLK, 1), lambda g: (g, 0, 0)), per_group, per_group],
        out_specs=col(0), out_shape=jax.ShapeDtypeStruct((s, n_groups * BLK), BF16),
        compiler_params=_params(("parallel",), VMEM_BIG),
    )(p, p, w, b.reshape(n_groups, BLK, 1), gv.reshape(n_groups, 1, BLK), gout.reshape(n_groups, 1, BLK))


def _sgu_bwd(p, dmix, w, b, gv, gout, n_heads, *, name):
    s = p.shape[0]
    n_groups = w.shape[0]
    nb = s // BLK

    def body(u_ref, v_ref, ds_ref, w_ref, b_ref, gv_ref, go_ref, duv_ref, dw_ref, db_ref, dgv_ref, dgo_ref):
        lower = _iota2(0) >= _iota2(1)
        wt = jnp.where(lower, w_ref[...], 0.0).astype(BF16)
        bias = b_ref[...]

        def chunk(c, carry):
            dw, db, dgv, dgo = carry
            rows = pl.ds(pl.multiple_of(c * BLK, BLK), BLK)
            up = u_ref[rows, :]
            vp = v_ref[rows, :]
            u = _gelu(up)
            vv = _gelu(vp)
            rv = _rstd(vv)
            vsb = (vv * rv * gv_ref[...]).astype(BF16)
            mixed = jnp.dot(wt, vsb, preferred_element_type=F32) + bias
            gated = u * mixed
            dgated, dgo_rows = _norm_bwd(gated, _rstd(gated), go_ref[...], ds_ref[rows, :])
            dmixed = dgated * u
            dmb = dmixed.astype(BF16)
            dvs = _dot_tn(wt, dmb)
            dvv, dgv_rows = _norm_bwd(vv, rv, gv_ref[...], dvs)
            duv_ref[0, rows, :] = dgated * mixed * _gelu_grad(up)
            duv_ref[1, rows, :] = dvv * _gelu_grad(vp)
            return (dw + _dot_nt(dmb, vsb), db + jnp.sum(dmixed, axis=1, keepdims=True),
                    dgv + jnp.sum(dgv_rows, axis=0, keepdims=True), dgo + jnp.sum(dgo_rows, axis=0, keepdims=True))

        row0 = jnp.zeros((1, BLK), F32)
        dw, db, dgv, dgo = lax.fori_loop(0, nb, chunk, (jnp.zeros((BLK, BLK), F32), jnp.zeros((BLK, 1), F32), row0, row0))
        dw_ref[...] = jnp.where(lower, dw, 0.0)
        db_ref[...] = db
        dgv_ref[...] = dgv
        dgo_ref[...] = dgo

    def col(off):
        return pl.BlockSpec((s, BLK), lambda g: (0, off + g))

    per_group = pl.BlockSpec((None, 1, BLK), lambda g: (g, 0, 0))
    square = pl.BlockSpec((None, BLK, BLK), lambda g: (g, 0, 0))
    column = pl.BlockSpec((None, BLK, 1), lambda g: (g, 0, 0))
    gain = jax.ShapeDtypeStruct((n_groups, 1, BLK), F32)
    return pl.pallas_call(
        body, name=name, grid=(n_groups,),
        in_specs=[col(3 * n_heads), col(3 * n_heads + n_groups), col(n_heads), square, column, per_group, per_group],
        out_specs=[pl.BlockSpec((2, s, BLK), lambda g: (0, 0, g)), square, column, per_group, per_group],
        out_shape=[jax.ShapeDtypeStruct((2, s, n_groups * BLK), F32), jax.ShapeDtypeStruct((n_groups, BLK, BLK), F32),
                   jax.ShapeDtypeStruct((n_groups, BLK, 1), F32), gain, gain],
        compiler_params=_params(("parallel",), VMEM_BIG),
    )(p, p, dmix, w, b.reshape(n_groups, BLK, 1), gv.reshape(n_groups, 1, BLK), gout.reshape(n_groups, 1, BLK))


CONV_ROWS = 256
HALO = 8


def _shift_down(ref, r0, n, first):
    cur = ref[pl.ds(r0, n), :]
    prev = jnp.zeros((HALO, cur.shape[1]), F32) if first else ref[pl.ds(r0 - HALO, HALO), :]
    ext = jnp.concatenate([prev, cur], axis=0)
    return pltpu.roll(ext, 1, 0)[HALO:], pltpu.roll(ext, 2, 0)[HALO:], cur


def _shift_up(ref, r0, n, last):
    cur = ref[pl.ds(r0, n), :]
    nxt = jnp.zeros((HALO, cur.shape[1]), F32) if last else ref[pl.ds(r0 + n, HALO), :]
    ext = jnp.concatenate([cur, nxt], axis=0)
    return cur, pltpu.roll(ext, n + HALO - 1, 0)[:n], pltpu.roll(ext, n + HALO - 2, 0)[:n]


def _conv_rows(x1, x2, x0, w_ref, b_ref):
    return ((b_ref[...] + x2 * w_ref[0:1, :]) + x1 * w_ref[1:2, :]) + x0 * w_ref[2:3, :]


def _conv_specs(s, f, tc):
    nf = f // tc
    gate = pl.BlockSpec((s, tc), lambda n: (0, n))
    val = pl.BlockSpec((s, tc), lambda n: (0, nf + n))
    wg = pl.BlockSpec((3, tc), lambda n: (0, n))
    wv = pl.BlockSpec((3, tc), lambda n: (0, nf + n))
    bg = pl.BlockSpec((1, tc), lambda n: (0, n))
    bv = pl.BlockSpec((1, tc), lambda n: (0, nf + n))
    return nf, gate, val, wg, wv, bg, bv


def _conv_fwd(up, cw, cb, *, name):
    s, f2 = up.shape
    f = f2 // 2
    tc = _pick(f, (256, 128))
    cr = min(CONV_ROWS, s)
    nf, gate, val, wg, wv, bg, bv = _conv_specs(s, f, tc)

    def body(g_ref, v_ref, wg_ref, wv_ref, bg_ref, bv_ref, out_ref):
        for r0 in range(0, s, cr):
            gc = _conv_rows(*_shift_down(g_ref, r0, cr, r0 == 0), wg_ref, bg_ref)
            vc = _conv_rows(*_shift_down(v_ref, r0, cr, r0 == 0), wv_ref, bv_ref)
            out_ref[pl.ds(r0, cr), :] = (gc * jax.nn.sigmoid(gc) * vc).astype(out_ref.dtype)

    return pl.pallas_call(
        body, name=name, grid=(nf,), in_specs=[gate, val, wg, wv, bg, bv], out_specs=gate,
        out_shape=jax.ShapeDtypeStruct((s, f), BF16), compiler_params=_params(("parallel",), VMEM_BIG),
    )(up, up, cw, cw, cb.reshape(1, f2), cb.reshape(1, f2))


def _conv_bwd(up, dact, cw, cb, *, name):
    s, f2 = up.shape
    f = f2 // 2
    tc = _pick(f, (256, 128))
    cr = min(CONV_ROWS, s)
    nf, gate, val, wg, wv, bg, bv = _conv_specs(s, f, tc)

    def body(g_ref, v_ref, da_ref, wg_ref, wv_ref, bg_ref, bv_ref, dup_ref, dw_ref, db_ref, dgc, dvc):
        zero = jnp.zeros((1, tc), F32)
        sums = [[zero] * 4, [zero] * 4]
        for r0 in range(0, s, cr):
            rows = pl.ds(r0, cr)
            gx = _shift_down(g_ref, r0, cr, r0 == 0)
            vx = _shift_down(v_ref, r0, cr, r0 == 0)
            gc = _conv_rows(*gx, wg_ref, bg_ref)
            vc = _conv_rows(*vx, wv_ref, bv_ref)
            sig = jax.nn.sigmoid(gc)
            da = da_ref[rows, :]
            d_gate = da * vc * (sig * (1.0 + gc * (1.0 - sig)))
            d_val = da * (gc * sig)
            dgc[rows, :] = d_gate
            dvc[rows, :] = d_val
            for part, (dc, (x1, x2, x0)) in enumerate(((d_gate, gx), (d_val, vx))):
                for tap, xs in enumerate((x2, x1, x0)):
                    sums[part][tap] = sums[part][tap] + jnp.sum(dc * xs, axis=0, keepdims=True)
                sums[part][3] = sums[part][3] + jnp.sum(dc, axis=0, keepdims=True)
        dw_ref[...] = jnp.zeros_like(dw_ref)
        db_ref[...] = jnp.zeros_like(db_ref)
        for part, (dc_ref, w_ref) in enumerate(((dgc, wg_ref), (dvc, wv_ref))):
            for tap in range(3):
                dw_ref[part, tap:tap + 1, :] = sums[part][tap]
            db_ref[part, 0:1, :] = sums[part][3]
            for r0 in range(0, s, cr):
                d0, d1, d2 = _shift_up(dc_ref, r0, cr, r0 + cr == s)
                dup_ref[part, pl.ds(r0, cr), :] = (d0 * w_ref[2:3, :] + d1 * w_ref[1:2, :]) + d2 * w_ref[0:1, :]

    small = pl.BlockSpec((2, 8, tc), lambda n: (0, 0, n))
    return pl.pallas_call(
        body, name=name, grid=(nf,), in_specs=[gate, val, gate, wg, wv, bg, bv],
        out_specs=[pl.BlockSpec((2, s, tc), lambda n: (0, 0, n)), small, small],
        out_shape=[jax.ShapeDtypeStruct((2, s, f), F32), jax.ShapeDtypeStruct((2, 8, f), F32),
                   jax.ShapeDtypeStruct((2, 8, f), F32)],
        scratch_shapes=[pltpu.VMEM((s, tc), F32)] * 2, compiler_params=_params(("parallel",), VMEM_BIG),
    )(up, up, dact, cw, cw, cb.reshape(1, f2), cb.reshape(1, f2))


def _adamw(w, g, m, v, *, name):
    shape = w.shape
    cols = shape[-1]
    rows = w.size // cols
    if rows * cols * 4 <= (1 << 20):
        tr = rows
    else:
        tr = next(t for t in (1024, 512, 256, 128, 64, 32, 16, 8) if rows % t == 0 and (t * cols * 4 <= (1 << 20) or t == 8))

    def body(w_ref, g_ref, m_ref, v_ref, d_ref, nm_ref, nv_ref):
        gr = g_ref[...]
        nm = ADAM_B1 * m_ref[...] + (1.0 - ADAM_B1) * gr
        nv = ADAM_B2 * v_ref[...] + (1.0 - ADAM_B2) * (gr * gr)
        m_hat = nm / (1.0 - ADAM_B1 ** ADAM_STEP)
        v_hat = nv / (1.0 - ADAM_B2 ** ADAM_STEP)
        d_ref[...] = -ADAM_LR * (m_hat / (jnp.sqrt(v_hat) + ADAM_EPS) + ADAM_WD * w_ref[...])
        nm_ref[...] = nm
        nv_ref[...] = nv

    blk = pl.BlockSpec((tr, cols), lambda i: (i, 0))
    out = jax.ShapeDtypeStruct((rows, cols), F32)
    res = pl.pallas_call(
        body, name=name, grid=(rows // tr,), in_specs=[blk] * 4, out_specs=[blk] * 3, out_shape=[out] * 3,
        compiler_params=_params(("parallel",)),
    )(*[t.reshape(rows, cols) for t in (w, g, m, v)])
    return [t.reshape(shape) for t in res]


def _place():
    x, y, c = lax.axis_index("x"), lax.axis_index("y"), lax.axis_index("c")
    others = [(1 - x, y), (x, 1 - y), (1 - x, 1 - y)]
    return x, y, c, others


def _remote(src, dst, send_sem, recv_sem, device):
    return pltpu.make_async_remote_copy(src_ref=src, dst_ref=dst, send_sem=send_sem, recv_sem=recv_sem, device_id=device,
                                        device_id_type=MESH)


def _hbm_call(body, name, args, out_shapes, n_sems, n_local):
    return pl.pallas_call(
        body, name=name, in_specs=[ANY] * len(args), out_specs=[ANY] * len(out_shapes), out_shape=out_shapes,
        scratch_shapes=[pltpu.SemaphoreType.DMA((n_sems,)), pltpu.SemaphoreType.DMA((n_sems,)),
                        pltpu.SemaphoreType.DMA((max(n_local, 1),))],
        compiler_params=pltpu.CompilerParams(has_side_effects=True),
    )(*args)


def _all_gather_weights(halved, whole, *, name):
    nh, nw = len(halved), len(whole)
    arrays = list(halved) + list(whole)

    def body(*refs):
        srcs, outs = refs[:nh + nw], refs[nh + nw:2 * (nh + nw)]
        send, recv, local = refs[2 * (nh + nw):]
        x, y, c, others = _place()
        me = 2 * x + y
        locals_ = [pltpu.make_async_copy(srcs[a], outs[a].at[me], local.at[a]) for a in range(nh + nw)]
        for cp in locals_:
            cp.start()
        sends = []
        for a in range(nh):
            half = srcs[a].shape[0] // 2
            rows = pl.ds(c * half, half)
            for j, (px, py) in enumerate(others):
                sends.append(_remote(srcs[a].at[rows], outs[a].at[me, rows], send.at[6 * a + j], recv.at[6 * a + j],
                                     (px, py, c)))
        for a in range(nw):
            for j, (px, py) in enumerate(others):
                sends.append(_remote(srcs[nh + a], outs[nh + a].at[me], send.at[6 * nh + 3 * a + j],
                                     recv.at[6 * nh + 3 * a + j], (px, py, c)))
        for cp in sends:
            cp.start()
        for a in range(nh):
            half = srcs[a].shape[0] // 2
            rows = pl.ds(c * half, half)
            for j, (px, py) in enumerate(others):
                got = outs[a].at[2 * px + py, rows]
                _remote(got, got, send.at[6 * a + j], recv.at[6 * a + j], (px, py, c)).wait_recv()
                fwd = _remote(got, got, send.at[6 * a + 3 + j], recv.at[6 * a + 3 + j], (x, y, 1 - c))
                fwd.start()
                sends.append(fwd)
        for a in range(nh):
            half = srcs[a].shape[0] // 2
            theirs = pl.ds((1 - c) * half, half)
            for j, (px, py) in enumerate(others):
                got = outs[a].at[2 * px + py, theirs]
                _remote(got, got, send.at[6 * a + 3 + j], recv.at[6 * a + 3 + j], (x, y, 1 - c)).wait_recv()
        for a in range(nw):
            for j, (px, py) in enumerate(others):
                got = outs[nh + a].at[2 * px + py]
                _remote(got, got, send.at[6 * nh + 3 * a + j], recv.at[6 * nh + 3 * a + j], (px, py, c)).wait_recv()
        for cp in sends:
            cp.wait_send()
        for cp in locals_:
            cp.wait()

    out_shapes = [jax.ShapeDtypeStruct((N_CHIPS,) + t.shape, t.dtype) for t in arrays]
    return _hbm_call(body, name, arrays, out_shapes, 6 * nh + 3 * nw, nh + nw)


def _pair_exchange(grads, *, name):
    n = len(grads)

    def body(*refs):
        srcs, outs = refs[:n], refs[n:2 * n]
        send, recv, _ = refs[2 * n:]
        x, y, c, _o = _place()
        cps = []
        for a in range(n):
            half = srcs[a].shape[1] // 2
            cps.append(_remote(srcs[a].at[:, pl.ds((1 - c) * half, half), :], outs[a], send.at[a], recv.at[a], (x, y, 1 - c)))
        for cp in cps:
            cp.start()
        for cp in cps:
            cp.wait()

    out_shapes = [jax.ShapeDtypeStruct((N_CHIPS, t.shape[1] // 2, t.shape[2]), t.dtype) for t in grads]
    return _hbm_call(body, name, list(grads), out_shapes, n, 0)


def _chip_exchange(partial, *, name):
    n = len(partial)

    def body(*refs):
        srcs, outs = refs[:n], refs[n:2 * n]
        send, recv, _ = refs[2 * n:]
        x, y, c, others = _place()
        cps = []
        for a in range(n):
            for j, (px, py) in enumerate(others):
                cps.append(_remote(srcs[a].at[2 * px + py], outs[a].at[j], send.at[3 * a + j], recv.at[3 * a + j], (px, py, c)))
        for cp in cps:
            cp.start()
        for cp in cps:
            cp.wait()

    out_shapes = [jax.ShapeDtypeStruct((3,) + t.shape[1:], t.dtype) for t in partial]
    return _hbm_call(body, name, list(partial), out_shapes, 3 * n, 0)


def _share_halves(halves, *, name):
    n = len(halves)

    def body(*refs):
        srcs, outs = refs[:n], refs[n:2 * n]
        send, recv, local = refs[2 * n:]
        x, y, c, _o = _place()
        cps = [_remote(srcs[a], outs[a].at[:, c], send.at[a], recv.at[a], (x, y, 1 - c)) for a in range(n)]
        mine = [pltpu.make_async_copy(srcs[a], outs[a].at[:, c], local.at[a]) for a in range(n)]
        for cp in cps + mine:
            cp.start()
        for cp in cps + mine:
            cp.wait()

    out_shapes = [jax.ShapeDtypeStruct((t.shape[0], 2) + t.shape[1:], t.dtype) for t in halves]
    return _hbm_call(body, name, list(halves), out_shapes, n, n)


def _all_gather_small(pack, *, name):
    def body(src, out, send, recv, local):
        x, y, c, others = _place()

        def slot(px, py, pc):
            return out.at[4 * px + 2 * py + pc]

        mine = pltpu.make_async_copy(src, slot(x, y, c), local.at[0])
        mine.start()
        first = [_remote(src, slot(x, y, c), send.at[0], recv.at[0], (x, y, 1 - c))]
        first += [_remote(src, slot(x, y, c), send.at[1 + j], recv.at[1 + j], (px, py, c)) for j, (px, py) in enumerate(others)]
        for cp in first:
            cp.start()
        passed = []
        for j, (px, py) in enumerate(others):
            got = slot(px, py, c)
            _remote(got, got, send.at[1 + j], recv.at[1 + j], (px, py, c)).wait_recv()
            fwd = _remote(got, got, send.at[4 + j], recv.at[4 + j], (x, y, 1 - c))
            fwd.start()
            passed.append(fwd)
        theirs = slot(x, y, 1 - c)
        _remote(theirs, theirs, send.at[0], recv.at[0], (x, y, 1 - c)).wait_recv()
        for j, (px, py) in enumerate(others):
            got = slot(px, py, 1 - c)
            _remote(got, got, send.at[4 + j], recv.at[4 + j], (x, y, 1 - c)).wait_recv()
        for cp in first + passed:
            cp.wait_send()
        mine.wait()

    return _hbm_call(body, name, [pack], [jax.ShapeDtypeStruct((N_DEV,) + pack.shape, pack.dtype)], 7, 1)[0]


def _row_tile(rows, cols):
    return next(t for t in (512, 256, 128, 64, 32, 16) if rows % t == 0 and (t * cols * 4 <= (1 << 20) or t == 16))


def _pair_sum(grad, theirs, place, *, name):
    _, r, cols = grad.shape
    r2 = r // 2
    tr = _row_tile(r2, cols)
    nr = r2 // tr

    def body(place_ref, g_ref, t_ref, all_ref):
        all_ref[...] = (g_ref[...] + t_ref[...]).astype(all_ref.dtype)

    grid_spec = pltpu.PrefetchScalarGridSpec(
        num_scalar_prefetch=1, grid=(N_CHIPS, nr),
        in_specs=[pl.BlockSpec((None, tr, cols), lambda k, i, pr: (k, pr[0] * nr + i, 0)),
                  pl.BlockSpec((None, tr, cols), lambda k, i, pr: (k, i, 0))],
        out_specs=pl.BlockSpec((None, tr, cols), lambda k, i, pr: (k, i, 0)),
    )
    return pl.pallas_call(
        body, name=name, grid_spec=grid_spec, out_shape=jax.ShapeDtypeStruct((N_CHIPS, r2, cols), BF16),
        compiler_params=_params(("parallel", "parallel")),
    )(place, grad, theirs)


def _chip_sum(grad, theirs, got, place, *, name):
    _, r, cols = grad.shape
    r2 = r // 2
    tr = _row_tile(r2, cols)
    nr = r2 // tr

    def body(place_ref, g_ref, t_ref, got_ref, out_ref):
        own = g_ref[...] + t_ref[...]
        out_ref[...] = ((own + got_ref[0].astype(F32)) + got_ref[1].astype(F32)) + got_ref[2].astype(F32)

    grid_spec = pltpu.PrefetchScalarGridSpec(
        num_scalar_prefetch=1, grid=(nr,),
        in_specs=[pl.BlockSpec((None, tr, cols), lambda i, pr: (pr[1], pr[0] * nr + i, 0)),
                  pl.BlockSpec((None, tr, cols), lambda i, pr: (pr[1], i, 0)),
                  pl.BlockSpec((3, tr, cols), lambda i, pr: (0, i, 0))],
        out_specs=pl.BlockSpec((tr, cols), lambda i, pr: (i, 0)),
    )
    return pl.pallas_call(
        body, name=name, grid_spec=grid_spec, out_shape=jax.ShapeDtypeStruct((r2, cols), F32),
        compiler_params=_params(("parallel",)),
    )(place, grad, theirs, got)


def _sum_devices(parts, *, name):
    _, rows, cols = parts.shape
    tr = _pick(rows, (256, 128, 64, 32, 16, 8))

    def body(p_ref, out_ref):
        acc = p_ref[0]
        for d in range(1, N_DEV):
            acc = acc + p_ref[d]
        out_ref[...] = acc

    return pl.pallas_call(
        body, name=name, grid=(rows // tr,), in_specs=[pl.BlockSpec((N_DEV, tr, cols), lambda i: (0, i, 0))],
        out_specs=pl.BlockSpec((tr, cols), lambda i: (i, 0)), out_shape=jax.ShapeDtypeStruct((rows, cols), F32),
        compiler_params=_params(("parallel",)),
    )(parts)


def _pack(parts):
    rows = []
    for t in parts:
        flat = t.reshape(-1, 128)
        pad = (-flat.shape[0]) % 8
        rows.append(jnp.pad(flat, ((0, pad), (0, 0))) if pad else flat)
    return jnp.concatenate(rows, axis=0)


def _unpack(pack, shapes):
    out, r0 = [], 0
    for shp in shapes:
        n = math.prod(shp) // 128
        out.append(pack[r0:r0 + n].reshape(shp))
        r0 += n + (-n) % 8
    return out


SMALL = ["attn_norm_g", "q_norm_g", "k_norm_g", "sgu_norm_g", "sgu_w", "sgu_b", "out_norm_a_g", "out_norm_b_g",
         "ffn_norm_g", "conv_b"]
BIG = ["w_in", "w_out", "w_up", "w_down"]
ORDER = ["attn_norm_g", "w_in", "q_norm_g", "k_norm_g", "sgu_norm_g", "sgu_w", "sgu_b", "out_norm_a_g", "out_norm_b_g",
         "w_out", "ffn_norm_g", "w_up", "conv_w", "conv_b", "w_down"]


def kernel(x, attn_norm_g, w_in, q_norm_g, k_norm_g, sgu_norm_g, sgu_w, sgu_b, out_norm_a_g, out_norm_b_g, w_out, ffn_norm_g, w_up, conv_w, conv_b, w_down, loss_target, m_attn_norm_g, m_w_in, m_q_norm_g, m_k_norm_g, m_sgu_norm_g, m_sgu_w, m_sgu_b, m_out_norm_a_g, m_out_norm_b_g, m_w_out, m_ffn_norm_g, m_w_up, m_conv_w, m_conv_b, m_w_down, v_attn_norm_g, v_w_in, v_q_norm_g, v_k_norm_g, v_sgu_norm_g, v_sgu_w, v_sgu_b, v_out_norm_a_g, v_out_norm_b_g, v_w_out, v_ffn_norm_g, v_w_up, v_conv_w, v_conv_b, v_w_down):
    W = dict(attn_norm_g=attn_norm_g, w_in=w_in, q_norm_g=q_norm_g, k_norm_g=k_norm_g, sgu_norm_g=sgu_norm_g, sgu_w=sgu_w,
             sgu_b=sgu_b, out_norm_a_g=out_norm_a_g, out_norm_b_g=out_norm_b_g, w_out=w_out, ffn_norm_g=ffn_norm_g, w_up=w_up,
             conv_w=conv_w, conv_b=conv_b, w_down=w_down)
    M = dict(attn_norm_g=m_attn_norm_g, w_in=m_w_in, q_norm_g=m_q_norm_g, k_norm_g=m_k_norm_g, sgu_norm_g=m_sgu_norm_g,
             sgu_w=m_sgu_w, sgu_b=m_sgu_b, out_norm_a_g=m_out_norm_a_g, out_norm_b_g=m_out_norm_b_g, w_out=m_w_out,
             ffn_norm_g=m_ffn_norm_g, w_up=m_w_up, conv_w=m_conv_w, conv_b=m_conv_b, w_down=m_w_down)
    V = dict(attn_norm_g=v_attn_norm_g, w_in=v_w_in, q_norm_g=v_q_norm_g, k_norm_g=v_k_norm_g, sgu_norm_g=v_sgu_norm_g,
             sgu_w=v_sgu_w, sgu_b=v_sgu_b, out_norm_a_g=v_out_norm_a_g, out_norm_b_g=v_out_norm_b_g, w_out=v_w_out,
             ffn_norm_g=v_ffn_norm_g, w_up=v_w_up, conv_w=v_conv_w, conv_b=v_conv_b, w_down=v_w_down)
    depth = w_in.shape[0]
    s, d = x.shape[1], x.shape[2]
    n_heads = out_norm_a_g.shape[1]
    core = lax.axis_index("c")
    chip = 2 * lax.axis_index("x") + lax.axis_index("y")
    place = jnp.stack([core, chip]).astype(jnp.int32)
    xs = x.reshape(s, d)

    full = []
    for l in range(depth):
        taps = jnp.pad(conv_w[l], ((0, 5), (0, 0)))
        got = _all_gather_weights([W[n][l].astype(BF16) for n in BIG], [taps], name=f"gather_weights_{l}")
        f_local = conv_w.shape[2]
        cw_full = jnp.transpose(got[4][:, :3, :], (1, 0, 2)).reshape(3, N_CHIPS * f_local)
        full.append(dict(w_in=got[0], w_out=got[1].reshape(-1, d), w_up=got[2], w_down=got[3].reshape(-1, d), conv_w=cw_full))

    saved = []
    cur = xs
    for l in range(depth):
        fw = full[l]
        h = _rmsnorm_fwd(cur, attn_norm_g[l], name="attn_norm")
        p = _mm(h, fw["w_in"], "nn", b_split=N_CHIPS, name="proj_in")
        att, o_raw, lsum = _attn_fwd(p, q_norm_g[l], k_norm_g[l], out_norm_a_g[l], n_heads, name="attn_fwd")
        sg = _sgu_fwd(p, sgu_w[l], sgu_b[l], sgu_norm_g[l], out_norm_b_g[l], n_heads, name="sgu_fwd")
        mix = jnp.stack([att, sg])
        x1 = _mm(mix, fw["w_out"], "nn", a_split=2, res=cur, name="proj_out")
        h2 = _rmsnorm_fwd(x1, ffn_norm_g[l], name="ffn_norm")
        up = _mm(h2, fw["w_up"], "nn", b_split=N_CHIPS, name="ffn_up")
        act = _conv_fwd(up, fw["conv_w"], conv_b[l], name="conv_fwd")
        x2 = _mm(act, fw["w_down"], "nn", res=x1, name="ffn_down")
        saved.append(dict(x0=cur, h=h, p=p, o_raw=o_raw, lsum=lsum, mix=mix, x1=x1, h2=h2, up=up, act=act))
        cur = x2

    dx, sq = _loss_head(cur, loss_target.reshape(s, d), name="loss_head")
    loss = lax.psum(sq[0, 0] * (0.5 / d), ("x", "y", "c"))

    small_grads = {n: [None] * depth for n in SMALL + ["conv_w"]}
    halves = {n: [None] * depth for n in BIG}
    for l in reversed(range(depth)):
        fw, sv = full[l], saved[l]
        dact = _mm(dx, fw["w_down"], "nt", name="d_act")
        g_down = _mm(sv["act"], dx, "tn", name="g_down")
        dup, dcw, dcb = _conv_bwd(sv["up"], dact, fw["conv_w"], conv_b[l], name="conv_bwd")
        g_up = _mm(sv["h2"], dup, "tn", b_split=2, o_split=N_CHIPS, name="g_up")
        dh2 = _mm(dup, fw["w_up"], "nt", a_split=2, b_split=N_CHIPS, name="d_h2")
        dx1, dg_ffn = _rmsnorm_bwd(sv["x1"], ffn_norm_g[l], dh2, dx, name="ffn_norm_bwd")
        dmix = _mm(dx1, fw["w_out"], "nt", name="d_mix")
        g_out = _mm(sv["mix"], dx1, "tn", a_split=2, name="g_out")
        dqkv, dgq, dgk, dgoa = _attn_bwd(sv["p"], sv["o_raw"], sv["lsum"], dmix, q_norm_g[l], k_norm_g[l], out_norm_a_g[l],
                                         n_heads, name="attn_bwd")
        duv, dsw, dsb, dgv, dgob = _sgu_bwd(sv["p"], dmix, sgu_w[l], sgu_b[l], sgu_norm_g[l], out_norm_b_g[l], n_heads,
                                            name="sgu_bwd")
        dp = jnp.concatenate([dqkv, duv], axis=0)
        g_in = _mm(sv["h"], dp, "tn", b_split=dp.shape[0], o_split=N_CHIPS, name="g_in")
        dh = _mm(dp, fw["w_in"], "nt", a_split=dp.shape[0], b_split=N_CHIPS, name="d_h")
        dx, dg_attn = _rmsnorm_bwd(sv["x0"], attn_norm_g[l], dh, dx1, name="attn_norm_bwd")

        small_grads["attn_norm_g"][l] = dg_attn.reshape(d)
        small_grads["q_norm_g"][l] = jnp.sum(dgq, axis=(0, 1))
        small_grads["k_norm_g"][l] = jnp.sum(dgk, axis=(0, 1))
        small_grads["sgu_norm_g"][l] = dgv.reshape(-1, BLK)
        small_grads["sgu_w"][l] = dsw
        small_grads["sgu_b"][l] = dsb.reshape(-1, BLK)
        small_grads["out_norm_a_g"][l] = dgoa.reshape(-1, BLK)
        small_grads["out_norm_b_g"][l] = dgob.reshape(-1, BLK)
        small_grads["ffn_norm_g"][l] = dg_ffn.reshape(d)
        small_grads["conv_b"][l] = dcb[:, 0, :].reshape(-1)
        small_grads["conv_w"][l] = jnp.transpose(dcw[:, :3, :], (1, 0, 2)).reshape(3, -1)

        grads = [g_in, g_out.reshape(N_CHIPS, -1, d), g_up, g_down.reshape(N_CHIPS, -1, d)]
        theirs = _pair_exchange(grads, name="pair_exchange")
        sums = [_pair_sum(g, t, place, name=f"pair_sum_{n}") for g, t, n in zip(grads, theirs, BIG)]
        got = _chip_exchange(sums, name="chip_exchange")
        for n, g, t, gt in zip(BIG, grads, theirs, got):
            halves[n][l] = _chip_sum(g, t, gt, place, name=f"chip_sum_{n}")

    shared = _share_halves([jnp.stack(halves[n]) for n in BIG], name="share_halves")
    G = {n: t.reshape(W[n].shape) for n, t in zip(BIG, shared)}

    names = SMALL + ["conv_w"]
    pack = _pack([jnp.stack(small_grads[n]) for n in names])
    total = _sum_devices(_all_gather_small(pack, name="gather_small"), name="sum_small")
    f_full = conv_b.shape[1]
    shapes = [W[n].shape for n in SMALL] + [(depth, 3, f_full)]
    for n, t in zip(names, _unpack(total, shapes)):
        G[n] = t
    f_local = conv_w.shape[2]
    G["conv_w"] = lax.dynamic_slice_in_dim(G["conv_w"], chip * f_local, f_local, axis=2)

    D_, NM, NV = {}, {}, {}
    for n in BIG + ["conv_w"]:
        D_[n], NM[n], NV[n] = _adamw(W[n], G[n], M[n], V[n], name=f"adamw_{n}")
    small_shapes = [W[n].shape for n in SMALL]
    res = _adamw(_pack([W[n] for n in SMALL]), _pack([G[n] for n in SMALL]), _pack([M[n] for n in SMALL]),
                 _pack([V[n] for n in SMALL]), name="adamw_small")
    for dst, t in zip((D_, NM, NV), res):
        for n, u in zip(SMALL, _unpack(t, small_shapes)):
            dst[n] = u

    return (loss, dx.reshape(x.shape), *[G[n] for n in ORDER], *[D_[n] for n in ORDER], *[NM[n] for n in ORDER],
            *[NV[n] for n in ORDER])
```

```python
import functools
import math

import jax
import jax.numpy as jnp
from jax import lax
from jax.experimental import pallas as pl
from jax.experimental.pallas import tpu as pltpu

F32 = jnp.float32
BF16 = jnp.bfloat16
EPS = 1e-6
BLK = 128
N_CHIPS = 4
N_DEV = 8
ADAM_LR, ADAM_B1, ADAM_B2, ADAM_EPS, ADAM_WD, ADAM_STEP = 0.001, 0.9, 0.999, 1e-08, 0.01, 10
VMEM_BIG = 48 * 1024 * 1024
MESH = pl.DeviceIdType.MESH
ANY = pl.BlockSpec(memory_space=pl.ANY)


def _pick(dim, prefs):
    for t in prefs:
        if dim % t == 0:
            return t
    raise ValueError(f"no tile in {prefs} divides {dim}")


def _params(sem=None, vmem=None):
    return pltpu.CompilerParams(dimension_semantics=sem, vmem_limit_bytes=vmem)


def _ldims(arr, split):
    if split == 1:
        return arr.shape
    p, r, cs = arr.shape
    assert p == split
    return (r, p * cs)


def _spec(tr, tc, split, cols, rc):
    if split == 1:
        return pl.BlockSpec((tr, tc), lambda i, j, k: rc(i, j, k))
    per = (cols // split) // tc

    def imap(i, j, k):
        r, c = rc(i, j, k)
        return (c // per, r, c % per)

    return pl.BlockSpec((None, tr, tc), imap)


def _fit(unit, cap):
    return max(t for t in range(128, min(unit, cap) + 1, 128) if unit % t == 0)


def _mm(a, b, mode, *, name, caps, a_split=1, b_split=1, o_split=1, out_dtype=F32, res=None):
    ar, ac = _ldims(a, a_split)
    br, bc = _ldims(b, b_split)
    if mode == "nn":
        m, k, n = ar, ac, bc
        assert br == k
        ku, nu, mu = math.gcd(k // a_split, k), math.gcd(n // b_split, n // o_split), m
    elif mode == "nt":
        m, k, n = ar, ac, br
        assert bc == k
        ku, nu, mu = math.gcd(k // a_split, k // b_split), n // o_split, m
    else:
        k, m, n = ar, ac, bc
        assert br == k
        ku, nu, mu = k, math.gcd(n // b_split, n // o_split), m // a_split
    tm, tn, tk = _fit(mu, caps[0]), _fit(nu, caps[1]), _fit(ku, caps[2])
    nk = k // tk
    if mode == "nn":
        a_spec = _spec(tm, tk, a_split, k, lambda i, j, kk: (i, kk))
        b_spec = _spec(tk, tn, b_split, n, lambda i, j, kk: (kk, j))
    elif mode == "nt":
        a_spec = _spec(tm, tk, a_split, k, lambda i, j, kk: (i, kk))
        b_spec = _spec(tn, tk, b_split, k, lambda i, j, kk: (j, kk))
    else:
        a_spec = _spec(tk, tm, a_split, m, lambda i, j, kk: (kk, i))
        b_spec = _spec(tk, tn, b_split, n, lambda i, j, kk: (kk, j))
    o_spec = _spec(tm, tn, o_split, n, lambda i, j, kk: (i, j))
    dims = {"nn": (((1,), (0,)), ((), ())), "nt": (((1,), (1,)), ((), ())), "tn": (((0,), (0,)), ((), ()))}[mode]

    def body(a_ref, b_ref, *rest):
        if res is None:
            o_ref, acc = rest
        else:
            r_ref, o_ref, acc = rest
        kk = pl.program_id(2)

        @pl.when(kk == 0)
        def _():
            acc[...] = jnp.zeros_like(acc)

        acc[...] += lax.dot_general(a_ref[...].astype(BF16), b_ref[...].astype(BF16), dims, preferred_element_type=F32)

        @pl.when(kk == nk - 1)
        def _():
            out = acc[...]
            if res is not None:
                out = out + r_ref[...]
            o_ref[...] = out.astype(o_ref.dtype)

    in_specs, args = [a_spec, b_spec], [a, b]
    if res is not None:
        in_specs.append(pl.BlockSpec((tm, tn), lambda i, j, kk: (i, j)))
        args.append(res)
    out_shape = (m, n) if o_split == 1 else (o_split, m, n // o_split)
    return pl.pallas_call(
        body, name=name, grid=(m // tm, n // tn, nk), in_specs=in_specs, out_specs=o_spec,
        out_shape=jax.ShapeDtypeStruct(out_shape, out_dtype), scratch_shapes=[pltpu.VMEM((tm, tn), F32)],
        compiler_params=_params(("parallel", "parallel", "arbitrary"), VMEM_BIG),
    )(*args)


def _rstd(v):
    return lax.rsqrt(jnp.mean(v * v, axis=-1, keepdims=True) + EPS)


def _norm_bwd(v, r, gain, dout):
    a = dout * gain
    dv = r * (a - v * (r * r * jnp.mean(a * v, axis=-1, keepdims=True)))
    return dv, dout * v * r


def _rmsnorm_fwd(x, g, *, name):
    s, d = x.shape
    tr = _pick(s, (256, 128))

    def body(x_ref, g_ref, o_ref):
        v = x_ref[...]
        o_ref[...] = (v * _rstd(v) * g_ref[...]).astype(o_ref.dtype)

    return pl.pallas_call(
        body, name=name, grid=(s // tr,),
        in_specs=[pl.BlockSpec((tr, d), lambda i: (i, 0)), pl.BlockSpec((1, d), lambda i: (0, 0))],
        out_specs=pl.BlockSpec((tr, d), lambda i: (i, 0)), out_shape=jax.ShapeDtypeStruct((s, d), BF16),
        compiler_params=_params(("parallel",)),
    )(x, g.reshape(1, d))


def _rmsnorm_bwd(x, g, dh, dres, *, name):
    s, d = x.shape
    tr = _pick(s, (256, 128))

    def body(x_ref, g_ref, dh_ref, dres_ref, dx_ref, dxb_ref, dg_ref):
        v = x_ref[...]
        dv, dgr = _norm_bwd(v, _rstd(v), g_ref[...], dh_ref[...])
        dx = dres_ref[...] + dv
        dx_ref[...] = dx
        dxb_ref[...] = dx.astype(BF16)
        part = jnp.sum(dgr, axis=0, keepdims=True)

        @pl.when(pl.program_id(0) == 0)
        def _():
            dg_ref[...] = part

        @pl.when(pl.program_id(0) > 0)
        def _():
            dg_ref[...] += part

    row = pl.BlockSpec((tr, d), lambda i: (i, 0))
    one = pl.BlockSpec((1, d), lambda i: (0, 0))
    return pl.pallas_call(
        body, name=name, grid=(s // tr,), in_specs=[row, one, row, row], out_specs=[row, row, one],
        out_shape=[jax.ShapeDtypeStruct((s, d), F32), jax.ShapeDtypeStruct((s, d), BF16), jax.ShapeDtypeStruct((1, d), F32)],
        compiler_params=_params(("arbitrary",)),
    )(x, g.reshape(1, d), dh, dres)


def _loss_head(y, target, *, name):
    s, d = y.shape
    tr = _pick(s, (256, 128))

    def body(y_ref, t_ref, dy_ref, dyb_ref, ls_ref):
        e = y_ref[...] - t_ref[...]
        dy = e * (1.0 / d)
        dy_ref[...] = dy
        dyb_ref[...] = dy.astype(BF16)
        part = jnp.full(ls_ref.shape, jnp.sum(e * e), F32)

        @pl.when(pl.program_id(0) == 0)
        def _():
            ls_ref[...] = part

        @pl.when(pl.program_id(0) > 0)
        def _():
            ls_ref[...] += part

    row = pl.BlockSpec((tr, d), lambda i: (i, 0))
    return pl.pallas_call(
        body, name=name, grid=(s // tr,), in_specs=[row, row], out_specs=[row, row, pl.BlockSpec((8, 128), lambda i: (0, 0))],
        out_shape=[jax.ShapeDtypeStruct((s, d), F32), jax.ShapeDtypeStruct((s, d), BF16), jax.ShapeDtypeStruct((8, 128), F32)],
        compiler_params=_params(("arbitrary",)),
    )(y, target)


def _iota2(axis):
    return lax.broadcasted_iota(jnp.int32, (BLK, BLK), axis)


def _tri_sum(v, tri):
    hi = v.astype(BF16)
    lo = (v - hi.astype(F32)).astype(BF16)
    return jnp.dot(hi, tri, preferred_element_type=F32) + jnp.dot(lo, tri, preferred_element_type=F32)


def _dot_nt(a, b):
    return lax.dot_general(a, b, (((1,), (1,)), ((), ())), preferred_element_type=F32)


def _dot_tn(a, b):
    return lax.dot_general(a, b, (((0,), (0,)), ((), ())), preferred_element_type=F32)


TQ_MAX = 512


def _sb_logits(qi, kj, q0, j, scale):
    z = _dot_nt(qi, kj) * scale
    tq = qi.shape[0]
    mask = (j * BLK + lax.broadcasted_iota(jnp.int32, (tq, BLK), 1)) < (q0 + lax.broadcasted_iota(jnp.int32, (tq, BLK), 0))
    lb = jnp.minimum(z, 0.0) - jnp.log(1.0 + jnp.exp(-jnp.abs(z)))
    l1m = jnp.where(mask, lb - z, 0.0)
    return lb, l1m, mask


def _attn_fwd(p, gq, gk, go, n_heads, *, name):
    s = p.shape[0]
    tq = min(TQ_MAX, s)
    per = tq // BLK
    scale = BLK ** -0.5

    def body(q_ref, k_ref, v_ref, gq_ref, gk_ref, go_ref, att_ref, o_ref, l_ref, qn, kn, vb):
        q = q_ref[...]
        k = k_ref[...]
        qn[...] = (q * _rstd(q) * gq_ref[...]).astype(BF16)
        kn[...] = (k * _rstd(k) * gk_ref[...]).astype(BF16)
        vb[...] = v_ref[...].astype(BF16)
        tri_gt = (_iota2(0) > _iota2(1)).astype(BF16)

        def q_block(i, _):
            q0 = pl.multiple_of(i * tq, tq)
            rows = pl.ds(q0, tq)
            qi = qn[rows, :]
            n_keys = (i + 1) * per

            def k_block(jj, carry):
                later, acc = carry
                j = n_keys - 1 - jj
                cols = pl.ds(pl.multiple_of(j * BLK, BLK), BLK)
                lb, l1m, mask = _sb_logits(qi, kn[cols, :], q0, j, scale)
                tail = _tri_sum(l1m, tri_gt) + later
                a = jnp.where(mask, jnp.exp(lb + tail), 0.0)
                acc = acc + jnp.dot(a.astype(BF16), vb[cols, :], preferred_element_type=F32)
                return later + jnp.sum(l1m, axis=1, keepdims=True), acc

            total, acc = lax.fori_loop(0, n_keys, k_block, (jnp.zeros((tq, 1), F32), jnp.zeros((tq, BLK), F32)))
            o_ref[rows, :] = acc
            l_ref[rows, :] = total
            att_ref[rows, :] = (acc * _rstd(acc) * go_ref[...]).astype(att_ref.dtype)
            return 0

        lax.fori_loop(0, s // tq, q_block, 0)

    def col(off):
        return pl.BlockSpec((s, BLK), lambda h: (0, off + h))

    gain = pl.BlockSpec((1, BLK), lambda h: (0, 0))
    per_head = pl.BlockSpec((None, 1, BLK), lambda h: (h, 0, 0))
    return pl.pallas_call(
        body, name=name, grid=(n_heads,),
        in_specs=[col(0), col(n_heads), col(2 * n_heads), gain, gain, per_head],
        out_specs=[col(0), col(0), pl.BlockSpec((None, s, 1), lambda h: (h, 0, 0))],
        out_shape=[jax.ShapeDtypeStruct((s, n_heads * BLK), BF16), jax.ShapeDtypeStruct((s, n_heads * BLK), F32),
                   jax.ShapeDtypeStruct((n_heads, s, 1), F32)],
        scratch_shapes=[pltpu.VMEM((s, BLK), BF16)] * 3,
        compiler_params=_params(("parallel",), VMEM_BIG),
    )(p, p, p, gq.reshape(1, BLK), gk.reshape(1, BLK), go.reshape(n_heads, 1, BLK))


def _attn_bwd(p, o_raw, lsum, dmix, gq, gk, go, n_heads, *, name):
    s = p.shape[0]
    tq = min(TQ_MAX, s)
    per = tq // BLK
    scale = BLK ** -0.5

    def body(q_ref, k_ref, v_ref, o_ref, l_ref, da_ref, gq_ref, gk_ref, go_ref,
             dqkv_ref, dgq_ref, dgk_ref, dgo_ref, qn, kn, vb, dob, dqn, dkn, dvv):
        q = q_ref[...]
        k = k_ref[...]
        rq = _rstd(q)
        rk = _rstd(k)
        qn[...] = (q * rq * gq_ref[...]).astype(BF16)
        kn[...] = (k * rk * gk_ref[...]).astype(BF16)
        vb[...] = v_ref[...].astype(BF16)
        o = o_ref[...]
        do, dgo_rows = _norm_bwd(o, _rstd(o), go_ref[...], da_ref[...])
        dob[...] = do.astype(BF16)
        dgo_ref[...] = jnp.sum(dgo_rows, axis=0, keepdims=True)
        dkn[...] = jnp.zeros_like(dkn)
        dvv[...] = jnp.zeros_like(dvv)
        tri_gt = (_iota2(0) > _iota2(1)).astype(BF16)
        tri_lt = (_iota2(0) < _iota2(1)).astype(BF16)

        def q_block(i, _):
            q0 = pl.multiple_of(i * tq, tq)
            rows = pl.ds(q0, tq)
            qi = qn[rows, :]
            doi = dob[rows, :]
            total = l_ref[rows, :]

            def k_block(j, carry):
                seen, gsum, dq = carry
                cols = pl.ds(pl.multiple_of(j * BLK, BLK), BLK)
                kj = kn[cols, :]
                lb, l1m, mask = _sb_logits(qi, kj, q0, j, scale)
                seen = seen + jnp.sum(l1m, axis=1, keepdims=True)
                tail = _tri_sum(l1m, tri_gt) + (total - seen)
                a = jnp.where(mask, jnp.exp(lb + tail), 0.0)
                g = _dot_nt(doi, vb[cols, :]) * a
                before = _tri_sum(g, tri_lt) + gsum
                beta = jnp.exp(lb)
                dz = jnp.where(mask, g * (1.0 - beta) - beta * before, 0.0)
                dzs = (dz * scale).astype(BF16)
                dkn[cols, :] += _dot_tn(dzs, qi)
                dvv[cols, :] += _dot_tn(a.astype(BF16), doi)
                dq = dq + jnp.dot(dzs, kj, preferred_element_type=F32)
                return seen, gsum + jnp.sum(g, axis=1, keepdims=True), dq

            zero = jnp.zeros((tq, 1), F32)
            _, _, dq = lax.fori_loop(0, (i + 1) * per, k_block, (zero, zero, jnp.zeros((tq, BLK), F32)))
            dqn[rows, :] = dq
            return 0

        lax.fori_loop(0, s // tq, q_block, 0)
        dq_raw, dgq_rows = _norm_bwd(q, rq, gq_ref[...], dqn[...])
        dk_raw, dgk_rows = _norm_bwd(k, rk, gk_ref[...], dkn[...])
        dqkv_ref[0] = dq_raw.astype(BF16)
        dqkv_ref[1] = dk_raw.astype(BF16)
        dqkv_ref[2] = dvv[...].astype(BF16)
        dgq_ref[...] = jnp.sum(dgq_rows, axis=0, keepdims=True)
        dgk_ref[...] = jnp.sum(dgk_rows, axis=0, keepdims=True)

    def col(off):
        return pl.BlockSpec((s, BLK), lambda h: (0, off + h))

    gain = pl.BlockSpec((1, BLK), lambda h: (0, 0))
    per_head = pl.BlockSpec((None, 1, BLK), lambda h: (h, 0, 0))
    head_gain = jax.ShapeDtypeStruct((n_heads, 1, BLK), F32)
    return pl.pallas_call(
        body, name=name, grid=(n_heads,),
        in_specs=[col(0), col(n_heads), col(2 * n_heads), col(0), pl.BlockSpec((None, s, 1), lambda h: (h, 0, 0)), col(0),
                  gain, gain, per_head],
        out_specs=[pl.BlockSpec((3, s, BLK), lambda h: (0, 0, h)), per_head, per_head, per_head],
        out_shape=[jax.ShapeDtypeStruct((3, s, n_heads * BLK), BF16), head_gain, head_gain, head_gain],
        scratch_shapes=[pltpu.VMEM((s, BLK), BF16)] * 4 + [pltpu.VMEM((s, BLK), F32)] * 3,
        compiler_params=_params(("parallel",), VMEM_BIG),
    )(p, p, p, o_raw, lsum, dmix, gq.reshape(1, BLK), gk.reshape(1, BLK), go.reshape(n_heads, 1, BLK))


_INV_SQRT2 = 0.7071067811865476
_INV_SQRT2PI = 0.3989422804014327


def _gelu(x):
    return 0.5 * x * (1.0 + lax.erf(x * _INV_SQRT2))


def _gelu_grad(x):
    return 0.5 * (1.0 + lax.erf(x * _INV_SQRT2)) + x * jnp.exp(-0.5 * x * x) * _INV_SQRT2PI


def _sgu_fwd(p, w, b, gv, gout, n_heads, *, name):
    s = p.shape[0]
    n_groups = w.shape[0]
    nb = s // BLK

    def body(u_ref, v_ref, w_ref, b_ref, gv_ref, go_ref, out_ref):
        wt = jnp.where(_iota2(0) >= _iota2(1), w_ref[...], 0.0).astype(BF16)
        bias = b_ref[...]

        def chunk(c, _):
            rows = pl.ds(pl.multiple_of(c * BLK, BLK), BLK)
            u = _gelu(u_ref[rows, :])
            vv = _gelu(v_ref[rows, :])
            vs = vv * _rstd(vv) * gv_ref[...]
            gated = u * (jnp.dot(wt, vs.astype(BF16), preferred_element_type=F32) + bias)
            out_ref[rows, :] = (gated * _rstd(gated) * go_ref[...]).astype(out_ref.dtype)
            return 0

        lax.fori_loop(0, nb, chunk, 0)

    def col(off):
        return pl.BlockSpec((s, BLK), lambda g: (0, off + g))

    per_group = pl.BlockSpec((None, 1, BLK), lambda g: (g, 0, 0))
    return pl.pallas_call(
        body, name=name, grid=(n_groups,),
        in_specs=[col(3 * n_heads), col(3 * n_heads + n_groups), pl.BlockSpec((None, BLK, BLK), lambda g: (g, 0, 0)),
                  pl.BlockSpec((None, BLK, 1), lambda g: (g, 0, 0)), per_group, per_group],
        out_specs=col(0), out_shape=jax.ShapeDtypeStruct((s, n_groups * BLK), BF16),
        compiler_params=_params(("parallel",), VMEM_BIG),
    )(p, p, w, b.reshape(n_groups, BLK, 1), gv.reshape(n_groups, 1, BLK), gout.reshape(n_groups, 1, BLK))


def _sgu_bwd(p, dmix, w, b, gv, gout, n_heads, *, name):
    s = p.shape[0]
    n_groups = w.shape[0]
    nb = s // BLK

    def body(u_ref, v_ref, ds_ref, w_ref, b_ref, gv_ref, go_ref, duv_ref, dw_ref, db_ref, dgv_ref, dgo_ref):
        lower = _iota2(0) >= _iota2(1)
        wt = jnp.where(lower, w_ref[...], 0.0).astype(BF16)
        bias = b_ref[...]

        def chunk(c, carry):
            dw, db, dgv, dgo = carry
            rows = pl.ds(pl.multiple_of(c * BLK, BLK), BLK)
            up = u_ref[rows, :]
            vp = v_ref[rows, :]
            u = _gelu(up)
            vv = _gelu(vp)
            rv = _rstd(vv)
            vsb = (vv * rv * gv_ref[...]).astype(BF16)
            mixed = jnp.dot(wt, vsb, preferred_element_type=F32) + bias
            gated = u * mixed
            dgated, dgo_rows = _norm_bwd(gated, _rstd(gated), go_ref[...], ds_ref[rows, :])
            dmixed = dgated * u
            dmb = dmixed.astype(BF16)
            dvs = _dot_tn(wt, dmb)
            dvv, dgv_rows = _norm_bwd(vv, rv, gv_ref[...], dvs)
            duv_ref[0, rows, :] = (dgated * mixed * _gelu_grad(up)).astype(BF16)
            duv_ref[1, rows, :] = (dvv * _gelu_grad(vp)).astype(BF16)
            return (dw + _dot_nt(dmb, vsb), db + jnp.sum(dmixed, axis=1, keepdims=True),
                    dgv + jnp.sum(dgv_rows, axis=0, keepdims=True), dgo + jnp.sum(dgo_rows, axis=0, keepdims=True))

        row0 = jnp.zeros((1, BLK), F32)
        dw, db, dgv, dgo = lax.fori_loop(0, nb, chunk, (jnp.zeros((BLK, BLK), F32), jnp.zeros((BLK, 1), F32), row0, row0))
        dw_ref[...] = jnp.where(lower, dw, 0.0)
        db_ref[...] = db
        dgv_ref[...] = dgv
        dgo_ref[...] = dgo

    def col(off):
        return pl.BlockSpec((s, BLK), lambda g: (0, off + g))

    per_group = pl.BlockSpec((None, 1, BLK), lambda g: (g, 0, 0))
    square = pl.BlockSpec((None, BLK, BLK), lambda g: (g, 0, 0))
    column = pl.BlockSpec((None, BLK, 1), lambda g: (g, 0, 0))
    gain = jax.ShapeDtypeStruct((n_groups, 1, BLK), F32)
    return pl.pallas_call(
        body, name=name, grid=(n_groups,),
        in_specs=[col(3 * n_heads), col(3 * n_heads + n_groups), col(n_heads), square, column, per_group, per_group],
        out_specs=[pl.BlockSpec((2, s, BLK), lambda g: (0, 0, g)), square, column, per_group, per_group],
        out_shape=[jax.ShapeDtypeStruct((2, s, n_groups * BLK), BF16), jax.ShapeDtypeStruct((n_groups, BLK, BLK), F32),
                   jax.ShapeDtypeStruct((n_groups, BLK, 1), F32), gain, gain],
        compiler_params=_params(("parallel",), VMEM_BIG),
    )(p, p, dmix, w, b.reshape(n_groups, BLK, 1), gv.reshape(n_groups, 1, BLK), gout.reshape(n_groups, 1, BLK))


CONV_ROWS = 256
HALO = 8


def _shift_down(ref, r0, n, first):
    cur = ref[pl.ds(r0, n), :]
    prev = jnp.zeros((HALO, cur.shape[1]), F32) if first else ref[pl.ds(r0 - HALO, HALO), :]
    ext = jnp.concatenate([prev, cur], axis=0)
    return pltpu.roll(ext, 1, 0)[HALO:], pltpu.roll(ext, 2, 0)[HALO:], cur


def _shift_up(ref, r0, n, last):
    cur = ref[pl.ds(r0, n), :]
    nxt = jnp.zeros((HALO, cur.shape[1]), F32) if last else ref[pl.ds(r0 + n, HALO), :]
    ext = jnp.concatenate([cur, nxt], axis=0)
    return cur, pltpu.roll(ext, n + HALO - 1, 0)[:n], pltpu.roll(ext, n + HALO - 2, 0)[:n]


def _conv_rows(x1, x2, x0, w_ref, b_ref):
    return ((b_ref[...] + x2 * w_ref[0:1, :]) + x1 * w_ref[1:2, :]) + x0 * w_ref[2:3, :]


def _conv_specs(s, f, tc):
    nf = f // tc
    gate = pl.BlockSpec((s, tc), lambda n: (0, n))
    val = pl.BlockSpec((s, tc), lambda n: (0, nf + n))
    wg = pl.BlockSpec((3, tc), lambda n: (0, n))
    wv = pl.BlockSpec((3, tc), lambda n: (0, nf + n))
    bg = pl.BlockSpec((1, tc), lambda n: (0, n))
    bv = pl.BlockSpec((1, tc), lambda n: (0, nf + n))
    return nf, gate, val, wg, wv, bg, bv


def _conv_fwd(up, cw, cb, *, name):
    s, f2 = up.shape
    f = f2 // 2
    tc = _pick(f, (256, 128))
    cr = min(CONV_ROWS, s)
    nf, gate, val, wg, wv, bg, bv = _conv_specs(s, f, tc)

    def body(g_ref, v_ref, wg_ref, wv_ref, bg_ref, bv_ref, out_ref):
        for r0 in range(0, s, cr):
            gc = _conv_rows(*_shift_down(g_ref, r0, cr, r0 == 0), wg_ref, bg_ref)
            vc = _conv_rows(*_shift_down(v_ref, r0, cr, r0 == 0), wv_ref, bv_ref)
            out_ref[pl.ds(r0, cr), :] = (gc * jax.nn.sigmoid(gc) * vc).astype(out_ref.dtype)

    return pl.pallas_call(
        body, name=name, grid=(nf,), in_specs=[gate, val, wg, wv, bg, bv], out_specs=gate,
        out_shape=jax.ShapeDtypeStruct((s, f), BF16), compiler_params=_params(("parallel",), VMEM_BIG),
    )(up, up, cw, cw, cb.reshape(1, f2), cb.reshape(1, f2))


def _conv_bwd(up, dact, cw, cb, *, name):
    s, f2 = up.shape
    f = f2 // 2
    tc = _pick(f, (256, 128))
    cr = min(CONV_ROWS, s)
    nf, gate, val, wg, wv, bg, bv = _conv_specs(s, f, tc)

    def body(g_ref, v_ref, da_ref, wg_ref, wv_ref, bg_ref, bv_ref, dup_ref, dw_ref, db_ref, dgc, dvc):
        zero = jnp.zeros((1, tc), F32)
        sums = [[zero] * 4, [zero] * 4]
        for r0 in range(0, s, cr):
            rows = pl.ds(r0, cr)
            gx = _shift_down(g_ref, r0, cr, r0 == 0)
            vx = _shift_down(v_ref, r0, cr, r0 == 0)
            gc = _conv_rows(*gx, wg_ref, bg_ref)
            vc = _conv_rows(*vx, wv_ref, bv_ref)
            sig = jax.nn.sigmoid(gc)
            da = da_ref[rows, :]
            d_gate = da * vc * (sig * (1.0 + gc * (1.0 - sig)))
            d_val = da * (gc * sig)
            dgc[rows, :] = d_gate
            dvc[rows, :] = d_val
            for part, (dc, (x1, x2, x0)) in enumerate(((d_gate, gx), (d_val, vx))):
                for tap, xs in enumerate((x2, x1, x0)):
                    sums[part][tap] = sums[part][tap] + jnp.sum(dc * xs, axis=0, keepdims=True)
                sums[part][3] = sums[part][3] + jnp.sum(dc, axis=0, keepdims=True)
        dw_ref[...] = jnp.zeros_like(dw_ref)
        db_ref[...] = jnp.zeros_like(db_ref)
        for part, (dc_ref, w_ref) in enumerate(((dgc, wg_ref), (dvc, wv_ref))):
            for tap in range(3):
                dw_ref[part, tap:tap + 1, :] = sums[part][tap]
            db_ref[part, 0:1, :] = sums[part][3]
            for r0 in range(0, s, cr):
                d0, d1, d2 = _shift_up(dc_ref, r0, cr, r0 + cr == s)
                dup_ref[part, pl.ds(r0, cr), :] = ((d0 * w_ref[2:3, :] + d1 * w_ref[1:2, :]) + d2 * w_ref[0:1, :]).astype(BF16)

    small = pl.BlockSpec((2, 8, tc), lambda n: (0, 0, n))
    return pl.pallas_call(
        body, name=name, grid=(nf,), in_specs=[gate, val, gate, wg, wv, bg, bv],
        out_specs=[pl.BlockSpec((2, s, tc), lambda n: (0, 0, n)), small, small],
        out_shape=[jax.ShapeDtypeStruct((2, s, f), BF16), jax.ShapeDtypeStruct((2, 8, f), F32),
                   jax.ShapeDtypeStruct((2, 8, f), F32)],
        scratch_shapes=[pltpu.VMEM((s, tc), F32)] * 2, compiler_params=_params(("parallel",), VMEM_BIG),
    )(up, up, dact, cw, cw, cb.reshape(1, f2), cb.reshape(1, f2))


def _adamw(w, g, m, v, *, name):
    shape = w.shape
    cols = shape[-1]
    rows = w.size // cols
    if rows * cols * 4 <= (1 << 20):
        tr = rows
    else:
        tr = next(t for t in (1024, 512, 256, 128, 64, 32, 16, 8) if rows % t == 0 and (t * cols * 4 <= (1 << 20) or t == 8))

    def body(w_ref, g_ref, m_ref, v_ref, d_ref, nm_ref, nv_ref):
        gr = g_ref[...]
        nm = ADAM_B1 * m_ref[...] + (1.0 - ADAM_B1) * gr
        nv = ADAM_B2 * v_ref[...] + (1.0 - ADAM_B2) * (gr * gr)
        m_hat = nm / (1.0 - ADAM_B1 ** ADAM_STEP)
        v_hat = nv / (1.0 - ADAM_B2 ** ADAM_STEP)
        d_ref[...] = -ADAM_LR * (m_hat / (jnp.sqrt(v_hat) + ADAM_EPS) + ADAM_WD * w_ref[...])
        nm_ref[...] = nm
        nv_ref[...] = nv

    blk = pl.BlockSpec((tr, cols), lambda i: (i, 0))
    out = jax.ShapeDtypeStruct((rows, cols), F32)
    res = pl.pallas_call(
        body, name=name, grid=(rows // tr,), in_specs=[blk] * 4, out_specs=[blk] * 3, out_shape=[out] * 3,
        compiler_params=_params(("parallel",)),
    )(*[t.reshape(rows, cols) for t in (w, g, m, v)])
    return [t.reshape(shape) for t in res]


def _place():
    x, y, c = lax.axis_index("x"), lax.axis_index("y"), lax.axis_index("c")
    others = [(1 - x, y), (x, 1 - y), (1 - x, 1 - y)]
    return x, y, c, others


def _remote(src, dst, send_sem, recv_sem, device):
    return pltpu.make_async_remote_copy(src_ref=src, dst_ref=dst, send_sem=send_sem, recv_sem=recv_sem, device_id=device,
                                        device_id_type=MESH)


def _hbm_call(body, name, args, out_shapes, n_sems, n_local, aliases=None):
    return pl.pallas_call(
        body, name=name, in_specs=[ANY] * len(args), out_specs=[ANY] * len(out_shapes), out_shape=out_shapes,
        scratch_shapes=[pltpu.SemaphoreType.DMA((n_sems,)), pltpu.SemaphoreType.DMA((n_sems,)),
                        pltpu.SemaphoreType.DMA((max(n_local, 1),))],
        input_output_aliases=aliases or {}, compiler_params=pltpu.CompilerParams(has_side_effects=True),
    )(*args)


def _all_gather_weights(halved, whole, *, name):
    nh, nw = len(halved), len(whole)
    arrays = list(halved) + list(whole)

    def body(*refs):
        srcs, outs = refs[:nh + nw], refs[nh + nw:2 * (nh + nw)]
        send, recv, local = refs[2 * (nh + nw):]
        x, y, c, others = _place()
        me = 2 * x + y
        locals_ = [pltpu.make_async_copy(srcs[nh + a], outs[nh + a].at[me], local.at[a]) for a in range(nw)]
        for cp in locals_:
            cp.start()
        sends = []
        for a in range(nh):
            half = outs[a].shape[1] // 2
            rows = pl.ds(c * half, half)
            for j, (px, py) in enumerate(others):
                sends.append(_remote(outs[a].at[me, rows], outs[a].at[me, rows], send.at[6 * a + j], recv.at[6 * a + j],
                                     (px, py, c)))
        for a in range(nw):
            for j, (px, py) in enumerate(others):
                sends.append(_remote(srcs[nh + a], outs[nh + a].at[me], send.at[6 * nh + 3 * a + j],
                                     recv.at[6 * nh + 3 * a + j], (px, py, c)))
        for cp in sends:
            cp.start()
        for a in range(nh):
            half = outs[a].shape[1] // 2
            rows = pl.ds(c * half, half)
            for j, (px, py) in enumerate(others):
                got = outs[a].at[2 * px + py, rows]
                _remote(got, got, send.at[6 * a + j], recv.at[6 * a + j], (px, py, c)).wait_recv()
                fwd = _remote(got, got, send.at[6 * a + 3 + j], recv.at[6 * a + 3 + j], (x, y, 1 - c))
                fwd.start()
                sends.append(fwd)
        for a in range(nh):
            half = outs[a].shape[1] // 2
            theirs = pl.ds((1 - c) * half, half)
            for j, (px, py) in enumerate(others):
                got = outs[a].at[2 * px + py, theirs]
                _remote(got, got, send.at[6 * a + 3 + j], recv.at[6 * a + 3 + j], (x, y, 1 - c)).wait_recv()
        for a in range(nw):
            for j, (px, py) in enumerate(others):
                got = outs[nh + a].at[2 * px + py]
                _remote(got, got, send.at[6 * nh + 3 * a + j], recv.at[6 * nh + 3 * a + j], (px, py, c)).wait_recv()
        for cp in sends:
            cp.wait_send()
        for cp in locals_:
            cp.wait()

    out_shapes = [jax.ShapeDtypeStruct(t.shape, t.dtype) for t in halved]
    out_shapes += [jax.ShapeDtypeStruct((N_CHIPS,) + t.shape, t.dtype) for t in whole]
    return _hbm_call(body, name, arrays, out_shapes, 6 * nh + 3 * nw, nw, aliases={a: a for a in range(nh)})


def _cast_into(w, layer, place, *, name):
    _, r, cols = w.shape
    tr = _row_tile(r, cols)

    def body(place_ref, w_ref, out_ref):
        out_ref[...] = w_ref[...].astype(out_ref.dtype)

    grid_spec = pltpu.PrefetchScalarGridSpec(
        num_scalar_prefetch=1, grid=(r // tr,), in_specs=[pl.BlockSpec((None, tr, cols), lambda i, pr: (layer, i, 0))],
        out_specs=pl.BlockSpec((None, tr, cols), lambda i, pr: (pr[1], i, 0)),
    )
    return pl.pallas_call(
        body, name=name, grid_spec=grid_spec, out_shape=jax.ShapeDtypeStruct((N_CHIPS, r, cols), BF16),
        compiler_params=_params(("parallel",)),
    )(place, w)


def _pair_exchange(grads, *, name):
    n = len(grads)

    def body(*refs):
        srcs, outs = refs[:n], refs[n:2 * n]
        send, recv, _ = refs[2 * n:]
        x, y, c, _o = _place()
        cps = []
        for a in range(n):
            half = srcs[a].shape[1] // 2
            cps.append(_remote(srcs[a].at[:, pl.ds((1 - c) * half, half), :], outs[a], send.at[a], recv.at[a], (x, y, 1 - c)))
        for cp in cps:
            cp.start()
        for cp in cps:
            cp.wait()

    out_shapes = [jax.ShapeDtypeStruct((N_CHIPS, t.shape[1] // 2, t.shape[2]), t.dtype) for t in grads]
    return _hbm_call(body, name, list(grads), out_shapes, n, 0)


def _chip_exchange(partial, *, name):
    n = len(partial)

    def body(*refs):
        srcs, outs = refs[:n], refs[n:2 * n]
        send, recv, _ = refs[2 * n:]
        x, y, c, others = _place()
        cps = []
        for a in range(n):
            for j, (px, py) in enumerate(others):
                cps.append(_remote(srcs[a].at[2 * px + py], outs[a].at[j], send.at[3 * a + j], recv.at[3 * a + j], (px, py, c)))
        for cp in cps:
            cp.start()
        for cp in cps:
            cp.wait()

    out_shapes = [jax.ShapeDtypeStruct((3,) + t.shape[1:], t.dtype) for t in partial]
    return _hbm_call(body, name, list(partial), out_shapes, 3 * n, 0)


def _share_halves(bufs, *, name):
    n = len(bufs)

    def body(*refs):
        outs = refs[n:2 * n]
        send, recv, _ = refs[2 * n:]
        x, y, c, _o = _place()
        cps = [_remote(outs[a].at[:, c], outs[a].at[:, c], send.at[a], recv.at[a], (x, y, 1 - c)) for a in range(n)]
        for cp in cps:
            cp.start()
        for a in range(n):
            cps[a].wait_send()
            theirs = outs[a].at[:, 1 - c]
            _remote(theirs, theirs, send.at[a], recv.at[a], (x, y, 1 - c)).wait_recv()

    out_shapes = [jax.ShapeDtypeStruct(t.shape, t.dtype) for t in bufs]
    return _hbm_call(body, name, list(bufs), out_shapes, n, 0, aliases={a: a for a in range(n)})


def _all_gather_small(pack, *, name):
    def body(src, out, send, recv, local):
        x, y, c, others = _place()

        def slot(px, py, pc):
            return out.at[4 * px + 2 * py + pc]

        mine = pltpu.make_async_copy(src, slot(x, y, c), local.at[0])
        mine.start()
        first = [_remote(src, slot(x, y, c), send.at[0], recv.at[0], (x, y, 1 - c))]
        first += [_remote(src, slot(x, y, c), send.at[1 + j], recv.at[1 + j], (px, py, c)) for j, (px, py) in enumerate(others)]
        for cp in first:
            cp.start()
        passed = []
        for j, (px, py) in enumerate(others):
            got = slot(px, py, c)
            _remote(got, got, send.at[1 + j], recv.at[1 + j], (px, py, c)).wait_recv()
            fwd = _remote(got, got, send.at[4 + j], recv.at[4 + j], (x, y, 1 - c))
            fwd.start()
            passed.append(fwd)
        theirs = slot(x, y, 1 - c)
        _remote(theirs, theirs, send.at[0], recv.at[0], (x, y, 1 - c)).wait_recv()
        for j, (px, py) in enumerate(others):
            got = slot(px, py, 1 - c)
            _remote(got, got, send.at[4 + j], recv.at[4 + j], (x, y, 1 - c)).wait_recv()
        for cp in first + passed:
            cp.wait_send()
        mine.wait()

    return _hbm_call(body, name, [pack], [jax.ShapeDtypeStruct((N_DEV,) + pack.shape, pack.dtype)], 7, 1)[0]


def _row_tile(rows, cols):
    return next(t for t in (512, 256, 128, 64, 32, 16) if rows % t == 0 and (t * cols * 4 <= (1 << 20) or t == 16))


def _pair_sum(grad, theirs, place, *, name):
    _, r, cols = grad.shape
    r2 = r // 2
    tr = _row_tile(r2, cols)
    nr = r2 // tr

    def body(place_ref, g_ref, t_ref, all_ref):
        all_ref[...] = (g_ref[...] + t_ref[...]).astype(all_ref.dtype)

    grid_spec = pltpu.PrefetchScalarGridSpec(
        num_scalar_prefetch=1, grid=(N_CHIPS, nr),
        in_specs=[pl.BlockSpec((None, tr, cols), lambda k, i, pr: (k, pr[0] * nr + i, 0)),
                  pl.BlockSpec((None, tr, cols), lambda k, i, pr: (k, i, 0))],
        out_specs=pl.BlockSpec((None, tr, cols), lambda k, i, pr: (k, i, 0)),
    )
    return pl.pallas_call(
        body, name=name, grid_spec=grid_spec, out_shape=jax.ShapeDtypeStruct((N_CHIPS, r2, cols), BF16),
        compiler_params=_params(("parallel", "parallel")),
    )(place, grad, theirs)


def _chip_sum(grad, theirs, got, place, buf, layer, depth, *, name):
    _, r, cols = grad.shape
    r2 = r // 2
    tr = _row_tile(r2, cols)
    nr = r2 // tr

    def body(place_ref, g_ref, t_ref, got_ref, *rest):
        own = g_ref[...] + t_ref[...]
        rest[-1][...] = ((own + got_ref[0].astype(F32)) + got_ref[1].astype(F32)) + got_ref[2].astype(F32)

    in_specs = [pl.BlockSpec((None, tr, cols), lambda i, pr: (pr[1], pr[0] * nr + i, 0)),
                pl.BlockSpec((None, tr, cols), lambda i, pr: (pr[1], i, 0)),
                pl.BlockSpec((3, tr, cols), lambda i, pr: (0, i, 0))]
    args = [place, grad, theirs, got]
    if buf is not None:
        in_specs.append(ANY)
        args.append(buf)
    grid_spec = pltpu.PrefetchScalarGridSpec(
        num_scalar_prefetch=1, grid=(nr,), in_specs=in_specs,
        out_specs=pl.BlockSpec((None, None, tr, cols), lambda i, pr: (layer, pr[0], i, 0)),
    )
    return pl.pallas_call(
        body, name=name, grid_spec=grid_spec, out_shape=jax.ShapeDtypeStruct((depth, 2, r2, cols), F32),
        input_output_aliases={} if buf is None else {4: 0}, compiler_params=_params(("parallel",)),
    )(*args)


def _sum_devices(parts, *, name):
    _, rows, cols = parts.shape
    tr = _pick(rows, (256, 128, 64, 32, 16, 8))

    def body(p_ref, out_ref):
        acc = p_ref[0]
        for d in range(1, N_DEV):
            acc = acc + p_ref[d]
        out_ref[...] = acc

    return pl.pallas_call(
        body, name=name, grid=(rows // tr,), in_specs=[pl.BlockSpec((N_DEV, tr, cols), lambda i: (0, i, 0))],
        out_specs=pl.BlockSpec((tr, cols), lambda i: (i, 0)), out_shape=jax.ShapeDtypeStruct((rows, cols), F32),
        compiler_params=_params(("parallel",)),
    )(parts)


def _pack(parts):
    rows = []
    for t in parts:
        flat = t.reshape(-1, 128)
        pad = (-flat.shape[0]) % 8
        rows.append(jnp.pad(flat, ((0, pad), (0, 0))) if pad else flat)
    return jnp.concatenate(rows, axis=0)


def _unpack(pack, shapes):
    out, r0 = [], 0
    for shp in shapes:
        n = math.prod(shp) // 128
        out.append(pack[r0:r0 + n].reshape(shp))
        r0 += n + (-n) % 8
    return out


SMALL = ["attn_norm_g", "q_norm_g", "k_norm_g", "sgu_norm_g", "sgu_w", "sgu_b", "out_norm_a_g", "out_norm_b_g",
         "ffn_norm_g", "conv_b"]
BIG = ["w_in", "w_out", "w_up", "w_down"]
ORDER = ["attn_norm_g", "w_in", "q_norm_g", "k_norm_g", "sgu_norm_g", "sgu_w", "sgu_b", "out_norm_a_g", "out_norm_b_g",
         "w_out", "ffn_norm_g", "w_up", "conv_w", "conv_b", "w_down"]


def kernel(x, attn_norm_g, w_in, q_norm_g, k_norm_g, sgu_norm_g, sgu_w, sgu_b, out_norm_a_g, out_norm_b_g, w_out, ffn_norm_g, w_up, conv_w, conv_b, w_down, loss_target, m_attn_norm_g, m_w_in, m_q_norm_g, m_k_norm_g, m_sgu_norm_g, m_sgu_w, m_sgu_b, m_out_norm_a_g, m_out_norm_b_g, m_w_out, m_ffn_norm_g, m_w_up, m_conv_w, m_conv_b, m_w_down, v_attn_norm_g, v_w_in, v_q_norm_g, v_k_norm_g, v_sgu_norm_g, v_sgu_w, v_sgu_b, v_out_norm_a_g, v_out_norm_b_g, v_w_out, v_ffn_norm_g, v_w_up, v_conv_w, v_conv_b, v_w_down):
    W = dict(attn_norm_g=attn_norm_g, w_in=w_in, q_norm_g=q_norm_g, k_norm_g=k_norm_g, sgu_norm_g=sgu_norm_g, sgu_w=sgu_w,
             sgu_b=sgu_b, out_norm_a_g=out_norm_a_g, out_norm_b_g=out_norm_b_g, w_out=w_out, ffn_norm_g=ffn_norm_g, w_up=w_up,
             conv_w=conv_w, conv_b=conv_b, w_down=w_down)
    M = dict(attn_norm_g=m_attn_norm_g, w_in=m_w_in, q_norm_g=m_q_norm_g, k_norm_g=m_k_norm_g, sgu_norm_g=m_sgu_norm_g,
             sgu_w=m_sgu_w, sgu_b=m_sgu_b, out_norm_a_g=m_out_norm_a_g, out_norm_b_g=m_out_norm_b_g, w_out=m_w_out,
             ffn_norm_g=m_ffn_norm_g, w_up=m_w_up, conv_w=m_conv_w, conv_b=m_conv_b, w_down=m_w_down)
    V = dict(attn_norm_g=v_attn_norm_g, w_in=v_w_in, q_norm_g=v_q_norm_g, k_norm_g=v_k_norm_g, sgu_norm_g=v_sgu_norm_g,
             sgu_w=v_sgu_w, sgu_b=v_sgu_b, out_norm_a_g=v_out_norm_a_g, out_norm_b_g=v_out_norm_b_g, w_out=v_w_out,
             ffn_norm_g=v_ffn_norm_g, w_up=v_w_up, conv_w=v_conv_w, conv_b=v_conv_b, w_down=v_w_down)
    depth = w_in.shape[0]
    s, d = x.shape[1], x.shape[2]
    n_heads = out_norm_a_g.shape[1]
    core = lax.axis_index("c")
    chip = 2 * lax.axis_index("x") + lax.axis_index("y")
    place = jnp.stack([core, chip]).astype(jnp.int32)
    xs = x.reshape(s, d)

    full = []
    for l in range(depth):
        taps = jnp.pad(conv_w[l], ((0, 5), (0, 0)))
        got = _all_gather_weights([_cast_into(W[n], l, place, name=f"cast_{n}") for n in BIG], [taps],
                                  name=f"gather_weights_{l}")
        f_local = conv_w.shape[2]
        cw_full = jnp.transpose(got[4][:, :3, :], (1, 0, 2)).reshape(3, N_CHIPS * f_local)
        full.append(dict(w_in=got[0], w_out=got[1].reshape(-1, d), w_up=got[2], w_down=got[3].reshape(-1, d), conv_w=cw_full))

    saved = []
    cur = xs
    for l in range(depth):
        fw = full[l]
        h = _rmsnorm_fwd(cur, attn_norm_g[l], name="attn_norm")
        p = _mm(h, fw["w_in"], "nn", b_split=N_CHIPS, caps=(2048, 256, 2048), name="proj_in")
        att, o_raw, lsum = _attn_fwd(p, q_norm_g[l], k_norm_g[l], out_norm_a_g[l], n_heads, name="attn_fwd")
        sg = _sgu_fwd(p, sgu_w[l], sgu_b[l], sgu_norm_g[l], out_norm_b_g[l], n_heads, name="sgu_fwd")
        mix = jnp.stack([att, sg])
        x1 = _mm(mix, fw["w_out"], "nn", a_split=2, res=cur, caps=(2048, 512, 1024), name="proj_out")
        h2 = _rmsnorm_fwd(x1, ffn_norm_g[l], name="ffn_norm")
        up = _mm(h2, fw["w_up"], "nn", b_split=N_CHIPS, caps=(2048, 256, 2048), name="ffn_up")
        act = _conv_fwd(up, fw["conv_w"], conv_b[l], name="conv_fwd")
        x2 = _mm(act, fw["w_down"], "nn", res=x1, caps=(2048, 512, 512), name="ffn_down")
        saved.append(dict(x0=cur, h=h, p=p, o_raw=o_raw, lsum=lsum, mix=mix, x1=x1, h2=h2, up=up, act=act))
        cur = x2

    dx, dxb, sq = _loss_head(cur, loss_target.reshape(s, d), name="loss_head")
    loss = lax.psum(sq[0, 0] * (0.5 / d), ("x", "y", "c"))

    small_grads = {n: [None] * depth for n in SMALL + ["conv_w"]}
    reduced = {n: None for n in BIG}
    for l in reversed(range(depth)):
        fw, sv = full[l], saved[l]
        dact = _mm(dxb, fw["w_down"], "nt", caps=(2048, 512, 2048), name="d_act")
        g_down = _mm(sv["act"], dxb, "tn", caps=(512, 2048, 2048), name="g_down")
        dup, dcw, dcb = _conv_bwd(sv["up"], dact, fw["conv_w"], conv_b[l], name="conv_bwd")
        g_up = _mm(sv["h2"], dup, "tn", b_split=2, o_split=N_CHIPS, caps=(2048, 256, 2048), name="g_up")
        dh2 = _mm(dup, fw["w_up"], "nt", a_split=2, b_split=N_CHIPS, caps=(1024, 512, 2816), name="d_h2")
        dx1, dx1b, dg_ffn = _rmsnorm_bwd(sv["x1"], ffn_norm_g[l], dh2, dx, name="ffn_norm_bwd")
        dmix = _mm(dx1b, fw["w_out"], "nt", caps=(2048, 512, 2048), name="d_mix")
        g_out = _mm(sv["mix"], dx1b, "tn", a_split=2, caps=(512, 2048, 2048), name="g_out")
        dqkv, dgq, dgk, dgoa = _attn_bwd(sv["p"], sv["o_raw"], sv["lsum"], dmix, q_norm_g[l], k_norm_g[l], out_norm_a_g[l],
                                         n_heads, name="attn_bwd")
        duv, dsw, dsb, dgv, dgob = _sgu_bwd(sv["p"], dmix, sgu_w[l], sgu_b[l], sgu_norm_g[l], out_norm_b_g[l], n_heads,
                                            name="sgu_bwd")
        dp = jnp.concatenate([dqkv[0], dqkv[1], dqkv[2], duv[0], duv[1]], axis=1)
        g_in = _mm(sv["h"], dp, "tn", o_split=N_CHIPS, caps=(2048, 256, 2048), name="g_in")
        dh = _mm(dp, fw["w_in"], "nt", b_split=N_CHIPS, caps=(2048, 512, 1280), name="d_h")
        dx, dxb, dg_attn = _rmsnorm_bwd(sv["x0"], attn_norm_g[l], dh, dx1, name="attn_norm_bwd")

        small_grads["attn_norm_g"][l] = dg_attn.reshape(d)
        small_grads["q_norm_g"][l] = jnp.sum(dgq, axis=(0, 1))
        small_grads["k_norm_g"][l] = jnp.sum(dgk, axis=(0, 1))
        small_grads["sgu_norm_g"][l] = dgv.reshape(-1, BLK)
        small_grads["sgu_w"][l] = dsw
        small_grads["sgu_b"][l] = dsb.reshape(-1, BLK)
        small_grads["out_norm_a_g"][l] = dgoa.reshape(-1, BLK)
        small_grads["out_norm_b_g"][l] = dgob.reshape(-1, BLK)
        small_grads["ffn_norm_g"][l] = dg_ffn.reshape(d)
        small_grads["conv_b"][l] = dcb[:, 0, :].reshape(-1)
        small_grads["conv_w"][l] = jnp.transpose(dcw[:, :3, :], (1, 0, 2)).reshape(3, -1)

        grads = [g_in, g_out.reshape(N_CHIPS, -1, d), g_up, g_down.reshape(N_CHIPS, -1, d)]
        theirs = _pair_exchange(grads, name="pair_exchange")
        sums = [_pair_sum(g, t, place, name=f"pair_sum_{n}") for g, t, n in zip(grads, theirs, BIG)]
        got = _chip_exchange(sums, name="chip_exchange")
        for n, g, t, gt in zip(BIG, grads, theirs, got):
            reduced[n] = _chip_sum(g, t, gt, place, reduced[n], l, depth, name=f"chip_sum_{n}")

    shared = _share_halves([reduced[n] for n in BIG], name="share_halves")
    G = {n: t.reshape(W[n].shape) for n, t in zip(BIG, shared)}

    names = SMALL + ["conv_w"]
    pack = _pack([jnp.stack(small_grads[n]) for n in names])
    total = _sum_devices(_all_gather_small(pack, name="gather_small"), name="sum_small")
    f_full = conv_b.shape[1]
    shapes = [W[n].shape for n in SMALL] + [(depth, 3, f_full)]
    for n, t in zip(names, _unpack(total, shapes)):
        G[n] = t
    f_local = conv_w.shape[2]
    G["conv_w"] = lax.dynamic_slice_in_dim(G["conv_w"], chip * f_local, f_local, axis=2)

    D_, NM, NV = {}, {}, {}
    for n in BIG + ["conv_w"]:
        D_[n], NM[n], NV[n] = _adamw(W[n], G[n], M[n], V[n], name=f"adamw_{n}")
    small_shapes = [W[n].shape for n in SMALL]
    res = _adamw(_pack([W[n] for n in SMALL]), _pack([G[n] for n in SMALL]), _pack([M[n] for n in SMALL]),
                 _pack([V[n] for n in SMALL]), name="adamw_small")
    for dst, t in zip((D_, NM, NV), res):
        for n, u in zip(SMALL, _unpack(t, small_shapes)):
            dst[n] = u

    return (loss, dx.reshape(x.shape), *[G[n] for n in ORDER], *[D_[n] for n in ORDER], *[NM[n] for n in ORDER],
            *[NV[n] for n in ORDER])
```

```python
import functools
import math

import jax
import jax.numpy as jnp
from jax import lax
from jax.experimental import pallas as pl
from jax.experimental.pallas import tpu as pltpu

F32 = jnp.float32
BF16 = jnp.bfloat16
EPS = 1e-6
BLK = 128
N_CHIPS = 4
N_DEV = 8
ADAM_LR, ADAM_B1, ADAM_B2, ADAM_EPS, ADAM_WD, ADAM_STEP = 0.001, 0.9, 0.999, 1e-08, 0.01, 10
VMEM_BIG = 48 * 1024 * 1024
MESH = pl.DeviceIdType.MESH
ANY = pl.BlockSpec(memory_space=pl.ANY)


def _pick(dim, prefs):
    for t in prefs:
        if dim % t == 0:
            return t
    raise ValueError(f"no tile in {prefs} divides {dim}")


def _params(sem=None, vmem=None):
    return pltpu.CompilerParams(dimension_semantics=sem, vmem_limit_bytes=vmem)


def _ldims(arr, split):
    if split == 1:
        return arr.shape
    p, r, cs = arr.shape
    assert p == split
    return (r, p * cs)


def _spec(tr, tc, split, cols, rc):
    if split == 1:
        return pl.BlockSpec((tr, tc), lambda i, j, k: rc(i, j, k))
    per = (cols // split) // tc

    def imap(i, j, k):
        r, c = rc(i, j, k)
        return (c // per, r, c % per)

    return pl.BlockSpec((None, tr, tc), imap)


def _fit(unit, cap):
    return max(t for t in range(128, min(unit, cap) + 1, 128) if unit % t == 0)


def _mm(a, b, mode, *, name, caps, a_split=1, b_split=1, o_split=1, out_dtype=F32, res=None, dep=None):
    ar, ac = _ldims(a, a_split)
    br, bc = _ldims(b, b_split)
    if mode == "nn":
        m, k, n = ar, ac, bc
        assert br == k
        ku, nu, mu = math.gcd(k // a_split, k), math.gcd(n // b_split, n // o_split), m
    elif mode == "nt":
        m, k, n = ar, ac, br
        assert bc == k
        ku, nu, mu = math.gcd(k // a_split, k // b_split), n // o_split, m
    else:
        k, m, n = ar, ac, bc
        assert br == k
        ku, nu, mu = k, math.gcd(n // b_split, n // o_split), m // a_split
    tm, tn, tk = _fit(mu, caps[0]), _fit(nu, caps[1]), _fit(ku, caps[2])
    nk = k // tk
    if mode == "nn":
        a_spec = _spec(tm, tk, a_split, k, lambda i, j, kk: (i, kk))
        b_spec = _spec(tk, tn, b_split, n, lambda i, j, kk: (kk, j))
    elif mode == "nt":
        a_spec = _spec(tm, tk, a_split, k, lambda i, j, kk: (i, kk))
        b_spec = _spec(tn, tk, b_split, k, lambda i, j, kk: (j, kk))
    else:
        a_spec = _spec(tk, tm, a_split, m, lambda i, j, kk: (kk, i))
        b_spec = _spec(tk, tn, b_split, n, lambda i, j, kk: (kk, j))
    o_spec = _spec(tm, tn, o_split, n, lambda i, j, kk: (i, j))
    dims = {"nn": (((1,), (0,)), ((), ())), "nt": (((1,), (1,)), ((), ())), "tn": (((0,), (0,)), ((), ()))}[mode]

    def body(a_ref, b_ref, *rest):
        if dep is not None:
            rest = rest[1:]
        if res is None:
            o_ref, acc = rest
        else:
            r_ref, o_ref, acc = rest
        kk = pl.program_id(2)

        @pl.when(kk == 0)
        def _():
            acc[...] = jnp.zeros_like(acc)

        acc[...] += lax.dot_general(a_ref[...].astype(BF16), b_ref[...].astype(BF16), dims, preferred_element_type=F32)

        @pl.when(kk == nk - 1)
        def _():
            out = acc[...]
            if res is not None:
                out = out + r_ref[...]
            o_ref[...] = out.astype(o_ref.dtype)

    in_specs, args = [a_spec, b_spec], [a, b]
    if dep is not None:
        in_specs.append(ANY)
        args.append(dep)
    if res is not None:
        in_specs.append(pl.BlockSpec((tm, tn), lambda i, j, kk: (i, j)))
        args.append(res)
    out_shape = (m, n) if o_split == 1 else (o_split, m, n // o_split)
    return pl.pallas_call(
        body, name=name, grid=(m // tm, n // tn, nk), in_specs=in_specs, out_specs=o_spec,
        out_shape=jax.ShapeDtypeStruct(out_shape, out_dtype), scratch_shapes=[pltpu.VMEM((tm, tn), F32)],
        compiler_params=_params(("parallel", "parallel", "arbitrary"), VMEM_BIG),
    )(*args)


def _rstd(v):
    return lax.rsqrt(jnp.mean(v * v, axis=-1, keepdims=True) + EPS)


def _norm_bwd(v, r, gain, dout):
    a = dout * gain
    dv = r * (a - v * (r * r * jnp.mean(a * v, axis=-1, keepdims=True)))
    return dv, dout * v * r


def _rmsnorm_fwd(x, g, *, name):
    s, d = x.shape
    tr = _pick(s, (256, 128))

    def body(x_ref, g_ref, o_ref):
        v = x_ref[...]
        o_ref[...] = (v * _rstd(v) * g_ref[...]).astype(o_ref.dtype)

    return pl.pallas_call(
        body, name=name, grid=(s // tr,),
        in_specs=[pl.BlockSpec((tr, d), lambda i: (i, 0)), pl.BlockSpec((1, d), lambda i: (0, 0))],
        out_specs=pl.BlockSpec((tr, d), lambda i: (i, 0)), out_shape=jax.ShapeDtypeStruct((s, d), BF16),
        compiler_params=_params(("parallel",)),
    )(x, g.reshape(1, d))


def _rmsnorm_bwd(x, g, dh, dres, *, name):
    s, d = x.shape
    tr = _pick(s, (256, 128))

    def body(x_ref, g_ref, dh_ref, dres_ref, dx_ref, dxb_ref, dg_ref):
        v = x_ref[...]
        dv, dgr = _norm_bwd(v, _rstd(v), g_ref[...], dh_ref[...])
        dx = dres_ref[...] + dv
        dx_ref[...] = dx
        dxb_ref[...] = dx.astype(BF16)
        part = jnp.sum(dgr, axis=0, keepdims=True)

        @pl.when(pl.program_id(0) == 0)
        def _():
            dg_ref[...] = part

        @pl.when(pl.program_id(0) > 0)
        def _():
            dg_ref[...] += part

    row = pl.BlockSpec((tr, d), lambda i: (i, 0))
    one = pl.BlockSpec((1, d), lambda i: (0, 0))
    return pl.pallas_call(
        body, name=name, grid=(s // tr,), in_specs=[row, one, row, row], out_specs=[row, row, one],
        out_shape=[jax.ShapeDtypeStruct((s, d), F32), jax.ShapeDtypeStruct((s, d), BF16), jax.ShapeDtypeStruct((1, d), F32)],
        compiler_params=_params(("arbitrary",)),
    )(x, g.reshape(1, d), dh, dres)


def _loss_head(y, target, *, name):
    s, d = y.shape
    tr = _pick(s, (256, 128))

    def body(y_ref, t_ref, dy_ref, dyb_ref, ls_ref):
        e = y_ref[...] - t_ref[...]
        dy = e * (1.0 / d)
        dy_ref[...] = dy
        dyb_ref[...] = dy.astype(BF16)
        part = jnp.full(ls_ref.shape, jnp.sum(e * e), F32)

        @pl.when(pl.program_id(0) == 0)
        def _():
            ls_ref[...] = part

        @pl.when(pl.program_id(0) > 0)
        def _():
            ls_ref[...] += part

    row = pl.BlockSpec((tr, d), lambda i: (i, 0))
    return pl.pallas_call(
        body, name=name, grid=(s // tr,), in_specs=[row, row], out_specs=[row, row, pl.BlockSpec((8, 128), lambda i: (0, 0))],
        out_shape=[jax.ShapeDtypeStruct((s, d), F32), jax.ShapeDtypeStruct((s, d), BF16), jax.ShapeDtypeStruct((8, 128), F32)],
        compiler_params=_params(("arbitrary",)),
    )(y, target)


def _iota2(axis):
    return lax.broadcasted_iota(jnp.int32, (BLK, BLK), axis)


def _tri_sum(v, tri):
    hi = v.astype(BF16)
    lo = (v - hi.astype(F32)).astype(BF16)
    return jnp.dot(hi, tri, preferred_element_type=F32) + jnp.dot(lo, tri, preferred_element_type=F32)


def _dot_nt(a, b):
    return lax.dot_general(a, b, (((1,), (1,)), ((), ())), preferred_element_type=F32)


def _dot_tn(a, b):
    return lax.dot_general(a, b, (((0,), (0,)), ((), ())), preferred_element_type=F32)


TQ_MAX = 512


def _sb_logits(qi, kj, q0, j, scale):
    z = _dot_nt(qi, kj) * scale
    tq = qi.shape[0]
    mask = (j * BLK + lax.broadcasted_iota(jnp.int32, (tq, BLK), 1)) < (q0 + lax.broadcasted_iota(jnp.int32, (tq, BLK), 0))
    lb = jnp.minimum(z, 0.0) - jnp.log(1.0 + jnp.exp(-jnp.abs(z)))
    l1m = jnp.where(mask, lb - z, 0.0)
    return lb, l1m, mask


def _attn_fwd(p, gq, gk, go, n_heads, *, name):
    s = p.shape[0]
    tq = min(TQ_MAX, s)
    per = tq // BLK
    scale = BLK ** -0.5

    def body(q_ref, k_ref, v_ref, gq_ref, gk_ref, go_ref, att_ref, o_ref, l_ref, qn, kn, vb):
        q = q_ref[...]
        k = k_ref[...]
        qn[...] = (q * _rstd(q) * gq_ref[...]).astype(BF16)
        kn[...] = (k * _rstd(k) * gk_ref[...]).astype(BF16)
        vb[...] = v_ref[...].astype(BF16)
        tri_gt = (_iota2(0) > _iota2(1)).astype(BF16)

        def q_block(i, _):
            q0 = pl.multiple_of(i * tq, tq)
            rows = pl.ds(q0, tq)
            qi = qn[rows, :]
            n_keys = (i + 1) * per

            def k_block(jj, carry):
                later, acc = carry
                j = n_keys - 1 - jj
                cols = pl.ds(pl.multiple_of(j * BLK, BLK), BLK)
                lb, l1m, mask = _sb_logits(qi, kn[cols, :], q0, j, scale)
                tail = _tri_sum(l1m, tri_gt) + later
                a = jnp.where(mask, jnp.exp(lb + tail), 0.0)
                acc = acc + jnp.dot(a.astype(BF16), vb[cols, :], preferred_element_type=F32)
                return later + jnp.sum(l1m, axis=1, keepdims=True), acc

            total, acc = lax.fori_loop(0, n_keys, k_block, (jnp.zeros((tq, 1), F32), jnp.zeros((tq, BLK), F32)))
            o_ref[rows, :] = acc
            l_ref[rows, :] = total
            att_ref[rows, :] = (acc * _rstd(acc) * go_ref[...]).astype(att_ref.dtype)
            return 0

        lax.fori_loop(0, s // tq, q_block, 0)

    def col(off):
        return pl.BlockSpec((s, BLK), lambda h: (0, off + h))

    gain = pl.BlockSpec((1, BLK), lambda h: (0, 0))
    per_head = pl.BlockSpec((None, 1, BLK), lambda h: (h, 0, 0))
    return pl.pallas_call(
        body, name=name, grid=(n_heads,),
        in_specs=[col(0), col(n_heads), col(2 * n_heads), gain, gain, per_head],
        out_specs=[col(0), col(0), pl.BlockSpec((None, s, 1), lambda h: (h, 0, 0))],
        out_shape=[jax.ShapeDtypeStruct((s, n_heads * BLK), BF16), jax.ShapeDtypeStruct((s, n_heads * BLK), F32),
                   jax.ShapeDtypeStruct((n_heads, s, 1), F32)],
        scratch_shapes=[pltpu.VMEM((s, BLK), BF16)] * 3,
        compiler_params=_params(("parallel",), VMEM_BIG),
    )(p, p, p, gq.reshape(1, BLK), gk.reshape(1, BLK), go.reshape(n_heads, 1, BLK))


def _attn_bwd(p, o_raw, lsum, dmix, gq, gk, go, n_heads, *, name):
    s = p.shape[0]
    tq = min(TQ_MAX, s)
    per = tq // BLK
    scale = BLK ** -0.5

    def body(q_ref, k_ref, v_ref, o_ref, l_ref, da_ref, gq_ref, gk_ref, go_ref,
             dqkv_ref, dgq_ref, dgk_ref, dgo_ref, qn, kn, vb, dob, dqn, dkn, dvv):
        q = q_ref[...]
        k = k_ref[...]
        rq = _rstd(q)
        rk = _rstd(k)
        qn[...] = (q * rq * gq_ref[...]).astype(BF16)
        kn[...] = (k * rk * gk_ref[...]).astype(BF16)
        vb[...] = v_ref[...].astype(BF16)
        o = o_ref[...]
        do, dgo_rows = _norm_bwd(o, _rstd(o), go_ref[...], da_ref[...])
        dob[...] = do.astype(BF16)
        dgo_ref[...] = jnp.sum(dgo_rows, axis=0, keepdims=True)
        dkn[...] = jnp.zeros_like(dkn)
        dvv[...] = jnp.zeros_like(dvv)
        tri_gt = (_iota2(0) > _iota2(1)).astype(BF16)
        tri_lt = (_iota2(0) < _iota2(1)).astype(BF16)

        def q_block(i, _):
            q0 = pl.multiple_of(i * tq, tq)
            rows = pl.ds(q0, tq)
            qi = qn[rows, :]
            doi = dob[rows, :]
            total = l_ref[rows, :]

            def k_block(j, carry):
                seen, gsum, dq = carry
                cols = pl.ds(pl.multiple_of(j * BLK, BLK), BLK)
                kj = kn[cols, :]
                lb, l1m, mask = _sb_logits(qi, kj, q0, j, scale)
                seen = seen + jnp.sum(l1m, axis=1, keepdims=True)
                tail = _tri_sum(l1m, tri_gt) + (total - seen)
                a = jnp.where(mask, jnp.exp(lb + tail), 0.0)
                g = _dot_nt(doi, vb[cols, :]) * a
                before = _tri_sum(g, tri_lt) + gsum
                beta = jnp.exp(lb)
                dz = jnp.where(mask, g * (1.0 - beta) - beta * before, 0.0)
                dzs = (dz * scale).astype(BF16)
                dkn[cols, :] += _dot_tn(dzs, qi)
                dvv[cols, :] += _dot_tn(a.astype(BF16), doi)
                dq = dq + jnp.dot(dzs, kj, preferred_element_type=F32)
                return seen, gsum + jnp.sum(g, axis=1, keepdims=True), dq

            zero = jnp.zeros((tq, 1), F32)
            _, _, dq = lax.fori_loop(0, (i + 1) * per, k_block, (zero, zero, jnp.zeros((tq, BLK), F32)))
            dqn[rows, :] = dq
            return 0

        lax.fori_loop(0, s // tq, q_block, 0)
        dq_raw, dgq_rows = _norm_bwd(q, rq, gq_ref[...], dqn[...])
        dk_raw, dgk_rows = _norm_bwd(k, rk, gk_ref[...], dkn[...])
        dqkv_ref[0] = dq_raw.astype(BF16)
        dqkv_ref[1] = dk_raw.astype(BF16)
        dqkv_ref[2] = dvv[...].astype(BF16)
        dgq_ref[...] = jnp.sum(dgq_rows, axis=0, keepdims=True)
        dgk_ref[...] = jnp.sum(dgk_rows, axis=0, keepdims=True)

    def col(off):
        return pl.BlockSpec((s, BLK), lambda h: (0, off + h))

    gain = pl.BlockSpec((1, BLK), lambda h: (0, 0))
    per_head = pl.BlockSpec((None, 1, BLK), lambda h: (h, 0, 0))
    head_gain = jax.ShapeDtypeStruct((n_heads, 1, BLK), F32)
    return pl.pallas_call(
        body, name=name, grid=(n_heads,),
        in_specs=[col(0), col(n_heads), col(2 * n_heads), col(0), pl.BlockSpec((None, s, 1), lambda h: (h, 0, 0)), col(0),
                  gain, gain, per_head],
        out_specs=[pl.BlockSpec((3, s, BLK), lambda h: (0, 0, h)), per_head, per_head, per_head],
        out_shape=[jax.ShapeDtypeStruct((3, s, n_heads * BLK), BF16), head_gain, head_gain, head_gain],
        scratch_shapes=[pltpu.VMEM((s, BLK), BF16)] * 4 + [pltpu.VMEM((s, BLK), F32)] * 3,
        compiler_params=_params(("parallel",), VMEM_BIG),
    )(p, p, p, o_raw, lsum, dmix, gq.reshape(1, BLK), gk.reshape(1, BLK), go.reshape(n_heads, 1, BLK))


_INV_SQRT2 = 0.7071067811865476
_INV_SQRT2PI = 0.3989422804014327


def _gelu(x):
    return 0.5 * x * (1.0 + lax.erf(x * _INV_SQRT2))


def _gelu_grad(x):
    return 0.5 * (1.0 + lax.erf(x * _INV_SQRT2)) + x * jnp.exp(-0.5 * x * x) * _INV_SQRT2PI


def _sgu_fwd(p, w, b, gv, gout, n_heads, *, name):
    s = p.shape[0]
    n_groups = w.shape[0]
    nb = s // BLK

    def body(u_ref, v_ref, w_ref, b_ref, gv_ref, go_ref, out_ref):
        wt = jnp.where(_iota2(0) >= _iota2(1), w_ref[...], 0.0).astype(BF16)
        bias = b_ref[...]

        def chunk(c, _):
            rows = pl.ds(pl.multiple_of(c * BLK, BLK), BLK)
            u = _gelu(u_ref[rows, :])
            vv = _gelu(v_ref[rows, :])
            vs = vv * _rstd(vv) * gv_ref[...]
            gated = u * (jnp.dot(wt, vs.astype(BF16), preferred_element_type=F32) + bias)
            out_ref[rows, :] = (gated * _rstd(gated) * go_ref[...]).astype(out_ref.dtype)
            return 0

        lax.fori_loop(0, nb, chunk, 0)

    def col(off):
        return pl.BlockSpec((s, BLK), lambda g: (0, off + g))

    per_group = pl.BlockSpec((None, 1, BLK), lambda g: (g, 0, 0))
    return pl.pallas_call(
        body, name=name, grid=(n_groups,),
        in_specs=[col(3 * n_heads), col(3 * n_heads + n_groups), pl.BlockSpec((None, BLK, BLK), lambda g: (g, 0, 0)),
                  pl.BlockSpec((None, BLK, 1), lambda g: (g, 0, 0)), per_group, per_group],
        out_specs=col(0), out_shape=jax.ShapeDtypeStruct((s, n_groups * BLK), BF16),
        compiler_params=_params(("parallel",), VMEM_BIG),
    )(p, p, w, b.reshape(n_groups, BLK, 1), gv.reshape(n_groups, 1, BLK), gout.reshape(n_groups, 1, BLK))


def _sgu_bwd(p, dmix, w, b, gv, gout, n_heads, *, name):
    s = p.shape[0]
    n_groups = w.shape[0]
    nb = s // BLK

    def body(u_ref, v_ref, ds_ref, w_ref, b_ref, gv_ref, go_ref, duv_ref, dw_ref, db_ref, dgv_ref, dgo_ref):
        lower = _iota2(0) >= _iota2(1)
        wt = jnp.where(lower, w_ref[...], 0.0).astype(BF16)
        bias = b_ref[...]

        def chunk(c, carry):
            dw, db, dgv, dgo = carry
            rows = pl.ds(pl.multiple_of(c * BLK, BLK), BLK)
            up = u_ref[rows, :]
            vp = v_ref[rows, :]
            u = _gelu(up)
            vv = _gelu(vp)
            rv = _rstd(vv)
            vsb = (vv * rv * gv_ref[...]).astype(BF16)
            mixed = jnp.dot(wt, vsb, preferred_element_type=F32) + bias
            gated = u * mixed
            dgated, dgo_rows = _norm_bwd(gated, _rstd(gated), go_ref[...], ds_ref[rows, :])
            dmixed = dgated * u
            dmb = dmixed.astype(BF16)
            dvs = _dot_tn(wt, dmb)
            dvv, dgv_rows = _norm_bwd(vv, rv, gv_ref[...], dvs)
            duv_ref[0, rows, :] = (dgated * mixed * _gelu_grad(up)).astype(BF16)
            duv_ref[1, rows, :] = (dvv * _gelu_grad(vp)).astype(BF16)
            return (dw + _dot_nt(dmb, vsb), db + jnp.sum(dmixed, axis=1, keepdims=True),
                    dgv + jnp.sum(dgv_rows, axis=0, keepdims=True), dgo + jnp.sum(dgo_rows, axis=0, keepdims=True))

        row0 = jnp.zeros((1, BLK), F32)
        dw, db, dgv, dgo = lax.fori_loop(0, nb, chunk, (jnp.zeros((BLK, BLK), F32), jnp.zeros((BLK, 1), F32), row0, row0))
        dw_ref[...] = jnp.where(lower, dw, 0.0)
        db_ref[...] = db
        dgv_ref[...] = dgv
        dgo_ref[...] = dgo

    def col(off):
        return pl.BlockSpec((s, BLK), lambda g: (0, off + g))

    per_group = pl.BlockSpec((None, 1, BLK), lambda g: (g, 0, 0))
    square = pl.BlockSpec((None, BLK, BLK), lambda g: (g, 0, 0))
    column = pl.BlockSpec((None, BLK, 1), lambda g: (g, 0, 0))
    gain = jax.ShapeDtypeStruct((n_groups, 1, BLK), F32)
    return pl.pallas_call(
        body, name=name, grid=(n_groups,),
        in_specs=[col(3 * n_heads), col(3 * n_heads + n_groups), col(n_heads), square, column, per_group, per_group],
        out_specs=[pl.BlockSpec((2, s, BLK), lambda g: (0, 0, g)), square, column, per_group, per_group],
        out_shape=[jax.ShapeDtypeStruct((2, s, n_groups * BLK), BF16), jax.ShapeDtypeStruct((n_groups, BLK, BLK), F32),
                   jax.ShapeDtypeStruct((n_groups, BLK, 1), F32), gain, gain],
        compiler_params=_params(("parallel",), VMEM_BIG),
    )(p, p, dmix, w, b.reshape(n_groups, BLK, 1), gv.reshape(n_groups, 1, BLK), gout.reshape(n_groups, 1, BLK))


CONV_ROWS = 256
HALO = 8


def _shift_down(ref, r0, n, first):
    cur = ref[pl.ds(r0, n), :]
    prev = jnp.zeros((HALO, cur.shape[1]), F32) if first else ref[pl.ds(r0 - HALO, HALO), :]
    ext = jnp.concatenate([prev, cur], axis=0)
    return pltpu.roll(ext, 1, 0)[HALO:], pltpu.roll(ext, 2, 0)[HALO:], cur


def _shift_up(ref, r0, n, last):
    cur = ref[pl.ds(r0, n), :]
    nxt = jnp.zeros((HALO, cur.shape[1]), F32) if last else ref[pl.ds(r0 + n, HALO), :]
    ext = jnp.concatenate([cur, nxt], axis=0)
    return cur, pltpu.roll(ext, n + HALO - 1, 0)[:n], pltpu.roll(ext, n + HALO - 2, 0)[:n]


def _conv_rows(x1, x2, x0, w_ref, b_ref):
    return ((b_ref[...] + x2 * w_ref[0:1, :]) + x1 * w_ref[1:2, :]) + x0 * w_ref[2:3, :]


def _conv_specs(s, f, tc):
    nf = f // tc
    gate = pl.BlockSpec((s, tc), lambda n: (0, n))
    val = pl.BlockSpec((s, tc), lambda n: (0, nf + n))
    wg = pl.BlockSpec((3, tc), lambda n: (0, n))
    wv = pl.BlockSpec((3, tc), lambda n: (0, nf + n))
    bg = pl.BlockSpec((1, tc), lambda n: (0, n))
    bv = pl.BlockSpec((1, tc), lambda n: (0, nf + n))
    return nf, gate, val, wg, wv, bg, bv


def _conv_fwd(up, cw, cb, *, name):
    s, f2 = up.shape
    f = f2 // 2
    tc = _pick(f, (256, 128))
    cr = min(CONV_ROWS, s)
    nf, gate, val, wg, wv, bg, bv = _conv_specs(s, f, tc)

    def body(g_ref, v_ref, wg_ref, wv_ref, bg_ref, bv_ref, out_ref):
        for r0 in range(0, s, cr):
            gc = _conv_rows(*_shift_down(g_ref, r0, cr, r0 == 0), wg_ref, bg_ref)
            vc = _conv_rows(*_shift_down(v_ref, r0, cr, r0 == 0), wv_ref, bv_ref)
            out_ref[pl.ds(r0, cr), :] = (gc * jax.nn.sigmoid(gc) * vc).astype(out_ref.dtype)

    return pl.pallas_call(
        body, name=name, grid=(nf,), in_specs=[gate, val, wg, wv, bg, bv], out_specs=gate,
        out_shape=jax.ShapeDtypeStruct((s, f), BF16), compiler_params=_params(("parallel",), VMEM_BIG),
    )(up, up, cw, cw, cb.reshape(1, f2), cb.reshape(1, f2))


def _conv_bwd(up, dact, cw, cb, *, name):
    s, f2 = up.shape
    f = f2 // 2
    tc = _pick(f, (256, 128))
    cr = min(CONV_ROWS, s)
    nf, gate, val, wg, wv, bg, bv = _conv_specs(s, f, tc)

    def body(g_ref, v_ref, da_ref, wg_ref, wv_ref, bg_ref, bv_ref, dup_ref, dw_ref, db_ref, dgc, dvc):
        zero = jnp.zeros((1, tc), F32)
        sums = [[zero] * 4, [zero] * 4]
        for r0 in range(0, s, cr):
            rows = pl.ds(r0, cr)
            gx = _shift_down(g_ref, r0, cr, r0 == 0)
            vx = _shift_down(v_ref, r0, cr, r0 == 0)
            gc = _conv_rows(*gx, wg_ref, bg_ref)
            vc = _conv_rows(*vx, wv_ref, bv_ref)
            sig = jax.nn.sigmoid(gc)
            da = da_ref[rows, :]
            d_gate = da * vc * (sig * (1.0 + gc * (1.0 - sig)))
            d_val = da * (gc * sig)
            dgc[rows, :] = d_gate
            dvc[rows, :] = d_val
            for part, (dc, (x1, x2, x0)) in enumerate(((d_gate, gx), (d_val, vx))):
                for tap, xs in enumerate((x2, x1, x0)):
                    sums[part][tap] = sums[part][tap] + jnp.sum(dc * xs, axis=0, keepdims=True)
                sums[part][3] = sums[part][3] + jnp.sum(dc, axis=0, keepdims=True)
        dw_ref[...] = jnp.zeros_like(dw_ref)
        db_ref[...] = jnp.zeros_like(db_ref)
        for part, (dc_ref, w_ref) in enumerate(((dgc, wg_ref), (dvc, wv_ref))):
            for tap in range(3):
                dw_ref[part, tap:tap + 1, :] = sums[part][tap]
            db_ref[part, 0:1, :] = sums[part][3]
            for r0 in range(0, s, cr):
                d0, d1, d2 = _shift_up(dc_ref, r0, cr, r0 + cr == s)
                dup_ref[part, pl.ds(r0, cr), :] = ((d0 * w_ref[2:3, :] + d1 * w_ref[1:2, :]) + d2 * w_ref[0:1, :]).astype(BF16)

    small = pl.BlockSpec((2, 8, tc), lambda n: (0, 0, n))
    return pl.pallas_call(
        body, name=name, grid=(nf,), in_specs=[gate, val, gate, wg, wv, bg, bv],
        out_specs=[pl.BlockSpec((2, s, tc), lambda n: (0, 0, n)), small, small],
        out_shape=[jax.ShapeDtypeStruct((2, s, f), BF16), jax.ShapeDtypeStruct((2, 8, f), F32),
                   jax.ShapeDtypeStruct((2, 8, f), F32)],
        scratch_shapes=[pltpu.VMEM((s, tc), F32)] * 2, compiler_params=_params(("parallel",), VMEM_BIG),
    )(up, up, dact, cw, cw, cb.reshape(1, f2), cb.reshape(1, f2))


def _adamw(w, g, m, v, *, name):
    shape = w.shape
    cols = shape[-1]
    rows = w.size // cols
    if rows * cols * 4 <= (1 << 20):
        tr = rows
    else:
        tr = next(t for t in (1024, 512, 256, 128, 64, 32, 16, 8) if rows % t == 0 and (t * cols * 4 <= (1 << 20) or t == 8))

    def body(w_ref, g_ref, m_ref, v_ref, d_ref, nm_ref, nv_ref):
        gr = g_ref[...]
        nm = ADAM_B1 * m_ref[...] + (1.0 - ADAM_B1) * gr
        nv = ADAM_B2 * v_ref[...] + (1.0 - ADAM_B2) * (gr * gr)
        m_hat = nm / (1.0 - ADAM_B1 ** ADAM_STEP)
        v_hat = nv / (1.0 - ADAM_B2 ** ADAM_STEP)
        d_ref[...] = -ADAM_LR * (m_hat / (jnp.sqrt(v_hat) + ADAM_EPS) + ADAM_WD * w_ref[...])
        nm_ref[...] = nm
        nv_ref[...] = nv

    blk = pl.BlockSpec((tr, cols), lambda i: (i, 0))
    out = jax.ShapeDtypeStruct((rows, cols), F32)
    res = pl.pallas_call(
        body, name=name, grid=(rows // tr,), in_specs=[blk] * 4, out_specs=[blk] * 3, out_shape=[out] * 3,
        compiler_params=_params(("parallel",)),
    )(*[t.reshape(rows, cols) for t in (w, g, m, v)])
    return [t.reshape(shape) for t in res]


def _place():
    x, y, c = lax.axis_index("x"), lax.axis_index("y"), lax.axis_index("c")
    others = [(1 - x, y), (x, 1 - y), (1 - x, 1 - y)]
    return x, y, c, others


def _remote(src, dst, send_sem, recv_sem, device):
    return pltpu.make_async_remote_copy(src_ref=src, dst_ref=dst, send_sem=send_sem, recv_sem=recv_sem, device_id=device,
                                        device_id_type=MESH)


def _hbm_call(body, name, args, out_shapes, n_sems, n_local, aliases=None):
    return pl.pallas_call(
        body, name=name, in_specs=[ANY] * len(args), out_specs=[ANY] * len(out_shapes), out_shape=out_shapes,
        scratch_shapes=[pltpu.SemaphoreType.DMA((n_sems,)), pltpu.SemaphoreType.DMA((n_sems,)),
                        pltpu.SemaphoreType.DMA((max(n_local, 1),))],
        input_output_aliases=aliases or {}, compiler_params=pltpu.CompilerParams(has_side_effects=True),
    )(*args)


def _all_gather_weights(halved, whole, *, name):
    nh, nw = len(halved), len(whole)
    arrays = list(halved) + list(whole)

    def body(*refs):
        srcs, outs = refs[:nh + nw], refs[nh + nw:2 * (nh + nw)]
        send, recv, local = refs[2 * (nh + nw):]
        x, y, c, others = _place()
        me = 2 * x + y
        locals_ = [pltpu.make_async_copy(srcs[nh + a], outs[nh + a].at[me], local.at[a]) for a in range(nw)]
        for cp in locals_:
            cp.start()
        sends = []
        for a in range(nh):
            half = outs[a].shape[1] // 2
            rows = pl.ds(c * half, half)
            for j, (px, py) in enumerate(others):
                sends.append(_remote(outs[a].at[me, rows], outs[a].at[me, rows], send.at[6 * a + j], recv.at[6 * a + j],
                                     (px, py, c)))
        for a in range(nw):
            for j, (px, py) in enumerate(others):
                sends.append(_remote(srcs[nh + a], outs[nh + a].at[me], send.at[6 * nh + 3 * a + j],
                                     recv.at[6 * nh + 3 * a + j], (px, py, c)))
        for cp in sends:
            cp.start()
        for a in range(nh):
            half = outs[a].shape[1] // 2
            rows = pl.ds(c * half, half)
            for j, (px, py) in enumerate(others):
                got = outs[a].at[2 * px + py, rows]
                _remote(got, got, send.at[6 * a + j], recv.at[6 * a + j], (px, py, c)).wait_recv()
                fwd = _remote(got, got, send.at[6 * a + 3 + j], recv.at[6 * a + 3 + j], (x, y, 1 - c))
                fwd.start()
                sends.append(fwd)
        for a in range(nh):
            half = outs[a].shape[1] // 2
            theirs = pl.ds((1 - c) * half, half)
            for j, (px, py) in enumerate(others):
                got = outs[a].at[2 * px + py, theirs]
                _remote(got, got, send.at[6 * a + 3 + j], recv.at[6 * a + 3 + j], (x, y, 1 - c)).wait_recv()
        for a in range(nw):
            for j, (px, py) in enumerate(others):
                got = outs[nh + a].at[2 * px + py]
                _remote(got, got, send.at[6 * nh + 3 * a + j], recv.at[6 * nh + 3 * a + j], (px, py, c)).wait_recv()
        for cp in sends:
            cp.wait_send()
        for cp in locals_:
            cp.wait()

    out_shapes = [jax.ShapeDtypeStruct(t.shape, t.dtype) for t in halved]
    out_shapes += [jax.ShapeDtypeStruct((N_CHIPS,) + t.shape, t.dtype) for t in whole]
    return _hbm_call(body, name, arrays, out_shapes, 6 * nh + 3 * nw, nw, aliases={a: a for a in range(nh)})


def _cast_into(w, layer, place, *, name):
    _, r, cols = w.shape
    tr = _row_tile(r, cols)

    def body(place_ref, w_ref, out_ref):
        out_ref[...] = w_ref[...].astype(out_ref.dtype)

    grid_spec = pltpu.PrefetchScalarGridSpec(
        num_scalar_prefetch=1, grid=(r // tr,), in_specs=[pl.BlockSpec((None, tr, cols), lambda i, pr: (layer, i, 0))],
        out_specs=pl.BlockSpec((None, tr, cols), lambda i, pr: (pr[1], i, 0)),
    )
    return pl.pallas_call(
        body, name=name, grid_spec=grid_spec, out_shape=jax.ShapeDtypeStruct((N_CHIPS, r, cols), BF16),
        compiler_params=_params(("parallel",)),
    )(place, w)


def _pair_exchange(grads, *, name):
    n = len(grads)

    def body(*refs):
        srcs, outs = refs[:n], refs[n:2 * n]
        send, recv, _ = refs[2 * n:]
        x, y, c, _o = _place()
        cps = []
        for a in range(n):
            half = srcs[a].shape[1] // 2
            cps.append(_remote(srcs[a].at[:, pl.ds((1 - c) * half, half), :], outs[a], send.at[a], recv.at[a], (x, y, 1 - c)))
        for cp in cps:
            cp.start()
        for cp in cps:
            cp.wait()

    out_shapes = [jax.ShapeDtypeStruct((N_CHIPS, t.shape[1] // 2, t.shape[2]), t.dtype) for t in grads]
    return _hbm_call(body, name, list(grads), out_shapes, n, 0)


HBM = pl.BlockSpec(memory_space=pltpu.HBM)
SEM = pl.BlockSpec(memory_space=pltpu.SEMAPHORE)
EFFECT = pltpu.SideEffectType.DATAFLOW_SIDE_EFFECTING
TOKEN = jax.ShapeDtypeStruct((8, 128), F32)


def _in_hbm(t):
    return pltpu.with_memory_space_constraint(t, pltpu.HBM)


def _gather_copies(buf, send, recv):
    x, y, c, others = _place()
    half = buf.shape[1] // 2
    rows = pl.ds(c * half, half)
    return [_remote(buf.at[2 * x + y, rows], buf.at[2 * x + y, rows], send.at[j], recv.at[j], (px, py, c))
            for j, (px, py) in enumerate(others)]


def _gather_start(bufs, *, name):
    n = len(bufs)

    def body(*refs):
        ins, sends, recvs, token = refs[:n], refs[n:2 * n], refs[2 * n:3 * n], refs[4 * n]
        for a in range(n):
            for cp in _gather_copies(ins[a], sends[a], recvs[a]):
                cp.start()
        token[...] = jnp.zeros_like(token)

    sems = [pltpu.SemaphoreType.DMA((3,))] * (2 * n)
    res = pl.pallas_call(
        body, name=name, out_shape=sems + [pltpu.HBM(t.shape, t.dtype) for t in bufs] + [TOKEN],
        in_specs=[HBM] * n, out_specs=[SEM] * (2 * n) + [HBM] * n + [pl.BlockSpec(memory_space=pltpu.VMEM)],
        input_output_aliases={a: 2 * n + a for a in range(n)}, compiler_params=pltpu.CompilerParams(has_side_effects=EFFECT),
    )(*[_in_hbm(t) for t in bufs])
    return [(res[2 * n + a], res[a], res[n + a]) for a in range(n)], res[3 * n]


def _gather_wait(state, after, *, name):
    buf, send, recv = state

    def body(buf_ref, send_ref, recv_ref, after_ref, out_ref):
        for cp in _gather_copies(buf_ref, send_ref, recv_ref):
            cp.wait_send()
            cp.wait_recv()

    return pl.pallas_call(
        body, name=name, out_shape=pltpu.HBM(buf.shape, buf.dtype), in_specs=[HBM, SEM, SEM, ANY], out_specs=HBM,
        input_output_aliases={0: 0}, compiler_params=pltpu.CompilerParams(has_side_effects=EFFECT),
    )(buf, send, recv, after)


def _forward_sibling(buf, *, name):
    def body(_in, out, send, recv, _local):
        x, y, c, others = _place()
        half = out.shape[1] // 2
        cps = []
        for j, (px, py) in enumerate(others):
            got = out.at[2 * px + py, pl.ds(c * half, half)]
            cps.append(_remote(got, got, send.at[j], recv.at[j], (x, y, 1 - c)))
        for cp in cps:
            cp.start()
        for j, (px, py) in enumerate(others):
            cps[j].wait_send()
            theirs = out.at[2 * px + py, pl.ds((1 - c) * half, half)]
            _remote(theirs, theirs, send.at[j], recv.at[j], (x, y, 1 - c)).wait_recv()

    return _hbm_call(body, name, [buf], [jax.ShapeDtypeStruct(buf.shape, buf.dtype)], 3, 0, aliases={0: 0})[0]


def _chip_copies(src, land, send, recv):
    _x, _y, c, others = _place()
    return [_remote(src.at[2 * px + py], land.at[j], send.at[j], recv.at[j], (px, py, c)) for j, (px, py) in enumerate(others)]


def _chip_start(partial, *, name):
    def body(src, land, send, recv, _src_thru, _land_thru, token):
        for cp in _chip_copies(src, land, send, recv):
            cp.start()
        token[...] = jnp.zeros_like(token)

    land_shape = (3,) + partial.shape[1:]
    sem = pltpu.SemaphoreType.DMA((3,))
    send, recv, src, land, token = pl.pallas_call(
        body, name=name, out_shape=[sem, sem, pltpu.HBM(partial.shape, partial.dtype), pltpu.HBM(land_shape, partial.dtype), TOKEN],
        in_specs=[HBM, HBM], out_specs=[SEM, SEM, HBM, HBM, pl.BlockSpec(memory_space=pltpu.VMEM)],
        input_output_aliases={0: 2, 1: 3}, compiler_params=pltpu.CompilerParams(has_side_effects=EFFECT),
    )(_in_hbm(partial), _in_hbm(lax.empty(land_shape, partial.dtype)))
    return (src, land, send, recv), token


def _chip_wait(state, after, *, name):
    src, land, send, recv = state

    def body(src_ref, land_ref, send_ref, recv_ref, after_ref, _src_out, _land_out):
        for cp in _chip_copies(src_ref, land_ref, send_ref, recv_ref):
            cp.wait_send()
            cp.wait_recv()

    return pl.pallas_call(
        body, name=name, out_shape=[pltpu.HBM(src.shape, src.dtype), pltpu.HBM(land.shape, land.dtype)],
        in_specs=[HBM, HBM, SEM, SEM, ANY], out_specs=[HBM, HBM], input_output_aliases={0: 0, 1: 1},
        compiler_params=pltpu.CompilerParams(has_side_effects=EFFECT),
    )(src, land, send, recv, after)[1]


def _share_halves(bufs, *, name):
    n = len(bufs)

    def body(*refs):
        outs = refs[n:2 * n]
        send, recv, _ = refs[2 * n:]
        x, y, c, _o = _place()
        cps = [_remote(outs[a].at[:, c], outs[a].at[:, c], send.at[a], recv.at[a], (x, y, 1 - c)) for a in range(n)]
        for cp in cps:
            cp.start()
        for a in range(n):
            cps[a].wait_send()
            theirs = outs[a].at[:, 1 - c]
            _remote(theirs, theirs, send.at[a], recv.at[a], (x, y, 1 - c)).wait_recv()

    out_shapes = [jax.ShapeDtypeStruct(t.shape, t.dtype) for t in bufs]
    return _hbm_call(body, name, list(bufs), out_shapes, n, 0, aliases={a: a for a in range(n)})


def _all_gather_small(pack, *, name):
    def body(src, out, send, recv, local):
        x, y, c, others = _place()

        def slot(px, py, pc):
            return out.at[4 * px + 2 * py + pc]

        mine = pltpu.make_async_copy(src, slot(x, y, c), local.at[0])
        mine.start()
        first = [_remote(src, slot(x, y, c), send.at[0], recv.at[0], (x, y, 1 - c))]
        first += [_remote(src, slot(x, y, c), send.at[1 + j], recv.at[1 + j], (px, py, c)) for j, (px, py) in enumerate(others)]
        for cp in first:
            cp.start()
        passed = []
        for j, (px, py) in enumerate(others):
            got = slot(px, py, c)
            _remote(got, got, send.at[1 + j], recv.at[1 + j], (px, py, c)).wait_recv()
            fwd = _remote(got, got, send.at[4 + j], recv.at[4 + j], (x, y, 1 - c))
            fwd.start()
            passed.append(fwd)
        theirs = slot(x, y, 1 - c)
        _remote(theirs, theirs, send.at[0], recv.at[0], (x, y, 1 - c)).wait_recv()
        for j, (px, py) in enumerate(others):
            got = slot(px, py, 1 - c)
            _remote(got, got, send.at[4 + j], recv.at[4 + j], (x, y, 1 - c)).wait_recv()
        for cp in first + passed:
            cp.wait_send()
        mine.wait()

    return _hbm_call(body, name, [pack], [jax.ShapeDtypeStruct((N_DEV,) + pack.shape, pack.dtype)], 7, 1)[0]


def _row_tile(rows, cols):
    return next(t for t in (512, 256, 128, 64, 32, 16) if rows % t == 0 and (t * cols * 4 <= (1 << 20) or t == 16))


def _pair_sum(grad, theirs, place, *, name):
    _, r, cols = grad.shape
    r2 = r // 2
    tr = _row_tile(r2, cols)
    nr = r2 // tr

    def body(place_ref, g_ref, t_ref, all_ref):
        all_ref[...] = (g_ref[...] + t_ref[...]).astype(all_ref.dtype)

    grid_spec = pltpu.PrefetchScalarGridSpec(
        num_scalar_prefetch=1, grid=(N_CHIPS, nr),
        in_specs=[pl.BlockSpec((None, tr, cols), lambda k, i, pr: (k, pr[0] * nr + i, 0)),
                  pl.BlockSpec((None, tr, cols), lambda k, i, pr: (k, i, 0))],
        out_specs=pl.BlockSpec((None, tr, cols), lambda k, i, pr: (k, i, 0)),
    )
    return pl.pallas_call(
        body, name=name, grid_spec=grid_spec, out_shape=jax.ShapeDtypeStruct((N_CHIPS, r2, cols), BF16),
        compiler_params=_params(("parallel", "parallel")),
    )(place, grad, theirs)


def _chip_sum(grad, theirs, got, place, buf, layer, depth, *, name):
    _, r, cols = grad.shape
    r2 = r // 2
    tr = _row_tile(r2, cols)
    nr = r2 // tr

    def body(place_ref, g_ref, t_ref, got_ref, *rest):
        own = g_ref[...] + t_ref[...]
        rest[-1][...] = ((own + got_ref[0].astype(F32)) + got_ref[1].astype(F32)) + got_ref[2].astype(F32)

    in_specs = [pl.BlockSpec((None, tr, cols), lambda i, pr: (pr[1], pr[0] * nr + i, 0)),
                pl.BlockSpec((None, tr, cols), lambda i, pr: (pr[1], i, 0)),
                pl.BlockSpec((3, tr, cols), lambda i, pr: (0, i, 0))]
    args = [place, grad, theirs, got]
    if buf is not None:
        in_specs.append(ANY)
        args.append(buf)
    grid_spec = pltpu.PrefetchScalarGridSpec(
        num_scalar_prefetch=1, grid=(nr,), in_specs=in_specs,
        out_specs=pl.BlockSpec((None, None, tr, cols), lambda i, pr: (layer, pr[0], i, 0)),
    )
    return pl.pallas_call(
        body, name=name, grid_spec=grid_spec, out_shape=jax.ShapeDtypeStruct((depth, 2, r2, cols), F32),
        input_output_aliases={} if buf is None else {4: 0}, compiler_params=_params(("parallel",)),
    )(*args)


def _sum_devices(parts, *, name):
    _, rows, cols = parts.shape
    tr = _pick(rows, (256, 128, 64, 32, 16, 8))

    def body(p_ref, out_ref):
        acc = p_ref[0]
        for d in range(1, N_DEV):
            acc = acc + p_ref[d]
        out_ref[...] = acc

    return pl.pallas_call(
        body, name=name, grid=(rows // tr,), in_specs=[pl.BlockSpec((N_DEV, tr, cols), lambda i: (0, i, 0))],
        out_specs=pl.BlockSpec((tr, cols), lambda i: (i, 0)), out_shape=jax.ShapeDtypeStruct((rows, cols), F32),
        compiler_params=_params(("parallel",)),
    )(parts)


def _pack(parts):
    rows = []
    for t in parts:
        flat = t.reshape(-1, 128)
        pad = (-flat.shape[0]) % 8
        rows.append(jnp.pad(flat, ((0, pad), (0, 0))) if pad else flat)
    return jnp.concatenate(rows, axis=0)


def _unpack(pack, shapes):
    out, r0 = [], 0
    for shp in shapes:
        n = math.prod(shp) // 128
        out.append(pack[r0:r0 + n].reshape(shp))
        r0 += n + (-n) % 8
    return out


SMALL = ["attn_norm_g", "q_norm_g", "k_norm_g", "sgu_norm_g", "sgu_w", "sgu_b", "out_norm_a_g", "out_norm_b_g",
         "ffn_norm_g", "conv_b"]
BIG = ["w_in", "w_out", "w_up", "w_down"]
ORDER = ["attn_norm_g", "w_in", "q_norm_g", "k_norm_g", "sgu_norm_g", "sgu_w", "sgu_b", "out_norm_a_g", "out_norm_b_g",
         "w_out", "ffn_norm_g", "w_up", "conv_w", "conv_b", "w_down"]


def kernel(x, attn_norm_g, w_in, q_norm_g, k_norm_g, sgu_norm_g, sgu_w, sgu_b, out_norm_a_g, out_norm_b_g, w_out, ffn_norm_g, w_up, conv_w, conv_b, w_down, loss_target, m_attn_norm_g, m_w_in, m_q_norm_g, m_k_norm_g, m_sgu_norm_g, m_sgu_w, m_sgu_b, m_out_norm_a_g, m_out_norm_b_g, m_w_out, m_ffn_norm_g, m_w_up, m_conv_w, m_conv_b, m_w_down, v_attn_norm_g, v_w_in, v_q_norm_g, v_k_norm_g, v_sgu_norm_g, v_sgu_w, v_sgu_b, v_out_norm_a_g, v_out_norm_b_g, v_w_out, v_ffn_norm_g, v_w_up, v_conv_w, v_conv_b, v_w_down):
    W = dict(attn_norm_g=attn_norm_g, w_in=w_in, q_norm_g=q_norm_g, k_norm_g=k_norm_g, sgu_norm_g=sgu_norm_g, sgu_w=sgu_w,
             sgu_b=sgu_b, out_norm_a_g=out_norm_a_g, out_norm_b_g=out_norm_b_g, w_out=w_out, ffn_norm_g=ffn_norm_g, w_up=w_up,
             conv_w=conv_w, conv_b=conv_b, w_down=w_down)
    M = dict(attn_norm_g=m_attn_norm_g, w_in=m_w_in, q_norm_g=m_q_norm_g, k_norm_g=m_k_norm_g, sgu_norm_g=m_sgu_norm_g,
             sgu_w=m_sgu_w, sgu_b=m_sgu_b, out_norm_a_g=m_out_norm_a_g, out_norm_b_g=m_out_norm_b_g, w_out=m_w_out,
             ffn_norm_g=m_ffn_norm_g, w_up=m_w_up, conv_w=m_conv_w, conv_b=m_conv_b, w_down=m_w_down)
    V = dict(attn_norm_g=v_attn_norm_g, w_in=v_w_in, q_norm_g=v_q_norm_g, k_norm_g=v_k_norm_g, sgu_norm_g=v_sgu_norm_g,
             sgu_w=v_sgu_w, sgu_b=v_sgu_b, out_norm_a_g=v_out_norm_a_g, out_norm_b_g=v_out_norm_b_g, w_out=v_w_out,
             ffn_norm_g=v_ffn_norm_g, w_up=v_w_up, conv_w=v_conv_w, conv_b=v_conv_b, w_down=v_w_down)
    depth = w_in.shape[0]
    s, d = x.shape[1], x.shape[2]
    n_heads = out_norm_a_g.shape[1]
    core = lax.axis_index("c")
    chip = 2 * lax.axis_index("x") + lax.axis_index("y")
    place = jnp.stack([core, chip]).astype(jnp.int32)
    xs = x.reshape(s, d)

    f_local = conv_w.shape[2]
    taps = jnp.pad(conv_w.reshape(depth * 3, f_local), ((0, 8 - depth * 3), (0, 0)))
    taps = _all_gather_weights([], [taps], name="gather_taps")[0][:, :depth * 3].reshape(N_CHIPS, depth, 3, f_local)
    cw_full = jnp.transpose(taps, (1, 2, 0, 3)).reshape(depth, 3, N_CHIPS * f_local)

    states, token = _gather_start([_cast_into(W[n], l, place, name=f"cast_{n}") for l in range(depth) for n in BIG],
                                  name="gather_start")
    states = {(l, n): states[len(BIG) * l + i] for l in range(depth) for i, n in enumerate(BIG)}

    def gathered(l, n, after):
        return _forward_sibling(_gather_wait(states[(l, n)], after, name=f"gather_wait_{n}"), name=f"forward_{n}")

    saved, full = [], []
    cur = xs
    for l in range(depth):
        gain = attn_norm_g[l] + token[0, 0] if l == 0 else attn_norm_g[l]
        h = _rmsnorm_fwd(cur, gain, name="attn_norm")
        w_in_l = gathered(l, "w_in", h)
        p = _mm(h, w_in_l, "nn", b_split=N_CHIPS, caps=(2048, 256, 2048), name="proj_in")
        att, o_raw, lsum = _attn_fwd(p, q_norm_g[l], k_norm_g[l], out_norm_a_g[l], n_heads, name="attn_fwd")
        sg = _sgu_fwd(p, sgu_w[l], sgu_b[l], sgu_norm_g[l], out_norm_b_g[l], n_heads, name="sgu_fwd")
        mix = jnp.stack([att, sg])
        w_out_l = gathered(l, "w_out", mix).reshape(-1, d)
        x1 = _mm(mix, w_out_l, "nn", a_split=2, res=cur, caps=(2048, 512, 1024), name="proj_out")
        h2 = _rmsnorm_fwd(x1, ffn_norm_g[l], name="ffn_norm")
        w_up_l = gathered(l, "w_up", h2)
        up = _mm(h2, w_up_l, "nn", b_split=N_CHIPS, caps=(2048, 256, 2048), name="ffn_up")
        act = _conv_fwd(up, cw_full[l], conv_b[l], name="conv_fwd")
        w_down_l = gathered(l, "w_down", act).reshape(-1, d)
        x2 = _mm(act, w_down_l, "nn", res=x1, caps=(2048, 512, 512), name="ffn_down")
        full.append(dict(w_in=w_in_l, w_out=w_out_l, w_up=w_up_l, w_down=w_down_l, conv_w=cw_full[l]))
        saved.append(dict(x0=cur, h=h, p=p, o_raw=o_raw, lsum=lsum, mix=mix, x1=x1, h2=h2, up=up, act=act))
        cur = x2

    dx, dxb, sq = _loss_head(cur, loss_target.reshape(s, d), name="loss_head")
    loss = lax.psum(sq[0, 0] * (0.5 / d), ("x", "y", "c"))

    small_grads = {n: [None] * depth for n in SMALL + ["conv_w"]}
    def reduce_begin(n, grad):
        theirs = _pair_exchange([grad], name=f"pair_exchange_{n}")[0]
        state, tok = _chip_start(_pair_sum(grad, theirs, place, name=f"pair_sum_{n}"), name=f"chip_start_{n}")
        return (grad, theirs, state), tok

    pending = {}
    for l in reversed(range(depth)):
        fw, sv = full[l], saved[l]
        dact = _mm(dxb, fw["w_down"], "nt", caps=(2048, 512, 2048), name="d_act")
        g_down = _mm(sv["act"], dxb, "tn", caps=(512, 2048, 2048), name="g_down")
        pending[(l, "w_down")], tok = reduce_begin("w_down", g_down.reshape(N_CHIPS, -1, d))
        dup, dcw, dcb = _conv_bwd(sv["up"], dact, fw["conv_w"], conv_b[l] + tok[0, 0], name="conv_bwd")
        g_up = _mm(sv["h2"], dup, "tn", b_split=2, o_split=N_CHIPS, caps=(2048, 256, 2048), name="g_up")
        pending[(l, "w_up")], tok = reduce_begin("w_up", g_up)
        dh2 = _mm(dup, fw["w_up"], "nt", a_split=2, b_split=N_CHIPS, caps=(1024, 512, 2816), dep=tok, name="d_h2")
        dx1, dx1b, dg_ffn = _rmsnorm_bwd(sv["x1"], ffn_norm_g[l], dh2, dx, name="ffn_norm_bwd")
        dmix = _mm(dx1b, fw["w_out"], "nt", caps=(2048, 512, 2048), name="d_mix")
        g_out = _mm(sv["mix"], dx1b, "tn", a_split=2, caps=(512, 2048, 2048), name="g_out")
        pending[(l, "w_out")], tok = reduce_begin("w_out", g_out.reshape(N_CHIPS, -1, d))
        dqkv, dgq, dgk, dgoa = _attn_bwd(sv["p"], sv["o_raw"], sv["lsum"], dmix, q_norm_g[l] + tok[0, 0], k_norm_g[l],
                                         out_norm_a_g[l], n_heads, name="attn_bwd")
        duv, dsw, dsb, dgv, dgob = _sgu_bwd(sv["p"], dmix, sgu_w[l], sgu_b[l], sgu_norm_g[l], out_norm_b_g[l], n_heads,
                                            name="sgu_bwd")
        dp = jnp.concatenate([dqkv[0], dqkv[1], dqkv[2], duv[0], duv[1]], axis=1)
        g_in = _mm(sv["h"], dp, "tn", o_split=N_CHIPS, caps=(2048, 256, 2048), name="g_in")
        pending[(l, "w_in")], tok = reduce_begin("w_in", g_in)
        dh = _mm(dp, fw["w_in"], "nt", b_split=N_CHIPS, caps=(2048, 512, 1280), dep=tok, name="d_h")
        dx, dxb, dg_attn = _rmsnorm_bwd(sv["x0"], attn_norm_g[l], dh, dx1, name="attn_norm_bwd")

        small_grads["attn_norm_g"][l] = dg_attn.reshape(d)
        small_grads["q_norm_g"][l] = jnp.sum(dgq, axis=(0, 1))
        small_grads["k_norm_g"][l] = jnp.sum(dgk, axis=(0, 1))
        small_grads["sgu_norm_g"][l] = dgv.reshape(-1, BLK)
        small_grads["sgu_w"][l] = dsw
        small_grads["sgu_b"][l] = dsb.reshape(-1, BLK)
        small_grads["out_norm_a_g"][l] = dgoa.reshape(-1, BLK)
        small_grads["out_norm_b_g"][l] = dgob.reshape(-1, BLK)
        small_grads["ffn_norm_g"][l] = dg_ffn.reshape(d)
        small_grads["conv_b"][l] = dcb[:, 0, :].reshape(-1)
        small_grads["conv_w"][l] = jnp.transpose(dcw[:, :3, :], (1, 0, 2)).reshape(3, -1)

    G, D_, NM, NV = {}, {}, {}, {}
    after = dx
    for n in ("w_down", "w_up", "w_out", "w_in"):
        buf = None
        for l in reversed(range(depth)):
            grad, theirs, state = pending[(l, n)]
            got = _chip_wait(state, after, name=f"chip_wait_{n}")
            buf = _chip_sum(grad, theirs, got, place, buf, l, depth, name=f"chip_sum_{n}")
            after = buf
        G[n] = _share_halves([buf], name=f"share_halves_{n}")[0].reshape(W[n].shape)
        D_[n], NM[n], NV[n] = _adamw(W[n], G[n], M[n], V[n], name=f"adamw_{n}")
        after = NV[n]

    names = SMALL + ["conv_w"]
    pack = _pack([jnp.stack(small_grads[n]) for n in names])
    total = _sum_devices(_all_gather_small(pack, name="gather_small"), name="sum_small")
    f_full = conv_b.shape[1]
    shapes = [W[n].shape for n in SMALL] + [(depth, 3, f_full)]
    for n, t in zip(names, _unpack(total, shapes)):
        G[n] = t
    G["conv_w"] = lax.dynamic_slice_in_dim(G["conv_w"], chip * f_local, f_local, axis=2)

    D_["conv_w"], NM["conv_w"], NV["conv_w"] = _adamw(conv_w, G["conv_w"], m_conv_w, v_conv_w, name="adamw_conv_w")
    small_shapes = [W[n].shape for n in SMALL]
    res = _adamw(_pack([W[n] for n in SMALL]), _pack([G[n] for n in SMALL]), _pack([M[n] for n in SMALL]),
                 _pack([V[n] for n in SMALL]), name="adamw_small")
    for dst, t in zip((D_, NM, NV), res):
        for n, u in zip(SMALL, _unpack(t, small_shapes)):
            dst[n] = u

    return (loss, dx.reshape(x.shape), *[G[n] for n in ORDER], *[D_[n] for n in ORDER], *[NM[n] for n in ORDER],
            *[NV[n] for n in ORDER])
```

```python
import functools
import math

import jax
import jax.numpy as jnp
from jax import lax
from jax.experimental import pallas as pl
from jax.experimental.pallas import tpu as pltpu

F32 = jnp.float32
BF16 = jnp.bfloat16
EPS = 1e-6
BLK = 128
N_CHIPS = 4
N_DEV = 8
ADAM_LR, ADAM_B1, ADAM_B2, ADAM_EPS, ADAM_WD, ADAM_STEP = 0.001, 0.9, 0.999, 1e-08, 0.01, 10
VMEM_BIG = 48 * 1024 * 1024
MESH = pl.DeviceIdType.MESH
ANY = pl.BlockSpec(memory_space=pl.ANY)


def _pick(dim, prefs):
    for t in prefs:
        if dim % t == 0:
            return t
    raise ValueError(f"no tile in {prefs} divides {dim}")


def _params(sem=None, vmem=None):
    return pltpu.CompilerParams(dimension_semantics=sem, vmem_limit_bytes=vmem)


def _ldims(arr, split):
    if split == 1:
        return arr.shape
    p, r, cs = arr.shape
    assert p == split
    return (r, p * cs)


def _spec(tr, tc, split, cols, rc):
    if split == 1:
        return pl.BlockSpec((tr, tc), lambda i, j, k: rc(i, j, k))
    per = (cols // split) // tc

    def imap(i, j, k):
        r, c = rc(i, j, k)
        return (c // per, r, c % per)

    return pl.BlockSpec((None, tr, tc), imap)


def _fit(unit, cap):
    return max(t for t in range(128, min(unit, cap) + 1, 128) if unit % t == 0)


def _mm(a, b, mode, *, name, caps, a_split=1, b_split=1, o_split=1, out_dtype=F32, res=None, dep=None):
    ar, ac = _ldims(a, a_split)
    br, bc = _ldims(b, b_split)
    if mode == "nn":
        m, k, n = ar, ac, bc
        assert br == k
        ku, nu, mu = math.gcd(k // a_split, k), math.gcd(n // b_split, n // o_split), m
    elif mode == "nt":
        m, k, n = ar, ac, br
        assert bc == k
        ku, nu, mu = math.gcd(k // a_split, k // b_split), n // o_split, m
    else:
        k, m, n = ar, ac, bc
        assert br == k
        ku, nu, mu = k, math.gcd(n // b_split, n // o_split), m // a_split
    tm, tn, tk = _fit(mu, caps[0]), _fit(nu, caps[1]), _fit(ku, caps[2])
    nk = k // tk
    if mode == "nn":
        a_spec = _spec(tm, tk, a_split, k, lambda i, j, kk: (i, kk))
        b_spec = _spec(tk, tn, b_split, n, lambda i, j, kk: (kk, j))
    elif mode == "nt":
        a_spec = _spec(tm, tk, a_split, k, lambda i, j, kk: (i, kk))
        b_spec = _spec(tn, tk, b_split, k, lambda i, j, kk: (j, kk))
    else:
        a_spec = _spec(tk, tm, a_split, m, lambda i, j, kk: (kk, i))
        b_spec = _spec(tk, tn, b_split, n, lambda i, j, kk: (kk, j))
    o_spec = _spec(tm, tn, o_split, n, lambda i, j, kk: (i, j))
    dims = {"nn": (((1,), (0,)), ((), ())), "nt": (((1,), (1,)), ((), ())), "tn": (((0,), (0,)), ((), ()))}[mode]

    def body(a_ref, b_ref, *rest):
        if dep is not None:
            rest = rest[1:]
        if res is None:
            o_ref, acc = rest
        else:
            r_ref, o_ref, acc = rest
        kk = pl.program_id(2)

        @pl.when(kk == 0)
        def _():
            acc[...] = jnp.zeros_like(acc)

        acc[...] += lax.dot_general(a_ref[...].astype(BF16), b_ref[...].astype(BF16), dims, preferred_element_type=F32)

        @pl.when(kk == nk - 1)
        def _():
            out = acc[...]
            if res is not None:
                out = out + r_ref[...]
            o_ref[...] = out.astype(o_ref.dtype)

    in_specs, args = [a_spec, b_spec], [a, b]
    if dep is not None:
        in_specs.append(ANY)
        args.append(dep)
    if res is not None:
        in_specs.append(pl.BlockSpec((tm, tn), lambda i, j, kk: (i, j)))
        args.append(res)
    out_shape = (m, n) if o_split == 1 else (o_split, m, n // o_split)
    return pl.pallas_call(
        body, name=name, grid=(m // tm, n // tn, nk), in_specs=in_specs, out_specs=o_spec,
        out_shape=jax.ShapeDtypeStruct(out_shape, out_dtype), scratch_shapes=[pltpu.VMEM((tm, tn), F32)],
        compiler_params=_params(("parallel", "parallel", "arbitrary"), VMEM_BIG),
    )(*args)


def _rstd(v):
    return lax.rsqrt(jnp.mean(v * v, axis=-1, keepdims=True) + EPS)


def _norm_bwd(v, r, gain, dout):
    a = dout * gain
    dv = r * (a - v * (r * r * jnp.mean(a * v, axis=-1, keepdims=True)))
    return dv, dout * v * r


def _rmsnorm_fwd(x, g, *, name):
    s, d = x.shape
    tr = _pick(s, (256, 128))

    def body(x_ref, g_ref, o_ref):
        v = x_ref[...]
        o_ref[...] = (v * _rstd(v) * g_ref[...]).astype(o_ref.dtype)

    return pl.pallas_call(
        body, name=name, grid=(s // tr,),
        in_specs=[pl.BlockSpec((tr, d), lambda i: (i, 0)), pl.BlockSpec((1, d), lambda i: (0, 0))],
        out_specs=pl.BlockSpec((tr, d), lambda i: (i, 0)), out_shape=jax.ShapeDtypeStruct((s, d), BF16),
        compiler_params=_params(("parallel",)),
    )(x, g.reshape(1, d))


def _rmsnorm_bwd(x, g, dh, dres, *, name):
    s, d = x.shape
    tr = _pick(s, (256, 128))

    def body(x_ref, g_ref, dh_ref, dres_ref, dx_ref, dxb_ref, dg_ref):
        v = x_ref[...]
        dv, dgr = _norm_bwd(v, _rstd(v), g_ref[...], dh_ref[...])
        dx = dres_ref[...] + dv
        dx_ref[...] = dx
        dxb_ref[...] = dx.astype(BF16)
        part = jnp.sum(dgr, axis=0, keepdims=True)

        @pl.when(pl.program_id(0) == 0)
        def _():
            dg_ref[...] = part

        @pl.when(pl.program_id(0) > 0)
        def _():
            dg_ref[...] += part

    row = pl.BlockSpec((tr, d), lambda i: (i, 0))
    one = pl.BlockSpec((1, d), lambda i: (0, 0))
    return pl.pallas_call(
        body, name=name, grid=(s // tr,), in_specs=[row, one, row, row], out_specs=[row, row, one],
        out_shape=[jax.ShapeDtypeStruct((s, d), F32), jax.ShapeDtypeStruct((s, d), BF16), jax.ShapeDtypeStruct((1, d), F32)],
        compiler_params=_params(("arbitrary",)),
    )(x, g.reshape(1, d), dh, dres)


def _loss_head(y, target, *, name):
    s, d = y.shape
    tr = _pick(s, (256, 128))

    def body(y_ref, t_ref, dy_ref, dyb_ref, ls_ref):
        e = y_ref[...] - t_ref[...]
        dy = e * (1.0 / d)
        dy_ref[...] = dy
        dyb_ref[...] = dy.astype(BF16)
        part = jnp.full(ls_ref.shape, jnp.sum(e * e), F32)

        @pl.when(pl.program_id(0) == 0)
        def _():
            ls_ref[...] = part

        @pl.when(pl.program_id(0) > 0)
        def _():
            ls_ref[...] += part

    row = pl.BlockSpec((tr, d), lambda i: (i, 0))
    return pl.pallas_call(
        body, name=name, grid=(s // tr,), in_specs=[row, row], out_specs=[row, row, pl.BlockSpec((8, 128), lambda i: (0, 0))],
        out_shape=[jax.ShapeDtypeStruct((s, d), F32), jax.ShapeDtypeStruct((s, d), BF16), jax.ShapeDtypeStruct((8, 128), F32)],
        compiler_params=_params(("arbitrary",)),
    )(y, target)


def _iota2(axis):
    return lax.broadcasted_iota(jnp.int32, (BLK, BLK), axis)


def _tri_sum(v, tri):
    hi = v.astype(BF16)
    lo = (v - hi.astype(F32)).astype(BF16)
    return jnp.dot(hi, tri, preferred_element_type=F32) + jnp.dot(lo, tri, preferred_element_type=F32)


def _dot_nt(a, b):
    return lax.dot_general(a, b, (((1,), (1,)), ((), ())), preferred_element_type=F32)


def _dot_tn(a, b):
    return lax.dot_general(a, b, (((0,), (0,)), ((), ())), preferred_element_type=F32)


TQ_MAX = 512


def _sb_unit(qi, kj, scale, rhs_gt, lower):
    z = _dot_nt(qi, kj) * scale
    lb = jnp.minimum(z, 0.0) - jnp.log(1.0 + jnp.exp(-jnp.abs(z)))
    l1m = lb - z
    if lower is not None:
        l1m = jnp.where(lower, l1m, 0.0)
    return lb, _tri_sum(l1m, rhs_gt)


def _attn_fwd(p, gq, gk, go, n_heads, *, name):
    s = p.shape[0]
    tq = min(TQ_MAX, s)
    per = tq // BLK
    scale = BLK ** -0.5

    def body(q_ref, k_ref, v_ref, gq_ref, gk_ref, go_ref, att_ref, o_ref, l_ref, qn, kn, vb, acc, later):
        q = q_ref[...]
        k = k_ref[...]
        qn[...] = (q * _rstd(q) * gq_ref[...]).astype(BF16)
        kn[...] = (k * _rstd(k) * gk_ref[...]).astype(BF16)
        vb[...] = v_ref[...].astype(BF16)
        lower = _iota2(0) > _iota2(1)
        rhs_gt = jnp.concatenate([lower.astype(BF16), jnp.ones((BLK, BLK), BF16)], axis=1)

        def unit(q0, r, j, diag):
            rows = pl.ds(pl.multiple_of(q0 + r * BLK, BLK), BLK)
            cols = pl.ds(pl.multiple_of(j * BLK, BLK), BLK)
            mine = pl.ds(r * BLK, BLK)
            lb, both = _sb_unit(qn[rows, :], kn[cols, :], scale, rhs_gt, lower if diag else None)
            after = later[mine, :]
            a = jnp.exp(lb + both[:, :BLK] + after)
            if diag:
                a = jnp.where(lower, a, 0.0)
            acc[mine, :] += jnp.dot(a.astype(BF16), vb[cols, :], preferred_element_type=F32)
            later[mine, :] = after + both[:, BLK:]

        def q_block(i, _):
            q0 = i * tq
            acc[...] = jnp.zeros_like(acc)
            later[...] = jnp.zeros_like(later)
            for jd in reversed(range(per)):
                for r in range(jd, per):
                    unit(q0, r, i * per + jd, r == jd)

            def k_block(jj, _):
                for r in range(per):
                    unit(q0, r, i * per - 1 - jj, False)
                return 0

            lax.fori_loop(0, i * per, k_block, 0)
            tile = pl.ds(pl.multiple_of(q0, tq), tq)
            o = acc[...]
            o_ref[tile, :] = o
            l_ref[tile, :] = later[...]
            att_ref[tile, :] = (o * _rstd(o) * go_ref[...]).astype(att_ref.dtype)
            return 0

        lax.fori_loop(0, s // tq, q_block, 0)

    def col(off):
        return pl.BlockSpec((s, BLK), lambda h: (0, off + h))

    gain = pl.BlockSpec((1, BLK), lambda h: (0, 0))
    per_head = pl.BlockSpec((None, 1, BLK), lambda h: (h, 0, 0))
    return pl.pallas_call(
        body, name=name, grid=(n_heads,),
        in_specs=[col(0), col(n_heads), col(2 * n_heads), gain, gain, per_head],
        out_specs=[col(0), col(0), pl.BlockSpec((None, s, BLK), lambda h: (h, 0, 0))],
        out_shape=[jax.ShapeDtypeStruct((s, n_heads * BLK), BF16), jax.ShapeDtypeStruct((s, n_heads * BLK), F32),
                   jax.ShapeDtypeStruct((n_heads, s, BLK), F32)],
        scratch_shapes=[pltpu.VMEM((s, BLK), BF16)] * 3 + [pltpu.VMEM((tq, BLK), F32)] * 2,
        compiler_params=_params(("parallel",), VMEM_BIG),
    )(p, p, p, gq.reshape(1, BLK), gk.reshape(1, BLK), go.reshape(n_heads, 1, BLK))


def _attn_bwd(p, o_raw, lsum, dmix, gq, gk, go, n_heads, *, name):
    s = p.shape[0]
    tq = min(TQ_MAX, s)
    per = tq // BLK
    scale = BLK ** -0.5

    def body(q_ref, k_ref, v_ref, o_ref, l_ref, da_ref, gq_ref, gk_ref, go_ref,
             dqkv_ref, dgq_ref, dgk_ref, dgo_ref, qn, kn, vb, dob, dqn, dkn, dvv, seen, gsum):
        q = q_ref[...]
        k = k_ref[...]
        rq = _rstd(q)
        rk = _rstd(k)
        qn[...] = (q * rq * gq_ref[...]).astype(BF16)
        kn[...] = (k * rk * gk_ref[...]).astype(BF16)
        vb[...] = v_ref[...].astype(BF16)
        o = o_ref[...]
        do, dgo_rows = _norm_bwd(o, _rstd(o), go_ref[...], da_ref[...])
        dob[...] = do.astype(BF16)
        dgo_ref[...] = jnp.sum(dgo_rows, axis=0, keepdims=True)
        dqn[...] = jnp.zeros_like(dqn)
        dkn[...] = jnp.zeros_like(dkn)
        dvv[...] = jnp.zeros_like(dvv)
        lower = _iota2(0) > _iota2(1)
        ones = jnp.ones((BLK, BLK), BF16)
        rhs_gt = jnp.concatenate([lower.astype(BF16), ones], axis=1)
        rhs_lt = jnp.concatenate([(_iota2(0) < _iota2(1)).astype(BF16), ones], axis=1)

        def unit(q0, r, j, diag):
            rows = pl.ds(pl.multiple_of(q0 + r * BLK, BLK), BLK)
            cols = pl.ds(pl.multiple_of(j * BLK, BLK), BLK)
            mine = pl.ds(r * BLK, BLK)
            kj = kn[cols, :]
            lb, both = _sb_unit(qn[rows, :], kj, scale, rhs_gt, lower if diag else None)
            upto = seen[mine, :] + both[:, BLK:]
            a = jnp.exp(lb + both[:, :BLK] + (l_ref[rows, :] - upto))
            if diag:
                a = jnp.where(lower, a, 0.0)
            g = _dot_nt(dob[rows, :], vb[cols, :]) * a
            bothg = _tri_sum(g, rhs_lt)
            earlier = gsum[mine, :]
            beta = jnp.exp(lb)
            dz = g * (1.0 - beta) - beta * (bothg[:, :BLK] + earlier)
            if diag:
                dz = jnp.where(lower, dz, 0.0)
            dzs = (dz * scale).astype(BF16)
            dqn[rows, :] += jnp.dot(dzs, kj, preferred_element_type=F32)
            seen[mine, :] = upto
            gsum[mine, :] = earlier + bothg[:, BLK:]
            return dzs, a.astype(BF16)

        def key_sums(q0, r_first, j, parts):
            rows = pl.ds(pl.multiple_of(q0 + r_first * BLK, BLK), (per - r_first) * BLK)
            cols = pl.ds(pl.multiple_of(j * BLK, BLK), BLK)
            dzs = jnp.concatenate([u[0] for u in parts], axis=0)
            ab = jnp.concatenate([u[1] for u in parts], axis=0)
            dkn[cols, :] += _dot_tn(dzs, qn[rows, :])
            dvv[cols, :] += _dot_tn(ab, dob[rows, :])

        def q_block(i, _):
            q0 = i * tq
            seen[...] = jnp.zeros_like(seen)
            gsum[...] = jnp.zeros_like(gsum)

            def k_block(j, _):
                key_sums(q0, 0, j, [unit(q0, r, j, False) for r in range(per)])
                return 0

            lax.fori_loop(0, i * per, k_block, 0)
            for jd in range(per):
                j = i * per + jd
                key_sums(q0, jd, j, [unit(q0, r, j, r == jd) for r in range(jd, per)])
            return 0

        lax.fori_loop(0, s // tq, q_block, 0)
        dq_raw, dgq_rows = _norm_bwd(q, rq, gq_ref[...], dqn[...])
        dk_raw, dgk_rows = _norm_bwd(k, rk, gk_ref[...], dkn[...])
        dqkv_ref[0] = dq_raw.astype(BF16)
        dqkv_ref[1] = dk_raw.astype(BF16)
        dqkv_ref[2] = dvv[...].astype(BF16)
        dgq_ref[...] = jnp.sum(dgq_rows, axis=0, keepdims=True)
        dgk_ref[...] = jnp.sum(dgk_rows, axis=0, keepdims=True)

    def col(off):
        return pl.BlockSpec((s, BLK), lambda h: (0, off + h))

    gain = pl.BlockSpec((1, BLK), lambda h: (0, 0))
    per_head = pl.BlockSpec((None, 1, BLK), lambda h: (h, 0, 0))
    head_gain = jax.ShapeDtypeStruct((n_heads, 1, BLK), F32)
    return pl.pallas_call(
        body, name=name, grid=(n_heads,),
        in_specs=[col(0), col(n_heads), col(2 * n_heads), col(0), pl.BlockSpec((None, s, BLK), lambda h: (h, 0, 0)), col(0),
                  gain, gain, per_head],
        out_specs=[pl.BlockSpec((3, s, BLK), lambda h: (0, 0, h)), per_head, per_head, per_head],
        out_shape=[jax.ShapeDtypeStruct((3, s, n_heads * BLK), BF16), head_gain, head_gain, head_gain],
        scratch_shapes=[pltpu.VMEM((s, BLK), BF16)] * 4 + [pltpu.VMEM((s, BLK), F32)] * 3 + [pltpu.VMEM((tq, BLK), F32)] * 2,
        compiler_params=_params(("parallel",), VMEM_BIG),
    )(p, p, p, o_raw, lsum, dmix, gq.reshape(1, BLK), gk.reshape(1, BLK), go.reshape(n_heads, 1, BLK))


_INV_SQRT2 = 0.7071067811865476
_INV_SQRT2PI = 0.3989422804014327


def _gelu(x):
    return 0.5 * x * (1.0 + lax.erf(x * _INV_SQRT2))


def _gelu_grad(x):
    return 0.5 * (1.0 + lax.erf(x * _INV_SQRT2)) + x * jnp.exp(-0.5 * x * x) * _INV_SQRT2PI


def _sgu_fwd(p, w, b, gv, gout, n_heads, *, name):
    s = p.shape[0]
    n_groups = w.shape[0]
    nb = s // BLK

    def body(u_ref, v_ref, w_ref, b_ref, gv_ref, go_ref, out_ref):
        wt = jnp.where(_iota2(0) >= _iota2(1), w_ref[...], 0.0).astype(BF16)
        bias = b_ref[...]

        def chunk(c, _):
            rows = pl.ds(pl.multiple_of(c * BLK, BLK), BLK)
            u = _gelu(u_ref[rows, :])
            vv = _gelu(v_ref[rows, :])
            vs = vv * _rstd(vv) * gv_ref[...]
            gated = u * (jnp.dot(wt, vs.astype(BF16), preferred_element_type=F32) + bias)
            out_ref[rows, :] = (gated * _rstd(gated) * go_ref[...]).astype(out_ref.dtype)
            return 0

        lax.fori_loop(0, nb, chunk, 0)

    def col(off):
        return pl.BlockSpec((s, BLK), lambda g: (0, off + g))

    per_group = pl.BlockSpec((None, 1, BLK), lambda g: (g, 0, 0))
    return pl.pallas_call(
        body, name=name, grid=(n_groups,),
        in_specs=[col(3 * n_heads), col(3 * n_heads + n_groups), pl.BlockSpec((None, BLK, BLK), lambda g: (g, 0, 0)),
                  pl.BlockSpec((None, BLK, 1), lambda g: (g, 0, 0)), per_group, per_group],
        out_specs=col(0), out_shape=jax.ShapeDtypeStruct((s, n_groups * BLK), BF16),
        compiler_params=_params(("parallel",), VMEM_BIG),
    )(p, p, w, b.reshape(n_groups, BLK, 1), gv.reshape(n_groups, 1, BLK), gout.reshape(n_groups, 1, BLK))


def _sgu_bwd(p, dmix, w, b, gv, gout, n_heads, *, name):
    s = p.shape[0]
    n_groups = w.shape[0]
    nb = s // BLK

    def body(u_ref, v_ref, ds_ref, w_ref, b_ref, gv_ref, go_ref, duv_ref, dw_ref, db_ref, dgv_ref, dgo_ref):
        lower = _iota2(0) >= _iota2(1)
        wt = jnp.where(lower, w_ref[...], 0.0).astype(BF16)
        bias = b_ref[...]

        def chunk(c, carry):
            dw, db, dgv, dgo = carry
            rows = pl.ds(pl.multiple_of(c * BLK, BLK), BLK)
            up = u_ref[rows, :]
            vp = v_ref[rows, :]
            u = _gelu(up)
            vv = _gelu(vp)
            rv = _rstd(vv)
            vsb = (vv * rv * gv_ref[...]).astype(BF16)
            mixed = jnp.dot(wt, vsb, preferred_element_type=F32) + bias
            gated = u * mixed
            dgated, dgo_rows = _norm_bwd(gated, _rstd(gated), go_ref[...], ds_ref[rows, :])
            dmixed = dgated * u
            dmb = dmixed.astype(BF16)
            dvs = _dot_tn(wt, dmb)
            dvv, dgv_rows = _norm_bwd(vv, rv, gv_ref[...], dvs)
            duv_ref[0, rows, :] = (dgated * mixed * _gelu_grad(up)).astype(BF16)
            duv_ref[1, rows, :] = (dvv * _gelu_grad(vp)).astype(BF16)
            return (dw + _dot_nt(dmb, vsb), db + jnp.sum(dmixed, axis=1, keepdims=True),
                    dgv + jnp.sum(dgv_rows, axis=0, keepdims=True), dgo + jnp.sum(dgo_rows, axis=0, keepdims=True))

        row0 = jnp.zeros((1, BLK), F32)
        dw, db, dgv, dgo = lax.fori_loop(0, nb, chunk, (jnp.zeros((BLK, BLK), F32), jnp.zeros((BLK, 1), F32), row0, row0))
        dw_ref[...] = jnp.where(lower, dw, 0.0)
        db_ref[...] = db
        dgv_ref[...] = dgv
        dgo_ref[...] = dgo

    def col(off):
        return pl.BlockSpec((s, BLK), lambda g: (0, off + g))

    per_group = pl.BlockSpec((None, 1, BLK), lambda g: (g, 0, 0))
    square = pl.BlockSpec((None, BLK, BLK), lambda g: (g, 0, 0))
    column = pl.BlockSpec((None, BLK, 1), lambda g: (g, 0, 0))
    gain = jax.ShapeDtypeStruct((n_groups, 1, BLK), F32)
    return pl.pallas_call(
        body, name=name, grid=(n_groups,),
        in_specs=[col(3 * n_heads), col(3 * n_heads + n_groups), col(n_heads), square, column, per_group, per_group],
        out_specs=[pl.BlockSpec((2, s, BLK), lambda g: (0, 0, g)), square, column, per_group, per_group],
        out_shape=[jax.ShapeDtypeStruct((2, s, n_groups * BLK), BF16), jax.ShapeDtypeStruct((n_groups, BLK, BLK), F32),
                   jax.ShapeDtypeStruct((n_groups, BLK, 1), F32), gain, gain],
        compiler_params=_params(("parallel",), VMEM_BIG),
    )(p, p, dmix, w, b.reshape(n_groups, BLK, 1), gv.reshape(n_groups, 1, BLK), gout.reshape(n_groups, 1, BLK))


CONV_ROWS = 256
HALO = 8


def _shift_down(ref, r0, n, first):
    cur = ref[pl.ds(r0, n), :]
    prev = jnp.zeros((HALO, cur.shape[1]), F32) if first else ref[pl.ds(r0 - HALO, HALO), :]
    ext = jnp.concatenate([prev, cur], axis=0)
    return pltpu.roll(ext, 1, 0)[HALO:], pltpu.roll(ext, 2, 0)[HALO:], cur


def _shift_up(ref, r0, n, last):
    cur = ref[pl.ds(r0, n), :]
    nxt = jnp.zeros((HALO, cur.shape[1]), F32) if last else ref[pl.ds(r0 + n, HALO), :]
    ext = jnp.concatenate([cur, nxt], axis=0)
    return cur, pltpu.roll(ext, n + HALO - 1, 0)[:n], pltpu.roll(ext, n + HALO - 2, 0)[:n]


def _conv_rows(x1, x2, x0, w_ref, b_ref):
    return ((b_ref[...] + x2 * w_ref[0:1, :]) + x1 * w_ref[1:2, :]) + x0 * w_ref[2:3, :]


def _conv_specs(s, f, tc):
    nf = f // tc
    gate = pl.BlockSpec((s, tc), lambda n: (0, n))
    val = pl.BlockSpec((s, tc), lambda n: (0, nf + n))
    wg = pl.BlockSpec((3, tc), lambda n: (0, n))
    wv = pl.BlockSpec((3, tc), lambda n: (0, nf + n))
    bg = pl.BlockSpec((1, tc), lambda n: (0, n))
    bv = pl.BlockSpec((1, tc), lambda n: (0, nf + n))
    return nf, gate, val, wg, wv, bg, bv


def _conv_fwd(up, cw, cb, *, name):
    s, f2 = up.shape
    f = f2 // 2
    tc = _pick(f, (256, 128))
    cr = min(CONV_ROWS, s)
    nf, gate, val, wg, wv, bg, bv = _conv_specs(s, f, tc)

    def body(g_ref, v_ref, wg_ref, wv_ref, bg_ref, bv_ref, out_ref):
        for r0 in range(0, s, cr):
            gc = _conv_rows(*_shift_down(g_ref, r0, cr, r0 == 0), wg_ref, bg_ref)
            vc = _conv_rows(*_shift_down(v_ref, r0, cr, r0 == 0), wv_ref, bv_ref)
            out_ref[pl.ds(r0, cr), :] = (gc * jax.nn.sigmoid(gc) * vc).astype(out_ref.dtype)

    return pl.pallas_call(
        body, name=name, grid=(nf,), in_specs=[gate, val, wg, wv, bg, bv], out_specs=gate,
        out_shape=jax.ShapeDtypeStruct((s, f), BF16), compiler_params=_params(("parallel",), VMEM_BIG),
    )(up, up, cw, cw, cb.reshape(1, f2), cb.reshape(1, f2))


def _conv_bwd(up, dact, cw, cb, *, name):
    s, f2 = up.shape
    f = f2 // 2
    tc = _pick(f, (256, 128))
    cr = min(CONV_ROWS, s)
    nf, gate, val, wg, wv, bg, bv = _conv_specs(s, f, tc)

    def body(g_ref, v_ref, da_ref, wg_ref, wv_ref, bg_ref, bv_ref, dup_ref, dw_ref, db_ref, dgc, dvc):
        zero = jnp.zeros((1, tc), F32)
        sums = [[zero] * 4, [zero] * 4]
        for r0 in range(0, s, cr):
            rows = pl.ds(r0, cr)
            gx = _shift_down(g_ref, r0, cr, r0 == 0)
            vx = _shift_down(v_ref, r0, cr, r0 == 0)
            gc = _conv_rows(*gx, wg_ref, bg_ref)
            vc = _conv_rows(*vx, wv_ref, bv_ref)
            sig = jax.nn.sigmoid(gc)
            da = da_ref[rows, :]
            d_gate = da * vc * (sig * (1.0 + gc * (1.0 - sig)))
            d_val = da * (gc * sig)
            dgc[rows, :] = d_gate
            dvc[rows, :] = d_val
            for part, (dc, (x1, x2, x0)) in enumerate(((d_gate, gx), (d_val, vx))):
                for tap, xs in enumerate((x2, x1, x0)):
                    sums[part][tap] = sums[part][tap] + jnp.sum(dc * xs, axis=0, keepdims=True)
                sums[part][3] = sums[part][3] + jnp.sum(dc, axis=0, keepdims=True)
        dw_ref[...] = jnp.zeros_like(dw_ref)
        db_ref[...] = jnp.zeros_like(db_ref)
        for part, (dc_ref, w_ref) in enumerate(((dgc, wg_ref), (dvc, wv_ref))):
            for tap in range(3):
                dw_ref[part, tap:tap + 1, :] = sums[part][tap]
            db_ref[part, 0:1, :] = sums[part][3]
            for r0 in range(0, s, cr):
                d0, d1, d2 = _shift_up(dc_ref, r0, cr, r0 + cr == s)
                dup_ref[part, pl.ds(r0, cr), :] = ((d0 * w_ref[2:3, :] + d1 * w_ref[1:2, :]) + d2 * w_ref[0:1, :]).astype(BF16)

    small = pl.BlockSpec((2, 8, tc), lambda n: (0, 0, n))
    return pl.pallas_call(
        body, name=name, grid=(nf,), in_specs=[gate, val, gate, wg, wv, bg, bv],
        out_specs=[pl.BlockSpec((2, s, tc), lambda n: (0, 0, n)), small, small],
        out_shape=[jax.ShapeDtypeStruct((2, s, f), BF16), jax.ShapeDtypeStruct((2, 8, f), F32),
                   jax.ShapeDtypeStruct((2, 8, f), F32)],
        scratch_shapes=[pltpu.VMEM((s, tc), F32)] * 2, compiler_params=_params(("parallel",), VMEM_BIG),
    )(up, up, dact, cw, cw, cb.reshape(1, f2), cb.reshape(1, f2))


def _adamw(w, g, m, v, *, name):
    shape = w.shape
    cols = shape[-1]
    rows = w.size // cols
    if rows * cols * 4 <= (1 << 20):
        tr = rows
    else:
        tr = next(t for t in (1024, 512, 256, 128, 64, 32, 16, 8) if rows % t == 0 and (t * cols * 4 <= (1 << 20) or t == 8))

    def body(w_ref, g_ref, m_ref, v_ref, d_ref, nm_ref, nv_ref):
        gr = g_ref[...]
        nm = ADAM_B1 * m_ref[...] + (1.0 - ADAM_B1) * gr
        nv = ADAM_B2 * v_ref[...] + (1.0 - ADAM_B2) * (gr * gr)
        m_hat = nm / (1.0 - ADAM_B1 ** ADAM_STEP)
        v_hat = nv / (1.0 - ADAM_B2 ** ADAM_STEP)
        d_ref[...] = -ADAM_LR * (m_hat / (jnp.sqrt(v_hat) + ADAM_EPS) + ADAM_WD * w_ref[...])
        nm_ref[...] = nm
        nv_ref[...] = nv

    blk = pl.BlockSpec((tr, cols), lambda i: (i, 0))
    out = jax.ShapeDtypeStruct((rows, cols), F32)
    res = pl.pallas_call(
        body, name=name, grid=(rows // tr,), in_specs=[blk] * 4, out_specs=[blk] * 3, out_shape=[out] * 3,
        compiler_params=_params(("parallel",)),
    )(*[t.reshape(rows, cols) for t in (w, g, m, v)])
    return [t.reshape(shape) for t in res]


def _place():
    x, y, c = lax.axis_index("x"), lax.axis_index("y"), lax.axis_index("c")
    others = [(1 - x, y), (x, 1 - y), (1 - x, 1 - y)]
    return x, y, c, others


def _remote(src, dst, send_sem, recv_sem, device):
    return pltpu.make_async_remote_copy(src_ref=src, dst_ref=dst, send_sem=send_sem, recv_sem=recv_sem, device_id=device,
                                        device_id_type=MESH)


def _hbm_call(body, name, args, out_shapes, n_sems, n_local, aliases=None):
    return pl.pallas_call(
        body, name=name, in_specs=[ANY] * len(args), out_specs=[ANY] * len(out_shapes), out_shape=out_shapes,
        scratch_shapes=[pltpu.SemaphoreType.DMA((n_sems,)), pltpu.SemaphoreType.DMA((n_sems,)),
                        pltpu.SemaphoreType.DMA((max(n_local, 1),))],
        input_output_aliases=aliases or {}, compiler_params=pltpu.CompilerParams(has_side_effects=True),
    )(*args)


def _all_gather_weights(halved, whole, *, name):
    nh, nw = len(halved), len(whole)
    arrays = list(halved) + list(whole)

    def body(*refs):
        srcs, outs = refs[:nh + nw], refs[nh + nw:2 * (nh + nw)]
        send, recv, local = refs[2 * (nh + nw):]
        x, y, c, others = _place()
        me = 2 * x + y
        locals_ = [pltpu.make_async_copy(srcs[nh + a], outs[nh + a].at[me], local.at[a]) for a in range(nw)]
        for cp in locals_:
            cp.start()
        sends = []
        for a in range(nh):
            half = outs[a].shape[1] // 2
            rows = pl.ds(c * half, half)
            for j, (px, py) in enumerate(others):
                sends.append(_remote(outs[a].at[me, rows], outs[a].at[me, rows], send.at[6 * a + j], recv.at[6 * a + j],
                                     (px, py, c)))
        for a in range(nw):
            for j, (px, py) in enumerate(others):
                sends.append(_remote(srcs[nh + a], outs[nh + a].at[me], send.at[6 * nh + 3 * a + j],
                                     recv.at[6 * nh + 3 * a + j], (px, py, c)))
        for cp in sends:
            cp.start()
        for a in range(nh):
            half = outs[a].shape[1] // 2
            rows = pl.ds(c * half, half)
            for j, (px, py) in enumerate(others):
                got = outs[a].at[2 * px + py, rows]
                _remote(got, got, send.at[6 * a + j], recv.at[6 * a + j], (px, py, c)).wait_recv()
                fwd = _remote(got, got, send.at[6 * a + 3 + j], recv.at[6 * a + 3 + j], (x, y, 1 - c))
                fwd.start()
                sends.append(fwd)
        for a in range(nh):
            half = outs[a].shape[1] // 2
            theirs = pl.ds((1 - c) * half, half)
            for j, (px, py) in enumerate(others):
                got = outs[a].at[2 * px + py, theirs]
                _remote(got, got, send.at[6 * a + 3 + j], recv.at[6 * a + 3 + j], (x, y, 1 - c)).wait_recv()
        for a in range(nw):
            for j, (px, py) in enumerate(others):
                got = outs[nh + a].at[2 * px + py]
                _remote(got, got, send.at[6 * nh + 3 * a + j], recv.at[6 * nh + 3 * a + j], (px, py, c)).wait_recv()
        for cp in sends:
            cp.wait_send()
        for cp in locals_:
            cp.wait()

    out_shapes = [jax.ShapeDtypeStruct(t.shape, t.dtype) for t in halved]
    out_shapes += [jax.ShapeDtypeStruct((N_CHIPS,) + t.shape, t.dtype) for t in whole]
    return _hbm_call(body, name, arrays, out_shapes, 6 * nh + 3 * nw, nw, aliases={a: a for a in range(nh)})


def _cast_into(w, layer, place, *, name, dep=None):
    _, r, cols = w.shape
    tr = _row_tile(r, cols)

    def body(place_ref, w_ref, *rest):
        rest[-1][...] = w_ref[...].astype(BF16)

    in_specs, args = [pl.BlockSpec((None, tr, cols), lambda i, pr: (layer, i, 0))], [place, w]
    if dep is not None:
        in_specs.append(ANY)
        args.append(dep)
    grid_spec = pltpu.PrefetchScalarGridSpec(
        num_scalar_prefetch=1, grid=(r // tr,), in_specs=in_specs,
        out_specs=pl.BlockSpec((None, tr, cols), lambda i, pr: (pr[1], i, 0)),
    )
    return pl.pallas_call(
        body, name=name, grid_spec=grid_spec, out_shape=jax.ShapeDtypeStruct((N_CHIPS, r, cols), BF16),
        compiler_params=_params(("parallel",)),
    )(*args)


def _pair_exchange(grads, *, name):
    n = len(grads)

    def body(*refs):
        srcs, outs = refs[:n], refs[n:2 * n]
        send, recv, _ = refs[2 * n:]
        x, y, c, _o = _place()
        cps = []
        for a in range(n):
            half = srcs[a].shape[1] // 2
            cps.append(_remote(srcs[a].at[:, pl.ds((1 - c) * half, half), :], outs[a], send.at[a], recv.at[a], (x, y, 1 - c)))
        for cp in cps:
            cp.start()
        for cp in cps:
            cp.wait()

    out_shapes = [jax.ShapeDtypeStruct((N_CHIPS, t.shape[1] // 2, t.shape[2]), t.dtype) for t in grads]
    return _hbm_call(body, name, list(grads), out_shapes, n, 0)


HBM = pl.BlockSpec(memory_space=pltpu.HBM)
SEM = pl.BlockSpec(memory_space=pltpu.SEMAPHORE)
EFFECT = pltpu.SideEffectType.DATAFLOW_SIDE_EFFECTING
TOKEN = jax.ShapeDtypeStruct((8, 128), F32)


def _in_hbm(t):
    return pltpu.with_memory_space_constraint(t, pltpu.HBM)


def _gather_copies(buf, send, recv):
    x, y, c, others = _place()
    half = buf.shape[1] // 2
    rows = pl.ds(c * half, half)
    return [_remote(buf.at[2 * x + y, rows], buf.at[2 * x + y, rows], send.at[j], recv.at[j], (px, py, c))
            for j, (px, py) in enumerate(others)]


def _gather_start(bufs, *, name):
    n = len(bufs)

    def body(*refs):
        ins, sends, recvs, token = refs[:n], refs[n:2 * n], refs[2 * n:3 * n], refs[4 * n]
        for a in range(n):
            for cp in _gather_copies(ins[a], sends[a], recvs[a]):
                cp.start()
        token[...] = jnp.zeros_like(token)

    sems = [pltpu.SemaphoreType.DMA((3,))] * (2 * n)
    res = pl.pallas_call(
        body, name=name, out_shape=sems + [pltpu.HBM(t.shape, t.dtype) for t in bufs] + [TOKEN],
        in_specs=[HBM] * n, out_specs=[SEM] * (2 * n) + [HBM] * n + [pl.BlockSpec(memory_space=pltpu.VMEM)],
        input_output_aliases={a: 2 * n + a for a in range(n)}, compiler_params=pltpu.CompilerParams(has_side_effects=EFFECT),
    )(*[_in_hbm(t) for t in bufs])
    return [(res[2 * n + a], res[a], res[n + a]) for a in range(n)], res[3 * n]


def _gather_wait(state, after, *, name):
    buf, send, recv = state

    def body(buf_ref, send_ref, recv_ref, after_ref, out_ref):
        for cp in _gather_copies(buf_ref, send_ref, recv_ref):
            cp.wait_send()
            cp.wait_recv()

    return pl.pallas_call(
        body, name=name, out_shape=pltpu.HBM(buf.shape, buf.dtype), in_specs=[HBM, SEM, SEM, ANY], out_specs=HBM,
        input_output_aliases={0: 0}, compiler_params=pltpu.CompilerParams(has_side_effects=EFFECT),
    )(buf, send, recv, after)


def _forward_sibling(buf, *, name):
    def body(_in, out, send, recv, _local):
        x, y, c, others = _place()
        half = out.shape[1] // 2
        cps = []
        for j, (px, py) in enumerate(others):
            got = out.at[2 * px + py, pl.ds(c * half, half)]
            cps.append(_remote(got, got, send.at[j], recv.at[j], (x, y, 1 - c)))
        for cp in cps:
            cp.start()
        for j, (px, py) in enumerate(others):
            cps[j].wait_send()
            theirs = out.at[2 * px + py, pl.ds((1 - c) * half, half)]
            _remote(theirs, theirs, send.at[j], recv.at[j], (x, y, 1 - c)).wait_recv()

    return _hbm_call(body, name, [buf], [jax.ShapeDtypeStruct(buf.shape, buf.dtype)], 3, 0, aliases={0: 0})[0]


def _chip_copies(src, land, send, recv):
    _x, _y, c, others = _place()
    return [_remote(src.at[2 * px + py], land.at[j], send.at[j], recv.at[j], (px, py, c)) for j, (px, py) in enumerate(others)]


def _chip_start(partial, *, name):
    def body(src, land, send, recv, _src_thru, _land_thru, token):
        for cp in _chip_copies(src, land, send, recv):
            cp.start()
        token[...] = jnp.zeros_like(token)

    land_shape = (3,) + partial.shape[1:]
    sem = pltpu.SemaphoreType.DMA((3,))
    send, recv, src, land, token = pl.pallas_call(
        body, name=name, out_shape=[sem, sem, pltpu.HBM(partial.shape, partial.dtype), pltpu.HBM(land_shape, partial.dtype), TOKEN],
        in_specs=[HBM, HBM], out_specs=[SEM, SEM, HBM, HBM, pl.BlockSpec(memory_space=pltpu.VMEM)],
        input_output_aliases={0: 2, 1: 3}, compiler_params=pltpu.CompilerParams(has_side_effects=EFFECT),
    )(_in_hbm(partial), _in_hbm(lax.empty(land_shape, partial.dtype)))
    return (src, land, send, recv), token


def _chip_wait(state, after, *, name):
    src, land, send, recv = state

    def body(src_ref, land_ref, send_ref, recv_ref, after_ref, _src_out, _land_out):
        for cp in _chip_copies(src_ref, land_ref, send_ref, recv_ref):
            cp.wait_send()
            cp.wait_recv()

    return pl.pallas_call(
        body, name=name, out_shape=[pltpu.HBM(src.shape, src.dtype), pltpu.HBM(land.shape, land.dtype)],
        in_specs=[HBM, HBM, SEM, SEM, ANY], out_specs=[HBM, HBM], input_output_aliases={0: 0, 1: 1},
        compiler_params=pltpu.CompilerParams(has_side_effects=EFFECT),
    )(src, land, send, recv, after)[1]


def _share_halves(bufs, *, name):
    n = len(bufs)

    def body(*refs):
        outs = refs[n:2 * n]
        send, recv, _ = refs[2 * n:]
        x, y, c, _o = _place()
        cps = [_remote(outs[a].at[:, c], outs[a].at[:, c], send.at[a], recv.at[a], (x, y, 1 - c)) for a in range(n)]
        for cp in cps:
            cp.start()
        for a in range(n):
            cps[a].wait_send()
            theirs = outs[a].at[:, 1 - c]
            _remote(theirs, theirs, send.at[a], recv.at[a], (x, y, 1 - c)).wait_recv()

    out_shapes = [jax.ShapeDtypeStruct(t.shape, t.dtype) for t in bufs]
    return _hbm_call(body, name, list(bufs), out_shapes, n, 0, aliases={a: a for a in range(n)})


def _all_gather_small(pack, *, name):
    def body(src, out, send, recv, local):
        x, y, c, others = _place()

        def slot(px, py, pc):
            return out.at[4 * px + 2 * py + pc]

        mine = pltpu.make_async_copy(src, slot(x, y, c), local.at[0])
        mine.start()
        first = [_remote(src, slot(x, y, c), send.at[0], recv.at[0], (x, y, 1 - c))]
        first += [_remote(src, slot(x, y, c), send.at[1 + j], recv.at[1 + j], (px, py, c)) for j, (px, py) in enumerate(others)]
        for cp in first:
            cp.start()
        passed = []
        for j, (px, py) in enumerate(others):
            got = slot(px, py, c)
            _remote(got, got, send.at[1 + j], recv.at[1 + j], (px, py, c)).wait_recv()
            fwd = _remote(got, got, send.at[4 + j], recv.at[4 + j], (x, y, 1 - c))
            fwd.start()
            passed.append(fwd)
        theirs = slot(x, y, 1 - c)
        _remote(theirs, theirs, send.at[0], recv.at[0], (x, y, 1 - c)).wait_recv()
        for j, (px, py) in enumerate(others):
            got = slot(px, py, 1 - c)
            _remote(got, got, send.at[4 + j], recv.at[4 + j], (x, y, 1 - c)).wait_recv()
        for cp in first + passed:
            cp.wait_send()
        mine.wait()

    return _hbm_call(body, name, [pack], [jax.ShapeDtypeStruct((N_DEV,) + pack.shape, pack.dtype)], 7, 1)[0]


def _row_tile(rows, cols):
    return next(t for t in (512, 256, 128, 64, 32, 16) if rows % t == 0 and (t * cols * 4 <= (1 << 20) or t == 16))


def _pair_sum(grad, theirs, place, *, name):
    _, r, cols = grad.shape
    r2 = r // 2
    tr = _row_tile(r2, cols)
    nr = r2 // tr

    def body(place_ref, g_ref, t_ref, all_ref):
        all_ref[...] = (g_ref[...].astype(F32) + t_ref[...].astype(F32)).astype(all_ref.dtype)

    grid_spec = pltpu.PrefetchScalarGridSpec(
        num_scalar_prefetch=1, grid=(N_CHIPS, nr),
        in_specs=[pl.BlockSpec((None, tr, cols), lambda k, i, pr: (k, pr[0] * nr + i, 0)),
                  pl.BlockSpec((None, tr, cols), lambda k, i, pr: (k, i, 0))],
        out_specs=pl.BlockSpec((None, tr, cols), lambda k, i, pr: (k, i, 0)),
    )
    return pl.pallas_call(
        body, name=name, grid_spec=grid_spec, out_shape=jax.ShapeDtypeStruct((N_CHIPS, r2, cols), BF16),
        compiler_params=_params(("parallel", "parallel")),
    )(place, grad, theirs)


def _chip_sum(grad, theirs, got, place, buf, layer, depth, *, name):
    _, r, cols = grad.shape
    r2 = r // 2
    tr = _row_tile(r2, cols)
    nr = r2 // tr

    def body(place_ref, g_ref, t_ref, got_ref, *rest):
        own = g_ref[...].astype(F32) + t_ref[...].astype(F32)
        rest[-1][...] = ((own + got_ref[0].astype(F32)) + got_ref[1].astype(F32)) + got_ref[2].astype(F32)

    in_specs = [pl.BlockSpec((None, tr, cols), lambda i, pr: (pr[1], pr[0] * nr + i, 0)),
                pl.BlockSpec((None, tr, cols), lambda i, pr: (pr[1], i, 0)),
                pl.BlockSpec((3, tr, cols), lambda i, pr: (0, i, 0))]
    args = [place, grad, theirs, got]
    if buf is not None:
        in_specs.append(ANY)
        args.append(buf)
    grid_spec = pltpu.PrefetchScalarGridSpec(
        num_scalar_prefetch=1, grid=(nr,), in_specs=in_specs,
        out_specs=pl.BlockSpec((None, None, tr, cols), lambda i, pr: (layer, pr[0], i, 0)),
    )
    return pl.pallas_call(
        body, name=name, grid_spec=grid_spec, out_shape=jax.ShapeDtypeStruct((depth, 2, r2, cols), F32),
        input_output_aliases={} if buf is None else {4: 0}, compiler_params=_params(("parallel",)),
    )(*args)


def _sum_devices(parts, *, name):
    _, rows, cols = parts.shape
    tr = _pick(rows, (256, 128, 64, 32, 16, 8))

    def body(p_ref, out_ref):
        acc = p_ref[0]
        for d in range(1, N_DEV):
            acc = acc + p_ref[d]
        out_ref[...] = acc

    return pl.pallas_call(
        body, name=name, grid=(rows // tr,), in_specs=[pl.BlockSpec((N_DEV, tr, cols), lambda i: (0, i, 0))],
        out_specs=pl.BlockSpec((tr, cols), lambda i: (i, 0)), out_shape=jax.ShapeDtypeStruct((rows, cols), F32),
        compiler_params=_params(("parallel",)),
    )(parts)


def _pack(parts):
    rows = []
    for t in parts:
        flat = t.reshape(-1, 128)
        pad = (-flat.shape[0]) % 8
        rows.append(jnp.pad(flat, ((0, pad), (0, 0))) if pad else flat)
    return jnp.concatenate(rows, axis=0)


def _unpack(pack, shapes):
    out, r0 = [], 0
    for shp in shapes:
        n = math.prod(shp) // 128
        out.append(pack[r0:r0 + n].reshape(shp))
        r0 += n + (-n) % 8
    return out


SMALL = ["attn_norm_g", "q_norm_g", "k_norm_g", "sgu_norm_g", "sgu_w", "sgu_b", "out_norm_a_g", "out_norm_b_g",
         "ffn_norm_g", "conv_b"]
BIG = ["w_in", "w_out", "w_up", "w_down"]
ORDER = ["attn_norm_g", "w_in", "q_norm_g", "k_norm_g", "sgu_norm_g", "sgu_w", "sgu_b", "out_norm_a_g", "out_norm_b_g",
         "w_out", "ffn_norm_g", "w_up", "conv_w", "conv_b", "w_down"]


def kernel(x, attn_norm_g, w_in, q_norm_g, k_norm_g, sgu_norm_g, sgu_w, sgu_b, out_norm_a_g, out_norm_b_g, w_out, ffn_norm_g, w_up, conv_w, conv_b, w_down, loss_target, m_attn_norm_g, m_w_in, m_q_norm_g, m_k_norm_g, m_sgu_norm_g, m_sgu_w, m_sgu_b, m_out_norm_a_g, m_out_norm_b_g, m_w_out, m_ffn_norm_g, m_w_up, m_conv_w, m_conv_b, m_w_down, v_attn_norm_g, v_w_in, v_q_norm_g, v_k_norm_g, v_sgu_norm_g, v_sgu_w, v_sgu_b, v_out_norm_a_g, v_out_norm_b_g, v_w_out, v_ffn_norm_g, v_w_up, v_conv_w, v_conv_b, v_w_down):
    W = dict(attn_norm_g=attn_norm_g, w_in=w_in, q_norm_g=q_norm_g, k_norm_g=k_norm_g, sgu_norm_g=sgu_norm_g, sgu_w=sgu_w,
             sgu_b=sgu_b, out_norm_a_g=out_norm_a_g, out_norm_b_g=out_norm_b_g, w_out=w_out, ffn_norm_g=ffn_norm_g, w_up=w_up,
             conv_w=conv_w, conv_b=conv_b, w_down=w_down)
    M = dict(attn_norm_g=m_attn_norm_g, w_in=m_w_in, q_norm_g=m_q_norm_g, k_norm_g=m_k_norm_g, sgu_norm_g=m_sgu_norm_g,
             sgu_w=m_sgu_w, sgu_b=m_sgu_b, out_norm_a_g=m_out_norm_a_g, out_norm_b_g=m_out_norm_b_g, w_out=m_w_out,
             ffn_norm_g=m_ffn_norm_g, w_up=m_w_up, conv_w=m_conv_w, conv_b=m_conv_b, w_down=m_w_down)
    V = dict(attn_norm_g=v_attn_norm_g, w_in=v_w_in, q_norm_g=v_q_norm_g, k_norm_g=v_k_norm_g, sgu_norm_g=v_sgu_norm_g,
             sgu_w=v_sgu_w, sgu_b=v_sgu_b, out_norm_a_g=v_out_norm_a_g, out_norm_b_g=v_out_norm_b_g, w_out=v_w_out,
             ffn_norm_g=v_ffn_norm_g, w_up=v_w_up, conv_w=v_conv_w, conv_b=v_conv_b, w_down=v_w_down)
    depth = w_in.shape[0]
    s, d = x.shape[1], x.shape[2]
    n_heads = out_norm_a_g.shape[1]
    core = lax.axis_index("c")
    chip = 2 * lax.axis_index("x") + lax.axis_index("y")
    place = jnp.stack([core, chip]).astype(jnp.int32)
    xs = x.reshape(s, d)

    f_local = conv_w.shape[2]
    taps = lax.dynamic_update_slice(jnp.zeros((N_CHIPS, 16, f_local), F32), conv_w.reshape(1, depth * 3, f_local),
                                    (chip, 0, 0))
    order = [(l, n) for l in range(depth) for n in BIG]
    first, token = _gather_start([_cast_into(w_in, 0, place, name="cast_w_in"), taps], name="gather_start_first")
    rest, token = _gather_start([_cast_into(W[n], l, place, dep=token, name=f"cast_{n}") for l, n in order[1:]],
                                name="gather_start_rest")
    states = dict(zip(order, [first[0]] + rest))

    def gathered(l, n, after):
        return _forward_sibling(_gather_wait(states[(l, n)], after, name=f"gather_wait_{n}"), name=f"forward_{n}")

    saved, full = [], []
    cur = xs
    for l in range(depth):
        gain = attn_norm_g[l] + token[0, 0] if l == 0 else attn_norm_g[l]
        h = _rmsnorm_fwd(cur, gain, name="attn_norm")
        w_in_l = gathered(l, "w_in", h)
        if l == 0:
            taps = _forward_sibling(_gather_wait(first[1], w_in_l, name="gather_wait_taps"), name="forward_taps")
            taps = taps[:, :depth * 3].reshape(N_CHIPS, depth, 3, f_local)
            cw_full = jnp.transpose(taps, (1, 2, 0, 3)).reshape(depth, 3, N_CHIPS * f_local)
        p = _mm(h, w_in_l, "nn", b_split=N_CHIPS, caps=(2048, 256, 2048), name="proj_in")
        att, o_raw, lsum = _attn_fwd(p, q_norm_g[l], k_norm_g[l], out_norm_a_g[l], n_heads, name="attn_fwd")
        sg = _sgu_fwd(p, sgu_w[l], sgu_b[l], sgu_norm_g[l], out_norm_b_g[l], n_heads, name="sgu_fwd")
        mix = jnp.stack([att, sg])
        w_out_l = gathered(l, "w_out", mix).reshape(-1, d)
        x1 = _mm(mix, w_out_l, "nn", a_split=2, res=cur, caps=(2048, 512, 1024), name="proj_out")
        h2 = _rmsnorm_fwd(x1, ffn_norm_g[l], name="ffn_norm")
        w_up_l = gathered(l, "w_up", h2)
        up = _mm(h2, w_up_l, "nn", b_split=N_CHIPS, caps=(2048, 256, 2048), name="ffn_up")
        act = _conv_fwd(up, cw_full[l], conv_b[l], name="conv_fwd")
        w_down_l = gathered(l, "w_down", act).reshape(-1, d)
        x2 = _mm(act, w_down_l, "nn", res=x1, caps=(2048, 512, 512), name="ffn_down")
        full.append(dict(w_in=w_in_l, w_out=w_out_l, w_up=w_up_l, w_down=w_down_l, conv_w=cw_full[l]))
        saved.append(dict(x0=cur, h=h, p=p, o_raw=o_raw, lsum=lsum, mix=mix, x1=x1, h2=h2, up=up, act=act))
        cur = x2

    dx, dxb, sq = _loss_head(cur, loss_target.reshape(s, d), name="loss_head")
    loss = lax.psum(sq[0, 0] * (0.5 / d), ("x", "y", "c"))

    small_grads = {n: [None] * depth for n in SMALL + ["conv_w"]}
    def reduce_begin(n, grad):
        theirs = _pair_exchange([grad], name=f"pair_exchange_{n}")[0]
        state, tok = _chip_start(_pair_sum(grad, theirs, place, name=f"pair_sum_{n}"), name=f"chip_start_{n}")
        return (grad, theirs, state), tok

    pending = {}
    for l in reversed(range(depth)):
        fw, sv = full[l], saved[l]
        dact = _mm(dxb, fw["w_down"], "nt", caps=(2048, 512, 2048), name="d_act")
        g_down = _mm(sv["act"], dxb, "tn", caps=(512, 2048, 2048), out_dtype=BF16, name="g_down")
        pending[(l, "w_down")], tok = reduce_begin("w_down", g_down.reshape(N_CHIPS, -1, d))
        dup, dcw, dcb = _conv_bwd(sv["up"], dact, fw["conv_w"], conv_b[l] + tok[0, 0], name="conv_bwd")
        g_up = _mm(sv["h2"], dup, "tn", b_split=2, o_split=N_CHIPS, caps=(2048, 256, 2048), out_dtype=BF16, name="g_up")
        pending[(l, "w_up")], tok = reduce_begin("w_up", g_up)
        dh2 = _mm(dup, fw["w_up"], "nt", a_split=2, b_split=N_CHIPS, caps=(1024, 512, 2816), dep=tok, name="d_h2")
        dx1, dx1b, dg_ffn = _rmsnorm_bwd(sv["x1"], ffn_norm_g[l], dh2, dx, name="ffn_norm_bwd")
        dmix = _mm(dx1b, fw["w_out"], "nt", caps=(2048, 512, 2048), name="d_mix")
        g_out = _mm(sv["mix"], dx1b, "tn", a_split=2, caps=(512, 2048, 2048), out_dtype=BF16, name="g_out")
        pending[(l, "w_out")], tok = reduce_begin("w_out", g_out.reshape(N_CHIPS, -1, d))
        dqkv, dgq, dgk, dgoa = _attn_bwd(sv["p"], sv["o_raw"], sv["lsum"], dmix, q_norm_g[l] + tok[0, 0], k_norm_g[l],
                                         out_norm_a_g[l], n_heads, name="attn_bwd")
        duv, dsw, dsb, dgv, dgob = _sgu_bwd(sv["p"], dmix, sgu_w[l], sgu_b[l], sgu_norm_g[l], out_norm_b_g[l], n_heads,
                                            name="sgu_bwd")
        dp = jnp.concatenate([dqkv[0], dqkv[1], dqkv[2], duv[0], duv[1]], axis=1)
        g_in = _mm(sv["h"], dp, "tn", o_split=N_CHIPS, caps=(2048, 256, 2048), out_dtype=BF16, name="g_in")
        pending[(l, "w_in")], tok = reduce_begin("w_in", g_in)
        dh = _mm(dp, fw["w_in"], "nt", b_split=N_CHIPS, caps=(2048, 512, 1280), dep=tok, name="d_h")
        dx, dxb, dg_attn = _rmsnorm_bwd(sv["x0"], attn_norm_g[l], dh, dx1, name="attn_norm_bwd")

        small_grads["attn_norm_g"][l] = dg_attn.reshape(d)
        small_grads["q_norm_g"][l] = jnp.sum(dgq, axis=(0, 1))
        small_grads["k_norm_g"][l] = jnp.sum(dgk, axis=(0, 1))
        small_grads["sgu_norm_g"][l] = dgv.reshape(-1, BLK)
        small_grads["sgu_w"][l] = dsw
        small_grads["sgu_b"][l] = dsb.reshape(-1, BLK)
        small_grads["out_norm_a_g"][l] = dgoa.reshape(-1, BLK)
        small_grads["out_norm_b_g"][l] = dgob.reshape(-1, BLK)
        small_grads["ffn_norm_g"][l] = dg_ffn.reshape(d)
        small_grads["conv_b"][l] = dcb[:, 0, :].reshape(-1)
        small_grads["conv_w"][l] = jnp.transpose(dcw[:, :3, :], (1, 0, 2)).reshape(3, -1)

    G, D_, NM, NV = {}, {}, {}, {}
    after = dx
    for n in ("w_down", "w_up", "w_out", "w_in"):
        buf = None
        for l in reversed(range(depth)):
            grad, theirs, state = pending[(l, n)]
            got = _chip_wait(state, after, name=f"chip_wait_{n}")
            buf = _chip_sum(grad, theirs, got, place, buf, l, depth, name=f"chip_sum_{n}")
            after = buf
        G[n] = _share_halves([buf], name=f"share_halves_{n}")[0].reshape(W[n].shape)
        D_[n], NM[n], NV[n] = _adamw(W[n], G[n], M[n], V[n], name=f"adamw_{n}")
        after = NV[n]

    names = SMALL + ["conv_w"]
    pack = _pack([jnp.stack(small_grads[n]) for n in names])
    total = _sum_devices(_all_gather_small(pack, name="gather_small"), name="sum_small")
    f_full = conv_b.shape[1]
    shapes = [W[n].shape for n in SMALL] + [(depth, 3, f_full)]
    for n, t in zip(names, _unpack(total, shapes)):
        G[n] = t
    G["conv_w"] = lax.dynamic_slice_in_dim(G["conv_w"], chip * f_local, f_local, axis=2)

    D_["conv_w"], NM["conv_w"], NV["conv_w"] = _adamw(conv_w, G["conv_w"], m_conv_w, v_conv_w, name="adamw_conv_w")
    small_shapes = [W[n].shape for n in SMALL]
    res = _adamw(_pack([W[n] for n in SMALL]), _pack([G[n] for n in SMALL]), _pack([M[n] for n in SMALL]),
                 _pack([V[n] for n in SMALL]), name="adamw_small")
    for dst, t in zip((D_, NM, NV), res):
        for n, u in zip(SMALL, _unpack(t, small_shapes)):
            dst[n] = u

    return (loss, dx.reshape(x.shape), *[G[n] for n in ORDER], *[D_[n] for n in ORDER], *[NM[n] for n in ORDER],
            *[NV[n] for n in ORDER])
```

```python
import functools
import math

import jax
import jax.numpy as jnp
from jax import lax
from jax.experimental import pallas as pl
from jax.experimental.pallas import tpu as pltpu

F32 = jnp.float32
BF16 = jnp.bfloat16
EPS = 1e-6
BLK = 128
N_CHIPS = 4
N_DEV = 8
ADAM_LR, ADAM_B1, ADAM_B2, ADAM_EPS, ADAM_WD, ADAM_STEP = 0.001, 0.9, 0.999, 1e-08, 0.01, 10
VMEM_BIG = 48 * 1024 * 1024
MESH = pl.DeviceIdType.MESH
ANY = pl.BlockSpec(memory_space=pl.ANY)


def _pick(dim, prefs):
    for t in prefs:
        if dim % t == 0:
            return t
    raise ValueError(f"no tile in {prefs} divides {dim}")


def _params(sem=None, vmem=None):
    return pltpu.CompilerParams(dimension_semantics=sem, vmem_limit_bytes=vmem)


def _ldims(arr, split):
    if split == 1:
        return arr.shape
    p, r, cs = arr.shape
    assert p == split
    return (r, p * cs)


def _spec(tr, tc, split, cols, rc):
    if split == 1:
        return pl.BlockSpec((tr, tc), lambda i, j, k: rc(i, j, k))
    per = (cols // split) // tc

    def imap(i, j, k):
        r, c = rc(i, j, k)
        return (c // per, r, c % per)

    return pl.BlockSpec((None, tr, tc), imap)


def _fit(unit, cap):
    return max(t for t in range(128, min(unit, cap) + 1, 128) if unit % t == 0)


def _mm(a, b, mode, *, name, caps, a_split=1, b_split=1, o_split=1, out_dtype=F32, res=None, dep=None):
    ar, ac = _ldims(a, a_split)
    br, bc = _ldims(b, b_split)
    if mode == "nn":
        m, k, n = ar, ac, bc
        assert br == k
        ku, nu, mu = math.gcd(k // a_split, k), math.gcd(n // b_split, n // o_split), m
    elif mode == "nt":
        m, k, n = ar, ac, br
        assert bc == k
        ku, nu, mu = math.gcd(k // a_split, k // b_split), n // o_split, m
    else:
        k, m, n = ar, ac, bc
        assert br == k
        ku, nu, mu = k, math.gcd(n // b_split, n // o_split), m // a_split
    tm, tn, tk = _fit(mu, caps[0]), _fit(nu, caps[1]), _fit(ku, caps[2])
    nk = k // tk
    if mode == "nn":
        a_spec = _spec(tm, tk, a_split, k, lambda i, j, kk: (i, kk))
        b_spec = _spec(tk, tn, b_split, n, lambda i, j, kk: (kk, j))
    elif mode == "nt":
        a_spec = _spec(tm, tk, a_split, k, lambda i, j, kk: (i, kk))
        b_spec = _spec(tn, tk, b_split, k, lambda i, j, kk: (j, kk))
    else:
        a_spec = _spec(tk, tm, a_split, m, lambda i, j, kk: (kk, i))
        b_spec = _spec(tk, tn, b_split, n, lambda i, j, kk: (kk, j))
    o_spec = _spec(tm, tn, o_split, n, lambda i, j, kk: (i, j))
    dims = {"nn": (((1,), (0,)), ((), ())), "nt": (((1,), (1,)), ((), ())), "tn": (((0,), (0,)), ((), ()))}[mode]

    def body(a_ref, b_ref, *rest):
        if dep is not None:
            rest = rest[1:]
        if res is None:
            o_ref, acc = rest
        else:
            r_ref, o_ref, acc = rest
        kk = pl.program_id(2)

        @pl.when(kk == 0)
        def _():
            acc[...] = jnp.zeros_like(acc)

        acc[...] += lax.dot_general(a_ref[...].astype(BF16), b_ref[...].astype(BF16), dims, preferred_element_type=F32)

        @pl.when(kk == nk - 1)
        def _():
            out = acc[...]
            if res is not None:
                out = out + r_ref[...]
            o_ref[...] = out.astype(o_ref.dtype)

    in_specs, args = [a_spec, b_spec], [a, b]
    if dep is not None:
        in_specs.append(ANY)
        args.append(dep)
    if res is not None:
        in_specs.append(pl.BlockSpec((tm, tn), lambda i, j, kk: (i, j)))
        args.append(res)
    out_shape = (m, n) if o_split == 1 else (o_split, m, n // o_split)
    return pl.pallas_call(
        body, name=name, grid=(m // tm, n // tn, nk), in_specs=in_specs, out_specs=o_spec,
        out_shape=jax.ShapeDtypeStruct(out_shape, out_dtype), scratch_shapes=[pltpu.VMEM((tm, tn), F32)],
        compiler_params=_params(("parallel", "parallel", "arbitrary"), VMEM_BIG),
    )(*args)


def _rstd(v):
    return lax.rsqrt(jnp.mean(v * v, axis=-1, keepdims=True) + EPS)


def _norm_bwd(v, r, gain, dout):
    a = dout * gain
    dv = r * (a - v * (r * r * jnp.mean(a * v, axis=-1, keepdims=True)))
    return dv, dout * v * r


def _rmsnorm_fwd(x, g, *, name):
    s, d = x.shape
    tr = _pick(s, (256, 128))

    def body(x_ref, g_ref, o_ref):
        v = x_ref[...]
        o_ref[...] = (v * _rstd(v) * g_ref[...]).astype(o_ref.dtype)

    return pl.pallas_call(
        body, name=name, grid=(s // tr,),
        in_specs=[pl.BlockSpec((tr, d), lambda i: (i, 0)), pl.BlockSpec((1, d), lambda i: (0, 0))],
        out_specs=pl.BlockSpec((tr, d), lambda i: (i, 0)), out_shape=jax.ShapeDtypeStruct((s, d), BF16),
        compiler_params=_params(("parallel",)),
    )(x, g.reshape(1, d))


def _rmsnorm_bwd(x, g, dh, dres, *, name):
    s, d = x.shape
    tr = _pick(s, (256, 128))

    def body(x_ref, g_ref, dh_ref, dres_ref, dx_ref, dxb_ref, dg_ref):
        v = x_ref[...]
        dv, dgr = _norm_bwd(v, _rstd(v), g_ref[...], dh_ref[...])
        dx = dres_ref[...] + dv
        dx_ref[...] = dx
        dxb_ref[...] = dx.astype(BF16)
        part = jnp.sum(dgr, axis=0, keepdims=True)

        @pl.when(pl.program_id(0) == 0)
        def _():
            dg_ref[...] = part

        @pl.when(pl.program_id(0) > 0)
        def _():
            dg_ref[...] += part

    row = pl.BlockSpec((tr, d), lambda i: (i, 0))
    one = pl.BlockSpec((1, d), lambda i: (0, 0))
    return pl.pallas_call(
        body, name=name, grid=(s // tr,), in_specs=[row, one, row, row], out_specs=[row, row, one],
        out_shape=[jax.ShapeDtypeStruct((s, d), F32), jax.ShapeDtypeStruct((s, d), BF16), jax.ShapeDtypeStruct((1, d), F32)],
        compiler_params=_params(("arbitrary",)),
    )(x, g.reshape(1, d), dh, dres)


def _loss_head(y, target, *, name):
    s, d = y.shape
    tr = _pick(s, (256, 128))

    def body(y_ref, t_ref, dy_ref, dyb_ref, ls_ref):
        e = y_ref[...] - t_ref[...]
        dy = e * (1.0 / d)
        dy_ref[...] = dy
        dyb_ref[...] = dy.astype(BF16)
        part = jnp.full(ls_ref.shape, jnp.sum(e * e), F32)

        @pl.when(pl.program_id(0) == 0)
        def _():
            ls_ref[...] = part

        @pl.when(pl.program_id(0) > 0)
        def _():
            ls_ref[...] += part

    row = pl.BlockSpec((tr, d), lambda i: (i, 0))
    return pl.pallas_call(
        body, name=name, grid=(s // tr,), in_specs=[row, row], out_specs=[row, row, pl.BlockSpec((8, 128), lambda i: (0, 0))],
        out_shape=[jax.ShapeDtypeStruct((s, d), F32), jax.ShapeDtypeStruct((s, d), BF16), jax.ShapeDtypeStruct((8, 128), F32)],
        compiler_params=_params(("arbitrary",)),
    )(y, target)


def _iota2(axis):
    return lax.broadcasted_iota(jnp.int32, (BLK, BLK), axis)


def _tri_sum(v, tri):
    hi = v.astype(BF16)
    lo = (v - hi.astype(F32)).astype(BF16)
    return jnp.dot(hi, tri, preferred_element_type=F32) + jnp.dot(lo, tri, preferred_element_type=F32)


def _dot_nt(a, b):
    return lax.dot_general(a, b, (((1,), (1,)), ((), ())), preferred_element_type=F32)


def _dot_tn(a, b):
    return lax.dot_general(a, b, (((0,), (0,)), ((), ())), preferred_element_type=F32)


TQ_MAX = 512
HP = 2
VMEM_ATTN_BWD = 56 * 1024 * 1024


def _lanes(hh):
    return slice(hh * BLK, (hh + 1) * BLK)


def _causal(n, diag):
    if not diag:
        return None
    return lax.broadcasted_iota(jnp.int32, (n, BLK), 1) < lax.broadcasted_iota(jnp.int32, (n, BLK), 0)


def _sb_sums(z, mask, rhs_gt):
    lb = jnp.minimum(z, 0.0) - jnp.log(1.0 + jnp.exp(-jnp.abs(z)))
    l1m = lb - z
    if mask is not None:
        l1m = jnp.where(mask, l1m, 0.0)
    return lb, _tri_sum(l1m, rhs_gt)


def _below(old, new, r0):
    return new if r0 == 0 else jnp.concatenate([old[:r0], new], axis=0)


def _attn_fwd(p, gq, gk, go, n_heads, *, name):
    s = p.shape[0]
    tq = min(TQ_MAX, s)
    per = tq // BLK
    scale = BLK ** -0.5

    def body(q_ref, k_ref, v_ref, gq_ref, gk_ref, go_ref, att_ref, o_ref, l_ref, qn, kn, vb):
        for hh in range(HP):
            q = q_ref[:, _lanes(hh)]
            k = k_ref[:, _lanes(hh)]
            qn[:, _lanes(hh)] = (q * _rstd(q) * gq_ref[...]).astype(BF16)
            kn[:, _lanes(hh)] = (k * _rstd(k) * gk_ref[...]).astype(BF16)
            vb[:, _lanes(hh)] = v_ref[:, _lanes(hh)].astype(BF16)
        rhs_gt = jnp.concatenate([(_iota2(0) > _iota2(1)).astype(BF16), jnp.ones((BLK, BLK), BF16)], axis=1)

        def step(q0, j, r0, diag, states):
            n = tq - r0
            rows = pl.ds(pl.multiple_of(q0 + r0, BLK), n)
            cols = pl.ds(pl.multiple_of(j * BLK, BLK), BLK)
            mask = _causal(n, diag)
            zs = [_dot_nt(qn[rows, _lanes(hh)], kn[cols, _lanes(hh)]) * scale for hh in range(HP)]
            sums = [_sb_sums(z, mask, rhs_gt) for z in zs]
            new = []
            for hh in range(HP):
                acc, later = states[hh]
                lb, both = sums[hh]
                a = jnp.exp(lb + both[:, :BLK] + later[r0:])
                if diag:
                    a = jnp.where(mask, a, 0.0)
                acc_new = acc[r0:] + jnp.dot(a.astype(BF16), vb[cols, _lanes(hh)], preferred_element_type=F32)
                new.append((_below(acc, acc_new, r0), _below(later, later[r0:] + both[:, BLK:], r0)))
            return tuple(new)

        def q_block(i, _):
            q0 = i * tq
            zero = jnp.zeros((tq, BLK), F32)
            states = ((zero, zero),) * HP
            for jd in reversed(range(per)):
                states = step(q0, i * per + jd, jd * BLK, True, states)
            states = lax.fori_loop(0, i * per, lambda jj, st: step(q0, i * per - 1 - jj, 0, False, st), states)
            tile = pl.ds(pl.multiple_of(q0, tq), tq)
            for hh in range(HP):
                o, total = states[hh]
                o_ref[tile, _lanes(hh)] = o
                l_ref[hh, tile, :] = total
                att_ref[tile, _lanes(hh)] = (o * _rstd(o) * go_ref[hh]).astype(att_ref.dtype)
            return 0

        lax.fori_loop(0, s // tq, q_block, 0)

    assert n_heads % HP == 0
    groups = n_heads // HP

    def col(part):
        return pl.BlockSpec((s, HP * BLK), lambda g: (0, part * groups + g))

    gain = pl.BlockSpec((1, BLK), lambda g: (0, 0))
    per_head = pl.BlockSpec((HP, 1, BLK), lambda g: (g, 0, 0))
    return pl.pallas_call(
        body, name=name, grid=(groups,),
        in_specs=[col(0), col(1), col(2), gain, gain, per_head],
        out_specs=[col(0), col(0), pl.BlockSpec((HP, s, BLK), lambda g: (g, 0, 0))],
        out_shape=[jax.ShapeDtypeStruct((s, n_heads * BLK), BF16), jax.ShapeDtypeStruct((s, n_heads * BLK), F32),
                   jax.ShapeDtypeStruct((n_heads, s, BLK), F32)],
        scratch_shapes=[pltpu.VMEM((s, HP * BLK), BF16)] * 3,
        compiler_params=_params(("parallel",), VMEM_BIG),
    )(p, p, p, gq.reshape(1, BLK), gk.reshape(1, BLK), go.reshape(n_heads, 1, BLK))


def _attn_bwd(p, o_raw, lsum, dmix, gq, gk, go, n_heads, *, name):
    s = p.shape[0]
    tq = min(TQ_MAX, s)
    per = tq // BLK
    scale = BLK ** -0.5

    def body(q_ref, k_ref, v_ref, o_ref, l_ref, da_ref, gq_ref, gk_ref, go_ref,
             dqkv_ref, dgq_ref, dgk_ref, dgo_ref, qn, kn, vb, dob, dqn, dkn, dvv):
        for hh in range(HP):
            q = q_ref[:, _lanes(hh)]
            k = k_ref[:, _lanes(hh)]
            qn[:, _lanes(hh)] = (q * _rstd(q) * gq_ref[...]).astype(BF16)
            kn[:, _lanes(hh)] = (k * _rstd(k) * gk_ref[...]).astype(BF16)
            vb[:, _lanes(hh)] = v_ref[:, _lanes(hh)].astype(BF16)
            o = o_ref[:, _lanes(hh)]
            do, dgo_rows = _norm_bwd(o, _rstd(o), go_ref[hh], da_ref[:, _lanes(hh)])
            dob[:, _lanes(hh)] = do.astype(BF16)
            dgo_ref[hh] = jnp.sum(dgo_rows, axis=0, keepdims=True)
        dkn[...] = jnp.zeros_like(dkn)
        dvv[...] = jnp.zeros_like(dvv)
        ones = jnp.ones((BLK, BLK), BF16)
        rhs_gt = jnp.concatenate([(_iota2(0) > _iota2(1)).astype(BF16), ones], axis=1)
        rhs_lt = jnp.concatenate([(_iota2(0) < _iota2(1)).astype(BF16), ones], axis=1)

        def step(q0, j, r0, diag, states):
            n = tq - r0
            rows = pl.ds(pl.multiple_of(q0 + r0, BLK), n)
            cols = pl.ds(pl.multiple_of(j * BLK, BLK), BLK)
            mask = _causal(n, diag)
            zs = [_dot_nt(qn[rows, _lanes(hh)], kn[cols, _lanes(hh)]) * scale for hh in range(HP)]
            das = [_dot_nt(dob[rows, _lanes(hh)], vb[cols, _lanes(hh)]) for hh in range(HP)]
            sums = [_sb_sums(z, mask, rhs_gt) for z in zs]
            mids = []
            for hh in range(HP):
                lb, both = sums[hh]
                upto = states[hh][0][r0:] + both[:, BLK:]
                a = jnp.exp(lb + both[:, :BLK] + (l_ref[hh, rows, :] - upto))
                if diag:
                    a = jnp.where(mask, a, 0.0)
                g = das[hh] * a
                mids.append((lb, upto, a, g, _tri_sum(g, rhs_lt)))
            new = []
            for hh in range(HP):
                seen, gsum, dq = states[hh]
                lb, upto, a, g, bothg = mids[hh]
                beta = jnp.exp(lb)
                dz = g * (1.0 - beta) - beta * (bothg[:, :BLK] + gsum[r0:])
                if diag:
                    dz = jnp.where(mask, dz, 0.0)
                dzs = (dz * scale).astype(BF16)
                dq_new = dq[r0:] + jnp.dot(dzs, kn[cols, _lanes(hh)], preferred_element_type=F32)
                dkn[cols, _lanes(hh)] += _dot_tn(dzs, qn[rows, _lanes(hh)])
                dvv[cols, _lanes(hh)] += _dot_tn(a.astype(BF16), dob[rows, _lanes(hh)])
                new.append((_below(seen, upto, r0), _below(gsum, gsum[r0:] + bothg[:, BLK:], r0), _below(dq, dq_new, r0)))
            return tuple(new)

        def q_block(i, _):
            q0 = i * tq
            zero = jnp.zeros((tq, BLK), F32)
            states = ((zero, zero, zero),) * HP
            states = lax.fori_loop(0, i * per, lambda j, st: step(q0, j, 0, False, st), states)
            for jd in range(per):
                states = step(q0, i * per + jd, jd * BLK, True, states)
            tile = pl.ds(pl.multiple_of(q0, tq), tq)
            for hh in range(HP):
                dqn[tile, _lanes(hh)] = states[hh][2]
            return 0

        lax.fori_loop(0, s // tq, q_block, 0)
        for hh in range(HP):
            q = q_ref[:, _lanes(hh)]
            k = k_ref[:, _lanes(hh)]
            dq_raw, dgq_rows = _norm_bwd(q, _rstd(q), gq_ref[...], dqn[:, _lanes(hh)])
            dk_raw, dgk_rows = _norm_bwd(k, _rstd(k), gk_ref[...], dkn[:, _lanes(hh)])
            dqkv_ref[0, :, _lanes(hh)] = dq_raw.astype(BF16)
            dqkv_ref[1, :, _lanes(hh)] = dk_raw.astype(BF16)
            dqkv_ref[2, :, _lanes(hh)] = dvv[:, _lanes(hh)].astype(BF16)
            dgq_ref[hh] = jnp.sum(dgq_rows, axis=0, keepdims=True)
            dgk_ref[hh] = jnp.sum(dgk_rows, axis=0, keepdims=True)

    assert n_heads % HP == 0
    groups = n_heads // HP

    def col(part):
        return pl.BlockSpec((s, HP * BLK), lambda g: (0, part * groups + g))

    gain = pl.BlockSpec((1, BLK), lambda g: (0, 0))
    per_head = pl.BlockSpec((HP, 1, BLK), lambda g: (g, 0, 0))
    head_gain = jax.ShapeDtypeStruct((n_heads, 1, BLK), F32)
    return pl.pallas_call(
        body, name=name, grid=(groups,),
        in_specs=[col(0), col(1), col(2), col(0), pl.BlockSpec((HP, s, BLK), lambda g: (g, 0, 0)), col(0),
                  gain, gain, per_head],
        out_specs=[pl.BlockSpec((3, s, HP * BLK), lambda g: (0, 0, g)), per_head, per_head, per_head],
        out_shape=[jax.ShapeDtypeStruct((3, s, n_heads * BLK), BF16), head_gain, head_gain, head_gain],
        scratch_shapes=[pltpu.VMEM((s, HP * BLK), BF16)] * 4 + [pltpu.VMEM((s, HP * BLK), F32)] * 3,
        compiler_params=_params(("parallel",), VMEM_ATTN_BWD),
    )(p, p, p, o_raw, lsum, dmix, gq.reshape(1, BLK), gk.reshape(1, BLK), go.reshape(n_heads, 1, BLK))


_INV_SQRT2 = 0.7071067811865476
_INV_SQRT2PI = 0.3989422804014327


def _gelu(x):
    return 0.5 * x * (1.0 + lax.erf(x * _INV_SQRT2))


def _gelu_grad(x):
    return 0.5 * (1.0 + lax.erf(x * _INV_SQRT2)) + x * jnp.exp(-0.5 * x * x) * _INV_SQRT2PI


def _sgu_fwd(p, w, b, gv, gout, n_heads, *, name):
    s = p.shape[0]
    n_groups = w.shape[0]
    nb = s // BLK

    def body(u_ref, v_ref, w_ref, b_ref, gv_ref, go_ref, out_ref):
        wt = jnp.where(_iota2(0) >= _iota2(1), w_ref[...], 0.0).astype(BF16)
        bias = b_ref[...]

        def chunk(c, _):
            rows = pl.ds(pl.multiple_of(c * BLK, BLK), BLK)
            u = _gelu(u_ref[rows, :])
            vv = _gelu(v_ref[rows, :])
            vs = vv * _rstd(vv) * gv_ref[...]
            gated = u * (jnp.dot(wt, vs.astype(BF16), preferred_element_type=F32) + bias)
            out_ref[rows, :] = (gated * _rstd(gated) * go_ref[...]).astype(out_ref.dtype)
            return 0

        lax.fori_loop(0, nb, chunk, 0)

    def col(off):
        return pl.BlockSpec((s, BLK), lambda g: (0, off + g))

    per_group = pl.BlockSpec((None, 1, BLK), lambda g: (g, 0, 0))
    return pl.pallas_call(
        body, name=name, grid=(n_groups,),
        in_specs=[col(3 * n_heads), col(3 * n_heads + n_groups), pl.BlockSpec((None, BLK, BLK), lambda g: (g, 0, 0)),
                  pl.BlockSpec((None, BLK, 1), lambda g: (g, 0, 0)), per_group, per_group],
        out_specs=col(0), out_shape=jax.ShapeDtypeStruct((s, n_groups * BLK), BF16),
        compiler_params=_params(("parallel",), VMEM_BIG),
    )(p, p, w, b.reshape(n_groups, BLK, 1), gv.reshape(n_groups, 1, BLK), gout.reshape(n_groups, 1, BLK))


def _sgu_bwd(p, dmix, w, b, gv, gout, n_heads, *, name):
    s = p.shape[0]
    n_groups = w.shape[0]
    nb = s // BLK

    def body(u_ref, v_ref, ds_ref, w_ref, b_ref, gv_ref, go_ref, duv_ref, dw_ref, db_ref, dgv_ref, dgo_ref):
        lower = _iota2(0) >= _iota2(1)
        wt = jnp.where(lower, w_ref[...], 0.0).astype(BF16)
        bias = b_ref[...]

        def chunk(c, carry):
            dw, db, dgv, dgo = carry
            rows = pl.ds(pl.multiple_of(c * BLK, BLK), BLK)
            up = u_ref[rows, :]
            vp = v_ref[rows, :]
            u = _gelu(up)
            vv = _gelu(vp)
            rv = _rstd(vv)
            vsb = (vv * rv * gv_ref[...]).astype(BF16)
            mixed = jnp.dot(wt, vsb, preferred_element_type=F32) + bias
            gated = u * mixed
            dgated, dgo_rows = _norm_bwd(gated, _rstd(gated), go_ref[...], ds_ref[rows, :])
            dmixed = dgated * u
            dmb = dmixed.astype(BF16)
            dvs = _dot_tn(wt, dmb)
            dvv, dgv_rows = _norm_bwd(vv, rv, gv_ref[...], dvs)
            duv_ref[0, rows, :] = (dgated * mixed * _gelu_grad(up)).astype(BF16)
            duv_ref[1, rows, :] = (dvv * _gelu_grad(vp)).astype(BF16)
            return (dw + _dot_nt(dmb, vsb), db + jnp.sum(dmixed, axis=1, keepdims=True),
                    dgv + jnp.sum(dgv_rows, axis=0, keepdims=True), dgo + jnp.sum(dgo_rows, axis=0, keepdims=True))

        row0 = jnp.zeros((1, BLK), F32)
        dw, db, dgv, dgo = lax.fori_loop(0, nb, chunk, (jnp.zeros((BLK, BLK), F32), jnp.zeros((BLK, 1), F32), row0, row0))
        dw_ref[...] = jnp.where(lower, dw, 0.0)
        db_ref[...] = db
        dgv_ref[...] = dgv
        dgo_ref[...] = dgo

    def col(off):
        return pl.BlockSpec((s, BLK), lambda g: (0, off + g))

    per_group = pl.BlockSpec((None, 1, BLK), lambda g: (g, 0, 0))
    square = pl.BlockSpec((None, BLK, BLK), lambda g: (g, 0, 0))
    column = pl.BlockSpec((None, BLK, 1), lambda g: (g, 0, 0))
    gain = jax.ShapeDtypeStruct((n_groups, 1, BLK), F32)
    return pl.pallas_call(
        body, name=name, grid=(n_groups,),
        in_specs=[col(3 * n_heads), col(3 * n_heads + n_groups), col(n_heads), square, column, per_group, per_group],
        out_specs=[pl.BlockSpec((2, s, BLK), lambda g: (0, 0, g)), square, column, per_group, per_group],
        out_shape=[jax.ShapeDtypeStruct((2, s, n_groups * BLK), BF16), jax.ShapeDtypeStruct((n_groups, BLK, BLK), F32),
                   jax.ShapeDtypeStruct((n_groups, BLK, 1), F32), gain, gain],
        compiler_params=_params(("parallel",), VMEM_BIG),
    )(p, p, dmix, w, b.reshape(n_groups, BLK, 1), gv.reshape(n_groups, 1, BLK), gout.reshape(n_groups, 1, BLK))


CONV_ROWS = 256
HALO = 8


def _shift_down(ref, r0, n, first):
    cur = ref[pl.ds(r0, n), :]
    prev = jnp.zeros((HALO, cur.shape[1]), F32) if first else ref[pl.ds(r0 - HALO, HALO), :]
    ext = jnp.concatenate([prev, cur], axis=0)
    return pltpu.roll(ext, 1, 0)[HALO:], pltpu.roll(ext, 2, 0)[HALO:], cur


def _shift_up(ref, r0, n, last):
    cur = ref[pl.ds(r0, n), :]
    nxt = jnp.zeros((HALO, cur.shape[1]), F32) if last else ref[pl.ds(r0 + n, HALO), :]
    ext = jnp.concatenate([cur, nxt], axis=0)
    return cur, pltpu.roll(ext, n + HALO - 1, 0)[:n], pltpu.roll(ext, n + HALO - 2, 0)[:n]


def _conv_rows(x1, x2, x0, w_ref, b_ref):
    return ((b_ref[...] + x2 * w_ref[0:1, :]) + x1 * w_ref[1:2, :]) + x0 * w_ref[2:3, :]


def _conv_specs(s, f, tc):
    nf = f // tc
    gate = pl.BlockSpec((s, tc), lambda n: (0, n))
    val = pl.BlockSpec((s, tc), lambda n: (0, nf + n))
    wg = pl.BlockSpec((3, tc), lambda n: (0, n))
    wv = pl.BlockSpec((3, tc), lambda n: (0, nf + n))
    bg = pl.BlockSpec((1, tc), lambda n: (0, n))
    bv = pl.BlockSpec((1, tc), lambda n: (0, nf + n))
    return nf, gate, val, wg, wv, bg, bv


def _conv_fwd(up, cw, cb, *, name):
    s, f2 = up.shape
    f = f2 // 2
    tc = _pick(f, (256, 128))
    cr = min(CONV_ROWS, s)
    nf, gate, val, wg, wv, bg, bv = _conv_specs(s, f, tc)

    def body(g_ref, v_ref, wg_ref, wv_ref, bg_ref, bv_ref, out_ref):
        for r0 in range(0, s, cr):
            gc = _conv_rows(*_shift_down(g_ref, r0, cr, r0 == 0), wg_ref, bg_ref)
            vc = _conv_rows(*_shift_down(v_ref, r0, cr, r0 == 0), wv_ref, bv_ref)
            out_ref[pl.ds(r0, cr), :] = (gc * jax.nn.sigmoid(gc) * vc).astype(out_ref.dtype)

    return pl.pallas_call(
        body, name=name, grid=(nf,), in_specs=[gate, val, wg, wv, bg, bv], out_specs=gate,
        out_shape=jax.ShapeDtypeStruct((s, f), BF16), compiler_params=_params(("parallel",), VMEM_BIG),
    )(up, up, cw, cw, cb.reshape(1, f2), cb.reshape(1, f2))


def _conv_bwd(up, dact, cw, cb, *, name):
    s, f2 = up.shape
    f = f2 // 2
    tc = _pick(f, (256, 128))
    cr = min(CONV_ROWS, s)
    nf, gate, val, wg, wv, bg, bv = _conv_specs(s, f, tc)

    def body(g_ref, v_ref, da_ref, wg_ref, wv_ref, bg_ref, bv_ref, dup_ref, dw_ref, db_ref, dgc, dvc):
        zero = jnp.zeros((1, tc), F32)
        sums = [[zero] * 4, [zero] * 4]
        for r0 in range(0, s, cr):
            rows = pl.ds(r0, cr)
            gx = _shift_down(g_ref, r0, cr, r0 == 0)
            vx = _shift_down(v_ref, r0, cr, r0 == 0)
            gc = _conv_rows(*gx, wg_ref, bg_ref)
            vc = _conv_rows(*vx, wv_ref, bv_ref)
            sig = jax.nn.sigmoid(gc)
            da = da_ref[rows, :]
            d_gate = da * vc * (sig * (1.0 + gc * (1.0 - sig)))
            d_val = da * (gc * sig)
            dgc[rows, :] = d_gate
            dvc[rows, :] = d_val
            for part, (dc, (x1, x2, x0)) in enumerate(((d_gate, gx), (d_val, vx))):
                for tap, xs in enumerate((x2, x1, x0)):
                    sums[part][tap] = sums[part][tap] + jnp.sum(dc * xs, axis=0, keepdims=True)
                sums[part][3] = sums[part][3] + jnp.sum(dc, axis=0, keepdims=True)
        dw_ref[...] = jnp.zeros_like(dw_ref)
        db_ref[...] = jnp.zeros_like(db_ref)
        for part, (dc_ref, w_ref) in enumerate(((dgc, wg_ref), (dvc, wv_ref))):
            for tap in range(3):
                dw_ref[part, tap:tap + 1, :] = sums[part][tap]
            db_ref[part, 0:1, :] = sums[part][3]
            for r0 in range(0, s, cr):
                d0, d1, d2 = _shift_up(dc_ref, r0, cr, r0 + cr == s)
                dup_ref[part, pl.ds(r0, cr), :] = ((d0 * w_ref[2:3, :] + d1 * w_ref[1:2, :]) + d2 * w_ref[0:1, :]).astype(BF16)

    small = pl.BlockSpec((2, 8, tc), lambda n: (0, 0, n))
    return pl.pallas_call(
        body, name=name, grid=(nf,), in_specs=[gate, val, gate, wg, wv, bg, bv],
        out_specs=[pl.BlockSpec((2, s, tc), lambda n: (0, 0, n)), small, small],
        out_shape=[jax.ShapeDtypeStruct((2, s, f), BF16), jax.ShapeDtypeStruct((2, 8, f), F32),
                   jax.ShapeDtypeStruct((2, 8, f), F32)],
        scratch_shapes=[pltpu.VMEM((s, tc), F32)] * 2, compiler_params=_params(("parallel",), VMEM_BIG),
    )(up, up, dact, cw, cw, cb.reshape(1, f2), cb.reshape(1, f2))


def _adamw(w, g, m, v, *, name):
    shape = w.shape
    cols = shape[-1]
    rows = w.size // cols
    if rows * cols * 4 <= (1 << 20):
        tr = rows
    else:
        tr = next(t for t in (1024, 512, 256, 128, 64, 32, 16, 8) if rows % t == 0 and (t * cols * 4 <= (1 << 20) or t == 8))

    def body(w_ref, g_ref, m_ref, v_ref, d_ref, nm_ref, nv_ref):
        gr = g_ref[...]
        nm = ADAM_B1 * m_ref[...] + (1.0 - ADAM_B1) * gr
        nv = ADAM_B2 * v_ref[...] + (1.0 - ADAM_B2) * (gr * gr)
        m_hat = nm / (1.0 - ADAM_B1 ** ADAM_STEP)
        v_hat = nv / (1.0 - ADAM_B2 ** ADAM_STEP)
        d_ref[...] = -ADAM_LR * (m_hat / (jnp.sqrt(v_hat) + ADAM_EPS) + ADAM_WD * w_ref[...])
        nm_ref[...] = nm
        nv_ref[...] = nv

    blk = pl.BlockSpec((tr, cols), lambda i: (i, 0))
    out = jax.ShapeDtypeStruct((rows, cols), F32)
    res = pl.pallas_call(
        body, name=name, grid=(rows // tr,), in_specs=[blk] * 4, out_specs=[blk] * 3, out_shape=[out] * 3,
        compiler_params=_params(("parallel",)),
    )(*[t.reshape(rows, cols) for t in (w, g, m, v)])
    return [t.reshape(shape) for t in res]


def _place():
    x, y, c = lax.axis_index("x"), lax.axis_index("y"), lax.axis_index("c")
    others = [(1 - x, y), (x, 1 - y), (1 - x, 1 - y)]
    return x, y, c, others


def _remote(src, dst, send_sem, recv_sem, device):
    return pltpu.make_async_remote_copy(src_ref=src, dst_ref=dst, send_sem=send_sem, recv_sem=recv_sem, device_id=device,
                                        device_id_type=MESH)


def _hbm_call(body, name, args, out_shapes, n_sems, n_local, aliases=None):
    return pl.pallas_call(
        body, name=name, in_specs=[ANY] * len(args), out_specs=[ANY] * len(out_shapes), out_shape=out_shapes,
        scratch_shapes=[pltpu.SemaphoreType.DMA((n_sems,)), pltpu.SemaphoreType.DMA((n_sems,)),
                        pltpu.SemaphoreType.DMA((max(n_local, 1),))],
        input_output_aliases=aliases or {}, compiler_params=pltpu.CompilerParams(has_side_effects=True),
    )(*args)


def _all_gather_weights(halved, whole, *, name):
    nh, nw = len(halved), len(whole)
    arrays = list(halved) + list(whole)

    def body(*refs):
        srcs, outs = refs[:nh + nw], refs[nh + nw:2 * (nh + nw)]
        send, recv, local = refs[2 * (nh + nw):]
        x, y, c, others = _place()
        me = 2 * x + y
        locals_ = [pltpu.make_async_copy(srcs[nh + a], outs[nh + a].at[me], local.at[a]) for a in range(nw)]
        for cp in locals_:
            cp.start()
        sends = []
        for a in range(nh):
            half = outs[a].shape[1] // 2
            rows = pl.ds(c * half, half)
            for j, (px, py) in enumerate(others):
                sends.append(_remote(outs[a].at[me, rows], outs[a].at[me, rows], send.at[6 * a + j], recv.at[6 * a + j],
                                     (px, py, c)))
        for a in range(nw):
            for j, (px, py) in enumerate(others):
                sends.append(_remote(srcs[nh + a], outs[nh + a].at[me], send.at[6 * nh + 3 * a + j],
                                     recv.at[6 * nh + 3 * a + j], (px, py, c)))
        for cp in sends:
            cp.start()
        for a in range(nh):
            half = outs[a].shape[1] // 2
            rows = pl.ds(c * half, half)
            for j, (px, py) in enumerate(others):
                got = outs[a].at[2 * px + py, rows]
                _remote(got, got, send.at[6 * a + j], recv.at[6 * a + j], (px, py, c)).wait_recv()
                fwd = _remote(got, got, send.at[6 * a + 3 + j], recv.at[6 * a + 3 + j], (x, y, 1 - c))
                fwd.start()
                sends.append(fwd)
        for a in range(nh):
            half = outs[a].shape[1] // 2
            theirs = pl.ds((1 - c) * half, half)
            for j, (px, py) in enumerate(others):
                got = outs[a].at[2 * px + py, theirs]
                _remote(got, got, send.at[6 * a + 3 + j], recv.at[6 * a + 3 + j], (x, y, 1 - c)).wait_recv()
        for a in range(nw):
            for j, (px, py) in enumerate(others):
                got = outs[nh + a].at[2 * px + py]
                _remote(got, got, send.at[6 * nh + 3 * a + j], recv.at[6 * nh + 3 * a + j], (px, py, c)).wait_recv()
        for cp in sends:
            cp.wait_send()
        for cp in locals_:
            cp.wait()

    out_shapes = [jax.ShapeDtypeStruct(t.shape, t.dtype) for t in halved]
    out_shapes += [jax.ShapeDtypeStruct((N_CHIPS,) + t.shape, t.dtype) for t in whole]
    return _hbm_call(body, name, arrays, out_shapes, 6 * nh + 3 * nw, nw, aliases={a: a for a in range(nh)})


def _cast_into(w, layer, place, *, name, dep=None):
    _, r, cols = w.shape
    tr = _row_tile(r, cols)

    def body(place_ref, w_ref, *rest):
        rest[-1][...] = w_ref[...].astype(BF16)

    in_specs, args = [pl.BlockSpec((None, tr, cols), lambda i, pr: (layer, i, 0))], [place, w]
    if dep is not None:
        in_specs.append(ANY)
        args.append(dep)
    grid_spec = pltpu.PrefetchScalarGridSpec(
        num_scalar_prefetch=1, grid=(r // tr,), in_specs=in_specs,
        out_specs=pl.BlockSpec((None, tr, cols), lambda i, pr: (pr[1], i, 0)),
    )
    return pl.pallas_call(
        body, name=name, grid_spec=grid_spec, out_shape=jax.ShapeDtypeStruct((N_CHIPS, r, cols), BF16),
        compiler_params=_params(("parallel",)),
    )(*args)


def _pair_exchange(grads, *, name):
    n = len(grads)

    def body(*refs):
        srcs, outs = refs[:n], refs[n:2 * n]
        send, recv, _ = refs[2 * n:]
        x, y, c, _o = _place()
        cps = []
        for a in range(n):
            half = srcs[a].shape[1] // 2
            cps.append(_remote(srcs[a].at[:, pl.ds((1 - c) * half, half), :], outs[a], send.at[a], recv.at[a], (x, y, 1 - c)))
        for cp in cps:
            cp.start()
        for cp in cps:
            cp.wait()

    out_shapes = [jax.ShapeDtypeStruct((N_CHIPS, t.shape[1] // 2, t.shape[2]), t.dtype) for t in grads]
    return _hbm_call(body, name, list(grads), out_shapes, n, 0)


HBM = pl.BlockSpec(memory_space=pltpu.HBM)
SEM = pl.BlockSpec(memory_space=pltpu.SEMAPHORE)
EFFECT = pltpu.SideEffectType.DATAFLOW_SIDE_EFFECTING
TOKEN = jax.ShapeDtypeStruct((8, 128), F32)


def _in_hbm(t):
    return pltpu.with_memory_space_constraint(t, pltpu.HBM)


def _gather_copies(buf, send, recv):
    x, y, c, others = _place()
    half = buf.shape[1] // 2
    rows = pl.ds(c * half, half)
    return [_remote(buf.at[2 * x + y, rows], buf.at[2 * x + y, rows], send.at[j], recv.at[j], (px, py, c))
            for j, (px, py) in enumerate(others)]


def _gather_start(bufs, *, name):
    n = len(bufs)

    def body(*refs):
        ins, sends, recvs, token = refs[:n], refs[n:2 * n], refs[2 * n:3 * n], refs[4 * n]
        for a in range(n):
            for cp in _gather_copies(ins[a], sends[a], recvs[a]):
                cp.start()
        token[...] = jnp.zeros_like(token)

    sems = [pltpu.SemaphoreType.DMA((3,))] * (2 * n)
    res = pl.pallas_call(
        body, name=name, out_shape=sems + [pltpu.HBM(t.shape, t.dtype) for t in bufs] + [TOKEN],
        in_specs=[HBM] * n, out_specs=[SEM] * (2 * n) + [HBM] * n + [pl.BlockSpec(memory_space=pltpu.VMEM)],
        input_output_aliases={a: 2 * n + a for a in range(n)}, compiler_params=pltpu.CompilerParams(has_side_effects=EFFECT),
    )(*[_in_hbm(t) for t in bufs])
    return [(res[2 * n + a], res[a], res[n + a]) for a in range(n)], res[3 * n]


def _gather_wait(state, after, *, name):
    buf, send, recv = state

    def body(buf_ref, send_ref, recv_ref, after_ref, out_ref):
        for cp in _gather_copies(buf_ref, send_ref, recv_ref):
            cp.wait_send()
            cp.wait_recv()

    return pl.pallas_call(
        body, name=name, out_shape=pltpu.HBM(buf.shape, buf.dtype), in_specs=[HBM, SEM, SEM, ANY], out_specs=HBM,
        input_output_aliases={0: 0}, compiler_params=pltpu.CompilerParams(has_side_effects=EFFECT),
    )(buf, send, recv, after)


def _forward_sibling(buf, *, name):
    def body(_in, out, send, recv, _local):
        x, y, c, others = _place()
        half = out.shape[1] // 2
        cps = []
        for j, (px, py) in enumerate(others):
            got = out.at[2 * px + py, pl.ds(c * half, half)]
            cps.append(_remote(got, got, send.at[j], recv.at[j], (x, y, 1 - c)))
        for cp in cps:
            cp.start()
        for j, (px, py) in enumerate(others):
            cps[j].wait_send()
            theirs = out.at[2 * px + py, pl.ds((1 - c) * half, half)]
            _remote(theirs, theirs, send.at[j], recv.at[j], (x, y, 1 - c)).wait_recv()

    return _hbm_call(body, name, [buf], [jax.ShapeDtypeStruct(buf.shape, buf.dtype)], 3, 0, aliases={0: 0})[0]


def _chip_copies(src, land, send, recv):
    _x, _y, c, others = _place()
    return [_remote(src.at[2 * px + py], land.at[j], send.at[j], recv.at[j], (px, py, c)) for j, (px, py) in enumerate(others)]


def _chip_start(partial, *, name):
    def body(src, land, send, recv, _src_thru, _land_thru, token):
        for cp in _chip_copies(src, land, send, recv):
            cp.start()
        token[...] = jnp.zeros_like(token)

    land_shape = (3,) + partial.shape[1:]
    sem = pltpu.SemaphoreType.DMA((3,))
    send, recv, src, land, token = pl.pallas_call(
        body, name=name, out_shape=[sem, sem, pltpu.HBM(partial.shape, partial.dtype), pltpu.HBM(land_shape, partial.dtype), TOKEN],
        in_specs=[HBM, HBM], out_specs=[SEM, SEM, HBM, HBM, pl.BlockSpec(memory_space=pltpu.VMEM)],
        input_output_aliases={0: 2, 1: 3}, compiler_params=pltpu.CompilerParams(has_side_effects=EFFECT),
    )(_in_hbm(partial), _in_hbm(lax.empty(land_shape, partial.dtype)))
    return (src, land, send, recv), token


def _chip_wait(state, after, *, name):
    src, land, send, recv = state

    def body(src_ref, land_ref, send_ref, recv_ref, after_ref, _src_out, _land_out):
        for cp in _chip_copies(src_ref, land_ref, send_ref, recv_ref):
            cp.wait_send()
            cp.wait_recv()

    return pl.pallas_call(
        body, name=name, out_shape=[pltpu.HBM(src.shape, src.dtype), pltpu.HBM(land.shape, land.dtype)],
        in_specs=[HBM, HBM, SEM, SEM, ANY], out_specs=[HBM, HBM], input_output_aliases={0: 0, 1: 1},
        compiler_params=pltpu.CompilerParams(has_side_effects=EFFECT),
    )(src, land, send, recv, after)[1]


def _share_halves(bufs, *, name):
    n = len(bufs)

    def body(*refs):
        outs = refs[n:2 * n]
        send, recv, _ = refs[2 * n:]
        x, y, c, _o = _place()
        cps = [_remote(outs[a].at[:, c], outs[a].at[:, c], send.at[a], recv.at[a], (x, y, 1 - c)) for a in range(n)]
        for cp in cps:
            cp.start()
        for a in range(n):
            cps[a].wait_send()
            theirs = outs[a].at[:, 1 - c]
            _remote(theirs, theirs, send.at[a], recv.at[a], (x, y, 1 - c)).wait_recv()

    out_shapes = [jax.ShapeDtypeStruct(t.shape, t.dtype) for t in bufs]
    return _hbm_call(body, name, list(bufs), out_shapes, n, 0, aliases={a: a for a in range(n)})


def _all_gather_small(pack, *, name):
    def body(src, out, send, recv, local):
        x, y, c, others = _place()

        def slot(px, py, pc):
            return out.at[4 * px + 2 * py + pc]

        mine = pltpu.make_async_copy(src, slot(x, y, c), local.at[0])
        mine.start()
        first = [_remote(src, slot(x, y, c), send.at[0], recv.at[0], (x, y, 1 - c))]
        first += [_remote(src, slot(x, y, c), send.at[1 + j], recv.at[1 + j], (px, py, c)) for j, (px, py) in enumerate(others)]
        for cp in first:
            cp.start()
        passed = []
        for j, (px, py) in enumerate(others):
            got = slot(px, py, c)
            _remote(got, got, send.at[1 + j], recv.at[1 + j], (px, py, c)).wait_recv()
            fwd = _remote(got, got, send.at[4 + j], recv.at[4 + j], (x, y, 1 - c))
            fwd.start()
            passed.append(fwd)
        theirs = slot(x, y, 1 - c)
        _remote(theirs, theirs, send.at[0], recv.at[0], (x, y, 1 - c)).wait_recv()
        for j, (px, py) in enumerate(others):
            got = slot(px, py, 1 - c)
            _remote(got, got, send.at[4 + j], recv.at[4 + j], (x, y, 1 - c)).wait_recv()
        for cp in first + passed:
            cp.wait_send()
        mine.wait()

    return _hbm_call(body, name, [pack], [jax.ShapeDtypeStruct((N_DEV,) + pack.shape, pack.dtype)], 7, 1)[0]


def _row_tile(rows, cols):
    return next(t for t in (512, 256, 128, 64, 32, 16) if rows % t == 0 and (t * cols * 4 <= (1 << 20) or t == 16))


def _pair_sum(grad, theirs, place, *, name):
    _, r, cols = grad.shape
    r2 = r // 2
    tr = _row_tile(r2, cols)
    nr = r2 // tr

    def body(place_ref, g_ref, t_ref, all_ref):
        all_ref[...] = (g_ref[...].astype(F32) + t_ref[...].astype(F32)).astype(all_ref.dtype)

    grid_spec = pltpu.PrefetchScalarGridSpec(
        num_scalar_prefetch=1, grid=(N_CHIPS, nr),
        in_specs=[pl.BlockSpec((None, tr, cols), lambda k, i, pr: (k, pr[0] * nr + i, 0)),
                  pl.BlockSpec((None, tr, cols), lambda k, i, pr: (k, i, 0))],
        out_specs=pl.BlockSpec((None, tr, cols), lambda k, i, pr: (k, i, 0)),
    )
    return pl.pallas_call(
        body, name=name, grid_spec=grid_spec, out_shape=jax.ShapeDtypeStruct((N_CHIPS, r2, cols), BF16),
        compiler_params=_params(("parallel", "parallel")),
    )(place, grad, theirs)


def _chip_sum(grad, theirs, got, place, buf, layer, depth, *, name):
    _, r, cols = grad.shape
    r2 = r // 2
    tr = _row_tile(r2, cols)
    nr = r2 // tr

    def body(place_ref, g_ref, t_ref, got_ref, *rest):
        own = g_ref[...].astype(F32) + t_ref[...].astype(F32)
        rest[-1][...] = ((own + got_ref[0].astype(F32)) + got_ref[1].astype(F32)) + got_ref[2].astype(F32)

    in_specs = [pl.BlockSpec((None, tr, cols), lambda i, pr: (pr[1], pr[0] * nr + i, 0)),
                pl.BlockSpec((None, tr, cols), lambda i, pr: (pr[1], i, 0)),
                pl.BlockSpec((3, tr, cols), lambda i, pr: (0, i, 0))]
    args = [place, grad, theirs, got]
    if buf is not None:
        in_specs.append(ANY)
        args.append(buf)
    grid_spec = pltpu.PrefetchScalarGridSpec(
        num_scalar_prefetch=1, grid=(nr,), in_specs=in_specs,
        out_specs=pl.BlockSpec((None, None, tr, cols), lambda i, pr: (layer, pr[0], i, 0)),
    )
    return pl.pallas_call(
        body, name=name, grid_spec=grid_spec, out_shape=jax.ShapeDtypeStruct((depth, 2, r2, cols), F32),
        input_output_aliases={} if buf is None else {4: 0}, compiler_params=_params(("parallel",)),
    )(*args)


def _sum_devices(parts, *, name):
    _, rows, cols = parts.shape
    tr = _pick(rows, (256, 128, 64, 32, 16, 8))

    def body(p_ref, out_ref):
        acc = p_ref[0]
        for d in range(1, N_DEV):
            acc = acc + p_ref[d]
        out_ref[...] = acc

    return pl.pallas_call(
        body, name=name, grid=(rows // tr,), in_specs=[pl.BlockSpec((N_DEV, tr, cols), lambda i: (0, i, 0))],
        out_specs=pl.BlockSpec((tr, cols), lambda i: (i, 0)), out_shape=jax.ShapeDtypeStruct((rows, cols), F32),
        compiler_params=_params(("parallel",)),
    )(parts)


def _pack(parts):
    rows = []
    for t in parts:
        flat = t.reshape(-1, 128)
        pad = (-flat.shape[0]) % 8
        rows.append(jnp.pad(flat, ((0, pad), (0, 0))) if pad else flat)
    return jnp.concatenate(rows, axis=0)


def _unpack(pack, shapes):
    out, r0 = [], 0
    for shp in shapes:
        n = math.prod(shp) // 128
        out.append(pack[r0:r0 + n].reshape(shp))
        r0 += n + (-n) % 8
    return out


SMALL = ["attn_norm_g", "q_norm_g", "k_norm_g", "sgu_norm_g", "sgu_w", "sgu_b", "out_norm_a_g", "out_norm_b_g",
         "ffn_norm_g", "conv_b"]
BIG = ["w_in", "w_out", "w_up", "w_down"]
ORDER = ["attn_norm_g", "w_in", "q_norm_g", "k_norm_g", "sgu_norm_g", "sgu_w", "sgu_b", "out_norm_a_g", "out_norm_b_g",
         "w_out", "ffn_norm_g", "w_up", "conv_w", "conv_b", "w_down"]


def kernel(x, attn_norm_g, w_in, q_norm_g, k_norm_g, sgu_norm_g, sgu_w, sgu_b, out_norm_a_g, out_norm_b_g, w_out, ffn_norm_g, w_up, conv_w, conv_b, w_down, loss_target, m_attn_norm_g, m_w_in, m_q_norm_g, m_k_norm_g, m_sgu_norm_g, m_sgu_w, m_sgu_b, m_out_norm_a_g, m_out_norm_b_g, m_w_out, m_ffn_norm_g, m_w_up, m_conv_w, m_conv_b, m_w_down, v_attn_norm_g, v_w_in, v_q_norm_g, v_k_norm_g, v_sgu_norm_g, v_sgu_w, v_sgu_b, v_out_norm_a_g, v_out_norm_b_g, v_w_out, v_ffn_norm_g, v_w_up, v_conv_w, v_conv_b, v_w_down):
    W = dict(attn_norm_g=attn_norm_g, w_in=w_in, q_norm_g=q_norm_g, k_norm_g=k_norm_g, sgu_norm_g=sgu_norm_g, sgu_w=sgu_w,
             sgu_b=sgu_b, out_norm_a_g=out_norm_a_g, out_norm_b_g=out_norm_b_g, w_out=w_out, ffn_norm_g=ffn_norm_g, w_up=w_up,
             conv_w=conv_w, conv_b=conv_b, w_down=w_down)
    M = dict(attn_norm_g=m_attn_norm_g, w_in=m_w_in, q_norm_g=m_q_norm_g, k_norm_g=m_k_norm_g, sgu_norm_g=m_sgu_norm_g,
             sgu_w=m_sgu_w, sgu_b=m_sgu_b, out_norm_a_g=m_out_norm_a_g, out_norm_b_g=m_out_norm_b_g, w_out=m_w_out,
             ffn_norm_g=m_ffn_norm_g, w_up=m_w_up, conv_w=m_conv_w, conv_b=m_conv_b, w_down=m_w_down)
    V = dict(attn_norm_g=v_attn_norm_g, w_in=v_w_in, q_norm_g=v_q_norm_g, k_norm_g=v_k_norm_g, sgu_norm_g=v_sgu_norm_g,
             sgu_w=v_sgu_w, sgu_b=v_sgu_b, out_norm_a_g=v_out_norm_a_g, out_norm_b_g=v_out_norm_b_g, w_out=v_w_out,
             ffn_norm_g=v_ffn_norm_g, w_up=v_w_up, conv_w=v_conv_w, conv_b=v_conv_b, w_down=v_w_down)
    depth = w_in.shape[0]
    s, d = x.shape[1], x.shape[2]
    n_heads = out_norm_a_g.shape[1]
    core = lax.axis_index("c")
    chip = 2 * lax.axis_index("x") + lax.axis_index("y")
    place = jnp.stack([core, chip]).astype(jnp.int32)
    xs = x.reshape(s, d)

    f_local = conv_w.shape[2]
    taps = lax.dynamic_update_slice(jnp.zeros((N_CHIPS, 16, f_local), F32), conv_w.reshape(1, depth * 3, f_local),
                                    (chip, 0, 0))
    order = [(l, n) for l in range(depth) for n in BIG]
    first, token = _gather_start([_cast_into(w_in, 0, place, name="cast_w_in"), taps], name="gather_start_first")
    rest, token = _gather_start([_cast_into(W[n], l, place, dep=token, name=f"cast_{n}") for l, n in order[1:]],
                                name="gather_start_rest")
    states = dict(zip(order, [first[0]] + rest))

    def gathered(l, n, after):
        return _forward_sibling(_gather_wait(states[(l, n)], after, name=f"gather_wait_{n}"), name=f"forward_{n}")

    saved, full = [], []
    cur = xs
    for l in range(depth):
        gain = attn_norm_g[l] + token[0, 0] if l == 0 else attn_norm_g[l]
        h = _rmsnorm_fwd(cur, gain, name="attn_norm")
        w_in_l = gathered(l, "w_in", h)
        if l == 0:
            taps = _forward_sibling(_gather_wait(first[1], w_in_l, name="gather_wait_taps"), name="forward_taps")
            taps = taps[:, :depth * 3].reshape(N_CHIPS, depth, 3, f_local)
            cw_full = jnp.transpose(taps, (1, 2, 0, 3)).reshape(depth, 3, N_CHIPS * f_local)
        p = _mm(h, w_in_l, "nn", b_split=N_CHIPS, caps=(2048, 256, 2048), name="proj_in")
        att, o_raw, lsum = _attn_fwd(p, q_norm_g[l], k_norm_g[l], out_norm_a_g[l], n_heads, name="attn_fwd")
        sg = _sgu_fwd(p, sgu_w[l], sgu_b[l], sgu_norm_g[l], out_norm_b_g[l], n_heads, name="sgu_fwd")
        mix = jnp.stack([att, sg])
        w_out_l = gathered(l, "w_out", mix).reshape(-1, d)
        x1 = _mm(mix, w_out_l, "nn", a_split=2, res=cur, caps=(2048, 512, 1024), name="proj_out")
        h2 = _rmsnorm_fwd(x1, ffn_norm_g[l], name="ffn_norm")
        w_up_l = gathered(l, "w_up", h2)
        up = _mm(h2, w_up_l, "nn", b_split=N_CHIPS, caps=(2048, 256, 2048), name="ffn_up")
        act = _conv_fwd(up, cw_full[l], conv_b[l], name="conv_fwd")
        w_down_l = gathered(l, "w_down", act).reshape(-1, d)
        x2 = _mm(act, w_down_l, "nn", res=x1, caps=(2048, 512, 512), name="ffn_down")
        full.append(dict(w_in=w_in_l, w_out=w_out_l, w_up=w_up_l, w_down=w_down_l, conv_w=cw_full[l]))
        saved.append(dict(x0=cur, h=h, p=p, o_raw=o_raw, lsum=lsum, mix=mix, x1=x1, h2=h2, up=up, act=act))
        cur = x2

    dx, dxb, sq = _loss_head(cur, loss_target.reshape(s, d), name="loss_head")
    loss = lax.psum(sq[0, 0] * (0.5 / d), ("x", "y", "c"))

    small_grads = {n: [None] * depth for n in SMALL + ["conv_w"]}
    def reduce_begin(n, grad):
        theirs = _pair_exchange([grad], name=f"pair_exchange_{n}")[0]
        state, tok = _chip_start(_pair_sum(grad, theirs, place, name=f"pair_sum_{n}"), name=f"chip_start_{n}")
        return (grad, theirs, state), tok

    pending = {}
    for l in reversed(range(depth)):
        fw, sv = full[l], saved[l]
        dact = _mm(dxb, fw["w_down"], "nt", caps=(2048, 512, 2048), name="d_act")
        g_down = _mm(sv["act"], dxb, "tn", caps=(512, 2048, 2048), out_dtype=BF16, name="g_down")
        pending[(l, "w_down")], tok = reduce_begin("w_down", g_down.reshape(N_CHIPS, -1, d))
        dup, dcw, dcb = _conv_bwd(sv["up"], dact, fw["conv_w"], conv_b[l] + tok[0, 0], name="conv_bwd")
        g_up = _mm(sv["h2"], dup, "tn", b_split=2, o_split=N_CHIPS, caps=(2048, 256, 2048), out_dtype=BF16, name="g_up")
        pending[(l, "w_up")], tok = reduce_begin("w_up", g_up)
        dh2 = _mm(dup, fw["w_up"], "nt", a_split=2, b_split=N_CHIPS, caps=(1024, 512, 2816), dep=tok, name="d_h2")
        dx1, dx1b, dg_ffn = _rmsnorm_bwd(sv["x1"], ffn_norm_g[l], dh2, dx, name="ffn_norm_bwd")
        dmix = _mm(dx1b, fw["w_out"], "nt", caps=(2048, 512, 2048), name="d_mix")
        g_out = _mm(sv["mix"], dx1b, "tn", a_split=2, caps=(512, 2048, 2048), out_dtype=BF16, name="g_out")
        pending[(l, "w_out")], tok = reduce_begin("w_out", g_out.reshape(N_CHIPS, -1, d))
        dqkv, dgq, dgk, dgoa = _attn_bwd(sv["p"], sv["o_raw"], sv["lsum"], dmix, q_norm_g[l] + tok[0, 0], k_norm_g[l],
                                         out_norm_a_g[l], n_heads, name="attn_bwd")
        duv, dsw, dsb, dgv, dgob = _sgu_bwd(sv["p"], dmix, sgu_w[l], sgu_b[l], sgu_norm_g[l], out_norm_b_g[l], n_heads,
                                            name="sgu_bwd")
        dp = jnp.concatenate([dqkv[0], dqkv[1], dqkv[2], duv[0], duv[1]], axis=1)
        g_in = _mm(sv["h"], dp, "tn", o_split=N_CHIPS, caps=(2048, 256, 2048), out_dtype=BF16, name="g_in")
        pending[(l, "w_in")], tok = reduce_begin("w_in", g_in)
        dh = _mm(dp, fw["w_in"], "nt", b_split=N_CHIPS, caps=(2048, 512, 1280), dep=tok, name="d_h")
        dx, dxb, dg_attn = _rmsnorm_bwd(sv["x0"], attn_norm_g[l], dh, dx1, name="attn_norm_bwd")

        small_grads["attn_norm_g"][l] = dg_attn.reshape(d)
        small_grads["q_norm_g"][l] = jnp.sum(dgq, axis=(0, 1))
        small_grads["k_norm_g"][l] = jnp.sum(dgk, axis=(0, 1))
        small_grads["sgu_norm_g"][l] = dgv.reshape(-1, BLK)
        small_grads["sgu_w"][l] = dsw
        small_grads["sgu_b"][l] = dsb.reshape(-1, BLK)
        small_grads["out_norm_a_g"][l] = dgoa.reshape(-1, BLK)
        small_grads["out_norm_b_g"][l] = dgob.reshape(-1, BLK)
        small_grads["ffn_norm_g"][l] = dg_ffn.reshape(d)
        small_grads["conv_b"][l] = dcb[:, 0, :].reshape(-1)
        small_grads["conv_w"][l] = jnp.transpose(dcw[:, :3, :], (1, 0, 2)).reshape(3, -1)

    G, D_, NM, NV = {}, {}, {}, {}
    after = dx
    for n in ("w_down", "w_up", "w_out", "w_in"):
        buf = None
        for l in reversed(range(depth)):
            grad, theirs, state = pending[(l, n)]
            got = _chip_wait(state, after, name=f"chip_wait_{n}")
            buf = _chip_sum(grad, theirs, got, place, buf, l, depth, name=f"chip_sum_{n}")
            after = buf
        G[n] = _share_halves([buf], name=f"share_halves_{n}")[0].reshape(W[n].shape)
        D_[n], NM[n], NV[n] = _adamw(W[n], G[n], M[n], V[n], name=f"adamw_{n}")
        after = NV[n]

    names = SMALL + ["conv_w"]
    pack = _pack([jnp.stack(small_grads[n]) for n in names])
    total = _sum_devices(_all_gather_small(pack, name="gather_small"), name="sum_small")
    f_full = conv_b.shape[1]
    shapes = [W[n].shape for n in SMALL] + [(depth, 3, f_full)]
    for n, t in zip(names, _unpack(total, shapes)):
        G[n] = t
    G["conv_w"] = lax.dynamic_slice_in_dim(G["conv_w"], chip * f_local, f_local, axis=2)

    D_["conv_w"], NM["conv_w"], NV["conv_w"] = _adamw(conv_w, G["conv_w"], m_conv_w, v_conv_w, name="adamw_conv_w")
    small_shapes = [W[n].shape for n in SMALL]
    res = _adamw(_pack([W[n] for n in SMALL]), _pack([G[n] for n in SMALL]), _pack([M[n] for n in SMALL]),
                 _pack([V[n] for n in SMALL]), name="adamw_small")
    for dst, t in zip((D_, NM, NV), res):
        for n, u in zip(SMALL, _unpack(t, small_shapes)):
            dst[n] = u

    return (loss, dx.reshape(x.shape), *[G[n] for n in ORDER], *[D_[n] for n in ORDER], *[NM[n] for n in ORDER],
            *[NV[n] for n in ORDER])
```

```python
import functools
import math

import jax
import jax.numpy as jnp
from jax import lax
from jax.experimental import pallas as pl
from jax.experimental.pallas import tpu as pltpu

F32 = jnp.float32
BF16 = jnp.bfloat16
EPS = 1e-6
BLK = 128
N_CHIPS = 4
N_DEV = 8
ADAM_LR, ADAM_B1, ADAM_B2, ADAM_EPS, ADAM_WD, ADAM_STEP = 0.001, 0.9, 0.999, 1e-08, 0.01, 10
VMEM_BIG = 48 * 1024 * 1024
MESH = pl.DeviceIdType.MESH
ANY = pl.BlockSpec(memory_space=pl.ANY)


def _pick(dim, prefs):
    for t in prefs:
        if dim % t == 0:
            return t
    raise ValueError(f"no tile in {prefs} divides {dim}")


def _params(sem=None, vmem=None):
    return pltpu.CompilerParams(dimension_semantics=sem, vmem_limit_bytes=vmem)


def _ldims(arr, split):
    if split == 1:
        return arr.shape
    p, r, cs = arr.shape
    assert p == split
    return (r, p * cs)


def _spec(tr, tc, split, cols, rc):
    if split == 1:
        return pl.BlockSpec((tr, tc), lambda i, j, k: rc(i, j, k))
    per = (cols // split) // tc

    def imap(i, j, k):
        r, c = rc(i, j, k)
        return (c // per, r, c % per)

    return pl.BlockSpec((None, tr, tc), imap)


def _fit(unit, cap):
    return max(t for t in range(128, min(unit, cap) + 1, 128) if unit % t == 0)


def _mm(a, b, mode, *, name, caps, a_split=1, b_split=1, o_split=1, out_dtype=F32, res=None, dep=None):
    ar, ac = _ldims(a, a_split)
    br, bc = _ldims(b, b_split)
    if mode == "nn":
        m, k, n = ar, ac, bc
        assert br == k
        ku, nu, mu = math.gcd(k // a_split, k), math.gcd(n // b_split, n // o_split), m
    elif mode == "nt":
        m, k, n = ar, ac, br
        assert bc == k
        ku, nu, mu = math.gcd(k // a_split, k // b_split), n // o_split, m
    else:
        k, m, n = ar, ac, bc
        assert br == k
        ku, nu, mu = k, math.gcd(n // b_split, n // o_split), m // a_split
    tm, tn, tk = _fit(mu, caps[0]), _fit(nu, caps[1]), _fit(ku, caps[2])
    nk = k // tk
    if mode == "nn":
        a_spec = _spec(tm, tk, a_split, k, lambda i, j, kk: (i, kk))
        b_spec = _spec(tk, tn, b_split, n, lambda i, j, kk: (kk, j))
    elif mode == "nt":
        a_spec = _spec(tm, tk, a_split, k, lambda i, j, kk: (i, kk))
        b_spec = _spec(tn, tk, b_split, k, lambda i, j, kk: (j, kk))
    else:
        a_spec = _spec(tk, tm, a_split, m, lambda i, j, kk: (kk, i))
        b_spec = _spec(tk, tn, b_split, n, lambda i, j, kk: (kk, j))
    o_spec = _spec(tm, tn, o_split, n, lambda i, j, kk: (i, j))
    dims = {"nn": (((1,), (0,)), ((), ())), "nt": (((1,), (1,)), ((), ())), "tn": (((0,), (0,)), ((), ()))}[mode]

    def body(a_ref, b_ref, *rest):
        if dep is not None:
            rest = rest[1:]
        if res is None:
            o_ref, acc = rest
        else:
            r_ref, o_ref, acc = rest
        kk = pl.program_id(2)

        @pl.when(kk == 0)
        def _():
            acc[...] = jnp.zeros_like(acc)

        acc[...] += lax.dot_general(a_ref[...].astype(BF16), b_ref[...].astype(BF16), dims, preferred_element_type=F32)

        @pl.when(kk == nk - 1)
        def _():
            out = acc[...]
            if res is not None:
                out = out + r_ref[...]
            o_ref[...] = out.astype(o_ref.dtype)

    in_specs, args = [a_spec, b_spec], [a, b]
    if dep is not None:
        in_specs.append(ANY)
        args.append(dep)
    if res is not None:
        in_specs.append(pl.BlockSpec((tm, tn), lambda i, j, kk: (i, j)))
        args.append(res)
    out_shape = (m, n) if o_split == 1 else (o_split, m, n // o_split)
    return pl.pallas_call(
        body, name=name, grid=(m // tm, n // tn, nk), in_specs=in_specs, out_specs=o_spec,
        out_shape=jax.ShapeDtypeStruct(out_shape, out_dtype), scratch_shapes=[pltpu.VMEM((tm, tn), F32)],
        compiler_params=_params(("parallel", "parallel", "arbitrary"), VMEM_BIG),
    )(*args)


def _rstd(v):
    return lax.rsqrt(jnp.mean(v * v, axis=-1, keepdims=True) + EPS)


def _norm_bwd(v, r, gain, dout):
    a = dout * gain
    dv = r * (a - v * (r * r * jnp.mean(a * v, axis=-1, keepdims=True)))
    return dv, dout * v * r


def _rmsnorm_fwd(x, g, *, name):
    s, d = x.shape
    tr = _pick(s, (256, 128))

    def body(x_ref, g_ref, o_ref):
        v = x_ref[...]
        o_ref[...] = (v * _rstd(v) * g_ref[...]).astype(o_ref.dtype)

    return pl.pallas_call(
        body, name=name, grid=(s // tr,),
        in_specs=[pl.BlockSpec((tr, d), lambda i: (i, 0)), pl.BlockSpec((1, d), lambda i: (0, 0))],
        out_specs=pl.BlockSpec((tr, d), lambda i: (i, 0)), out_shape=jax.ShapeDtypeStruct((s, d), BF16),
        compiler_params=_params(("parallel",)),
    )(x, g.reshape(1, d))


def _rmsnorm_bwd(x, g, dh, dres, *, name):
    s, d = x.shape
    tr = _pick(s, (256, 128))

    def body(x_ref, g_ref, dh_ref, dres_ref, dx_ref, dxb_ref, dg_ref):
        v = x_ref[...]
        dv, dgr = _norm_bwd(v, _rstd(v), g_ref[...], dh_ref[...])
        dx = dres_ref[...] + dv
        dx_ref[...] = dx
        dxb_ref[...] = dx.astype(BF16)
        part = jnp.sum(dgr, axis=0, keepdims=True)

        @pl.when(pl.program_id(0) == 0)
        def _():
            dg_ref[...] = part

        @pl.when(pl.program_id(0) > 0)
        def _():
            dg_ref[...] += part

    row = pl.BlockSpec((tr, d), lambda i: (i, 0))
    one = pl.BlockSpec((1, d), lambda i: (0, 0))
    return pl.pallas_call(
        body, name=name, grid=(s // tr,), in_specs=[row, one, row, row], out_specs=[row, row, one],
        out_shape=[jax.ShapeDtypeStruct((s, d), F32), jax.ShapeDtypeStruct((s, d), BF16), jax.ShapeDtypeStruct((1, d), F32)],
        compiler_params=_params(("arbitrary",)),
    )(x, g.reshape(1, d), dh, dres)


def _loss_head(y, target, *, name):
    s, d = y.shape
    tr = _pick(s, (256, 128))

    def body(y_ref, t_ref, dy_ref, dyb_ref, ls_ref):
        e = y_ref[...] - t_ref[...]
        dy = e * (1.0 / d)
        dy_ref[...] = dy
        dyb_ref[...] = dy.astype(BF16)
        part = jnp.full(ls_ref.shape, jnp.sum(e * e), F32)

        @pl.when(pl.program_id(0) == 0)
        def _():
            ls_ref[...] = part

        @pl.when(pl.program_id(0) > 0)
        def _():
            ls_ref[...] += part

    row = pl.BlockSpec((tr, d), lambda i: (i, 0))
    return pl.pallas_call(
        body, name=name, grid=(s // tr,), in_specs=[row, row], out_specs=[row, row, pl.BlockSpec((8, 128), lambda i: (0, 0))],
        out_shape=[jax.ShapeDtypeStruct((s, d), F32), jax.ShapeDtypeStruct((s, d), BF16), jax.ShapeDtypeStruct((8, 128), F32)],
        compiler_params=_params(("arbitrary",)),
    )(y, target)


def _iota2(axis):
    return lax.broadcasted_iota(jnp.int32, (BLK, BLK), axis)


def _tri_sum(v, tri):
    hi = v.astype(BF16)
    lo = (v - hi.astype(F32)).astype(BF16)
    return jnp.dot(hi, tri, preferred_element_type=F32) + jnp.dot(lo, tri, preferred_element_type=F32)


def _dot_nt(a, b):
    return lax.dot_general(a, b, (((1,), (1,)), ((), ())), preferred_element_type=F32)


def _dot_tn(a, b):
    return lax.dot_general(a, b, (((0,), (0,)), ((), ())), preferred_element_type=F32)


TQ_MAX = 512
HP = 2
VMEM_ATTN_BWD = 56 * 1024 * 1024


def _lanes(hh):
    return slice(hh * BLK, (hh + 1) * BLK)


def _causal(n, diag):
    if not diag:
        return None
    return lax.broadcasted_iota(jnp.int32, (n, BLK), 1) < lax.broadcasted_iota(jnp.int32, (n, BLK), 0)


def _sb_sums(z, mask, rhs_gt):
    lb = jnp.minimum(z, 0.0) - jnp.log(1.0 + jnp.exp(-jnp.abs(z)))
    l1m = lb - z
    if mask is not None:
        l1m = jnp.where(mask, l1m, 0.0)
    return lb, _tri_sum(l1m, rhs_gt)


def _below(old, new, r0):
    return new if r0 == 0 else jnp.concatenate([old[:r0], new], axis=0)


def _attn_fwd(p, gq, gk, go, n_heads, *, name):
    s = p.shape[0]
    tq = min(TQ_MAX, s)
    per = tq // BLK
    scale = BLK ** -0.5

    def body(q_ref, k_ref, v_ref, gq_ref, gk_ref, go_ref, att_ref, o_ref, l_ref, qn, kn, vb):
        for hh in range(HP):
            q = q_ref[:, _lanes(hh)]
            k = k_ref[:, _lanes(hh)]
            qn[:, _lanes(hh)] = (q * _rstd(q) * gq_ref[...]).astype(BF16)
            kn[:, _lanes(hh)] = (k * _rstd(k) * gk_ref[...]).astype(BF16)
            vb[:, _lanes(hh)] = v_ref[:, _lanes(hh)].astype(BF16)
        rhs_gt = jnp.concatenate([(_iota2(0) > _iota2(1)).astype(BF16), jnp.ones((BLK, BLK), BF16)], axis=1)

        def step(q0, j, r0, diag, states):
            n = tq - r0
            rows = pl.ds(pl.multiple_of(q0 + r0, BLK), n)
            cols = pl.ds(pl.multiple_of(j * BLK, BLK), BLK)
            mask = _causal(n, diag)
            zs = [_dot_nt(qn[rows, _lanes(hh)], kn[cols, _lanes(hh)]) * scale for hh in range(HP)]
            sums = [_sb_sums(z, mask, rhs_gt) for z in zs]
            new = []
            for hh in range(HP):
                acc, later = states[hh]
                lb, both = sums[hh]
                a = jnp.exp(lb + both[:, :BLK] + later[r0:])
                if diag:
                    a = jnp.where(mask, a, 0.0)
                acc_new = acc[r0:] + jnp.dot(a.astype(BF16), vb[cols, _lanes(hh)], preferred_element_type=F32)
                new.append((_below(acc, acc_new, r0), _below(later, later[r0:] + both[:, BLK:], r0)))
            return tuple(new)

        def q_block(i, _):
            q0 = i * tq
            zero = jnp.zeros((tq, BLK), F32)
            states = ((zero, zero),) * HP
            for jd in reversed(range(per)):
                states = step(q0, i * per + jd, jd * BLK, True, states)
            states = lax.fori_loop(0, i * per, lambda jj, st: step(q0, i * per - 1 - jj, 0, False, st), states)
            tile = pl.ds(pl.multiple_of(q0, tq), tq)
            for hh in range(HP):
                o, total = states[hh]
                o_ref[tile, _lanes(hh)] = o
                l_ref[hh, tile, :] = total
                att_ref[tile, _lanes(hh)] = (o * _rstd(o) * go_ref[hh]).astype(att_ref.dtype)
            return 0

        lax.fori_loop(0, s // tq, q_block, 0)

    assert n_heads % HP == 0
    groups = n_heads // HP

    def col(part):
        return pl.BlockSpec((s, HP * BLK), lambda g: (0, part * groups + g))

    gain = pl.BlockSpec((1, BLK), lambda g: (0, 0))
    per_head = pl.BlockSpec((HP, 1, BLK), lambda g: (g, 0, 0))
    return pl.pallas_call(
        body, name=name, grid=(groups,),
        in_specs=[col(0), col(1), col(2), gain, gain, per_head],
        out_specs=[pl.BlockSpec((None, s, HP * BLK), lambda g: (0, 0, g)), col(0), pl.BlockSpec((HP, s, BLK), lambda g: (g, 0, 0))],
        out_shape=[jax.ShapeDtypeStruct((2, s, n_heads * BLK), BF16), jax.ShapeDtypeStruct((s, n_heads * BLK), F32),
                   jax.ShapeDtypeStruct((n_heads, s, BLK), F32)],
        scratch_shapes=[pltpu.VMEM((s, HP * BLK), BF16)] * 3,
        compiler_params=_params(("parallel",), VMEM_BIG),
    )(p, p, p, gq.reshape(1, BLK), gk.reshape(1, BLK), go.reshape(n_heads, 1, BLK))


def _attn_bwd(p, o_raw, lsum, dmix, gq, gk, go, n_heads, *, name):
    s = p.shape[0]
    tq = min(TQ_MAX, s)
    per = tq // BLK
    scale = BLK ** -0.5

    def body(q_ref, k_ref, v_ref, o_ref, l_ref, da_ref, gq_ref, gk_ref, go_ref,
             dqkv_ref, dgq_ref, dgk_ref, dgo_ref, qn, kn, vb, dob, dqn, dkn, dvv):
        for hh in range(HP):
            q = q_ref[:, _lanes(hh)]
            k = k_ref[:, _lanes(hh)]
            qn[:, _lanes(hh)] = (q * _rstd(q) * gq_ref[...]).astype(BF16)
            kn[:, _lanes(hh)] = (k * _rstd(k) * gk_ref[...]).astype(BF16)
            vb[:, _lanes(hh)] = v_ref[:, _lanes(hh)].astype(BF16)
            o = o_ref[:, _lanes(hh)]
            do, dgo_rows = _norm_bwd(o, _rstd(o), go_ref[hh], da_ref[:, _lanes(hh)])
            dob[:, _lanes(hh)] = do.astype(BF16)
            dgo_ref[hh] = jnp.sum(dgo_rows, axis=0, keepdims=True)
        dkn[...] = jnp.zeros_like(dkn)
        dvv[...] = jnp.zeros_like(dvv)
        ones = jnp.ones((BLK, BLK), BF16)
        rhs_gt = jnp.concatenate([(_iota2(0) > _iota2(1)).astype(BF16), ones], axis=1)
        rhs_lt = jnp.concatenate([(_iota2(0) < _iota2(1)).astype(BF16), ones], axis=1)

        def step(q0, j, r0, diag, states):
            n = tq - r0
            rows = pl.ds(pl.multiple_of(q0 + r0, BLK), n)
            cols = pl.ds(pl.multiple_of(j * BLK, BLK), BLK)
            mask = _causal(n, diag)
            zs = [_dot_nt(qn[rows, _lanes(hh)], kn[cols, _lanes(hh)]) * scale for hh in range(HP)]
            das = [_dot_nt(dob[rows, _lanes(hh)], vb[cols, _lanes(hh)]) for hh in range(HP)]
            sums = [_sb_sums(z, mask, rhs_gt) for z in zs]
            mids = []
            for hh in range(HP):
                lb, both = sums[hh]
                upto = states[hh][0][r0:] + both[:, BLK:]
                a = jnp.exp(lb + both[:, :BLK] + (l_ref[hh, rows, :] - upto))
                if diag:
                    a = jnp.where(mask, a, 0.0)
                g = das[hh] * a
                mids.append((lb, upto, a, g, _tri_sum(g, rhs_lt)))
            new = []
            for hh in range(HP):
                seen, gsum, dq = states[hh]
                lb, upto, a, g, bothg = mids[hh]
                beta = jnp.exp(lb)
                dz = g * (1.0 - beta) - beta * (bothg[:, :BLK] + gsum[r0:])
                if diag:
                    dz = jnp.where(mask, dz, 0.0)
                dzs = (dz * scale).astype(BF16)
                dq_new = dq[r0:] + jnp.dot(dzs, kn[cols, _lanes(hh)], preferred_element_type=F32)
                dkn[cols, _lanes(hh)] += _dot_tn(dzs, qn[rows, _lanes(hh)])
                dvv[cols, _lanes(hh)] += _dot_tn(a.astype(BF16), dob[rows, _lanes(hh)])
                new.append((_below(seen, upto, r0), _below(gsum, gsum[r0:] + bothg[:, BLK:], r0), _below(dq, dq_new, r0)))
            return tuple(new)

        def q_block(i, _):
            q0 = i * tq
            zero = jnp.zeros((tq, BLK), F32)
            states = ((zero, zero, zero),) * HP
            states = lax.fori_loop(0, i * per, lambda j, st: step(q0, j, 0, False, st), states)
            for jd in range(per):
                states = step(q0, i * per + jd, jd * BLK, True, states)
            tile = pl.ds(pl.multiple_of(q0, tq), tq)
            for hh in range(HP):
                dqn[tile, _lanes(hh)] = states[hh][2]
            return 0

        lax.fori_loop(0, s // tq, q_block, 0)
        for hh in range(HP):
            q = q_ref[:, _lanes(hh)]
            k = k_ref[:, _lanes(hh)]
            dq_raw, dgq_rows = _norm_bwd(q, _rstd(q), gq_ref[...], dqn[:, _lanes(hh)])
            dk_raw, dgk_rows = _norm_bwd(k, _rstd(k), gk_ref[...], dkn[:, _lanes(hh)])
            dqkv_ref[0, :, _lanes(hh)] = dq_raw.astype(BF16)
            dqkv_ref[1, :, _lanes(hh)] = dk_raw.astype(BF16)
            dqkv_ref[2, :, _lanes(hh)] = dvv[:, _lanes(hh)].astype(BF16)
            dgq_ref[hh] = jnp.sum(dgq_rows, axis=0, keepdims=True)
            dgk_ref[hh] = jnp.sum(dgk_rows, axis=0, keepdims=True)

    assert n_heads % HP == 0
    groups = n_heads // HP

    def col(part):
        return pl.BlockSpec((s, HP * BLK), lambda g: (0, part * groups + g))

    gain = pl.BlockSpec((1, BLK), lambda g: (0, 0))
    per_head = pl.BlockSpec((HP, 1, BLK), lambda g: (g, 0, 0))
    head_gain = jax.ShapeDtypeStruct((n_heads, 1, BLK), F32)
    return pl.pallas_call(
        body, name=name, grid=(groups,),
        in_specs=[col(0), col(1), col(2), col(0), pl.BlockSpec((HP, s, BLK), lambda g: (g, 0, 0)), col(0),
                  gain, gain, per_head],
        out_specs=[pl.BlockSpec((3, s, HP * BLK), lambda g: (0, 0, g)), per_head, per_head, per_head],
        out_shape=[jax.ShapeDtypeStruct((3, s, n_heads * BLK), BF16), head_gain, head_gain, head_gain],
        scratch_shapes=[pltpu.VMEM((s, HP * BLK), BF16)] * 4 + [pltpu.VMEM((s, HP * BLK), F32)] * 3,
        compiler_params=_params(("parallel",), VMEM_ATTN_BWD),
    )(p, p, p, o_raw, lsum, dmix, gq.reshape(1, BLK), gk.reshape(1, BLK), go.reshape(n_heads, 1, BLK))


_INV_SQRT2 = 0.7071067811865476
_INV_SQRT2PI = 0.3989422804014327


def _gelu(x):
    return 0.5 * x * (1.0 + lax.erf(x * _INV_SQRT2))


def _gelu_grad(x):
    return 0.5 * (1.0 + lax.erf(x * _INV_SQRT2)) + x * jnp.exp(-0.5 * x * x) * _INV_SQRT2PI


def _sgu_fwd(p, mix, w, b, gv, gout, n_heads, *, name):
    s = p.shape[0]
    n_groups = w.shape[0]
    nb = s // BLK
    assert mix.shape == (2, s, n_groups * BLK)

    def body(u_ref, v_ref, w_ref, b_ref, gv_ref, go_ref, _mix_ref, out_ref):
        wt = jnp.where(_iota2(0) >= _iota2(1), w_ref[...], 0.0).astype(BF16)
        bias = b_ref[...]

        def chunk(c, _):
            rows = pl.ds(pl.multiple_of(c * BLK, BLK), BLK)
            u = _gelu(u_ref[rows, :])
            vv = _gelu(v_ref[rows, :])
            vs = vv * _rstd(vv) * gv_ref[...]
            gated = u * (jnp.dot(wt, vs.astype(BF16), preferred_element_type=F32) + bias)
            out_ref[rows, :] = (gated * _rstd(gated) * go_ref[...]).astype(out_ref.dtype)
            return 0

        lax.fori_loop(0, nb, chunk, 0)

    def col(off):
        return pl.BlockSpec((s, BLK), lambda g: (0, off + g))

    per_group = pl.BlockSpec((None, 1, BLK), lambda g: (g, 0, 0))
    return pl.pallas_call(
        body, name=name, grid=(n_groups,),
        in_specs=[col(3 * n_heads), col(3 * n_heads + n_groups), pl.BlockSpec((None, BLK, BLK), lambda g: (g, 0, 0)),
                  pl.BlockSpec((None, BLK, 1), lambda g: (g, 0, 0)), per_group, per_group, ANY],
        out_specs=pl.BlockSpec((None, s, BLK), lambda g: (1, 0, g)), out_shape=jax.ShapeDtypeStruct(mix.shape, mix.dtype),
        input_output_aliases={6: 0}, compiler_params=_params(("parallel",), VMEM_BIG),
    )(p, p, w, b.reshape(n_groups, BLK, 1), gv.reshape(n_groups, 1, BLK), gout.reshape(n_groups, 1, BLK), mix)


def _sgu_bwd(p, dmix, w, b, gv, gout, n_heads, *, name):
    s = p.shape[0]
    n_groups = w.shape[0]
    nb = s // BLK

    def body(u_ref, v_ref, ds_ref, w_ref, b_ref, gv_ref, go_ref, duv_ref, dw_ref, db_ref, dgv_ref, dgo_ref):
        lower = _iota2(0) >= _iota2(1)
        wt = jnp.where(lower, w_ref[...], 0.0).astype(BF16)
        bias = b_ref[...]

        def chunk(c, carry):
            dw, db, dgv, dgo = carry
            rows = pl.ds(pl.multiple_of(c * BLK, BLK), BLK)
            up = u_ref[rows, :]
            vp = v_ref[rows, :]
            u = _gelu(up)
            vv = _gelu(vp)
            rv = _rstd(vv)
            vsb = (vv * rv * gv_ref[...]).astype(BF16)
            mixed = jnp.dot(wt, vsb, preferred_element_type=F32) + bias
            gated = u * mixed
            dgated, dgo_rows = _norm_bwd(gated, _rstd(gated), go_ref[...], ds_ref[rows, :])
            dmixed = dgated * u
            dmb = dmixed.astype(BF16)
            dvs = _dot_tn(wt, dmb)
            dvv, dgv_rows = _norm_bwd(vv, rv, gv_ref[...], dvs)
            duv_ref[0, rows, :] = (dgated * mixed * _gelu_grad(up)).astype(BF16)
            duv_ref[1, rows, :] = (dvv * _gelu_grad(vp)).astype(BF16)
            return (dw + _dot_nt(dmb, vsb), db + jnp.sum(dmixed, axis=1, keepdims=True),
                    dgv + jnp.sum(dgv_rows, axis=0, keepdims=True), dgo + jnp.sum(dgo_rows, axis=0, keepdims=True))

        row0 = jnp.zeros((1, BLK), F32)
        dw, db, dgv, dgo = lax.fori_loop(0, nb, chunk, (jnp.zeros((BLK, BLK), F32), jnp.zeros((BLK, 1), F32), row0, row0))
        dw_ref[...] = jnp.where(lower, dw, 0.0)
        db_ref[...] = db
        dgv_ref[...] = dgv
        dgo_ref[...] = dgo

    def col(off):
        return pl.BlockSpec((s, BLK), lambda g: (0, off + g))

    per_group = pl.BlockSpec((None, 1, BLK), lambda g: (g, 0, 0))
    square = pl.BlockSpec((None, BLK, BLK), lambda g: (g, 0, 0))
    column = pl.BlockSpec((None, BLK, 1), lambda g: (g, 0, 0))
    gain = jax.ShapeDtypeStruct((n_groups, 1, BLK), F32)
    return pl.pallas_call(
        body, name=name, grid=(n_groups,),
        in_specs=[col(3 * n_heads), col(3 * n_heads + n_groups), col(n_heads), square, column, per_group, per_group],
        out_specs=[pl.BlockSpec((2, s, BLK), lambda g: (0, 0, g)), square, column, per_group, per_group],
        out_shape=[jax.ShapeDtypeStruct((2, s, n_groups * BLK), BF16), jax.ShapeDtypeStruct((n_groups, BLK, BLK), F32),
                   jax.ShapeDtypeStruct((n_groups, BLK, 1), F32), gain, gain],
        compiler_params=_params(("parallel",), VMEM_BIG),
    )(p, p, dmix, w, b.reshape(n_groups, BLK, 1), gv.reshape(n_groups, 1, BLK), gout.reshape(n_groups, 1, BLK))


CONV_ROWS = 256
HALO = 8


def _shift_down(ref, r0, n, first):
    cur = ref[pl.ds(r0, n), :]
    prev = jnp.zeros((HALO, cur.shape[1]), F32) if first else ref[pl.ds(r0 - HALO, HALO), :]
    ext = jnp.concatenate([prev, cur], axis=0)
    return pltpu.roll(ext, 1, 0)[HALO:], pltpu.roll(ext, 2, 0)[HALO:], cur


def _shift_up(ref, r0, n, last):
    cur = ref[pl.ds(r0, n), :]
    nxt = jnp.zeros((HALO, cur.shape[1]), F32) if last else ref[pl.ds(r0 + n, HALO), :]
    ext = jnp.concatenate([cur, nxt], axis=0)
    return cur, pltpu.roll(ext, n + HALO - 1, 0)[:n], pltpu.roll(ext, n + HALO - 2, 0)[:n]


def _conv_rows(x1, x2, x0, w_ref, b_ref):
    return ((b_ref[...] + x2 * w_ref[0:1, :]) + x1 * w_ref[1:2, :]) + x0 * w_ref[2:3, :]


def _conv_specs(s, f, tc):
    nf = f // tc
    gate = pl.BlockSpec((s, tc), lambda n: (0, n))
    val = pl.BlockSpec((s, tc), lambda n: (0, nf + n))
    wg = pl.BlockSpec((3, tc), lambda n: (0, n))
    wv = pl.BlockSpec((3, tc), lambda n: (0, nf + n))
    bg = pl.BlockSpec((1, tc), lambda n: (0, n))
    bv = pl.BlockSpec((1, tc), lambda n: (0, nf + n))
    return nf, gate, val, wg, wv, bg, bv


def _conv_fwd(up, cw, cb, *, name):
    s, f2 = up.shape
    f = f2 // 2
    tc = _pick(f, (256, 128))
    cr = min(CONV_ROWS, s)
    nf, gate, val, wg, wv, bg, bv = _conv_specs(s, f, tc)

    def body(g_ref, v_ref, wg_ref, wv_ref, bg_ref, bv_ref, out_ref):
        for r0 in range(0, s, cr):
            gc = _conv_rows(*_shift_down(g_ref, r0, cr, r0 == 0), wg_ref, bg_ref)
            vc = _conv_rows(*_shift_down(v_ref, r0, cr, r0 == 0), wv_ref, bv_ref)
            out_ref[pl.ds(r0, cr), :] = (gc * jax.nn.sigmoid(gc) * vc).astype(out_ref.dtype)

    return pl.pallas_call(
        body, name=name, grid=(nf,), in_specs=[gate, val, wg, wv, bg, bv], out_specs=gate,
        out_shape=jax.ShapeDtypeStruct((s, f), BF16), compiler_params=_params(("parallel",), VMEM_BIG),
    )(up, up, cw, cw, cb.reshape(1, f2), cb.reshape(1, f2))


def _conv_bwd(up, dact, cw, cb, *, name):
    s, f2 = up.shape
    f = f2 // 2
    tc = _pick(f, (256, 128))
    cr = min(CONV_ROWS, s)
    nf, gate, val, wg, wv, bg, bv = _conv_specs(s, f, tc)

    def body(g_ref, v_ref, da_ref, wg_ref, wv_ref, bg_ref, bv_ref, dup_ref, dw_ref, db_ref, dgc, dvc):
        zero = jnp.zeros((1, tc), F32)
        sums = [[zero] * 4, [zero] * 4]
        for r0 in range(0, s, cr):
            rows = pl.ds(r0, cr)
            gx = _shift_down(g_ref, r0, cr, r0 == 0)
            vx = _shift_down(v_ref, r0, cr, r0 == 0)
            gc = _conv_rows(*gx, wg_ref, bg_ref)
            vc = _conv_rows(*vx, wv_ref, bv_ref)
            sig = jax.nn.sigmoid(gc)
            da = da_ref[rows, :]
            d_gate = da * vc * (sig * (1.0 + gc * (1.0 - sig)))
            d_val = da * (gc * sig)
            dgc[rows, :] = d_gate
            dvc[rows, :] = d_val
            for part, (dc, (x1, x2, x0)) in enumerate(((d_gate, gx), (d_val, vx))):
                for tap, xs in enumerate((x2, x1, x0)):
                    sums[part][tap] = sums[part][tap] + jnp.sum(dc * xs, axis=0, keepdims=True)
                sums[part][3] = sums[part][3] + jnp.sum(dc, axis=0, keepdims=True)
        dw_ref[...] = jnp.zeros_like(dw_ref)
        db_ref[...] = jnp.zeros_like(db_ref)
        for part, (dc_ref, w_ref) in enumerate(((dgc, wg_ref), (dvc, wv_ref))):
            for tap in range(3):
                dw_ref[part, tap:tap + 1, :] = sums[part][tap]
            db_ref[part, 0:1, :] = sums[part][3]
            for r0 in range(0, s, cr):
                d0, d1, d2 = _shift_up(dc_ref, r0, cr, r0 + cr == s)
                dup_ref[part, pl.ds(r0, cr), :] = ((d0 * w_ref[2:3, :] + d1 * w_ref[1:2, :]) + d2 * w_ref[0:1, :]).astype(BF16)

    small = pl.BlockSpec((2, 8, tc), lambda n: (0, 0, n))
    return pl.pallas_call(
        body, name=name, grid=(nf,), in_specs=[gate, val, gate, wg, wv, bg, bv],
        out_specs=[pl.BlockSpec((2, s, tc), lambda n: (0, 0, n)), small, small],
        out_shape=[jax.ShapeDtypeStruct((2, s, f), BF16), jax.ShapeDtypeStruct((2, 8, f), F32),
                   jax.ShapeDtypeStruct((2, 8, f), F32)],
        scratch_shapes=[pltpu.VMEM((s, tc), F32)] * 2, compiler_params=_params(("parallel",), VMEM_BIG),
    )(up, up, dact, cw, cw, cb.reshape(1, f2), cb.reshape(1, f2))


def _adamw(w, g, m, v, *, name, dep=None):
    shape = w.shape
    cols = shape[-1]
    rows = w.size // cols
    if rows * cols * 4 <= (2 << 20):
        tr = rows
    else:
        tr = next(t for t in (1024, 512, 256, 128, 64, 32, 16, 8) if rows % t == 0 and (t * cols * 4 <= (2 << 20) or t == 8))

    def body(w_ref, g_ref, m_ref, v_ref, *rest):
        d_ref, nm_ref, nv_ref = rest[-3:]
        gr = g_ref[...]
        nm = ADAM_B1 * m_ref[...] + (1.0 - ADAM_B1) * gr
        nv = ADAM_B2 * v_ref[...] + (1.0 - ADAM_B2) * (gr * gr)
        m_hat = nm / (1.0 - ADAM_B1 ** ADAM_STEP)
        v_hat = nv / (1.0 - ADAM_B2 ** ADAM_STEP)
        d_ref[...] = -ADAM_LR * (m_hat / (jnp.sqrt(v_hat) + ADAM_EPS) + ADAM_WD * w_ref[...])
        nm_ref[...] = nm
        nv_ref[...] = nv

    blk = pl.BlockSpec((tr, cols), lambda i: (i, 0))
    out = jax.ShapeDtypeStruct((rows, cols), F32)
    res = pl.pallas_call(
        body, name=name, grid=(rows // tr,), in_specs=[blk] * 4 + ([] if dep is None else [ANY]), out_specs=[blk] * 3,
        out_shape=[out] * 3, compiler_params=_params(("parallel",), VMEM_BIG),
    )(*[t.reshape(rows, cols) for t in (w, g, m, v)], *([] if dep is None else [dep]))
    return [t.reshape(shape) for t in res]


def _place():
    x, y, c = lax.axis_index("x"), lax.axis_index("y"), lax.axis_index("c")
    others = [(1 - x, y), (x, 1 - y), (1 - x, 1 - y)]
    return x, y, c, others


def _remote(src, dst, send_sem, recv_sem, device):
    return pltpu.make_async_remote_copy(src_ref=src, dst_ref=dst, send_sem=send_sem, recv_sem=recv_sem, device_id=device,
                                        device_id_type=MESH)


def _hbm_call(body, name, args, out_shapes, n_sems, n_local, aliases=None):
    return pl.pallas_call(
        body, name=name, in_specs=[ANY] * len(args), out_specs=[ANY] * len(out_shapes), out_shape=out_shapes,
        scratch_shapes=[pltpu.SemaphoreType.DMA((n_sems,)), pltpu.SemaphoreType.DMA((n_sems,)),
                        pltpu.SemaphoreType.DMA((max(n_local, 1),))],
        input_output_aliases=aliases or {}, compiler_params=pltpu.CompilerParams(has_side_effects=True),
    )(*args)


def _all_gather_weights(halved, whole, *, name):
    nh, nw = len(halved), len(whole)
    arrays = list(halved) + list(whole)

    def body(*refs):
        srcs, outs = refs[:nh + nw], refs[nh + nw:2 * (nh + nw)]
        send, recv, local = refs[2 * (nh + nw):]
        x, y, c, others = _place()
        me = 2 * x + y
        locals_ = [pltpu.make_async_copy(srcs[nh + a], outs[nh + a].at[me], local.at[a]) for a in range(nw)]
        for cp in locals_:
            cp.start()
        sends = []
        for a in range(nh):
            half = outs[a].shape[1] // 2
            rows = pl.ds(c * half, half)
            for j, (px, py) in enumerate(others):
                sends.append(_remote(outs[a].at[me, rows], outs[a].at[me, rows], send.at[6 * a + j], recv.at[6 * a + j],
                                     (px, py, c)))
        for a in range(nw):
            for j, (px, py) in enumerate(others):
                sends.append(_remote(srcs[nh + a], outs[nh + a].at[me], send.at[6 * nh + 3 * a + j],
                                     recv.at[6 * nh + 3 * a + j], (px, py, c)))
        for cp in sends:
            cp.start()
        for a in range(nh):
            half = outs[a].shape[1] // 2
            rows = pl.ds(c * half, half)
            for j, (px, py) in enumerate(others):
                got = outs[a].at[2 * px + py, rows]
                _remote(got, got, send.at[6 * a + j], recv.at[6 * a + j], (px, py, c)).wait_recv()
                fwd = _remote(got, got, send.at[6 * a + 3 + j], recv.at[6 * a + 3 + j], (x, y, 1 - c))
                fwd.start()
                sends.append(fwd)
        for a in range(nh):
            half = outs[a].shape[1] // 2
            theirs = pl.ds((1 - c) * half, half)
            for j, (px, py) in enumerate(others):
                got = outs[a].at[2 * px + py, theirs]
                _remote(got, got, send.at[6 * a + 3 + j], recv.at[6 * a + 3 + j], (x, y, 1 - c)).wait_recv()
        for a in range(nw):
            for j, (px, py) in enumerate(others):
                got = outs[nh + a].at[2 * px + py]
                _remote(got, got, send.at[6 * nh + 3 * a + j], recv.at[6 * nh + 3 * a + j], (px, py, c)).wait_recv()
        for cp in sends:
            cp.wait_send()
        for cp in locals_:
            cp.wait()

    out_shapes = [jax.ShapeDtypeStruct(t.shape, t.dtype) for t in halved]
    out_shapes += [jax.ShapeDtypeStruct((N_CHIPS,) + t.shape, t.dtype) for t in whole]
    return _hbm_call(body, name, arrays, out_shapes, 6 * nh + 3 * nw, nw, aliases={a: a for a in range(nh)})


def _cast_into(w, layer, place, *, name, dep=None):
    _, r, cols = w.shape
    tr = _row_tile(r, cols)

    def body(place_ref, w_ref, *rest):
        rest[-1][...] = w_ref[...].astype(BF16)

    in_specs, args = [pl.BlockSpec((None, tr, cols), lambda i, pr: (layer, i, 0))], [place, w]
    if dep is not None:
        in_specs.append(ANY)
        args.append(dep)
    grid_spec = pltpu.PrefetchScalarGridSpec(
        num_scalar_prefetch=1, grid=(r // tr,), in_specs=in_specs,
        out_specs=pl.BlockSpec((None, tr, cols), lambda i, pr: (pr[1], i, 0)),
    )
    return pl.pallas_call(
        body, name=name, grid_spec=grid_spec, out_shape=jax.ShapeDtypeStruct((N_CHIPS, r, cols), BF16),
        compiler_params=_params(("parallel",), VMEM_BIG),
    )(*args)


HBM = pl.BlockSpec(memory_space=pltpu.HBM)
SEM = pl.BlockSpec(memory_space=pltpu.SEMAPHORE)
EFFECT = pltpu.SideEffectType.DATAFLOW_SIDE_EFFECTING
TOKEN = jax.ShapeDtypeStruct((8, 128), F32)


def _in_hbm(t):
    return pltpu.with_memory_space_constraint(t, pltpu.HBM)


def _gather_copies(buf, send, recv):
    x, y, c, others = _place()
    half = buf.shape[1] // 2
    rows = pl.ds(c * half, half)
    return [_remote(buf.at[2 * x + y, rows], buf.at[2 * x + y, rows], send.at[j], recv.at[j], (px, py, c))
            for j, (px, py) in enumerate(others)]


def _gather_start(bufs, *, name):
    n = len(bufs)

    def body(*refs):
        ins, sends, recvs, token = refs[:n], refs[n:2 * n], refs[2 * n:3 * n], refs[4 * n]
        for a in range(n):
            for cp in _gather_copies(ins[a], sends[a], recvs[a]):
                cp.start()
        token[...] = jnp.zeros_like(token)

    sems = [pltpu.SemaphoreType.DMA((3,))] * (2 * n)
    res = pl.pallas_call(
        body, name=name, out_shape=sems + [pltpu.HBM(t.shape, t.dtype) for t in bufs] + [TOKEN],
        in_specs=[HBM] * n, out_specs=[SEM] * (2 * n) + [HBM] * n + [pl.BlockSpec(memory_space=pltpu.VMEM)],
        input_output_aliases={a: 2 * n + a for a in range(n)}, compiler_params=pltpu.CompilerParams(has_side_effects=EFFECT),
    )(*[_in_hbm(t) for t in bufs])
    return [(res[2 * n + a], res[a], res[n + a]) for a in range(n)], res[3 * n]


def _gather_wait(state, after, *, name):
    buf, send, recv = state

    def body(buf_ref, send_ref, recv_ref, after_ref, out_ref):
        for cp in _gather_copies(buf_ref, send_ref, recv_ref):
            cp.wait_send()
            cp.wait_recv()

    return pl.pallas_call(
        body, name=name, out_shape=pltpu.HBM(buf.shape, buf.dtype), in_specs=[HBM, SEM, SEM, ANY], out_specs=HBM,
        input_output_aliases={0: 0}, compiler_params=pltpu.CompilerParams(has_side_effects=EFFECT),
    )(buf, send, recv, after)


def _chip_copies(src, land, send, recv):
    _x, _y, c, others = _place()
    return [_remote(src.at[2 * px + py], land.at[j], send.at[j], recv.at[j], (px, py, c)) for j, (px, py) in enumerate(others)]


def _chip_start(partial, *, name):
    def body(src, land, send, recv, _src_thru, _land_thru, token):
        for cp in _chip_copies(src, land, send, recv):
            cp.start()
        token[...] = jnp.zeros_like(token)

    land_shape = (3,) + partial.shape[1:]
    sem = pltpu.SemaphoreType.DMA((3,))
    send, recv, src, land, token = pl.pallas_call(
        body, name=name, out_shape=[sem, sem, pltpu.HBM(partial.shape, partial.dtype), pltpu.HBM(land_shape, partial.dtype), TOKEN],
        in_specs=[HBM, HBM], out_specs=[SEM, SEM, HBM, HBM, pl.BlockSpec(memory_space=pltpu.VMEM)],
        input_output_aliases={0: 2, 1: 3}, compiler_params=pltpu.CompilerParams(has_side_effects=EFFECT),
    )(_in_hbm(partial), _in_hbm(lax.empty(land_shape, partial.dtype)))
    return (src, land, send, recv), token


def _chip_wait(state, after, *, name):
    src, land, send, recv = state

    def body(src_ref, land_ref, send_ref, recv_ref, after_ref, _src_out, _land_out):
        for cp in _chip_copies(src_ref, land_ref, send_ref, recv_ref):
            cp.wait_send()
            cp.wait_recv()

    return pl.pallas_call(
        body, name=name, out_shape=[pltpu.HBM(src.shape, src.dtype), pltpu.HBM(land.shape, land.dtype)],
        in_specs=[HBM, HBM, SEM, SEM, ANY], out_specs=[HBM, HBM], input_output_aliases={0: 0, 1: 1},
        compiler_params=pltpu.CompilerParams(has_side_effects=EFFECT),
    )(src, land, send, recv, after)[1]


def _split_start(bufs, copies, n_copies, *, name):
    n = len(bufs)

    def body(*refs):
        for cp in copies(refs[:n], refs[n], refs[n + 1]):
            cp.start()
        refs[-1][...] = jnp.zeros_like(refs[-1])

    sem = pltpu.SemaphoreType.DMA((n_copies,))
    res = pl.pallas_call(
        body, name=name, out_shape=[sem, sem] + [pltpu.HBM(t.shape, t.dtype) for t in bufs] + [TOKEN],
        in_specs=[HBM] * n, out_specs=[SEM, SEM] + [HBM] * n + [pl.BlockSpec(memory_space=pltpu.VMEM)],
        input_output_aliases={a: 2 + a for a in range(n)}, compiler_params=pltpu.CompilerParams(has_side_effects=EFFECT),
    )(*[_in_hbm(t) for t in bufs])
    return (list(res[2:2 + n]), res[0], res[1]), res[-1]


def _split_wait(state, copies, after, *, name):
    bufs, send, recv = state
    n = len(bufs)

    def body(*refs):
        for cp in copies(refs[:n], refs[n], refs[n + 1]):
            cp.wait_send()
            cp.wait_recv()

    return list(pl.pallas_call(
        body, name=name, out_shape=[pltpu.HBM(t.shape, t.dtype) for t in bufs], in_specs=[HBM] * n + [SEM, SEM, ANY],
        out_specs=[HBM] * n, input_output_aliases={a: a for a in range(n)},
        compiler_params=pltpu.CompilerParams(has_side_effects=EFFECT),
    )(*bufs, send, recv, after))


def _hand_over_copies(refs, send, recv):
    x, y, c, others = _place()
    half = refs[0].shape[1] // 2
    got = [refs[0].at[2 * px + py, pl.ds(c * half, half)] for px, py in others]
    return [_remote(got[j], got[j], send.at[j], recv.at[j], (x, y, 1 - c)) for j in range(3)]


def _pair_copies(refs, send, recv):
    x, y, c, _o = _place()
    half = refs[0].shape[1] // 2
    return [_remote(refs[0].at[:, pl.ds((1 - c) * half, half), :], refs[1], send.at[0], recv.at[0], (x, y, 1 - c))]


def _share_copies(refs, send, recv):
    x, y, c, _o = _place()
    return [_remote(refs[0].at[:, c], refs[0].at[:, c], send.at[0], recv.at[0], (x, y, 1 - c))]


def _all_gather_small(pack, *, name):
    def body(src, out, send, recv, local):
        x, y, c, others = _place()

        def slot(px, py, pc):
            return out.at[4 * px + 2 * py + pc]

        mine = pltpu.make_async_copy(src, slot(x, y, c), local.at[0])
        mine.start()
        first = [_remote(src, slot(x, y, c), send.at[0], recv.at[0], (x, y, 1 - c))]
        first += [_remote(src, slot(x, y, c), send.at[1 + j], recv.at[1 + j], (px, py, c)) for j, (px, py) in enumerate(others)]
        for cp in first:
            cp.start()
        passed = []
        for j, (px, py) in enumerate(others):
            got = slot(px, py, c)
            _remote(got, got, send.at[1 + j], recv.at[1 + j], (px, py, c)).wait_recv()
            fwd = _remote(got, got, send.at[4 + j], recv.at[4 + j], (x, y, 1 - c))
            fwd.start()
            passed.append(fwd)
        theirs = slot(x, y, 1 - c)
        _remote(theirs, theirs, send.at[0], recv.at[0], (x, y, 1 - c)).wait_recv()
        for j, (px, py) in enumerate(others):
            got = slot(px, py, 1 - c)
            _remote(got, got, send.at[4 + j], recv.at[4 + j], (x, y, 1 - c)).wait_recv()
        for cp in first + passed:
            cp.wait_send()
        mine.wait()

    return _hbm_call(body, name, [pack], [jax.ShapeDtypeStruct((N_DEV,) + pack.shape, pack.dtype)], 7, 1)[0]


def _row_tile(rows, cols):
    return max(t for t in range(16, rows + 1, 16) if rows % t == 0 and (t * cols * 4 <= (4 << 20) or t == 16))


def _pair_sum(grad, theirs, place, *, name):
    _, r, cols = grad.shape
    r2 = r // 2
    tr = _row_tile(r2, cols)
    nr = r2 // tr

    def body(place_ref, g_ref, t_ref, all_ref):
        all_ref[...] = (g_ref[...].astype(F32) + t_ref[...].astype(F32)).astype(all_ref.dtype)

    grid_spec = pltpu.PrefetchScalarGridSpec(
        num_scalar_prefetch=1, grid=(N_CHIPS, nr),
        in_specs=[pl.BlockSpec((None, tr, cols), lambda k, i, pr: (k, pr[0] * nr + i, 0)),
                  pl.BlockSpec((None, tr, cols), lambda k, i, pr: (k, i, 0))],
        out_specs=pl.BlockSpec((None, tr, cols), lambda k, i, pr: (k, i, 0)),
    )
    return pl.pallas_call(
        body, name=name, grid_spec=grid_spec, out_shape=jax.ShapeDtypeStruct((N_CHIPS, r2, cols), BF16),
        compiler_params=_params(("parallel", "parallel"), VMEM_BIG),
    )(place, grad, theirs)


def _chip_sum(grad, theirs, got, place, buf, layer, depth, *, name):
    _, r, cols = grad.shape
    r2 = r // 2
    tr = _row_tile(r2, cols)
    nr = r2 // tr

    def body(place_ref, g_ref, t_ref, got_ref, *rest):
        own = g_ref[...].astype(F32) + t_ref[...].astype(F32)
        rest[-1][...] = ((own + got_ref[0].astype(F32)) + got_ref[1].astype(F32)) + got_ref[2].astype(F32)

    in_specs = [pl.BlockSpec((None, tr, cols), lambda i, pr: (pr[1], pr[0] * nr + i, 0)),
                pl.BlockSpec((None, tr, cols), lambda i, pr: (pr[1], i, 0)),
                pl.BlockSpec((3, tr, cols), lambda i, pr: (0, i, 0))]
    args = [place, grad, theirs, got]
    if buf is not None:
        in_specs.append(ANY)
        args.append(buf)
    grid_spec = pltpu.PrefetchScalarGridSpec(
        num_scalar_prefetch=1, grid=(nr,), in_specs=in_specs,
        out_specs=pl.BlockSpec((None, None, tr, cols), lambda i, pr: (layer, pr[0], i, 0)),
    )
    return pl.pallas_call(
        body, name=name, grid_spec=grid_spec, out_shape=jax.ShapeDtypeStruct((depth, 2, r2, cols), F32),
        input_output_aliases={} if buf is None else {4: 0}, compiler_params=_params(("parallel",), VMEM_BIG),
    )(*args)


def _sum_devices(parts, *, name):
    _, rows, cols = parts.shape
    tr = rows if N_DEV * rows * cols * 4 <= (16 << 20) else _pick(rows, (256, 128, 64, 32, 16, 8))

    def body(p_ref, out_ref):
        acc = p_ref[0]
        for d in range(1, N_DEV):
            acc = acc + p_ref[d]
        out_ref[...] = acc

    return pl.pallas_call(
        body, name=name, grid=(rows // tr,), in_specs=[pl.BlockSpec((N_DEV, tr, cols), lambda i: (0, i, 0))],
        out_specs=pl.BlockSpec((tr, cols), lambda i: (i, 0)), out_shape=jax.ShapeDtypeStruct((rows, cols), F32),
        compiler_params=_params(("parallel",), VMEM_BIG),
    )(parts)


def _pack(parts):
    rows = []
    for t in parts:
        flat = t.reshape(-1, 128)
        pad = (-flat.shape[0]) % 8
        rows.append(jnp.pad(flat, ((0, pad), (0, 0))) if pad else flat)
    return jnp.concatenate(rows, axis=0)


def _unpack(pack, shapes):
    out, r0 = [], 0
    for shp in shapes:
        n = math.prod(shp) // 128
        out.append(pack[r0:r0 + n].reshape(shp))
        r0 += n + (-n) % 8
    return out


SMALL = ["attn_norm_g", "q_norm_g", "k_norm_g", "sgu_norm_g", "sgu_w", "sgu_b", "out_norm_a_g", "out_norm_b_g",
         "ffn_norm_g", "conv_b"]
BIG = ["w_in", "w_out", "w_up", "w_down"]
ORDER = ["attn_norm_g", "w_in", "q_norm_g", "k_norm_g", "sgu_norm_g", "sgu_w", "sgu_b", "out_norm_a_g", "out_norm_b_g",
         "w_out", "ffn_norm_g", "w_up", "conv_w", "conv_b", "w_down"]


def kernel(x, attn_norm_g, w_in, q_norm_g, k_norm_g, sgu_norm_g, sgu_w, sgu_b, out_norm_a_g, out_norm_b_g, w_out, ffn_norm_g, w_up, conv_w, conv_b, w_down, loss_target, m_attn_norm_g, m_w_in, m_q_norm_g, m_k_norm_g, m_sgu_norm_g, m_sgu_w, m_sgu_b, m_out_norm_a_g, m_out_norm_b_g, m_w_out, m_ffn_norm_g, m_w_up, m_conv_w, m_conv_b, m_w_down, v_attn_norm_g, v_w_in, v_q_norm_g, v_k_norm_g, v_sgu_norm_g, v_sgu_w, v_sgu_b, v_out_norm_a_g, v_out_norm_b_g, v_w_out, v_ffn_norm_g, v_w_up, v_conv_w, v_conv_b, v_w_down):
    W = dict(attn_norm_g=attn_norm_g, w_in=w_in, q_norm_g=q_norm_g, k_norm_g=k_norm_g, sgu_norm_g=sgu_norm_g, sgu_w=sgu_w,
             sgu_b=sgu_b, out_norm_a_g=out_norm_a_g, out_norm_b_g=out_norm_b_g, w_out=w_out, ffn_norm_g=ffn_norm_g, w_up=w_up,
             conv_w=conv_w, conv_b=conv_b, w_down=w_down)
    M = dict(attn_norm_g=m_attn_norm_g, w_in=m_w_in, q_norm_g=m_q_norm_g, k_norm_g=m_k_norm_g, sgu_norm_g=m_sgu_norm_g,
             sgu_w=m_sgu_w, sgu_b=m_sgu_b, out_norm_a_g=m_out_norm_a_g, out_norm_b_g=m_out_norm_b_g, w_out=m_w_out,
             ffn_norm_g=m_ffn_norm_g, w_up=m_w_up, conv_w=m_conv_w, conv_b=m_conv_b, w_down=m_w_down)
    V = dict(attn_norm_g=v_attn_norm_g, w_in=v_w_in, q_norm_g=v_q_norm_g, k_norm_g=v_k_norm_g, sgu_norm_g=v_sgu_norm_g,
             sgu_w=v_sgu_w, sgu_b=v_sgu_b, out_norm_a_g=v_out_norm_a_g, out_norm_b_g=v_out_norm_b_g, w_out=v_w_out,
             ffn_norm_g=v_ffn_norm_g, w_up=v_w_up, conv_w=v_conv_w, conv_b=v_conv_b, w_down=v_w_down)
    depth = w_in.shape[0]
    s, d = x.shape[1], x.shape[2]
    n_heads = out_norm_a_g.shape[1]
    core = lax.axis_index("c")
    chip = 2 * lax.axis_index("x") + lax.axis_index("y")
    place = jnp.stack([core, chip]).astype(jnp.int32)
    xs = x.reshape(s, d)

    f_local = conv_w.shape[2]
    taps = lax.dynamic_update_slice(jnp.zeros((N_CHIPS, 16, f_local), F32), conv_w.reshape(1, depth * 3, f_local),
                                    (chip, 0, 0))
    order = [(l, n) for l in range(depth) for n in BIG]
    first, token = _gather_start([_cast_into(w_in, 0, place, name="cast_w_in"), taps], name="gather_start_first")
    rest, token = _gather_start([_cast_into(W[n], l, place, dep=token, name=f"cast_{n}") for l, n in order[1:]],
                                name="gather_start_rest")
    states = dict(zip(order, [first[0]] + rest))

    states["taps"] = first[1]

    def landed(key, after, tag):
        buf = _gather_wait(states[key], after, name=f"gather_wait_{tag}")
        return _split_start([buf], _hand_over_copies, 3, name=f"hand_over_{tag}")

    def whole(state, after, tag):
        return _split_wait(state, _hand_over_copies, after, name=f"hand_over_wait_{tag}")[0]

    saved, full = [], []
    cur = xs
    for l in range(depth):
        gain = attn_norm_g[l] + token[0, 0] if l == 0 else attn_norm_g[l]
        h = _rmsnorm_fwd(cur, gain, name="attn_norm")
        if l == 0:
            ho_in, _ = landed((0, "w_in"), h, "w_in")
        w_in_l = whole(ho_in, h, "w_in")
        if l == 0:
            ho_taps, token = landed("taps", w_in_l, "taps")
        p = _mm(h, w_in_l, "nn", b_split=N_CHIPS, caps=(2048, 256, 2048), dep=token, name="proj_in")
        if l == 0:
            taps = whole(ho_taps, p, "taps")[:, :depth * 3].reshape(N_CHIPS, depth, 3, f_local)
            cw_full = jnp.transpose(taps, (1, 2, 0, 3)).reshape(depth, 3, N_CHIPS * f_local)
        mix, o_raw, lsum = _attn_fwd(p, q_norm_g[l], k_norm_g[l], out_norm_a_g[l], n_heads, name="attn_fwd")
        ho_out, token = landed((l, "w_out"), mix, "w_out")
        mix = _sgu_fwd(p, mix, sgu_w[l], sgu_b[l], sgu_norm_g[l] + token[0, 0], out_norm_b_g[l], n_heads, name="sgu_fwd")
        w_out_l = whole(ho_out, mix, "w_out").reshape(-1, d)
        ho_up, token = landed((l, "w_up"), w_out_l, "w_up")
        x1 = _mm(mix, w_out_l, "nn", a_split=2, res=cur, caps=(2048, 512, 1024), dep=token, name="proj_out")
        h2 = _rmsnorm_fwd(x1, ffn_norm_g[l], name="ffn_norm")
        w_up_l = whole(ho_up, h2, "w_up")
        up = _mm(h2, w_up_l, "nn", b_split=N_CHIPS, caps=(2048, 256, 2048), name="ffn_up")
        ho_down, token = landed((l, "w_down"), up, "w_down")
        act = _conv_fwd(up, cw_full[l], conv_b[l] + token[0, 0], name="conv_fwd")
        w_down_l = whole(ho_down, act, "w_down").reshape(-1, d)
        if l + 1 < depth:
            ho_in, token = landed((l + 1, "w_in"), w_down_l, "w_in")
        x2 = _mm(act, w_down_l, "nn", res=x1, caps=(1024, 512, 2816), dep=token, name="ffn_down")
        full.append(dict(w_in=w_in_l, w_out=w_out_l, w_up=w_up_l, w_down=w_down_l, conv_w=cw_full[l]))
        saved.append(dict(x0=cur, h=h, p=p, o_raw=o_raw, lsum=lsum, mix=mix, x1=x1, h2=h2, up=up, act=act))
        cur = x2

    dx, dxb, sq = _loss_head(cur, loss_target.reshape(s, d), name="loss_head")
    loss = lax.psum(sq[0, 0] * (0.5 / d), ("x", "y", "c"))

    small_grads = {n: [None] * depth for n in SMALL + ["conv_w"]}
    def pair_begin(n, grad):
        land = lax.empty((N_CHIPS, grad.shape[1] // 2, grad.shape[2]), grad.dtype)
        return _split_start([grad, land], _pair_copies, 1, name=f"pair_start_{n}")

    def chip_begin(n, state, after):
        grad, theirs = _split_wait(state, _pair_copies, after, name=f"pair_wait_{n}")
        state, tok = _chip_start(_pair_sum(grad, theirs, place, name=f"pair_sum_{n}"), name=f"chip_start_{n}")
        return (grad, theirs, state), tok

    pending = {}
    for l in reversed(range(depth)):
        fw, sv = full[l], saved[l]
        dact = _mm(dxb, fw["w_down"], "nt", caps=(2048, 512, 2048), name="d_act")
        g_down = _mm(sv["act"], dxb, "tn", caps=(512, 2048, 2048), out_dtype=BF16, name="g_down")
        pair, tok = pair_begin("w_down", g_down.reshape(N_CHIPS, -1, d))
        dup, dcw, dcb = _conv_bwd(sv["up"], dact, fw["conv_w"], conv_b[l] + tok[0, 0], name="conv_bwd")
        pending[(l, "w_down")], tok = chip_begin("w_down", pair, dup)
        g_up = _mm(sv["h2"], dup, "tn", b_split=2, o_split=N_CHIPS, caps=(2048, 256, 2048), out_dtype=BF16, dep=tok,
                   name="g_up")
        pair, tok = pair_begin("w_up", g_up)
        dh2 = _mm(dup, fw["w_up"], "nt", a_split=2, b_split=N_CHIPS, caps=(1024, 512, 2816), dep=tok, name="d_h2")
        pending[(l, "w_up")], tok = chip_begin("w_up", pair, dh2)
        dx1, dx1b, dg_ffn = _rmsnorm_bwd(sv["x1"], ffn_norm_g[l] + tok[0, 0], dh2, dx, name="ffn_norm_bwd")
        dmix = _mm(dx1b, fw["w_out"], "nt", caps=(2048, 512, 2048), name="d_mix")
        g_out = _mm(sv["mix"], dx1b, "tn", a_split=2, caps=(512, 2048, 2048), out_dtype=BF16, name="g_out")
        pair, tok = pair_begin("w_out", g_out.reshape(N_CHIPS, -1, d))
        dqkv, dgq, dgk, dgoa = _attn_bwd(sv["p"], sv["o_raw"], sv["lsum"], dmix, q_norm_g[l] + tok[0, 0], k_norm_g[l],
                                         out_norm_a_g[l], n_heads, name="attn_bwd")
        pending[(l, "w_out")], tok = chip_begin("w_out", pair, dqkv)
        duv, dsw, dsb, dgv, dgob = _sgu_bwd(sv["p"], dmix, sgu_w[l], sgu_b[l], sgu_norm_g[l] + tok[0, 0], out_norm_b_g[l],
                                            n_heads, name="sgu_bwd")
        dp = jnp.concatenate([dqkv[0], dqkv[1], dqkv[2], duv[0], duv[1]], axis=1)
        g_in = _mm(sv["h"], dp, "tn", o_split=N_CHIPS, caps=(2048, 256, 2048), out_dtype=BF16, name="g_in")
        pair, tok = pair_begin("w_in", g_in)
        dh = _mm(dp, fw["w_in"], "nt", b_split=N_CHIPS, caps=(2048, 512, 1280), dep=tok, name="d_h")
        pending[(l, "w_in")], tok = chip_begin("w_in", pair, dh)
        dx, dxb, dg_attn = _rmsnorm_bwd(sv["x0"], attn_norm_g[l] + tok[0, 0], dh, dx1, name="attn_norm_bwd")

        small_grads["attn_norm_g"][l] = dg_attn.reshape(d)
        small_grads["q_norm_g"][l] = jnp.sum(dgq, axis=(0, 1))
        small_grads["k_norm_g"][l] = jnp.sum(dgk, axis=(0, 1))
        small_grads["sgu_norm_g"][l] = dgv.reshape(-1, BLK)
        small_grads["sgu_w"][l] = dsw
        small_grads["sgu_b"][l] = dsb.reshape(-1, BLK)
        small_grads["out_norm_a_g"][l] = dgoa.reshape(-1, BLK)
        small_grads["out_norm_b_g"][l] = dgob.reshape(-1, BLK)
        small_grads["ffn_norm_g"][l] = dg_ffn.reshape(d)
        small_grads["conv_b"][l] = dcb[:, 0, :].reshape(-1)
        small_grads["conv_w"][l] = jnp.transpose(dcw[:, :3, :], (1, 0, 2)).reshape(3, -1)

    G, D_, NM, NV = {}, {}, {}, {}
    after, prev = dx, None
    for n in ("w_down", "w_up", "w_out", "w_in"):
        buf = None
        for l in reversed(range(depth)):
            grad, theirs, state = pending[(l, n)]
            got = _chip_wait(state, after, name=f"chip_wait_{n}")
            buf = _chip_sum(grad, theirs, got, place, buf, l, depth, name=f"chip_sum_{n}")
            after = buf
        share, tok = _split_start([buf], _share_copies, 1, name=f"share_start_{n}")
        if prev is not None:
            D_[prev], NM[prev], NV[prev] = _adamw(W[prev], G[prev], M[prev], V[prev], dep=tok, name=f"adamw_{prev}")
            after = NV[prev]
        G[n] = _split_wait(share, _share_copies, after, name=f"share_wait_{n}")[0].reshape(W[n].shape)
        after, prev = G[n], n
    D_[prev], NM[prev], NV[prev] = _adamw(W[prev], G[prev], M[prev], V[prev], name=f"adamw_{prev}")

    names = SMALL + ["conv_w"]
    pack = _pack([jnp.stack(small_grads[n]) for n in names])
    total = _sum_devices(_all_gather_small(pack, name="gather_small"), name="sum_small")
    f_full = conv_b.shape[1]
    shapes = [W[n].shape for n in SMALL] + [(depth, 3, f_full)]
    for n, t in zip(names, _unpack(total, shapes)):
        G[n] = t
    G["conv_w"] = lax.dynamic_slice_in_dim(G["conv_w"], chip * f_local, f_local, axis=2)

    D_["conv_w"], NM["conv_w"], NV["conv_w"] = _adamw(conv_w, G["conv_w"], m_conv_w, v_conv_w, name="adamw_conv_w")
    small_shapes = [W[n].shape for n in SMALL]
    res = _adamw(_pack([W[n] for n in SMALL]), _pack([G[n] for n in SMALL]), _pack([M[n] for n in SMALL]),
                 _pack([V[n] for n in SMALL]), name="adamw_small")
    for dst, t in zip((D_, NM, NV), res):
        for n, u in zip(SMALL, _unpack(t, small_shapes)):
            dst[n] = u

    return (loss, dx.reshape(x.shape), *[G[n] for n in ORDER], *[D_[n] for n in ORDER], *[NM[n] for n in ORDER],
            *[NV[n] for n in ORDER])
```

```python
import functools
import math

import jax
import jax.numpy as jnp
from jax import lax
from jax.experimental import pallas as pl
from jax.experimental.pallas import tpu as pltpu

F32 = jnp.float32
BF16 = jnp.bfloat16
EPS = 1e-6
BLK = 128
N_CHIPS = 4
N_DEV = 8
ADAM_LR, ADAM_B1, ADAM_B2, ADAM_EPS, ADAM_WD, ADAM_STEP = 0.001, 0.9, 0.999, 1e-08, 0.01, 10
VMEM_BIG = 48 * 1024 * 1024
MESH = pl.DeviceIdType.MESH
ANY = pl.BlockSpec(memory_space=pl.ANY)


def _pick(dim, prefs):
    for t in prefs:
        if dim % t == 0:
            return t
    raise ValueError(f"no tile in {prefs} divides {dim}")


def _params(sem=None, vmem=None):
    return pltpu.CompilerParams(dimension_semantics=sem, vmem_limit_bytes=vmem)


def _ldims(arr, split):
    if split == 1:
        return arr.shape
    p, r, cs = arr.shape
    assert p == split
    return (r, p * cs)


def _spec(tr, tc, split, cols, rc):
    if split == 1:
        return pl.BlockSpec((tr, tc), lambda i, j, k: rc(i, j, k))
    per = (cols // split) // tc

    def imap(i, j, k):
        r, c = rc(i, j, k)
        return (c // per, r, c % per)

    return pl.BlockSpec((None, tr, tc), imap)


def _fit(unit, cap):
    return max(t for t in range(128, min(unit, cap) + 1, 128) if unit % t == 0)


def _mm(a, b, mode, *, name, caps, a_split=1, b_split=1, o_split=1, out_dtype=F32, res=None, dep=None):
    ar, ac = _ldims(a, a_split)
    br, bc = _ldims(b, b_split)
    if mode == "nn":
        m, k, n = ar, ac, bc
        assert br == k
        ku, nu, mu = math.gcd(k // a_split, k), math.gcd(n // b_split, n // o_split), m
    elif mode == "nt":
        m, k, n = ar, ac, br
        assert bc == k
        ku, nu, mu = math.gcd(k // a_split, k // b_split), n // o_split, m
    else:
        k, m, n = ar, ac, bc
        assert br == k
        ku, nu, mu = k, math.gcd(n // b_split, n // o_split), m // a_split
    tm, tn, tk = _fit(mu, caps[0]), _fit(nu, caps[1]), _fit(ku, caps[2])
    nk = k // tk
    if mode == "nn":
        a_spec = _spec(tm, tk, a_split, k, lambda i, j, kk: (i, kk))
        b_spec = _spec(tk, tn, b_split, n, lambda i, j, kk: (kk, j))
    elif mode == "nt":
        a_spec = _spec(tm, tk, a_split, k, lambda i, j, kk: (i, kk))
        b_spec = _spec(tn, tk, b_split, k, lambda i, j, kk: (j, kk))
    else:
        a_spec = _spec(tk, tm, a_split, m, lambda i, j, kk: (kk, i))
        b_spec = _spec(tk, tn, b_split, n, lambda i, j, kk: (kk, j))
    o_spec = _spec(tm, tn, o_split, n, lambda i, j, kk: (i, j))
    dims = {"nn": (((1,), (0,)), ((), ())), "nt": (((1,), (1,)), ((), ())), "tn": (((0,), (0,)), ((), ()))}[mode]

    def body(a_ref, b_ref, *rest):
        if dep is not None:
            rest = rest[1:]
        if res is None:
            o_ref, acc = rest
        else:
            r_ref, o_ref, acc = rest
        kk = pl.program_id(2)

        @pl.when(kk == 0)
        def _():
            acc[...] = jnp.zeros_like(acc)

        acc[...] += lax.dot_general(a_ref[...].astype(BF16), b_ref[...].astype(BF16), dims, preferred_element_type=F32)

        @pl.when(kk == nk - 1)
        def _():
            out = acc[...]
            if res is not None:
                out = out + r_ref[...]
            o_ref[...] = out.astype(o_ref.dtype)

    in_specs, args = [a_spec, b_spec], [a, b]
    if dep is not None:
        in_specs.append(ANY)
        args.append(dep)
    if res is not None:
        in_specs.append(pl.BlockSpec((tm, tn), lambda i, j, kk: (i, j)))
        args.append(res)
    out_shape = (m, n) if o_split == 1 else (o_split, m, n // o_split)
    return pl.pallas_call(
        body, name=name, grid=(m // tm, n // tn, nk), in_specs=in_specs, out_specs=o_spec,
        out_shape=jax.ShapeDtypeStruct(out_shape, out_dtype), scratch_shapes=[pltpu.VMEM((tm, tn), F32)],
        compiler_params=_params(("parallel", "parallel", "arbitrary"), VMEM_BIG),
    )(*args)


def _rstd(v):
    return lax.rsqrt(jnp.mean(v * v, axis=-1, keepdims=True) + EPS)


def _norm_bwd(v, r, gain, dout):
    a = dout * gain
    dv = r * (a - v * (r * r * jnp.mean(a * v, axis=-1, keepdims=True)))
    return dv, dout * v * r


def _rmsnorm_fwd(x, g, *, name):
    s, d = x.shape
    tr = _pick(s, (256, 128))

    def body(x_ref, g_ref, o_ref):
        v = x_ref[...]
        o_ref[...] = (v * _rstd(v) * g_ref[...]).astype(o_ref.dtype)

    return pl.pallas_call(
        body, name=name, grid=(s // tr,),
        in_specs=[pl.BlockSpec((tr, d), lambda i: (i, 0)), pl.BlockSpec((1, d), lambda i: (0, 0))],
        out_specs=pl.BlockSpec((tr, d), lambda i: (i, 0)), out_shape=jax.ShapeDtypeStruct((s, d), BF16),
        compiler_params=_params(("parallel",)),
    )(x, g.reshape(1, d))


def _rmsnorm_bwd(x, g, dh, dres, *, name):
    s, d = x.shape
    tr = _pick(s, (256, 128))

    def body(x_ref, g_ref, dh_ref, dres_ref, dx_ref, dxb_ref, dg_ref):
        v = x_ref[...]
        dv, dgr = _norm_bwd(v, _rstd(v), g_ref[...], dh_ref[...])
        dx = dres_ref[...] + dv
        dx_ref[...] = dx
        dxb_ref[...] = dx.astype(BF16)
        part = jnp.sum(dgr, axis=0, keepdims=True)

        @pl.when(pl.program_id(0) == 0)
        def _():
            dg_ref[...] = part

        @pl.when(pl.program_id(0) > 0)
        def _():
            dg_ref[...] += part

    row = pl.BlockSpec((tr, d), lambda i: (i, 0))
    one = pl.BlockSpec((1, d), lambda i: (0, 0))
    return pl.pallas_call(
        body, name=name, grid=(s // tr,), in_specs=[row, one, row, row], out_specs=[row, row, one],
        out_shape=[jax.ShapeDtypeStruct((s, d), F32), jax.ShapeDtypeStruct((s, d), BF16), jax.ShapeDtypeStruct((1, d), F32)],
        compiler_params=_params(("arbitrary",)),
    )(x, g.reshape(1, d), dh, dres)


def _loss_head(y, target, *, name):
    s, d = y.shape
    tr = _pick(s, (256, 128))

    def body(y_ref, t_ref, dy_ref, dyb_ref, ls_ref):
        e = y_ref[...] - t_ref[...]
        dy = e * (1.0 / d)
        dy_ref[...] = dy
        dyb_ref[...] = dy.astype(BF16)
        part = jnp.full(ls_ref.shape, jnp.sum(e * e), F32)

        @pl.when(pl.program_id(0) == 0)
        def _():
            ls_ref[...] = part

        @pl.when(pl.program_id(0) > 0)
        def _():
            ls_ref[...] += part

    row = pl.BlockSpec((tr, d), lambda i: (i, 0))
    return pl.pallas_call(
        body, name=name, grid=(s // tr,), in_specs=[row, row], out_specs=[row, row, pl.BlockSpec((8, 128), lambda i: (0, 0))],
        out_shape=[jax.ShapeDtypeStruct((s, d), F32), jax.ShapeDtypeStruct((s, d), BF16), jax.ShapeDtypeStruct((8, 128), F32)],
        compiler_params=_params(("arbitrary",)),
    )(y, target)


def _iota2(axis):
    return lax.broadcasted_iota(jnp.int32, (BLK, BLK), axis)


def _tri_sum(v, tri):
    hi = v.astype(BF16)
    lo = (v - hi.astype(F32)).astype(BF16)
    return jnp.dot(hi, tri, preferred_element_type=F32) + jnp.dot(lo, tri, preferred_element_type=F32)


def _dot_nt(a, b):
    return lax.dot_general(a, b, (((1,), (1,)), ((), ())), preferred_element_type=F32)


def _dot_tn(a, b):
    return lax.dot_general(a, b, (((0,), (0,)), ((), ())), preferred_element_type=F32)


TQ_MAX = 1024
TQ_MAX_BWD = 512
HP = 2
VMEM_ATTN_BWD = 56 * 1024 * 1024


def _lanes(hh):
    return slice(hh * BLK, (hh + 1) * BLK)


def _causal(n, diag):
    if not diag:
        return None
    return lax.broadcasted_iota(jnp.int32, (n, BLK), 1) < lax.broadcasted_iota(jnp.int32, (n, BLK), 0)


def _sb_sums(z, mask, rhs_gt):
    lb = jnp.minimum(z, 0.0) - jnp.log(1.0 + jnp.exp(-jnp.abs(z)))
    l1m = lb - z
    if mask is not None:
        l1m = jnp.where(mask, l1m, 0.0)
    return lb, _tri_sum(l1m, rhs_gt)


def _below(old, new, r0):
    return new if r0 == 0 else jnp.concatenate([old[:r0], new], axis=0)


def _attn_fwd(p, gq, gk, go, n_heads, *, name):
    s = p.shape[0]
    tq = min(TQ_MAX, s)
    per = tq // BLK
    scale = BLK ** -0.5

    def body(q_ref, k_ref, v_ref, gq_ref, gk_ref, go_ref, att_ref, o_ref, l_ref, qn, kn, vb):
        for hh in range(HP):
            q = q_ref[:, _lanes(hh)]
            k = k_ref[:, _lanes(hh)]
            qn[:, _lanes(hh)] = (q * _rstd(q) * gq_ref[...] * scale).astype(BF16)
            kn[:, _lanes(hh)] = (k * _rstd(k) * gk_ref[...]).astype(BF16)
            vb[:, _lanes(hh)] = v_ref[:, _lanes(hh)].astype(BF16)
        rhs_gt = jnp.concatenate([(_iota2(0) > _iota2(1)).astype(BF16), jnp.ones((BLK, BLK), BF16)], axis=1)

        def step(q0, j, r0, diag, states):
            n = tq - r0
            rows = pl.ds(pl.multiple_of(q0 + r0, BLK), n)
            cols = pl.ds(pl.multiple_of(j * BLK, BLK), BLK)
            mask = _causal(n, diag)
            zs = [_dot_nt(qn[rows, _lanes(hh)], kn[cols, _lanes(hh)]) for hh in range(HP)]
            sums = [_sb_sums(z, mask, rhs_gt) for z in zs]
            new = []
            for hh in range(HP):
                acc, later = states[hh]
                lb, both = sums[hh]
                a = jnp.exp(lb + both[:, :BLK] + later[r0:])
                if diag:
                    a = jnp.where(mask, a, 0.0)
                acc_new = acc[r0:] + jnp.dot(a.astype(BF16), vb[cols, _lanes(hh)], preferred_element_type=F32)
                new.append((_below(acc, acc_new, r0), _below(later, later[r0:] + both[:, BLK:], r0)))
            return tuple(new)

        def q_block(i, _):
            q0 = i * tq
            zero = jnp.zeros((tq, BLK), F32)
            states = ((zero, zero),) * HP
            for jd in reversed(range(per)):
                states = step(q0, i * per + jd, jd * BLK, True, states)
            states = lax.fori_loop(0, i * per, lambda jj, st: step(q0, i * per - 1 - jj, 0, False, st), states)
            tile = pl.ds(pl.multiple_of(q0, tq), tq)
            for hh in range(HP):
                o, total = states[hh]
                o_ref[tile, _lanes(hh)] = o
                l_ref[hh, tile, :] = total
                att_ref[tile, _lanes(hh)] = (o * _rstd(o) * go_ref[hh]).astype(att_ref.dtype)
            return 0

        lax.fori_loop(0, s // tq, q_block, 0)

    assert n_heads % HP == 0
    groups = n_heads // HP

    def col(part):
        return pl.BlockSpec((s, HP * BLK), lambda g: (0, part * groups + g))

    gain = pl.BlockSpec((1, BLK), lambda g: (0, 0))
    per_head = pl.BlockSpec((HP, 1, BLK), lambda g: (g, 0, 0))
    return pl.pallas_call(
        body, name=name, grid=(groups,),
        in_specs=[col(0), col(1), col(2), gain, gain, per_head],
        out_specs=[pl.BlockSpec((None, s, HP * BLK), lambda g: (0, 0, g)), col(0), pl.BlockSpec((HP, s, BLK), lambda g: (g, 0, 0))],
        out_shape=[jax.ShapeDtypeStruct((2, s, n_heads * BLK), BF16), jax.ShapeDtypeStruct((s, n_heads * BLK), F32),
                   jax.ShapeDtypeStruct((n_heads, s, BLK), F32)],
        scratch_shapes=[pltpu.VMEM((s, HP * BLK), BF16)] * 3,
        compiler_params=_params(("parallel",), VMEM_BIG),
    )(p, p, p, gq.reshape(1, BLK), gk.reshape(1, BLK), go.reshape(n_heads, 1, BLK))


def _attn_bwd(p, o_raw, lsum, dmix, gq, gk, go, n_heads, *, name):
    s = p.shape[0]
    tq = min(TQ_MAX_BWD, s)
    per = tq // BLK
    scale = BLK ** -0.5

    def body(q_ref, k_ref, v_ref, o_ref, l_ref, da_ref, gq_ref, gk_ref, go_ref,
             dqkv_ref, dgq_ref, dgk_ref, dgo_ref, qn, kn, vb, dob, dqn, dkn, dvv):
        for hh in range(HP):
            q = q_ref[:, _lanes(hh)]
            k = k_ref[:, _lanes(hh)]
            qn[:, _lanes(hh)] = (q * _rstd(q) * gq_ref[...] * scale).astype(BF16)
            kn[:, _lanes(hh)] = (k * _rstd(k) * gk_ref[...]).astype(BF16)
            vb[:, _lanes(hh)] = v_ref[:, _lanes(hh)].astype(BF16)
            o = o_ref[:, _lanes(hh)]
            do, dgo_rows = _norm_bwd(o, _rstd(o), go_ref[hh], da_ref[:, _lanes(hh)])
            dob[:, _lanes(hh)] = do.astype(BF16)
            dgo_ref[hh] = jnp.sum(dgo_rows, axis=0, keepdims=True)
        dkn[...] = jnp.zeros_like(dkn)
        dvv[...] = jnp.zeros_like(dvv)
        ones = jnp.ones((BLK, BLK), BF16)
        rhs_gt = jnp.concatenate([(_iota2(0) > _iota2(1)).astype(BF16), ones], axis=1)
        rhs_lt = jnp.concatenate([(_iota2(0) < _iota2(1)).astype(BF16), ones], axis=1)

        def step(q0, j, r0, diag, states):
            n = tq - r0
            rows = pl.ds(pl.multiple_of(q0 + r0, BLK), n)
            cols = pl.ds(pl.multiple_of(j * BLK, BLK), BLK)
            mask = _causal(n, diag)
            zs = [_dot_nt(qn[rows, _lanes(hh)], kn[cols, _lanes(hh)]) for hh in range(HP)]
            das = [_dot_nt(dob[rows, _lanes(hh)], vb[cols, _lanes(hh)]) for hh in range(HP)]
            sums = [_sb_sums(z, mask, rhs_gt) for z in zs]
            mids = []
            for hh in range(HP):
                lb, both = sums[hh]
                upto = states[hh][0][r0:] + both[:, BLK:]
                a = jnp.exp(lb + both[:, :BLK] + (l_ref[hh, rows, :] - upto))
                if diag:
                    a = jnp.where(mask, a, 0.0)
                g = das[hh] * a
                mids.append((lb, upto, a, g, _tri_sum(g, rhs_lt)))
            new = []
            for hh in range(HP):
                seen, gsum, dq = states[hh]
                lb, upto, a, g, bothg = mids[hh]
                beta = jnp.exp(lb)
                dz = g * (1.0 - beta) - beta * (bothg[:, :BLK] + gsum[r0:])
                if diag:
                    dz = jnp.where(mask, dz, 0.0)
                dzs = dz.astype(BF16)
                dq_new = dq[r0:] + jnp.dot(dzs, kn[cols, _lanes(hh)], preferred_element_type=F32)
                dkn[cols, _lanes(hh)] += _dot_tn(dzs, qn[rows, _lanes(hh)])
                dvv[cols, _lanes(hh)] += _dot_tn(a.astype(BF16), dob[rows, _lanes(hh)])
                new.append((_below(seen, upto, r0), _below(gsum, gsum[r0:] + bothg[:, BLK:], r0), _below(dq, dq_new, r0)))
            return tuple(new)

        def q_block(i, _):
            q0 = i * tq
            zero = jnp.zeros((tq, BLK), F32)
            states = ((zero, zero, zero),) * HP
            states = lax.fori_loop(0, i * per, lambda j, st: step(q0, j, 0, False, st), states)
            for jd in range(per):
                states = step(q0, i * per + jd, jd * BLK, True, states)
            tile = pl.ds(pl.multiple_of(q0, tq), tq)
            for hh in range(HP):
                dqn[tile, _lanes(hh)] = states[hh][2]
            return 0

        lax.fori_loop(0, s // tq, q_block, 0)
        for hh in range(HP):
            q = q_ref[:, _lanes(hh)]
            k = k_ref[:, _lanes(hh)]
            dq_raw, dgq_rows = _norm_bwd(q, _rstd(q), gq_ref[...], dqn[:, _lanes(hh)] * scale)
            dk_raw, dgk_rows = _norm_bwd(k, _rstd(k), gk_ref[...], dkn[:, _lanes(hh)])
            dqkv_ref[0, :, _lanes(hh)] = dq_raw.astype(BF16)
            dqkv_ref[1, :, _lanes(hh)] = dk_raw.astype(BF16)
            dqkv_ref[2, :, _lanes(hh)] = dvv[:, _lanes(hh)].astype(BF16)
            dgq_ref[hh] = jnp.sum(dgq_rows, axis=0, keepdims=True)
            dgk_ref[hh] = jnp.sum(dgk_rows, axis=0, keepdims=True)

    assert n_heads % HP == 0
    groups = n_heads // HP

    def col(part):
        return pl.BlockSpec((s, HP * BLK), lambda g: (0, part * groups + g))

    gain = pl.BlockSpec((1, BLK), lambda g: (0, 0))
    per_head = pl.BlockSpec((HP, 1, BLK), lambda g: (g, 0, 0))
    head_gain = jax.ShapeDtypeStruct((n_heads, 1, BLK), F32)
    return pl.pallas_call(
        body, name=name, grid=(groups,),
        in_specs=[col(0), col(1), col(2), col(0), pl.BlockSpec((HP, s, BLK), lambda g: (g, 0, 0)), col(0),
                  gain, gain, per_head],
        out_specs=[pl.BlockSpec((3, s, HP * BLK), lambda g: (0, 0, g)), per_head, per_head, per_head],
        out_shape=[jax.ShapeDtypeStruct((3, s, n_heads * BLK), BF16), head_gain, head_gain, head_gain],
        scratch_shapes=[pltpu.VMEM((s, HP * BLK), BF16)] * 4 + [pltpu.VMEM((s, HP * BLK), F32)] * 3,
        compiler_params=_params(("parallel",), VMEM_ATTN_BWD),
    )(p, p, p, o_raw, lsum, dmix, gq.reshape(1, BLK), gk.reshape(1, BLK), go.reshape(n_heads, 1, BLK))


_INV_SQRT2 = 0.7071067811865476
_INV_SQRT2PI = 0.3989422804014327


def _gelu(x):
    return 0.5 * x * (1.0 + lax.erf(x * _INV_SQRT2))


def _gelu_and_grad(x):
    cdf = 0.5 * (1.0 + lax.erf(x * _INV_SQRT2))
    return x * cdf, cdf + x * jnp.exp(-0.5 * x * x) * _INV_SQRT2PI


def _sgu_fwd(p, mix, w, b, gv, gout, n_heads, *, name):
    s = p.shape[0]
    n_groups = w.shape[0]
    nb = s // BLK
    assert mix.shape == (2, s, n_groups * BLK)

    def body(u_ref, v_ref, w_ref, b_ref, gv_ref, go_ref, _mix_ref, out_ref):
        wt = jnp.where(_iota2(0) >= _iota2(1), w_ref[...], 0.0).astype(BF16)
        bias = b_ref[...]

        def chunk(c, _):
            rows = pl.ds(pl.multiple_of(c * BLK, BLK), BLK)
            u = _gelu(u_ref[rows, :])
            vv = _gelu(v_ref[rows, :])
            vs = vv * _rstd(vv) * gv_ref[...]
            gated = u * (jnp.dot(wt, vs.astype(BF16), preferred_element_type=F32) + bias)
            out_ref[rows, :] = (gated * _rstd(gated) * go_ref[...]).astype(out_ref.dtype)
            return 0

        lax.fori_loop(0, nb, chunk, 0)

    def col(off):
        return pl.BlockSpec((s, BLK), lambda g: (0, off + g))

    per_group = pl.BlockSpec((None, 1, BLK), lambda g: (g, 0, 0))
    return pl.pallas_call(
        body, name=name, grid=(n_groups,),
        in_specs=[col(3 * n_heads), col(3 * n_heads + n_groups), pl.BlockSpec((None, BLK, BLK), lambda g: (g, 0, 0)),
                  pl.BlockSpec((None, BLK, 1), lambda g: (g, 0, 0)), per_group, per_group, ANY],
        out_specs=pl.BlockSpec((None, s, BLK), lambda g: (1, 0, g)), out_shape=jax.ShapeDtypeStruct(mix.shape, mix.dtype),
        input_output_aliases={6: 0}, compiler_params=_params(("parallel",), VMEM_BIG),
    )(p, p, w, b.reshape(n_groups, BLK, 1), gv.reshape(n_groups, 1, BLK), gout.reshape(n_groups, 1, BLK), mix)


def _sgu_bwd(p, dmix, w, b, gv, gout, n_heads, *, name):
    s = p.shape[0]
    n_groups = w.shape[0]
    nb = s // BLK

    def body(u_ref, v_ref, ds_ref, w_ref, b_ref, gv_ref, go_ref, duv_ref, dw_ref, db_ref, dgv_ref, dgo_ref):
        lower = _iota2(0) >= _iota2(1)
        wt = jnp.where(lower, w_ref[...], 0.0).astype(BF16)
        bias = b_ref[...]

        def chunk(c, carry):
            dw, db, dgv, dgo = carry
            rows = pl.ds(pl.multiple_of(c * BLK, BLK), BLK)
            up = u_ref[rows, :]
            vp = v_ref[rows, :]
            u, u_grad = _gelu_and_grad(up)
            vv, vv_grad = _gelu_and_grad(vp)
            rv = _rstd(vv)
            vsb = (vv * rv * gv_ref[...]).astype(BF16)
            mixed = jnp.dot(wt, vsb, preferred_element_type=F32) + bias
            gated = u * mixed
            dgated, dgo_rows = _norm_bwd(gated, _rstd(gated), go_ref[...], ds_ref[rows, :])
            dmixed = dgated * u
            dmb = dmixed.astype(BF16)
            dvs = _dot_tn(wt, dmb)
            dvv, dgv_rows = _norm_bwd(vv, rv, gv_ref[...], dvs)
            duv_ref[0, rows, :] = (dgated * mixed * u_grad).astype(BF16)
            duv_ref[1, rows, :] = (dvv * vv_grad).astype(BF16)
            return (dw + _dot_nt(dmb, vsb), db + jnp.sum(dmixed, axis=1, keepdims=True),
                    dgv + jnp.sum(dgv_rows, axis=0, keepdims=True), dgo + jnp.sum(dgo_rows, axis=0, keepdims=True))

        row0 = jnp.zeros((1, BLK), F32)
        dw, db, dgv, dgo = lax.fori_loop(0, nb, chunk, (jnp.zeros((BLK, BLK), F32), jnp.zeros((BLK, 1), F32), row0, row0))
        dw_ref[...] = jnp.where(lower, dw, 0.0)
        db_ref[...] = db
        dgv_ref[...] = dgv
        dgo_ref[...] = dgo

    def col(off):
        return pl.BlockSpec((s, BLK), lambda g: (0, off + g))

    per_group = pl.BlockSpec((None, 1, BLK), lambda g: (g, 0, 0))
    square = pl.BlockSpec((None, BLK, BLK), lambda g: (g, 0, 0))
    column = pl.BlockSpec((None, BLK, 1), lambda g: (g, 0, 0))
    gain = jax.ShapeDtypeStruct((n_groups, 1, BLK), F32)
    return pl.pallas_call(
        body, name=name, grid=(n_groups,),
        in_specs=[col(3 * n_heads), col(3 * n_heads + n_groups), col(n_heads), square, column, per_group, per_group],
        out_specs=[pl.BlockSpec((2, s, BLK), lambda g: (0, 0, g)), square, column, per_group, per_group],
        out_shape=[jax.ShapeDtypeStruct((2, s, n_groups * BLK), BF16), jax.ShapeDtypeStruct((n_groups, BLK, BLK), F32),
                   jax.ShapeDtypeStruct((n_groups, BLK, 1), F32), gain, gain],
        compiler_params=_params(("parallel",), VMEM_BIG),
    )(p, p, dmix, w, b.reshape(n_groups, BLK, 1), gv.reshape(n_groups, 1, BLK), gout.reshape(n_groups, 1, BLK))


CONV_ROWS = 256
HALO = 8


def _shift_down(ref, r0, n, first):
    cur = ref[pl.ds(r0, n), :]
    prev = jnp.zeros((HALO, cur.shape[1]), F32) if first else ref[pl.ds(r0 - HALO, HALO), :]
    ext = jnp.concatenate([prev, cur], axis=0)
    return pltpu.roll(ext, 1, 0)[HALO:], pltpu.roll(ext, 2, 0)[HALO:], cur


def _shift_up(ref, r0, n, last):
    cur = ref[pl.ds(r0, n), :]
    nxt = jnp.zeros((HALO, cur.shape[1]), F32) if last else ref[pl.ds(r0 + n, HALO), :]
    ext = jnp.concatenate([cur, nxt], axis=0)
    return cur, pltpu.roll(ext, n + HALO - 1, 0)[:n], pltpu.roll(ext, n + HALO - 2, 0)[:n]


def _conv_rows(x1, x2, x0, w_ref, b_ref):
    return ((b_ref[...] + x2 * w_ref[0:1, :]) + x1 * w_ref[1:2, :]) + x0 * w_ref[2:3, :]


def _conv_specs(s, f, tc):
    nf = f // tc
    gate = pl.BlockSpec((s, tc), lambda n: (0, n))
    val = pl.BlockSpec((s, tc), lambda n: (0, nf + n))
    wg = pl.BlockSpec((3, tc), lambda n: (0, n))
    wv = pl.BlockSpec((3, tc), lambda n: (0, nf + n))
    bg = pl.BlockSpec((1, tc), lambda n: (0, n))
    bv = pl.BlockSpec((1, tc), lambda n: (0, nf + n))
    return nf, gate, val, wg, wv, bg, bv


def _conv_fwd(up, cw, cb, *, name):
    s, f2 = up.shape
    f = f2 // 2
    tc = _pick(f, (256, 128))
    cr = min(CONV_ROWS, s)
    nf, gate, val, wg, wv, bg, bv = _conv_specs(s, f, tc)

    def body(g_ref, v_ref, wg_ref, wv_ref, bg_ref, bv_ref, out_ref):
        for r0 in range(0, s, cr):
            gc = _conv_rows(*_shift_down(g_ref, r0, cr, r0 == 0), wg_ref, bg_ref)
            vc = _conv_rows(*_shift_down(v_ref, r0, cr, r0 == 0), wv_ref, bv_ref)
            out_ref[pl.ds(r0, cr), :] = (gc * jax.nn.sigmoid(gc) * vc).astype(out_ref.dtype)

    return pl.pallas_call(
        body, name=name, grid=(nf,), in_specs=[gate, val, wg, wv, bg, bv], out_specs=gate,
        out_shape=jax.ShapeDtypeStruct((s, f), BF16), compiler_params=_params(("parallel",), VMEM_BIG),
    )(up, up, cw, cw, cb.reshape(1, f2), cb.reshape(1, f2))


def _conv_bwd(up, dact, cw, cb, *, name):
    s, f2 = up.shape
    f = f2 // 2
    tc = _pick(f, (256, 128))
    cr = min(CONV_ROWS, s)
    nf, gate, val, wg, wv, bg, bv = _conv_specs(s, f, tc)

    def body(g_ref, v_ref, da_ref, wg_ref, wv_ref, bg_ref, bv_ref, dup_ref, dw_ref, db_ref, dgc, dvc):
        zero = jnp.zeros((1, tc), F32)
        sums = [[zero] * 4, [zero] * 4]
        for r0 in range(0, s, cr):
            rows = pl.ds(r0, cr)
            gx = _shift_down(g_ref, r0, cr, r0 == 0)
            vx = _shift_down(v_ref, r0, cr, r0 == 0)
            gc = _conv_rows(*gx, wg_ref, bg_ref)
            vc = _conv_rows(*vx, wv_ref, bv_ref)
            sig = jax.nn.sigmoid(gc)
            da = da_ref[rows, :]
            d_gate = da * vc * (sig * (1.0 + gc * (1.0 - sig)))
            d_val = da * (gc * sig)
            dgc[rows, :] = d_gate
            dvc[rows, :] = d_val
            for part, (dc, (x1, x2, x0)) in enumerate(((d_gate, gx), (d_val, vx))):
                for tap, xs in enumerate((x2, x1, x0)):
                    sums[part][tap] = sums[part][tap] + jnp.sum(dc * xs, axis=0, keepdims=True)
                sums[part][3] = sums[part][3] + jnp.sum(dc, axis=0, keepdims=True)
        dw_ref[...] = jnp.zeros_like(dw_ref)
        db_ref[...] = jnp.zeros_like(db_ref)
        for part, (dc_ref, w_ref) in enumerate(((dgc, wg_ref), (dvc, wv_ref))):
            for tap in range(3):
                dw_ref[part, tap:tap + 1, :] = sums[part][tap]
            db_ref[part, 0:1, :] = sums[part][3]
            for r0 in range(0, s, cr):
                d0, d1, d2 = _shift_up(dc_ref, r0, cr, r0 + cr == s)
                dup_ref[part, pl.ds(r0, cr), :] = ((d0 * w_ref[2:3, :] + d1 * w_ref[1:2, :]) + d2 * w_ref[0:1, :]).astype(BF16)

    small = pl.BlockSpec((2, 8, tc), lambda n: (0, 0, n))
    return pl.pallas_call(
        body, name=name, grid=(nf,), in_specs=[gate, val, gate, wg, wv, bg, bv],
        out_specs=[pl.BlockSpec((2, s, tc), lambda n: (0, 0, n)), small, small],
        out_shape=[jax.ShapeDtypeStruct((2, s, f), BF16), jax.ShapeDtypeStruct((2, 8, f), F32),
                   jax.ShapeDtypeStruct((2, 8, f), F32)],
        scratch_shapes=[pltpu.VMEM((s, tc), F32)] * 2, compiler_params=_params(("parallel",), VMEM_BIG),
    )(up, up, dact, cw, cw, cb.reshape(1, f2), cb.reshape(1, f2))


def _adamw(w, g, m, v, *, name, dep=None, copy_g=False):
    shape = w.shape
    cols = shape[-1]
    rows = w.size // cols
    if rows * cols * 4 <= (2 << 20):
        tr = rows
    else:
        tr = next(t for t in (1024, 512, 256, 128, 64, 32, 16, 8) if rows % t == 0 and (t * cols * 4 <= (2 << 20) or t == 8))

    n_out = 4 if copy_g else 3

    def body(w_ref, g_ref, m_ref, v_ref, *rest):
        d_ref, nm_ref, nv_ref = rest[-n_out:][:3]
        gr = g_ref[...]
        if copy_g:
            rest[-1][...] = gr
        nm = ADAM_B1 * m_ref[...] + (1.0 - ADAM_B1) * gr
        nv = ADAM_B2 * v_ref[...] + (1.0 - ADAM_B2) * (gr * gr)
        m_hat = nm / (1.0 - ADAM_B1 ** ADAM_STEP)
        v_hat = nv / (1.0 - ADAM_B2 ** ADAM_STEP)
        d_ref[...] = -ADAM_LR * (m_hat / (jnp.sqrt(v_hat) + ADAM_EPS) + ADAM_WD * w_ref[...])
        nm_ref[...] = nm
        nv_ref[...] = nv

    blk = pl.BlockSpec((tr, cols), lambda i: (i, 0))
    out = jax.ShapeDtypeStruct((rows, cols), F32)
    res = pl.pallas_call(
        body, name=name, grid=(rows // tr,), in_specs=[blk] * 4 + ([] if dep is None else [ANY]), out_specs=[blk] * n_out,
        out_shape=[out] * n_out, compiler_params=_params(("parallel",), VMEM_BIG),
    )(*[t.reshape(rows, cols) for t in (w, g, m, v)], *([] if dep is None else [dep]))
    return [t.reshape(shape) for t in res]


def _place():
    x, y, c = lax.axis_index("x"), lax.axis_index("y"), lax.axis_index("c")
    others = [(1 - x, y), (x, 1 - y), (1 - x, 1 - y)]
    return x, y, c, others


def _remote(src, dst, send_sem, recv_sem, device):
    return pltpu.make_async_remote_copy(src_ref=src, dst_ref=dst, send_sem=send_sem, recv_sem=recv_sem, device_id=device,
                                        device_id_type=MESH)


def _hbm_call(body, name, args, out_shapes, n_sems, n_local, aliases=None):
    return pl.pallas_call(
        body, name=name, in_specs=[ANY] * len(args), out_specs=[ANY] * len(out_shapes), out_shape=out_shapes,
        scratch_shapes=[pltpu.SemaphoreType.DMA((n_sems,)), pltpu.SemaphoreType.DMA((n_sems,)),
                        pltpu.SemaphoreType.DMA((max(n_local, 1),))],
        input_output_aliases=aliases or {}, compiler_params=pltpu.CompilerParams(has_side_effects=True),
    )(*args)


def _all_gather_weights(halved, whole, *, name):
    nh, nw = len(halved), len(whole)
    arrays = list(halved) + list(whole)

    def body(*refs):
        srcs, outs = refs[:nh + nw], refs[nh + nw:2 * (nh + nw)]
        send, recv, local = refs[2 * (nh + nw):]
        x, y, c, others = _place()
        me = 2 * x + y
        locals_ = [pltpu.make_async_copy(srcs[nh + a], outs[nh + a].at[me], local.at[a]) for a in range(nw)]
        for cp in locals_:
            cp.start()
        sends = []
        for a in range(nh):
            half = outs[a].shape[1] // 2
            rows = pl.ds(c * half, half)
            for j, (px, py) in enumerate(others):
                sends.append(_remote(outs[a].at[me, rows], outs[a].at[me, rows], send.at[6 * a + j], recv.at[6 * a + j],
                                     (px, py, c)))
        for a in range(nw):
            for j, (px, py) in enumerate(others):
                sends.append(_remote(srcs[nh + a], outs[nh + a].at[me], send.at[6 * nh + 3 * a + j],
                                     recv.at[6 * nh + 3 * a + j], (px, py, c)))
        for cp in sends:
            cp.start()
        for a in range(nh):
            half = outs[a].shape[1] // 2
            rows = pl.ds(c * half, half)
            for j, (px, py) in enumerate(others):
                got = outs[a].at[2 * px + py, rows]
                _remote(got, got, send.at[6 * a + j], recv.at[6 * a + j], (px, py, c)).wait_recv()
                fwd = _remote(got, got, send.at[6 * a + 3 + j], recv.at[6 * a + 3 + j], (x, y, 1 - c))
                fwd.start()
                sends.append(fwd)
        for a in range(nh):
            half = outs[a].shape[1] // 2
            theirs = pl.ds((1 - c) * half, half)
            for j, (px, py) in enumerate(others):
                got = outs[a].at[2 * px + py, theirs]
                _remote(got, got, send.at[6 * a + 3 + j], recv.at[6 * a + 3 + j], (x, y, 1 - c)).wait_recv()
        for a in range(nw):
            for j, (px, py) in enumerate(others):
                got = outs[nh + a].at[2 * px + py]
                _remote(got, got, send.at[6 * nh + 3 * a + j], recv.at[6 * nh + 3 * a + j], (px, py, c)).wait_recv()
        for cp in sends:
            cp.wait_send()
        for cp in locals_:
            cp.wait()

    out_shapes = [jax.ShapeDtypeStruct(t.shape, t.dtype) for t in halved]
    out_shapes += [jax.ShapeDtypeStruct((N_CHIPS,) + t.shape, t.dtype) for t in whole]
    return _hbm_call(body, name, arrays, out_shapes, 6 * nh + 3 * nw, nw, aliases={a: a for a in range(nh)})


def _cast_into(w, layer, place, *, name, dep=None):
    _, r, cols = w.shape
    tr = _row_tile(r, cols)

    def body(place_ref, w_ref, *rest):
        rest[-1][...] = w_ref[...].astype(BF16)

    in_specs, args = [pl.BlockSpec((None, tr, cols), lambda i, pr: (layer, i, 0))], [place, w]
    if dep is not None:
        in_specs.append(ANY)
        args.append(dep)
    grid_spec = pltpu.PrefetchScalarGridSpec(
        num_scalar_prefetch=1, grid=(r // tr,), in_specs=in_specs,
        out_specs=pl.BlockSpec((None, tr, cols), lambda i, pr: (pr[1], i, 0)),
    )
    return pl.pallas_call(
        body, name=name, grid_spec=grid_spec, out_shape=jax.ShapeDtypeStruct((N_CHIPS, r, cols), BF16),
        compiler_params=_params(("parallel",), VMEM_BIG),
    )(*args)


HBM = pl.BlockSpec(memory_space=pltpu.HBM)
SEM = pl.BlockSpec(memory_space=pltpu.SEMAPHORE)
EFFECT = pltpu.SideEffectType.DATAFLOW_SIDE_EFFECTING
TOKEN = jax.ShapeDtypeStruct((8, 128), F32)


def _in_hbm(t):
    return pltpu.with_memory_space_constraint(t, pltpu.HBM)


def _gather_copies(buf, send, recv):
    x, y, c, others = _place()
    half = buf.shape[1] // 2
    rows = pl.ds(c * half, half)
    return [_remote(buf.at[2 * x + y, rows], buf.at[2 * x + y, rows], send.at[j], recv.at[j], (px, py, c))
            for j, (px, py) in enumerate(others)]


def _gather_start(bufs, *, name):
    n = len(bufs)

    def body(*refs):
        ins, sends, recvs, token = refs[:n], refs[n:2 * n], refs[2 * n:3 * n], refs[4 * n]
        for a in range(n):
            for cp in _gather_copies(ins[a], sends[a], recvs[a]):
                cp.start()
        token[...] = jnp.zeros_like(token)

    sems = [pltpu.SemaphoreType.DMA((3,))] * (2 * n)
    res = pl.pallas_call(
        body, name=name, out_shape=sems + [pltpu.HBM(t.shape, t.dtype) for t in bufs] + [TOKEN],
        in_specs=[HBM] * n, out_specs=[SEM] * (2 * n) + [HBM] * n + [pl.BlockSpec(memory_space=pltpu.VMEM)],
        input_output_aliases={a: 2 * n + a for a in range(n)}, compiler_params=pltpu.CompilerParams(has_side_effects=EFFECT),
    )(*[_in_hbm(t) for t in bufs])
    return [(res[2 * n + a], res[a], res[n + a]) for a in range(n)], res[3 * n]


def _gather_wait(state, after, *, name):
    buf, send, recv = state

    def body(buf_ref, send_ref, recv_ref, after_ref, out_ref):
        for cp in _gather_copies(buf_ref, send_ref, recv_ref):
            cp.wait_send()
            cp.wait_recv()

    return pl.pallas_call(
        body, name=name, out_shape=pltpu.HBM(buf.shape, buf.dtype), in_specs=[HBM, SEM, SEM, ANY], out_specs=HBM,
        input_output_aliases={0: 0}, compiler_params=pltpu.CompilerParams(has_side_effects=EFFECT),
    )(buf, send, recv, after)


def _chip_copies(src, land, send, recv):
    _x, _y, c, others = _place()
    return [_remote(src.at[2 * px + py], land.at[j], send.at[j], recv.at[j], (px, py, c)) for j, (px, py) in enumerate(others)]


def _chip_start(partial, *, name):
    def body(src, land, send, recv, _src_thru, _land_thru, token):
        for cp in _chip_copies(src, land, send, recv):
            cp.start()
        token[...] = jnp.zeros_like(token)

    land_shape = (3,) + partial.shape[1:]
    sem = pltpu.SemaphoreType.DMA((3,))
    send, recv, src, land, token = pl.pallas_call(
        body, name=name, out_shape=[sem, sem, pltpu.HBM(partial.shape, partial.dtype), pltpu.HBM(land_shape, partial.dtype), TOKEN],
        in_specs=[HBM, HBM], out_specs=[SEM, SEM, HBM, HBM, pl.BlockSpec(memory_space=pltpu.VMEM)],
        input_output_aliases={0: 2, 1: 3}, compiler_params=pltpu.CompilerParams(has_side_effects=EFFECT),
    )(_in_hbm(partial), _in_hbm(lax.empty(land_shape, partial.dtype)))
    return (src, land, send, recv), token


def _chip_wait(state, after, *, name):
    src, land, send, recv = state

    def body(src_ref, land_ref, send_ref, recv_ref, after_ref, _src_out, _land_out):
        for cp in _chip_copies(src_ref, land_ref, send_ref, recv_ref):
            cp.wait_send()
            cp.wait_recv()

    return pl.pallas_call(
        body, name=name, out_shape=[pltpu.HBM(src.shape, src.dtype), pltpu.HBM(land.shape, land.dtype)],
        in_specs=[HBM, HBM, SEM, SEM, ANY], out_specs=[HBM, HBM], input_output_aliases={0: 0, 1: 1},
        compiler_params=pltpu.CompilerParams(has_side_effects=EFFECT),
    )(src, land, send, recv, after)[1]


def _split_start(bufs, copies, n_copies, *, name):
    n = len(bufs)

    def body(*refs):
        for cp in copies(refs[:n], refs[n], refs[n + 1]):
            cp.start()
        refs[-1][...] = jnp.zeros_like(refs[-1])

    sem = pltpu.SemaphoreType.DMA((n_copies,))
    res = pl.pallas_call(
        body, name=name, out_shape=[sem, sem] + [pltpu.HBM(t.shape, t.dtype) for t in bufs] + [TOKEN],
        in_specs=[HBM] * n, out_specs=[SEM, SEM] + [HBM] * n + [pl.BlockSpec(memory_space=pltpu.VMEM)],
        input_output_aliases={a: 2 + a for a in range(n)}, compiler_params=pltpu.CompilerParams(has_side_effects=EFFECT),
    )(*[_in_hbm(t) for t in bufs])
    return (list(res[2:2 + n]), res[0], res[1]), res[-1]


def _split_wait(state, copies, after, *, name):
    bufs, send, recv = state
    n = len(bufs)

    def body(*refs):
        for cp in copies(refs[:n], refs[n], refs[n + 1]):
            cp.wait_send()
            cp.wait_recv()

    return list(pl.pallas_call(
        body, name=name, out_shape=[pltpu.HBM(t.shape, t.dtype) for t in bufs], in_specs=[HBM] * n + [SEM, SEM, ANY],
        out_specs=[HBM] * n, input_output_aliases={a: a for a in range(n)},
        compiler_params=pltpu.CompilerParams(has_side_effects=EFFECT),
    )(*bufs, send, recv, after))


def _hand_over_copies(refs, send, recv):
    x, y, c, others = _place()
    half = refs[0].shape[1] // 2
    got = [refs[0].at[2 * px + py, pl.ds(c * half, half)] for px, py in others]
    return [_remote(got[j], got[j], send.at[j], recv.at[j], (x, y, 1 - c)) for j in range(3)]


def _pair_copies(refs, send, recv):
    x, y, c, _o = _place()
    half = refs[0].shape[1] // 2
    return [_remote(refs[0].at[:, pl.ds((1 - c) * half, half), :], refs[1], send.at[0], recv.at[0], (x, y, 1 - c))]


def _share_copies(refs, send, recv):
    x, y, c, _o = _place()
    return [_remote(refs[0].at[:, c], refs[0].at[:, c], send.at[0], recv.at[0], (x, y, 1 - c))]


def _small_copies(refs, send, recv):
    x, y, c, others = _place()
    peers = [(x, y, 1 - c)] + [(px, py, pc) for px, py in others for pc in (c, 1 - c)]
    slot = refs[1].at[4 * x + 2 * y + c]
    return [_remote(refs[0], slot, send.at[k], recv.at[k], peer) for k, peer in enumerate(peers)]


def _row_tile(rows, cols):
    return max(t for t in range(16, rows + 1, 16) if rows % t == 0 and (t * cols * 4 <= (4 << 20) or t == 16))


def _pair_sum(grad, theirs, place, *, name):
    _, r, cols = grad.shape
    r2 = r // 2
    tr = _row_tile(r2, cols)
    nr = r2 // tr

    def body(place_ref, g_ref, t_ref, all_ref):
        all_ref[...] = (g_ref[...].astype(F32) + t_ref[...].astype(F32)).astype(all_ref.dtype)

    grid_spec = pltpu.PrefetchScalarGridSpec(
        num_scalar_prefetch=1, grid=(N_CHIPS, nr),
        in_specs=[pl.BlockSpec((None, tr, cols), lambda k, i, pr: (k, pr[0] * nr + i, 0)),
                  pl.BlockSpec((None, tr, cols), lambda k, i, pr: (k, i, 0))],
        out_specs=pl.BlockSpec((None, tr, cols), lambda k, i, pr: (k, i, 0)),
    )
    return pl.pallas_call(
        body, name=name, grid_spec=grid_spec, out_shape=jax.ShapeDtypeStruct((N_CHIPS, r2, cols), BF16),
        compiler_params=_params(("parallel", "parallel"), VMEM_BIG),
    )(place, grad, theirs)


def _chip_sum(grad, theirs, got, place, buf, layer, depth, *, name):
    _, r, cols = grad.shape
    r2 = r // 2
    tr = _row_tile(r2, cols)
    nr = r2 // tr

    def body(place_ref, g_ref, t_ref, got_ref, *rest):
        own = g_ref[...].astype(F32) + t_ref[...].astype(F32)
        rest[-1][...] = ((own + got_ref[0].astype(F32)) + got_ref[1].astype(F32)) + got_ref[2].astype(F32)

    in_specs = [pl.BlockSpec((None, tr, cols), lambda i, pr: (pr[1], pr[0] * nr + i, 0)),
                pl.BlockSpec((None, tr, cols), lambda i, pr: (pr[1], i, 0)),
                pl.BlockSpec((3, tr, cols), lambda i, pr: (0, i, 0))]
    args = [place, grad, theirs, got]
    if buf is not None:
        in_specs.append(ANY)
        args.append(buf)
    grid_spec = pltpu.PrefetchScalarGridSpec(
        num_scalar_prefetch=1, grid=(nr,), in_specs=in_specs,
        out_specs=pl.BlockSpec((None, None, tr, cols), lambda i, pr: (layer, pr[0], i, 0)),
    )
    return pl.pallas_call(
        body, name=name, grid_spec=grid_spec, out_shape=jax.ShapeDtypeStruct((depth, 2, r2, cols), F32),
        input_output_aliases={} if buf is None else {4: 0}, compiler_params=_params(("parallel",), VMEM_BIG),
    )(*args)


def _sum_devices(parts, own, place, *, name):
    _, rows, cols = parts.shape
    tr = rows if N_DEV * rows * cols * 4 <= (16 << 20) else _pick(rows, (256, 128, 64, 32, 16, 8))

    def body(place_ref, p_ref, own_ref, out_ref):
        me = 2 * place_ref[1] + place_ref[0]
        acc = None
        for dev in range(N_DEV):
            term = jnp.where(me == dev, own_ref[...], p_ref[dev])
            acc = term if acc is None else acc + term
        out_ref[...] = acc

    grid_spec = pltpu.PrefetchScalarGridSpec(
        num_scalar_prefetch=1, grid=(rows // tr,),
        in_specs=[pl.BlockSpec((N_DEV, tr, cols), lambda i, pr: (0, i, 0)), pl.BlockSpec((tr, cols), lambda i, pr: (i, 0))],
        out_specs=pl.BlockSpec((tr, cols), lambda i, pr: (i, 0)),
    )
    return pl.pallas_call(
        body, name=name, grid_spec=grid_spec, out_shape=jax.ShapeDtypeStruct((rows, cols), F32),
        compiler_params=_params(("parallel",), VMEM_BIG),
    )(place, parts, own)


def _pack(parts):
    rows = []
    for t in parts:
        flat = t.reshape(-1, 128)
        pad = (-flat.shape[0]) % 8
        rows.append(jnp.pad(flat, ((0, pad), (0, 0))) if pad else flat)
    return jnp.concatenate(rows, axis=0)


def _unpack(pack, shapes):
    out, r0 = [], 0
    for shp in shapes:
        n = math.prod(shp) // 128
        out.append(pack[r0:r0 + n].reshape(shp))
        r0 += n + (-n) % 8
    return out


SMALL = ["attn_norm_g", "q_norm_g", "k_norm_g", "sgu_norm_g", "sgu_w", "sgu_b", "out_norm_a_g", "out_norm_b_g",
         "ffn_norm_g", "conv_b"]
BIG = ["w_in", "w_out", "w_up", "w_down"]
ORDER = ["attn_norm_g", "w_in", "q_norm_g", "k_norm_g", "sgu_norm_g", "sgu_w", "sgu_b", "out_norm_a_g", "out_norm_b_g",
         "w_out", "ffn_norm_g", "w_up", "conv_w", "conv_b", "w_down"]


def kernel(x, attn_norm_g, w_in, q_norm_g, k_norm_g, sgu_norm_g, sgu_w, sgu_b, out_norm_a_g, out_norm_b_g, w_out, ffn_norm_g, w_up, conv_w, conv_b, w_down, loss_target, m_attn_norm_g, m_w_in, m_q_norm_g, m_k_norm_g, m_sgu_norm_g, m_sgu_w, m_sgu_b, m_out_norm_a_g, m_out_norm_b_g, m_w_out, m_ffn_norm_g, m_w_up, m_conv_w, m_conv_b, m_w_down, v_attn_norm_g, v_w_in, v_q_norm_g, v_k_norm_g, v_sgu_norm_g, v_sgu_w, v_sgu_b, v_out_norm_a_g, v_out_norm_b_g, v_w_out, v_ffn_norm_g, v_w_up, v_conv_w, v_conv_b, v_w_down):
    W = dict(attn_norm_g=attn_norm_g, w_in=w_in, q_norm_g=q_norm_g, k_norm_g=k_norm_g, sgu_norm_g=sgu_norm_g, sgu_w=sgu_w,
             sgu_b=sgu_b, out_norm_a_g=out_norm_a_g, out_norm_b_g=out_norm_b_g, w_out=w_out, ffn_norm_g=ffn_norm_g, w_up=w_up,
             conv_w=conv_w, conv_b=conv_b, w_down=w_down)
    M = dict(attn_norm_g=m_attn_norm_g, w_in=m_w_in, q_norm_g=m_q_norm_g, k_norm_g=m_k_norm_g, sgu_norm_g=m_sgu_norm_g,
             sgu_w=m_sgu_w, sgu_b=m_sgu_b, out_norm_a_g=m_out_norm_a_g, out_norm_b_g=m_out_norm_b_g, w_out=m_w_out,
             ffn_norm_g=m_ffn_norm_g, w_up=m_w_up, conv_w=m_conv_w, conv_b=m_conv_b, w_down=m_w_down)
    V = dict(attn_norm_g=v_attn_norm_g, w_in=v_w_in, q_norm_g=v_q_norm_g, k_norm_g=v_k_norm_g, sgu_norm_g=v_sgu_norm_g,
             sgu_w=v_sgu_w, sgu_b=v_sgu_b, out_norm_a_g=v_out_norm_a_g, out_norm_b_g=v_out_norm_b_g, w_out=v_w_out,
             ffn_norm_g=v_ffn_norm_g, w_up=v_w_up, conv_w=v_conv_w, conv_b=v_conv_b, w_down=v_w_down)
    depth = w_in.shape[0]
    s, d = x.shape[1], x.shape[2]
    n_heads = out_norm_a_g.shape[1]
    core = lax.axis_index("c")
    chip = 2 * lax.axis_index("x") + lax.axis_index("y")
    place = jnp.stack([core, chip]).astype(jnp.int32)
    xs = x.reshape(s, d)

    f_local = conv_w.shape[2]
    taps = lax.dynamic_update_slice(jnp.zeros((N_CHIPS, 16, f_local), F32), conv_w.reshape(1, depth * 3, f_local),
                                    (chip, 0, 0))
    order = [(l, n) for l in range(depth) for n in BIG]
    first, token = _gather_start([_cast_into(w_in, 0, place, name="cast_w_in"), taps], name="gather_start_first")
    rest, token = _gather_start([_cast_into(W[n], l, place, dep=token, name=f"cast_{n}") for l, n in order[1:]],
                                name="gather_start_rest")
    states = dict(zip(order, [first[0]] + rest))

    states["taps"] = first[1]

    def landed(key, after, tag):
        buf = _gather_wait(states[key], after, name=f"gather_wait_{tag}")
        return _split_start([buf], _hand_over_copies, 3, name=f"hand_over_{tag}")

    def whole(state, after, tag):
        return _split_wait(state, _hand_over_copies, after, name=f"hand_over_wait_{tag}")[0]

    saved, full = [], []
    cur = xs
    for l in range(depth):
        gain = attn_norm_g[l] + token[0, 0] if l == 0 else attn_norm_g[l]
        h = _rmsnorm_fwd(cur, gain, name="attn_norm")
        if l == 0:
            ho_in, _ = landed((0, "w_in"), h, "w_in")
        w_in_l = whole(ho_in, h, "w_in")
        if l == 0:
            ho_taps, token = landed("taps", w_in_l, "taps")
        p = _mm(h, w_in_l, "nn", b_split=N_CHIPS, caps=(2048, 256, 2048), dep=token, name="proj_in")
        if l == 0:
            taps = whole(ho_taps, p, "taps")[:, :depth * 3].reshape(N_CHIPS, depth, 3, f_local)
            cw_full = jnp.transpose(taps, (1, 2, 0, 3)).reshape(depth, 3, N_CHIPS * f_local)
        mix, o_raw, lsum = _attn_fwd(p, q_norm_g[l], k_norm_g[l], out_norm_a_g[l], n_heads, name="attn_fwd")
        ho_out, token = landed((l, "w_out"), mix, "w_out")
        mix = _sgu_fwd(p, mix, sgu_w[l], sgu_b[l], sgu_norm_g[l] + token[0, 0], out_norm_b_g[l], n_heads, name="sgu_fwd")
        w_out_l = whole(ho_out, mix, "w_out").reshape(-1, d)
        ho_up, token = landed((l, "w_up"), w_out_l, "w_up")
        x1 = _mm(mix, w_out_l, "nn", a_split=2, res=cur, caps=(2048, 512, 1024), dep=token, name="proj_out")
        h2 = _rmsnorm_fwd(x1, ffn_norm_g[l], name="ffn_norm")
        w_up_l = whole(ho_up, h2, "w_up")
        up = _mm(h2, w_up_l, "nn", b_split=N_CHIPS, caps=(2048, 256, 2048), name="ffn_up")
        ho_down, token = landed((l, "w_down"), up, "w_down")
        act = _conv_fwd(up, cw_full[l], conv_b[l] + token[0, 0], name="conv_fwd")
        w_down_l = whole(ho_down, act, "w_down").reshape(-1, d)
        if l + 1 < depth:
            ho_in, token = landed((l + 1, "w_in"), w_down_l, "w_in")
        x2 = _mm(act, w_down_l, "nn", res=x1, caps=(1024, 512, 2816), dep=token, name="ffn_down")
        full.append(dict(w_in=w_in_l, w_out=w_out_l, w_up=w_up_l, w_down=w_down_l, conv_w=cw_full[l]))
        saved.append(dict(x0=cur, h=h, p=p, o_raw=o_raw, lsum=lsum, mix=mix, x1=x1, h2=h2, up=up, act=act))
        cur = x2

    dx, dxb, sq = _loss_head(cur, loss_target.reshape(s, d), name="loss_head")
    loss = lax.psum(sq[0, 0] * (0.5 / d), ("x", "y", "c"))

    small_grads = {n: [None] * depth for n in SMALL + ["conv_w"]}
    def pair_begin(n, grad):
        land = lax.empty((N_CHIPS, grad.shape[1] // 2, grad.shape[2]), grad.dtype)
        return _split_start([grad, land], _pair_copies, 1, name=f"pair_start_{n}")

    def chip_begin(n, state, after):
        grad, theirs = _split_wait(state, _pair_copies, after, name=f"pair_wait_{n}")
        state, tok = _chip_start(_pair_sum(grad, theirs, place, name=f"pair_sum_{n}"), name=f"chip_start_{n}")
        return (grad, theirs, state), tok

    pending = {}
    for l in reversed(range(depth)):
        fw, sv = full[l], saved[l]
        dact = _mm(dxb, fw["w_down"], "nt", caps=(2048, 512, 2048), name="d_act")
        g_down = _mm(sv["act"], dxb, "tn", caps=(512, 2048, 2048), out_dtype=BF16, name="g_down")
        pair, tok = pair_begin("w_down", g_down.reshape(N_CHIPS, -1, d))
        dup, dcw, dcb = _conv_bwd(sv["up"], dact, fw["conv_w"], conv_b[l] + tok[0, 0], name="conv_bwd")
        pending[(l, "w_down")], tok = chip_begin("w_down", pair, dup)
        g_up = _mm(sv["h2"], dup, "tn", b_split=2, o_split=N_CHIPS, caps=(2048, 256, 2048), out_dtype=BF16, dep=tok,
                   name="g_up")
        pair, tok = pair_begin("w_up", g_up)
        dh2 = _mm(dup, fw["w_up"], "nt", a_split=2, b_split=N_CHIPS, caps=(1024, 512, 2816), dep=tok, name="d_h2")
        pending[(l, "w_up")], tok = chip_begin("w_up", pair, dh2)
        dx1, dx1b, dg_ffn = _rmsnorm_bwd(sv["x1"], ffn_norm_g[l] + tok[0, 0], dh2, dx, name="ffn_norm_bwd")
        dmix = _mm(dx1b, fw["w_out"], "nt", caps=(2048, 512, 2048), name="d_mix")
        g_out = _mm(sv["mix"], dx1b, "tn", a_split=2, caps=(512, 2048, 2048), out_dtype=BF16, name="g_out")
        pair, tok = pair_begin("w_out", g_out.reshape(N_CHIPS, -1, d))
        dqkv, dgq, dgk, dgoa = _attn_bwd(sv["p"], sv["o_raw"], sv["lsum"], dmix, q_norm_g[l] + tok[0, 0], k_norm_g[l],
                                         out_norm_a_g[l], n_heads, name="attn_bwd")
        pending[(l, "w_out")], tok = chip_begin("w_out", pair, dqkv)
        duv, dsw, dsb, dgv, dgob = _sgu_bwd(sv["p"], dmix, sgu_w[l], sgu_b[l], sgu_norm_g[l] + tok[0, 0], out_norm_b_g[l],
                                            n_heads, name="sgu_bwd")
        dp = jnp.concatenate([dqkv[0], dqkv[1], dqkv[2], duv[0], duv[1]], axis=1)
        g_in = _mm(sv["h"], dp, "tn", o_split=N_CHIPS, caps=(2048, 256, 2048), out_dtype=BF16, name="g_in")
        pair, tok = pair_begin("w_in", g_in)
        dh = _mm(dp, fw["w_in"], "nt", b_split=N_CHIPS, caps=(2048, 512, 1280), dep=tok, name="d_h")
        pending[(l, "w_in")], tok = chip_begin("w_in", pair, dh)
        dx, dxb, dg_attn = _rmsnorm_bwd(sv["x0"], attn_norm_g[l] + tok[0, 0], dh, dx1, name="attn_norm_bwd")

        small_grads["attn_norm_g"][l] = dg_attn.reshape(d)
        small_grads["q_norm_g"][l] = jnp.sum(dgq, axis=(0, 1))
        small_grads["k_norm_g"][l] = jnp.sum(dgk, axis=(0, 1))
        small_grads["sgu_norm_g"][l] = dgv.reshape(-1, BLK)
        small_grads["sgu_w"][l] = dsw
        small_grads["sgu_b"][l] = dsb.reshape(-1, BLK)
        small_grads["out_norm_a_g"][l] = dgoa.reshape(-1, BLK)
        small_grads["out_norm_b_g"][l] = dgob.reshape(-1, BLK)
        small_grads["ffn_norm_g"][l] = dg_ffn.reshape(d)
        small_grads["conv_b"][l] = dcb[:, 0, :].reshape(-1)
        small_grads["conv_w"][l] = jnp.transpose(dcw[:, :3, :], (1, 0, 2)).reshape(3, -1)

    names = SMALL + ["conv_w"]
    pack = _pack([jnp.stack(small_grads[n]) for n in names])
    small, tok = _split_start([pack, lax.empty((N_DEV,) + pack.shape, F32)], _small_copies, N_DEV - 1, name="small_start")
    G, D_, NM, NV = {}, {}, {}, {}
    after, prev = tok, None
    for n in ("w_down", "w_up", "w_out", "w_in"):
        buf = None
        for l in reversed(range(depth)):
            grad, theirs, state = pending[(l, n)]
            got = _chip_wait(state, after, name=f"chip_wait_{n}")
            buf = _chip_sum(grad, theirs, got, place, buf, l, depth, name=f"chip_sum_{n}")
            after = buf
        share, tok = _split_start([buf], _share_copies, 1, name=f"share_start_{n}")
        if prev is not None:
            D_[prev], NM[prev], NV[prev], G[prev] = _adamw(W[prev], G[prev], M[prev], V[prev], dep=tok, copy_g=True,
                                                           name=f"adamw_{prev}")
            after = NV[prev]
        G[n] = _split_wait(share, _share_copies, after, name=f"share_wait_{n}")[0].reshape(W[n].shape)
        after, prev = G[n], n
    D_[prev], NM[prev], NV[prev], G[prev] = _adamw(W[prev], G[prev], M[prev], V[prev], copy_g=True, name=f"adamw_{prev}")

    pack, parts = _split_wait(small, _small_copies, NV[prev], name="small_wait")
    total = _sum_devices(parts, pack, place, name="sum_small")
    f_full = conv_b.shape[1]
    shapes = [W[n].shape for n in SMALL] + [(depth, 3, f_full)]
    for n, t in zip(names, _unpack(total, shapes)):
        G[n] = t
    G["conv_w"] = lax.dynamic_slice_in_dim(G["conv_w"], chip * f_local, f_local, axis=2)

    D_["conv_w"], NM["conv_w"], NV["conv_w"] = _adamw(conv_w, G["conv_w"], m_conv_w, v_conv_w, name="adamw_conv_w")
    small_shapes = [W[n].shape for n in SMALL]
    res = _adamw(_pack([W[n] for n in SMALL]), _pack([G[n] for n in SMALL]), _pack([M[n] for n in SMALL]),
                 _pack([V[n] for n in SMALL]), name="adamw_small")
    for dst, t in zip((D_, NM, NV), res):
        for n, u in zip(SMALL, _unpack(t, small_shapes)):
            dst[n] = u

    return (loss, dx.reshape(x.shape), *[G[n] for n in ORDER], *[D_[n] for n in ORDER], *[NM[n] for n in ORDER],
            *[NV[n] for n in ORDER])
```

```python
import functools
import math

import jax
import jax.numpy as jnp
from jax import lax
from jax.experimental import pallas as pl
from jax.experimental.pallas import tpu as pltpu

F32 = jnp.float32
BF16 = jnp.bfloat16
EPS = 1e-6
BLK = 128
N_CHIPS = 4
N_DEV = 8
ADAM_LR, ADAM_B1, ADAM_B2, ADAM_EPS, ADAM_WD, ADAM_STEP = 0.001, 0.9, 0.999, 1e-08, 0.01, 10
VMEM_BIG = 48 * 1024 * 1024
MESH = pl.DeviceIdType.MESH
ANY = pl.BlockSpec(memory_space=pl.ANY)


def _pick(dim, prefs):
    for t in prefs:
        if dim % t == 0:
            return t
    raise ValueError(f"no tile in {prefs} divides {dim}")


def _params(sem=None, vmem=None):
    return pltpu.CompilerParams(dimension_semantics=sem, vmem_limit_bytes=vmem)


def _ldims(arr, split):
    if split == 1:
        return arr.shape
    p, r, cs = arr.shape
    assert p == split
    return (r, p * cs)


def _spec(tr, tc, split, cols, rc):
    if split == 1:
        return pl.BlockSpec((tr, tc), lambda i, j, k: rc(i, j, k))
    per = (cols // split) // tc

    def imap(i, j, k):
        r, c = rc(i, j, k)
        return (c // per, r, c % per)

    return pl.BlockSpec((None, tr, tc), imap)


def _fit(unit, cap):
    return max(t for t in range(128, min(unit, cap) + 1, 128) if unit % t == 0)


def _mm(a, b, mode, *, name, caps, a_split=1, b_split=1, o_split=1, out_dtype=F32, res=None, dep=None):
    ar, ac = _ldims(a, a_split)
    br, bc = _ldims(b, b_split)
    if mode == "nn":
        m, k, n = ar, ac, bc
        assert br == k
        ku, nu, mu = math.gcd(k // a_split, k), math.gcd(n // b_split, n // o_split), m
    elif mode == "nt":
        m, k, n = ar, ac, br
        assert bc == k
        ku, nu, mu = math.gcd(k // a_split, k // b_split), n // o_split, m
    else:
        k, m, n = ar, ac, bc
        assert br == k
        ku, nu, mu = k, math.gcd(n // b_split, n // o_split), m // a_split
    tm, tn, tk = _fit(mu, caps[0]), _fit(nu, caps[1]), _fit(ku, caps[2])
    nk = k // tk
    if mode == "nn":
        a_spec = _spec(tm, tk, a_split, k, lambda i, j, kk: (i, kk))
        b_spec = _spec(tk, tn, b_split, n, lambda i, j, kk: (kk, j))
    elif mode == "nt":
        a_spec = _spec(tm, tk, a_split, k, lambda i, j, kk: (i, kk))
        b_spec = _spec(tn, tk, b_split, k, lambda i, j, kk: (j, kk))
    else:
        a_spec = _spec(tk, tm, a_split, m, lambda i, j, kk: (kk, i))
        b_spec = _spec(tk, tn, b_split, n, lambda i, j, kk: (kk, j))
    o_spec = _spec(tm, tn, o_split, n, lambda i, j, kk: (i, j))
    dims = {"nn": (((1,), (0,)), ((), ())), "nt": (((1,), (1,)), ((), ())), "tn": (((0,), (0,)), ((), ()))}[mode]

    def body(a_ref, b_ref, *rest):
        if dep is not None:
            rest = rest[1:]
        if res is None:
            o_ref, acc = rest
        else:
            r_ref, o_ref, acc = rest
        kk = pl.program_id(2)

        @pl.when(kk == 0)
        def _():
            acc[...] = jnp.zeros_like(acc)

        acc[...] += lax.dot_general(a_ref[...].astype(BF16), b_ref[...].astype(BF16), dims, preferred_element_type=F32)

        @pl.when(kk == nk - 1)
        def _():
            out = acc[...]
            if res is not None:
                out = out + r_ref[...]
            o_ref[...] = out.astype(o_ref.dtype)

    in_specs, args = [a_spec, b_spec], [a, b]
    if dep is not None:
        in_specs.append(ANY)
        args.append(dep)
    if res is not None:
        in_specs.append(pl.BlockSpec((tm, tn), lambda i, j, kk: (i, j)))
        args.append(res)
    out_shape = (m, n) if o_split == 1 else (o_split, m, n // o_split)
    return pl.pallas_call(
        body, name=name, grid=(m // tm, n // tn, nk), in_specs=in_specs, out_specs=o_spec,
        out_shape=jax.ShapeDtypeStruct(out_shape, out_dtype), scratch_shapes=[pltpu.VMEM((tm, tn), F32)],
        compiler_params=_params(("parallel", "parallel", "arbitrary"), VMEM_BIG),
    )(*args)


def _rstd(v):
    return lax.rsqrt(jnp.mean(v * v, axis=-1, keepdims=True) + EPS)


def _norm_bwd(v, r, gain, dout):
    a = dout * gain
    dv = r * (a - v * (r * r * jnp.mean(a * v, axis=-1, keepdims=True)))
    return dv, dout * v * r


def _rmsnorm_fwd(x, g, *, name):
    s, d = x.shape
    tr = _pick(s, (256, 128))

    def body(x_ref, g_ref, o_ref):
        v = x_ref[...]
        o_ref[...] = (v * _rstd(v) * g_ref[...]).astype(o_ref.dtype)

    return pl.pallas_call(
        body, name=name, grid=(s // tr,),
        in_specs=[pl.BlockSpec((tr, d), lambda i: (i, 0)), pl.BlockSpec((1, d), lambda i: (0, 0))],
        out_specs=pl.BlockSpec((tr, d), lambda i: (i, 0)), out_shape=jax.ShapeDtypeStruct((s, d), BF16),
        compiler_params=_params(("parallel",)),
    )(x, g.reshape(1, d))


def _rmsnorm_bwd(x, g, dh, dres, *, name):
    s, d = x.shape
    tr = _pick(s, (256, 128))

    def body(x_ref, g_ref, dh_ref, dres_ref, dx_ref, dxb_ref, dg_ref):
        v = x_ref[...]
        dv, dgr = _norm_bwd(v, _rstd(v), g_ref[...], dh_ref[...])
        dx = dres_ref[...] + dv
        dx_ref[...] = dx
        dxb_ref[...] = dx.astype(BF16)
        part = jnp.sum(dgr, axis=0, keepdims=True)

        @pl.when(pl.program_id(0) == 0)
        def _():
            dg_ref[...] = part

        @pl.when(pl.program_id(0) > 0)
        def _():
            dg_ref[...] += part

    row = pl.BlockSpec((tr, d), lambda i: (i, 0))
    one = pl.BlockSpec((1, d), lambda i: (0, 0))
    return pl.pallas_call(
        body, name=name, grid=(s // tr,), in_specs=[row, one, row, row], out_specs=[row, row, one],
        out_shape=[jax.ShapeDtypeStruct((s, d), F32), jax.ShapeDtypeStruct((s, d), BF16), jax.ShapeDtypeStruct((1, d), F32)],
        compiler_params=_params(("arbitrary",)),
    )(x, g.reshape(1, d), dh, dres)


def _loss_head(y, target, *, name):
    s, d = y.shape
    tr = _pick(s, (256, 128))

    def body(y_ref, t_ref, dy_ref, dyb_ref, ls_ref):
        e = y_ref[...] - t_ref[...]
        dy = e * (1.0 / d)
        dy_ref[...] = dy
        dyb_ref[...] = dy.astype(BF16)
        part = jnp.full(ls_ref.shape, jnp.sum(e * e), F32)

        @pl.when(pl.program_id(0) == 0)
        def _():
            ls_ref[...] = part

        @pl.when(pl.program_id(0) > 0)
        def _():
            ls_ref[...] += part

    row = pl.BlockSpec((tr, d), lambda i: (i, 0))
    return pl.pallas_call(
        body, name=name, grid=(s // tr,), in_specs=[row, row], out_specs=[row, row, pl.BlockSpec((8, 128), lambda i: (0, 0))],
        out_shape=[jax.ShapeDtypeStruct((s, d), F32), jax.ShapeDtypeStruct((s, d), BF16), jax.ShapeDtypeStruct((8, 128), F32)],
        compiler_params=_params(("arbitrary",)),
    )(y, target)


def _iota2(axis):
    return lax.broadcasted_iota(jnp.int32, (BLK, BLK), axis)


def _tri_sum(v, tri):
    hi = v.astype(BF16)
    lo = (v - hi.astype(F32)).astype(BF16)
    return jnp.dot(hi, tri, preferred_element_type=F32) + jnp.dot(lo, tri, preferred_element_type=F32)


def _dot_nt(a, b):
    return lax.dot_general(a, b, (((1,), (1,)), ((), ())), preferred_element_type=F32)


def _dot_tn(a, b):
    return lax.dot_general(a, b, (((0,), (0,)), ((), ())), preferred_element_type=F32)


TQ_MAX = 1024
TQ_MAX_BWD = 512
HP = 2
VMEM_ATTN_BWD = 56 * 1024 * 1024


def _lanes(hh):
    return slice(hh * BLK, (hh + 1) * BLK)


def _causal(n, diag):
    if not diag:
        return None
    return lax.broadcasted_iota(jnp.int32, (n, BLK), 1) < lax.broadcasted_iota(jnp.int32, (n, BLK), 0)


def _sb_sums(z, mask, rhs_gt):
    lb = jnp.minimum(z, 0.0) - jnp.log(1.0 + jnp.exp(-jnp.abs(z)))
    l1m = lb - z
    if mask is not None:
        l1m = jnp.where(mask, l1m, 0.0)
    return lb, _tri_sum(l1m, rhs_gt)


def _below(old, new, r0):
    return new if r0 == 0 else jnp.concatenate([old[:r0], new], axis=0)


def _attn_fwd(p, gq, gk, go, n_heads, *, name):
    s = p.shape[0]
    tq = min(TQ_MAX, s)
    per = tq // BLK
    scale = BLK ** -0.5

    def body(q_ref, k_ref, v_ref, gq_ref, gk_ref, go_ref, att_ref, o_ref, l_ref, qn, kn, vb):
        for hh in range(HP):
            q = q_ref[:, _lanes(hh)]
            k = k_ref[:, _lanes(hh)]
            qn[:, _lanes(hh)] = (q * _rstd(q) * gq_ref[...] * scale).astype(BF16)
            kn[:, _lanes(hh)] = (k * _rstd(k) * gk_ref[...]).astype(BF16)
            vb[:, _lanes(hh)] = v_ref[:, _lanes(hh)].astype(BF16)
        rhs_gt = jnp.concatenate([(_iota2(0) > _iota2(1)).astype(BF16), jnp.ones((BLK, BLK), BF16)], axis=1)

        def step(q0, j, r0, diag, states):
            n = tq - r0
            rows = pl.ds(pl.multiple_of(q0 + r0, BLK), n)
            cols = pl.ds(pl.multiple_of(j * BLK, BLK), BLK)
            mask = _causal(n, diag)
            zs = [_dot_nt(qn[rows, _lanes(hh)], kn[cols, _lanes(hh)]) for hh in range(HP)]
            sums = [_sb_sums(z, mask, rhs_gt) for z in zs]
            new = []
            for hh in range(HP):
                acc, later = states[hh]
                lb, both = sums[hh]
                a = jnp.exp(lb + both[:, :BLK] + later[r0:])
                if diag:
                    a = jnp.where(mask, a, 0.0)
                acc_new = acc[r0:] + jnp.dot(a.astype(BF16), vb[cols, _lanes(hh)], preferred_element_type=F32)
                new.append((_below(acc, acc_new, r0), _below(later, later[r0:] + both[:, BLK:], r0)))
            return tuple(new)

        def q_block(i, _):
            q0 = i * tq
            zero = jnp.zeros((tq, BLK), F32)
            states = ((zero, zero),) * HP
            for jd in reversed(range(per)):
                states = step(q0, i * per + jd, jd * BLK, True, states)
            states = lax.fori_loop(0, i * per, lambda jj, st: step(q0, i * per - 1 - jj, 0, False, st), states)
            tile = pl.ds(pl.multiple_of(q0, tq), tq)
            for hh in range(HP):
                o, total = states[hh]
                o_ref[tile, _lanes(hh)] = o
                l_ref[hh, tile, :] = total
                att_ref[tile, _lanes(hh)] = (o * _rstd(o) * go_ref[hh]).astype(att_ref.dtype)
            return 0

        lax.fori_loop(0, s // tq, q_block, 0)

    assert n_heads % HP == 0
    groups = n_heads // HP

    def col(part):
        return pl.BlockSpec((s, HP * BLK), lambda g: (0, part * groups + g))

    gain = pl.BlockSpec((1, BLK), lambda g: (0, 0))
    per_head = pl.BlockSpec((HP, 1, BLK), lambda g: (g, 0, 0))
    return pl.pallas_call(
        body, name=name, grid=(groups,),
        in_specs=[col(0), col(1), col(2), gain, gain, per_head],
        out_specs=[pl.BlockSpec((None, s, HP * BLK), lambda g: (0, 0, g)), col(0), pl.BlockSpec((HP, s, BLK), lambda g: (g, 0, 0))],
        out_shape=[jax.ShapeDtypeStruct((2, s, n_heads * BLK), BF16), jax.ShapeDtypeStruct((s, n_heads * BLK), F32),
                   jax.ShapeDtypeStruct((n_heads, s, BLK), F32)],
        scratch_shapes=[pltpu.VMEM((s, HP * BLK), BF16)] * 3,
        compiler_params=_params(("parallel",), VMEM_BIG),
    )(p, p, p, gq.reshape(1, BLK), gk.reshape(1, BLK), go.reshape(n_heads, 1, BLK))


def _attn_bwd(p, o_raw, lsum, dmix, gq, gk, go, n_heads, *, name):
    s = p.shape[0]
    tq = min(TQ_MAX_BWD, s)
    per = tq // BLK
    scale = BLK ** -0.5

    def body(q_ref, k_ref, v_ref, o_ref, l_ref, da_ref, gq_ref, gk_ref, go_ref,
             dqkv_ref, dgq_ref, dgk_ref, dgo_ref, qn, kn, vb, dob, dqn, dkn, dvv):
        for hh in range(HP):
            q = q_ref[:, _lanes(hh)]
            k = k_ref[:, _lanes(hh)]
            qn[:, _lanes(hh)] = (q * _rstd(q) * gq_ref[...] * scale).astype(BF16)
            kn[:, _lanes(hh)] = (k * _rstd(k) * gk_ref[...]).astype(BF16)
            vb[:, _lanes(hh)] = v_ref[:, _lanes(hh)].astype(BF16)
            o = o_ref[:, _lanes(hh)]
            do, dgo_rows = _norm_bwd(o, _rstd(o), go_ref[hh], da_ref[:, _lanes(hh)])
            dob[:, _lanes(hh)] = do.astype(BF16)
            dgo_ref[hh] = jnp.sum(dgo_rows, axis=0, keepdims=True)
        dkn[...] = jnp.zeros_like(dkn)
        dvv[...] = jnp.zeros_like(dvv)
        ones = jnp.ones((BLK, BLK), BF16)
        rhs_gt = jnp.concatenate([(_iota2(0) > _iota2(1)).astype(BF16), ones], axis=1)
        rhs_lt = jnp.concatenate([(_iota2(0) < _iota2(1)).astype(BF16), ones], axis=1)

        def step(q0, j, r0, diag, states):
            n = tq - r0
            rows = pl.ds(pl.multiple_of(q0 + r0, BLK), n)
            cols = pl.ds(pl.multiple_of(j * BLK, BLK), BLK)
            mask = _causal(n, diag)
            zs = [_dot_nt(qn[rows, _lanes(hh)], kn[cols, _lanes(hh)]) for hh in range(HP)]
            das = [_dot_nt(dob[rows, _lanes(hh)], vb[cols, _lanes(hh)]) for hh in range(HP)]
            sums = [_sb_sums(z, mask, rhs_gt) for z in zs]
            mids = []
            for hh in range(HP):
                lb, both = sums[hh]
                upto = states[hh][0][r0:] + both[:, BLK:]
                a = jnp.exp(lb + both[:, :BLK] + (l_ref[hh, rows, :] - upto))
                if diag:
                    a = jnp.where(mask, a, 0.0)
                g = das[hh] * a
                mids.append((lb, upto, a, g, _tri_sum(g, rhs_lt)))
            new = []
            for hh in range(HP):
                seen, gsum, dq = states[hh]
                lb, upto, a, g, bothg = mids[hh]
                beta = jnp.exp(lb)
                dz = g * (1.0 - beta) - beta * (bothg[:, :BLK] + gsum[r0:])
                if diag:
                    dz = jnp.where(mask, dz, 0.0)
                dzs = dz.astype(BF16)
                dq_new = dq[r0:] + jnp.dot(dzs, kn[cols, _lanes(hh)], preferred_element_type=F32)
                dkn[cols, _lanes(hh)] += _dot_tn(dzs, qn[rows, _lanes(hh)])
                dvv[cols, _lanes(hh)] += _dot_tn(a.astype(BF16), dob[rows, _lanes(hh)])
                new.append((_below(seen, upto, r0), _below(gsum, gsum[r0:] + bothg[:, BLK:], r0), _below(dq, dq_new, r0)))
            return tuple(new)

        def q_block(i, _):
            q0 = i * tq
            zero = jnp.zeros((tq, BLK), F32)
            states = ((zero, zero, zero),) * HP
            states = lax.fori_loop(0, i * per, lambda j, st: step(q0, j, 0, False, st), states)
            for jd in range(per):
                states = step(q0, i * per + jd, jd * BLK, True, states)
            tile = pl.ds(pl.multiple_of(q0, tq), tq)
            for hh in range(HP):
                dqn[tile, _lanes(hh)] = states[hh][2]
            return 0

        lax.fori_loop(0, s // tq, q_block, 0)
        for hh in range(HP):
            q = q_ref[:, _lanes(hh)]
            k = k_ref[:, _lanes(hh)]
            dq_raw, dgq_rows = _norm_bwd(q, _rstd(q), gq_ref[...], dqn[:, _lanes(hh)] * scale)
            dk_raw, dgk_rows = _norm_bwd(k, _rstd(k), gk_ref[...], dkn[:, _lanes(hh)])
            dqkv_ref[0, :, _lanes(hh)] = dq_raw.astype(BF16)
            dqkv_ref[1, :, _lanes(hh)] = dk_raw.astype(BF16)
            dqkv_ref[2, :, _lanes(hh)] = dvv[:, _lanes(hh)].astype(BF16)
            dgq_ref[hh] = jnp.sum(dgq_rows, axis=0, keepdims=True)
            dgk_ref[hh] = jnp.sum(dgk_rows, axis=0, keepdims=True)

    assert n_heads % HP == 0
    groups = n_heads // HP

    def col(part):
        return pl.BlockSpec((s, HP * BLK), lambda g: (0, part * groups + g))

    gain = pl.BlockSpec((1, BLK), lambda g: (0, 0))
    per_head = pl.BlockSpec((HP, 1, BLK), lambda g: (g, 0, 0))
    head_gain = jax.ShapeDtypeStruct((n_heads, 1, BLK), F32)
    return pl.pallas_call(
        body, name=name, grid=(groups,),
        in_specs=[col(0), col(1), col(2), col(0), pl.BlockSpec((HP, s, BLK), lambda g: (g, 0, 0)), col(0),
                  gain, gain, per_head],
        out_specs=[pl.BlockSpec((3, s, HP * BLK), lambda g: (0, 0, g)), per_head, per_head, per_head],
        out_shape=[jax.ShapeDtypeStruct((3, s, n_heads * BLK), BF16), head_gain, head_gain, head_gain],
        scratch_shapes=[pltpu.VMEM((s, HP * BLK), BF16)] * 4 + [pltpu.VMEM((s, HP * BLK), F32)] * 3,
        compiler_params=_params(("parallel",), VMEM_ATTN_BWD),
    )(p, p, p, o_raw, lsum, dmix, gq.reshape(1, BLK), gk.reshape(1, BLK), go.reshape(n_heads, 1, BLK))


SGU_TOGETHER = 4
_INV_SQRT2 = 0.7071067811865476
_INV_SQRT2PI = 0.3989422804014327


def _gelu(x):
    return 0.5 * x * (1.0 + lax.erf(x * _INV_SQRT2))


def _gelu_and_grad(x):
    cdf = 0.5 * (1.0 + lax.erf(x * _INV_SQRT2))
    return x * cdf, cdf + x * jnp.exp(-0.5 * x * x) * _INV_SQRT2PI


def _sgu_fwd(p, mix, w, b, gv, gout, n_heads, *, name):
    s = p.shape[0]
    n_groups = w.shape[0]
    nb = s // BLK
    assert mix.shape == (2, s, n_groups * BLK)

    def body(u_ref, v_ref, w_ref, b_ref, gv_ref, go_ref, _mix_ref, out_ref):
        wt = jnp.where(_iota2(0) >= _iota2(1), w_ref[...], 0.0).astype(BF16)
        bias = b_ref[...]

        def chunks(i, _):
            rows = [pl.ds(pl.multiple_of((i * SGU_TOGETHER + k) * BLK, BLK), BLK) for k in range(SGU_TOGETHER)]
            us = [_gelu(u_ref[r, :]) for r in rows]
            vss = []
            for r in rows:
                vv = _gelu(v_ref[r, :])
                vss.append((vv * _rstd(vv) * gv_ref[...]).astype(BF16))
            mixed = [jnp.dot(wt, vs, preferred_element_type=F32) + bias for vs in vss]
            for r, u, m in zip(rows, us, mixed):
                gated = u * m
                out_ref[r, :] = (gated * _rstd(gated) * go_ref[...]).astype(out_ref.dtype)
            return 0

        assert nb % SGU_TOGETHER == 0
        lax.fori_loop(0, nb // SGU_TOGETHER, chunks, 0)

    def col(off):
        return pl.BlockSpec((s, BLK), lambda g: (0, off + g))

    per_group = pl.BlockSpec((None, 1, BLK), lambda g: (g, 0, 0))
    return pl.pallas_call(
        body, name=name, grid=(n_groups,),
        in_specs=[col(3 * n_heads), col(3 * n_heads + n_groups), pl.BlockSpec((None, BLK, BLK), lambda g: (g, 0, 0)),
                  pl.BlockSpec((None, BLK, 1), lambda g: (g, 0, 0)), per_group, per_group, ANY],
        out_specs=pl.BlockSpec((None, s, BLK), lambda g: (1, 0, g)), out_shape=jax.ShapeDtypeStruct(mix.shape, mix.dtype),
        input_output_aliases={6: 0}, compiler_params=_params(("parallel",), VMEM_BIG),
    )(p, p, w, b.reshape(n_groups, BLK, 1), gv.reshape(n_groups, 1, BLK), gout.reshape(n_groups, 1, BLK), mix)


def _sgu_bwd(p, dmix, w, b, gv, gout, n_heads, *, name):
    s = p.shape[0]
    n_groups = w.shape[0]
    nb = s // BLK

    def body(u_ref, v_ref, ds_ref, w_ref, b_ref, gv_ref, go_ref, duv_ref, dw_ref, db_ref, dgv_ref, dgo_ref):
        lower = _iota2(0) >= _iota2(1)
        wt = jnp.where(lower, w_ref[...], 0.0).astype(BF16)
        bias = b_ref[...]

        def chunks(i, carry):
            dw, db, dgv, dgo = carry
            rows = [pl.ds(pl.multiple_of((i * SGU_TOGETHER + k) * BLK, BLK), BLK) for k in range(SGU_TOGETHER)]
            pre = []
            for r in rows:
                u, u_grad = _gelu_and_grad(u_ref[r, :])
                vv, vv_grad = _gelu_and_grad(v_ref[r, :])
                rv = _rstd(vv)
                pre.append((u, u_grad, vv, vv_grad, rv, (vv * rv * gv_ref[...]).astype(BF16)))
            mixed = [jnp.dot(wt, t[5], preferred_element_type=F32) + bias for t in pre]
            mid = []
            for r, t, m in zip(rows, pre, mixed):
                gated = t[0] * m
                dgated, dgo_rows = _norm_bwd(gated, _rstd(gated), go_ref[...], ds_ref[r, :])
                dmixed = dgated * t[0]
                duv_ref[0, r, :] = (dgated * m * t[1]).astype(BF16)
                dgo = dgo + jnp.sum(dgo_rows, axis=0, keepdims=True)
                db = db + jnp.sum(dmixed, axis=1, keepdims=True)
                mid.append(dmixed.astype(BF16))
            dvss = [_dot_tn(wt, dmb) for dmb in mid]
            for dmb, t in zip(mid, pre):
                dw = dw + _dot_nt(dmb, t[5])
            for r, t, dvs in zip(rows, pre, dvss):
                dvv, dgv_rows = _norm_bwd(t[2], t[4], gv_ref[...], dvs)
                duv_ref[1, r, :] = (dvv * t[3]).astype(BF16)
                dgv = dgv + jnp.sum(dgv_rows, axis=0, keepdims=True)
            return dw, db, dgv, dgo

        assert nb % SGU_TOGETHER == 0
        row0 = jnp.zeros((1, BLK), F32)
        dw, db, dgv, dgo = lax.fori_loop(0, nb // SGU_TOGETHER, chunks,
                                         (jnp.zeros((BLK, BLK), F32), jnp.zeros((BLK, 1), F32), row0, row0))
        dw_ref[...] = jnp.where(lower, dw, 0.0)
        db_ref[...] = db
        dgv_ref[...] = dgv
        dgo_ref[...] = dgo

    def col(off):
        return pl.BlockSpec((s, BLK), lambda g: (0, off + g))

    per_group = pl.BlockSpec((None, 1, BLK), lambda g: (g, 0, 0))
    square = pl.BlockSpec((None, BLK, BLK), lambda g: (g, 0, 0))
    column = pl.BlockSpec((None, BLK, 1), lambda g: (g, 0, 0))
    gain = jax.ShapeDtypeStruct((n_groups, 1, BLK), F32)
    return pl.pallas_call(
        body, name=name, grid=(n_groups,),
        in_specs=[col(3 * n_heads), col(3 * n_heads + n_groups), col(n_heads), square, column, per_group, per_group],
        out_specs=[pl.BlockSpec((2, s, BLK), lambda g: (0, 0, g)), square, column, per_group, per_group],
        out_shape=[jax.ShapeDtypeStruct((2, s, n_groups * BLK), BF16), jax.ShapeDtypeStruct((n_groups, BLK, BLK), F32),
                   jax.ShapeDtypeStruct((n_groups, BLK, 1), F32), gain, gain],
        compiler_params=_params(("parallel",), VMEM_BIG),
    )(p, p, dmix, w, b.reshape(n_groups, BLK, 1), gv.reshape(n_groups, 1, BLK), gout.reshape(n_groups, 1, BLK))


CONV_ROWS = 256
HALO = 8


def _shift_down(ref, r0, n, first):
    cur = ref[pl.ds(r0, n), :]
    prev = jnp.zeros((HALO, cur.shape[1]), F32) if first else ref[pl.ds(r0 - HALO, HALO), :]
    ext = jnp.concatenate([prev, cur], axis=0)
    return pltpu.roll(ext, 1, 0)[HALO:], pltpu.roll(ext, 2, 0)[HALO:], cur


def _shift_up(ref, r0, n, last):
    cur = ref[pl.ds(r0, n), :]
    nxt = jnp.zeros((HALO, cur.shape[1]), F32) if last else ref[pl.ds(r0 + n, HALO), :]
    ext = jnp.concatenate([cur, nxt], axis=0)
    return cur, pltpu.roll(ext, n + HALO - 1, 0)[:n], pltpu.roll(ext, n + HALO - 2, 0)[:n]


def _conv_rows(x1, x2, x0, w_ref, b_ref):
    return ((b_ref[...] + x2 * w_ref[0:1, :]) + x1 * w_ref[1:2, :]) + x0 * w_ref[2:3, :]


def _conv_specs(s, f, tc):
    nf = f // tc
    gate = pl.BlockSpec((s, tc), lambda n: (0, n))
    val = pl.BlockSpec((s, tc), lambda n: (0, nf + n))
    wg = pl.BlockSpec((3, tc), lambda n: (0, n))
    wv = pl.BlockSpec((3, tc), lambda n: (0, nf + n))
    bg = pl.BlockSpec((1, tc), lambda n: (0, n))
    bv = pl.BlockSpec((1, tc), lambda n: (0, nf + n))
    return nf, gate, val, wg, wv, bg, bv


def _up_conv_fwd(h, w_up, cw, cb, *, name):
    s, d = h.shape
    chips, _, per_chip = w_up.shape
    f = chips * per_chip // 2
    tc = _pick(math.gcd(f, per_chip), (256, 128))
    cr = min(CONV_ROWS, s)
    nf, gate, _val, wg, wv, bg, bv = _conv_specs(s, f, tc)
    per = per_chip // tc

    def body(h_ref, mg_ref, mv_ref, wg_ref, wv_ref, bg_ref, bv_ref, ug_ref, uv_ref, out_ref):
        hb = h_ref[...]
        ug_ref[...] = jnp.dot(hb, mg_ref[...], preferred_element_type=F32)
        uv_ref[...] = jnp.dot(hb, mv_ref[...], preferred_element_type=F32)
        for r0 in range(0, s, cr):
            gc = _conv_rows(*_shift_down(ug_ref, r0, cr, r0 == 0), wg_ref, bg_ref)
            vc = _conv_rows(*_shift_down(uv_ref, r0, cr, r0 == 0), wv_ref, bv_ref)
            out_ref[pl.ds(r0, cr), :] = (gc * jax.nn.sigmoid(gc) * vc).astype(out_ref.dtype)

    def cols(first):
        return pl.BlockSpec((None, d, tc), lambda n: ((first + n) // per, 0, (first + n) % per))

    half = jax.ShapeDtypeStruct((s, f), F32)
    return pl.pallas_call(
        body, name=name, grid=(nf,), in_specs=[pl.BlockSpec((s, d), lambda n: (0, 0)), cols(0), cols(nf), wg, wv, bg, bv],
        out_specs=[gate, gate, gate], out_shape=[half, half, jax.ShapeDtypeStruct((s, f), BF16)],
        compiler_params=_params(("arbitrary",), VMEM_BIG),
    )(h, w_up, w_up, cw, cw, cb.reshape(1, 2 * f), cb.reshape(1, 2 * f))


def _conv_bwd(up_gate, up_val, dy, w_down, cw, cb, *, name):
    s, f = up_gate.shape
    d = dy.shape[1]
    f2 = 2 * f
    tc = _pick(f, (256, 128))
    cr = min(CONV_ROWS, s)
    nf, gate, _val, wg, wv, bg, bv = _conv_specs(s, f, tc)

    def body(g_ref, v_ref, dy_ref, wd_ref, wg_ref, wv_ref, bg_ref, bv_ref, dup_ref, dw_ref, db_ref, dgc, dvc, da_ref):
        da_ref[...] = _dot_nt(dy_ref[...], wd_ref[...])
        zero = jnp.zeros((1, tc), F32)
        sums = [[zero] * 4, [zero] * 4]
        for r0 in range(0, s, cr):
            rows = pl.ds(r0, cr)
            gx = _shift_down(g_ref, r0, cr, r0 == 0)
            vx = _shift_down(v_ref, r0, cr, r0 == 0)
            gc = _conv_rows(*gx, wg_ref, bg_ref)
            vc = _conv_rows(*vx, wv_ref, bv_ref)
            sig = jax.nn.sigmoid(gc)
            da = da_ref[rows, :]
            d_gate = da * vc * (sig * (1.0 + gc * (1.0 - sig)))
            d_val = da * (gc * sig)
            dgc[rows, :] = d_gate
            dvc[rows, :] = d_val
            for part, (dc, (x1, x2, x0)) in enumerate(((d_gate, gx), (d_val, vx))):
                for tap, xs in enumerate((x2, x1, x0)):
                    sums[part][tap] = sums[part][tap] + jnp.sum(dc * xs, axis=0, keepdims=True)
                sums[part][3] = sums[part][3] + jnp.sum(dc, axis=0, keepdims=True)
        dw_ref[...] = jnp.zeros_like(dw_ref)
        db_ref[...] = jnp.zeros_like(db_ref)
        for part, (dc_ref, w_ref) in enumerate(((dgc, wg_ref), (dvc, wv_ref))):
            for tap in range(3):
                dw_ref[part, tap:tap + 1, :] = sums[part][tap]
            db_ref[part, 0:1, :] = sums[part][3]
            for r0 in range(0, s, cr):
                d0, d1, d2 = _shift_up(dc_ref, r0, cr, r0 + cr == s)
                dup_ref[part, pl.ds(r0, cr), :] = ((d0 * w_ref[2:3, :] + d1 * w_ref[1:2, :]) + d2 * w_ref[0:1, :]).astype(BF16)

    small = pl.BlockSpec((2, 8, tc), lambda n: (0, 0, n))
    return pl.pallas_call(
        body, name=name, grid=(nf,),
        in_specs=[gate, gate, pl.BlockSpec((s, d), lambda n: (0, 0)), pl.BlockSpec((tc, d), lambda n: (n, 0)), wg, wv, bg, bv],
        out_specs=[pl.BlockSpec((2, s, tc), lambda n: (0, 0, n)), small, small],
        out_shape=[jax.ShapeDtypeStruct((2, s, f), BF16), jax.ShapeDtypeStruct((2, 8, f), F32),
                   jax.ShapeDtypeStruct((2, 8, f), F32)],
        scratch_shapes=[pltpu.VMEM((s, tc), F32)] * 3, compiler_params=_params(("arbitrary",), VMEM_BIG),
    )(up_gate, up_val, dy, w_down, cw, cw, cb.reshape(1, f2), cb.reshape(1, f2))


def _adamw(w, g, m, v, *, name, dep=None, copy_g=False):
    shape = w.shape
    cols = shape[-1]
    rows = w.size // cols
    if rows * cols * 4 <= (2 << 20):
        tr = rows
    else:
        tr = next(t for t in (1024, 512, 256, 128, 64, 32, 16, 8) if rows % t == 0 and (t * cols * 4 <= (2 << 20) or t == 8))

    n_out = 4 if copy_g else 3

    def body(w_ref, g_ref, m_ref, v_ref, *rest):
        d_ref, nm_ref, nv_ref = rest[-n_out:][:3]
        gr = g_ref[...]
        if copy_g:
            rest[-1][...] = gr
        nm = ADAM_B1 * m_ref[...] + (1.0 - ADAM_B1) * gr
        nv = ADAM_B2 * v_ref[...] + (1.0 - ADAM_B2) * (gr * gr)
        m_hat = nm / (1.0 - ADAM_B1 ** ADAM_STEP)
        v_hat = nv / (1.0 - ADAM_B2 ** ADAM_STEP)
        d_ref[...] = -ADAM_LR * (m_hat / (jnp.sqrt(v_hat) + ADAM_EPS) + ADAM_WD * w_ref[...])
        nm_ref[...] = nm
        nv_ref[...] = nv

    blk = pl.BlockSpec((tr, cols), lambda i: (i, 0))
    out = jax.ShapeDtypeStruct((rows, cols), F32)
    res = pl.pallas_call(
        body, name=name, grid=(rows // tr,), in_specs=[blk] * 4 + ([] if dep is None else [ANY]), out_specs=[blk] * n_out,
        out_shape=[out] * n_out, compiler_params=_params(("parallel",), VMEM_BIG),
    )(*[t.reshape(rows, cols) for t in (w, g, m, v)], *([] if dep is None else [dep]))
    return [t.reshape(shape) for t in res]


def _place():
    x, y, c = lax.axis_index("x"), lax.axis_index("y"), lax.axis_index("c")
    others = [(1 - x, y), (x, 1 - y), (1 - x, 1 - y)]
    return x, y, c, others


def _remote(src, dst, send_sem, recv_sem, device):
    return pltpu.make_async_remote_copy(src_ref=src, dst_ref=dst, send_sem=send_sem, recv_sem=recv_sem, device_id=device,
                                        device_id_type=MESH)


def _hbm_call(body, name, args, out_shapes, n_sems, n_local, aliases=None):
    return pl.pallas_call(
        body, name=name, in_specs=[ANY] * len(args), out_specs=[ANY] * len(out_shapes), out_shape=out_shapes,
        scratch_shapes=[pltpu.SemaphoreType.DMA((n_sems,)), pltpu.SemaphoreType.DMA((n_sems,)),
                        pltpu.SemaphoreType.DMA((max(n_local, 1),))],
        input_output_aliases=aliases or {}, compiler_params=pltpu.CompilerParams(has_side_effects=True),
    )(*args)


def _all_gather_weights(halved, whole, *, name):
    nh, nw = len(halved), len(whole)
    arrays = list(halved) + list(whole)

    def body(*refs):
        srcs, outs = refs[:nh + nw], refs[nh + nw:2 * (nh + nw)]
        send, recv, local = refs[2 * (nh + nw):]
        x, y, c, others = _place()
        me = 2 * x + y
        locals_ = [pltpu.make_async_copy(srcs[nh + a], outs[nh + a].at[me], local.at[a]) for a in range(nw)]
        for cp in locals_:
            cp.start()
        sends = []
        for a in range(nh):
            half = outs[a].shape[1] // 2
            rows = pl.ds(c * half, half)
            for j, (px, py) in enumerate(others):
                sends.append(_remote(outs[a].at[me, rows], outs[a].at[me, rows], send.at[6 * a + j], recv.at[6 * a + j],
                                     (px, py, c)))
        for a in range(nw):
            for j, (px, py) in enumerate(others):
                sends.append(_remote(srcs[nh + a], outs[nh + a].at[me], send.at[6 * nh + 3 * a + j],
                                     recv.at[6 * nh + 3 * a + j], (px, py, c)))
        for cp in sends:
            cp.start()
        for a in range(nh):
            half = outs[a].shape[1] // 2
            rows = pl.ds(c * half, half)
            for j, (px, py) in enumerate(others):
                got = outs[a].at[2 * px + py, rows]
                _remote(got, got, send.at[6 * a + j], recv.at[6 * a + j], (px, py, c)).wait_recv()
                fwd = _remote(got, got, send.at[6 * a + 3 + j], recv.at[6 * a + 3 + j], (x, y, 1 - c))
                fwd.start()
                sends.append(fwd)
        for a in range(nh):
            half = outs[a].shape[1] // 2
            theirs = pl.ds((1 - c) * half, half)
            for j, (px, py) in enumerate(others):
                got = outs[a].at[2 * px + py, theirs]
                _remote(got, got, send.at[6 * a + 3 + j], recv.at[6 * a + 3 + j], (x, y, 1 - c)).wait_recv()
        for a in range(nw):
            for j, (px, py) in enumerate(others):
                got = outs[nh + a].at[2 * px + py]
                _remote(got, got, send.at[6 * nh + 3 * a + j], recv.at[6 * nh + 3 * a + j], (px, py, c)).wait_recv()
        for cp in sends:
            cp.wait_send()
        for cp in locals_:
            cp.wait()

    out_shapes = [jax.ShapeDtypeStruct(t.shape, t.dtype) for t in halved]
    out_shapes += [jax.ShapeDtypeStruct((N_CHIPS,) + t.shape, t.dtype) for t in whole]
    return _hbm_call(body, name, arrays, out_shapes, 6 * nh + 3 * nw, nw, aliases={a: a for a in range(nh)})


def _cast_into(w, layer, place, *, name, dep=None):
    _, r, cols = w.shape
    tr = _row_tile(r, cols)

    def body(place_ref, w_ref, *rest):
        rest[-1][...] = w_ref[...].astype(BF16)

    in_specs, args = [pl.BlockSpec((None, tr, cols), lambda i, pr: (layer, i, 0))], [place, w]
    if dep is not None:
        in_specs.append(ANY)
        args.append(dep)
    grid_spec = pltpu.PrefetchScalarGridSpec(
        num_scalar_prefetch=1, grid=(r // tr,), in_specs=in_specs,
        out_specs=pl.BlockSpec((None, tr, cols), lambda i, pr: (pr[1], i, 0)),
    )
    return pl.pallas_call(
        body, name=name, grid_spec=grid_spec, out_shape=jax.ShapeDtypeStruct((N_CHIPS, r, cols), BF16),
        compiler_params=_params(("parallel",), VMEM_BIG),
    )(*args)


HBM = pl.BlockSpec(memory_space=pltpu.HBM)
SEM = pl.BlockSpec(memory_space=pltpu.SEMAPHORE)
EFFECT = pltpu.SideEffectType.DATAFLOW_SIDE_EFFECTING
TOKEN = jax.ShapeDtypeStruct((8, 128), F32)


def _in_hbm(t):
    return pltpu.with_memory_space_constraint(t, pltpu.HBM)


def _gather_copies(buf, send, recv):
    x, y, c, others = _place()
    half = buf.shape[1] // 2
    rows = pl.ds(c * half, half)
    return [_remote(buf.at[2 * x + y, rows], buf.at[2 * x + y, rows], send.at[j], recv.at[j], (px, py, c))
            for j, (px, py) in enumerate(others)]


def _gather_start(bufs, *, name):
    n = len(bufs)

    def body(*refs):
        ins, sends, recvs, token = refs[:n], refs[n:2 * n], refs[2 * n:3 * n], refs[4 * n]
        for a in range(n):
            for cp in _gather_copies(ins[a], sends[a], recvs[a]):
                cp.start()
        token[...] = jnp.zeros_like(token)

    sems = [pltpu.SemaphoreType.DMA((3,))] * (2 * n)
    res = pl.pallas_call(
        body, name=name, out_shape=sems + [pltpu.HBM(t.shape, t.dtype) for t in bufs] + [TOKEN],
        in_specs=[HBM] * n, out_specs=[SEM] * (2 * n) + [HBM] * n + [pl.BlockSpec(memory_space=pltpu.VMEM)],
        input_output_aliases={a: 2 * n + a for a in range(n)}, compiler_params=pltpu.CompilerParams(has_side_effects=EFFECT),
    )(*[_in_hbm(t) for t in bufs])
    return [(res[2 * n + a], res[a], res[n + a]) for a in range(n)], res[3 * n]


def _gather_wait(state, after, *, name):
    buf, send, recv = state

    def body(buf_ref, send_ref, recv_ref, after_ref, out_ref):
        for cp in _gather_copies(buf_ref, send_ref, recv_ref):
            cp.wait_send()
            cp.wait_recv()

    return pl.pallas_call(
        body, name=name, out_shape=pltpu.HBM(buf.shape, buf.dtype), in_specs=[HBM, SEM, SEM, ANY], out_specs=HBM,
        input_output_aliases={0: 0}, compiler_params=pltpu.CompilerParams(has_side_effects=EFFECT),
    )(buf, send, recv, after)


def _chip_copies(src, land, send, recv):
    _x, _y, c, others = _place()
    return [_remote(src.at[2 * px + py], land.at[j], send.at[j], recv.at[j], (px, py, c)) for j, (px, py) in enumerate(others)]


def _chip_start(partial, *, name):
    def body(src, land, send, recv, _src_thru, _land_thru, token):
        for cp in _chip_copies(src, land, send, recv):
            cp.start()
        token[...] = jnp.zeros_like(token)

    land_shape = (3,) + partial.shape[1:]
    sem = pltpu.SemaphoreType.DMA((3,))
    send, recv, src, land, token = pl.pallas_call(
        body, name=name, out_shape=[sem, sem, pltpu.HBM(partial.shape, partial.dtype), pltpu.HBM(land_shape, partial.dtype), TOKEN],
        in_specs=[HBM, HBM], out_specs=[SEM, SEM, HBM, HBM, pl.BlockSpec(memory_space=pltpu.VMEM)],
        input_output_aliases={0: 2, 1: 3}, compiler_params=pltpu.CompilerParams(has_side_effects=EFFECT),
    )(_in_hbm(partial), _in_hbm(lax.empty(land_shape, partial.dtype)))
    return (src, land, send, recv), token


def _chip_wait(state, after, *, name):
    src, land, send, recv = state

    def body(src_ref, land_ref, send_ref, recv_ref, after_ref, _src_out, _land_out):
        for cp in _chip_copies(src_ref, land_ref, send_ref, recv_ref):
            cp.wait_send()
            cp.wait_recv()

    return pl.pallas_call(
        body, name=name, out_shape=[pltpu.HBM(src.shape, src.dtype), pltpu.HBM(land.shape, land.dtype)],
        in_specs=[HBM, HBM, SEM, SEM, ANY], out_specs=[HBM, HBM], input_output_aliases={0: 0, 1: 1},
        compiler_params=pltpu.CompilerParams(has_side_effects=EFFECT),
    )(src, land, send, recv, after)[1]


def _split_start(bufs, copies, n_copies, *, name):
    n = len(bufs)

    def body(*refs):
        for cp in copies(refs[:n], refs[n], refs[n + 1]):
            cp.start()
        refs[-1][...] = jnp.zeros_like(refs[-1])

    sem = pltpu.SemaphoreType.DMA((n_copies,))
    res = pl.pallas_call(
        body, name=name, out_shape=[sem, sem] + [pltpu.HBM(t.shape, t.dtype) for t in bufs] + [TOKEN],
        in_specs=[HBM] * n, out_specs=[SEM, SEM] + [HBM] * n + [pl.BlockSpec(memory_space=pltpu.VMEM)],
        input_output_aliases={a: 2 + a for a in range(n)}, compiler_params=pltpu.CompilerParams(has_side_effects=EFFECT),
    )(*[_in_hbm(t) for t in bufs])
    return (list(res[2:2 + n]), res[0], res[1]), res[-1]


def _split_wait(state, copies, after, *, name):
    bufs, send, recv = state
    n = len(bufs)

    def body(*refs):
        for cp in copies(refs[:n], refs[n], refs[n + 1]):
            cp.wait_send()
            cp.wait_recv()

    return list(pl.pallas_call(
        body, name=name, out_shape=[pltpu.HBM(t.shape, t.dtype) for t in bufs], in_specs=[HBM] * n + [SEM, SEM, ANY],
        out_specs=[HBM] * n, input_output_aliases={a: a for a in range(n)},
        compiler_params=pltpu.CompilerParams(has_side_effects=EFFECT),
    )(*bufs, send, recv, after))


def _hand_over_copies(refs, send, recv):
    x, y, c, others = _place()
    half = refs[0].shape[1] // 2
    got = [refs[0].at[2 * px + py, pl.ds(c * half, half)] for px, py in others]
    return [_remote(got[j], got[j], send.at[j], recv.at[j], (x, y, 1 - c)) for j in range(3)]


def _pair_copies(refs, send, recv):
    x, y, c, _o = _place()
    half = refs[0].shape[1] // 2
    return [_remote(refs[0].at[:, pl.ds((1 - c) * half, half), :], refs[1], send.at[0], recv.at[0], (x, y, 1 - c))]


def _share_copies(refs, send, recv):
    x, y, c, _o = _place()
    return [_remote(refs[0].at[:, c], refs[0].at[:, c], send.at[0], recv.at[0], (x, y, 1 - c))]


def _small_copies(refs, send, recv):
    x, y, c, others = _place()
    peers = [(x, y, 1 - c)] + [(px, py, pc) for px, py in others for pc in (c, 1 - c)]
    slot = refs[1].at[4 * x + 2 * y + c]
    return [_remote(refs[0], slot, send.at[k], recv.at[k], peer) for k, peer in enumerate(peers)]


def _row_tile(rows, cols):
    return max(t for t in range(16, rows + 1, 16) if rows % t == 0 and (t * cols * 4 <= (4 << 20) or t == 16))


def _pair_sum(grad, theirs, place, *, name):
    _, r, cols = grad.shape
    r2 = r // 2
    tr = _row_tile(r2, cols)
    nr = r2 // tr

    def body(place_ref, g_ref, t_ref, all_ref):
        all_ref[...] = (g_ref[...].astype(F32) + t_ref[...].astype(F32)).astype(all_ref.dtype)

    grid_spec = pltpu.PrefetchScalarGridSpec(
        num_scalar_prefetch=1, grid=(N_CHIPS, nr),
        in_specs=[pl.BlockSpec((None, tr, cols), lambda k, i, pr: (k, pr[0] * nr + i, 0)),
                  pl.BlockSpec((None, tr, cols), lambda k, i, pr: (k, i, 0))],
        out_specs=pl.BlockSpec((None, tr, cols), lambda k, i, pr: (k, i, 0)),
    )
    return pl.pallas_call(
        body, name=name, grid_spec=grid_spec, out_shape=jax.ShapeDtypeStruct((N_CHIPS, r2, cols), BF16),
        compiler_params=_params(("parallel", "parallel"), VMEM_BIG),
    )(place, grad, theirs)


def _chip_sum(grad, theirs, got, place, buf, layer, depth, *, name):
    _, r, cols = grad.shape
    r2 = r // 2
    tr = _row_tile(r2, cols)
    nr = r2 // tr

    def body(place_ref, g_ref, t_ref, got_ref, *rest):
        own = g_ref[...].astype(F32) + t_ref[...].astype(F32)
        rest[-1][...] = ((own + got_ref[0].astype(F32)) + got_ref[1].astype(F32)) + got_ref[2].astype(F32)

    in_specs = [pl.BlockSpec((None, tr, cols), lambda i, pr: (pr[1], pr[0] * nr + i, 0)),
                pl.BlockSpec((None, tr, cols), lambda i, pr: (pr[1], i, 0)),
                pl.BlockSpec((3, tr, cols), lambda i, pr: (0, i, 0))]
    args = [place, grad, theirs, got]
    if buf is not None:
        in_specs.append(ANY)
        args.append(buf)
    grid_spec = pltpu.PrefetchScalarGridSpec(
        num_scalar_prefetch=1, grid=(nr,), in_specs=in_specs,
        out_specs=pl.BlockSpec((None, None, tr, cols), lambda i, pr: (layer, pr[0], i, 0)),
    )
    return pl.pallas_call(
        body, name=name, grid_spec=grid_spec, out_shape=jax.ShapeDtypeStruct((depth, 2, r2, cols), F32),
        input_output_aliases={} if buf is None else {4: 0}, compiler_params=_params(("parallel",), VMEM_BIG),
    )(*args)


def _sum_devices(parts, own, place, *, name):
    _, rows, cols = parts.shape
    tr = rows if N_DEV * rows * cols * 4 <= (16 << 20) else _pick(rows, (256, 128, 64, 32, 16, 8))

    def body(place_ref, p_ref, own_ref, out_ref):
        me = 2 * place_ref[1] + place_ref[0]
        acc = None
        for dev in range(N_DEV):
            term = jnp.where(me == dev, own_ref[...], p_ref[dev])
            acc = term if acc is None else acc + term
        out_ref[...] = acc

    grid_spec = pltpu.PrefetchScalarGridSpec(
        num_scalar_prefetch=1, grid=(rows // tr,),
        in_specs=[pl.BlockSpec((N_DEV, tr, cols), lambda i, pr: (0, i, 0)), pl.BlockSpec((tr, cols), lambda i, pr: (i, 0))],
        out_specs=pl.BlockSpec((tr, cols), lambda i, pr: (i, 0)),
    )
    return pl.pallas_call(
        body, name=name, grid_spec=grid_spec, out_shape=jax.ShapeDtypeStruct((rows, cols), F32),
        compiler_params=_params(("parallel",), VMEM_BIG),
    )(place, parts, own)


def _pack(parts):
    rows = []
    for t in parts:
        flat = t.reshape(-1, 128)
        pad = (-flat.shape[0]) % 8
        rows.append(jnp.pad(flat, ((0, pad), (0, 0))) if pad else flat)
    return jnp.concatenate(rows, axis=0)


def _unpack(pack, shapes):
    out, r0 = [], 0
    for shp in shapes:
        n = math.prod(shp) // 128
        out.append(pack[r0:r0 + n].reshape(shp))
        r0 += n + (-n) % 8
    return out


SMALL = ["attn_norm_g", "q_norm_g", "k_norm_g", "sgu_norm_g", "sgu_w", "sgu_b", "out_norm_a_g", "out_norm_b_g",
         "ffn_norm_g", "conv_b"]
BIG = ["w_in", "w_out", "w_up", "w_down"]
ORDER = ["attn_norm_g", "w_in", "q_norm_g", "k_norm_g", "sgu_norm_g", "sgu_w", "sgu_b", "out_norm_a_g", "out_norm_b_g",
         "w_out", "ffn_norm_g", "w_up", "conv_w", "conv_b", "w_down"]


def kernel(x, attn_norm_g, w_in, q_norm_g, k_norm_g, sgu_norm_g, sgu_w, sgu_b, out_norm_a_g, out_norm_b_g, w_out, ffn_norm_g, w_up, conv_w, conv_b, w_down, loss_target, m_attn_norm_g, m_w_in, m_q_norm_g, m_k_norm_g, m_sgu_norm_g, m_sgu_w, m_sgu_b, m_out_norm_a_g, m_out_norm_b_g, m_w_out, m_ffn_norm_g, m_w_up, m_conv_w, m_conv_b, m_w_down, v_attn_norm_g, v_w_in, v_q_norm_g, v_k_norm_g, v_sgu_norm_g, v_sgu_w, v_sgu_b, v_out_norm_a_g, v_out_norm_b_g, v_w_out, v_ffn_norm_g, v_w_up, v_conv_w, v_conv_b, v_w_down):
    W = dict(attn_norm_g=attn_norm_g, w_in=w_in, q_norm_g=q_norm_g, k_norm_g=k_norm_g, sgu_norm_g=sgu_norm_g, sgu_w=sgu_w,
             sgu_b=sgu_b, out_norm_a_g=out_norm_a_g, out_norm_b_g=out_norm_b_g, w_out=w_out, ffn_norm_g=ffn_norm_g, w_up=w_up,
             conv_w=conv_w, conv_b=conv_b, w_down=w_down)
    M = dict(attn_norm_g=m_attn_norm_g, w_in=m_w_in, q_norm_g=m_q_norm_g, k_norm_g=m_k_norm_g, sgu_norm_g=m_sgu_norm_g,
             sgu_w=m_sgu_w, sgu_b=m_sgu_b, out_norm_a_g=m_out_norm_a_g, out_norm_b_g=m_out_norm_b_g, w_out=m_w_out,
             ffn_norm_g=m_ffn_norm_g, w_up=m_w_up, conv_w=m_conv_w, conv_b=m_conv_b, w_down=m_w_down)
    V = dict(attn_norm_g=v_attn_norm_g, w_in=v_w_in, q_norm_g=v_q_norm_g, k_norm_g=v_k_norm_g, sgu_norm_g=v_sgu_norm_g,
             sgu_w=v_sgu_w, sgu_b=v_sgu_b, out_norm_a_g=v_out_norm_a_g, out_norm_b_g=v_out_norm_b_g, w_out=v_w_out,
             ffn_norm_g=v_ffn_norm_g, w_up=v_w_up, conv_w=v_conv_w, conv_b=v_conv_b, w_down=v_w_down)
    depth = w_in.shape[0]
    s, d = x.shape[1], x.shape[2]
    n_heads = out_norm_a_g.shape[1]
    core = lax.axis_index("c")
    chip = 2 * lax.axis_index("x") + lax.axis_index("y")
    place = jnp.stack([core, chip]).astype(jnp.int32)
    xs = x.reshape(s, d)

    f_local = conv_w.shape[2]
    taps = lax.dynamic_update_slice(jnp.zeros((N_CHIPS, 16, f_local), F32), conv_w.reshape(1, depth * 3, f_local),
                                    (chip, 0, 0))
    order = [(l, n) for l in range(depth) for n in BIG]
    first, token = _gather_start([_cast_into(w_in, 0, place, name="cast_w_in"), taps], name="gather_start_first")
    rest, token = _gather_start([_cast_into(W[n], l, place, dep=token, name=f"cast_{n}") for l, n in order[1:]],
                                name="gather_start_rest")
    states = dict(zip(order, [first[0]] + rest))

    states["taps"] = first[1]

    def landed(key, after, tag):
        buf = _gather_wait(states[key], after, name=f"gather_wait_{tag}")
        return _split_start([buf], _hand_over_copies, 3, name=f"hand_over_{tag}")

    def whole(state, after, tag):
        return _split_wait(state, _hand_over_copies, after, name=f"hand_over_wait_{tag}")[0]

    saved, full = [], []
    cur = xs
    for l in range(depth):
        gain = attn_norm_g[l] + token[0, 0] if l == 0 else attn_norm_g[l]
        h = _rmsnorm_fwd(cur, gain, name="attn_norm")
        if l == 0:
            ho_in, _ = landed((0, "w_in"), h, "w_in")
        w_in_l = whole(ho_in, h, "w_in")
        if l == 0:
            ho_taps, token = landed("taps", w_in_l, "taps")
        p = _mm(h, w_in_l, "nn", b_split=N_CHIPS, caps=(2048, 256, 2048), dep=token, name="proj_in")
        if l == 0:
            taps = whole(ho_taps, p, "taps")[:, :depth * 3].reshape(N_CHIPS, depth, 3, f_local)
            cw_full = jnp.transpose(taps, (1, 2, 0, 3)).reshape(depth, 3, N_CHIPS * f_local)
        mix, o_raw, lsum = _attn_fwd(p, q_norm_g[l], k_norm_g[l], out_norm_a_g[l], n_heads, name="attn_fwd")
        ho_out, token = landed((l, "w_out"), mix, "w_out")
        mix = _sgu_fwd(p, mix, sgu_w[l], sgu_b[l], sgu_norm_g[l] + token[0, 0], out_norm_b_g[l], n_heads, name="sgu_fwd")
        w_out_l = whole(ho_out, mix, "w_out").reshape(-1, d)
        ho_up, token = landed((l, "w_up"), w_out_l, "w_up")
        x1 = _mm(mix, w_out_l, "nn", a_split=2, res=cur, caps=(2048, 512, 1024), dep=token, name="proj_out")
        h2 = _rmsnorm_fwd(x1, ffn_norm_g[l], name="ffn_norm")
        w_up_l = whole(ho_up, h2, "w_up")
        up_gate, up_val, act = _up_conv_fwd(h2, w_up_l, cw_full[l], conv_b[l], name="ffn_up_conv")
        ho_down, token = landed((l, "w_down"), act, "w_down")
        w_down_l = whole(ho_down, act, "w_down").reshape(-1, d)
        if l + 1 < depth:
            ho_in, token = landed((l + 1, "w_in"), w_down_l, "w_in")
        x2 = _mm(act, w_down_l, "nn", res=x1, caps=(1024, 512, 2816), dep=token, name="ffn_down")
        full.append(dict(w_in=w_in_l, w_out=w_out_l, w_up=w_up_l, w_down=w_down_l, conv_w=cw_full[l]))
        saved.append(dict(x0=cur, h=h, p=p, o_raw=o_raw, lsum=lsum, mix=mix, x1=x1, h2=h2, up_gate=up_gate, up_val=up_val,
                          act=act))
        cur = x2

    dx, dxb, sq = _loss_head(cur, loss_target.reshape(s, d), name="loss_head")
    loss = lax.psum(sq[0, 0] * (0.5 / d), ("x", "y", "c"))

    small_grads = {n: [None] * depth for n in SMALL + ["conv_w"]}
    def pair_begin(n, grad):
        land = lax.empty((N_CHIPS, grad.shape[1] // 2, grad.shape[2]), grad.dtype)
        return _split_start([grad, land], _pair_copies, 1, name=f"pair_start_{n}")

    def chip_begin(n, state, after):
        grad, theirs = _split_wait(state, _pair_copies, after, name=f"pair_wait_{n}")
        state, tok = _chip_start(_pair_sum(grad, theirs, place, name=f"pair_sum_{n}"), name=f"chip_start_{n}")
        return (grad, theirs, state), tok

    pending = {}
    for l in reversed(range(depth)):
        fw, sv = full[l], saved[l]
        g_down = _mm(sv["act"], dxb, "tn", caps=(512, 2048, 2048), out_dtype=BF16, name="g_down")
        pair, tok = pair_begin("w_down", g_down.reshape(N_CHIPS, -1, d))
        dup, dcw, dcb = _conv_bwd(sv["up_gate"], sv["up_val"], dxb, fw["w_down"], fw["conv_w"], conv_b[l] + tok[0, 0],
                                  name="conv_bwd")
        pending[(l, "w_down")], tok = chip_begin("w_down", pair, dup)
        g_up = _mm(sv["h2"], dup, "tn", b_split=2, o_split=N_CHIPS, caps=(2048, 256, 2048), out_dtype=BF16, dep=tok,
                   name="g_up")
        pair, tok = pair_begin("w_up", g_up)
        dh2 = _mm(dup, fw["w_up"], "nt", a_split=2, b_split=N_CHIPS, caps=(1024, 512, 2816), dep=tok, name="d_h2")
        pending[(l, "w_up")], tok = chip_begin("w_up", pair, dh2)
        dx1, dx1b, dg_ffn = _rmsnorm_bwd(sv["x1"], ffn_norm_g[l] + tok[0, 0], dh2, dx, name="ffn_norm_bwd")
        dmix = _mm(dx1b, fw["w_out"], "nt", caps=(2048, 512, 2048), name="d_mix")
        g_out = _mm(sv["mix"], dx1b, "tn", a_split=2, caps=(512, 2048, 2048), out_dtype=BF16, name="g_out")
        pair, tok = pair_begin("w_out", g_out.reshape(N_CHIPS, -1, d))
        dqkv, dgq, dgk, dgoa = _attn_bwd(sv["p"], sv["o_raw"], sv["lsum"], dmix, q_norm_g[l] + tok[0, 0], k_norm_g[l],
                                         out_norm_a_g[l], n_heads, name="attn_bwd")
        pending[(l, "w_out")], tok = chip_begin("w_out", pair, dqkv)
        duv, dsw, dsb, dgv, dgob = _sgu_bwd(sv["p"], dmix, sgu_w[l], sgu_b[l], sgu_norm_g[l] + tok[0, 0], out_norm_b_g[l],
                                            n_heads, name="sgu_bwd")
        dp = jnp.concatenate([dqkv[0], dqkv[1], dqkv[2], duv[0], duv[1]], axis=1)
        g_in = _mm(sv["h"], dp, "tn", o_split=N_CHIPS, caps=(2048, 256, 2048), out_dtype=BF16, name="g_in")
        pair, tok = pair_begin("w_in", g_in)
        dh = _mm(dp, fw["w_in"], "nt", b_split=N_CHIPS, caps=(2048, 512, 1280), dep=tok, name="d_h")
        pending[(l, "w_in")], tok = chip_begin("w_in", pair, dh)
        dx, dxb, dg_attn = _rmsnorm_bwd(sv["x0"], attn_norm_g[l] + tok[0, 0], dh, dx1, name="attn_norm_bwd")

        small_grads["attn_norm_g"][l] = dg_attn.reshape(d)
        small_grads["q_norm_g"][l] = jnp.sum(dgq, axis=(0, 1))
        small_grads["k_norm_g"][l] = jnp.sum(dgk, axis=(0, 1))
        small_grads["sgu_norm_g"][l] = dgv.reshape(-1, BLK)
        small_grads["sgu_w"][l] = dsw
        small_grads["sgu_b"][l] = dsb.reshape(-1, BLK)
        small_grads["out_norm_a_g"][l] = dgoa.reshape(-1, BLK)
        small_grads["out_norm_b_g"][l] = dgob.reshape(-1, BLK)
        small_grads["ffn_norm_g"][l] = dg_ffn.reshape(d)
        small_grads["conv_b"][l] = dcb[:, 0, :].reshape(-1)
        small_grads["conv_w"][l] = jnp.transpose(dcw[:, :3, :], (1, 0, 2)).reshape(3, -1)

    names = SMALL + ["conv_w"]
    pack = _pack([jnp.stack(small_grads[n]) for n in names])
    small, tok = _split_start([pack, lax.empty((N_DEV,) + pack.shape, F32)], _small_copies, N_DEV - 1, name="small_start")
    G, D_, NM, NV = {}, {}, {}, {}
    after, prev = tok, None
    for n in ("w_down", "w_up", "w_out", "w_in"):
        buf = None
        for l in reversed(range(depth)):
            grad, theirs, state = pending[(l, n)]
            got = _chip_wait(state, after, name=f"chip_wait_{n}")
            buf = _chip_sum(grad, theirs, got, place, buf, l, depth, name=f"chip_sum_{n}")
            after = buf
        share, tok = _split_start([buf], _share_copies, 1, name=f"share_start_{n}")
        if prev is not None:
            D_[prev], NM[prev], NV[prev], G[prev] = _adamw(W[prev], G[prev], M[prev], V[prev], dep=tok, copy_g=True,
                                                           name=f"adamw_{prev}")
            after = NV[prev]
        G[n] = _split_wait(share, _share_copies, after, name=f"share_wait_{n}")[0].reshape(W[n].shape)
        after, prev = G[n], n
    D_[prev], NM[prev], NV[prev], G[prev] = _adamw(W[prev], G[prev], M[prev], V[prev], copy_g=True, name=f"adamw_{prev}")

    pack, parts = _split_wait(small, _small_copies, NV[prev], name="small_wait")
    total = _sum_devices(parts, pack, place, name="sum_small")
    f_full = conv_b.shape[1]
    shapes = [W[n].shape for n in SMALL] + [(depth, 3, f_full)]
    for n, t in zip(names, _unpack(total, shapes)):
        G[n] = t
    G["conv_w"] = lax.dynamic_slice_in_dim(G["conv_w"], chip * f_local, f_local, axis=2)

    D_["conv_w"], NM["conv_w"], NV["conv_w"] = _adamw(conv_w, G["conv_w"], m_conv_w, v_conv_w, name="adamw_conv_w")
    small_shapes = [W[n].shape for n in SMALL]
    res = _adamw(_pack([W[n] for n in SMALL]), _pack([G[n] for n in SMALL]), _pack([M[n] for n in SMALL]),
                 _pack([V[n] for n in SMALL]), name="adamw_small")
    for dst, t in zip((D_, NM, NV), res):
        for n, u in zip(SMALL, _unpack(t, small_shapes)):
            dst[n] = u

    return (loss, dx.reshape(x.shape), *[G[n] for n in ORDER], *[D_[n] for n in ORDER], *[NM[n] for n in ORDER],
            *[NV[n] for n in ORDER])
```

```python
import functools
import math

import jax
import jax.numpy as jnp
from jax import lax
from jax.experimental import pallas as pl
from jax.experimental.pallas import tpu as pltpu

F32 = jnp.float32
BF16 = jnp.bfloat16
EPS = 1e-6
BLK = 128
N_CHIPS = 4
N_DEV = 8
ADAM_LR, ADAM_B1, ADAM_B2, ADAM_EPS, ADAM_WD, ADAM_STEP = 0.001, 0.9, 0.999, 1e-08, 0.01, 10
VMEM_BIG = 48 * 1024 * 1024
MESH = pl.DeviceIdType.MESH
ANY = pl.BlockSpec(memory_space=pl.ANY)


def _pick(dim, prefs):
    for t in prefs:
        if dim % t == 0:
            return t
    raise ValueError(f"no tile in {prefs} divides {dim}")


def _params(sem=None, vmem=None):
    return pltpu.CompilerParams(dimension_semantics=sem, vmem_limit_bytes=vmem)


def _ldims(arr, split):
    if split == 1:
        return arr.shape
    p, r, cs = arr.shape
    assert p == split
    return (r, p * cs)


def _spec(tr, tc, split, cols, rc):
    if split == 1:
        return pl.BlockSpec((tr, tc), lambda i, j, k: rc(i, j, k))
    per = (cols // split) // tc

    def imap(i, j, k):
        r, c = rc(i, j, k)
        return (c // per, r, c % per)

    return pl.BlockSpec((None, tr, tc), imap)


def _fit(unit, cap):
    return max(t for t in range(128, min(unit, cap) + 1, 128) if unit % t == 0)


def _mm(a, b, mode, *, name, caps, a_split=1, b_split=1, o_split=1, out_dtype=F32, res=None, dep=None):
    ar, ac = _ldims(a, a_split)
    br, bc = _ldims(b, b_split)
    if mode == "nn":
        m, k, n = ar, ac, bc
        assert br == k
        ku, nu, mu = math.gcd(k // a_split, k), math.gcd(n // b_split, n // o_split), m
    elif mode == "nt":
        m, k, n = ar, ac, br
        assert bc == k
        ku, nu, mu = math.gcd(k // a_split, k // b_split), n // o_split, m
    else:
        k, m, n = ar, ac, bc
        assert br == k
        ku, nu, mu = k, math.gcd(n // b_split, n // o_split), m // a_split
    tm, tn, tk = _fit(mu, caps[0]), _fit(nu, caps[1]), _fit(ku, caps[2])
    nk = k // tk
    if mode == "nn":
        a_spec = _spec(tm, tk, a_split, k, lambda i, j, kk: (i, kk))
        b_spec = _spec(tk, tn, b_split, n, lambda i, j, kk: (kk, j))
    elif mode == "nt":
        a_spec = _spec(tm, tk, a_split, k, lambda i, j, kk: (i, kk))
        b_spec = _spec(tn, tk, b_split, k, lambda i, j, kk: (j, kk))
    else:
        a_spec = _spec(tk, tm, a_split, m, lambda i, j, kk: (kk, i))
        b_spec = _spec(tk, tn, b_split, n, lambda i, j, kk: (kk, j))
    o_spec = _spec(tm, tn, o_split, n, lambda i, j, kk: (i, j))
    dims = {"nn": (((1,), (0,)), ((), ())), "nt": (((1,), (1,)), ((), ())), "tn": (((0,), (0,)), ((), ()))}[mode]

    def body(a_ref, b_ref, *rest):
        if dep is not None:
            rest = rest[1:]
        if res is None:
            o_ref, acc = rest
        else:
            r_ref, o_ref, acc = rest
        kk = pl.program_id(2)

        @pl.when(kk == 0)
        def _():
            acc[...] = jnp.zeros_like(acc)

        acc[...] += lax.dot_general(a_ref[...].astype(BF16), b_ref[...].astype(BF16), dims, preferred_element_type=F32)

        @pl.when(kk == nk - 1)
        def _():
            out = acc[...]
            if res is not None:
                out = out + r_ref[...]
            o_ref[...] = out.astype(o_ref.dtype)

    in_specs, args = [a_spec, b_spec], [a, b]
    if dep is not None:
        in_specs.append(ANY)
        args.append(dep)
    if res is not None:
        in_specs.append(pl.BlockSpec((tm, tn), lambda i, j, kk: (i, j)))
        args.append(res)
    out_shape = (m, n) if o_split == 1 else (o_split, m, n // o_split)
    return pl.pallas_call(
        body, name=name, grid=(m // tm, n // tn, nk), in_specs=in_specs, out_specs=o_spec,
        out_shape=jax.ShapeDtypeStruct(out_shape, out_dtype), scratch_shapes=[pltpu.VMEM((tm, tn), F32)],
        compiler_params=_params(("parallel", "parallel", "arbitrary"), VMEM_BIG),
    )(*args)


def _rstd(v):
    return lax.rsqrt(jnp.mean(v * v, axis=-1, keepdims=True) + EPS)


def _norm_bwd(v, r, gain, dout):
    a = dout * gain
    dv = r * (a - v * (r * r * jnp.mean(a * v, axis=-1, keepdims=True)))
    return dv, dout * v * r


def _rmsnorm_fwd(x, g, *, name):
    s, d = x.shape
    tr = _pick(s, (256, 128))

    def body(x_ref, g_ref, o_ref):
        v = x_ref[...]
        o_ref[...] = (v * _rstd(v) * g_ref[...]).astype(o_ref.dtype)

    return pl.pallas_call(
        body, name=name, grid=(s // tr,),
        in_specs=[pl.BlockSpec((tr, d), lambda i: (i, 0)), pl.BlockSpec((1, d), lambda i: (0, 0))],
        out_specs=pl.BlockSpec((tr, d), lambda i: (i, 0)), out_shape=jax.ShapeDtypeStruct((s, d), BF16),
        compiler_params=_params(("parallel",)),
    )(x, g.reshape(1, d))


def _rmsnorm_bwd(x, g, dh, dres, *, name):
    s, d = x.shape
    tr = _pick(s, (256, 128))

    def body(x_ref, g_ref, dh_ref, dres_ref, dx_ref, dxb_ref, dg_ref):
        v = x_ref[...]
        dv, dgr = _norm_bwd(v, _rstd(v), g_ref[...], dh_ref[...])
        dx = dres_ref[...] + dv
        dx_ref[...] = dx
        dxb_ref[...] = dx.astype(BF16)
        part = jnp.sum(dgr, axis=0, keepdims=True)

        @pl.when(pl.program_id(0) == 0)
        def _():
            dg_ref[...] = part

        @pl.when(pl.program_id(0) > 0)
        def _():
            dg_ref[...] += part

    row = pl.BlockSpec((tr, d), lambda i: (i, 0))
    one = pl.BlockSpec((1, d), lambda i: (0, 0))
    return pl.pallas_call(
        body, name=name, grid=(s // tr,), in_specs=[row, one, row, row], out_specs=[row, row, one],
        out_shape=[jax.ShapeDtypeStruct((s, d), F32), jax.ShapeDtypeStruct((s, d), BF16), jax.ShapeDtypeStruct((1, d), F32)],
        compiler_params=_params(("arbitrary",)),
    )(x, g.reshape(1, d), dh, dres)


def _loss_head(y, target, *, name):
    s, d = y.shape
    tr = _pick(s, (256, 128))

    def body(y_ref, t_ref, dy_ref, dyb_ref, ls_ref):
        e = y_ref[...] - t_ref[...]
        dy = e * (1.0 / d)
        dy_ref[...] = dy
        dyb_ref[...] = dy.astype(BF16)
        part = jnp.full(ls_ref.shape, jnp.sum(e * e), F32)

        @pl.when(pl.program_id(0) == 0)
        def _():
            ls_ref[...] = part

        @pl.when(pl.program_id(0) > 0)
        def _():
            ls_ref[...] += part

    row = pl.BlockSpec((tr, d), lambda i: (i, 0))
    return pl.pallas_call(
        body, name=name, grid=(s // tr,), in_specs=[row, row], out_specs=[row, row, pl.BlockSpec((8, 128), lambda i: (0, 0))],
        out_shape=[jax.ShapeDtypeStruct((s, d), F32), jax.ShapeDtypeStruct((s, d), BF16), jax.ShapeDtypeStruct((8, 128), F32)],
        compiler_params=_params(("arbitrary",)),
    )(y, target)


def _iota2(axis):
    return lax.broadcasted_iota(jnp.int32, (BLK, BLK), axis)


def _tri_sum(v, tri):
    hi = v.astype(BF16)
    lo = (v - hi.astype(F32)).astype(BF16)
    return jnp.dot(hi, tri, preferred_element_type=F32) + jnp.dot(lo, tri, preferred_element_type=F32)


def _dot_nt(a, b):
    return lax.dot_general(a, b, (((1,), (1,)), ((), ())), preferred_element_type=F32)


def _dot_tn(a, b):
    return lax.dot_general(a, b, (((0,), (0,)), ((), ())), preferred_element_type=F32)


TQ_MAX = 1024
TQ_MAX_BWD = 1024
HP = 2
VMEM_ATTN_BWD = 56 * 1024 * 1024


def _lanes(hh):
    return slice(hh * BLK, (hh + 1) * BLK)


def _causal(n, diag):
    if not diag:
        return None
    return lax.broadcasted_iota(jnp.int32, (n, BLK), 1) < lax.broadcasted_iota(jnp.int32, (n, BLK), 0)


def _sb_sums(z, mask, rhs_gt):
    lb = jnp.minimum(z, 0.0) - jnp.log(1.0 + jnp.exp(-jnp.abs(z)))
    l1m = lb - z
    if mask is not None:
        l1m = jnp.where(mask, l1m, 0.0)
    return lb, _tri_sum(l1m, rhs_gt)


def _below(old, new, r0):
    return new if r0 == 0 else jnp.concatenate([old[:r0], new], axis=0)


def _attn_fwd(p, gq, gk, go, n_heads, *, name):
    s = p.shape[0]
    tq = min(TQ_MAX, s)
    per = tq // BLK
    scale = BLK ** -0.5

    def body(q_ref, k_ref, v_ref, gq_ref, gk_ref, go_ref, att_ref, o_ref, l_ref, qn, kn, vb):
        for hh in range(HP):
            q = q_ref[:, _lanes(hh)]
            k = k_ref[:, _lanes(hh)]
            qn[:, _lanes(hh)] = (q * _rstd(q) * gq_ref[...] * scale).astype(BF16)
            kn[:, _lanes(hh)] = (k * _rstd(k) * gk_ref[...]).astype(BF16)
            vb[:, _lanes(hh)] = v_ref[:, _lanes(hh)].astype(BF16)
        rhs_gt = jnp.concatenate([(_iota2(0) > _iota2(1)).astype(BF16), jnp.ones((BLK, BLK), BF16)], axis=1)

        def step(q0, j, r0, diag, states):
            n = tq - r0
            rows = pl.ds(pl.multiple_of(q0 + r0, BLK), n)
            cols = pl.ds(pl.multiple_of(j * BLK, BLK), BLK)
            mask = _causal(n, diag)
            zs = [_dot_nt(qn[rows, _lanes(hh)], kn[cols, _lanes(hh)]) for hh in range(HP)]
            sums = [_sb_sums(z, mask, rhs_gt) for z in zs]
            new = []
            for hh in range(HP):
                acc, later = states[hh]
                lb, both = sums[hh]
                a = jnp.exp(lb + both[:, :BLK] + later[r0:])
                if diag:
                    a = jnp.where(mask, a, 0.0)
                acc_new = acc[r0:] + jnp.dot(a.astype(BF16), vb[cols, _lanes(hh)], preferred_element_type=F32)
                new.append((_below(acc, acc_new, r0), _below(later, later[r0:] + both[:, BLK:], r0)))
            return tuple(new)

        def q_block(i, _):
            q0 = i * tq
            zero = jnp.zeros((tq, BLK), F32)
            states = ((zero, zero),) * HP
            for jd in reversed(range(per)):
                states = step(q0, i * per + jd, jd * BLK, True, states)
            states = lax.fori_loop(0, i * per, lambda jj, st: step(q0, i * per - 1 - jj, 0, False, st), states)
            tile = pl.ds(pl.multiple_of(q0, tq), tq)
            for hh in range(HP):
                o, total = states[hh]
                o_ref[tile, _lanes(hh)] = o
                l_ref[hh, tile, :] = total
                att_ref[tile, _lanes(hh)] = (o * _rstd(o) * go_ref[hh]).astype(att_ref.dtype)
            return 0

        lax.fori_loop(0, s // tq, q_block, 0)

    assert n_heads % HP == 0
    groups = n_heads // HP

    def col(part):
        return pl.BlockSpec((s, HP * BLK), lambda g: (0, part * groups + g))

    gain = pl.BlockSpec((1, BLK), lambda g: (0, 0))
    per_head = pl.BlockSpec((HP, 1, BLK), lambda g: (g, 0, 0))
    return pl.pallas_call(
        body, name=name, grid=(groups,),
        in_specs=[col(0), col(1), col(2), gain, gain, per_head],
        out_specs=[pl.BlockSpec((None, s, HP * BLK), lambda g: (0, 0, g)), col(0), pl.BlockSpec((HP, s, BLK), lambda g: (g, 0, 0))],
        out_shape=[jax.ShapeDtypeStruct((2, s, n_heads * BLK), BF16), jax.ShapeDtypeStruct((s, n_heads * BLK), F32),
                   jax.ShapeDtypeStruct((n_heads, s, BLK), F32)],
        scratch_shapes=[pltpu.VMEM((s, HP * BLK), BF16)] * 3,
        compiler_params=_params(("parallel",), VMEM_BIG),
    )(p, p, p, gq.reshape(1, BLK), gk.reshape(1, BLK), go.reshape(n_heads, 1, BLK))


def _attn_bwd(p, o_raw, lsum, dmix, gq, gk, go, n_heads, *, name):
    s = p.shape[0]
    tq = min(TQ_MAX_BWD, s)
    per = tq // BLK
    scale = BLK ** -0.5

    def body(q_ref, k_ref, v_ref, o_ref, l_ref, da_ref, gq_ref, gk_ref, go_ref,
             dqkv_ref, dgq_ref, dgk_ref, dgo_ref, qn, kn, vb, dob, dqn, dkn, dvv):
        for hh in range(HP):
            q = q_ref[:, _lanes(hh)]
            k = k_ref[:, _lanes(hh)]
            qn[:, _lanes(hh)] = (q * _rstd(q) * gq_ref[...] * scale).astype(BF16)
            kn[:, _lanes(hh)] = (k * _rstd(k) * gk_ref[...]).astype(BF16)
            vb[:, _lanes(hh)] = v_ref[:, _lanes(hh)].astype(BF16)
            o = o_ref[:, _lanes(hh)]
            do, dgo_rows = _norm_bwd(o, _rstd(o), go_ref[hh], da_ref[:, _lanes(hh)])
            dob[:, _lanes(hh)] = do.astype(BF16)
            dgo_ref[hh] = jnp.sum(dgo_rows, axis=0, keepdims=True)
        dkn[...] = jnp.zeros_like(dkn)
        dvv[...] = jnp.zeros_like(dvv)
        ones = jnp.ones((BLK, BLK), BF16)
        rhs_gt = jnp.concatenate([(_iota2(0) > _iota2(1)).astype(BF16), ones], axis=1)
        rhs_lt = jnp.concatenate([(_iota2(0) < _iota2(1)).astype(BF16), ones], axis=1)

        def step(q0, j, r0, diag, states):
            n = tq - r0
            rows = pl.ds(pl.multiple_of(q0 + r0, BLK), n)
            cols = pl.ds(pl.multiple_of(j * BLK, BLK), BLK)
            mask = _causal(n, diag)
            zs = [_dot_nt(qn[rows, _lanes(hh)], kn[cols, _lanes(hh)]) for hh in range(HP)]
            das = [_dot_nt(dob[rows, _lanes(hh)], vb[cols, _lanes(hh)]) for hh in range(HP)]
            sums = [_sb_sums(z, mask, rhs_gt) for z in zs]
            mids = []
            for hh in range(HP):
                lb, both = sums[hh]
                upto = states[hh][0][r0:] + both[:, BLK:]
                a = jnp.exp(lb + both[:, :BLK] + (l_ref[hh, rows, :] - upto))
                if diag:
                    a = jnp.where(mask, a, 0.0)
                g = das[hh] * a
                mids.append((lb, upto, a, g, _tri_sum(g, rhs_lt)))
            new = []
            for hh in range(HP):
                seen, gsum, dq = states[hh]
                lb, upto, a, g, bothg = mids[hh]
                beta = jnp.exp(lb)
                dz = g * (1.0 - beta) - beta * (bothg[:, :BLK] + gsum[r0:])
                if diag:
                    dz = jnp.where(mask, dz, 0.0)
                dzs = dz.astype(BF16)
                dq_new = dq[r0:] + jnp.dot(dzs, kn[cols, _lanes(hh)], preferred_element_type=F32)
                dkn[cols, _lanes(hh)] += _dot_tn(dzs, qn[rows, _lanes(hh)])
                dvv[cols, _lanes(hh)] += _dot_tn(a.astype(BF16), dob[rows, _lanes(hh)])
                new.append((_below(seen, upto, r0), _below(gsum, gsum[r0:] + bothg[:, BLK:], r0), _below(dq, dq_new, r0)))
            return tuple(new)

        def q_block(i, _):
            q0 = i * tq
            zero = jnp.zeros((tq, BLK), F32)
            states = ((zero, zero, zero),) * HP
            states = lax.fori_loop(0, i * per, lambda j, st: step(q0, j, 0, False, st), states)
            for jd in range(per):
                states = step(q0, i * per + jd, jd * BLK, True, states)
            tile = pl.ds(pl.multiple_of(q0, tq), tq)
            for hh in range(HP):
                dqn[tile, _lanes(hh)] = states[hh][2]
            return 0

        lax.fori_loop(0, s // tq, q_block, 0)
        for hh in range(HP):
            q = q_ref[:, _lanes(hh)]
            k = k_ref[:, _lanes(hh)]
            dq_raw, dgq_rows = _norm_bwd(q, _rstd(q), gq_ref[...], dqn[:, _lanes(hh)] * scale)
            dk_raw, dgk_rows = _norm_bwd(k, _rstd(k), gk_ref[...], dkn[:, _lanes(hh)])
            dqkv_ref[0, :, _lanes(hh)] = dq_raw.astype(BF16)
            dqkv_ref[1, :, _lanes(hh)] = dk_raw.astype(BF16)
            dqkv_ref[2, :, _lanes(hh)] = dvv[:, _lanes(hh)].astype(BF16)
            dgq_ref[hh] = jnp.sum(dgq_rows, axis=0, keepdims=True)
            dgk_ref[hh] = jnp.sum(dgk_rows, axis=0, keepdims=True)

    assert n_heads % HP == 0
    groups = n_heads // HP

    once = pl.Buffered(1)

    def col(part):
        return pl.BlockSpec((s, HP * BLK), lambda g: (0, part * groups + g), pipeline_mode=once)

    gain = pl.BlockSpec((1, BLK), lambda g: (0, 0))
    per_head = pl.BlockSpec((HP, 1, BLK), lambda g: (g, 0, 0))
    head_gain = jax.ShapeDtypeStruct((n_heads, 1, BLK), F32)
    return pl.pallas_call(
        body, name=name, grid=(groups,),
        in_specs=[col(0), col(1), col(2), col(0), pl.BlockSpec((HP, s, BLK), lambda g: (g, 0, 0), pipeline_mode=once), col(0),
                  gain, gain, per_head],
        out_specs=[pl.BlockSpec((3, s, HP * BLK), lambda g: (0, 0, g)), per_head, per_head, per_head],
        out_shape=[jax.ShapeDtypeStruct((3, s, n_heads * BLK), BF16), head_gain, head_gain, head_gain],
        scratch_shapes=[pltpu.VMEM((s, HP * BLK), BF16)] * 4 + [pltpu.VMEM((s, HP * BLK), F32)] * 3,
        compiler_params=_params(("parallel",), VMEM_ATTN_BWD),
    )(p, p, p, o_raw, lsum, dmix, gq.reshape(1, BLK), gk.reshape(1, BLK), go.reshape(n_heads, 1, BLK))


SGU_TOGETHER = 4
_INV_SQRT2 = 0.7071067811865476
_INV_SQRT2PI = 0.3989422804014327


def _gelu(x):
    return 0.5 * x * (1.0 + lax.erf(x * _INV_SQRT2))


def _gelu_and_grad(x):
    cdf = 0.5 * (1.0 + lax.erf(x * _INV_SQRT2))
    return x * cdf, cdf + x * jnp.exp(-0.5 * x * x) * _INV_SQRT2PI


def _sgu_fwd(p, mix, w, b, gv, gout, n_heads, *, name):
    s = p.shape[0]
    n_groups = w.shape[0]
    nb = s // BLK
    assert mix.shape == (2, s, n_groups * BLK)

    def body(u_ref, v_ref, w_ref, b_ref, gv_ref, go_ref, _mix_ref, out_ref):
        wt = jnp.where(_iota2(0) >= _iota2(1), w_ref[...], 0.0).astype(BF16)
        bias = b_ref[...]

        def chunks(i, _):
            rows = [pl.ds(pl.multiple_of((i * SGU_TOGETHER + k) * BLK, BLK), BLK) for k in range(SGU_TOGETHER)]
            us = [_gelu(u_ref[r, :]) for r in rows]
            vss = []
            for r in rows:
                vv = _gelu(v_ref[r, :])
                vss.append((vv * _rstd(vv) * gv_ref[...]).astype(BF16))
            mixed = [jnp.dot(wt, vs, preferred_element_type=F32) + bias for vs in vss]
            for r, u, m in zip(rows, us, mixed):
                gated = u * m
                out_ref[r, :] = (gated * _rstd(gated) * go_ref[...]).astype(out_ref.dtype)
            return 0

        assert nb % SGU_TOGETHER == 0
        lax.fori_loop(0, nb // SGU_TOGETHER, chunks, 0)

    def col(off):
        return pl.BlockSpec((s, BLK), lambda g: (0, off + g))

    per_group = pl.BlockSpec((None, 1, BLK), lambda g: (g, 0, 0))
    return pl.pallas_call(
        body, name=name, grid=(n_groups,),
        in_specs=[col(3 * n_heads), col(3 * n_heads + n_groups), pl.BlockSpec((None, BLK, BLK), lambda g: (g, 0, 0)),
                  pl.BlockSpec((None, BLK, 1), lambda g: (g, 0, 0)), per_group, per_group, ANY],
        out_specs=pl.BlockSpec((None, s, BLK), lambda g: (1, 0, g)), out_shape=jax.ShapeDtypeStruct(mix.shape, mix.dtype),
        input_output_aliases={6: 0}, compiler_params=_params(("parallel",), VMEM_BIG),
    )(p, p, w, b.reshape(n_groups, BLK, 1), gv.reshape(n_groups, 1, BLK), gout.reshape(n_groups, 1, BLK), mix)


def _sgu_bwd(p, dmix, w, b, gv, gout, n_heads, *, name):
    s = p.shape[0]
    n_groups = w.shape[0]
    nb = s // BLK

    def body(u_ref, v_ref, ds_ref, w_ref, b_ref, gv_ref, go_ref, duv_ref, dw_ref, db_ref, dgv_ref, dgo_ref):
        lower = _iota2(0) >= _iota2(1)
        wt = jnp.where(lower, w_ref[...], 0.0).astype(BF16)
        bias = b_ref[...]

        def chunks(i, carry):
            dw, db, dgv, dgo = carry
            rows = [pl.ds(pl.multiple_of((i * SGU_TOGETHER + k) * BLK, BLK), BLK) for k in range(SGU_TOGETHER)]
            pre = []
            for r in rows:
                u, u_grad = _gelu_and_grad(u_ref[r, :])
                vv, vv_grad = _gelu_and_grad(v_ref[r, :])
                rv = _rstd(vv)
                pre.append((u, u_grad, vv, vv_grad, rv, (vv * rv * gv_ref[...]).astype(BF16)))
            mixed = [jnp.dot(wt, t[5], preferred_element_type=F32) + bias for t in pre]
            mid = []
            for r, t, m in zip(rows, pre, mixed):
                gated = t[0] * m
                dgated, dgo_rows = _norm_bwd(gated, _rstd(gated), go_ref[...], ds_ref[r, :])
                dmixed = dgated * t[0]
                duv_ref[0, r, :] = (dgated * m * t[1]).astype(BF16)
                dgo = dgo + jnp.sum(dgo_rows, axis=0, keepdims=True)
                db = db + jnp.sum(dmixed, axis=1, keepdims=True)
                mid.append(dmixed.astype(BF16))
            dvss = [_dot_tn(wt, dmb) for dmb in mid]
            for dmb, t in zip(mid, pre):
                dw = dw + _dot_nt(dmb, t[5])
            for r, t, dvs in zip(rows, pre, dvss):
                dvv, dgv_rows = _norm_bwd(t[2], t[4], gv_ref[...], dvs)
                duv_ref[1, r, :] = (dvv * t[3]).astype(BF16)
                dgv = dgv + jnp.sum(dgv_rows, axis=0, keepdims=True)
            return dw, db, dgv, dgo

        assert nb % SGU_TOGETHER == 0
        row0 = jnp.zeros((1, BLK), F32)
        dw, db, dgv, dgo = lax.fori_loop(0, nb // SGU_TOGETHER, chunks,
                                         (jnp.zeros((BLK, BLK), F32), jnp.zeros((BLK, 1), F32), row0, row0))
        dw_ref[...] = jnp.where(lower, dw, 0.0)
        db_ref[...] = db
        dgv_ref[...] = dgv
        dgo_ref[...] = dgo

    def col(off):
        return pl.BlockSpec((s, BLK), lambda g: (0, off + g))

    per_group = pl.BlockSpec((None, 1, BLK), lambda g: (g, 0, 0))
    square = pl.BlockSpec((None, BLK, BLK), lambda g: (g, 0, 0))
    column = pl.BlockSpec((None, BLK, 1), lambda g: (g, 0, 0))
    gain = jax.ShapeDtypeStruct((n_groups, 1, BLK), F32)
    return pl.pallas_call(
        body, name=name, grid=(n_groups,),
        in_specs=[col(3 * n_heads), col(3 * n_heads + n_groups), col(n_heads), square, column, per_group, per_group],
        out_specs=[pl.BlockSpec((2, s, BLK), lambda g: (0, 0, g)), square, column, per_group, per_group],
        out_shape=[jax.ShapeDtypeStruct((2, s, n_groups * BLK), BF16), jax.ShapeDtypeStruct((n_groups, BLK, BLK), F32),
                   jax.ShapeDtypeStruct((n_groups, BLK, 1), F32), gain, gain],
        compiler_params=_params(("parallel",), VMEM_BIG),
    )(p, p, dmix, w, b.reshape(n_groups, BLK, 1), gv.reshape(n_groups, 1, BLK), gout.reshape(n_groups, 1, BLK))


CONV_ROWS = 256
HALO = 8


def _shift_down(ref, r0, n, first):
    cur = ref[pl.ds(r0, n), :]
    prev = jnp.zeros((HALO, cur.shape[1]), F32) if first else ref[pl.ds(r0 - HALO, HALO), :]
    ext = jnp.concatenate([prev, cur], axis=0)
    return pltpu.roll(ext, 1, 0)[HALO:], pltpu.roll(ext, 2, 0)[HALO:], cur


def _shift_up(ref, r0, n, last):
    cur = ref[pl.ds(r0, n), :]
    nxt = jnp.zeros((HALO, cur.shape[1]), F32) if last else ref[pl.ds(r0 + n, HALO), :]
    ext = jnp.concatenate([cur, nxt], axis=0)
    return cur, pltpu.roll(ext, n + HALO - 1, 0)[:n], pltpu.roll(ext, n + HALO - 2, 0)[:n]


def _conv_rows(x1, x2, x0, w_ref, b_ref):
    return ((b_ref[...] + x2 * w_ref[0:1, :]) + x1 * w_ref[1:2, :]) + x0 * w_ref[2:3, :]


def _conv_specs(s, f, tc):
    nf = f // tc
    gate = pl.BlockSpec((s, tc), lambda n: (0, n))
    val = pl.BlockSpec((s, tc), lambda n: (0, nf + n))
    wg = pl.BlockSpec((3, tc), lambda n: (0, n))
    wv = pl.BlockSpec((3, tc), lambda n: (0, nf + n))
    bg = pl.BlockSpec((1, tc), lambda n: (0, n))
    bv = pl.BlockSpec((1, tc), lambda n: (0, nf + n))
    return nf, gate, val, wg, wv, bg, bv


def _up_conv_fwd(h, w_up, cw, cb, *, name):
    s, d = h.shape
    chips, _, per_chip = w_up.shape
    f = chips * per_chip // 2
    tc = _pick(math.gcd(f, per_chip), (256, 128))
    cr = min(CONV_ROWS, s)
    nf, gate, _val, wg, wv, bg, bv = _conv_specs(s, f, tc)
    per = per_chip // tc

    def body(h_ref, mg_ref, mv_ref, wg_ref, wv_ref, bg_ref, bv_ref, ug_ref, uv_ref, out_ref):
        hb = h_ref[...]
        ug_ref[...] = jnp.dot(hb, mg_ref[...], preferred_element_type=F32)
        uv_ref[...] = jnp.dot(hb, mv_ref[...], preferred_element_type=F32)
        for r0 in range(0, s, cr):
            gc = _conv_rows(*_shift_down(ug_ref, r0, cr, r0 == 0), wg_ref, bg_ref)
            vc = _conv_rows(*_shift_down(uv_ref, r0, cr, r0 == 0), wv_ref, bv_ref)
            out_ref[pl.ds(r0, cr), :] = (gc * jax.nn.sigmoid(gc) * vc).astype(out_ref.dtype)

    def cols(first):
        return pl.BlockSpec((None, d, tc), lambda n: ((first + n) // per, 0, (first + n) % per))

    half = jax.ShapeDtypeStruct((s, f), F32)
    return pl.pallas_call(
        body, name=name, grid=(nf,), in_specs=[pl.BlockSpec((s, d), lambda n: (0, 0)), cols(0), cols(nf), wg, wv, bg, bv],
        out_specs=[gate, gate, gate], out_shape=[half, half, jax.ShapeDtypeStruct((s, f), BF16)],
        compiler_params=_params(("arbitrary",), VMEM_BIG),
    )(h, w_up, w_up, cw, cw, cb.reshape(1, 2 * f), cb.reshape(1, 2 * f))


def _conv_bwd(up_gate, up_val, dy, w_down, cw, cb, *, name):
    s, f = up_gate.shape
    d = dy.shape[1]
    f2 = 2 * f
    tc = _pick(f, (256, 128))
    cr = min(CONV_ROWS, s)
    nf, gate, _val, wg, wv, bg, bv = _conv_specs(s, f, tc)

    def body(g_ref, v_ref, dy_ref, wd_first, wd_next, wg_ref, wv_ref, bg_ref, bv_ref, dup_ref, dw_ref, db_ref,
             dgc, dvc, da_ref, da_next):
        @pl.when(pl.program_id(0) == 0)
        def _():
            da_ref[...] = _dot_nt(dy_ref[...], wd_first[...])

        da_next[...] = _dot_nt(dy_ref[...], wd_next[...])
        zero = jnp.zeros((1, tc), F32)
        sums = [[zero] * 4, [zero] * 4]
        for r0 in range(0, s, cr):
            rows = pl.ds(r0, cr)
            gx = _shift_down(g_ref, r0, cr, r0 == 0)
            vx = _shift_down(v_ref, r0, cr, r0 == 0)
            gc = _conv_rows(*gx, wg_ref, bg_ref)
            vc = _conv_rows(*vx, wv_ref, bv_ref)
            sig = jax.nn.sigmoid(gc)
            da = da_ref[rows, :]
            d_gate = da * vc * (sig * (1.0 + gc * (1.0 - sig)))
            d_val = da * (gc * sig)
            dgc[rows, :] = d_gate
            dvc[rows, :] = d_val
            for part, (dc, (x1, x2, x0)) in enumerate(((d_gate, gx), (d_val, vx))):
                for tap, xs in enumerate((x2, x1, x0)):
                    sums[part][tap] = sums[part][tap] + jnp.sum(dc * xs, axis=0, keepdims=True)
                sums[part][3] = sums[part][3] + jnp.sum(dc, axis=0, keepdims=True)
        dw_ref[...] = jnp.zeros_like(dw_ref)
        db_ref[...] = jnp.zeros_like(db_ref)
        for part, (dc_ref, w_ref) in enumerate(((dgc, wg_ref), (dvc, wv_ref))):
            for tap in range(3):
                dw_ref[part, tap:tap + 1, :] = sums[part][tap]
            db_ref[part, 0:1, :] = sums[part][3]
            for r0 in range(0, s, cr):
                d0, d1, d2 = _shift_up(dc_ref, r0, cr, r0 + cr == s)
                dup_ref[part, pl.ds(r0, cr), :] = ((d0 * w_ref[2:3, :] + d1 * w_ref[1:2, :]) + d2 * w_ref[0:1, :]).astype(BF16)
        da_ref[...] = da_next[...]

    small = pl.BlockSpec((2, 8, tc), lambda n: (0, 0, n))
    return pl.pallas_call(
        body, name=name, grid=(nf,),
        in_specs=[gate, gate, pl.BlockSpec((s, d), lambda n: (0, 0)), pl.BlockSpec((tc, d), lambda n: (0, 0)),
                  pl.BlockSpec((tc, d), lambda n: (jnp.minimum(n + 1, nf - 1), 0)), wg, wv, bg, bv],
        out_specs=[pl.BlockSpec((2, s, tc), lambda n: (0, 0, n)), small, small],
        out_shape=[jax.ShapeDtypeStruct((2, s, f), BF16), jax.ShapeDtypeStruct((2, 8, f), F32),
                   jax.ShapeDtypeStruct((2, 8, f), F32)],
        scratch_shapes=[pltpu.VMEM((s, tc), F32)] * 4,
        compiler_params=_params(("arbitrary",), VMEM_BIG),
    )(up_gate, up_val, dy, w_down, w_down, cw, cw, cb.reshape(1, f2), cb.reshape(1, f2))


def _adamw(w, g, m, v, *, name, dep=None, copy_g=False):
    shape = w.shape
    cols = shape[-1]
    rows = w.size // cols
    if rows * cols * 4 <= (2 << 20):
        tr = rows
    else:
        tr = next(t for t in (1024, 512, 256, 128, 64, 32, 16, 8) if rows % t == 0 and (t * cols * 4 <= (2 << 20) or t == 8))

    n_out = 4 if copy_g else 3

    def body(w_ref, g_ref, m_ref, v_ref, *rest):
        d_ref, nm_ref, nv_ref = rest[-n_out:][:3]
        gr = g_ref[...]
        if copy_g:
            rest[-1][...] = gr
        nm = ADAM_B1 * m_ref[...] + (1.0 - ADAM_B1) * gr
        nv = ADAM_B2 * v_ref[...] + (1.0 - ADAM_B2) * (gr * gr)
        m_hat = nm / (1.0 - ADAM_B1 ** ADAM_STEP)
        v_hat = nv / (1.0 - ADAM_B2 ** ADAM_STEP)
        d_ref[...] = -ADAM_LR * (m_hat / (jnp.sqrt(v_hat) + ADAM_EPS) + ADAM_WD * w_ref[...])
        nm_ref[...] = nm
        nv_ref[...] = nv

    blk = pl.BlockSpec((tr, cols), lambda i: (i, 0))
    out = jax.ShapeDtypeStruct((rows, cols), F32)
    res = pl.pallas_call(
        body, name=name, grid=(rows // tr,), in_specs=[blk] * 4 + ([] if dep is None else [ANY]), out_specs=[blk] * n_out,
        out_shape=[out] * n_out, compiler_params=_params(("parallel",), VMEM_BIG),
    )(*[t.reshape(rows, cols) for t in (w, g, m, v)], *([] if dep is None else [dep]))
    return [t.reshape(shape) for t in res]


def _place():
    x, y, c = lax.axis_index("x"), lax.axis_index("y"), lax.axis_index("c")
    others = [(1 - x, y), (x, 1 - y), (1 - x, 1 - y)]
    return x, y, c, others


def _remote(src, dst, send_sem, recv_sem, device):
    return pltpu.make_async_remote_copy(src_ref=src, dst_ref=dst, send_sem=send_sem, recv_sem=recv_sem, device_id=device,
                                        device_id_type=MESH)


def _hbm_call(body, name, args, out_shapes, n_sems, n_local, aliases=None):
    return pl.pallas_call(
        body, name=name, in_specs=[ANY] * len(args), out_specs=[ANY] * len(out_shapes), out_shape=out_shapes,
        scratch_shapes=[pltpu.SemaphoreType.DMA((n_sems,)), pltpu.SemaphoreType.DMA((n_sems,)),
                        pltpu.SemaphoreType.DMA((max(n_local, 1),))],
        input_output_aliases=aliases or {}, compiler_params=pltpu.CompilerParams(has_side_effects=True),
    )(*args)


def _all_gather_weights(halved, whole, *, name):
    nh, nw = len(halved), len(whole)
    arrays = list(halved) + list(whole)

    def body(*refs):
        srcs, outs = refs[:nh + nw], refs[nh + nw:2 * (nh + nw)]
        send, recv, local = refs[2 * (nh + nw):]
        x, y, c, others = _place()
        me = 2 * x + y
        locals_ = [pltpu.make_async_copy(srcs[nh + a], outs[nh + a].at[me], local.at[a]) for a in range(nw)]
        for cp in locals_:
            cp.start()
        sends = []
        for a in range(nh):
            half = outs[a].shape[1] // 2
            rows = pl.ds(c * half, half)
            for j, (px, py) in enumerate(others):
                sends.append(_remote(outs[a].at[me, rows], outs[a].at[me, rows], send.at[6 * a + j], recv.at[6 * a + j],
                                     (px, py, c)))
        for a in range(nw):
            for j, (px, py) in enumerate(others):
                sends.append(_remote(srcs[nh + a], outs[nh + a].at[me], send.at[6 * nh + 3 * a + j],
                                     recv.at[6 * nh + 3 * a + j], (px, py, c)))
        for cp in sends:
            cp.start()
        for a in range(nh):
            half = outs[a].shape[1] // 2
            rows = pl.ds(c * half, half)
            for j, (px, py) in enumerate(others):
                got = outs[a].at[2 * px + py, rows]
                _remote(got, got, send.at[6 * a + j], recv.at[6 * a + j], (px, py, c)).wait_recv()
                fwd = _remote(got, got, send.at[6 * a + 3 + j], recv.at[6 * a + 3 + j], (x, y, 1 - c))
                fwd.start()
                sends.append(fwd)
        for a in range(nh):
            half = outs[a].shape[1] // 2
            theirs = pl.ds((1 - c) * half, half)
            for j, (px, py) in enumerate(others):
                got = outs[a].at[2 * px + py, theirs]
                _remote(got, got, send.at[6 * a + 3 + j], recv.at[6 * a + 3 + j], (x, y, 1 - c)).wait_recv()
        for a in range(nw):
            for j, (px, py) in enumerate(others):
                got = outs[nh + a].at[2 * px + py]
                _remote(got, got, send.at[6 * nh + 3 * a + j], recv.at[6 * nh + 3 * a + j], (px, py, c)).wait_recv()
        for cp in sends:
            cp.wait_send()
        for cp in locals_:
            cp.wait()

    out_shapes = [jax.ShapeDtypeStruct(t.shape, t.dtype) for t in halved]
    out_shapes += [jax.ShapeDtypeStruct((N_CHIPS,) + t.shape, t.dtype) for t in whole]
    return _hbm_call(body, name, arrays, out_shapes, 6 * nh + 3 * nw, nw, aliases={a: a for a in range(nh)})


def _cast_into(w, layer, place, *, name, dep=None):
    _, r, cols = w.shape
    tr = _row_tile(r, cols)

    def body(place_ref, w_ref, *rest):
        rest[-1][...] = w_ref[...].astype(BF16)

    in_specs, args = [pl.BlockSpec((None, tr, cols), lambda i, pr: (layer, i, 0))], [place, w]
    if dep is not None:
        in_specs.append(ANY)
        args.append(dep)
    grid_spec = pltpu.PrefetchScalarGridSpec(
        num_scalar_prefetch=1, grid=(r // tr,), in_specs=in_specs,
        out_specs=pl.BlockSpec((None, tr, cols), lambda i, pr: (pr[1], i, 0)),
    )
    return pl.pallas_call(
        body, name=name, grid_spec=grid_spec, out_shape=jax.ShapeDtypeStruct((N_CHIPS, r, cols), BF16),
        compiler_params=_params(("parallel",), VMEM_BIG),
    )(*args)


HBM = pl.BlockSpec(memory_space=pltpu.HBM)
SEM = pl.BlockSpec(memory_space=pltpu.SEMAPHORE)
EFFECT = pltpu.SideEffectType.DATAFLOW_SIDE_EFFECTING
TOKEN = jax.ShapeDtypeStruct((8, 128), F32)


def _in_hbm(t):
    return pltpu.with_memory_space_constraint(t, pltpu.HBM)


def _gather_copies(buf, send, recv):
    x, y, c, others = _place()
    half = buf.shape[1] // 2
    rows = pl.ds(c * half, half)
    return [_remote(buf.at[2 * x + y, rows], buf.at[2 * x + y, rows], send.at[j], recv.at[j], (px, py, c))
            for j, (px, py) in enumerate(others)]


def _gather_start(bufs, *, name):
    n = len(bufs)

    def body(*refs):
        ins, sends, recvs, token = refs[:n], refs[n:2 * n], refs[2 * n:3 * n], refs[4 * n]
        for a in range(n):
            for cp in _gather_copies(ins[a], sends[a], recvs[a]):
                cp.start()
        token[...] = jnp.zeros_like(token)

    sems = [pltpu.SemaphoreType.DMA((3,))] * (2 * n)
    res = pl.pallas_call(
        body, name=name, out_shape=sems + [pltpu.HBM(t.shape, t.dtype) for t in bufs] + [TOKEN],
        in_specs=[HBM] * n, out_specs=[SEM] * (2 * n) + [HBM] * n + [pl.BlockSpec(memory_space=pltpu.VMEM)],
        input_output_aliases={a: 2 * n + a for a in range(n)}, compiler_params=pltpu.CompilerParams(has_side_effects=EFFECT),
    )(*[_in_hbm(t) for t in bufs])
    return [(res[2 * n + a], res[a], res[n + a]) for a in range(n)], res[3 * n]


def _gather_wait(state, after, *, name):
    buf, send, recv = state

    def body(buf_ref, send_ref, recv_ref, after_ref, out_ref):
        for cp in _gather_copies(buf_ref, send_ref, recv_ref):
            cp.wait_send()
            cp.wait_recv()

    return pl.pallas_call(
        body, name=name, out_shape=pltpu.HBM(buf.shape, buf.dtype), in_specs=[HBM, SEM, SEM, ANY], out_specs=HBM,
        input_output_aliases={0: 0}, compiler_params=pltpu.CompilerParams(has_side_effects=EFFECT),
    )(buf, send, recv, after)


def _chip_copies(src, land, send, recv):
    _x, _y, c, others = _place()
    return [_remote(src.at[2 * px + py], land.at[j], send.at[j], recv.at[j], (px, py, c)) for j, (px, py) in enumerate(others)]


def _chip_start(partial, *, name):
    def body(src, land, send, recv, _src_thru, _land_thru, token):
        for cp in _chip_copies(src, land, send, recv):
            cp.start()
        token[...] = jnp.zeros_like(token)

    land_shape = (3,) + partial.shape[1:]
    sem = pltpu.SemaphoreType.DMA((3,))
    send, recv, src, land, token = pl.pallas_call(
        body, name=name, out_shape=[sem, sem, pltpu.HBM(partial.shape, partial.dtype), pltpu.HBM(land_shape, partial.dtype), TOKEN],
        in_specs=[HBM, HBM], out_specs=[SEM, SEM, HBM, HBM, pl.BlockSpec(memory_space=pltpu.VMEM)],
        input_output_aliases={0: 2, 1: 3}, compiler_params=pltpu.CompilerParams(has_side_effects=EFFECT),
    )(_in_hbm(partial), _in_hbm(lax.empty(land_shape, partial.dtype)))
    return (src, land, send, recv), token


def _chip_wait(state, after, *, name):
    src, land, send, recv = state

    def body(src_ref, land_ref, send_ref, recv_ref, after_ref, _src_out, _land_out):
        for cp in _chip_copies(src_ref, land_ref, send_ref, recv_ref):
            cp.wait_send()
            cp.wait_recv()

    return pl.pallas_call(
        body, name=name, out_shape=[pltpu.HBM(src.shape, src.dtype), pltpu.HBM(land.shape, land.dtype)],
        in_specs=[HBM, HBM, SEM, SEM, ANY], out_specs=[HBM, HBM], input_output_aliases={0: 0, 1: 1},
        compiler_params=pltpu.CompilerParams(has_side_effects=EFFECT),
    )(src, land, send, recv, after)[1]


def _split_start(bufs, copies, n_copies, *, name):
    n = len(bufs)

    def body(*refs):
        for cp in copies(refs[:n], refs[n], refs[n + 1]):
            cp.start()
        refs[-1][...] = jnp.zeros_like(refs[-1])

    sem = pltpu.SemaphoreType.DMA((n_copies,))
    res = pl.pallas_call(
        body, name=name, out_shape=[sem, sem] + [pltpu.HBM(t.shape, t.dtype) for t in bufs] + [TOKEN],
        in_specs=[HBM] * n, out_specs=[SEM, SEM] + [HBM] * n + [pl.BlockSpec(memory_space=pltpu.VMEM)],
        input_output_aliases={a: 2 + a for a in range(n)}, compiler_params=pltpu.CompilerParams(has_side_effects=EFFECT),
    )(*[_in_hbm(t) for t in bufs])
    return (list(res[2:2 + n]), res[0], res[1]), res[-1]


def _split_wait(state, copies, after, *, name):
    bufs, send, recv = state
    n = len(bufs)

    def body(*refs):
        for cp in copies(refs[:n], refs[n], refs[n + 1]):
            cp.wait_send()
            cp.wait_recv()

    return list(pl.pallas_call(
        body, name=name, out_shape=[pltpu.HBM(t.shape, t.dtype) for t in bufs], in_specs=[HBM] * n + [SEM, SEM, ANY],
        out_specs=[HBM] * n, input_output_aliases={a: a for a in range(n)},
        compiler_params=pltpu.CompilerParams(has_side_effects=EFFECT),
    )(*bufs, send, recv, after))


def _hand_over_copies(refs, send, recv):
    x, y, c, others = _place()
    half = refs[0].shape[1] // 2
    got = [refs[0].at[2 * px + py, pl.ds(c * half, half)] for px, py in others]
    return [_remote(got[j], got[j], send.at[j], recv.at[j], (x, y, 1 - c)) for j in range(3)]


def _pair_copies(refs, send, recv):
    x, y, c, _o = _place()
    half = refs[0].shape[1] // 2
    return [_remote(refs[0].at[:, pl.ds((1 - c) * half, half), :], refs[1], send.at[0], recv.at[0], (x, y, 1 - c))]


def _share_copies(refs, send, recv):
    x, y, c, _o = _place()
    return [_remote(refs[0].at[:, c], refs[0].at[:, c], send.at[0], recv.at[0], (x, y, 1 - c))]


def _small_copies(refs, send, recv):
    x, y, c, others = _place()
    peers = [(x, y, 1 - c)] + [(px, py, pc) for px, py in others for pc in (c, 1 - c)]
    slot = refs[1].at[4 * x + 2 * y + c]
    return [_remote(refs[0], slot, send.at[k], recv.at[k], peer) for k, peer in enumerate(peers)]


def _row_tile(rows, cols):
    return max(t for t in range(16, rows + 1, 16) if rows % t == 0 and (t * cols * 4 <= (4 << 20) or t == 16))


def _pair_sum(grad, theirs, place, *, name):
    _, r, cols = grad.shape
    r2 = r // 2
    tr = _row_tile(r2, cols)
    nr = r2 // tr

    def body(place_ref, g_ref, t_ref, all_ref):
        all_ref[...] = (g_ref[...].astype(F32) + t_ref[...].astype(F32)).astype(all_ref.dtype)

    grid_spec = pltpu.PrefetchScalarGridSpec(
        num_scalar_prefetch=1, grid=(N_CHIPS, nr),
        in_specs=[pl.BlockSpec((None, tr, cols), lambda k, i, pr: (k, pr[0] * nr + i, 0)),
                  pl.BlockSpec((None, tr, cols), lambda k, i, pr: (k, i, 0))],
        out_specs=pl.BlockSpec((None, tr, cols), lambda k, i, pr: (k, i, 0)),
    )
    return pl.pallas_call(
        body, name=name, grid_spec=grid_spec, out_shape=jax.ShapeDtypeStruct((N_CHIPS, r2, cols), BF16),
        compiler_params=_params(("parallel", "parallel"), VMEM_BIG),
    )(place, grad, theirs)


def _chip_sum(grad, theirs, got, place, buf, layer, depth, *, name):
    _, r, cols = grad.shape
    r2 = r // 2
    tr = _row_tile(r2, cols)
    nr = r2 // tr

    def body(place_ref, g_ref, t_ref, got_ref, *rest):
        own = g_ref[...].astype(F32) + t_ref[...].astype(F32)
        rest[-1][...] = ((own + got_ref[0].astype(F32)) + got_ref[1].astype(F32)) + got_ref[2].astype(F32)

    in_specs = [pl.BlockSpec((None, tr, cols), lambda i, pr: (pr[1], pr[0] * nr + i, 0)),
                pl.BlockSpec((None, tr, cols), lambda i, pr: (pr[1], i, 0)),
                pl.BlockSpec((3, tr, cols), lambda i, pr: (0, i, 0))]
    args = [place, grad, theirs, got]
    if buf is not None:
        in_specs.append(ANY)
        args.append(buf)
    grid_spec = pltpu.PrefetchScalarGridSpec(
        num_scalar_prefetch=1, grid=(nr,), in_specs=in_specs,
        out_specs=pl.BlockSpec((None, None, tr, cols), lambda i, pr: (layer, pr[0], i, 0)),
    )
    return pl.pallas_call(
        body, name=name, grid_spec=grid_spec, out_shape=jax.ShapeDtypeStruct((depth, 2, r2, cols), F32),
        input_output_aliases={} if buf is None else {4: 0}, compiler_params=_params(("parallel",), VMEM_BIG),
    )(*args)


def _sum_devices(parts, own, place, *, name):
    _, rows, cols = parts.shape
    tr = rows if N_DEV * rows * cols * 4 <= (16 << 20) else _pick(rows, (256, 128, 64, 32, 16, 8))

    def body(place_ref, p_ref, own_ref, out_ref):
        me = 2 * place_ref[1] + place_ref[0]
        acc = None
        for dev in range(N_DEV):
            term = jnp.where(me == dev, own_ref[...], p_ref[dev])
            acc = term if acc is None else acc + term
        out_ref[...] = acc

    grid_spec = pltpu.PrefetchScalarGridSpec(
        num_scalar_prefetch=1, grid=(rows // tr,),
        in_specs=[pl.BlockSpec((N_DEV, tr, cols), lambda i, pr: (0, i, 0)), pl.BlockSpec((tr, cols), lambda i, pr: (i, 0))],
        out_specs=pl.BlockSpec((tr, cols), lambda i, pr: (i, 0)),
    )
    return pl.pallas_call(
        body, name=name, grid_spec=grid_spec, out_shape=jax.ShapeDtypeStruct((rows, cols), F32),
        compiler_params=_params(("parallel",), VMEM_BIG),
    )(place, parts, own)


def _pack(parts):
    rows = []
    for t in parts:
        flat = t.reshape(-1, 128)
        pad = (-flat.shape[0]) % 8
        rows.append(jnp.pad(flat, ((0, pad), (0, 0))) if pad else flat)
    return jnp.concatenate(rows, axis=0)


def _unpack(pack, shapes):
    out, r0 = [], 0
    for shp in shapes:
        n = math.prod(shp) // 128
        out.append(pack[r0:r0 + n].reshape(shp))
        r0 += n + (-n) % 8
    return out


SMALL = ["attn_norm_g", "q_norm_g", "k_norm_g", "sgu_norm_g", "sgu_w", "sgu_b", "out_norm_a_g", "out_norm_b_g",
         "ffn_norm_g", "conv_b"]
BIG = ["w_in", "w_out", "w_up", "w_down"]
ORDER = ["attn_norm_g", "w_in", "q_norm_g", "k_norm_g", "sgu_norm_g", "sgu_w", "sgu_b", "out_norm_a_g", "out_norm_b_g",
         "w_out", "ffn_norm_g", "w_up", "conv_w", "conv_b", "w_down"]


def kernel(x, attn_norm_g, w_in, q_norm_g, k_norm_g, sgu_norm_g, sgu_w, sgu_b, out_norm_a_g, out_norm_b_g, w_out, ffn_norm_g, w_up, conv_w, conv_b, w_down, loss_target, m_attn_norm_g, m_w_in, m_q_norm_g, m_k_norm_g, m_sgu_norm_g, m_sgu_w, m_sgu_b, m_out_norm_a_g, m_out_norm_b_g, m_w_out, m_ffn_norm_g, m_w_up, m_conv_w, m_conv_b, m_w_down, v_attn_norm_g, v_w_in, v_q_norm_g, v_k_norm_g, v_sgu_norm_g, v_sgu_w, v_sgu_b, v_out_norm_a_g, v_out_norm_b_g, v_w_out, v_ffn_norm_g, v_w_up, v_conv_w, v_conv_b, v_w_down):
    W = dict(attn_norm_g=attn_norm_g, w_in=w_in, q_norm_g=q_norm_g, k_norm_g=k_norm_g, sgu_norm_g=sgu_norm_g, sgu_w=sgu_w,
             sgu_b=sgu_b, out_norm_a_g=out_norm_a_g, out_norm_b_g=out_norm_b_g, w_out=w_out, ffn_norm_g=ffn_norm_g, w_up=w_up,
             conv_w=conv_w, conv_b=conv_b, w_down=w_down)
    M = dict(attn_norm_g=m_attn_norm_g, w_in=m_w_in, q_norm_g=m_q_norm_g, k_norm_g=m_k_norm_g, sgu_norm_g=m_sgu_norm_g,
             sgu_w=m_sgu_w, sgu_b=m_sgu_b, out_norm_a_g=m_out_norm_a_g, out_norm_b_g=m_out_norm_b_g, w_out=m_w_out,
             ffn_norm_g=m_ffn_norm_g, w_up=m_w_up, conv_w=m_conv_w, conv_b=m_conv_b, w_down=m_w_down)
    V = dict(attn_norm_g=v_attn_norm_g, w_in=v_w_in, q_norm_g=v_q_norm_g, k_norm_g=v_k_norm_g, sgu_norm_g=v_sgu_norm_g,
             sgu_w=v_sgu_w, sgu_b=v_sgu_b, out_norm_a_g=v_out_norm_a_g, out_norm_b_g=v_out_norm_b_g, w_out=v_w_out,
             ffn_norm_g=v_ffn_norm_g, w_up=v_w_up, conv_w=v_conv_w, conv_b=v_conv_b, w_down=v_w_down)
    depth = w_in.shape[0]
    s, d = x.shape[1], x.shape[2]
    n_heads = out_norm_a_g.shape[1]
    core = lax.axis_index("c")
    chip = 2 * lax.axis_index("x") + lax.axis_index("y")
    place = jnp.stack([core, chip]).astype(jnp.int32)
    xs = x.reshape(s, d)

    f_local = conv_w.shape[2]
    taps = lax.dynamic_update_slice(jnp.zeros((N_CHIPS, 16, f_local), F32), conv_w.reshape(1, depth * 3, f_local),
                                    (chip, 0, 0))
    order = [(l, n) for l in range(depth) for n in BIG]
    first, token = _gather_start([_cast_into(w_in, 0, place, name="cast_w_in"), taps], name="gather_start_first")
    rest, token = _gather_start([_cast_into(W[n], l, place, dep=token, name=f"cast_{n}") for l, n in order[1:]],
                                name="gather_start_rest")
    states = dict(zip(order, [first[0]] + rest))

    states["taps"] = first[1]

    def landed(key, after, tag):
        buf = _gather_wait(states[key], after, name=f"gather_wait_{tag}")
        return _split_start([buf], _hand_over_copies, 3, name=f"hand_over_{tag}")

    def whole(state, after, tag):
        return _split_wait(state, _hand_over_copies, after, name=f"hand_over_wait_{tag}")[0]

    saved, full = [], []
    cur = xs
    for l in range(depth):
        gain = attn_norm_g[l] + token[0, 0] if l == 0 else attn_norm_g[l]
        h = _rmsnorm_fwd(cur, gain, name="attn_norm")
        if l == 0:
            ho_in, _ = landed((0, "w_in"), h, "w_in")
        w_in_l = whole(ho_in, h, "w_in")
        if l == 0:
            ho_taps, token = landed("taps", w_in_l, "taps")
        p = _mm(h, w_in_l, "nn", b_split=N_CHIPS, caps=(2048, 256, 2048), dep=token, name="proj_in")
        if l == 0:
            taps = whole(ho_taps, p, "taps")[:, :depth * 3].reshape(N_CHIPS, depth, 3, f_local)
            cw_full = jnp.transpose(taps, (1, 2, 0, 3)).reshape(depth, 3, N_CHIPS * f_local)
        mix, o_raw, lsum = _attn_fwd(p, q_norm_g[l], k_norm_g[l], out_norm_a_g[l], n_heads, name="attn_fwd")
        ho_out, token = landed((l, "w_out"), mix, "w_out")
        mix = _sgu_fwd(p, mix, sgu_w[l], sgu_b[l], sgu_norm_g[l] + token[0, 0], out_norm_b_g[l], n_heads, name="sgu_fwd")
        w_out_l = whole(ho_out, mix, "w_out").reshape(-1, d)
        ho_up, token = landed((l, "w_up"), w_out_l, "w_up")
        x1 = _mm(mix, w_out_l, "nn", a_split=2, res=cur, caps=(2048, 512, 1024), dep=token, name="proj_out")
        h2 = _rmsnorm_fwd(x1, ffn_norm_g[l], name="ffn_norm")
        w_up_l = whole(ho_up, h2, "w_up")
        up_gate, up_val, act = _up_conv_fwd(h2, w_up_l, cw_full[l], conv_b[l], name="ffn_up_conv")
        ho_down, token = landed((l, "w_down"), act, "w_down")
        w_down_l = whole(ho_down, act, "w_down").reshape(-1, d)
        if l + 1 < depth:
            ho_in, token = landed((l + 1, "w_in"), w_down_l, "w_in")
        x2 = _mm(act, w_down_l, "nn", res=x1, caps=(1024, 512, 2816), dep=token, name="ffn_down")
        full.append(dict(w_in=w_in_l, w_out=w_out_l, w_up=w_up_l, w_down=w_down_l, conv_w=cw_full[l]))
        saved.append(dict(x0=cur, h=h, p=p, o_raw=o_raw, lsum=lsum, mix=mix, x1=x1, h2=h2, up_gate=up_gate, up_val=up_val,
                          act=act))
        cur = x2

    dx, dxb, sq = _loss_head(cur, loss_target.reshape(s, d), name="loss_head")
    loss = lax.psum(sq[0, 0] * (0.5 / d), ("x", "y", "c"))

    small_grads = {n: [None] * depth for n in SMALL + ["conv_w"]}
    def pair_begin(n, grad):
        land = lax.empty((N_CHIPS, grad.shape[1] // 2, grad.shape[2]), grad.dtype)
        return _split_start([grad, land], _pair_copies, 1, name=f"pair_start_{n}")

    def chip_begin(n, state, after):
        grad, theirs = _split_wait(state, _pair_copies, after, name=f"pair_wait_{n}")
        state, tok = _chip_start(_pair_sum(grad, theirs, place, name=f"pair_sum_{n}"), name=f"chip_start_{n}")
        return (grad, theirs, state), tok

    pending = {}
    for l in reversed(range(depth)):
        fw, sv = full[l], saved[l]
        g_down = _mm(sv["act"], dxb, "tn", caps=(512, 2048, 2048), out_dtype=BF16, name="g_down")
        pair, tok = pair_begin("w_down", g_down.reshape(N_CHIPS, -1, d))
        dup, dcw, dcb = _conv_bwd(sv["up_gate"], sv["up_val"], dxb, fw["w_down"], fw["conv_w"], conv_b[l] + tok[0, 0],
                                  name="conv_bwd")
        pending[(l, "w_down")], tok = chip_begin("w_down", pair, dup)
        g_up = _mm(sv["h2"], dup, "tn", b_split=2, o_split=N_CHIPS, caps=(2048, 256, 2048), out_dtype=BF16, dep=tok,
                   name="g_up")
        pair, tok = pair_begin("w_up", g_up)
        dh2 = _mm(dup, fw["w_up"], "nt", a_split=2, b_split=N_CHIPS, caps=(1024, 512, 2816), dep=tok, name="d_h2")
        pending[(l, "w_up")], tok = chip_begin("w_up", pair, dh2)
        dx1, dx1b, dg_ffn = _rmsnorm_bwd(sv["x1"], ffn_norm_g[l] + tok[0, 0], dh2, dx, name="ffn_norm_bwd")
        dmix = _mm(dx1b, fw["w_out"], "nt", caps=(2048, 512, 2048), name="d_mix")
        g_out = _mm(sv["mix"], dx1b, "tn", a_split=2, caps=(512, 2048, 2048), out_dtype=BF16, name="g_out")
        pair, tok = pair_begin("w_out", g_out.reshape(N_CHIPS, -1, d))
        dqkv, dgq, dgk, dgoa = _attn_bwd(sv["p"], sv["o_raw"], sv["lsum"], dmix, q_norm_g[l] + tok[0, 0], k_norm_g[l],
                                         out_norm_a_g[l], n_heads, name="attn_bwd")
        pending[(l, "w_out")], tok = chip_begin("w_out", pair, dqkv)
        duv, dsw, dsb, dgv, dgob = _sgu_bwd(sv["p"], dmix, sgu_w[l], sgu_b[l], sgu_norm_g[l] + tok[0, 0], out_norm_b_g[l],
                                            n_heads, name="sgu_bwd")
        dp = jnp.concatenate([dqkv[0], dqkv[1], dqkv[2], duv[0], duv[1]], axis=1)
        g_in = _mm(sv["h"], dp, "tn", o_split=N_CHIPS, caps=(2048, 256, 2048), out_dtype=BF16, name="g_in")
        pair, tok = pair_begin("w_in", g_in)
        dh = _mm(dp, fw["w_in"], "nt", b_split=N_CHIPS, caps=(2048, 512, 1280), dep=tok, name="d_h")
        pending[(l, "w_in")], tok = chip_begin("w_in", pair, dh)
        dx, dxb, dg_attn = _rmsnorm_bwd(sv["x0"], attn_norm_g[l] + tok[0, 0], dh, dx1, name="attn_norm_bwd")

        small_grads["attn_norm_g"][l] = dg_attn.reshape(d)
        small_grads["q_norm_g"][l] = jnp.sum(dgq, axis=(0, 1))
        small_grads["k_norm_g"][l] = jnp.sum(dgk, axis=(0, 1))
        small_grads["sgu_norm_g"][l] = dgv.reshape(-1, BLK)
        small_grads["sgu_w"][l] = dsw
        small_grads["sgu_b"][l] = dsb.reshape(-1, BLK)
        small_grads["out_norm_a_g"][l] = dgoa.reshape(-1, BLK)
        small_grads["out_norm_b_g"][l] = dgob.reshape(-1, BLK)
        small_grads["ffn_norm_g"][l] = dg_ffn.reshape(d)
        small_grads["conv_b"][l] = dcb[:, 0, :].reshape(-1)
        small_grads["conv_w"][l] = jnp.transpose(dcw[:, :3, :], (1, 0, 2)).reshape(3, -1)

    names = SMALL + ["conv_w"]
    pack = _pack([jnp.stack(small_grads[n]) for n in names])
    small, tok = _split_start([pack, lax.empty((N_DEV,) + pack.shape, F32)], _small_copies, N_DEV - 1, name="small_start")
    G, D_, NM, NV = {}, {}, {}, {}
    after, prev = tok, None
    for n in ("w_down", "w_up", "w_out", "w_in"):
        buf = None
        for l in reversed(range(depth)):
            grad, theirs, state = pending[(l, n)]
            got = _chip_wait(state, after, name=f"chip_wait_{n}")
            buf = _chip_sum(grad, theirs, got, place, buf, l, depth, name=f"chip_sum_{n}")
            after = buf
        share, tok = _split_start([buf], _share_copies, 1, name=f"share_start_{n}")
        if prev is not None:
            D_[prev], NM[prev], NV[prev], G[prev] = _adamw(W[prev], G[prev], M[prev], V[prev], dep=tok, copy_g=True,
                                                           name=f"adamw_{prev}")
            after = NV[prev]
        G[n] = _split_wait(share, _share_copies, after, name=f"share_wait_{n}")[0].reshape(W[n].shape)
        after, prev = G[n], n
    D_[prev], NM[prev], NV[prev], G[prev] = _adamw(W[prev], G[prev], M[prev], V[prev], copy_g=True, name=f"adamw_{prev}")

    pack, parts = _split_wait(small, _small_copies, NV[prev], name="small_wait")
    total = _sum_devices(parts, pack, place, name="sum_small")
    f_full = conv_b.shape[1]
    shapes = [W[n].shape for n in SMALL] + [(depth, 3, f_full)]
    for n, t in zip(names, _unpack(total, shapes)):
        G[n] = t
    G["conv_w"] = lax.dynamic_slice_in_dim(G["conv_w"], chip * f_local, f_local, axis=2)

    D_["conv_w"], NM["conv_w"], NV["conv_w"] = _adamw(conv_w, G["conv_w"], m_conv_w, v_conv_w, name="adamw_conv_w")
    small_shapes = [W[n].shape for n in SMALL]
    res = _adamw(_pack([W[n] for n in SMALL]), _pack([G[n] for n in SMALL]), _pack([M[n] for n in SMALL]),
                 _pack([V[n] for n in SMALL]), name="adamw_small")
    for dst, t in zip((D_, NM, NV), res):
        for n, u in zip(SMALL, _unpack(t, small_shapes)):
            dst[n] = u

    return (loss, dx.reshape(x.shape), *[G[n] for n in ORDER], *[D_[n] for n in ORDER], *[NM[n] for n in ORDER],
            *[NV[n] for n in ORDER])
```

```python
import functools
import math

import jax
import jax.numpy as jnp
from jax import lax
from jax.experimental import pallas as pl
from jax.experimental.pallas import tpu as pltpu

F32 = jnp.float32
BF16 = jnp.bfloat16
EPS = 1e-6
BLK = 128
N_CHIPS = 4
N_DEV = 8
ADAM_LR, ADAM_B1, ADAM_B2, ADAM_EPS, ADAM_WD, ADAM_STEP = 0.001, 0.9, 0.999, 1e-08, 0.01, 10
VMEM_BIG = 48 * 1024 * 1024
MESH = pl.DeviceIdType.MESH
ANY = pl.BlockSpec(memory_space=pl.ANY)


def _pick(dim, prefs):
    for t in prefs:
        if dim % t == 0:
            return t
    raise ValueError(f"no tile in {prefs} divides {dim}")


def _params(sem=None, vmem=None):
    return pltpu.CompilerParams(dimension_semantics=sem, vmem_limit_bytes=vmem)


def _ldims(arr, split):
    if split == 1:
        return arr.shape
    p, r, cs = arr.shape
    assert p == split
    return (r, p * cs)


def _spec(tr, tc, split, cols, rc):
    if split == 1:
        return pl.BlockSpec((tr, tc), lambda i, j, k: rc(i, j, k))
    per = (cols // split) // tc

    def imap(i, j, k):
        r, c = rc(i, j, k)
        return (c // per, r, c % per)

    return pl.BlockSpec((None, tr, tc), imap)


def _fit(unit, cap):
    return max(t for t in range(128, min(unit, cap) + 1, 128) if unit % t == 0)


def _mm(a, b, mode, *, name, caps, a_split=1, b_split=1, o_split=1, out_dtype=F32, res=None, dep=None):
    ar, ac = _ldims(a, a_split)
    br, bc = _ldims(b, b_split)
    if mode == "nn":
        m, k, n = ar, ac, bc
        assert br == k
        ku, nu, mu = math.gcd(k // a_split, k), math.gcd(n // b_split, n // o_split), m
    elif mode == "nt":
        m, k, n = ar, ac, br
        assert bc == k
        ku, nu, mu = math.gcd(k // a_split, k // b_split), n // o_split, m
    else:
        k, m, n = ar, ac, bc
        assert br == k
        ku, nu, mu = k, math.gcd(n // b_split, n // o_split), m // a_split
    tm, tn, tk = _fit(mu, caps[0]), _fit(nu, caps[1]), _fit(ku, caps[2])
    nk = k // tk
    if mode == "nn":
        a_spec = _spec(tm, tk, a_split, k, lambda i, j, kk: (i, kk))
        b_spec = _spec(tk, tn, b_split, n, lambda i, j, kk: (kk, j))
    elif mode == "nt":
        a_spec = _spec(tm, tk, a_split, k, lambda i, j, kk: (i, kk))
        b_spec = _spec(tn, tk, b_split, k, lambda i, j, kk: (j, kk))
    else:
        a_spec = _spec(tk, tm, a_split, m, lambda i, j, kk: (kk, i))
        b_spec = _spec(tk, tn, b_split, n, lambda i, j, kk: (kk, j))
    o_spec = _spec(tm, tn, o_split, n, lambda i, j, kk: (i, j))
    dims = {"nn": (((1,), (0,)), ((), ())), "nt": (((1,), (1,)), ((), ())), "tn": (((0,), (0,)), ((), ()))}[mode]

    def body(a_ref, b_ref, *rest):
        if dep is not None:
            rest = rest[1:]
        if res is None:
            o_ref, acc = rest
        else:
            r_ref, o_ref, acc = rest
        kk = pl.program_id(2)

        @pl.when(kk == 0)
        def _():
            acc[...] = jnp.zeros_like(acc)

        acc[...] += lax.dot_general(a_ref[...].astype(BF16), b_ref[...].astype(BF16), dims, preferred_element_type=F32)

        @pl.when(kk == nk - 1)
        def _():
            out = acc[...]
            if res is not None:
                out = out + r_ref[...]
            o_ref[...] = out.astype(o_ref.dtype)

    in_specs, args = [a_spec, b_spec], [a, b]
    if dep is not None:
        in_specs.append(ANY)
        args.append(dep)
    if res is not None:
        in_specs.append(pl.BlockSpec((tm, tn), lambda i, j, kk: (i, j)))
        args.append(res)
    out_shape = (m, n) if o_split == 1 else (o_split, m, n // o_split)
    return pl.pallas_call(
        body, name=name, grid=(m // tm, n // tn, nk), in_specs=in_specs, out_specs=o_spec,
        out_shape=jax.ShapeDtypeStruct(out_shape, out_dtype), scratch_shapes=[pltpu.VMEM((tm, tn), F32)],
        compiler_params=_params(("parallel", "parallel", "arbitrary"), VMEM_BIG),
    )(*args)


def _rstd(v):
    return lax.rsqrt(jnp.mean(v * v, axis=-1, keepdims=True) + EPS)


def _norm_bwd(v, r, gain, dout):
    a = dout * gain
    dv = r * (a - v * (r * r * jnp.mean(a * v, axis=-1, keepdims=True)))
    return dv, dout * v * r


def _rmsnorm_fwd(x, g, *, name):
    s, d = x.shape
    tr = _pick(s, (256, 128))

    def body(x_ref, g_ref, o_ref):
        v = x_ref[...]
        o_ref[...] = (v * _rstd(v) * g_ref[...]).astype(o_ref.dtype)

    return pl.pallas_call(
        body, name=name, grid=(s // tr,),
        in_specs=[pl.BlockSpec((tr, d), lambda i: (i, 0)), pl.BlockSpec((1, d), lambda i: (0, 0))],
        out_specs=pl.BlockSpec((tr, d), lambda i: (i, 0)), out_shape=jax.ShapeDtypeStruct((s, d), BF16),
        compiler_params=_params(("parallel",)),
    )(x, g.reshape(1, d))


def _rmsnorm_bwd(x, g, dh, dres, *, name):
    s, d = x.shape
    tr = _pick(s, (256, 128))

    def body(x_ref, g_ref, dh_ref, dres_ref, dx_ref, dxb_ref, dg_ref):
        v = x_ref[...]
        dv, dgr = _norm_bwd(v, _rstd(v), g_ref[...], dh_ref[...])
        dx = dres_ref[...] + dv
        dx_ref[...] = dx
        dxb_ref[...] = dx.astype(BF16)
        part = jnp.sum(dgr, axis=0, keepdims=True)

        @pl.when(pl.program_id(0) == 0)
        def _():
            dg_ref[...] = part

        @pl.when(pl.program_id(0) > 0)
        def _():
            dg_ref[...] += part

    row = pl.BlockSpec((tr, d), lambda i: (i, 0))
    one = pl.BlockSpec((1, d), lambda i: (0, 0))
    return pl.pallas_call(
        body, name=name, grid=(s // tr,), in_specs=[row, one, row, row], out_specs=[row, row, one],
        out_shape=[jax.ShapeDtypeStruct((s, d), F32), jax.ShapeDtypeStruct((s, d), BF16), jax.ShapeDtypeStruct((1, d), F32)],
        compiler_params=_params(("arbitrary",)),
    )(x, g.reshape(1, d), dh, dres)


def _loss_head(y, target, *, name):
    s, d = y.shape
    tr = _pick(s, (256, 128))

    def body(y_ref, t_ref, dy_ref, dyb_ref, ls_ref):
        e = y_ref[...] - t_ref[...]
        dy = e * (1.0 / d)
        dy_ref[...] = dy
        dyb_ref[...] = dy.astype(BF16)
        part = jnp.full(ls_ref.shape, jnp.sum(e * e), F32)

        @pl.when(pl.program_id(0) == 0)
        def _():
            ls_ref[...] = part

        @pl.when(pl.program_id(0) > 0)
        def _():
            ls_ref[...] += part

    row = pl.BlockSpec((tr, d), lambda i: (i, 0))
    return pl.pallas_call(
        body, name=name, grid=(s // tr,), in_specs=[row, row], out_specs=[row, row, pl.BlockSpec((8, 128), lambda i: (0, 0))],
        out_shape=[jax.ShapeDtypeStruct((s, d), F32), jax.ShapeDtypeStruct((s, d), BF16), jax.ShapeDtypeStruct((8, 128), F32)],
        compiler_params=_params(("arbitrary",)),
    )(y, target)


def _iota2(axis):
    return lax.broadcasted_iota(jnp.int32, (BLK, BLK), axis)


def _tri_sum(v, tri):
    hi = v.astype(BF16)
    lo = (v - hi.astype(F32)).astype(BF16)
    return jnp.dot(hi, tri, preferred_element_type=F32) + jnp.dot(lo, tri, preferred_element_type=F32)


def _dot_nt(a, b):
    return lax.dot_general(a, b, (((1,), (1,)), ((), ())), preferred_element_type=F32)


def _dot_tn(a, b):
    return lax.dot_general(a, b, (((0,), (0,)), ((), ())), preferred_element_type=F32)


TQ_MAX = 1024
TQ_MAX_BWD = 1024
HP = 2
VMEM_ATTN_BWD = 56 * 1024 * 1024


def _lanes(hh):
    return slice(hh * BLK, (hh + 1) * BLK)


def _causal(n, diag):
    if not diag:
        return None
    return lax.broadcasted_iota(jnp.int32, (n, BLK), 1) < lax.broadcasted_iota(jnp.int32, (n, BLK), 0)


def _sb_sums(z, mask, rhs_gt):
    lb = jnp.minimum(z, 0.0) - jnp.log(1.0 + jnp.exp(-jnp.abs(z)))
    l1m = lb - z
    if mask is not None:
        l1m = jnp.where(mask, l1m, 0.0)
    return lb, _tri_sum(l1m, rhs_gt)


def _below(old, new, r0):
    return new if r0 == 0 else jnp.concatenate([old[:r0], new], axis=0)


def _attn_fwd(p, gq, gk, go, n_heads, *, name):
    s = p.shape[0]
    tq = min(TQ_MAX, s)
    per = tq // BLK
    scale = BLK ** -0.5

    def body(q_ref, k_ref, v_ref, gq_ref, gk_ref, go_ref, att_ref, o_ref, l_ref, qn, kn, vb):
        for hh in range(HP):
            q = q_ref[:, _lanes(hh)]
            k = k_ref[:, _lanes(hh)]
            qn[:, _lanes(hh)] = (q * _rstd(q) * gq_ref[...] * scale).astype(BF16)
            kn[:, _lanes(hh)] = (k * _rstd(k) * gk_ref[...]).astype(BF16)
            vb[:, _lanes(hh)] = v_ref[:, _lanes(hh)].astype(BF16)
        rhs_gt = jnp.concatenate([(_iota2(0) > _iota2(1)).astype(BF16), jnp.ones((BLK, BLK), BF16)], axis=1)

        def step(q0, j, r0, diag, states):
            n = tq - r0
            rows = pl.ds(pl.multiple_of(q0 + r0, BLK), n)
            cols = pl.ds(pl.multiple_of(j * BLK, BLK), BLK)
            mask = _causal(n, diag)
            zs = [_dot_nt(qn[rows, _lanes(hh)], kn[cols, _lanes(hh)]) for hh in range(HP)]
            sums = [_sb_sums(z, mask, rhs_gt) for z in zs]
            new = []
            for hh in range(HP):
                acc, later = states[hh]
                lb, both = sums[hh]
                a = jnp.exp(lb + both[:, :BLK] + later[r0:])
                if diag:
                    a = jnp.where(mask, a, 0.0)
                acc_new = acc[r0:] + jnp.dot(a.astype(BF16), vb[cols, _lanes(hh)], preferred_element_type=F32)
                new.append((_below(acc, acc_new, r0), _below(later, later[r0:] + both[:, BLK:], r0)))
            return tuple(new)

        def q_block(i, _):
            q0 = i * tq
            zero = jnp.zeros((tq, BLK), F32)
            states = ((zero, zero),) * HP
            for jd in reversed(range(per)):
                states = step(q0, i * per + jd, jd * BLK, True, states)
            states = lax.fori_loop(0, i * per, lambda jj, st: step(q0, i * per - 1 - jj, 0, False, st), states)
            tile = pl.ds(pl.multiple_of(q0, tq), tq)
            for hh in range(HP):
                o, total = states[hh]
                o_ref[tile, _lanes(hh)] = o
                l_ref[hh, tile, :] = total
                att_ref[tile, _lanes(hh)] = (o * _rstd(o) * go_ref[hh]).astype(att_ref.dtype)
            return 0

        lax.fori_loop(0, s // tq, q_block, 0)

    assert n_heads % HP == 0
    groups = n_heads // HP

    def col(part):
        return pl.BlockSpec((s, HP * BLK), lambda g: (0, part * groups + g))

    gain = pl.BlockSpec((1, BLK), lambda g: (0, 0))
    per_head = pl.BlockSpec((HP, 1, BLK), lambda g: (g, 0, 0))
    return pl.pallas_call(
        body, name=name, grid=(groups,),
        in_specs=[col(0), col(1), col(2), gain, gain, per_head],
        out_specs=[pl.BlockSpec((None, s, HP * BLK), lambda g: (0, 0, g)), col(0), pl.BlockSpec((HP, s, BLK), lambda g: (g, 0, 0))],
        out_shape=[jax.ShapeDtypeStruct((2, s, n_heads * BLK), BF16), jax.ShapeDtypeStruct((s, n_heads * BLK), F32),
                   jax.ShapeDtypeStruct((n_heads, s, BLK), F32)],
        scratch_shapes=[pltpu.VMEM((s, HP * BLK), BF16)] * 3,
        compiler_params=_params(("parallel",), VMEM_BIG),
    )(p, p, p, gq.reshape(1, BLK), gk.reshape(1, BLK), go.reshape(n_heads, 1, BLK))


def _attn_bwd(p, o_raw, lsum, dmix, gq, gk, go, n_heads, *, name):
    s = p.shape[0]
    tq = min(TQ_MAX_BWD, s)
    per = tq // BLK
    scale = BLK ** -0.5

    def body(q_ref, k_ref, v_ref, o_ref, l_ref, da_ref, gq_ref, gk_ref, go_ref,
             dqkv_ref, dgq_ref, dgk_ref, dgo_ref, qn, kn, vb, dob, dqn, dkn, dvv):
        for hh in range(HP):
            q = q_ref[:, _lanes(hh)]
            k = k_ref[:, _lanes(hh)]
            qn[:, _lanes(hh)] = (q * _rstd(q) * gq_ref[...] * scale).astype(BF16)
            kn[:, _lanes(hh)] = (k * _rstd(k) * gk_ref[...]).astype(BF16)
            vb[:, _lanes(hh)] = v_ref[:, _lanes(hh)].astype(BF16)
            o = o_ref[:, _lanes(hh)]
            do, dgo_rows = _norm_bwd(o, _rstd(o), go_ref[hh], da_ref[:, _lanes(hh)])
            dob[:, _lanes(hh)] = do.astype(BF16)
            dgo_ref[hh] = jnp.sum(dgo_rows, axis=0, keepdims=True)
        dkn[...] = jnp.zeros_like(dkn)
        dvv[...] = jnp.zeros_like(dvv)
        ones = jnp.ones((BLK, BLK), BF16)
        rhs_gt = jnp.concatenate([(_iota2(0) > _iota2(1)).astype(BF16), ones], axis=1)
        rhs_lt = jnp.concatenate([(_iota2(0) < _iota2(1)).astype(BF16), ones], axis=1)

        def step(q0, j, r0, diag, states):
            n = tq - r0
            rows = pl.ds(pl.multiple_of(q0 + r0, BLK), n)
            cols = pl.ds(pl.multiple_of(j * BLK, BLK), BLK)
            mask = _causal(n, diag)
            zs = [_dot_nt(qn[rows, _lanes(hh)], kn[cols, _lanes(hh)]) for hh in range(HP)]
            das = [_dot_nt(dob[rows, _lanes(hh)], vb[cols, _lanes(hh)]) for hh in range(HP)]
            sums = [_sb_sums(z, mask, rhs_gt) for z in zs]
            mids = []
            for hh in range(HP):
                lb, both = sums[hh]
                upto = states[hh][0][r0:] + both[:, BLK:]
                a = jnp.exp(lb + both[:, :BLK] + (l_ref[hh, rows, :] - upto))
                if diag:
                    a = jnp.where(mask, a, 0.0)
                g = das[hh] * a
                mids.append((lb, upto, a, g, _tri_sum(g, rhs_lt)))
            new = []
            for hh in range(HP):
                seen, gsum, dq = states[hh]
                lb, upto, a, g, bothg = mids[hh]
                beta = jnp.exp(lb)
                dz = g * (1.0 - beta) - beta * (bothg[:, :BLK] + gsum[r0:])
                if diag:
                    dz = jnp.where(mask, dz, 0.0)
                dzs = dz.astype(BF16)
                dq_new = dq[r0:] + jnp.dot(dzs, kn[cols, _lanes(hh)], preferred_element_type=F32)
                dkn[cols, _lanes(hh)] += _dot_tn(dzs, qn[rows, _lanes(hh)])
                dvv[cols, _lanes(hh)] += _dot_tn(a.astype(BF16), dob[rows, _lanes(hh)])
                new.append((_below(seen, upto, r0), _below(gsum, gsum[r0:] + bothg[:, BLK:], r0), _below(dq, dq_new, r0)))
            return tuple(new)

        def q_block(i, _):
            q0 = i * tq
            zero = jnp.zeros((tq, BLK), F32)
            states = ((zero, zero, zero),) * HP
            states = lax.fori_loop(0, i * per, lambda j, st: step(q0, j, 0, False, st), states)
            for jd in range(per):
                states = step(q0, i * per + jd, jd * BLK, True, states)
            tile = pl.ds(pl.multiple_of(q0, tq), tq)
            for hh in range(HP):
                dqn[tile, _lanes(hh)] = states[hh][2]
            return 0

        lax.fori_loop(0, s // tq, q_block, 0)
        for hh in range(HP):
            q = q_ref[:, _lanes(hh)]
            k = k_ref[:, _lanes(hh)]
            dq_raw, dgq_rows = _norm_bwd(q, _rstd(q), gq_ref[...], dqn[:, _lanes(hh)] * scale)
            dk_raw, dgk_rows = _norm_bwd(k, _rstd(k), gk_ref[...], dkn[:, _lanes(hh)])
            dqkv_ref[0, :, _lanes(hh)] = dq_raw.astype(BF16)
            dqkv_ref[1, :, _lanes(hh)] = dk_raw.astype(BF16)
            dqkv_ref[2, :, _lanes(hh)] = dvv[:, _lanes(hh)].astype(BF16)
            dgq_ref[hh] = jnp.sum(dgq_rows, axis=0, keepdims=True)
            dgk_ref[hh] = jnp.sum(dgk_rows, axis=0, keepdims=True)

    assert n_heads % HP == 0
    groups = n_heads // HP

    once = pl.Buffered(1)

    def col(part):
        return pl.BlockSpec((s, HP * BLK), lambda g: (0, part * groups + g), pipeline_mode=once)

    gain = pl.BlockSpec((1, BLK), lambda g: (0, 0))
    per_head = pl.BlockSpec((HP, 1, BLK), lambda g: (g, 0, 0))
    head_gain = jax.ShapeDtypeStruct((n_heads, 1, BLK), F32)
    return pl.pallas_call(
        body, name=name, grid=(groups,),
        in_specs=[col(0), col(1), col(2), col(0), pl.BlockSpec((HP, s, BLK), lambda g: (g, 0, 0), pipeline_mode=once), col(0),
                  gain, gain, per_head],
        out_specs=[pl.BlockSpec((3, s, HP * BLK), lambda g: (0, 0, g)), per_head, per_head, per_head],
        out_shape=[jax.ShapeDtypeStruct((3, s, n_heads * BLK), BF16), head_gain, head_gain, head_gain],
        scratch_shapes=[pltpu.VMEM((s, HP * BLK), BF16)] * 4 + [pltpu.VMEM((s, HP * BLK), F32)] * 3,
        compiler_params=_params(("parallel",), VMEM_ATTN_BWD),
    )(p, p, p, o_raw, lsum, dmix, gq.reshape(1, BLK), gk.reshape(1, BLK), go.reshape(n_heads, 1, BLK))


SGU_TOGETHER = 4
_INV_SQRT2 = 0.7071067811865476
_INV_SQRT2PI = 0.3989422804014327


def _gelu(x):
    return 0.5 * x * (1.0 + lax.erf(x * _INV_SQRT2))


def _gelu_and_grad(x):
    cdf = 0.5 * (1.0 + lax.erf(x * _INV_SQRT2))
    return x * cdf, cdf + x * jnp.exp(-0.5 * x * x) * _INV_SQRT2PI


def _sgu_fwd(p, mix, w, b, gv, gout, n_heads, *, name):
    s = p.shape[0]
    n_groups = w.shape[0]
    nb = s // BLK
    assert mix.shape == (2, s, n_groups * BLK)

    def body(u_ref, v_ref, w_ref, b_ref, gv_ref, go_ref, _mix_ref, out_ref):
        wt = jnp.where(_iota2(0) >= _iota2(1), w_ref[...], 0.0).astype(BF16)
        bias = b_ref[...]

        def chunks(i, _):
            rows = [pl.ds(pl.multiple_of((i * SGU_TOGETHER + k) * BLK, BLK), BLK) for k in range(SGU_TOGETHER)]
            us = [_gelu(u_ref[r, :]) for r in rows]
            vss = []
            for r in rows:
                vv = _gelu(v_ref[r, :])
                vss.append((vv * _rstd(vv) * gv_ref[...]).astype(BF16))
            mixed = [jnp.dot(wt, vs, preferred_element_type=F32) + bias for vs in vss]
            for r, u, m in zip(rows, us, mixed):
                gated = u * m
                out_ref[r, :] = (gated * _rstd(gated) * go_ref[...]).astype(out_ref.dtype)
            return 0

        assert nb % SGU_TOGETHER == 0
        lax.fori_loop(0, nb // SGU_TOGETHER, chunks, 0)

    def col(off):
        return pl.BlockSpec((s, BLK), lambda g: (0, off + g))

    per_group = pl.BlockSpec((None, 1, BLK), lambda g: (g, 0, 0))
    return pl.pallas_call(
        body, name=name, grid=(n_groups,),
        in_specs=[col(3 * n_heads), col(3 * n_heads + n_groups), pl.BlockSpec((None, BLK, BLK), lambda g: (g, 0, 0)),
                  pl.BlockSpec((None, BLK, 1), lambda g: (g, 0, 0)), per_group, per_group, ANY],
        out_specs=pl.BlockSpec((None, s, BLK), lambda g: (1, 0, g)), out_shape=jax.ShapeDtypeStruct(mix.shape, mix.dtype),
        input_output_aliases={6: 0}, compiler_params=_params(("parallel",), VMEM_BIG),
    )(p, p, w, b.reshape(n_groups, BLK, 1), gv.reshape(n_groups, 1, BLK), gout.reshape(n_groups, 1, BLK), mix)


def _sgu_bwd(p, dmix, w, b, gv, gout, n_heads, *, name):
    s = p.shape[0]
    n_groups = w.shape[0]
    nb = s // BLK

    def body(u_ref, v_ref, ds_ref, w_ref, b_ref, gv_ref, go_ref, duv_ref, dw_ref, db_ref, dgv_ref, dgo_ref):
        lower = _iota2(0) >= _iota2(1)
        wt = jnp.where(lower, w_ref[...], 0.0).astype(BF16)
        bias = b_ref[...]

        def chunks(i, carry):
            dw, db, dgv, dgo = carry
            rows = [pl.ds(pl.multiple_of((i * SGU_TOGETHER + k) * BLK, BLK), BLK) for k in range(SGU_TOGETHER)]
            pre = []
            for r in rows:
                u, u_grad = _gelu_and_grad(u_ref[r, :])
                vv, vv_grad = _gelu_and_grad(v_ref[r, :])
                rv = _rstd(vv)
                pre.append((u, u_grad, vv, vv_grad, rv, (vv * rv * gv_ref[...]).astype(BF16)))
            mixed = [jnp.dot(wt, t[5], preferred_element_type=F32) + bias for t in pre]
            mid = []
            for r, t, m in zip(rows, pre, mixed):
                gated = t[0] * m
                dgated, dgo_rows = _norm_bwd(gated, _rstd(gated), go_ref[...], ds_ref[r, :])
                dmixed = dgated * t[0]
                duv_ref[0, r, :] = (dgated * m * t[1]).astype(BF16)
                dgo = dgo + jnp.sum(dgo_rows, axis=0, keepdims=True)
                db = db + jnp.sum(dmixed, axis=1, keepdims=True)
                mid.append(dmixed.astype(BF16))
            dvss = [_dot_tn(wt, dmb) for dmb in mid]
            for dmb, t in zip(mid, pre):
                dw = dw + _dot_nt(dmb, t[5])
            for r, t, dvs in zip(rows, pre, dvss):
                dvv, dgv_rows = _norm_bwd(t[2], t[4], gv_ref[...], dvs)
                duv_ref[1, r, :] = (dvv * t[3]).astype(BF16)
                dgv = dgv + jnp.sum(dgv_rows, axis=0, keepdims=True)
            return dw, db, dgv, dgo

        assert nb % SGU_TOGETHER == 0
        row0 = jnp.zeros((1, BLK), F32)
        dw, db, dgv, dgo = lax.fori_loop(0, nb // SGU_TOGETHER, chunks,
                                         (jnp.zeros((BLK, BLK), F32), jnp.zeros((BLK, 1), F32), row0, row0))
        dw_ref[...] = jnp.where(lower, dw, 0.0)
        db_ref[...] = db
        dgv_ref[...] = dgv
        dgo_ref[...] = dgo

    def col(off):
        return pl.BlockSpec((s, BLK), lambda g: (0, off + g))

    per_group = pl.BlockSpec((None, 1, BLK), lambda g: (g, 0, 0))
    square = pl.BlockSpec((None, BLK, BLK), lambda g: (g, 0, 0))
    column = pl.BlockSpec((None, BLK, 1), lambda g: (g, 0, 0))
    gain = jax.ShapeDtypeStruct((n_groups, 1, BLK), F32)
    return pl.pallas_call(
        body, name=name, grid=(n_groups,),
        in_specs=[col(3 * n_heads), col(3 * n_heads + n_groups), col(n_heads), square, column, per_group, per_group],
        out_specs=[pl.BlockSpec((2, s, BLK), lambda g: (0, 0, g)), square, column, per_group, per_group],
        out_shape=[jax.ShapeDtypeStruct((2, s, n_groups * BLK), BF16), jax.ShapeDtypeStruct((n_groups, BLK, BLK), F32),
                   jax.ShapeDtypeStruct((n_groups, BLK, 1), F32), gain, gain],
        compiler_params=_params(("parallel",), VMEM_BIG),
    )(p, p, dmix, w, b.reshape(n_groups, BLK, 1), gv.reshape(n_groups, 1, BLK), gout.reshape(n_groups, 1, BLK))


CONV_ROWS = 256
HALO = 8


def _shift_down(ref, r0, n, first):
    cur = ref[pl.ds(r0, n), :]
    prev = jnp.zeros((HALO, cur.shape[1]), F32) if first else ref[pl.ds(r0 - HALO, HALO), :]
    ext = jnp.concatenate([prev, cur], axis=0)
    return pltpu.roll(ext, 1, 0)[HALO:], pltpu.roll(ext, 2, 0)[HALO:], cur


def _shift_up(ref, r0, n, last):
    cur = ref[pl.ds(r0, n), :]
    nxt = jnp.zeros((HALO, cur.shape[1]), F32) if last else ref[pl.ds(r0 + n, HALO), :]
    ext = jnp.concatenate([cur, nxt], axis=0)
    return cur, pltpu.roll(ext, n + HALO - 1, 0)[:n], pltpu.roll(ext, n + HALO - 2, 0)[:n]


def _conv_rows(x1, x2, x0, w_ref, b_ref):
    return ((b_ref[...] + x2 * w_ref[0:1, :]) + x1 * w_ref[1:2, :]) + x0 * w_ref[2:3, :]


def _conv_specs(s, f, tc):
    nf = f // tc
    gate = pl.BlockSpec((s, tc), lambda n: (0, n))
    val = pl.BlockSpec((s, tc), lambda n: (0, nf + n))
    wg = pl.BlockSpec((3, tc), lambda n: (0, n))
    wv = pl.BlockSpec((3, tc), lambda n: (0, nf + n))
    bg = pl.BlockSpec((1, tc), lambda n: (0, n))
    bv = pl.BlockSpec((1, tc), lambda n: (0, nf + n))
    return nf, gate, val, wg, wv, bg, bv


def _up_conv_fwd(h, w_up, cw, cb, *, name):
    s, d = h.shape
    chips, _, per_chip = w_up.shape
    f = chips * per_chip // 2
    tc = _pick(math.gcd(f, per_chip), (256, 128))
    cr = min(CONV_ROWS, s)
    nf, gate, _val, wg, wv, bg, bv = _conv_specs(s, f, tc)
    per = per_chip // tc

    def body(h_ref, mg_ref, mv_ref, wg_ref, wv_ref, bg_ref, bv_ref, ug_ref, uv_ref, out_ref):
        hb = h_ref[...]
        ug_ref[...] = jnp.dot(hb, mg_ref[...], preferred_element_type=F32)
        uv_ref[...] = jnp.dot(hb, mv_ref[...], preferred_element_type=F32)
        for r0 in range(0, s, cr):
            gc = _conv_rows(*_shift_down(ug_ref, r0, cr, r0 == 0), wg_ref, bg_ref)
            vc = _conv_rows(*_shift_down(uv_ref, r0, cr, r0 == 0), wv_ref, bv_ref)
            out_ref[pl.ds(r0, cr), :] = (gc * jax.nn.sigmoid(gc) * vc).astype(out_ref.dtype)

    def cols(first):
        return pl.BlockSpec((None, d, tc), lambda n: ((first + n) // per, 0, (first + n) % per))

    half = jax.ShapeDtypeStruct((s, f), F32)
    return pl.pallas_call(
        body, name=name, grid=(nf,), in_specs=[pl.BlockSpec((s, d), lambda n: (0, 0)), cols(0), cols(nf), wg, wv, bg, bv],
        out_specs=[gate, gate, gate], out_shape=[half, half, jax.ShapeDtypeStruct((s, f), BF16)],
        compiler_params=_params(("arbitrary",), VMEM_BIG),
    )(h, w_up, w_up, cw, cw, cb.reshape(1, 2 * f), cb.reshape(1, 2 * f))


def _conv_bwd(up_gate, up_val, dy, w_down, cw, cb, *, name):
    s, f = up_gate.shape
    d = dy.shape[1]
    f2 = 2 * f
    tc = _pick(f, (256, 128))
    cr = min(CONV_ROWS, s)
    nf, gate, _val, wg, wv, bg, bv = _conv_specs(s, f, tc)

    def body(g_ref, v_ref, dy_ref, wd_first, wd_next, wg_ref, wv_ref, bg_ref, bv_ref, dup_ref, dw_ref, db_ref,
             dgc, dvc, da_ref, da_next):
        @pl.when(pl.program_id(0) == 0)
        def _():
            da_ref[...] = _dot_nt(dy_ref[...], wd_first[...])

        da_next[...] = _dot_nt(dy_ref[...], wd_next[...])
        zero = jnp.zeros((1, tc), F32)
        sums = [[zero] * 4, [zero] * 4]
        for r0 in range(0, s, cr):
            rows = pl.ds(r0, cr)
            gx = _shift_down(g_ref, r0, cr, r0 == 0)
            vx = _shift_down(v_ref, r0, cr, r0 == 0)
            gc = _conv_rows(*gx, wg_ref, bg_ref)
            vc = _conv_rows(*vx, wv_ref, bv_ref)
            sig = jax.nn.sigmoid(gc)
            da = da_ref[rows, :]
            d_gate = da * vc * (sig * (1.0 + gc * (1.0 - sig)))
            d_val = da * (gc * sig)
            dgc[rows, :] = d_gate
            dvc[rows, :] = d_val
            for part, (dc, (x1, x2, x0)) in enumerate(((d_gate, gx), (d_val, vx))):
                for tap, xs in enumerate((x2, x1, x0)):
                    sums[part][tap] = sums[part][tap] + jnp.sum(dc * xs, axis=0, keepdims=True)
                sums[part][3] = sums[part][3] + jnp.sum(dc, axis=0, keepdims=True)
        dw_ref[...] = jnp.zeros_like(dw_ref)
        db_ref[...] = jnp.zeros_like(db_ref)
        for part, (dc_ref, w_ref) in enumerate(((dgc, wg_ref), (dvc, wv_ref))):
            for tap in range(3):
                dw_ref[part, tap:tap + 1, :] = sums[part][tap]
            db_ref[part, 0:1, :] = sums[part][3]
            for r0 in range(0, s, cr):
                d0, d1, d2 = _shift_up(dc_ref, r0, cr, r0 + cr == s)
                dup_ref[part, pl.ds(r0, cr), :] = ((d0 * w_ref[2:3, :] + d1 * w_ref[1:2, :]) + d2 * w_ref[0:1, :]).astype(BF16)
        da_ref[...] = da_next[...]

    small = pl.BlockSpec((2, 8, tc), lambda n: (0, 0, n))
    return pl.pallas_call(
        body, name=name, grid=(nf,),
        in_specs=[gate, gate, pl.BlockSpec((s, d), lambda n: (0, 0)), pl.BlockSpec((tc, d), lambda n: (0, 0)),
                  pl.BlockSpec((tc, d), lambda n: (jnp.minimum(n + 1, nf - 1), 0)), wg, wv, bg, bv],
        out_specs=[pl.BlockSpec((2, s, tc), lambda n: (0, 0, n)), small, small],
        out_shape=[jax.ShapeDtypeStruct((2, s, f), BF16), jax.ShapeDtypeStruct((2, 8, f), F32),
                   jax.ShapeDtypeStruct((2, 8, f), F32)],
        scratch_shapes=[pltpu.VMEM((s, tc), F32)] * 4,
        compiler_params=_params(("arbitrary",), VMEM_BIG),
    )(up_gate, up_val, dy, w_down, w_down, cw, cw, cb.reshape(1, f2), cb.reshape(1, f2))


def _adamw(w, g, m, v, *, name, dep=None, copy_g=False):
    shape = w.shape
    cols = shape[-1]
    rows = w.size // cols
    if rows * cols * 4 <= (2 << 20):
        tr = rows
    else:
        tr = next(t for t in (1024, 512, 256, 128, 64, 32, 16, 8) if rows % t == 0 and (t * cols * 4 <= (2 << 20) or t == 8))

    n_out = 4 if copy_g else 3

    def body(w_ref, g_ref, m_ref, v_ref, *rest):
        d_ref, nm_ref, nv_ref = rest[-n_out:][:3]
        gr = g_ref[...]
        if copy_g:
            rest[-1][...] = gr
        nm = ADAM_B1 * m_ref[...] + (1.0 - ADAM_B1) * gr
        nv = ADAM_B2 * v_ref[...] + (1.0 - ADAM_B2) * (gr * gr)
        m_hat = nm / (1.0 - ADAM_B1 ** ADAM_STEP)
        v_hat = nv / (1.0 - ADAM_B2 ** ADAM_STEP)
        d_ref[...] = -ADAM_LR * (m_hat / (jnp.sqrt(v_hat) + ADAM_EPS) + ADAM_WD * w_ref[...])
        nm_ref[...] = nm
        nv_ref[...] = nv

    blk = pl.BlockSpec((tr, cols), lambda i: (i, 0))
    out = jax.ShapeDtypeStruct((rows, cols), F32)
    res = pl.pallas_call(
        body, name=name, grid=(rows // tr,), in_specs=[blk] * 4 + ([] if dep is None else [ANY]), out_specs=[blk] * n_out,
        out_shape=[out] * n_out, compiler_params=_params(("parallel",), VMEM_BIG),
    )(*[t.reshape(rows, cols) for t in (w, g, m, v)], *([] if dep is None else [dep]))
    return [t.reshape(shape) for t in res]


def _place():
    x, y, c = lax.axis_index("x"), lax.axis_index("y"), lax.axis_index("c")
    others = [(1 - x, y), (x, 1 - y), (1 - x, 1 - y)]
    return x, y, c, others


def _remote(src, dst, send_sem, recv_sem, device):
    return pltpu.make_async_remote_copy(src_ref=src, dst_ref=dst, send_sem=send_sem, recv_sem=recv_sem, device_id=device,
                                        device_id_type=MESH)


def _cast_into(w, layer, place, *, name, dep=None):
    _, r, cols = w.shape
    tr = _row_tile(r, cols)

    def body(place_ref, w_ref, *rest):
        rest[-1][...] = w_ref[...].astype(BF16)

    in_specs, args = [pl.BlockSpec((None, tr, cols), lambda i, pr: (layer, i, 0))], [place, w]
    if dep is not None:
        in_specs.append(ANY)
        args.append(dep)
    grid_spec = pltpu.PrefetchScalarGridSpec(
        num_scalar_prefetch=1, grid=(r // tr,), in_specs=in_specs,
        out_specs=pl.BlockSpec((None, tr, cols), lambda i, pr: (pr[1], i, 0)),
    )
    return pl.pallas_call(
        body, name=name, grid_spec=grid_spec, out_shape=jax.ShapeDtypeStruct((N_CHIPS, r, cols), BF16),
        compiler_params=_params(("parallel",), VMEM_BIG),
    )(*args)


HBM = pl.BlockSpec(memory_space=pltpu.HBM)
SEM = pl.BlockSpec(memory_space=pltpu.SEMAPHORE)
EFFECT = pltpu.SideEffectType.DATAFLOW_SIDE_EFFECTING
TOKEN = jax.ShapeDtypeStruct((8, 128), F32)


def _in_hbm(t):
    return pltpu.with_memory_space_constraint(t, pltpu.HBM)


def _gather_copies(buf, send, recv):
    x, y, c, others = _place()
    half = buf.shape[1] // 2
    rows = pl.ds(c * half, half)
    return [_remote(buf.at[2 * x + y, rows], buf.at[2 * x + y, rows], send.at[j], recv.at[j], (px, py, c))
            for j, (px, py) in enumerate(others)]


def _gather_start(bufs, *, name):
    n = len(bufs)

    def body(*refs):
        ins, sends, recvs, token = refs[:n], refs[n:2 * n], refs[2 * n:3 * n], refs[4 * n]
        for a in range(n):
            for cp in _gather_copies(ins[a], sends[a], recvs[a]):
                cp.start()
        token[...] = jnp.zeros_like(token)

    sems = [pltpu.SemaphoreType.DMA((3,))] * (2 * n)
    res = pl.pallas_call(
        body, name=name, out_shape=sems + [pltpu.HBM(t.shape, t.dtype) for t in bufs] + [TOKEN],
        in_specs=[HBM] * n, out_specs=[SEM] * (2 * n) + [HBM] * n + [pl.BlockSpec(memory_space=pltpu.VMEM)],
        input_output_aliases={a: 2 * n + a for a in range(n)}, compiler_params=pltpu.CompilerParams(has_side_effects=EFFECT),
    )(*[_in_hbm(t) for t in bufs])
    return [(res[2 * n + a], res[a], res[n + a]) for a in range(n)], res[3 * n]


def _gather_wait(state, after, *, name):
    buf, send, recv = state

    def body(buf_ref, send_ref, recv_ref, after_ref, out_ref):
        for cp in _gather_copies(buf_ref, send_ref, recv_ref):
            cp.wait_send()
            cp.wait_recv()

    return pl.pallas_call(
        body, name=name, out_shape=pltpu.HBM(buf.shape, buf.dtype), in_specs=[HBM, SEM, SEM, ANY], out_specs=HBM,
        input_output_aliases={0: 0}, compiler_params=pltpu.CompilerParams(has_side_effects=EFFECT),
    )(buf, send, recv, after)


def _chip_copies(src, land, send, recv):
    _x, _y, c, others = _place()
    return [_remote(src.at[2 * px + py], land.at[j], send.at[j], recv.at[j], (px, py, c)) for j, (px, py) in enumerate(others)]


def _chip_start(partial, *, name):
    def body(src, land, send, recv, _src_thru, _land_thru, token):
        for cp in _chip_copies(src, land, send, recv):
            cp.start()
        token[...] = jnp.zeros_like(token)

    land_shape = (3,) + partial.shape[1:]
    sem = pltpu.SemaphoreType.DMA((3,))
    send, recv, src, land, token = pl.pallas_call(
        body, name=name, out_shape=[sem, sem, pltpu.HBM(partial.shape, partial.dtype), pltpu.HBM(land_shape, partial.dtype), TOKEN],
        in_specs=[HBM, HBM], out_specs=[SEM, SEM, HBM, HBM, pl.BlockSpec(memory_space=pltpu.VMEM)],
        input_output_aliases={0: 2, 1: 3}, compiler_params=pltpu.CompilerParams(has_side_effects=EFFECT),
    )(_in_hbm(partial), _in_hbm(lax.empty(land_shape, partial.dtype)))
    return (src, land, send, recv), token


def _chip_wait(state, after, *, name):
    src, land, send, recv = state

    def body(src_ref, land_ref, send_ref, recv_ref, after_ref, _src_out, _land_out):
        for cp in _chip_copies(src_ref, land_ref, send_ref, recv_ref):
            cp.wait_send()
            cp.wait_recv()

    return pl.pallas_call(
        body, name=name, out_shape=[pltpu.HBM(src.shape, src.dtype), pltpu.HBM(land.shape, land.dtype)],
        in_specs=[HBM, HBM, SEM, SEM, ANY], out_specs=[HBM, HBM], input_output_aliases={0: 0, 1: 1},
        compiler_params=pltpu.CompilerParams(has_side_effects=EFFECT),
    )(src, land, send, recv, after)[1]


def _split_start(bufs, copies, n_copies, *, name):
    n = len(bufs)

    def body(*refs):
        for cp in copies(refs[:n], refs[n], refs[n + 1]):
            cp.start()
        refs[-1][...] = jnp.zeros_like(refs[-1])

    sem = pltpu.SemaphoreType.DMA((n_copies,))
    res = pl.pallas_call(
        body, name=name, out_shape=[sem, sem] + [pltpu.HBM(t.shape, t.dtype) for t in bufs] + [TOKEN],
        in_specs=[HBM] * n, out_specs=[SEM, SEM] + [HBM] * n + [pl.BlockSpec(memory_space=pltpu.VMEM)],
        input_output_aliases={a: 2 + a for a in range(n)}, compiler_params=pltpu.CompilerParams(has_side_effects=EFFECT),
    )(*[_in_hbm(t) for t in bufs])
    return (list(res[2:2 + n]), res[0], res[1]), res[-1]


def _split_wait(state, copies, after, *, name):
    bufs, send, recv = state
    n = len(bufs)

    def body(*refs):
        for cp in copies(refs[:n], refs[n], refs[n + 1]):
            cp.wait_send()
            cp.wait_recv()

    return list(pl.pallas_call(
        body, name=name, out_shape=[pltpu.HBM(t.shape, t.dtype) for t in bufs], in_specs=[HBM] * n + [SEM, SEM, ANY],
        out_specs=[HBM] * n, input_output_aliases={a: a for a in range(n)},
        compiler_params=pltpu.CompilerParams(has_side_effects=EFFECT),
    )(*bufs, send, recv, after))


def _hand_over_copies(refs, send, recv):
    x, y, c, others = _place()
    half = refs[0].shape[1] // 2
    got = [refs[0].at[2 * px + py, pl.ds(c * half, half)] for px, py in others]
    return [_remote(got[j], got[j], send.at[j], recv.at[j], (x, y, 1 - c)) for j in range(3)]


def _pair_copies(refs, send, recv):
    x, y, c, _o = _place()
    half = refs[0].shape[1] // 2
    return [_remote(refs[0].at[:, pl.ds((1 - c) * half, half), :], refs[1], send.at[0], recv.at[0], (x, y, 1 - c))]


def _share_copies(refs, send, recv):
    x, y, c, _o = _place()
    return [_remote(refs[0].at[:, c], refs[0].at[:, c], send.at[0], recv.at[0], (x, y, 1 - c))]


def _small_copies(refs, send, recv):
    x, y, c, others = _place()
    peers = [(x, y, 1 - c)] + [(px, py, pc) for px, py in others for pc in (c, 1 - c)]
    slot = refs[1].at[4 * x + 2 * y + c]
    return [_remote(refs[0], slot, send.at[k], recv.at[k], peer) for k, peer in enumerate(peers)]


def _row_tile(rows, cols):
    return max(t for t in range(16, rows + 1, 16) if rows % t == 0 and (t * cols * 4 <= (4 << 20) or t == 16))


def _pair_sum(grad, theirs, place, *, name):
    _, r, cols = grad.shape
    r2 = r // 2
    tr = _row_tile(r2, cols)
    nr = r2 // tr

    def body(place_ref, g_ref, t_ref, all_ref):
        all_ref[...] = (g_ref[...].astype(F32) + t_ref[...].astype(F32)).astype(all_ref.dtype)

    def other(k, pr):
        return k + (k >= pr[1]).astype(jnp.int32)

    grid_spec = pltpu.PrefetchScalarGridSpec(
        num_scalar_prefetch=1, grid=(N_CHIPS - 1, nr),
        in_specs=[pl.BlockSpec((None, tr, cols), lambda k, i, pr: (other(k, pr), pr[0] * nr + i, 0)),
                  pl.BlockSpec((None, tr, cols), lambda k, i, pr: (other(k, pr), i, 0))],
        out_specs=pl.BlockSpec((None, tr, cols), lambda k, i, pr: (other(k, pr), i, 0)),
    )
    return pl.pallas_call(
        body, name=name, grid_spec=grid_spec, out_shape=jax.ShapeDtypeStruct((N_CHIPS, r2, cols), BF16),
        compiler_params=_params(("parallel", "parallel"), VMEM_BIG),
    )(place, grad, theirs)


def _chip_sum(grad, theirs, got, place, buf, layer, depth, *, name):
    _, r, cols = grad.shape
    r2 = r // 2
    tr = _row_tile(r2, cols)
    nr = r2 // tr

    def body(place_ref, g_ref, t_ref, got_ref, *rest):
        own = g_ref[...].astype(F32) + t_ref[...].astype(F32)
        rest[-1][...] = ((own + got_ref[0].astype(F32)) + got_ref[1].astype(F32)) + got_ref[2].astype(F32)

    in_specs = [pl.BlockSpec((None, tr, cols), lambda i, pr: (pr[1], pr[0] * nr + i, 0)),
                pl.BlockSpec((None, tr, cols), lambda i, pr: (pr[1], i, 0)),
                pl.BlockSpec((3, tr, cols), lambda i, pr: (0, i, 0))]
    args = [place, grad, theirs, got]
    if buf is not None:
        in_specs.append(ANY)
        args.append(buf)
    grid_spec = pltpu.PrefetchScalarGridSpec(
        num_scalar_prefetch=1, grid=(nr,), in_specs=in_specs,
        out_specs=pl.BlockSpec((None, None, tr, cols), lambda i, pr: (layer, pr[0], i, 0)),
    )
    return pl.pallas_call(
        body, name=name, grid_spec=grid_spec, out_shape=jax.ShapeDtypeStruct((depth, 2, r2, cols), F32),
        input_output_aliases={} if buf is None else {4: 0}, compiler_params=_params(("parallel",), VMEM_BIG),
    )(*args)


def _sum_devices(parts, own, place, *, name):
    _, rows, cols = parts.shape
    tr = rows if N_DEV * rows * cols * 4 <= (16 << 20) else _pick(rows, (256, 128, 64, 32, 16, 8))

    def body(place_ref, p_ref, own_ref, out_ref):
        me = 2 * place_ref[1] + place_ref[0]
        acc = None
        for dev in range(N_DEV):
            term = jnp.where(me == dev, own_ref[...], p_ref[dev])
            acc = term if acc is None else acc + term
        out_ref[...] = acc

    grid_spec = pltpu.PrefetchScalarGridSpec(
        num_scalar_prefetch=1, grid=(rows // tr,),
        in_specs=[pl.BlockSpec((N_DEV, tr, cols), lambda i, pr: (0, i, 0)), pl.BlockSpec((tr, cols), lambda i, pr: (i, 0))],
        out_specs=pl.BlockSpec((tr, cols), lambda i, pr: (i, 0)),
    )
    return pl.pallas_call(
        body, name=name, grid_spec=grid_spec, out_shape=jax.ShapeDtypeStruct((rows, cols), F32),
        compiler_params=_params(("parallel",), VMEM_BIG),
    )(place, parts, own)


def _pack(parts):
    rows = []
    for t in parts:
        flat = t.reshape(-1, 128)
        pad = (-flat.shape[0]) % 8
        rows.append(jnp.pad(flat, ((0, pad), (0, 0))) if pad else flat)
    return jnp.concatenate(rows, axis=0)


def _unpack(pack, shapes):
    out, r0 = [], 0
    for shp in shapes:
        n = math.prod(shp) // 128
        out.append(pack[r0:r0 + n].reshape(shp))
        r0 += n + (-n) % 8
    return out


SMALL = ["attn_norm_g", "q_norm_g", "k_norm_g", "sgu_norm_g", "sgu_w", "sgu_b", "out_norm_a_g", "out_norm_b_g",
         "ffn_norm_g", "conv_b"]
BIG = ["w_in", "w_out", "w_up", "w_down"]
ORDER = ["attn_norm_g", "w_in", "q_norm_g", "k_norm_g", "sgu_norm_g", "sgu_w", "sgu_b", "out_norm_a_g", "out_norm_b_g",
         "w_out", "ffn_norm_g", "w_up", "conv_w", "conv_b", "w_down"]


def kernel(x, attn_norm_g, w_in, q_norm_g, k_norm_g, sgu_norm_g, sgu_w, sgu_b, out_norm_a_g, out_norm_b_g, w_out, ffn_norm_g, w_up, conv_w, conv_b, w_down, loss_target, m_attn_norm_g, m_w_in, m_q_norm_g, m_k_norm_g, m_sgu_norm_g, m_sgu_w, m_sgu_b, m_out_norm_a_g, m_out_norm_b_g, m_w_out, m_ffn_norm_g, m_w_up, m_conv_w, m_conv_b, m_w_down, v_attn_norm_g, v_w_in, v_q_norm_g, v_k_norm_g, v_sgu_norm_g, v_sgu_w, v_sgu_b, v_out_norm_a_g, v_out_norm_b_g, v_w_out, v_ffn_norm_g, v_w_up, v_conv_w, v_conv_b, v_w_down):
    W = dict(attn_norm_g=attn_norm_g, w_in=w_in, q_norm_g=q_norm_g, k_norm_g=k_norm_g, sgu_norm_g=sgu_norm_g, sgu_w=sgu_w,
             sgu_b=sgu_b, out_norm_a_g=out_norm_a_g, out_norm_b_g=out_norm_b_g, w_out=w_out, ffn_norm_g=ffn_norm_g, w_up=w_up,
             conv_w=conv_w, conv_b=conv_b, w_down=w_down)
    M = dict(attn_norm_g=m_attn_norm_g, w_in=m_w_in, q_norm_g=m_q_norm_g, k_norm_g=m_k_norm_g, sgu_norm_g=m_sgu_norm_g,
             sgu_w=m_sgu_w, sgu_b=m_sgu_b, out_norm_a_g=m_out_norm_a_g, out_norm_b_g=m_out_norm_b_g, w_out=m_w_out,
             ffn_norm_g=m_ffn_norm_g, w_up=m_w_up, conv_w=m_conv_w, conv_b=m_conv_b, w_down=m_w_down)
    V = dict(attn_norm_g=v_attn_norm_g, w_in=v_w_in, q_norm_g=v_q_norm_g, k_norm_g=v_k_norm_g, sgu_norm_g=v_sgu_norm_g,
             sgu_w=v_sgu_w, sgu_b=v_sgu_b, out_norm_a_g=v_out_norm_a_g, out_norm_b_g=v_out_norm_b_g, w_out=v_w_out,
             ffn_norm_g=v_ffn_norm_g, w_up=v_w_up, conv_w=v_conv_w, conv_b=v_conv_b, w_down=v_w_down)
    depth = w_in.shape[0]
    s, d = x.shape[1], x.shape[2]
    n_heads = out_norm_a_g.shape[1]
    core = lax.axis_index("c")
    chip = 2 * lax.axis_index("x") + lax.axis_index("y")
    place = jnp.stack([core, chip]).astype(jnp.int32)
    xs = x.reshape(s, d)

    f_local = conv_w.shape[2]
    taps = lax.dynamic_update_slice(jnp.zeros((N_CHIPS, 16, f_local), F32), conv_w.reshape(1, depth * 3, f_local),
                                    (chip, 0, 0))
    order = [(l, n) for l in range(depth) for n in BIG]
    first, token = _gather_start([_cast_into(w_in, 0, place, name="cast_w_in"), taps], name="gather_start_first")
    rest, token = _gather_start([_cast_into(W[n], l, place, dep=token, name=f"cast_{n}") for l, n in order[1:]],
                                name="gather_start_rest")
    states = dict(zip(order, [first[0]] + rest))

    states["taps"] = first[1]

    def landed(key, after, tag):
        buf = _gather_wait(states[key], after, name=f"gather_wait_{tag}")
        return _split_start([buf], _hand_over_copies, 3, name=f"hand_over_{tag}")

    def whole(state, after, tag):
        return _split_wait(state, _hand_over_copies, after, name=f"hand_over_wait_{tag}")[0]

    saved, full = [], []
    cur = xs
    for l in range(depth):
        gain = attn_norm_g[l] + token[0, 0] if l == 0 else attn_norm_g[l]
        h = _rmsnorm_fwd(cur, gain, name="attn_norm")
        if l == 0:
            ho_in, _ = landed((0, "w_in"), h, "w_in")
        w_in_l = whole(ho_in, h, "w_in")
        if l == 0:
            ho_taps, token = landed("taps", w_in_l, "taps")
        p = _mm(h, w_in_l, "nn", b_split=N_CHIPS, caps=(2048, 256, 2048), dep=token, name="proj_in")
        if l == 0:
            taps = whole(ho_taps, p, "taps")[:, :depth * 3].reshape(N_CHIPS, depth, 3, f_local)
            cw_full = jnp.transpose(taps, (1, 2, 0, 3)).reshape(depth, 3, N_CHIPS * f_local)
        mix, o_raw, lsum = _attn_fwd(p, q_norm_g[l], k_norm_g[l], out_norm_a_g[l], n_heads, name="attn_fwd")
        ho_out, token = landed((l, "w_out"), mix, "w_out")
        mix = _sgu_fwd(p, mix, sgu_w[l], sgu_b[l], sgu_norm_g[l] + token[0, 0], out_norm_b_g[l], n_heads, name="sgu_fwd")
        w_out_l = whole(ho_out, mix, "w_out").reshape(-1, d)
        ho_up, token = landed((l, "w_up"), w_out_l, "w_up")
        x1 = _mm(mix, w_out_l, "nn", a_split=2, res=cur, caps=(2048, 512, 1024), dep=token, name="proj_out")
        h2 = _rmsnorm_fwd(x1, ffn_norm_g[l], name="ffn_norm")
        w_up_l = whole(ho_up, h2, "w_up")
        up_gate, up_val, act = _up_conv_fwd(h2, w_up_l, cw_full[l], conv_b[l], name="ffn_up_conv")
        ho_down, token = landed((l, "w_down"), act, "w_down")
        w_down_l = whole(ho_down, act, "w_down").reshape(-1, d)
        if l + 1 < depth:
            ho_in, token = landed((l + 1, "w_in"), w_down_l, "w_in")
        x2 = _mm(act, w_down_l, "nn", res=x1, caps=(1024, 512, 2816), dep=token, name="ffn_down")
        full.append(dict(w_in=w_in_l, w_out=w_out_l, w_up=w_up_l, w_down=w_down_l, conv_w=cw_full[l]))
        saved.append(dict(x0=cur, h=h, p=p, o_raw=o_raw, lsum=lsum, mix=mix, x1=x1, h2=h2, up_gate=up_gate, up_val=up_val,
                          act=act))
        cur = x2

    dx, dxb, sq = _loss_head(cur, loss_target.reshape(s, d), name="loss_head")
    loss = lax.psum(sq[0, 0] * (0.5 / d), ("x", "y", "c"))

    small_grads = {n: [None] * depth for n in SMALL + ["conv_w"]}
    def pair_begin(n, grad):
        land = lax.empty((N_CHIPS, grad.shape[1] // 2, grad.shape[2]), grad.dtype)
        return _split_start([grad, land], _pair_copies, 1, name=f"pair_start_{n}")

    def chip_begin(n, state, after):
        grad, theirs = _split_wait(state, _pair_copies, after, name=f"pair_wait_{n}")
        state, tok = _chip_start(_pair_sum(grad, theirs, place, name=f"pair_sum_{n}"), name=f"chip_start_{n}")
        return (grad, theirs, state), tok

    pending = {}
    for l in reversed(range(depth)):
        fw, sv = full[l], saved[l]
        g_down = _mm(sv["act"], dxb, "tn", caps=(512, 2048, 2048), out_dtype=BF16, name="g_down")
        pair, tok = pair_begin("w_down", g_down.reshape(N_CHIPS, -1, d))
        dup, dcw, dcb = _conv_bwd(sv["up_gate"], sv["up_val"], dxb, fw["w_down"], fw["conv_w"], conv_b[l] + tok[0, 0],
                                  name="conv_bwd")
        pending[(l, "w_down")], tok = chip_begin("w_down", pair, dup)
        g_up = _mm(sv["h2"], dup, "tn", b_split=2, o_split=N_CHIPS, caps=(2048, 256, 2048), out_dtype=BF16, dep=tok,
                   name="g_up")
        pair, tok = pair_begin("w_up", g_up)
        dh2 = _mm(dup, fw["w_up"], "nt", a_split=2, b_split=N_CHIPS, caps=(2048, 512, 2816), dep=tok, name="d_h2")
        pending[(l, "w_up")], tok = chip_begin("w_up", pair, dh2)
        dx1, dx1b, dg_ffn = _rmsnorm_bwd(sv["x1"], ffn_norm_g[l] + tok[0, 0], dh2, dx, name="ffn_norm_bwd")
        dmix = _mm(dx1b, fw["w_out"], "nt", caps=(2048, 512, 2048), name="d_mix")
        g_out = _mm(sv["mix"], dx1b, "tn", a_split=2, caps=(512, 2048, 2048), out_dtype=BF16, name="g_out")
        pair, tok = pair_begin("w_out", g_out.reshape(N_CHIPS, -1, d))
        dqkv, dgq, dgk, dgoa = _attn_bwd(sv["p"], sv["o_raw"], sv["lsum"], dmix, q_norm_g[l] + tok[0, 0], k_norm_g[l],
                                         out_norm_a_g[l], n_heads, name="attn_bwd")
        pending[(l, "w_out")], tok = chip_begin("w_out", pair, dqkv)
        duv, dsw, dsb, dgv, dgob = _sgu_bwd(sv["p"], dmix, sgu_w[l], sgu_b[l], sgu_norm_g[l] + tok[0, 0], out_norm_b_g[l],
                                            n_heads, name="sgu_bwd")
        dp = jnp.concatenate([dqkv[0], dqkv[1], dqkv[2], duv[0], duv[1]], axis=1)
        g_in = _mm(sv["h"], dp, "tn", o_split=N_CHIPS, caps=(2048, 256, 2048), out_dtype=BF16, name="g_in")
        pair, tok = pair_begin("w_in", g_in)
        dh = _mm(dp, fw["w_in"], "nt", b_split=N_CHIPS, caps=(2048, 512, 1280), dep=tok, name="d_h")
        pending[(l, "w_in")], tok = chip_begin("w_in", pair, dh)
        dx, dxb, dg_attn = _rmsnorm_bwd(sv["x0"], attn_norm_g[l] + tok[0, 0], dh, dx1, name="attn_norm_bwd")

        small_grads["attn_norm_g"][l] = dg_attn.reshape(d)
        small_grads["q_norm_g"][l] = jnp.sum(dgq, axis=(0, 1))
        small_grads["k_norm_g"][l] = jnp.sum(dgk, axis=(0, 1))
        small_grads["sgu_norm_g"][l] = dgv.reshape(-1, BLK)
        small_grads["sgu_w"][l] = dsw
        small_grads["sgu_b"][l] = dsb.reshape(-1, BLK)
        small_grads["out_norm_a_g"][l] = dgoa.reshape(-1, BLK)
        small_grads["out_norm_b_g"][l] = dgob.reshape(-1, BLK)
        small_grads["ffn_norm_g"][l] = dg_ffn.reshape(d)
        small_grads["conv_b"][l] = dcb[:, 0, :].reshape(-1)
        small_grads["conv_w"][l] = jnp.transpose(dcw[:, :3, :], (1, 0, 2)).reshape(3, -1)

    names = SMALL + ["conv_w"]
    pack = _pack([jnp.stack(small_grads[n]) for n in names])
    small, tok = _split_start([pack, lax.empty((N_DEV,) + pack.shape, F32)], _small_copies, N_DEV - 1, name="small_start")
    G, D_, NM, NV = {}, {}, {}, {}
    after, prev = tok, None
    for n in ("w_down", "w_up", "w_out", "w_in"):
        buf = None
        for l in reversed(range(depth)):
            grad, theirs, state = pending[(l, n)]
            got = _chip_wait(state, after, name=f"chip_wait_{n}")
            buf = _chip_sum(grad, theirs, got, place, buf, l, depth, name=f"chip_sum_{n}")
            after = buf
        share, tok = _split_start([buf], _share_copies, 1, name=f"share_start_{n}")
        if prev is not None:
            D_[prev], NM[prev], NV[prev], G[prev] = _adamw(W[prev], G[prev], M[prev], V[prev], dep=tok, copy_g=True,
                                                           name=f"adamw_{prev}")
            after = NV[prev]
        G[n] = _split_wait(share, _share_copies, after, name=f"share_wait_{n}")[0].reshape(W[n].shape)
        after, prev = G[n], n
    D_[prev], NM[prev], NV[prev], G[prev] = _adamw(W[prev], G[prev], M[prev], V[prev], copy_g=True, name=f"adamw_{prev}")

    pack, parts = _split_wait(small, _small_copies, NV[prev], name="small_wait")
    total = _sum_devices(parts, pack, place, name="sum_small")
    f_full = conv_b.shape[1]
    shapes = [W[n].shape for n in SMALL] + [(depth, 3, f_full)]
    for n, t in zip(names, _unpack(total, shapes)):
        G[n] = t
    G["conv_w"] = lax.dynamic_slice_in_dim(G["conv_w"], chip * f_local, f_local, axis=2)

    D_["conv_w"], NM["conv_w"], NV["conv_w"] = _adamw(conv_w, G["conv_w"], m_conv_w, v_conv_w, name="adamw_conv_w")
    small_shapes = [W[n].shape for n in SMALL]
    res = _adamw(_pack([W[n] for n in SMALL]), _pack([G[n] for n in SMALL]), _pack([M[n] for n in SMALL]),
                 _pack([V[n] for n in SMALL]), name="adamw_small")
    for dst, t in zip((D_, NM, NV), res):
        for n, u in zip(SMALL, _unpack(t, small_shapes)):
            dst[n] = u

    return (loss, dx.reshape(x.shape), *[G[n] for n in ORDER], *[D_[n] for n in ORDER], *[NM[n] for n in ORDER],
            *[NV[n] for n in ORDER])
```

```python
import functools
import math

import jax
import jax.numpy as jnp
from jax import lax
from jax.experimental import pallas as pl
from jax.experimental.pallas import tpu as pltpu

F32 = jnp.float32
BF16 = jnp.bfloat16
EPS = 1e-6
BLK = 128
N_CHIPS = 4
N_DEV = 8
ADAM_LR, ADAM_B1, ADAM_B2, ADAM_EPS, ADAM_WD, ADAM_STEP = 0.001, 0.9, 0.999, 1e-08, 0.01, 10
VMEM_BIG = 48 * 1024 * 1024
MESH = pl.DeviceIdType.MESH
ANY = pl.BlockSpec(memory_space=pl.ANY)


def _pick(dim, prefs):
    for t in prefs:
        if dim % t == 0:
            return t
    raise ValueError(f"no tile in {prefs} divides {dim}")


def _params(sem=None, vmem=None):
    return pltpu.CompilerParams(dimension_semantics=sem, vmem_limit_bytes=vmem)


def _ldims(arr, split):
    if split == 1:
        return arr.shape
    p, r, cs = arr.shape
    assert p == split
    return (r, p * cs)


def _spec(tr, tc, split, cols, rc):
    if split == 1:
        return pl.BlockSpec((tr, tc), lambda i, j, k: rc(i, j, k))
    per = (cols // split) // tc

    def imap(i, j, k):
        r, c = rc(i, j, k)
        return (c // per, r, c % per)

    return pl.BlockSpec((None, tr, tc), imap)


def _fit(unit, cap):
    return max(t for t in range(128, min(unit, cap) + 1, 128) if unit % t == 0)


def _mm(a, b, mode, *, name, caps, a_split=1, b_split=1, o_split=1, out_dtype=F32, res=None, dep=None):
    ar, ac = _ldims(a, a_split)
    br, bc = _ldims(b, b_split)
    if mode == "nn":
        m, k, n = ar, ac, bc
        assert br == k
        ku, nu, mu = math.gcd(k // a_split, k), math.gcd(n // b_split, n // o_split), m
    elif mode == "nt":
        m, k, n = ar, ac, br
        assert bc == k
        ku, nu, mu = math.gcd(k // a_split, k // b_split), n // o_split, m
    else:
        k, m, n = ar, ac, bc
        assert br == k
        ku, nu, mu = k, math.gcd(n // b_split, n // o_split), m // a_split
    tm, tn, tk = _fit(mu, caps[0]), _fit(nu, caps[1]), _fit(ku, caps[2])
    nk = k // tk
    if mode == "nn":
        a_spec = _spec(tm, tk, a_split, k, lambda i, j, kk: (i, kk))
        b_spec = _spec(tk, tn, b_split, n, lambda i, j, kk: (kk, j))
    elif mode == "nt":
        a_spec = _spec(tm, tk, a_split, k, lambda i, j, kk: (i, kk))
        b_spec = _spec(tn, tk, b_split, k, lambda i, j, kk: (j, kk))
    else:
        a_spec = _spec(tk, tm, a_split, m, lambda i, j, kk: (kk, i))
        b_spec = _spec(tk, tn, b_split, n, lambda i, j, kk: (kk, j))
    o_spec = _spec(tm, tn, o_split, n, lambda i, j, kk: (i, j))
    dims = {"nn": (((1,), (0,)), ((), ())), "nt": (((1,), (1,)), ((), ())), "tn": (((0,), (0,)), ((), ()))}[mode]

    def body(a_ref, b_ref, *rest):
        if dep is not None:
            rest = rest[1:]
        if res is None:
            o_ref, acc = rest
        else:
            r_ref, o_ref, acc = rest
        kk = pl.program_id(2)

        @pl.when(kk == 0)
        def _():
            acc[...] = jnp.zeros_like(acc)

        acc[...] += lax.dot_general(a_ref[...].astype(BF16), b_ref[...].astype(BF16), dims, preferred_element_type=F32)

        @pl.when(kk == nk - 1)
        def _():
            out = acc[...]
            if res is not None:
                out = out + r_ref[...]
            o_ref[...] = out.astype(o_ref.dtype)

    in_specs, args = [a_spec, b_spec], [a, b]
    if dep is not None:
        in_specs.append(ANY)
        args.append(dep)
    if res is not None:
        in_specs.append(pl.BlockSpec((tm, tn), lambda i, j, kk: (i, j)))
        args.append(res)
    out_shape = (m, n) if o_split == 1 else (o_split, m, n // o_split)
    return pl.pallas_call(
        body, name=name, grid=(m // tm, n // tn, nk), in_specs=in_specs, out_specs=o_spec,
        out_shape=jax.ShapeDtypeStruct(out_shape, out_dtype), scratch_shapes=[pltpu.VMEM((tm, tn), F32)],
        compiler_params=_params(("parallel", "parallel", "arbitrary"), VMEM_BIG),
    )(*args)


def _rstd(v):
    return lax.rsqrt(jnp.mean(v * v, axis=-1, keepdims=True) + EPS)


def _norm_bwd(v, r, gain, dout):
    a = dout * gain
    dv = r * (a - v * (r * r * jnp.mean(a * v, axis=-1, keepdims=True)))
    return dv, dout * v * r


def _rmsnorm_fwd(x, g, layer, *, name):
    s, d = x.shape
    tr = _pick(s, (256, 128))

    def body(x_ref, g_ref, o_ref):
        v = x_ref[...]
        o_ref[...] = (v * _rstd(v) * g_ref[...]).astype(o_ref.dtype)

    return pl.pallas_call(
        body, name=name, grid=(s // tr,),
        in_specs=[pl.BlockSpec((tr, d), lambda i: (i, 0)), pl.BlockSpec((None, 1, d), lambda i: (layer, 0, 0))],
        out_specs=pl.BlockSpec((tr, d), lambda i: (i, 0)), out_shape=jax.ShapeDtypeStruct((s, d), BF16),
        compiler_params=_params(("parallel",)),
    )(x, g.reshape(-1, 1, d))


def _rmsnorm_bwd(x, g, layer, dh, dres, *, name):
    s, d = x.shape
    tr = _pick(s, (256, 128))

    def body(x_ref, g_ref, dh_ref, dres_ref, dx_ref, dxb_ref, dg_ref):
        v = x_ref[...]
        dv, dgr = _norm_bwd(v, _rstd(v), g_ref[...], dh_ref[...])
        dx = dres_ref[...] + dv
        dx_ref[...] = dx
        dxb_ref[...] = dx.astype(BF16)
        part = jnp.sum(dgr, axis=0, keepdims=True)

        @pl.when(pl.program_id(0) == 0)
        def _():
            dg_ref[...] = part

        @pl.when(pl.program_id(0) > 0)
        def _():
            dg_ref[...] += part

    row = pl.BlockSpec((tr, d), lambda i: (i, 0))
    one = pl.BlockSpec((1, d), lambda i: (0, 0))
    return pl.pallas_call(
        body, name=name, grid=(s // tr,), in_specs=[row, pl.BlockSpec((None, 1, d), lambda i: (layer, 0, 0)), row, row],
        out_specs=[row, row, one],
        out_shape=[jax.ShapeDtypeStruct((s, d), F32), jax.ShapeDtypeStruct((s, d), BF16), jax.ShapeDtypeStruct((1, d), F32)],
        compiler_params=_params(("arbitrary",)),
    )(x, g.reshape(-1, 1, d), dh, dres)


def _loss_head(y, target, *, name):
    s, d = y.shape
    tr = _pick(s, (256, 128))

    def body(y_ref, t_ref, dy_ref, dyb_ref, ls_ref):
        e = y_ref[...] - t_ref[...]
        dy = e * (1.0 / d)
        dy_ref[...] = dy
        dyb_ref[...] = dy.astype(BF16)
        part = jnp.full(ls_ref.shape, jnp.sum(e * e), F32)

        @pl.when(pl.program_id(0) == 0)
        def _():
            ls_ref[...] = part

        @pl.when(pl.program_id(0) > 0)
        def _():
            ls_ref[...] += part

    row = pl.BlockSpec((tr, d), lambda i: (i, 0))
    return pl.pallas_call(
        body, name=name, grid=(s // tr,), in_specs=[row, row], out_specs=[row, row, pl.BlockSpec((8, 128), lambda i: (0, 0))],
        out_shape=[jax.ShapeDtypeStruct((s, d), F32), jax.ShapeDtypeStruct((s, d), BF16), jax.ShapeDtypeStruct((8, 128), F32)],
        compiler_params=_params(("arbitrary",)),
    )(y, target)


def _iota2(axis):
    return lax.broadcasted_iota(jnp.int32, (BLK, BLK), axis)


def _tri_sum(v, tri):
    hi = v.astype(BF16)
    lo = (v - hi.astype(F32)).astype(BF16)
    return jnp.dot(hi, tri, preferred_element_type=F32) + jnp.dot(lo, tri, preferred_element_type=F32)


def _dot_nt(a, b):
    return lax.dot_general(a, b, (((1,), (1,)), ((), ())), preferred_element_type=F32)


def _dot_tn(a, b):
    return lax.dot_general(a, b, (((0,), (0,)), ((), ())), preferred_element_type=F32)


TQ_MAX = 1024
TQ_MAX_BWD = 1024
HP = 2
VMEM_ATTN_BWD = 56 * 1024 * 1024


def _lanes(hh):
    return slice(hh * BLK, (hh + 1) * BLK)


def _causal(n, diag):
    if not diag:
        return None
    return lax.broadcasted_iota(jnp.int32, (n, BLK), 1) < lax.broadcasted_iota(jnp.int32, (n, BLK), 0)


def _sb_sums(z, mask, rhs_gt):
    lb = jnp.minimum(z, 0.0) - jnp.log(1.0 + jnp.exp(-jnp.abs(z)))
    l1m = lb - z
    if mask is not None:
        l1m = jnp.where(mask, l1m, 0.0)
    return lb, _tri_sum(l1m, rhs_gt)


def _below(old, new, r0):
    return new if r0 == 0 else jnp.concatenate([old[:r0], new], axis=0)


def _attn_fwd(p, gq, gk, go, layer, n_heads, *, name):
    s = p.shape[0]
    tq = min(TQ_MAX, s)
    per = tq // BLK
    scale = BLK ** -0.5

    def body(q_ref, k_ref, v_ref, gq_ref, gk_ref, go_ref, att_ref, o_ref, l_ref, qn, kn, vb):
        for hh in range(HP):
            q = q_ref[:, _lanes(hh)]
            k = k_ref[:, _lanes(hh)]
            qn[:, _lanes(hh)] = (q * _rstd(q) * gq_ref[...] * scale).astype(BF16)
            kn[:, _lanes(hh)] = (k * _rstd(k) * gk_ref[...]).astype(BF16)
            vb[:, _lanes(hh)] = v_ref[:, _lanes(hh)].astype(BF16)
        rhs_gt = jnp.concatenate([(_iota2(0) > _iota2(1)).astype(BF16), jnp.ones((BLK, BLK), BF16)], axis=1)

        def step(q0, j, r0, diag, states):
            n = tq - r0
            rows = pl.ds(pl.multiple_of(q0 + r0, BLK), n)
            cols = pl.ds(pl.multiple_of(j * BLK, BLK), BLK)
            mask = _causal(n, diag)
            zs = [_dot_nt(qn[rows, _lanes(hh)], kn[cols, _lanes(hh)]) for hh in range(HP)]
            sums = [_sb_sums(z, mask, rhs_gt) for z in zs]
            new = []
            for hh in range(HP):
                acc, later = states[hh]
                lb, both = sums[hh]
                a = jnp.exp(lb + both[:, :BLK] + later[r0:])
                if diag:
                    a = jnp.where(mask, a, 0.0)
                acc_new = acc[r0:] + jnp.dot(a.astype(BF16), vb[cols, _lanes(hh)], preferred_element_type=F32)
                new.append((_below(acc, acc_new, r0), _below(later, later[r0:] + both[:, BLK:], r0)))
            return tuple(new)

        def q_block(i, _):
            q0 = i * tq
            zero = jnp.zeros((tq, BLK), F32)
            states = ((zero, zero),) * HP
            for jd in reversed(range(per)):
                states = step(q0, i * per + jd, jd * BLK, True, states)
            states = lax.fori_loop(0, i * per, lambda jj, st: step(q0, i * per - 1 - jj, 0, False, st), states)
            tile = pl.ds(pl.multiple_of(q0, tq), tq)
            for hh in range(HP):
                o, total = states[hh]
                o_ref[tile, _lanes(hh)] = o
                l_ref[hh, tile, :] = total
                att_ref[tile, _lanes(hh)] = (o * _rstd(o) * go_ref[hh]).astype(att_ref.dtype)
            return 0

        lax.fori_loop(0, s // tq, q_block, 0)

    assert n_heads % HP == 0
    groups = n_heads // HP

    def col(part):
        return pl.BlockSpec((s, HP * BLK), lambda g: (0, part * groups + g))

    gain = pl.BlockSpec((None, 1, BLK), lambda g: (layer, 0, 0))
    per_head = pl.BlockSpec((None, HP, 1, BLK), lambda g: (layer, g, 0, 0))
    return pl.pallas_call(
        body, name=name, grid=(groups,),
        in_specs=[col(0), col(1), col(2), gain, gain, per_head],
        out_specs=[pl.BlockSpec((None, s, HP * BLK), lambda g: (0, 0, g)), col(0), pl.BlockSpec((HP, s, BLK), lambda g: (g, 0, 0))],
        out_shape=[jax.ShapeDtypeStruct((2, s, n_heads * BLK), BF16), jax.ShapeDtypeStruct((s, n_heads * BLK), F32),
                   jax.ShapeDtypeStruct((n_heads, s, BLK), F32)],
        scratch_shapes=[pltpu.VMEM((s, HP * BLK), BF16)] * 3,
        compiler_params=_params(("parallel",), VMEM_BIG),
    )(p, p, p, gq.reshape(-1, 1, BLK), gk.reshape(-1, 1, BLK), go.reshape(-1, n_heads, 1, BLK))


def _attn_bwd(p, o_raw, lsum, dmix, gq, gk, go, layer, n_heads, *, name):
    s = p.shape[0]
    tq = min(TQ_MAX_BWD, s)
    per = tq // BLK
    scale = BLK ** -0.5

    def body(q_ref, k_ref, v_ref, o_ref, l_ref, da_ref, gq_ref, gk_ref, go_ref,
             dqkv_ref, dgq_ref, dgk_ref, dgo_ref, qn, kn, vb, dob, dqn, dkn, dvv):
        for hh in range(HP):
            q = q_ref[:, _lanes(hh)]
            k = k_ref[:, _lanes(hh)]
            qn[:, _lanes(hh)] = (q * _rstd(q) * gq_ref[...] * scale).astype(BF16)
            kn[:, _lanes(hh)] = (k * _rstd(k) * gk_ref[...]).astype(BF16)
            vb[:, _lanes(hh)] = v_ref[:, _lanes(hh)].astype(BF16)
            o = o_ref[:, _lanes(hh)]
            do, dgo_rows = _norm_bwd(o, _rstd(o), go_ref[hh], da_ref[:, _lanes(hh)])
            dob[:, _lanes(hh)] = do.astype(BF16)
            dgo_ref[hh] = jnp.sum(dgo_rows, axis=0, keepdims=True)
        dkn[...] = jnp.zeros_like(dkn)
        dvv[...] = jnp.zeros_like(dvv)
        ones = jnp.ones((BLK, BLK), BF16)
        rhs_gt = jnp.concatenate([(_iota2(0) > _iota2(1)).astype(BF16), ones], axis=1)
        rhs_lt = jnp.concatenate([(_iota2(0) < _iota2(1)).astype(BF16), ones], axis=1)

        def step(q0, j, r0, diag, states):
            n = tq - r0
            rows = pl.ds(pl.multiple_of(q0 + r0, BLK), n)
            cols = pl.ds(pl.multiple_of(j * BLK, BLK), BLK)
            mask = _causal(n, diag)
            zs = [_dot_nt(qn[rows, _lanes(hh)], kn[cols, _lanes(hh)]) for hh in range(HP)]
            das = [_dot_nt(dob[rows, _lanes(hh)], vb[cols, _lanes(hh)]) for hh in range(HP)]
            sums = [_sb_sums(z, mask, rhs_gt) for z in zs]
            mids = []
            for hh in range(HP):
                lb, both = sums[hh]
                upto = states[hh][0][r0:] + both[:, BLK:]
                a = jnp.exp(lb + both[:, :BLK] + (l_ref[hh, rows, :] - upto))
                if diag:
                    a = jnp.where(mask, a, 0.0)
                g = das[hh] * a
                mids.append((lb, upto, a, g, _tri_sum(g, rhs_lt)))
            new = []
            for hh in range(HP):
                seen, gsum, dq = states[hh]
                lb, upto, a, g, bothg = mids[hh]
                beta = jnp.exp(lb)
                dz = g * (1.0 - beta) - beta * (bothg[:, :BLK] + gsum[r0:])
                if diag:
                    dz = jnp.where(mask, dz, 0.0)
                dzs = dz.astype(BF16)
                dq_new = dq[r0:] + jnp.dot(dzs, kn[cols, _lanes(hh)], preferred_element_type=F32)
                dkn[cols, _lanes(hh)] += _dot_tn(dzs, qn[rows, _lanes(hh)])
                dvv[cols, _lanes(hh)] += _dot_tn(a.astype(BF16), dob[rows, _lanes(hh)])
                new.append((_below(seen, upto, r0), _below(gsum, gsum[r0:] + bothg[:, BLK:], r0), _below(dq, dq_new, r0)))
            return tuple(new)

        def q_block(i, _):
            q0 = i * tq
            zero = jnp.zeros((tq, BLK), F32)
            states = ((zero, zero, zero),) * HP
            states = lax.fori_loop(0, i * per, lambda j, st: step(q0, j, 0, False, st), states)
            for jd in range(per):
                states = step(q0, i * per + jd, jd * BLK, True, states)
            tile = pl.ds(pl.multiple_of(q0, tq), tq)
            for hh in range(HP):
                dqn[tile, _lanes(hh)] = states[hh][2]
            return 0

        lax.fori_loop(0, s // tq, q_block, 0)
        for hh in range(HP):
            q = q_ref[:, _lanes(hh)]
            k = k_ref[:, _lanes(hh)]
            dq_raw, dgq_rows = _norm_bwd(q, _rstd(q), gq_ref[...], dqn[:, _lanes(hh)] * scale)
            dk_raw, dgk_rows = _norm_bwd(k, _rstd(k), gk_ref[...], dkn[:, _lanes(hh)])
            dqkv_ref[0, :, _lanes(hh)] = dq_raw.astype(BF16)
            dqkv_ref[1, :, _lanes(hh)] = dk_raw.astype(BF16)
            dqkv_ref[2, :, _lanes(hh)] = dvv[:, _lanes(hh)].astype(BF16)
            dgq_ref[hh] = jnp.sum(dgq_rows, axis=0, keepdims=True)
            dgk_ref[hh] = jnp.sum(dgk_rows, axis=0, keepdims=True)

    assert n_heads % HP == 0
    groups = n_heads // HP

    once = pl.Buffered(1)

    def col(part):
        return pl.BlockSpec((s, HP * BLK), lambda g: (0, part * groups + g), pipeline_mode=once)

    gain = pl.BlockSpec((None, 1, BLK), lambda g: (layer, 0, 0))
    per_head = pl.BlockSpec((HP, 1, BLK), lambda g: (g, 0, 0))
    head_gain = jax.ShapeDtypeStruct((n_heads, 1, BLK), F32)
    return pl.pallas_call(
        body, name=name, grid=(groups,),
        in_specs=[col(0), col(1), col(2), col(0), pl.BlockSpec((HP, s, BLK), lambda g: (g, 0, 0), pipeline_mode=once), col(0),
                  gain, gain, pl.BlockSpec((None, HP, 1, BLK), lambda g: (layer, g, 0, 0))],
        out_specs=[pl.BlockSpec((3, s, HP * BLK), lambda g: (0, 0, g)), per_head, per_head, per_head],
        out_shape=[jax.ShapeDtypeStruct((3, s, n_heads * BLK), BF16), head_gain, head_gain, head_gain],
        scratch_shapes=[pltpu.VMEM((s, HP * BLK), BF16)] * 4 + [pltpu.VMEM((s, HP * BLK), F32)] * 3,
        compiler_params=_params(("parallel",), VMEM_ATTN_BWD),
    )(p, p, p, o_raw, lsum, dmix, gq.reshape(-1, 1, BLK), gk.reshape(-1, 1, BLK), go.reshape(-1, n_heads, 1, BLK))


SGU_TOGETHER = 4
_INV_SQRT2 = 0.7071067811865476
_INV_SQRT2PI = 0.3989422804014327


def _gelu(x):
    return 0.5 * x * (1.0 + lax.erf(x * _INV_SQRT2))


def _gelu_and_grad(x):
    cdf = 0.5 * (1.0 + lax.erf(x * _INV_SQRT2))
    return x * cdf, cdf + x * jnp.exp(-0.5 * x * x) * _INV_SQRT2PI


def _sgu_fwd(p, mix, w, b, gv, gout, layer, n_heads, *, name):
    s = p.shape[0]
    n_groups = w.shape[1]
    nb = s // BLK
    assert mix.shape == (2, s, n_groups * BLK)

    def body(u_ref, v_ref, w_ref, b_ref, gv_ref, go_ref, _mix_ref, out_ref):
        wt = jnp.where(_iota2(0) >= _iota2(1), w_ref[...], 0.0).astype(BF16)
        bias = b_ref[...]

        def chunks(i, _):
            rows = [pl.ds(pl.multiple_of((i * SGU_TOGETHER + k) * BLK, BLK), BLK) for k in range(SGU_TOGETHER)]
            us = [_gelu(u_ref[r, :]) for r in rows]
            vss = []
            for r in rows:
                vv = _gelu(v_ref[r, :])
                vss.append((vv * _rstd(vv) * gv_ref[...]).astype(BF16))
            mixed = [jnp.dot(wt, vs, preferred_element_type=F32) + bias for vs in vss]
            for r, u, m in zip(rows, us, mixed):
                gated = u * m
                out_ref[r, :] = (gated * _rstd(gated) * go_ref[...]).astype(out_ref.dtype)
            return 0

        assert nb % SGU_TOGETHER == 0
        lax.fori_loop(0, nb // SGU_TOGETHER, chunks, 0)

    def col(off):
        return pl.BlockSpec((s, BLK), lambda g: (0, off + g))

    per_group = pl.BlockSpec((None, None, 1, BLK), lambda g: (layer, g, 0, 0))
    return pl.pallas_call(
        body, name=name, grid=(n_groups,),
        in_specs=[col(3 * n_heads), col(3 * n_heads + n_groups), pl.BlockSpec((None, None, BLK, BLK), lambda g: (layer, g, 0, 0)),
                  pl.BlockSpec((None, None, BLK, 1), lambda g: (layer, g, 0, 0)), per_group, per_group, ANY],
        out_specs=pl.BlockSpec((None, s, BLK), lambda g: (1, 0, g)), out_shape=jax.ShapeDtypeStruct(mix.shape, mix.dtype),
        input_output_aliases={6: 0}, compiler_params=_params(("parallel",), VMEM_BIG),
    )(p, p, w, b.reshape(-1, n_groups, BLK, 1), gv.reshape(-1, n_groups, 1, BLK), gout.reshape(-1, n_groups, 1, BLK), mix)


def _sgu_bwd(p, dmix, w, b, gv, gout, layer, n_heads, *, name):
    s = p.shape[0]
    n_groups = w.shape[1]
    nb = s // BLK

    def body(u_ref, v_ref, ds_ref, w_ref, b_ref, gv_ref, go_ref, duv_ref, dw_ref, db_ref, dgv_ref, dgo_ref):
        lower = _iota2(0) >= _iota2(1)
        wt = jnp.where(lower, w_ref[...], 0.0).astype(BF16)
        bias = b_ref[...]

        def chunks(i, carry):
            dw, db, dgv, dgo = carry
            rows = [pl.ds(pl.multiple_of((i * SGU_TOGETHER + k) * BLK, BLK), BLK) for k in range(SGU_TOGETHER)]
            pre = []
            for r in rows:
                u, u_grad = _gelu_and_grad(u_ref[r, :])
                vv, vv_grad = _gelu_and_grad(v_ref[r, :])
                rv = _rstd(vv)
                pre.append((u, u_grad, vv, vv_grad, rv, (vv * rv * gv_ref[...]).astype(BF16)))
            mixed = [jnp.dot(wt, t[5], preferred_element_type=F32) + bias for t in pre]
            mid = []
            for r, t, m in zip(rows, pre, mixed):
                gated = t[0] * m
                dgated, dgo_rows = _norm_bwd(gated, _rstd(gated), go_ref[...], ds_ref[r, :])
                dmixed = dgated * t[0]
                duv_ref[0, r, :] = (dgated * m * t[1]).astype(BF16)
                dgo = dgo + jnp.sum(dgo_rows, axis=0, keepdims=True)
                db = db + jnp.sum(dmixed, axis=1, keepdims=True)
                mid.append(dmixed.astype(BF16))
            dvss = [_dot_tn(wt, dmb) for dmb in mid]
            for dmb, t in zip(mid, pre):
                dw = dw + _dot_nt(dmb, t[5])
            for r, t, dvs in zip(rows, pre, dvss):
                dvv, dgv_rows = _norm_bwd(t[2], t[4], gv_ref[...], dvs)
                duv_ref[1, r, :] = (dvv * t[3]).astype(BF16)
                dgv = dgv + jnp.sum(dgv_rows, axis=0, keepdims=True)
            return dw, db, dgv, dgo

        assert nb % SGU_TOGETHER == 0
        row0 = jnp.zeros((1, BLK), F32)
        dw, db, dgv, dgo = lax.fori_loop(0, nb // SGU_TOGETHER, chunks,
                                         (jnp.zeros((BLK, BLK), F32), jnp.zeros((BLK, 1), F32), row0, row0))
        dw_ref[...] = jnp.where(lower, dw, 0.0)
        db_ref[...] = db
        dgv_ref[...] = dgv
        dgo_ref[...] = dgo

    def col(off):
        return pl.BlockSpec((s, BLK), lambda g: (0, off + g))

    per_group = pl.BlockSpec((None, 1, BLK), lambda g: (g, 0, 0))
    square = pl.BlockSpec((None, BLK, BLK), lambda g: (g, 0, 0))
    column = pl.BlockSpec((None, BLK, 1), lambda g: (g, 0, 0))
    gain = jax.ShapeDtypeStruct((n_groups, 1, BLK), F32)
    return pl.pallas_call(
        body, name=name, grid=(n_groups,),
        in_specs=[col(3 * n_heads), col(3 * n_heads + n_groups), col(n_heads),
                  pl.BlockSpec((None, None, BLK, BLK), lambda g: (layer, g, 0, 0)),
                  pl.BlockSpec((None, None, BLK, 1), lambda g: (layer, g, 0, 0)),
                  pl.BlockSpec((None, None, 1, BLK), lambda g: (layer, g, 0, 0)),
                  pl.BlockSpec((None, None, 1, BLK), lambda g: (layer, g, 0, 0))],
        out_specs=[pl.BlockSpec((2, s, BLK), lambda g: (0, 0, g)), square, column, per_group, per_group],
        out_shape=[jax.ShapeDtypeStruct((2, s, n_groups * BLK), BF16), jax.ShapeDtypeStruct((n_groups, BLK, BLK), F32),
                   jax.ShapeDtypeStruct((n_groups, BLK, 1), F32), gain, gain],
        compiler_params=_params(("parallel",), VMEM_BIG),
    )(p, p, dmix, w, b.reshape(-1, n_groups, BLK, 1), gv.reshape(-1, n_groups, 1, BLK), gout.reshape(-1, n_groups, 1, BLK))


CONV_ROWS = 256
HALO = 8


def _shift_down(ref, r0, n, first):
    cur = ref[pl.ds(r0, n), :]
    prev = jnp.zeros((HALO, cur.shape[1]), F32) if first else ref[pl.ds(r0 - HALO, HALO), :]
    ext = jnp.concatenate([prev, cur], axis=0)
    return pltpu.roll(ext, 1, 0)[HALO:], pltpu.roll(ext, 2, 0)[HALO:], cur


def _shift_up(ref, r0, n, last):
    cur = ref[pl.ds(r0, n), :]
    nxt = jnp.zeros((HALO, cur.shape[1]), F32) if last else ref[pl.ds(r0 + n, HALO), :]
    ext = jnp.concatenate([cur, nxt], axis=0)
    return cur, pltpu.roll(ext, n + HALO - 1, 0)[:n], pltpu.roll(ext, n + HALO - 2, 0)[:n]


def _conv_rows(x1, x2, x0, w_ref, b_ref):
    return ((b_ref[...] + x2 * w_ref[0:1, :]) + x1 * w_ref[1:2, :]) + x0 * w_ref[2:3, :]


def _conv_specs(s, f, tc, layer):
    nf = f // tc
    gate = pl.BlockSpec((s, tc), lambda n: (0, n))
    wg = pl.BlockSpec((None, 3, tc), lambda n: (layer, 0, n))
    wv = pl.BlockSpec((None, 3, tc), lambda n: (layer, 0, nf + n))
    bg = pl.BlockSpec((None, 1, tc), lambda n: (layer, 0, n))
    bv = pl.BlockSpec((None, 1, tc), lambda n: (layer, 0, nf + n))
    return nf, gate, wg, wv, bg, bv


def _up_conv_fwd(h, w_up, cw, cb, layer, *, name):
    s, d = h.shape
    chips, _, per_chip = w_up.shape
    f = chips * per_chip // 2
    tc = _pick(math.gcd(f, per_chip), (256, 128))
    cr = min(CONV_ROWS, s)
    nf, gate, wg, wv, bg, bv = _conv_specs(s, f, tc, layer)
    per = per_chip // tc

    def body(h_ref, mg_ref, mv_ref, wg_ref, wv_ref, bg_ref, bv_ref, ug_ref, uv_ref, out_ref):
        hb = h_ref[...]
        ug_ref[...] = jnp.dot(hb, mg_ref[...], preferred_element_type=F32)
        uv_ref[...] = jnp.dot(hb, mv_ref[...], preferred_element_type=F32)
        for r0 in range(0, s, cr):
            gc = _conv_rows(*_shift_down(ug_ref, r0, cr, r0 == 0), wg_ref, bg_ref)
            vc = _conv_rows(*_shift_down(uv_ref, r0, cr, r0 == 0), wv_ref, bv_ref)
            out_ref[pl.ds(r0, cr), :] = (gc * jax.nn.sigmoid(gc) * vc).astype(out_ref.dtype)

    def cols(first):
        return pl.BlockSpec((None, d, tc), lambda n: ((first + n) // per, 0, (first + n) % per))

    half = jax.ShapeDtypeStruct((s, f), F32)
    return pl.pallas_call(
        body, name=name, grid=(nf,), in_specs=[pl.BlockSpec((s, d), lambda n: (0, 0)), cols(0), cols(nf), wg, wv, bg, bv],
        out_specs=[gate, gate, gate], out_shape=[half, half, jax.ShapeDtypeStruct((s, f), BF16)],
        compiler_params=_params(("arbitrary",), VMEM_BIG),
    )(h, w_up, w_up, cw, cw, cb.reshape(-1, 1, 2 * f), cb.reshape(-1, 1, 2 * f))


def _conv_bwd(up_gate, up_val, dy, w_down, cw, cb, layer, *, name):
    s, f = up_gate.shape
    d = dy.shape[1]
    f2 = 2 * f
    tc = _pick(f, (256, 128))
    cr = min(CONV_ROWS, s)
    nf, gate, wg, wv, bg, bv = _conv_specs(s, f, tc, layer)

    def body(g_ref, v_ref, dy_ref, wd_first, wd_next, wg_ref, wv_ref, bg_ref, bv_ref, dup_ref, dw_ref, db_ref,
             dgc, dvc, da_ref, da_next):
        @pl.when(pl.program_id(0) == 0)
        def _():
            da_ref[...] = _dot_nt(dy_ref[...], wd_first[...])

        da_next[...] = _dot_nt(dy_ref[...], wd_next[...])
        zero = jnp.zeros((1, tc), F32)
        sums = [[zero] * 4, [zero] * 4]
        for r0 in range(0, s, cr):
            rows = pl.ds(r0, cr)
            gx = _shift_down(g_ref, r0, cr, r0 == 0)
            vx = _shift_down(v_ref, r0, cr, r0 == 0)
            gc = _conv_rows(*gx, wg_ref, bg_ref)
            vc = _conv_rows(*vx, wv_ref, bv_ref)
            sig = jax.nn.sigmoid(gc)
            da = da_ref[rows, :]
            d_gate = da * vc * (sig * (1.0 + gc * (1.0 - sig)))
            d_val = da * (gc * sig)
            dgc[rows, :] = d_gate
            dvc[rows, :] = d_val
            for part, (dc, (x1, x2, x0)) in enumerate(((d_gate, gx), (d_val, vx))):
                for tap, xs in enumerate((x2, x1, x0)):
                    sums[part][tap] = sums[part][tap] + jnp.sum(dc * xs, axis=0, keepdims=True)
                sums[part][3] = sums[part][3] + jnp.sum(dc, axis=0, keepdims=True)
        dw_ref[...] = jnp.zeros_like(dw_ref)
        db_ref[...] = jnp.zeros_like(db_ref)
        for part, (dc_ref, w_ref) in enumerate(((dgc, wg_ref), (dvc, wv_ref))):
            for tap in range(3):
                dw_ref[part, tap:tap + 1, :] = sums[part][tap]
            db_ref[part, 0:1, :] = sums[part][3]
            for r0 in range(0, s, cr):
                d0, d1, d2 = _shift_up(dc_ref, r0, cr, r0 + cr == s)
                dup_ref[part, pl.ds(r0, cr), :] = ((d0 * w_ref[2:3, :] + d1 * w_ref[1:2, :]) + d2 * w_ref[0:1, :]).astype(BF16)
        da_ref[...] = da_next[...]

    small = pl.BlockSpec((2, 8, tc), lambda n: (0, 0, n))
    return pl.pallas_call(
        body, name=name, grid=(nf,),
        in_specs=[gate, gate, pl.BlockSpec((s, d), lambda n: (0, 0)), pl.BlockSpec((tc, d), lambda n: (0, 0)),
                  pl.BlockSpec((tc, d), lambda n: (jnp.minimum(n + 1, nf - 1), 0)), wg, wv, bg, bv],
        out_specs=[pl.BlockSpec((2, s, tc), lambda n: (0, 0, n)), small, small],
        out_shape=[jax.ShapeDtypeStruct((2, s, f), BF16), jax.ShapeDtypeStruct((2, 8, f), F32),
                   jax.ShapeDtypeStruct((2, 8, f), F32)],
        scratch_shapes=[pltpu.VMEM((s, tc), F32)] * 4,
        compiler_params=_params(("arbitrary",), VMEM_BIG),
    )(up_gate, up_val, dy, w_down, w_down, cw, cw, cb.reshape(-1, 1, f2), cb.reshape(-1, 1, f2))


def _adamw(w, g, m, v, *, name, dep=None, copy_g=False):
    shape = w.shape
    cols = shape[-1]
    rows = w.size // cols
    if rows * cols * 4 <= (2 << 20):
        tr = rows
    else:
        tr = next(t for t in (1024, 512, 256, 128, 64, 32, 16, 8) if rows % t == 0 and (t * cols * 4 <= (2 << 20) or t == 8))

    n_out = 4 if copy_g else 3

    def body(w_ref, g_ref, m_ref, v_ref, *rest):
        d_ref, nm_ref, nv_ref = rest[-n_out:][:3]
        gr = g_ref[...]
        if copy_g:
            rest[-1][...] = gr
        nm = ADAM_B1 * m_ref[...] + (1.0 - ADAM_B1) * gr
        nv = ADAM_B2 * v_ref[...] + (1.0 - ADAM_B2) * (gr * gr)
        m_hat = nm / (1.0 - ADAM_B1 ** ADAM_STEP)
        v_hat = nv / (1.0 - ADAM_B2 ** ADAM_STEP)
        d_ref[...] = -ADAM_LR * (m_hat / (jnp.sqrt(v_hat) + ADAM_EPS) + ADAM_WD * w_ref[...])
        nm_ref[...] = nm
        nv_ref[...] = nv

    blk = pl.BlockSpec((tr, cols), lambda i: (i, 0))
    out = jax.ShapeDtypeStruct((rows, cols), F32)
    res = pl.pallas_call(
        body, name=name, grid=(rows // tr,), in_specs=[blk] * 4 + ([] if dep is None else [ANY]), out_specs=[blk] * n_out,
        out_shape=[out] * n_out, compiler_params=_params(("parallel",), VMEM_BIG),
    )(*[t.reshape(rows, cols) for t in (w, g, m, v)], *([] if dep is None else [dep]))
    return [t.reshape(shape) for t in res]


def _place():
    x, y, c = lax.axis_index("x"), lax.axis_index("y"), lax.axis_index("c")
    others = [(1 - x, y), (x, 1 - y), (1 - x, 1 - y)]
    return x, y, c, others


def _remote(src, dst, send_sem, recv_sem, device):
    return pltpu.make_async_remote_copy(src_ref=src, dst_ref=dst, send_sem=send_sem, recv_sem=recv_sem, device_id=device,
                                        device_id_type=MESH)


def _cast_into(w, layer, place, *, name, dep=None):
    _, r, cols = w.shape
    tr = _row_tile(r, cols)

    def body(place_ref, w_ref, *rest):
        rest[-1][...] = w_ref[...].astype(BF16)

    in_specs, args = [pl.BlockSpec((None, tr, cols), lambda i, pr: (layer, i, 0))], [place, w]
    if dep is not None:
        in_specs.append(ANY)
        args.append(dep)
    grid_spec = pltpu.PrefetchScalarGridSpec(
        num_scalar_prefetch=1, grid=(r // tr,), in_specs=in_specs,
        out_specs=pl.BlockSpec((None, tr, cols), lambda i, pr: (pr[1], i, 0)),
    )
    return pl.pallas_call(
        body, name=name, grid_spec=grid_spec, out_shape=jax.ShapeDtypeStruct((N_CHIPS, r, cols), BF16),
        compiler_params=_params(("parallel",), VMEM_BIG),
    )(*args)


HBM = pl.BlockSpec(memory_space=pltpu.HBM)
SEM = pl.BlockSpec(memory_space=pltpu.SEMAPHORE)
EFFECT = pltpu.SideEffectType.DATAFLOW_SIDE_EFFECTING
TOKEN = jax.ShapeDtypeStruct((8, 128), F32)


def _in_hbm(t):
    return pltpu.with_memory_space_constraint(t, pltpu.HBM)


def _gather_copies(buf, send, recv):
    x, y, c, others = _place()
    half = buf.shape[1] // 2
    rows = pl.ds(c * half, half)
    return [_remote(buf.at[2 * x + y, rows], buf.at[2 * x + y, rows], send.at[j], recv.at[j], (px, py, c))
            for j, (px, py) in enumerate(others)]


def _gather_start(bufs, *, name):
    n = len(bufs)

    def body(*refs):
        ins, sends, recvs, token = refs[:n], refs[n:2 * n], refs[2 * n:3 * n], refs[4 * n]
        for a in range(n):
            for cp in _gather_copies(ins[a], sends[a], recvs[a]):
                cp.start()
        token[...] = jnp.zeros_like(token)

    sems = [pltpu.SemaphoreType.DMA((3,))] * (2 * n)
    res = pl.pallas_call(
        body, name=name, out_shape=sems + [pltpu.HBM(t.shape, t.dtype) for t in bufs] + [TOKEN],
        in_specs=[HBM] * n, out_specs=[SEM] * (2 * n) + [HBM] * n + [pl.BlockSpec(memory_space=pltpu.VMEM)],
        input_output_aliases={a: 2 * n + a for a in range(n)}, compiler_params=pltpu.CompilerParams(has_side_effects=EFFECT),
    )(*[_in_hbm(t) for t in bufs])
    return [(res[2 * n + a], res[a], res[n + a]) for a in range(n)], res[3 * n]


def _gather_wait(state, after, *, name):
    buf, send, recv = state

    def body(buf_ref, send_ref, recv_ref, after_ref, out_ref):
        for cp in _gather_copies(buf_ref, send_ref, recv_ref):
            cp.wait_send()
            cp.wait_recv()

    return pl.pallas_call(
        body, name=name, out_shape=pltpu.HBM(buf.shape, buf.dtype), in_specs=[HBM, SEM, SEM, ANY], out_specs=HBM,
        input_output_aliases={0: 0}, compiler_params=pltpu.CompilerParams(has_side_effects=EFFECT),
    )(buf, send, recv, after)


def _chip_copies(src, land, send, recv):
    _x, _y, c, others = _place()
    return [_remote(src.at[2 * px + py], land.at[j], send.at[j], recv.at[j], (px, py, c)) for j, (px, py) in enumerate(others)]


def _chip_start(partial, *, name):
    def body(src, land, send, recv, _src_thru, _land_thru, token):
        for cp in _chip_copies(src, land, send, recv):
            cp.start()
        token[...] = jnp.zeros_like(token)

    land_shape = (3,) + partial.shape[1:]
    sem = pltpu.SemaphoreType.DMA((3,))
    send, recv, src, land, token = pl.pallas_call(
        body, name=name, out_shape=[sem, sem, pltpu.HBM(partial.shape, partial.dtype), pltpu.HBM(land_shape, partial.dtype), TOKEN],
        in_specs=[HBM, HBM], out_specs=[SEM, SEM, HBM, HBM, pl.BlockSpec(memory_space=pltpu.VMEM)],
        input_output_aliases={0: 2, 1: 3}, compiler_params=pltpu.CompilerParams(has_side_effects=EFFECT),
    )(_in_hbm(partial), _in_hbm(lax.empty(land_shape, partial.dtype)))
    return (src, land, send, recv), token


def _chip_wait(state, after, *, name):
    src, land, send, recv = state

    def body(src_ref, land_ref, send_ref, recv_ref, after_ref, _src_out, _land_out):
        for cp in _chip_copies(src_ref, land_ref, send_ref, recv_ref):
            cp.wait_send()
            cp.wait_recv()

    return pl.pallas_call(
        body, name=name, out_shape=[pltpu.HBM(src.shape, src.dtype), pltpu.HBM(land.shape, land.dtype)],
        in_specs=[HBM, HBM, SEM, SEM, ANY], out_specs=[HBM, HBM], input_output_aliases={0: 0, 1: 1},
        compiler_params=pltpu.CompilerParams(has_side_effects=EFFECT),
    )(src, land, send, recv, after)[1]


def _split_start(bufs, copies, n_copies, *, name):
    n = len(bufs)

    def body(*refs):
        for cp in copies(refs[:n], refs[n], refs[n + 1]):
            cp.start()
        refs[-1][...] = jnp.zeros_like(refs[-1])

    sem = pltpu.SemaphoreType.DMA((n_copies,))
    res = pl.pallas_call(
        body, name=name, out_shape=[sem, sem] + [pltpu.HBM(t.shape, t.dtype) for t in bufs] + [TOKEN],
        in_specs=[HBM] * n, out_specs=[SEM, SEM] + [HBM] * n + [pl.BlockSpec(memory_space=pltpu.VMEM)],
        input_output_aliases={a: 2 + a for a in range(n)}, compiler_params=pltpu.CompilerParams(has_side_effects=EFFECT),
    )(*[_in_hbm(t) for t in bufs])
    return (list(res[2:2 + n]), res[0], res[1]), res[-1]


def _split_wait(state, copies, after, *, name):
    bufs, send, recv = state
    n = len(bufs)

    def body(*refs):
        for cp in copies(refs[:n], refs[n], refs[n + 1]):
            cp.wait_send()
            cp.wait_recv()

    return list(pl.pallas_call(
        body, name=name, out_shape=[pltpu.HBM(t.shape, t.dtype) for t in bufs], in_specs=[HBM] * n + [SEM, SEM, ANY],
        out_specs=[HBM] * n, input_output_aliases={a: a for a in range(n)},
        compiler_params=pltpu.CompilerParams(has_side_effects=EFFECT),
    )(*bufs, send, recv, after))


def _hand_over_copies(refs, send, recv):
    x, y, c, others = _place()
    half = refs[0].shape[1] // 2
    got = [refs[0].at[2 * px + py, pl.ds(c * half, half)] for px, py in others]
    return [_remote(got[j], got[j], send.at[j], recv.at[j], (x, y, 1 - c)) for j in range(3)]


def _pair_copies(refs, send, recv):
    x, y, c, _o = _place()
    half = refs[0].shape[1] // 2
    return [_remote(refs[0].at[:, pl.ds((1 - c) * half, half), :], refs[1], send.at[0], recv.at[0], (x, y, 1 - c))]


def _share_copies(refs, send, recv):
    x, y, c, _o = _place()
    return [_remote(refs[0].at[:, c], refs[0].at[:, c], send.at[0], recv.at[0], (x, y, 1 - c))]


def _small_copies(refs, send, recv):
    x, y, c, others = _place()
    peers = [(x, y, 1 - c)] + [(px, py, pc) for px, py in others for pc in (c, 1 - c)]
    slot = refs[1].at[4 * x + 2 * y + c]
    return [_remote(refs[0], slot, send.at[k], recv.at[k], peer) for k, peer in enumerate(peers)]


def _row_tile(rows, cols):
    return max(t for t in range(16, rows + 1, 16) if rows % t == 0 and (t * cols * 4 <= (4 << 20) or t == 16))


def _pair_sum(grad, theirs, place, *, name):
    _, r, cols = grad.shape
    r2 = r // 2
    tr = _row_tile(r2, cols)
    nr = r2 // tr

    def body(place_ref, g_ref, t_ref, all_ref):
        all_ref[...] = (g_ref[...].astype(F32) + t_ref[...].astype(F32)).astype(all_ref.dtype)

    def other(k, pr):
        return k + (k >= pr[1]).astype(jnp.int32)

    grid_spec = pltpu.PrefetchScalarGridSpec(
        num_scalar_prefetch=1, grid=(N_CHIPS - 1, nr),
        in_specs=[pl.BlockSpec((None, tr, cols), lambda k, i, pr: (other(k, pr), pr[0] * nr + i, 0)),
                  pl.BlockSpec((None, tr, cols), lambda k, i, pr: (other(k, pr), i, 0))],
        out_specs=pl.BlockSpec((None, tr, cols), lambda k, i, pr: (other(k, pr), i, 0)),
    )
    return pl.pallas_call(
        body, name=name, grid_spec=grid_spec, out_shape=jax.ShapeDtypeStruct((N_CHIPS, r2, cols), BF16),
        compiler_params=_params(("parallel", "parallel"), VMEM_BIG),
    )(place, grad, theirs)


def _chip_sum(grad, theirs, got, place, buf, layer, depth, *, name):
    _, r, cols = grad.shape
    r2 = r // 2
    tr = _row_tile(r2, cols)
    nr = r2 // tr

    def body(place_ref, g_ref, t_ref, got_ref, *rest):
        own = g_ref[...].astype(F32) + t_ref[...].astype(F32)
        rest[-1][...] = ((own + got_ref[0].astype(F32)) + got_ref[1].astype(F32)) + got_ref[2].astype(F32)

    in_specs = [pl.BlockSpec((None, tr, cols), lambda i, pr: (pr[1], pr[0] * nr + i, 0)),
                pl.BlockSpec((None, tr, cols), lambda i, pr: (pr[1], i, 0)),
                pl.BlockSpec((3, tr, cols), lambda i, pr: (0, i, 0))]
    args = [place, grad, theirs, got]
    if buf is not None:
        in_specs.append(ANY)
        args.append(buf)
    grid_spec = pltpu.PrefetchScalarGridSpec(
        num_scalar_prefetch=1, grid=(nr,), in_specs=in_specs,
        out_specs=pl.BlockSpec((None, None, tr, cols), lambda i, pr: (layer, pr[0], i, 0)),
    )
    return pl.pallas_call(
        body, name=name, grid_spec=grid_spec, out_shape=jax.ShapeDtypeStruct((depth, 2, r2, cols), F32),
        input_output_aliases={} if buf is None else {4: 0}, compiler_params=_params(("parallel",), VMEM_BIG),
    )(*args)


def _sum_devices(parts, own, place, *, name):
    _, rows, cols = parts.shape
    tr = rows if N_DEV * rows * cols * 4 <= (16 << 20) else _pick(rows, (256, 128, 64, 32, 16, 8))

    def body(place_ref, p_ref, own_ref, out_ref):
        me = 2 * place_ref[1] + place_ref[0]
        acc = None
        for dev in range(N_DEV):
            term = jnp.where(me == dev, own_ref[...], p_ref[dev])
            acc = term if acc is None else acc + term
        out_ref[...] = acc

    grid_spec = pltpu.PrefetchScalarGridSpec(
        num_scalar_prefetch=1, grid=(rows // tr,),
        in_specs=[pl.BlockSpec((N_DEV, tr, cols), lambda i, pr: (0, i, 0)), pl.BlockSpec((tr, cols), lambda i, pr: (i, 0))],
        out_specs=pl.BlockSpec((tr, cols), lambda i, pr: (i, 0)),
    )
    return pl.pallas_call(
        body, name=name, grid_spec=grid_spec, out_shape=jax.ShapeDtypeStruct((rows, cols), F32),
        compiler_params=_params(("parallel",), VMEM_BIG),
    )(place, parts, own)


def _pack(parts):
    rows = []
    for t in parts:
        flat = t.reshape(-1, 128)
        pad = (-flat.shape[0]) % 8
        rows.append(jnp.pad(flat, ((0, pad), (0, 0))) if pad else flat)
    return jnp.concatenate(rows, axis=0)


def _unpack(pack, shapes):
    out, r0 = [], 0
    for shp in shapes:
        n = math.prod(shp) // 128
        out.append(pack[r0:r0 + n].reshape(shp))
        r0 += n + (-n) % 8
    return out


SMALL = ["attn_norm_g", "q_norm_g", "k_norm_g", "sgu_norm_g", "sgu_w", "sgu_b", "out_norm_a_g", "out_norm_b_g",
         "ffn_norm_g", "conv_b"]
BIG = ["w_in", "w_out", "w_up", "w_down"]
ORDER = ["attn_norm_g", "w_in", "q_norm_g", "k_norm_g", "sgu_norm_g", "sgu_w", "sgu_b", "out_norm_a_g", "out_norm_b_g",
         "w_out", "ffn_norm_g", "w_up", "conv_w", "conv_b", "w_down"]


def kernel(x, attn_norm_g, w_in, q_norm_g, k_norm_g, sgu_norm_g, sgu_w, sgu_b, out_norm_a_g, out_norm_b_g, w_out, ffn_norm_g, w_up, conv_w, conv_b, w_down, loss_target, m_attn_norm_g, m_w_in, m_q_norm_g, m_k_norm_g, m_sgu_norm_g, m_sgu_w, m_sgu_b, m_out_norm_a_g, m_out_norm_b_g, m_w_out, m_ffn_norm_g, m_w_up, m_conv_w, m_conv_b, m_w_down, v_attn_norm_g, v_w_in, v_q_norm_g, v_k_norm_g, v_sgu_norm_g, v_sgu_w, v_sgu_b, v_out_norm_a_g, v_out_norm_b_g, v_w_out, v_ffn_norm_g, v_w_up, v_conv_w, v_conv_b, v_w_down):
    W = dict(attn_norm_g=attn_norm_g, w_in=w_in, q_norm_g=q_norm_g, k_norm_g=k_norm_g, sgu_norm_g=sgu_norm_g, sgu_w=sgu_w,
             sgu_b=sgu_b, out_norm_a_g=out_norm_a_g, out_norm_b_g=out_norm_b_g, w_out=w_out, ffn_norm_g=ffn_norm_g, w_up=w_up,
             conv_w=conv_w, conv_b=conv_b, w_down=w_down)
    M = dict(attn_norm_g=m_attn_norm_g, w_in=m_w_in, q_norm_g=m_q_norm_g, k_norm_g=m_k_norm_g, sgu_norm_g=m_sgu_norm_g,
             sgu_w=m_sgu_w, sgu_b=m_sgu_b, out_norm_a_g=m_out_norm_a_g, out_norm_b_g=m_out_norm_b_g, w_out=m_w_out,
             ffn_norm_g=m_ffn_norm_g, w_up=m_w_up, conv_w=m_conv_w, conv_b=m_conv_b, w_down=m_w_down)
    V = dict(attn_norm_g=v_attn_norm_g, w_in=v_w_in, q_norm_g=v_q_norm_g, k_norm_g=v_k_norm_g, sgu_norm_g=v_sgu_norm_g,
             sgu_w=v_sgu_w, sgu_b=v_sgu_b, out_norm_a_g=v_out_norm_a_g, out_norm_b_g=v_out_norm_b_g, w_out=v_w_out,
             ffn_norm_g=v_ffn_norm_g, w_up=v_w_up, conv_w=v_conv_w, conv_b=v_conv_b, w_down=v_w_down)
    depth = w_in.shape[0]
    s, d = x.shape[1], x.shape[2]
    n_heads = out_norm_a_g.shape[1]
    core = lax.axis_index("c")
    chip = 2 * lax.axis_index("x") + lax.axis_index("y")
    place = jnp.stack([core, chip]).astype(jnp.int32)
    xs = x.reshape(s, d)

    f_local = conv_w.shape[2]
    taps = lax.dynamic_update_slice(jnp.zeros((N_CHIPS, 16, f_local), F32), conv_w.reshape(1, depth * 3, f_local),
                                    (chip, 0, 0))
    order = [(l, n) for l in range(depth) for n in BIG]
    first, token = _gather_start([_cast_into(w_in, 0, place, name="cast_w_in"), taps], name="gather_start_first")
    rest, token = _gather_start([_cast_into(W[n], l, place, dep=token, name=f"cast_{n}") for l, n in order[1:]],
                                name="gather_start_rest")
    states = dict(zip(order, [first[0]] + rest))

    states["taps"] = first[1]

    def landed(key, after, tag):
        buf = _gather_wait(states[key], after, name=f"gather_wait_{tag}")
        return _split_start([buf], _hand_over_copies, 3, name=f"hand_over_{tag}")

    def whole(state, after, tag):
        return _split_wait(state, _hand_over_copies, after, name=f"hand_over_wait_{tag}")[0]

    saved, full = [], []
    cur = xs
    for l in range(depth):
        gain = attn_norm_g + token[0, 0] if l == 0 else attn_norm_g
        h = _rmsnorm_fwd(cur, gain, l, name="attn_norm")
        if l == 0:
            ho_in, _ = landed((0, "w_in"), h, "w_in")
        w_in_l = whole(ho_in, h, "w_in")
        if l == 0:
            ho_taps, token = landed("taps", w_in_l, "taps")
        p = _mm(h, w_in_l, "nn", b_split=N_CHIPS, caps=(2048, 256, 2048), dep=token, name="proj_in")
        if l == 0:
            taps = whole(ho_taps, p, "taps")[:, :depth * 3].reshape(N_CHIPS, depth, 3, f_local)
            cw_full = jnp.transpose(taps, (1, 2, 0, 3)).reshape(depth, 3, N_CHIPS * f_local)
        mix, o_raw, lsum = _attn_fwd(p, q_norm_g, k_norm_g, out_norm_a_g, l, n_heads, name="attn_fwd")
        ho_out, token = landed((l, "w_out"), mix, "w_out")
        mix = _sgu_fwd(p, mix, sgu_w, sgu_b, sgu_norm_g + token[0, 0], out_norm_b_g, l, n_heads, name="sgu_fwd")
        w_out_l = whole(ho_out, mix, "w_out").reshape(-1, d)
        ho_up, token = landed((l, "w_up"), w_out_l, "w_up")
        x1 = _mm(mix, w_out_l, "nn", a_split=2, res=cur, caps=(2048, 512, 1024), dep=token, name="proj_out")
        h2 = _rmsnorm_fwd(x1, ffn_norm_g, l, name="ffn_norm")
        w_up_l = whole(ho_up, h2, "w_up")
        up_gate, up_val, act = _up_conv_fwd(h2, w_up_l, cw_full, conv_b, l, name="ffn_up_conv")
        ho_down, token = landed((l, "w_down"), act, "w_down")
        w_down_l = whole(ho_down, act, "w_down").reshape(-1, d)
        if l + 1 < depth:
            ho_in, token = landed((l + 1, "w_in"), w_down_l, "w_in")
        x2 = _mm(act, w_down_l, "nn", res=x1, caps=(1024, 512, 2816), dep=token, name="ffn_down")
        full.append(dict(w_in=w_in_l, w_out=w_out_l, w_up=w_up_l, w_down=w_down_l))
        saved.append(dict(x0=cur, h=h, p=p, o_raw=o_raw, lsum=lsum, mix=mix, x1=x1, h2=h2, up_gate=up_gate, up_val=up_val,
                          act=act))
        cur = x2

    dx, dxb, sq = _loss_head(cur, loss_target.reshape(s, d), name="loss_head")
    loss = lax.psum(sq[0, 0] * (0.5 / d), ("x", "y", "c"))

    small_grads = {n: [None] * depth for n in SMALL + ["conv_w"]}
    def pair_begin(n, grad):
        land = lax.empty((N_CHIPS, grad.shape[1] // 2, grad.shape[2]), grad.dtype)
        return _split_start([grad, land], _pair_copies, 1, name=f"pair_start_{n}")

    def chip_begin(n, state, after):
        grad, theirs = _split_wait(state, _pair_copies, after, name=f"pair_wait_{n}")
        state, tok = _chip_start(_pair_sum(grad, theirs, place, name=f"pair_sum_{n}"), name=f"chip_start_{n}")
        return (grad, theirs, state), tok

    pending = {}
    for l in reversed(range(depth)):
        fw, sv = full[l], saved[l]
        g_down = _mm(sv["act"], dxb, "tn", caps=(512, 2048, 2048), out_dtype=BF16, name="g_down")
        pair, tok = pair_begin("w_down", g_down.reshape(N_CHIPS, -1, d))
        dup, dcw, dcb = _conv_bwd(sv["up_gate"], sv["up_val"], dxb, fw["w_down"], cw_full, conv_b + tok[0, 0], l,
                                  name="conv_bwd")
        pending[(l, "w_down")], tok = chip_begin("w_down", pair, dup)
        g_up = _mm(sv["h2"], dup, "tn", b_split=2, o_split=N_CHIPS, caps=(2048, 256, 2048), out_dtype=BF16, dep=tok,
                   name="g_up")
        pair, tok = pair_begin("w_up", g_up)
        dh2 = _mm(dup, fw["w_up"], "nt", a_split=2, b_split=N_CHIPS, caps=(2048, 512, 2816), dep=tok, name="d_h2")
        pending[(l, "w_up")], tok = chip_begin("w_up", pair, dh2)
        dx1, dx1b, dg_ffn = _rmsnorm_bwd(sv["x1"], ffn_norm_g + tok[0, 0], l, dh2, dx, name="ffn_norm_bwd")
        dmix = _mm(dx1b, fw["w_out"], "nt", caps=(2048, 512, 2048), name="d_mix")
        g_out = _mm(sv["mix"], dx1b, "tn", a_split=2, caps=(512, 2048, 2048), out_dtype=BF16, name="g_out")
        pair, tok = pair_begin("w_out", g_out.reshape(N_CHIPS, -1, d))
        dqkv, dgq, dgk, dgoa = _attn_bwd(sv["p"], sv["o_raw"], sv["lsum"], dmix, q_norm_g + tok[0, 0], k_norm_g,
                                         out_norm_a_g, l, n_heads, name="attn_bwd")
        pending[(l, "w_out")], tok = chip_begin("w_out", pair, dqkv)
        duv, dsw, dsb, dgv, dgob = _sgu_bwd(sv["p"], dmix, sgu_w, sgu_b, sgu_norm_g + tok[0, 0], out_norm_b_g, l,
                                            n_heads, name="sgu_bwd")
        dp = jnp.concatenate([dqkv[0], dqkv[1], dqkv[2], duv[0], duv[1]], axis=1)
        g_in = _mm(sv["h"], dp, "tn", o_split=N_CHIPS, caps=(2048, 256, 2048), out_dtype=BF16, name="g_in")
        pair, tok = pair_begin("w_in", g_in)
        dh = _mm(dp, fw["w_in"], "nt", b_split=N_CHIPS, caps=(2048, 512, 1280), dep=tok, name="d_h")
        pending[(l, "w_in")], tok = chip_begin("w_in", pair, dh)
        dx, dxb, dg_attn = _rmsnorm_bwd(sv["x0"], attn_norm_g + tok[0, 0], l, dh, dx1, name="attn_norm_bwd")

        small_grads["attn_norm_g"][l] = dg_attn.reshape(d)
        small_grads["q_norm_g"][l] = jnp.sum(dgq, axis=(0, 1))
        small_grads["k_norm_g"][l] = jnp.sum(dgk, axis=(0, 1))
        small_grads["sgu_norm_g"][l] = dgv.reshape(-1, BLK)
        small_grads["sgu_w"][l] = dsw
        small_grads["sgu_b"][l] = dsb.reshape(-1, BLK)
        small_grads["out_norm_a_g"][l] = dgoa.reshape(-1, BLK)
        small_grads["out_norm_b_g"][l] = dgob.reshape(-1, BLK)
        small_grads["ffn_norm_g"][l] = dg_ffn.reshape(d)
        small_grads["conv_b"][l] = dcb[:, 0, :].reshape(-1)
        small_grads["conv_w"][l] = jnp.transpose(dcw[:, :3, :], (1, 0, 2)).reshape(3, -1)

    names = SMALL + ["conv_w"]
    pack = _pack([jnp.stack(small_grads[n]) for n in names])
    small, tok = _split_start([pack, lax.empty((N_DEV,) + pack.shape, F32)], _small_copies, N_DEV - 1, name="small_start")
    G, D_, NM, NV = {}, {}, {}, {}
    after, prev = tok, None
    for n in ("w_down", "w_up", "w_out", "w_in"):
        buf = None
        for l in reversed(range(depth)):
            grad, theirs, state = pending[(l, n)]
            got = _chip_wait(state, after, name=f"chip_wait_{n}")
            buf = _chip_sum(grad, theirs, got, place, buf, l, depth, name=f"chip_sum_{n}")
            after = buf
        share, tok = _split_start([buf], _share_copies, 1, name=f"share_start_{n}")
        if prev is not None:
            D_[prev], NM[prev], NV[prev], G[prev] = _adamw(W[prev], G[prev], M[prev], V[prev], dep=tok, copy_g=True,
                                                           name=f"adamw_{prev}")
            after = NV[prev]
        G[n] = _split_wait(share, _share_copies, after, name=f"share_wait_{n}")[0].reshape(W[n].shape)
        after, prev = G[n], n
    D_[prev], NM[prev], NV[prev], G[prev] = _adamw(W[prev], G[prev], M[prev], V[prev], copy_g=True, name=f"adamw_{prev}")

    pack, parts = _split_wait(small, _small_copies, NV[prev], name="small_wait")
    total = _sum_devices(parts, pack, place, name="sum_small")
    f_full = conv_b.shape[1]
    shapes = [W[n].shape for n in SMALL] + [(depth, 3, f_full)]
    for n, t in zip(names, _unpack(total, shapes)):
        G[n] = t
    G["conv_w"] = lax.dynamic_slice_in_dim(G["conv_w"], chip * f_local, f_local, axis=2)

    D_["conv_w"], NM["conv_w"], NV["conv_w"] = _adamw(conv_w, G["conv_w"], m_conv_w, v_conv_w, name="adamw_conv_w")
    small_shapes = [W[n].shape for n in SMALL]
    res = _adamw(_pack([W[n] for n in SMALL]), _pack([G[n] for n in SMALL]), _pack([M[n] for n in SMALL]),
                 _pack([V[n] for n in SMALL]), name="adamw_small")
    for dst, t in zip((D_, NM, NV), res):
        for n, u in zip(SMALL, _unpack(t, small_shapes)):
            dst[n] = u

    return (loss, dx.reshape(x.shape), *[G[n] for n in ORDER], *[D_[n] for n in ORDER], *[NM[n] for n in ORDER],
            *[NV[n] for n in ORDER])
```

```python
import functools
import math

import jax
import jax.numpy as jnp
from jax import lax
from jax.experimental import pallas as pl
from jax.experimental.pallas import tpu as pltpu

F32 = jnp.float32
BF16 = jnp.bfloat16
EPS = 1e-6
BLK = 128
N_CHIPS = 4
N_DEV = 8
ADAM_LR, ADAM_B1, ADAM_B2, ADAM_EPS, ADAM_WD, ADAM_STEP = 0.001, 0.9, 0.999, 1e-08, 0.01, 10
VMEM_BIG = 48 * 1024 * 1024
MESH = pl.DeviceIdType.MESH
ANY = pl.BlockSpec(memory_space=pl.ANY)


def _pick(dim, prefs):
    for t in prefs:
        if dim % t == 0:
            return t
    raise ValueError(f"no tile in {prefs} divides {dim}")


def _params(sem=None, vmem=None):
    return pltpu.CompilerParams(dimension_semantics=sem, vmem_limit_bytes=vmem)


def _ldims(arr, split):
    if split == 1:
        return arr.shape
    p, r, cs = arr.shape
    assert p == split
    return (r, p * cs)


def _spec(tr, tc, split, cols, rc):
    if split == 1:
        return pl.BlockSpec((tr, tc), lambda i, j, k: rc(i, j, k))
    per = (cols // split) // tc

    def imap(i, j, k):
        r, c = rc(i, j, k)
        return (c // per, r, c % per)

    return pl.BlockSpec((None, tr, tc), imap)


def _fit(unit, cap):
    return max(t for t in range(128, min(unit, cap) + 1, 128) if unit % t == 0)


def _mm(a, b, mode, *, name, caps, a_split=1, b_split=1, o_split=1, out_dtype=F32, res=None, dep=None):
    ar, ac = _ldims(a, a_split)
    br, bc = _ldims(b, b_split)
    if mode == "nn":
        m, k, n = ar, ac, bc
        assert br == k
        ku, nu, mu = math.gcd(k // a_split, k), math.gcd(n // b_split, n // o_split), m
    elif mode == "nt":
        m, k, n = ar, ac, br
        assert bc == k
        ku, nu, mu = math.gcd(k // a_split, k // b_split), n // o_split, m
    else:
        k, m, n = ar, ac, bc
        assert br == k
        ku, nu, mu = k, math.gcd(n // b_split, n // o_split), m // a_split
    tm, tn, tk = _fit(mu, caps[0]), _fit(nu, caps[1]), _fit(ku, caps[2])
    nk = k // tk
    if mode == "nn":
        a_spec = _spec(tm, tk, a_split, k, lambda i, j, kk: (i, kk))
        b_spec = _spec(tk, tn, b_split, n, lambda i, j, kk: (kk, j))
    elif mode == "nt":
        a_spec = _spec(tm, tk, a_split, k, lambda i, j, kk: (i, kk))
        b_spec = _spec(tn, tk, b_split, k, lambda i, j, kk: (j, kk))
    else:
        a_spec = _spec(tk, tm, a_split, m, lambda i, j, kk: (kk, i))
        b_spec = _spec(tk, tn, b_split, n, lambda i, j, kk: (kk, j))
    o_spec = _spec(tm, tn, o_split, n, lambda i, j, kk: (i, j))
    dims = {"nn": (((1,), (0,)), ((), ())), "nt": (((1,), (1,)), ((), ())), "tn": (((0,), (0,)), ((), ()))}[mode]

    def body(a_ref, b_ref, *rest):
        if dep is not None:
            rest = rest[1:]
        if res is None:
            o_ref, acc = rest
        else:
            r_ref, o_ref, acc = rest
        kk = pl.program_id(2)

        @pl.when(kk == 0)
        def _():
            acc[...] = jnp.zeros_like(acc)

        acc[...] += lax.dot_general(a_ref[...].astype(BF16), b_ref[...].astype(BF16), dims, preferred_element_type=F32)

        @pl.when(kk == nk - 1)
        def _():
            out = acc[...]
            if res is not None:
                out = out + r_ref[...]
            o_ref[...] = out.astype(o_ref.dtype)

    in_specs, args = [a_spec, b_spec], [a, b]
    if dep is not None:
        in_specs.append(ANY)
        args.append(dep)
    if res is not None:
        in_specs.append(pl.BlockSpec((tm, tn), lambda i, j, kk: (i, j)))
        args.append(res)
    out_shape = (m, n) if o_split == 1 else (o_split, m, n // o_split)
    return pl.pallas_call(
        body, name=name, grid=(m // tm, n // tn, nk), in_specs=in_specs, out_specs=o_spec,
        out_shape=jax.ShapeDtypeStruct(out_shape, out_dtype), scratch_shapes=[pltpu.VMEM((tm, tn), F32)],
        compiler_params=_params(("parallel", "parallel", "arbitrary"), VMEM_BIG),
    )(*args)


def _rstd(v):
    return lax.rsqrt(jnp.mean(v * v, axis=-1, keepdims=True) + EPS)


def _norm_bwd(v, r, gain, dout):
    a = dout * gain
    dv = r * (a - v * (r * r * jnp.mean(a * v, axis=-1, keepdims=True)))
    return dv, dout * v * r


def _rmsnorm_fwd(x, g, layer, *, name):
    s, d = x.shape
    tr = _pick(s, (256, 128))

    def body(x_ref, g_ref, o_ref):
        v = x_ref[...]
        o_ref[...] = (v * _rstd(v) * g_ref[...]).astype(o_ref.dtype)

    return pl.pallas_call(
        body, name=name, grid=(s // tr,),
        in_specs=[pl.BlockSpec((tr, d), lambda i: (i, 0)), pl.BlockSpec((None, 1, d), lambda i: (layer, 0, 0))],
        out_specs=pl.BlockSpec((tr, d), lambda i: (i, 0)), out_shape=jax.ShapeDtypeStruct((s, d), BF16),
        compiler_params=_params(("parallel",)),
    )(x, g.reshape(-1, 1, d))


def _rmsnorm_bwd(x, g, layer, dh, dres, *, name):
    s, d = x.shape
    tr = _pick(s, (256, 128))

    def body(x_ref, g_ref, dh_ref, dres_ref, dx_ref, dxb_ref, dg_ref):
        v = x_ref[...]
        dv, dgr = _norm_bwd(v, _rstd(v), g_ref[...], dh_ref[...])
        dx = dres_ref[...] + dv
        dx_ref[...] = dx
        dxb_ref[...] = dx.astype(BF16)
        part = jnp.sum(dgr, axis=0, keepdims=True)

        @pl.when(pl.program_id(0) == 0)
        def _():
            dg_ref[...] = part

        @pl.when(pl.program_id(0) > 0)
        def _():
            dg_ref[...] += part

    row = pl.BlockSpec((tr, d), lambda i: (i, 0))
    one = pl.BlockSpec((1, d), lambda i: (0, 0))
    return pl.pallas_call(
        body, name=name, grid=(s // tr,), in_specs=[row, pl.BlockSpec((None, 1, d), lambda i: (layer, 0, 0)), row, row],
        out_specs=[row, row, one],
        out_shape=[jax.ShapeDtypeStruct((s, d), F32), jax.ShapeDtypeStruct((s, d), BF16), jax.ShapeDtypeStruct((1, d), F32)],
        compiler_params=_params(("arbitrary",)),
    )(x, g.reshape(-1, 1, d), dh, dres)


def _loss_head(y, target, *, name):
    s, d = y.shape
    tr = _pick(s, (256, 128))

    def body(y_ref, t_ref, dy_ref, dyb_ref, ls_ref):
        e = y_ref[...] - t_ref[...]
        dy = e * (1.0 / d)
        dy_ref[...] = dy
        dyb_ref[...] = dy.astype(BF16)
        part = jnp.full(ls_ref.shape, jnp.sum(e * e), F32)

        @pl.when(pl.program_id(0) == 0)
        def _():
            ls_ref[...] = part

        @pl.when(pl.program_id(0) > 0)
        def _():
            ls_ref[...] += part

    row = pl.BlockSpec((tr, d), lambda i: (i, 0))
    return pl.pallas_call(
        body, name=name, grid=(s // tr,), in_specs=[row, row], out_specs=[row, row, pl.BlockSpec((8, 128), lambda i: (0, 0))],
        out_shape=[jax.ShapeDtypeStruct((s, d), F32), jax.ShapeDtypeStruct((s, d), BF16), jax.ShapeDtypeStruct((8, 128), F32)],
        compiler_params=_params(("arbitrary",)),
    )(y, target)


def _iota2(axis):
    return lax.broadcasted_iota(jnp.int32, (BLK, BLK), axis)


def _tri_sum(v, tri):
    hi = v.astype(BF16)
    lo = (v - hi.astype(F32)).astype(BF16)
    return jnp.dot(hi, tri, preferred_element_type=F32) + jnp.dot(lo, tri, preferred_element_type=F32)


def _dot_nt(a, b):
    return lax.dot_general(a, b, (((1,), (1,)), ((), ())), preferred_element_type=F32)


def _dot_tn(a, b):
    return lax.dot_general(a, b, (((0,), (0,)), ((), ())), preferred_element_type=F32)


TQ_MAX = 1024
TQ_MAX_BWD = 1024
HP = 2
VMEM_ATTN_BWD = 56 * 1024 * 1024


def _lanes(hh):
    return slice(hh * BLK, (hh + 1) * BLK)


def _causal(n, diag):
    if not diag:
        return None
    return lax.broadcasted_iota(jnp.int32, (n, BLK), 1) < lax.broadcasted_iota(jnp.int32, (n, BLK), 0)


def _sb_sums(z, mask, rhs_gt):
    lb = jnp.minimum(z, 0.0) - jnp.log(1.0 + jnp.exp(-jnp.abs(z)))
    l1m = lb - z
    if mask is not None:
        l1m = jnp.where(mask, l1m, 0.0)
    return lb, _tri_sum(l1m, rhs_gt)


def _below(old, new, r0):
    return new if r0 == 0 else jnp.concatenate([old[:r0], new], axis=0)


def _attn_fwd(p, gq, gk, go, layer, n_heads, *, name):
    s = p.shape[0]
    tq = min(TQ_MAX, s)
    per = tq // BLK
    scale = BLK ** -0.5

    def body(q_ref, k_ref, v_ref, gq_ref, gk_ref, go_ref, att_ref, o_ref, l_ref, qn, kn, vb):
        for hh in range(HP):
            q = q_ref[:, _lanes(hh)]
            k = k_ref[:, _lanes(hh)]
            qn[:, _lanes(hh)] = (q * _rstd(q) * gq_ref[...] * scale).astype(BF16)
            kn[:, _lanes(hh)] = (k * _rstd(k) * gk_ref[...]).astype(BF16)
            vb[:, _lanes(hh)] = v_ref[:, _lanes(hh)].astype(BF16)
        rhs_gt = jnp.concatenate([(_iota2(0) > _iota2(1)).astype(BF16), jnp.ones((BLK, BLK), BF16)], axis=1)

        def step(q0, j, r0, diag, states):
            n = tq - r0
            rows = pl.ds(pl.multiple_of(q0 + r0, BLK), n)
            cols = pl.ds(pl.multiple_of(j * BLK, BLK), BLK)
            mask = _causal(n, diag)
            zs = [_dot_nt(qn[rows, _lanes(hh)], kn[cols, _lanes(hh)]) for hh in range(HP)]
            sums = [_sb_sums(z, mask, rhs_gt) for z in zs]
            new = []
            for hh in range(HP):
                acc, later = states[hh]
                lb, both = sums[hh]
                a = jnp.exp(lb + both[:, :BLK] + later[r0:])
                if diag:
                    a = jnp.where(mask, a, 0.0)
                acc_new = acc[r0:] + jnp.dot(a.astype(BF16), vb[cols, _lanes(hh)], preferred_element_type=F32)
                new.append((_below(acc, acc_new, r0), _below(later, later[r0:] + both[:, BLK:], r0)))
            return tuple(new)

        def q_block(i, _):
            q0 = i * tq
            zero = jnp.zeros((tq, BLK), F32)
            states = ((zero, zero),) * HP
            for jd in reversed(range(per)):
                states = step(q0, i * per + jd, jd * BLK, True, states)
            states = lax.fori_loop(0, i * per, lambda jj, st: step(q0, i * per - 1 - jj, 0, False, st), states)
            tile = pl.ds(pl.multiple_of(q0, tq), tq)
            for hh in range(HP):
                o, total = states[hh]
                o_ref[tile, _lanes(hh)] = o
                l_ref[hh, tile, :] = total
                att_ref[tile, _lanes(hh)] = (o * _rstd(o) * go_ref[hh]).astype(att_ref.dtype)
            return 0

        lax.fori_loop(0, s // tq, q_block, 0)

    assert n_heads % HP == 0
    groups = n_heads // HP

    def col(part):
        return pl.BlockSpec((s, HP * BLK), lambda g: (0, part * groups + g))

    gain = pl.BlockSpec((None, 1, BLK), lambda g: (layer, 0, 0))
    per_head = pl.BlockSpec((None, HP, 1, BLK), lambda g: (layer, g, 0, 0))
    return pl.pallas_call(
        body, name=name, grid=(groups,),
        in_specs=[col(0), col(1), col(2), gain, gain, per_head],
        out_specs=[pl.BlockSpec((None, s, HP * BLK), lambda g: (0, 0, g)), col(0), pl.BlockSpec((HP, s, BLK), lambda g: (g, 0, 0))],
        out_shape=[jax.ShapeDtypeStruct((2, s, n_heads * BLK), BF16), jax.ShapeDtypeStruct((s, n_heads * BLK), F32),
                   jax.ShapeDtypeStruct((n_heads, s, BLK), F32)],
        scratch_shapes=[pltpu.VMEM((s, HP * BLK), BF16)] * 3,
        compiler_params=_params(("parallel",), VMEM_BIG),
    )(p, p, p, gq.reshape(-1, 1, BLK), gk.reshape(-1, 1, BLK), go.reshape(-1, n_heads, 1, BLK))


def _attn_bwd(p, o_raw, lsum, dmix, gq, gk, go, layer, n_heads, *, name):
    s = p.shape[0]
    tq = min(TQ_MAX_BWD, s)
    per = tq // BLK
    scale = BLK ** -0.5

    def body(q_ref, k_ref, v_ref, o_ref, l_ref, da_ref, gq_ref, gk_ref, go_ref,
             dqkv_ref, dgq_ref, dgk_ref, dgo_ref, qn, kn, vb, dob, dqn, dkn, dvv):
        for hh in range(HP):
            q = q_ref[:, _lanes(hh)]
            k = k_ref[:, _lanes(hh)]
            qn[:, _lanes(hh)] = (q * _rstd(q) * gq_ref[...] * scale).astype(BF16)
            kn[:, _lanes(hh)] = (k * _rstd(k) * gk_ref[...]).astype(BF16)
            vb[:, _lanes(hh)] = v_ref[:, _lanes(hh)].astype(BF16)
            o = o_ref[:, _lanes(hh)]
            do, dgo_rows = _norm_bwd(o, _rstd(o), go_ref[hh], da_ref[:, _lanes(hh)])
            dob[:, _lanes(hh)] = do.astype(BF16)
            dgo_ref[hh] = jnp.sum(dgo_rows, axis=0, keepdims=True)
        dkn[...] = jnp.zeros_like(dkn)
        dvv[...] = jnp.zeros_like(dvv)
        ones = jnp.ones((BLK, BLK), BF16)
        rhs_gt = jnp.concatenate([(_iota2(0) > _iota2(1)).astype(BF16), ones], axis=1)
        rhs_lt = jnp.concatenate([(_iota2(0) < _iota2(1)).astype(BF16), ones], axis=1)

        def step(q0, j, r0, diag, states):
            n = tq - r0
            rows = pl.ds(pl.multiple_of(q0 + r0, BLK), n)
            cols = pl.ds(pl.multiple_of(j * BLK, BLK), BLK)
            mask = _causal(n, diag)
            zs = [_dot_nt(qn[rows, _lanes(hh)], kn[cols, _lanes(hh)]) for hh in range(HP)]
            das = [_dot_nt(dob[rows, _lanes(hh)], vb[cols, _lanes(hh)]) for hh in range(HP)]
            sums = [_sb_sums(z, mask, rhs_gt) for z in zs]
            mids = []
            for hh in range(HP):
                lb, both = sums[hh]
                upto = states[hh][0][r0:] + both[:, BLK:]
                a = jnp.exp(lb + both[:, :BLK] + (l_ref[hh, rows, :] - upto))
                if diag:
                    a = jnp.where(mask, a, 0.0)
                g = das[hh] * a
                mids.append((lb, upto, a, g, _tri_sum(g, rhs_lt)))
            new = []
            for hh in range(HP):
                seen, gsum, dq = states[hh]
                lb, upto, a, g, bothg = mids[hh]
                beta = jnp.exp(lb)
                dz = g * (1.0 - beta) - beta * (bothg[:, :BLK] + gsum[r0:])
                if diag:
                    dz = jnp.where(mask, dz, 0.0)
                dzs = dz.astype(BF16)
                dq_new = dq[r0:] + jnp.dot(dzs, kn[cols, _lanes(hh)], preferred_element_type=F32)
                dkn[cols, _lanes(hh)] += _dot_tn(dzs, qn[rows, _lanes(hh)])
                dvv[cols, _lanes(hh)] += _dot_tn(a.astype(BF16), dob[rows, _lanes(hh)])
                new.append((_below(seen, upto, r0), _below(gsum, gsum[r0:] + bothg[:, BLK:], r0), _below(dq, dq_new, r0)))
            return tuple(new)

        def q_block(i, _):
            q0 = i * tq
            zero = jnp.zeros((tq, BLK), F32)
            states = ((zero, zero, zero),) * HP
            states = lax.fori_loop(0, i * per, lambda j, st: step(q0, j, 0, False, st), states)
            for jd in range(per):
                states = step(q0, i * per + jd, jd * BLK, True, states)
            tile = pl.ds(pl.multiple_of(q0, tq), tq)
            for hh in range(HP):
                dqn[tile, _lanes(hh)] = states[hh][2]
            return 0

        lax.fori_loop(0, s // tq, q_block, 0)
        for hh in range(HP):
            q = q_ref[:, _lanes(hh)]
            k = k_ref[:, _lanes(hh)]
            dq_raw, dgq_rows = _norm_bwd(q, _rstd(q), gq_ref[...], dqn[:, _lanes(hh)] * scale)
            dk_raw, dgk_rows = _norm_bwd(k, _rstd(k), gk_ref[...], dkn[:, _lanes(hh)])
            dqkv_ref[0, :, _lanes(hh)] = dq_raw.astype(BF16)
            dqkv_ref[1, :, _lanes(hh)] = dk_raw.astype(BF16)
            dqkv_ref[2, :, _lanes(hh)] = dvv[:, _lanes(hh)].astype(BF16)
            dgq_ref[hh] = jnp.sum(dgq_rows, axis=0, keepdims=True)
            dgk_ref[hh] = jnp.sum(dgk_rows, axis=0, keepdims=True)

    assert n_heads % HP == 0
    groups = n_heads // HP

    once = pl.Buffered(1)

    def col(part):
        return pl.BlockSpec((s, HP * BLK), lambda g: (0, part * groups + g), pipeline_mode=once)

    gain = pl.BlockSpec((None, 1, BLK), lambda g: (layer, 0, 0))
    per_head = pl.BlockSpec((HP, 1, BLK), lambda g: (g, 0, 0))
    head_gain = jax.ShapeDtypeStruct((n_heads, 1, BLK), F32)
    return pl.pallas_call(
        body, name=name, grid=(groups,),
        in_specs=[col(0), col(1), col(2), col(0), pl.BlockSpec((HP, s, BLK), lambda g: (g, 0, 0), pipeline_mode=once), col(0),
                  gain, gain, pl.BlockSpec((None, HP, 1, BLK), lambda g: (layer, g, 0, 0))],
        out_specs=[pl.BlockSpec((3, s, HP * BLK), lambda g: (0, 0, g)), per_head, per_head, per_head],
        out_shape=[jax.ShapeDtypeStruct((3, s, n_heads * BLK), BF16), head_gain, head_gain, head_gain],
        scratch_shapes=[pltpu.VMEM((s, HP * BLK), BF16)] * 4 + [pltpu.VMEM((s, HP * BLK), F32)] * 3,
        compiler_params=_params(("parallel",), VMEM_ATTN_BWD),
    )(p, p, p, o_raw, lsum, dmix, gq.reshape(-1, 1, BLK), gk.reshape(-1, 1, BLK), go.reshape(-1, n_heads, 1, BLK))


SGU_TOGETHER = 4
_INV_SQRT2 = 0.7071067811865476
_INV_SQRT2PI = 0.3989422804014327


def _gelu(x):
    return 0.5 * x * (1.0 + lax.erf(x * _INV_SQRT2))


def _gelu_and_grad(x):
    cdf = 0.5 * (1.0 + lax.erf(x * _INV_SQRT2))
    return x * cdf, cdf + x * jnp.exp(-0.5 * x * x) * _INV_SQRT2PI


def _sgu_fwd(p, mix, w, b, gv, gout, layer, n_heads, *, name):
    s = p.shape[0]
    n_groups = w.shape[1]
    nb = s // BLK
    assert mix.shape == (2, s, n_groups * BLK)

    def body(u_ref, v_ref, w_ref, b_ref, gv_ref, go_ref, _mix_ref, out_ref):
        wt = jnp.where(_iota2(0) >= _iota2(1), w_ref[...], 0.0).astype(BF16)
        bias = b_ref[...]

        def chunks(i, _):
            rows = [pl.ds(pl.multiple_of((i * SGU_TOGETHER + k) * BLK, BLK), BLK) for k in range(SGU_TOGETHER)]
            us = [_gelu(u_ref[r, :]) for r in rows]
            vss = []
            for r in rows:
                vv = _gelu(v_ref[r, :])
                vss.append((vv * _rstd(vv) * gv_ref[...]).astype(BF16))
            mixed = [jnp.dot(wt, vs, preferred_element_type=F32) + bias for vs in vss]
            for r, u, m in zip(rows, us, mixed):
                gated = u * m
                out_ref[r, :] = (gated * _rstd(gated) * go_ref[...]).astype(out_ref.dtype)
            return 0

        assert nb % SGU_TOGETHER == 0
        lax.fori_loop(0, nb // SGU_TOGETHER, chunks, 0)

    def col(off):
        return pl.BlockSpec((s, BLK), lambda g: (0, off + g))

    per_group = pl.BlockSpec((None, None, 1, BLK), lambda g: (layer, g, 0, 0))
    return pl.pallas_call(
        body, name=name, grid=(n_groups,),
        in_specs=[col(3 * n_heads), col(3 * n_heads + n_groups), pl.BlockSpec((None, None, BLK, BLK), lambda g: (layer, g, 0, 0)),
                  pl.BlockSpec((None, None, BLK, 1), lambda g: (layer, g, 0, 0)), per_group, per_group, ANY],
        out_specs=pl.BlockSpec((None, s, BLK), lambda g: (1, 0, g)), out_shape=jax.ShapeDtypeStruct(mix.shape, mix.dtype),
        input_output_aliases={6: 0}, compiler_params=_params(("parallel",), VMEM_BIG),
    )(p, p, w, b.reshape(-1, n_groups, BLK, 1), gv.reshape(-1, n_groups, 1, BLK), gout.reshape(-1, n_groups, 1, BLK), mix)


def _sgu_bwd(p, dmix, w, b, gv, gout, layer, n_heads, *, name):
    s = p.shape[0]
    n_groups = w.shape[1]
    nb = s // BLK

    def body(u_ref, v_ref, ds_ref, w_ref, b_ref, gv_ref, go_ref, duv_ref, dw_ref, db_ref, dgv_ref, dgo_ref):
        lower = _iota2(0) >= _iota2(1)
        wt = jnp.where(lower, w_ref[...], 0.0).astype(BF16)
        bias = b_ref[...]

        def chunks(i, carry):
            dw, db, dgv, dgo = carry
            rows = [pl.ds(pl.multiple_of((i * SGU_TOGETHER + k) * BLK, BLK), BLK) for k in range(SGU_TOGETHER)]
            pre = []
            for r in rows:
                u, u_grad = _gelu_and_grad(u_ref[r, :])
                vv, vv_grad = _gelu_and_grad(v_ref[r, :])
                rv = _rstd(vv)
                pre.append((u, u_grad, vv, vv_grad, rv, (vv * rv * gv_ref[...]).astype(BF16)))
            mixed = [jnp.dot(wt, t[5], preferred_element_type=F32) + bias for t in pre]
            mid = []
            for r, t, m in zip(rows, pre, mixed):
                gated = t[0] * m
                dgated, dgo_rows = _norm_bwd(gated, _rstd(gated), go_ref[...], ds_ref[r, :])
                dmixed = dgated * t[0]
                duv_ref[0, r, :] = (dgated * m * t[1]).astype(BF16)
                dgo = dgo + jnp.sum(dgo_rows, axis=0, keepdims=True)
                db = db + jnp.sum(dmixed, axis=1, keepdims=True)
                mid.append(dmixed.astype(BF16))
            dvss = [_dot_tn(wt, dmb) for dmb in mid]
            for dmb, t in zip(mid, pre):
                dw = dw + _dot_nt(dmb, t[5])
            for r, t, dvs in zip(rows, pre, dvss):
                dvv, dgv_rows = _norm_bwd(t[2], t[4], gv_ref[...], dvs)
                duv_ref[1, r, :] = (dvv * t[3]).astype(BF16)
                dgv = dgv + jnp.sum(dgv_rows, axis=0, keepdims=True)
            return dw, db, dgv, dgo

        assert nb % SGU_TOGETHER == 0
        row0 = jnp.zeros((1, BLK), F32)
        dw, db, dgv, dgo = lax.fori_loop(0, nb // SGU_TOGETHER, chunks,
                                         (jnp.zeros((BLK, BLK), F32), jnp.zeros((BLK, 1), F32), row0, row0))
        dw_ref[...] = jnp.where(lower, dw, 0.0)
        db_ref[...] = db
        dgv_ref[...] = dgv
        dgo_ref[...] = dgo

    def col(off):
        return pl.BlockSpec((s, BLK), lambda g: (0, off + g))

    per_group = pl.BlockSpec((None, 1, BLK), lambda g: (g, 0, 0))
    square = pl.BlockSpec((None, BLK, BLK), lambda g: (g, 0, 0))
    column = pl.BlockSpec((None, BLK, 1), lambda g: (g, 0, 0))
    gain = jax.ShapeDtypeStruct((n_groups, 1, BLK), F32)
    return pl.pallas_call(
        body, name=name, grid=(n_groups,),
        in_specs=[col(3 * n_heads), col(3 * n_heads + n_groups), col(n_heads),
                  pl.BlockSpec((None, None, BLK, BLK), lambda g: (layer, g, 0, 0)),
                  pl.BlockSpec((None, None, BLK, 1), lambda g: (layer, g, 0, 0)),
                  pl.BlockSpec((None, None, 1, BLK), lambda g: (layer, g, 0, 0)),
                  pl.BlockSpec((None, None, 1, BLK), lambda g: (layer, g, 0, 0))],
        out_specs=[pl.BlockSpec((2, s, BLK), lambda g: (0, 0, g)), square, column, per_group, per_group],
        out_shape=[jax.ShapeDtypeStruct((2, s, n_groups * BLK), BF16), jax.ShapeDtypeStruct((n_groups, BLK, BLK), F32),
                   jax.ShapeDtypeStruct((n_groups, BLK, 1), F32), gain, gain],
        compiler_params=_params(("parallel",), VMEM_BIG),
    )(p, p, dmix, w, b.reshape(-1, n_groups, BLK, 1), gv.reshape(-1, n_groups, 1, BLK), gout.reshape(-1, n_groups, 1, BLK))


CONV_ROWS = 256
HALO = 8


def _shift_down(ref, r0, n, first):
    cur = ref[pl.ds(r0, n), :]
    prev = jnp.zeros((HALO, cur.shape[1]), F32) if first else ref[pl.ds(r0 - HALO, HALO), :]
    ext = jnp.concatenate([prev, cur], axis=0)
    return pltpu.roll(ext, 1, 0)[HALO:], pltpu.roll(ext, 2, 0)[HALO:], cur


def _shift_up(ref, r0, n, last):
    cur = ref[pl.ds(r0, n), :]
    nxt = jnp.zeros((HALO, cur.shape[1]), F32) if last else ref[pl.ds(r0 + n, HALO), :]
    ext = jnp.concatenate([cur, nxt], axis=0)
    return cur, pltpu.roll(ext, n + HALO - 1, 0)[:n], pltpu.roll(ext, n + HALO - 2, 0)[:n]


def _conv_rows(x1, x2, x0, w_ref, b_ref):
    return ((b_ref[...] + x2 * w_ref[0:1, :]) + x1 * w_ref[1:2, :]) + x0 * w_ref[2:3, :]


def _conv_specs(s, f, tc, layer):
    nf = f // tc
    gate = pl.BlockSpec((s, tc), lambda n: (0, n))
    wg = pl.BlockSpec((None, 3, tc), lambda n: (layer, 0, n))
    wv = pl.BlockSpec((None, 3, tc), lambda n: (layer, 0, nf + n))
    bg = pl.BlockSpec((None, 1, tc), lambda n: (layer, 0, n))
    bv = pl.BlockSpec((None, 1, tc), lambda n: (layer, 0, nf + n))
    return nf, gate, wg, wv, bg, bv


def _up_conv_fwd(h, w_up, cw, cb, layer, *, name):
    s, d = h.shape
    chips, _, per_chip = w_up.shape
    f = chips * per_chip // 2
    tc = _pick(math.gcd(f, per_chip), (256, 128))
    cr = min(CONV_ROWS, s)
    nf, gate, wg, wv, bg, bv = _conv_specs(s, f, tc, layer)
    per = per_chip // tc

    def body(h_ref, mg_ref, mv_ref, wg_ref, wv_ref, bg_ref, bv_ref, ug_ref, uv_ref, out_ref):
        hb = h_ref[...]
        ug_ref[...] = jnp.dot(hb, mg_ref[...], preferred_element_type=F32)
        uv_ref[...] = jnp.dot(hb, mv_ref[...], preferred_element_type=F32)
        for r0 in range(0, s, cr):
            gc = _conv_rows(*_shift_down(ug_ref, r0, cr, r0 == 0), wg_ref, bg_ref)
            vc = _conv_rows(*_shift_down(uv_ref, r0, cr, r0 == 0), wv_ref, bv_ref)
            out_ref[pl.ds(r0, cr), :] = (gc * jax.nn.sigmoid(gc) * vc).astype(out_ref.dtype)

    def cols(first):
        return pl.BlockSpec((None, d, tc), lambda n: ((first + n) // per, 0, (first + n) % per))

    half = jax.ShapeDtypeStruct((s, f), F32)
    return pl.pallas_call(
        body, name=name, grid=(nf,), in_specs=[pl.BlockSpec((s, d), lambda n: (0, 0)), cols(0), cols(nf), wg, wv, bg, bv],
        out_specs=[gate, gate, gate], out_shape=[half, half, jax.ShapeDtypeStruct((s, f), BF16)],
        compiler_params=_params(("arbitrary",), VMEM_BIG),
    )(h, w_up, w_up, cw, cw, cb.reshape(-1, 1, 2 * f), cb.reshape(-1, 1, 2 * f))


def _conv_bwd(up_gate, up_val, dy, w_down, cw, cb, layer, *, name):
    s, f = up_gate.shape
    d = dy.shape[1]
    f2 = 2 * f
    tc = _pick(f, (256, 128))
    cr = min(CONV_ROWS, s)
    nf, gate, wg, wv, bg, bv = _conv_specs(s, f, tc, layer)

    def body(g_ref, v_ref, dy_ref, wd_first, wd_next, wg_ref, wv_ref, bg_ref, bv_ref, dup_ref, dw_ref, db_ref,
             dgc, dvc, da_ref, da_next):
        @pl.when(pl.program_id(0) == 0)
        def _():
            da_ref[...] = _dot_nt(dy_ref[...], wd_first[...])

        da_next[...] = _dot_nt(dy_ref[...], wd_next[...])
        zero = jnp.zeros((1, tc), F32)
        sums = [[zero] * 4, [zero] * 4]
        for r0 in range(0, s, cr):
            rows = pl.ds(r0, cr)
            gx = _shift_down(g_ref, r0, cr, r0 == 0)
            vx = _shift_down(v_ref, r0, cr, r0 == 0)
            gc = _conv_rows(*gx, wg_ref, bg_ref)
            vc = _conv_rows(*vx, wv_ref, bv_ref)
            sig = jax.nn.sigmoid(gc)
            da = da_ref[rows, :]
            d_gate = da * vc * (sig * (1.0 + gc * (1.0 - sig)))
            d_val = da * (gc * sig)
            dgc[rows, :] = d_gate
            dvc[rows, :] = d_val
            for part, (dc, (x1, x2, x0)) in enumerate(((d_gate, gx), (d_val, vx))):
                for tap, xs in enumerate((x2, x1, x0)):
                    sums[part][tap] = sums[part][tap] + jnp.sum(dc * xs, axis=0, keepdims=True)
                sums[part][3] = sums[part][3] + jnp.sum(dc, axis=0, keepdims=True)
        dw_ref[...] = jnp.zeros_like(dw_ref)
        db_ref[...] = jnp.zeros_like(db_ref)
        for part, (dc_ref, w_ref) in enumerate(((dgc, wg_ref), (dvc, wv_ref))):
            for tap in range(3):
                dw_ref[part, tap:tap + 1, :] = sums[part][tap]
            db_ref[part, 0:1, :] = sums[part][3]
            for r0 in range(0, s, cr):
                d0, d1, d2 = _shift_up(dc_ref, r0, cr, r0 + cr == s)
                dup_ref[part, pl.ds(r0, cr), :] = ((d0 * w_ref[2:3, :] + d1 * w_ref[1:2, :]) + d2 * w_ref[0:1, :]).astype(BF16)
        da_ref[...] = da_next[...]

    small = pl.BlockSpec((2, 8, tc), lambda n: (0, 0, n))
    return pl.pallas_call(
        body, name=name, grid=(nf,),
        in_specs=[gate, gate, pl.BlockSpec((s, d), lambda n: (0, 0)), pl.BlockSpec((tc, d), lambda n: (0, 0)),
                  pl.BlockSpec((tc, d), lambda n: (jnp.minimum(n + 1, nf - 1), 0)), wg, wv, bg, bv],
        out_specs=[pl.BlockSpec((2, s, tc), lambda n: (0, 0, n)), small, small],
        out_shape=[jax.ShapeDtypeStruct((2, s, f), BF16), jax.ShapeDtypeStruct((2, 8, f), F32),
                   jax.ShapeDtypeStruct((2, 8, f), F32)],
        scratch_shapes=[pltpu.VMEM((s, tc), F32)] * 4,
        compiler_params=_params(("arbitrary",), VMEM_BIG),
    )(up_gate, up_val, dy, w_down, w_down, cw, cw, cb.reshape(-1, 1, f2), cb.reshape(-1, 1, f2))


def _adamw(w, g, m, v, *, name, dep=None, copy_g=False):
    shape = w.shape
    cols = shape[-1]
    rows = w.size // cols
    if rows * cols * 4 <= (2 << 20):
        tr = rows
    else:
        tr = next(t for t in (1024, 512, 256, 128, 64, 32, 16, 8) if rows % t == 0 and (t * cols * 4 <= (2 << 20) or t == 8))

    n_out = 4 if copy_g else 3

    def body(w_ref, g_ref, m_ref, v_ref, *rest):
        d_ref, nm_ref, nv_ref = rest[-n_out:][:3]
        gr = g_ref[...]
        if copy_g:
            rest[-1][...] = gr
        nm = ADAM_B1 * m_ref[...] + (1.0 - ADAM_B1) * gr
        nv = ADAM_B2 * v_ref[...] + (1.0 - ADAM_B2) * (gr * gr)
        m_hat = nm / (1.0 - ADAM_B1 ** ADAM_STEP)
        v_hat = nv / (1.0 - ADAM_B2 ** ADAM_STEP)
        d_ref[...] = -ADAM_LR * (m_hat / (jnp.sqrt(v_hat) + ADAM_EPS) + ADAM_WD * w_ref[...])
        nm_ref[...] = nm
        nv_ref[...] = nv

    blk = pl.BlockSpec((tr, cols), lambda i: (i, 0))
    out = jax.ShapeDtypeStruct((rows, cols), F32)
    res = pl.pallas_call(
        body, name=name, grid=(rows // tr,), in_specs=[blk] * 4 + ([] if dep is None else [ANY]), out_specs=[blk] * n_out,
        out_shape=[out] * n_out, compiler_params=_params(("parallel",), VMEM_BIG),
    )(*[t.reshape(rows, cols) for t in (w, g, m, v)], *([] if dep is None else [dep]))
    return [t.reshape(shape) for t in res]


def _place():
    x, y, c = lax.axis_index("x"), lax.axis_index("y"), lax.axis_index("c")
    others = [(1 - x, y), (x, 1 - y), (1 - x, 1 - y)]
    return x, y, c, others


def _remote(src, dst, send_sem, recv_sem, device):
    return pltpu.make_async_remote_copy(src_ref=src, dst_ref=dst, send_sem=send_sem, recv_sem=recv_sem, device_id=device,
                                        device_id_type=MESH)


def _cast_into(w, layer, place, *, name, dep=None):
    _, r, cols = w.shape
    tr = _row_tile(r, cols)

    def body(place_ref, w_ref, *rest):
        rest[-1][...] = w_ref[...].astype(BF16)

    in_specs, args = [pl.BlockSpec((None, tr, cols), lambda i, pr: (layer, i, 0))], [place, w]
    if dep is not None:
        in_specs.append(ANY)
        args.append(dep)
    grid_spec = pltpu.PrefetchScalarGridSpec(
        num_scalar_prefetch=1, grid=(r // tr,), in_specs=in_specs,
        out_specs=pl.BlockSpec((None, tr, cols), lambda i, pr: (pr[1], i, 0)),
    )
    return pl.pallas_call(
        body, name=name, grid_spec=grid_spec, out_shape=jax.ShapeDtypeStruct((N_CHIPS, r, cols), BF16),
        compiler_params=_params(("parallel",), VMEM_BIG),
    )(*args)


HBM = pl.BlockSpec(memory_space=pltpu.HBM)
SEM = pl.BlockSpec(memory_space=pltpu.SEMAPHORE)
EFFECT = pltpu.SideEffectType.DATAFLOW_SIDE_EFFECTING
TOKEN = jax.ShapeDtypeStruct((8, 128), F32)


def _in_hbm(t):
    return pltpu.with_memory_space_constraint(t, pltpu.HBM)


def _gather_copies(buf, send, recv):
    x, y, c, others = _place()
    half = buf.shape[1] // 2
    rows = pl.ds(c * half, half)
    return [_remote(buf.at[2 * x + y, rows], buf.at[2 * x + y, rows], send.at[j], recv.at[j], (px, py, c))
            for j, (px, py) in enumerate(others)]


def _gather_start(bufs, *, name):
    n = len(bufs)

    def body(*refs):
        ins, sends, recvs, token = refs[:n], refs[n:2 * n], refs[2 * n:3 * n], refs[4 * n]
        for a in range(n):
            for cp in _gather_copies(ins[a], sends[a], recvs[a]):
                cp.start()
        token[...] = jnp.zeros_like(token)

    sems = [pltpu.SemaphoreType.DMA((3,))] * (2 * n)
    res = pl.pallas_call(
        body, name=name, out_shape=sems + [pltpu.HBM(t.shape, t.dtype) for t in bufs] + [TOKEN],
        in_specs=[HBM] * n, out_specs=[SEM] * (2 * n) + [HBM] * n + [pl.BlockSpec(memory_space=pltpu.VMEM)],
        input_output_aliases={a: 2 * n + a for a in range(n)}, compiler_params=pltpu.CompilerParams(has_side_effects=EFFECT),
    )(*[_in_hbm(t) for t in bufs])
    return [(res[2 * n + a], res[a], res[n + a]) for a in range(n)], res[3 * n]


def _gather_wait(state, after, *, name):
    buf, send, recv = state

    def body(buf_ref, send_ref, recv_ref, after_ref, out_ref):
        for cp in _gather_copies(buf_ref, send_ref, recv_ref):
            cp.wait_send()
            cp.wait_recv()

    return pl.pallas_call(
        body, name=name, out_shape=pltpu.HBM(buf.shape, buf.dtype), in_specs=[HBM, SEM, SEM, ANY], out_specs=HBM,
        input_output_aliases={0: 0}, compiler_params=pltpu.CompilerParams(has_side_effects=EFFECT),
    )(buf, send, recv, after)


def _chip_copies(src, land, send, recv):
    _x, _y, c, others = _place()
    return [_remote(src.at[2 * px + py], land.at[j], send.at[j], recv.at[j], (px, py, c)) for j, (px, py) in enumerate(others)]


def _chip_start(partial, *, name):
    def body(src, land, send, recv, _src_thru, _land_thru, token):
        for cp in _chip_copies(src, land, send, recv):
            cp.start()
        token[...] = jnp.zeros_like(token)

    land_shape = (3,) + partial.shape[1:]
    sem = pltpu.SemaphoreType.DMA((3,))
    send, recv, src, land, token = pl.pallas_call(
        body, name=name, out_shape=[sem, sem, pltpu.HBM(partial.shape, partial.dtype), pltpu.HBM(land_shape, partial.dtype), TOKEN],
        in_specs=[HBM, HBM], out_specs=[SEM, SEM, HBM, HBM, pl.BlockSpec(memory_space=pltpu.VMEM)],
        input_output_aliases={0: 2, 1: 3}, compiler_params=pltpu.CompilerParams(has_side_effects=EFFECT),
    )(_in_hbm(partial), _in_hbm(lax.empty(land_shape, partial.dtype)))
    return (src, land, send, recv), token


def _chip_wait(state, after, *, name):
    src, land, send, recv = state

    def body(src_ref, land_ref, send_ref, recv_ref, after_ref, _src_out, _land_out):
        for cp in _chip_copies(src_ref, land_ref, send_ref, recv_ref):
            cp.wait_send()
            cp.wait_recv()

    return pl.pallas_call(
        body, name=name, out_shape=[pltpu.HBM(src.shape, src.dtype), pltpu.HBM(land.shape, land.dtype)],
        in_specs=[HBM, HBM, SEM, SEM, ANY], out_specs=[HBM, HBM], input_output_aliases={0: 0, 1: 1},
        compiler_params=pltpu.CompilerParams(has_side_effects=EFFECT),
    )(src, land, send, recv, after)[1]


def _split_start(bufs, copies, n_copies, *, name):
    n = len(bufs)

    def body(*refs):
        for cp in copies(refs[:n], refs[n], refs[n + 1]):
            cp.start()
        refs[-1][...] = jnp.zeros_like(refs[-1])

    sem = pltpu.SemaphoreType.DMA((n_copies,))
    res = pl.pallas_call(
        body, name=name, out_shape=[sem, sem] + [pltpu.HBM(t.shape, t.dtype) for t in bufs] + [TOKEN],
        in_specs=[HBM] * n, out_specs=[SEM, SEM] + [HBM] * n + [pl.BlockSpec(memory_space=pltpu.VMEM)],
        input_output_aliases={a: 2 + a for a in range(n)}, compiler_params=pltpu.CompilerParams(has_side_effects=EFFECT),
    )(*[_in_hbm(t) for t in bufs])
    return (list(res[2:2 + n]), res[0], res[1]), res[-1]


def _split_wait(state, copies, after, *, name):
    bufs, send, recv = state
    n = len(bufs)

    def body(*refs):
        for cp in copies(refs[:n], refs[n], refs[n + 1]):
            cp.wait_send()
            cp.wait_recv()

    return list(pl.pallas_call(
        body, name=name, out_shape=[pltpu.HBM(t.shape, t.dtype) for t in bufs], in_specs=[HBM] * n + [SEM, SEM, ANY],
        out_specs=[HBM] * n, input_output_aliases={a: a for a in range(n)},
        compiler_params=pltpu.CompilerParams(has_side_effects=EFFECT),
    )(*bufs, send, recv, after))


def _hand_over_copies(refs, send, recv):
    x, y, c, others = _place()
    half = refs[0].shape[1] // 2
    got = [refs[0].at[2 * px + py, pl.ds(c * half, half)] for px, py in others]
    return [_remote(got[j], got[j], send.at[j], recv.at[j], (x, y, 1 - c)) for j in range(3)]


def _pair_copies(refs, send, recv):
    x, y, c, _o = _place()
    half = refs[0].shape[1] // 2
    return [_remote(refs[0].at[:, pl.ds((1 - c) * half, half), :], refs[1], send.at[0], recv.at[0], (x, y, 1 - c))]


def _share_copies(refs, send, recv):
    x, y, c, _o = _place()
    return [_remote(refs[0].at[:, c], refs[0].at[:, c], send.at[0], recv.at[0], (x, y, 1 - c))]


def _small_copies(refs, send, recv):
    x, y, c, others = _place()
    peers = [(x, y, 1 - c)] + [(px, py, pc) for px, py in others for pc in (c, 1 - c)]
    slot = refs[1].at[4 * x + 2 * y + c]
    return [_remote(refs[0], slot, send.at[k], recv.at[k], peer) for k, peer in enumerate(peers)]


def _row_tile(rows, cols):
    return max(t for t in range(16, rows + 1, 16) if rows % t == 0 and (t * cols * 4 <= (4 << 20) or t == 16))


def _pair_sum(grad, theirs, place, *, name):
    _, r, cols = grad.shape
    r2 = r // 2
    tr = _row_tile(r2, cols)
    nr = r2 // tr

    def body(place_ref, g_ref, t_ref, all_ref):
        all_ref[...] = (g_ref[...].astype(F32) + t_ref[...].astype(F32)).astype(all_ref.dtype)

    def other(k, pr):
        return k + (k >= pr[1]).astype(jnp.int32)

    grid_spec = pltpu.PrefetchScalarGridSpec(
        num_scalar_prefetch=1, grid=(N_CHIPS - 1, nr),
        in_specs=[pl.BlockSpec((None, tr, cols), lambda k, i, pr: (other(k, pr), pr[0] * nr + i, 0)),
                  pl.BlockSpec((None, tr, cols), lambda k, i, pr: (other(k, pr), i, 0))],
        out_specs=pl.BlockSpec((None, tr, cols), lambda k, i, pr: (other(k, pr), i, 0)),
    )
    return pl.pallas_call(
        body, name=name, grid_spec=grid_spec, out_shape=jax.ShapeDtypeStruct((N_CHIPS, r2, cols), BF16),
        compiler_params=_params(("parallel", "parallel"), VMEM_BIG),
    )(place, grad, theirs)


def _chip_sum(grad, theirs, got, place, buf, layer, depth, *, name):
    _, r, cols = grad.shape
    r2 = r // 2
    tr = _row_tile(r2, cols)
    nr = r2 // tr

    def body(place_ref, g_ref, t_ref, got_ref, *rest):
        own = g_ref[...].astype(F32) + t_ref[...].astype(F32)
        rest[-1][...] = ((own + got_ref[0].astype(F32)) + got_ref[1].astype(F32)) + got_ref[2].astype(F32)

    in_specs = [pl.BlockSpec((None, tr, cols), lambda i, pr: (pr[1], pr[0] * nr + i, 0)),
                pl.BlockSpec((None, tr, cols), lambda i, pr: (pr[1], i, 0)),
                pl.BlockSpec((3, tr, cols), lambda i, pr: (0, i, 0))]
    args = [place, grad, theirs, got]
    if buf is not None:
        in_specs.append(ANY)
        args.append(buf)
    grid_spec = pltpu.PrefetchScalarGridSpec(
        num_scalar_prefetch=1, grid=(nr,), in_specs=in_specs,
        out_specs=pl.BlockSpec((None, None, tr, cols), lambda i, pr: (layer, pr[0], i, 0)),
    )
    return pl.pallas_call(
        body, name=name, grid_spec=grid_spec, out_shape=jax.ShapeDtypeStruct((depth, 2, r2, cols), F32),
        input_output_aliases={} if buf is None else {4: 0}, compiler_params=_params(("parallel",), VMEM_BIG),
    )(*args)


def _sum_devices(parts, own, place, *, name):
    _, rows, cols = parts.shape
    tr = rows if N_DEV * rows * cols * 4 <= (16 << 20) else _pick(rows, (256, 128, 64, 32, 16, 8))

    def body(place_ref, p_ref, own_ref, out_ref):
        me = 2 * place_ref[1] + place_ref[0]
        acc = None
        for dev in range(N_DEV):
            term = jnp.where(me == dev, own_ref[...], p_ref[dev])
            acc = term if acc is None else acc + term
        out_ref[...] = acc

    grid_spec = pltpu.PrefetchScalarGridSpec(
        num_scalar_prefetch=1, grid=(rows // tr,),
        in_specs=[pl.BlockSpec((N_DEV, tr, cols), lambda i, pr: (0, i, 0)), pl.BlockSpec((tr, cols), lambda i, pr: (i, 0))],
        out_specs=pl.BlockSpec((tr, cols), lambda i, pr: (i, 0)),
    )
    return pl.pallas_call(
        body, name=name, grid_spec=grid_spec, out_shape=jax.ShapeDtypeStruct((rows, cols), F32),
        compiler_params=_params(("parallel",), VMEM_BIG),
    )(place, parts, own)


def _pack(parts):
    rows = []
    for t in parts:
        flat = t.reshape(-1, 128)
        pad = (-flat.shape[0]) % 8
        rows.append(jnp.pad(flat, ((0, pad), (0, 0))) if pad else flat)
    return jnp.concatenate(rows, axis=0)


def _unpack(pack, shapes):
    out, r0 = [], 0
    for shp in shapes:
        n = math.prod(shp) // 128
        out.append(pack[r0:r0 + n].reshape(shp))
        r0 += n + (-n) % 8
    return out


SMALL = ["attn_norm_g", "q_norm_g", "k_norm_g", "sgu_norm_g", "sgu_w", "sgu_b", "out_norm_a_g", "out_norm_b_g",
         "ffn_norm_g", "conv_b"]
BIG = ["w_in", "w_out", "w_up", "w_down"]
ORDER = ["attn_norm_g", "w_in", "q_norm_g", "k_norm_g", "sgu_norm_g", "sgu_w", "sgu_b", "out_norm_a_g", "out_norm_b_g",
         "w_out", "ffn_norm_g", "w_up", "conv_w", "conv_b", "w_down"]


def kernel(x, attn_norm_g, w_in, q_norm_g, k_norm_g, sgu_norm_g, sgu_w, sgu_b, out_norm_a_g, out_norm_b_g, w_out, ffn_norm_g, w_up, conv_w, conv_b, w_down, loss_target, m_attn_norm_g, m_w_in, m_q_norm_g, m_k_norm_g, m_sgu_norm_g, m_sgu_w, m_sgu_b, m_out_norm_a_g, m_out_norm_b_g, m_w_out, m_ffn_norm_g, m_w_up, m_conv_w, m_conv_b, m_w_down, v_attn_norm_g, v_w_in, v_q_norm_g, v_k_norm_g, v_sgu_norm_g, v_sgu_w, v_sgu_b, v_out_norm_a_g, v_out_norm_b_g, v_w_out, v_ffn_norm_g, v_w_up, v_conv_w, v_conv_b, v_w_down):
    W = dict(attn_norm_g=attn_norm_g, w_in=w_in, q_norm_g=q_norm_g, k_norm_g=k_norm_g, sgu_norm_g=sgu_norm_g, sgu_w=sgu_w,
             sgu_b=sgu_b, out_norm_a_g=out_norm_a_g, out_norm_b_g=out_norm_b_g, w_out=w_out, ffn_norm_g=ffn_norm_g, w_up=w_up,
             conv_w=conv_w, conv_b=conv_b, w_down=w_down)
    M = dict(attn_norm_g=m_attn_norm_g, w_in=m_w_in, q_norm_g=m_q_norm_g, k_norm_g=m_k_norm_g, sgu_norm_g=m_sgu_norm_g,
             sgu_w=m_sgu_w, sgu_b=m_sgu_b, out_norm_a_g=m_out_norm_a_g, out_norm_b_g=m_out_norm_b_g, w_out=m_w_out,
             ffn_norm_g=m_ffn_norm_g, w_up=m_w_up, conv_w=m_conv_w, conv_b=m_conv_b, w_down=m_w_down)
    V = dict(attn_norm_g=v_attn_norm_g, w_in=v_w_in, q_norm_g=v_q_norm_g, k_norm_g=v_k_norm_g, sgu_norm_g=v_sgu_norm_g,
             sgu_w=v_sgu_w, sgu_b=v_sgu_b, out_norm_a_g=v_out_norm_a_g, out_norm_b_g=v_out_norm_b_g, w_out=v_w_out,
             ffn_norm_g=v_ffn_norm_g, w_up=v_w_up, conv_w=v_conv_w, conv_b=v_conv_b, w_down=v_w_down)
    depth = w_in.shape[0]
    s, d = x.shape[1], x.shape[2]
    n_heads = out_norm_a_g.shape[1]
    core = lax.axis_index("c")
    chip = 2 * lax.axis_index("x") + lax.axis_index("y")
    place = jnp.stack([core, chip]).astype(jnp.int32)
    xs = x.reshape(s, d)

    f_local = conv_w.shape[2]
    taps = lax.dynamic_update_slice(jnp.zeros((N_CHIPS, 16, f_local), F32), conv_w.reshape(1, depth * 3, f_local),
                                    (chip, 0, 0))
    order = [(l, n) for l in range(depth) for n in BIG]
    first, token = _gather_start([_cast_into(w_in, 0, place, name="cast_w_in"), taps], name="gather_start_first")
    rest, token = _gather_start([_cast_into(W[n], l, place, dep=token, name=f"cast_{n}") for l, n in order[1:]],
                                name="gather_start_rest")
    states = dict(zip(order, [first[0]] + rest))

    states["taps"] = first[1]

    def landed(key, after, tag):
        buf = _gather_wait(states[key], after, name=f"gather_wait_{tag}")
        return _split_start([buf], _hand_over_copies, 3, name=f"hand_over_{tag}")

    def whole(state, after, tag):
        return _split_wait(state, _hand_over_copies, after, name=f"hand_over_wait_{tag}")[0]

    saved, full = [], []
    cur = xs
    for l in range(depth):
        gain = attn_norm_g + token[0, 0] if l == 0 else attn_norm_g
        h = _rmsnorm_fwd(cur, gain, l, name="attn_norm")
        if l == 0:
            ho_in, _ = landed((0, "w_in"), h, "w_in")
        w_in_l = whole(ho_in, h, "w_in")
        if l == 0:
            ho_taps, token = landed("taps", w_in_l, "taps")
        p = _mm(h, w_in_l, "nn", b_split=N_CHIPS, caps=(2048, 256, 2048), dep=token, name="proj_in")
        if l == 0:
            taps = whole(ho_taps, p, "taps")[:, :depth * 3].reshape(N_CHIPS, depth, 3, f_local)
            cw_full = jnp.transpose(taps, (1, 2, 0, 3)).reshape(depth, 3, N_CHIPS * f_local)
        mix, o_raw, lsum = _attn_fwd(p, q_norm_g, k_norm_g, out_norm_a_g, l, n_heads, name="attn_fwd")
        ho_out, token = landed((l, "w_out"), mix, "w_out")
        mix = _sgu_fwd(p, mix, sgu_w, sgu_b, sgu_norm_g + token[0, 0], out_norm_b_g, l, n_heads, name="sgu_fwd")
        w_out_l = whole(ho_out, mix, "w_out").reshape(-1, d)
        ho_up, token = landed((l, "w_up"), w_out_l, "w_up")
        x1 = _mm(mix, w_out_l, "nn", a_split=2, res=cur, caps=(2048, 512, 1024), dep=token, name="proj_out")
        h2 = _rmsnorm_fwd(x1, ffn_norm_g, l, name="ffn_norm")
        w_up_l = whole(ho_up, h2, "w_up")
        if l > 0:
            ho_down, token = landed((l, "w_down"), w_up_l, "w_down")
        up_gate, up_val, act = _up_conv_fwd(h2, w_up_l, cw_full, conv_b + token[0, 0], l, name="ffn_up_conv")
        if l == 0:
            ho_down, token = landed((l, "w_down"), act, "w_down")
        w_down_l = whole(ho_down, act, "w_down").reshape(-1, d)
        if l + 1 < depth:
            ho_in, token = landed((l + 1, "w_in"), w_down_l, "w_in")
        x2 = _mm(act, w_down_l, "nn", res=x1, caps=(1024, 512, 2816), dep=token, name="ffn_down")
        full.append(dict(w_in=w_in_l, w_out=w_out_l, w_up=w_up_l, w_down=w_down_l))
        saved.append(dict(x0=cur, h=h, p=p, o_raw=o_raw, lsum=lsum, mix=mix, x1=x1, h2=h2, up_gate=up_gate, up_val=up_val,
                          act=act))
        cur = x2

    dx, dxb, sq = _loss_head(cur, loss_target.reshape(s, d), name="loss_head")
    loss = lax.psum(sq[0, 0] * (0.5 / d), ("x", "y", "c"))

    small_grads = {n: [None] * depth for n in SMALL + ["conv_w"]}
    def pair_begin(n, grad):
        land = lax.empty((N_CHIPS, grad.shape[1] // 2, grad.shape[2]), grad.dtype)
        return _split_start([grad, land], _pair_copies, 1, name=f"pair_start_{n}")

    def chip_begin(n, state, after):
        grad, theirs = _split_wait(state, _pair_copies, after, name=f"pair_wait_{n}")
        state, tok = _chip_start(_pair_sum(grad, theirs, place, name=f"pair_sum_{n}"), name=f"chip_start_{n}")
        return (grad, theirs, state), tok

    pending = {}
    for l in reversed(range(depth)):
        fw, sv = full[l], saved[l]
        g_down = _mm(sv["act"], dxb, "tn", caps=(512, 2048, 2048), out_dtype=BF16, name="g_down")
        pair, tok = pair_begin("w_down", g_down.reshape(N_CHIPS, -1, d))
        dup, dcw, dcb = _conv_bwd(sv["up_gate"], sv["up_val"], dxb, fw["w_down"], cw_full, conv_b + tok[0, 0], l,
                                  name="conv_bwd")
        pending[(l, "w_down")], tok = chip_begin("w_down", pair, dup)
        g_up = _mm(sv["h2"], dup, "tn", b_split=2, o_split=N_CHIPS, caps=(2048, 256, 2048), out_dtype=BF16, dep=tok,
                   name="g_up")
        pair, tok = pair_begin("w_up", g_up)
        dh2 = _mm(dup, fw["w_up"], "nt", a_split=2, b_split=N_CHIPS, caps=(2048, 512, 2816), dep=tok, name="d_h2")
        pending[(l, "w_up")], tok = chip_begin("w_up", pair, dh2)
        dx1, dx1b, dg_ffn = _rmsnorm_bwd(sv["x1"], ffn_norm_g + tok[0, 0], l, dh2, dx, name="ffn_norm_bwd")
        dmix = _mm(dx1b, fw["w_out"], "nt", caps=(2048, 512, 2048), name="d_mix")
        g_out = _mm(sv["mix"], dx1b, "tn", a_split=2, caps=(512, 2048, 2048), out_dtype=BF16, name="g_out")
        pair, tok = pair_begin("w_out", g_out.reshape(N_CHIPS, -1, d))
        dqkv, dgq, dgk, dgoa = _attn_bwd(sv["p"], sv["o_raw"], sv["lsum"], dmix, q_norm_g + tok[0, 0], k_norm_g,
                                         out_norm_a_g, l, n_heads, name="attn_bwd")
        pending[(l, "w_out")], tok = chip_begin("w_out", pair, dqkv)
        duv, dsw, dsb, dgv, dgob = _sgu_bwd(sv["p"], dmix, sgu_w, sgu_b, sgu_norm_g + tok[0, 0], out_norm_b_g, l,
                                            n_heads, name="sgu_bwd")
        dp = jnp.concatenate([dqkv[0], dqkv[1], dqkv[2], duv[0], duv[1]], axis=1)
        g_in = _mm(sv["h"], dp, "tn", o_split=N_CHIPS, caps=(2048, 256, 2048), out_dtype=BF16, name="g_in")
        pair, tok = pair_begin("w_in", g_in)
        dh = _mm(dp, fw["w_in"], "nt", b_split=N_CHIPS, caps=(2048, 512, 1280), dep=tok, name="d_h")
        pending[(l, "w_in")], tok = chip_begin("w_in", pair, dh)
        dx, dxb, dg_attn = _rmsnorm_bwd(sv["x0"], attn_norm_g + tok[0, 0], l, dh, dx1, name="attn_norm_bwd")

        small_grads["attn_norm_g"][l] = dg_attn.reshape(d)
        small_grads["q_norm_g"][l] = jnp.sum(dgq, axis=(0, 1))
        small_grads["k_norm_g"][l] = jnp.sum(dgk, axis=(0, 1))
        small_grads["sgu_norm_g"][l] = dgv.reshape(-1, BLK)
        small_grads["sgu_w"][l] = dsw
        small_grads["sgu_b"][l] = dsb.reshape(-1, BLK)
        small_grads["out_norm_a_g"][l] = dgoa.reshape(-1, BLK)
        small_grads["out_norm_b_g"][l] = dgob.reshape(-1, BLK)
        small_grads["ffn_norm_g"][l] = dg_ffn.reshape(d)
        small_grads["conv_b"][l] = dcb[:, 0, :].reshape(-1)
        small_grads["conv_w"][l] = jnp.transpose(dcw[:, :3, :], (1, 0, 2)).reshape(3, -1)

    names = SMALL + ["conv_w"]
    pack = _pack([jnp.stack(small_grads[n]) for n in names])
    small, tok = _split_start([pack, lax.empty((N_DEV,) + pack.shape, F32)], _small_copies, N_DEV - 1, name="small_start")
    G, D_, NM, NV = {}, {}, {}, {}
    after, prev = tok, None
    for n in ("w_down", "w_up", "w_out", "w_in"):
        buf = None
        for l in reversed(range(depth)):
            grad, theirs, state = pending[(l, n)]
            got = _chip_wait(state, after, name=f"chip_wait_{n}")
            buf = _chip_sum(grad, theirs, got, place, buf, l, depth, name=f"chip_sum_{n}")
            after = buf
        share, tok = _split_start([buf], _share_copies, 1, name=f"share_start_{n}")
        if prev is not None:
            D_[prev], NM[prev], NV[prev], G[prev] = _adamw(W[prev], G[prev], M[prev], V[prev], dep=tok, copy_g=True,
                                                           name=f"adamw_{prev}")
            after = NV[prev]
        G[n] = _split_wait(share, _share_copies, after, name=f"share_wait_{n}")[0].reshape(W[n].shape)
        after, prev = G[n], n
    D_[prev], NM[prev], NV[prev], G[prev] = _adamw(W[prev], G[prev], M[prev], V[prev], copy_g=True, name=f"adamw_{prev}")

    pack, parts = _split_wait(small, _small_copies, NV[prev], name="small_wait")
    total = _sum_devices(parts, pack, place, name="sum_small")
    f_full = conv_b.shape[1]
    shapes = [W[n].shape for n in SMALL] + [(depth, 3, f_full)]
    for n, t in zip(names, _unpack(total, shapes)):
        G[n] = t
    G["conv_w"] = lax.dynamic_slice_in_dim(G["conv_w"], chip * f_local, f_local, axis=2)

    D_["conv_w"], NM["conv_w"], NV["conv_w"] = _adamw(conv_w, G["conv_w"], m_conv_w, v_conv_w, name="adamw_conv_w")
    small_shapes = [W[n].shape for n in SMALL]
    res = _adamw(_pack([W[n] for n in SMALL]), _pack([G[n] for n in SMALL]), _pack([M[n] for n in SMALL]),
                 _pack([V[n] for n in SMALL]), name="adamw_small")
    for dst, t in zip((D_, NM, NV), res):
        for n, u in zip(SMALL, _unpack(t, small_shapes)):
            dst[n] = u

    return (loss, dx.reshape(x.shape), *[G[n] for n in ORDER], *[D_[n] for n in ORDER], *[NM[n] for n in ORDER],
            *[NV[n] for n in ORDER])
```

```python
import functools
import math

import jax
import jax.numpy as jnp
from jax import lax
from jax.experimental import pallas as pl
from jax.experimental.pallas import tpu as pltpu

F32 = jnp.float32
BF16 = jnp.bfloat16
EPS = 1e-6
BLK = 128
N_CHIPS = 4
N_DEV = 8
ADAM_LR, ADAM_B1, ADAM_B2, ADAM_EPS, ADAM_WD, ADAM_STEP = 0.001, 0.9, 0.999, 1e-08, 0.01, 10
VMEM_BIG = 48 * 1024 * 1024
MESH = pl.DeviceIdType.MESH
ANY = pl.BlockSpec(memory_space=pl.ANY)


def _pick(dim, prefs):
    for t in prefs:
        if dim % t == 0:
            return t
    raise ValueError(f"no tile in {prefs} divides {dim}")


def _params(sem=None, vmem=None):
    return pltpu.CompilerParams(dimension_semantics=sem, vmem_limit_bytes=vmem)


def _ldims(arr, split):
    if split == 1:
        return arr.shape
    p, r, cs = arr.shape
    assert p == split
    return (r, p * cs)


def _spec(tr, tc, split, cols, rc):
    if split == 1:
        return pl.BlockSpec((tr, tc), lambda i, j, k: rc(i, j, k))
    per = (cols // split) // tc

    def imap(i, j, k):
        r, c = rc(i, j, k)
        return (c // per, r, c % per)

    return pl.BlockSpec((None, tr, tc), imap)


def _fit(unit, cap):
    return max(t for t in range(128, min(unit, cap) + 1, 128) if unit % t == 0)


def _mm(a, b, mode, *, name, caps, a_split=1, b_split=1, o_split=1, out_dtype=F32, res=None, dep=None):
    ar, ac = _ldims(a, a_split)
    br, bc = _ldims(b, b_split)
    if mode == "nn":
        m, k, n = ar, ac, bc
        assert br == k
        ku, nu, mu = math.gcd(k // a_split, k), math.gcd(n // b_split, n // o_split), m
    elif mode == "nt":
        m, k, n = ar, ac, br
        assert bc == k
        ku, nu, mu = math.gcd(k // a_split, k // b_split), n // o_split, m
    else:
        k, m, n = ar, ac, bc
        assert br == k
        ku, nu, mu = k, math.gcd(n // b_split, n // o_split), m // a_split
    tm, tn, tk = _fit(mu, caps[0]), _fit(nu, caps[1]), _fit(ku, caps[2])
    nk = k // tk
    if mode == "nn":
        a_spec = _spec(tm, tk, a_split, k, lambda i, j, kk: (i, kk))
        b_spec = _spec(tk, tn, b_split, n, lambda i, j, kk: (kk, j))
    elif mode == "nt":
        a_spec = _spec(tm, tk, a_split, k, lambda i, j, kk: (i, kk))
        b_spec = _spec(tn, tk, b_split, k, lambda i, j, kk: (j, kk))
    else:
        a_spec = _spec(tk, tm, a_split, m, lambda i, j, kk: (kk, i))
        b_spec = _spec(tk, tn, b_split, n, lambda i, j, kk: (kk, j))
    o_spec = _spec(tm, tn, o_split, n, lambda i, j, kk: (i, j))
    dims = {"nn": (((1,), (0,)), ((), ())), "nt": (((1,), (1,)), ((), ())), "tn": (((0,), (0,)), ((), ()))}[mode]

    def body(a_ref, b_ref, *rest):
        if dep is not None:
            rest = rest[1:]
        if res is None:
            o_ref, acc = rest
        else:
            r_ref, o_ref, acc = rest
        kk = pl.program_id(2)

        @pl.when(kk == 0)
        def _():
            acc[...] = jnp.zeros_like(acc)

        acc[...] += lax.dot_general(a_ref[...].astype(BF16), b_ref[...].astype(BF16), dims, preferred_element_type=F32)

        @pl.when(kk == nk - 1)
        def _():
            out = acc[...]
            if res is not None:
                out = out + r_ref[...]
            o_ref[...] = out.astype(o_ref.dtype)

    in_specs, args = [a_spec, b_spec], [a, b]
    if dep is not None:
        in_specs.append(ANY)
        args.append(dep)
    if res is not None:
        in_specs.append(pl.BlockSpec((tm, tn), lambda i, j, kk: (i, j)))
        args.append(res)
    out_shape = (m, n) if o_split == 1 else (o_split, m, n // o_split)
    return pl.pallas_call(
        body, name=name, grid=(m // tm, n // tn, nk), in_specs=in_specs, out_specs=o_spec,
        out_shape=jax.ShapeDtypeStruct(out_shape, out_dtype), scratch_shapes=[pltpu.VMEM((tm, tn), F32)],
        compiler_params=_params(("parallel", "parallel", "arbitrary"), VMEM_BIG),
    )(*args)


def _rstd(v):
    return lax.rsqrt(jnp.mean(v * v, axis=-1, keepdims=True) + EPS)


def _norm_bwd(v, r, gain, dout):
    a = dout * gain
    dv = r * (a - v * (r * r * jnp.mean(a * v, axis=-1, keepdims=True)))
    return dv, dout * v * r


def _rmsnorm_fwd(x, g, layer, *, name):
    s, d = x.shape
    tr = _pick(s, (256, 128))

    def body(x_ref, g_ref, o_ref):
        v = x_ref[...]
        o_ref[...] = (v * _rstd(v) * g_ref[...]).astype(o_ref.dtype)

    return pl.pallas_call(
        body, name=name, grid=(s // tr,),
        in_specs=[pl.BlockSpec((tr, d), lambda i: (i, 0)), pl.BlockSpec((None, 1, d), lambda i: (layer, 0, 0))],
        out_specs=pl.BlockSpec((tr, d), lambda i: (i, 0)), out_shape=jax.ShapeDtypeStruct((s, d), BF16),
        compiler_params=_params(("parallel",)),
    )(x, g.reshape(-1, 1, d))


def _rmsnorm_bwd(x, g, layer, dh, dres, *, name):
    s, d = x.shape
    tr = _pick(s, (256, 128))

    def body(x_ref, g_ref, dh_ref, dres_ref, dx_ref, dxb_ref, dg_ref):
        v = x_ref[...]
        dv, dgr = _norm_bwd(v, _rstd(v), g_ref[...], dh_ref[...])
        dx = dres_ref[...] + dv
        dx_ref[...] = dx
        dxb_ref[...] = dx.astype(BF16)
        part = jnp.sum(dgr, axis=0, keepdims=True)

        @pl.when(pl.program_id(0) == 0)
        def _():
            dg_ref[...] = part

        @pl.when(pl.program_id(0) > 0)
        def _():
            dg_ref[...] += part

    row = pl.BlockSpec((tr, d), lambda i: (i, 0))
    one = pl.BlockSpec((1, d), lambda i: (0, 0))
    return pl.pallas_call(
        body, name=name, grid=(s // tr,), in_specs=[row, pl.BlockSpec((None, 1, d), lambda i: (layer, 0, 0)), row, row],
        out_specs=[row, row, one],
        out_shape=[jax.ShapeDtypeStruct((s, d), F32), jax.ShapeDtypeStruct((s, d), BF16), jax.ShapeDtypeStruct((1, d), F32)],
        compiler_params=_params(("arbitrary",)),
    )(x, g.reshape(-1, 1, d), dh, dres)


def _loss_head(y, target, *, name):
    s, d = y.shape
    tr = _pick(s, (256, 128))

    def body(y_ref, t_ref, dy_ref, dyb_ref, ls_ref):
        e = y_ref[...] - t_ref[...]
        dy = e * (1.0 / d)
        dy_ref[...] = dy
        dyb_ref[...] = dy.astype(BF16)
        part = jnp.full(ls_ref.shape, jnp.sum(e * e), F32)

        @pl.when(pl.program_id(0) == 0)
        def _():
            ls_ref[...] = part

        @pl.when(pl.program_id(0) > 0)
        def _():
            ls_ref[...] += part

    row = pl.BlockSpec((tr, d), lambda i: (i, 0))
    return pl.pallas_call(
        body, name=name, grid=(s // tr,), in_specs=[row, row], out_specs=[row, row, pl.BlockSpec((8, 128), lambda i: (0, 0))],
        out_shape=[jax.ShapeDtypeStruct((s, d), F32), jax.ShapeDtypeStruct((s, d), BF16), jax.ShapeDtypeStruct((8, 128), F32)],
        compiler_params=_params(("arbitrary",)),
    )(y, target)


def _iota2(axis):
    return lax.broadcasted_iota(jnp.int32, (BLK, BLK), axis)


def _tri_sum(v, tri):
    hi = v.astype(BF16)
    lo = (v - hi.astype(F32)).astype(BF16)
    return jnp.dot(hi, tri, preferred_element_type=F32) + jnp.dot(lo, tri, preferred_element_type=F32)


def _dot_nt(a, b):
    return lax.dot_general(a, b, (((1,), (1,)), ((), ())), preferred_element_type=F32)


def _dot_tn(a, b):
    return lax.dot_general(a, b, (((0,), (0,)), ((), ())), preferred_element_type=F32)


TQ_MAX = 1024
TQ_MAX_BWD = 1024
HP = 2
VMEM_ATTN_BWD = 56 * 1024 * 1024


def _lanes(hh):
    return slice(hh * BLK, (hh + 1) * BLK)


def _causal(n, diag):
    if not diag:
        return None
    return lax.broadcasted_iota(jnp.int32, (n, BLK), 1) < lax.broadcasted_iota(jnp.int32, (n, BLK), 0)


def _sb_sums(z, mask, rhs_gt):
    lb = jnp.minimum(z, 0.0) - jnp.log(1.0 + jnp.exp(-jnp.abs(z)))
    l1m = lb - z
    if mask is not None:
        l1m = jnp.where(mask, l1m, 0.0)
    return lb, _tri_sum(l1m, rhs_gt)


def _below(old, new, r0):
    return new if r0 == 0 else jnp.concatenate([old[:r0], new], axis=0)


def _attn_fwd(p, gq, gk, go, layer, n_heads, *, name):
    s = p.shape[0]
    tq = min(TQ_MAX, s)
    per = tq // BLK
    scale = BLK ** -0.5

    def body(q_ref, k_ref, v_ref, gq_ref, gk_ref, go_ref, att_ref, o_ref, l_ref, qn, kn, vb):
        for hh in range(HP):
            q = q_ref[:, _lanes(hh)]
            k = k_ref[:, _lanes(hh)]
            qn[:, _lanes(hh)] = (q * _rstd(q) * gq_ref[...] * scale).astype(BF16)
            kn[:, _lanes(hh)] = (k * _rstd(k) * gk_ref[...]).astype(BF16)
            vb[:, _lanes(hh)] = v_ref[:, _lanes(hh)].astype(BF16)
        rhs_gt = jnp.concatenate([(_iota2(0) > _iota2(1)).astype(BF16), jnp.ones((BLK, BLK), BF16)], axis=1)

        def step(q0, j, r0, diag, states):
            n = tq - r0
            rows = pl.ds(pl.multiple_of(q0 + r0, BLK), n)
            cols = pl.ds(pl.multiple_of(j * BLK, BLK), BLK)
            mask = _causal(n, diag)
            zs = [_dot_nt(qn[rows, _lanes(hh)], kn[cols, _lanes(hh)]) for hh in range(HP)]
            sums = [_sb_sums(z, mask, rhs_gt) for z in zs]
            new = []
            for hh in range(HP):
                acc, later = states[hh]
                lb, both = sums[hh]
                a = jnp.exp(lb + both[:, :BLK] + later[r0:])
                if diag:
                    a = jnp.where(mask, a, 0.0)
                acc_new = acc[r0:] + jnp.dot(a.astype(BF16), vb[cols, _lanes(hh)], preferred_element_type=F32)
                new.append((_below(acc, acc_new, r0), _below(later, later[r0:] + both[:, BLK:], r0)))
            return tuple(new)

        def q_block(i, _):
            q0 = i * tq
            zero = jnp.zeros((tq, BLK), F32)
            states = ((zero, zero),) * HP
            for jd in reversed(range(per)):
                states = step(q0, i * per + jd, jd * BLK, True, states)
            states = lax.fori_loop(0, i * per, lambda jj, st: step(q0, i * per - 1 - jj, 0, False, st), states)
            tile = pl.ds(pl.multiple_of(q0, tq), tq)
            for hh in range(HP):
                o, total = states[hh]
                o_ref[tile, _lanes(hh)] = o
                l_ref[hh, tile, :] = total
                att_ref[tile, _lanes(hh)] = (o * _rstd(o) * go_ref[hh]).astype(att_ref.dtype)
            return 0

        lax.fori_loop(0, s // tq, q_block, 0)

    assert n_heads % HP == 0
    groups = n_heads // HP

    def col(part):
        return pl.BlockSpec((s, HP * BLK), lambda g: (0, part * groups + g))

    gain = pl.BlockSpec((None, 1, BLK), lambda g: (layer, 0, 0))
    per_head = pl.BlockSpec((None, HP, 1, BLK), lambda g: (layer, g, 0, 0))
    return pl.pallas_call(
        body, name=name, grid=(groups,),
        in_specs=[col(0), col(1), col(2), gain, gain, per_head],
        out_specs=[pl.BlockSpec((None, s, HP * BLK), lambda g: (0, 0, g)), col(0), pl.BlockSpec((HP, s, BLK), lambda g: (g, 0, 0))],
        out_shape=[jax.ShapeDtypeStruct((2, s, n_heads * BLK), BF16), jax.ShapeDtypeStruct((s, n_heads * BLK), F32),
                   jax.ShapeDtypeStruct((n_heads, s, BLK), F32)],
        scratch_shapes=[pltpu.VMEM((s, HP * BLK), BF16)] * 3,
        compiler_params=_params(("parallel",), VMEM_BIG),
    )(p, p, p, gq.reshape(-1, 1, BLK), gk.reshape(-1, 1, BLK), go.reshape(-1, n_heads, 1, BLK))


def _attn_bwd(p, o_raw, lsum, dmix, gq, gk, go, layer, n_heads, *, name):
    s = p.shape[0]
    tq = min(TQ_MAX_BWD, s)
    per = tq // BLK
    scale = BLK ** -0.5

    def body(q_ref, k_ref, v_ref, o_ref, l_ref, da_ref, gq_ref, gk_ref, go_ref,
             dqkv_ref, dgq_ref, dgk_ref, dgo_ref, qn, kn, vb, dob, dqn, dkn, dvv):
        for hh in range(HP):
            q = q_ref[:, _lanes(hh)]
            k = k_ref[:, _lanes(hh)]
            qn[:, _lanes(hh)] = (q * _rstd(q) * gq_ref[...] * scale).astype(BF16)
            kn[:, _lanes(hh)] = (k * _rstd(k) * gk_ref[...]).astype(BF16)
            vb[:, _lanes(hh)] = v_ref[:, _lanes(hh)].astype(BF16)
            o = o_ref[:, _lanes(hh)]
            do, dgo_rows = _norm_bwd(o, _rstd(o), go_ref[hh], da_ref[:, _lanes(hh)])
            dob[:, _lanes(hh)] = do.astype(BF16)
            dgo_ref[hh] = jnp.sum(dgo_rows, axis=0, keepdims=True)
        dkn[...] = jnp.zeros_like(dkn)
        dvv[...] = jnp.zeros_like(dvv)
        ones = jnp.ones((BLK, BLK), BF16)
        rhs_gt = jnp.concatenate([(_iota2(0) > _iota2(1)).astype(BF16), ones], axis=1)
        rhs_lt = jnp.concatenate([(_iota2(0) < _iota2(1)).astype(BF16), ones], axis=1)

        def step(q0, j, r0, diag, states):
            n = tq - r0
            rows = pl.ds(pl.multiple_of(q0 + r0, BLK), n)
            cols = pl.ds(pl.multiple_of(j * BLK, BLK), BLK)
            mask = _causal(n, diag)
            zs = [_dot_nt(qn[rows, _lanes(hh)], kn[cols, _lanes(hh)]) for hh in range(HP)]
            das = [_dot_nt(dob[rows, _lanes(hh)], vb[cols, _lanes(hh)]) for hh in range(HP)]
            sums = [_sb_sums(z, mask, rhs_gt) for z in zs]
            mids = []
            for hh in range(HP):
                lb, both = sums[hh]
                upto = states[hh][0][r0:] + both[:, BLK:]
                a = jnp.exp(lb + both[:, :BLK] + (l_ref[hh, rows, :] - upto))
                if diag:
                    a = jnp.where(mask, a, 0.0)
                g = das[hh] * a
                mids.append((lb, upto, a, g, _tri_sum(g, rhs_lt)))
            new = []
            for hh in range(HP):
                seen, gsum, dq = states[hh]
                lb, upto, a, g, bothg = mids[hh]
                beta = jnp.exp(lb)
                dz = g * (1.0 - beta) - beta * (bothg[:, :BLK] + gsum[r0:])
                if diag:
                    dz = jnp.where(mask, dz, 0.0)
                dzs = dz.astype(BF16)
                dq_new = dq[r0:] + jnp.dot(dzs, kn[cols, _lanes(hh)], preferred_element_type=F32)
                dkn[cols, _lanes(hh)] += _dot_tn(dzs, qn[rows, _lanes(hh)])
                dvv[cols, _lanes(hh)] += _dot_tn(a.astype(BF16), dob[rows, _lanes(hh)])
                new.append((_below(seen, upto, r0), _below(gsum, gsum[r0:] + bothg[:, BLK:], r0), _below(dq, dq_new, r0)))
            return tuple(new)

        def q_block(i, _):
            q0 = i * tq
            zero = jnp.zeros((tq, BLK), F32)
            states = ((zero, zero, zero),) * HP
            states = lax.fori_loop(0, i * per, lambda j, st: step(q0, j, 0, False, st), states)
            for jd in range(per):
                states = step(q0, i * per + jd, jd * BLK, True, states)
            tile = pl.ds(pl.multiple_of(q0, tq), tq)
            for hh in range(HP):
                dqn[tile, _lanes(hh)] = states[hh][2]
            return 0

        lax.fori_loop(0, s // tq, q_block, 0)
        for hh in range(HP):
            q = q_ref[:, _lanes(hh)]
            k = k_ref[:, _lanes(hh)]
            dq_raw, dgq_rows = _norm_bwd(q, _rstd(q), gq_ref[...], dqn[:, _lanes(hh)] * scale)
            dk_raw, dgk_rows = _norm_bwd(k, _rstd(k), gk_ref[...], dkn[:, _lanes(hh)])
            dqkv_ref[0, :, _lanes(hh)] = dq_raw.astype(BF16)
            dqkv_ref[1, :, _lanes(hh)] = dk_raw.astype(BF16)
            dqkv_ref[2, :, _lanes(hh)] = dvv[:, _lanes(hh)].astype(BF16)
            dgq_ref[hh] = jnp.sum(dgq_rows, axis=0, keepdims=True)
            dgk_ref[hh] = jnp.sum(dgk_rows, axis=0, keepdims=True)

    assert n_heads % HP == 0
    groups = n_heads // HP

    once = pl.Buffered(1)

    def col(part):
        return pl.BlockSpec((s, HP * BLK), lambda g: (0, part * groups + g), pipeline_mode=once)

    gain = pl.BlockSpec((None, 1, BLK), lambda g: (layer, 0, 0))
    per_head = pl.BlockSpec((HP, 1, BLK), lambda g: (g, 0, 0))
    head_gain = jax.ShapeDtypeStruct((n_heads, 1, BLK), F32)
    return pl.pallas_call(
        body, name=name, grid=(groups,),
        in_specs=[col(0), col(1), col(2), col(0), pl.BlockSpec((HP, s, BLK), lambda g: (g, 0, 0), pipeline_mode=once), col(0),
                  gain, gain, pl.BlockSpec((None, HP, 1, BLK), lambda g: (layer, g, 0, 0))],
        out_specs=[pl.BlockSpec((3, s, HP * BLK), lambda g: (0, 0, g)), per_head, per_head, per_head],
        out_shape=[jax.ShapeDtypeStruct((3, s, n_heads * BLK), BF16), head_gain, head_gain, head_gain],
        scratch_shapes=[pltpu.VMEM((s, HP * BLK), BF16)] * 4 + [pltpu.VMEM((s, HP * BLK), F32)] * 3,
        compiler_params=_params(("parallel",), VMEM_ATTN_BWD),
    )(p, p, p, o_raw, lsum, dmix, gq.reshape(-1, 1, BLK), gk.reshape(-1, 1, BLK), go.reshape(-1, n_heads, 1, BLK))


SGU_TOGETHER = 4
_INV_SQRT2 = 0.7071067811865476
_INV_SQRT2PI = 0.3989422804014327


def _gelu(x):
    return 0.5 * x * (1.0 + lax.erf(x * _INV_SQRT2))


def _gelu_and_grad(x):
    cdf = 0.5 * (1.0 + lax.erf(x * _INV_SQRT2))
    return x * cdf, cdf + x * jnp.exp(-0.5 * x * x) * _INV_SQRT2PI


def _sgu_fwd(p, mix, w, b, gv, gout, layer, n_heads, *, name):
    s = p.shape[0]
    n_groups = w.shape[1]
    nb = s // BLK
    assert mix.shape == (2, s, n_groups * BLK)

    def body(u_ref, v_ref, w_ref, b_ref, gv_ref, go_ref, _mix_ref, out_ref):
        wt = jnp.where(_iota2(0) >= _iota2(1), w_ref[...], 0.0).astype(BF16)
        bias = b_ref[...]

        def chunks(i, _):
            rows = [pl.ds(pl.multiple_of((i * SGU_TOGETHER + k) * BLK, BLK), BLK) for k in range(SGU_TOGETHER)]
            us = [_gelu(u_ref[r, :]) for r in rows]
            vss = []
            for r in rows:
                vv = _gelu(v_ref[r, :])
                vss.append((vv * _rstd(vv) * gv_ref[...]).astype(BF16))
            mixed = [jnp.dot(wt, vs, preferred_element_type=F32) + bias for vs in vss]
            for r, u, m in zip(rows, us, mixed):
                gated = u * m
                out_ref[r, :] = (gated * _rstd(gated) * go_ref[...]).astype(out_ref.dtype)
            return 0

        assert nb % SGU_TOGETHER == 0
        lax.fori_loop(0, nb // SGU_TOGETHER, chunks, 0)

    def col(off):
        return pl.BlockSpec((s, BLK), lambda g: (0, off + g))

    per_group = pl.BlockSpec((None, None, 1, BLK), lambda g: (layer, g, 0, 0))
    return pl.pallas_call(
        body, name=name, grid=(n_groups,),
        in_specs=[col(3 * n_heads), col(3 * n_heads + n_groups), pl.BlockSpec((None, None, BLK, BLK), lambda g: (layer, g, 0, 0)),
                  pl.BlockSpec((None, None, BLK, 1), lambda g: (layer, g, 0, 0)), per_group, per_group, ANY],
        out_specs=pl.BlockSpec((None, s, BLK), lambda g: (1, 0, g)), out_shape=jax.ShapeDtypeStruct(mix.shape, mix.dtype),
        input_output_aliases={6: 0}, compiler_params=_params(("parallel",), VMEM_BIG),
    )(p, p, w, b.reshape(-1, n_groups, BLK, 1), gv.reshape(-1, n_groups, 1, BLK), gout.reshape(-1, n_groups, 1, BLK), mix)


def _sgu_bwd(p, dmix, w, b, gv, gout, layer, n_heads, *, name):
    s = p.shape[0]
    n_groups = w.shape[1]
    nb = s // BLK

    def body(u_ref, v_ref, ds_ref, w_ref, b_ref, gv_ref, go_ref, duv_ref, dw_ref, db_ref, dgv_ref, dgo_ref):
        lower = _iota2(0) >= _iota2(1)
        wt = jnp.where(lower, w_ref[...], 0.0).astype(BF16)
        bias = b_ref[...]

        def chunks(i, carry):
            dw, db, dgv, dgo = carry
            rows = [pl.ds(pl.multiple_of((i * SGU_TOGETHER + k) * BLK, BLK), BLK) for k in range(SGU_TOGETHER)]
            pre = []
            for r in rows:
                u, u_grad = _gelu_and_grad(u_ref[r, :])
                vv, vv_grad = _gelu_and_grad(v_ref[r, :])
                rv = _rstd(vv)
                pre.append((u, u_grad, vv, vv_grad, rv, (vv * rv * gv_ref[...]).astype(BF16)))
            mixed = [jnp.dot(wt, t[5], preferred_element_type=F32) + bias for t in pre]
            mid = []
            for r, t, m in zip(rows, pre, mixed):
                gated = t[0] * m
                dgated, dgo_rows = _norm_bwd(gated, _rstd(gated), go_ref[...], ds_ref[r, :])
                dmixed = dgated * t[0]
                duv_ref[0, r, :] = (dgated * m * t[1]).astype(BF16)
                dgo = dgo + jnp.sum(dgo_rows, axis=0, keepdims=True)
                db = db + jnp.sum(dmixed, axis=1, keepdims=True)
                mid.append(dmixed.astype(BF16))
            dvss = [_dot_tn(wt, dmb) for dmb in mid]
            for dmb, t in zip(mid, pre):
                dw = dw + _dot_nt(dmb, t[5])
            for r, t, dvs in zip(rows, pre, dvss):
                dvv, dgv_rows = _norm_bwd(t[2], t[4], gv_ref[...], dvs)
                duv_ref[1, r, :] = (dvv * t[3]).astype(BF16)
                dgv = dgv + jnp.sum(dgv_rows, axis=0, keepdims=True)
            return dw, db, dgv, dgo

        assert nb % SGU_TOGETHER == 0
        row0 = jnp.zeros((1, BLK), F32)
        dw, db, dgv, dgo = lax.fori_loop(0, nb // SGU_TOGETHER, chunks,
                                         (jnp.zeros((BLK, BLK), F32), jnp.zeros((BLK, 1), F32), row0, row0))
        dw_ref[...] = jnp.where(lower, dw, 0.0)
        db_ref[...] = db
        dgv_ref[...] = dgv
        dgo_ref[...] = dgo

    def col(off):
        return pl.BlockSpec((s, BLK), lambda g: (0, off + g))

    per_group = pl.BlockSpec((None, 1, BLK), lambda g: (g, 0, 0))
    square = pl.BlockSpec((None, BLK, BLK), lambda g: (g, 0, 0))
    column = pl.BlockSpec((None, BLK, 1), lambda g: (g, 0, 0))
    gain = jax.ShapeDtypeStruct((n_groups, 1, BLK), F32)
    return pl.pallas_call(
        body, name=name, grid=(n_groups,),
        in_specs=[col(3 * n_heads), col(3 * n_heads + n_groups), col(n_heads),
                  pl.BlockSpec((None, None, BLK, BLK), lambda g: (layer, g, 0, 0)),
                  pl.BlockSpec((None, None, BLK, 1), lambda g: (layer, g, 0, 0)),
                  pl.BlockSpec((None, None, 1, BLK), lambda g: (layer, g, 0, 0)),
                  pl.BlockSpec((None, None, 1, BLK), lambda g: (layer, g, 0, 0))],
        out_specs=[pl.BlockSpec((2, s, BLK), lambda g: (0, 0, g)), square, column, per_group, per_group],
        out_shape=[jax.ShapeDtypeStruct((2, s, n_groups * BLK), BF16), jax.ShapeDtypeStruct((n_groups, BLK, BLK), F32),
                   jax.ShapeDtypeStruct((n_groups, BLK, 1), F32), gain, gain],
        compiler_params=_params(("parallel",), VMEM_BIG),
    )(p, p, dmix, w, b.reshape(-1, n_groups, BLK, 1), gv.reshape(-1, n_groups, 1, BLK), gout.reshape(-1, n_groups, 1, BLK))


CONV_ROWS = 256
HALO = 8


def _shift_down(ref, r0, n, first):
    cur = ref[pl.ds(r0, n), :]
    prev = jnp.zeros((HALO, cur.shape[1]), F32) if first else ref[pl.ds(r0 - HALO, HALO), :]
    ext = jnp.concatenate([prev, cur], axis=0)
    return pltpu.roll(ext, 1, 0)[HALO:], pltpu.roll(ext, 2, 0)[HALO:], cur


def _shift_up(ref, r0, n, last):
    cur = ref[pl.ds(r0, n), :]
    nxt = jnp.zeros((HALO, cur.shape[1]), F32) if last else ref[pl.ds(r0 + n, HALO), :]
    ext = jnp.concatenate([cur, nxt], axis=0)
    return cur, pltpu.roll(ext, n + HALO - 1, 0)[:n], pltpu.roll(ext, n + HALO - 2, 0)[:n]


def _conv_rows(x1, x2, x0, w_ref, b_ref):
    return ((b_ref[...] + x2 * w_ref[0:1, :]) + x1 * w_ref[1:2, :]) + x0 * w_ref[2:3, :]


def _conv_specs(s, f, tc, layer):
    nf = f // tc
    gate = pl.BlockSpec((s, tc), lambda n: (0, n))
    wg = pl.BlockSpec((None, 3, tc), lambda n: (layer, 0, n))
    wv = pl.BlockSpec((None, 3, tc), lambda n: (layer, 0, nf + n))
    bg = pl.BlockSpec((None, 1, tc), lambda n: (layer, 0, n))
    bv = pl.BlockSpec((None, 1, tc), lambda n: (layer, 0, nf + n))
    return nf, gate, wg, wv, bg, bv


def _up_conv_fwd(h, w_up, cw, cb, layer, *, name):
    s, d = h.shape
    chips, _, per_chip = w_up.shape
    f = chips * per_chip // 2
    tc = _pick(math.gcd(f, per_chip), (256, 128))
    cr = min(CONV_ROWS, s)
    nf, gate, wg, wv, bg, bv = _conv_specs(s, f, tc, layer)
    per = per_chip // tc

    def body(h_ref, mg_ref, mv_ref, wg_ref, wv_ref, bg_ref, bv_ref, ug_ref, uv_ref, out_ref):
        hb = h_ref[...]
        ug_ref[...] = jnp.dot(hb, mg_ref[...], preferred_element_type=F32)
        uv_ref[...] = jnp.dot(hb, mv_ref[...], preferred_element_type=F32)
        for r0 in range(0, s, cr):
            gc = _conv_rows(*_shift_down(ug_ref, r0, cr, r0 == 0), wg_ref, bg_ref)
            vc = _conv_rows(*_shift_down(uv_ref, r0, cr, r0 == 0), wv_ref, bv_ref)
            out_ref[pl.ds(r0, cr), :] = (gc * jax.nn.sigmoid(gc) * vc).astype(out_ref.dtype)

    def cols(first):
        return pl.BlockSpec((None, d, tc), lambda n: ((first + n) // per, 0, (first + n) % per))

    half = jax.ShapeDtypeStruct((s, f), F32)
    return pl.pallas_call(
        body, name=name, grid=(nf,), in_specs=[pl.BlockSpec((s, d), lambda n: (0, 0)), cols(0), cols(nf), wg, wv, bg, bv],
        out_specs=[gate, gate, gate], out_shape=[half, half, jax.ShapeDtypeStruct((s, f), BF16)],
        compiler_params=_params(("arbitrary",), VMEM_BIG),
    )(h, w_up, w_up, cw, cw, cb.reshape(-1, 1, 2 * f), cb.reshape(-1, 1, 2 * f))


def _conv_bwd(up_gate, up_val, dy, w_down, cw, cb, layer, *, name):
    s, f = up_gate.shape
    d = dy.shape[1]
    f2 = 2 * f
    tc = _pick(f, (256, 128))
    cr = min(CONV_ROWS, s)
    nf, gate, wg, wv, bg, bv = _conv_specs(s, f, tc, layer)

    def body(g_ref, v_ref, dy_ref, wd_first, wd_next, wg_ref, wv_ref, bg_ref, bv_ref, dup_ref, dw_ref, db_ref,
             dgc, dvc, da_ref, da_next):
        @pl.when(pl.program_id(0) == 0)
        def _():
            da_ref[...] = _dot_nt(dy_ref[...], wd_first[...])

        da_next[...] = _dot_nt(dy_ref[...], wd_next[...])
        zero = jnp.zeros((1, tc), F32)
        sums = [[zero] * 4, [zero] * 4]
        for r0 in range(0, s, cr):
            rows = pl.ds(r0, cr)
            gx = _shift_down(g_ref, r0, cr, r0 == 0)
            vx = _shift_down(v_ref, r0, cr, r0 == 0)
            gc = _conv_rows(*gx, wg_ref, bg_ref)
            vc = _conv_rows(*vx, wv_ref, bv_ref)
            sig = jax.nn.sigmoid(gc)
            da = da_ref[rows, :]
            d_gate = da * vc * (sig * (1.0 + gc * (1.0 - sig)))
            d_val = da * (gc * sig)
            dgc[rows, :] = d_gate
            dvc[rows, :] = d_val
            for part, (dc, (x1, x2, x0)) in enumerate(((d_gate, gx), (d_val, vx))):
                for tap, xs in enumerate((x2, x1, x0)):
                    sums[part][tap] = sums[part][tap] + jnp.sum(dc * xs, axis=0, keepdims=True)
                sums[part][3] = sums[part][3] + jnp.sum(dc, axis=0, keepdims=True)
        dw_ref[...] = jnp.zeros_like(dw_ref)
        db_ref[...] = jnp.zeros_like(db_ref)
        for part, (dc_ref, w_ref) in enumerate(((dgc, wg_ref), (dvc, wv_ref))):
            for tap in range(3):
                dw_ref[part, tap:tap + 1, :] = sums[part][tap]
            db_ref[part, 0:1, :] = sums[part][3]
            for r0 in range(0, s, cr):
                d0, d1, d2 = _shift_up(dc_ref, r0, cr, r0 + cr == s)
                dup_ref[part, pl.ds(r0, cr), :] = ((d0 * w_ref[2:3, :] + d1 * w_ref[1:2, :]) + d2 * w_ref[0:1, :]).astype(BF16)
        da_ref[...] = da_next[...]

    small = pl.BlockSpec((2, 8, tc), lambda n: (0, 0, n))
    return pl.pallas_call(
        body, name=name, grid=(nf,),
        in_specs=[gate, gate, pl.BlockSpec((s, d), lambda n: (0, 0)), pl.BlockSpec((tc, d), lambda n: (0, 0)),
                  pl.BlockSpec((tc, d), lambda n: (jnp.minimum(n + 1, nf - 1), 0)), wg, wv, bg, bv],
        out_specs=[pl.BlockSpec((2, s, tc), lambda n: (0, 0, n)), small, small],
        out_shape=[jax.ShapeDtypeStruct((2, s, f), BF16), jax.ShapeDtypeStruct((2, 8, f), F32),
                   jax.ShapeDtypeStruct((2, 8, f), F32)],
        scratch_shapes=[pltpu.VMEM((s, tc), F32)] * 4,
        compiler_params=_params(("arbitrary",), VMEM_BIG),
    )(up_gate, up_val, dy, w_down, w_down, cw, cw, cb.reshape(-1, 1, f2), cb.reshape(-1, 1, f2))


def _adamw(w, g, m, v, *, name, dep=None, copy_g=False):
    shape = w.shape
    cols = shape[-1]
    rows = w.size // cols
    if rows * cols * 4 <= (2 << 20):
        tr = rows
    else:
        tr = next(t for t in (1024, 512, 256, 128, 64, 32, 16, 8) if rows % t == 0 and (t * cols * 4 <= (2 << 20) or t == 8))

    n_out = 4 if copy_g else 3

    def body(w_ref, g_ref, m_ref, v_ref, *rest):
        d_ref, nm_ref, nv_ref = rest[-n_out:][:3]
        gr = g_ref[...]
        if copy_g:
            rest[-1][...] = gr
        nm = ADAM_B1 * m_ref[...] + (1.0 - ADAM_B1) * gr
        nv = ADAM_B2 * v_ref[...] + (1.0 - ADAM_B2) * (gr * gr)
        m_hat = nm / (1.0 - ADAM_B1 ** ADAM_STEP)
        v_hat = nv / (1.0 - ADAM_B2 ** ADAM_STEP)
        d_ref[...] = -ADAM_LR * (m_hat / (jnp.sqrt(v_hat) + ADAM_EPS) + ADAM_WD * w_ref[...])
        nm_ref[...] = nm
        nv_ref[...] = nv

    blk = pl.BlockSpec((tr, cols), lambda i: (i, 0))
    out = jax.ShapeDtypeStruct((rows, cols), F32)
    res = pl.pallas_call(
        body, name=name, grid=(rows // tr,), in_specs=[blk] * 4 + ([] if dep is None else [ANY]), out_specs=[blk] * n_out,
        out_shape=[out] * n_out, compiler_params=_params(("parallel",), VMEM_BIG),
    )(*[t.reshape(rows, cols) for t in (w, g, m, v)], *([] if dep is None else [dep]))
    return [t.reshape(shape) for t in res]


def _place():
    x, y, c = lax.axis_index("x"), lax.axis_index("y"), lax.axis_index("c")
    others = [(1 - x, y), (x, 1 - y), (1 - x, 1 - y)]
    return x, y, c, others


def _remote(src, dst, send_sem, recv_sem, device):
    return pltpu.make_async_remote_copy(src_ref=src, dst_ref=dst, send_sem=send_sem, recv_sem=recv_sem, device_id=device,
                                        device_id_type=MESH)


def _cast_into(w, layer, place, *, name, dep=None):
    _, r, cols = w.shape
    tr = _row_tile(r, cols)

    def body(place_ref, w_ref, *rest):
        rest[-1][...] = w_ref[...].astype(BF16)

    in_specs, args = [pl.BlockSpec((None, tr, cols), lambda i, pr: (layer, i, 0))], [place, w]
    if dep is not None:
        in_specs.append(ANY)
        args.append(dep)
    grid_spec = pltpu.PrefetchScalarGridSpec(
        num_scalar_prefetch=1, grid=(r // tr,), in_specs=in_specs,
        out_specs=pl.BlockSpec((None, tr, cols), lambda i, pr: (pr[1], i, 0)),
    )
    return pl.pallas_call(
        body, name=name, grid_spec=grid_spec, out_shape=jax.ShapeDtypeStruct((N_CHIPS, r, cols), BF16),
        compiler_params=_params(("parallel",), VMEM_BIG),
    )(*args)


HBM = pl.BlockSpec(memory_space=pltpu.HBM)
SEM = pl.BlockSpec(memory_space=pltpu.SEMAPHORE)
EFFECT = pltpu.SideEffectType.DATAFLOW_SIDE_EFFECTING
TOKEN = jax.ShapeDtypeStruct((8, 128), F32)


def _in_hbm(t):
    return pltpu.with_memory_space_constraint(t, pltpu.HBM)


def _gather_copies(buf, send, recv):
    x, y, c, others = _place()
    half = buf.shape[1] // 2
    rows = pl.ds(c * half, half)
    return [_remote(buf.at[2 * x + y, rows], buf.at[2 * x + y, rows], send.at[j], recv.at[j], (px, py, c))
            for j, (px, py) in enumerate(others)]


def _gather_start(bufs, *, name):
    n = len(bufs)

    def body(*refs):
        ins, sends, recvs, token = refs[:n], refs[n:2 * n], refs[2 * n:3 * n], refs[4 * n]
        for a in range(n):
            for cp in _gather_copies(ins[a], sends[a], recvs[a]):
                cp.start()
        token[...] = jnp.zeros_like(token)

    sems = [pltpu.SemaphoreType.DMA((3,))] * (2 * n)
    res = pl.pallas_call(
        body, name=name, out_shape=sems + [pltpu.HBM(t.shape, t.dtype) for t in bufs] + [TOKEN],
        in_specs=[HBM] * n, out_specs=[SEM] * (2 * n) + [HBM] * n + [pl.BlockSpec(memory_space=pltpu.VMEM)],
        input_output_aliases={a: 2 * n + a for a in range(n)}, compiler_params=pltpu.CompilerParams(has_side_effects=EFFECT),
    )(*[_in_hbm(t) for t in bufs])
    return [(res[2 * n + a], res[a], res[n + a]) for a in range(n)], res[3 * n]


def _gather_wait(state, after, *, name):
    buf, send, recv = state

    def body(buf_ref, send_ref, recv_ref, after_ref, out_ref):
        for cp in _gather_copies(buf_ref, send_ref, recv_ref):
            cp.wait_send()
            cp.wait_recv()

    return pl.pallas_call(
        body, name=name, out_shape=pltpu.HBM(buf.shape, buf.dtype), in_specs=[HBM, SEM, SEM, ANY], out_specs=HBM,
        input_output_aliases={0: 0}, compiler_params=pltpu.CompilerParams(has_side_effects=EFFECT),
    )(buf, send, recv, after)


def _chip_copies(src, land, send, recv):
    _x, _y, c, others = _place()
    return [_remote(src.at[2 * px + py], land.at[j], send.at[j], recv.at[j], (px, py, c)) for j, (px, py) in enumerate(others)]


def _chip_start(partial, *, name):
    def body(src, land, send, recv, _src_thru, _land_thru, token):
        for cp in _chip_copies(src, land, send, recv):
            cp.start()
        token[...] = jnp.zeros_like(token)

    land_shape = (3,) + partial.shape[1:]
    sem = pltpu.SemaphoreType.DMA((3,))
    send, recv, src, land, token = pl.pallas_call(
        body, name=name, out_shape=[sem, sem, pltpu.HBM(partial.shape, partial.dtype), pltpu.HBM(land_shape, partial.dtype), TOKEN],
        in_specs=[HBM, HBM], out_specs=[SEM, SEM, HBM, HBM, pl.BlockSpec(memory_space=pltpu.VMEM)],
        input_output_aliases={0: 2, 1: 3}, compiler_params=pltpu.CompilerParams(has_side_effects=EFFECT),
    )(_in_hbm(partial), _in_hbm(lax.empty(land_shape, partial.dtype)))
    return (src, land, send, recv), token


def _chip_wait(state, after, *, name):
    src, land, send, recv = state

    def body(src_ref, land_ref, send_ref, recv_ref, after_ref, _src_out, _land_out):
        for cp in _chip_copies(src_ref, land_ref, send_ref, recv_ref):
            cp.wait_send()
            cp.wait_recv()

    return pl.pallas_call(
        body, name=name, out_shape=[pltpu.HBM(src.shape, src.dtype), pltpu.HBM(land.shape, land.dtype)],
        in_specs=[HBM, HBM, SEM, SEM, ANY], out_specs=[HBM, HBM], input_output_aliases={0: 0, 1: 1},
        compiler_params=pltpu.CompilerParams(has_side_effects=EFFECT),
    )(src, land, send, recv, after)[1]


SIBLING_ID = 1


def _split_start(bufs, copies, n_copies, *, name, sibling_only=False):
    n = len(bufs)

    def body(*refs):
        if sibling_only:
            x, y, c, _o = _place()
            barrier = pltpu.get_barrier_semaphore()
            pl.semaphore_signal(barrier, inc=1, device_id=(x, y, 1 - c), device_id_type=MESH)
            pl.semaphore_wait(barrier, 1)
        for cp in copies(refs[:n], refs[n], refs[n + 1]):
            cp.start()
        refs[-1][...] = jnp.zeros_like(refs[-1])

    sem = pltpu.SemaphoreType.DMA((n_copies,))
    res = pl.pallas_call(
        body, name=name, out_shape=[sem, sem] + [pltpu.HBM(t.shape, t.dtype) for t in bufs] + [TOKEN],
        in_specs=[HBM] * n, out_specs=[SEM, SEM] + [HBM] * n + [pl.BlockSpec(memory_space=pltpu.VMEM)],
        input_output_aliases={a: 2 + a for a in range(n)},
        compiler_params=pltpu.CompilerParams(has_side_effects=EFFECT, collective_id=SIBLING_ID if sibling_only else None),
    )(*[_in_hbm(t) for t in bufs])
    return (list(res[2:2 + n]), res[0], res[1]), res[-1]


def _split_wait(state, copies, after, *, name):
    bufs, send, recv = state
    n = len(bufs)

    def body(*refs):
        for cp in copies(refs[:n], refs[n], refs[n + 1]):
            cp.wait_send()
            cp.wait_recv()

    return list(pl.pallas_call(
        body, name=name, out_shape=[pltpu.HBM(t.shape, t.dtype) for t in bufs], in_specs=[HBM] * n + [SEM, SEM, ANY],
        out_specs=[HBM] * n, input_output_aliases={a: a for a in range(n)},
        compiler_params=pltpu.CompilerParams(has_side_effects=EFFECT),
    )(*bufs, send, recv, after))


def _hand_over_copies(refs, send, recv):
    x, y, c, others = _place()
    half = refs[0].shape[1] // 2
    got = [refs[0].at[2 * px + py, pl.ds(c * half, half)] for px, py in others]
    return [_remote(got[j], got[j], send.at[j], recv.at[j], (x, y, 1 - c)) for j in range(3)]


def _pair_copies(refs, send, recv):
    x, y, c, _o = _place()
    half = refs[0].shape[1] // 2
    return [_remote(refs[0].at[:, pl.ds((1 - c) * half, half), :], refs[1], send.at[0], recv.at[0], (x, y, 1 - c))]


def _share_copies(refs, send, recv):
    x, y, c, _o = _place()
    return [_remote(refs[0].at[:, c], refs[0].at[:, c], send.at[0], recv.at[0], (x, y, 1 - c))]


def _small_copies(refs, send, recv):
    x, y, c, others = _place()
    peers = [(x, y, 1 - c)] + [(px, py, pc) for px, py in others for pc in (c, 1 - c)]
    slot = refs[1].at[4 * x + 2 * y + c]
    return [_remote(refs[0], slot, send.at[k], recv.at[k], peer) for k, peer in enumerate(peers)]


def _row_tile(rows, cols):
    return max(t for t in range(16, rows + 1, 16) if rows % t == 0 and (t * cols * 4 <= (4 << 20) or t == 16))


def _pair_sum(grad, theirs, place, *, name):
    _, r, cols = grad.shape
    r2 = r // 2
    tr = _row_tile(r2, cols)
    nr = r2 // tr

    def body(place_ref, g_ref, t_ref, all_ref):
        all_ref[...] = (g_ref[...].astype(F32) + t_ref[...].astype(F32)).astype(all_ref.dtype)

    def other(k, pr):
        return k + (k >= pr[1]).astype(jnp.int32)

    grid_spec = pltpu.PrefetchScalarGridSpec(
        num_scalar_prefetch=1, grid=(N_CHIPS - 1, nr),
        in_specs=[pl.BlockSpec((None, tr, cols), lambda k, i, pr: (other(k, pr), pr[0] * nr + i, 0)),
                  pl.BlockSpec((None, tr, cols), lambda k, i, pr: (other(k, pr), i, 0))],
        out_specs=pl.BlockSpec((None, tr, cols), lambda k, i, pr: (other(k, pr), i, 0)),
    )
    return pl.pallas_call(
        body, name=name, grid_spec=grid_spec, out_shape=jax.ShapeDtypeStruct((N_CHIPS, r2, cols), BF16),
        compiler_params=_params(("parallel", "parallel"), VMEM_BIG),
    )(place, grad, theirs)


def _chip_sum(grad, theirs, got, place, buf, layer, depth, *, name):
    _, r, cols = grad.shape
    r2 = r // 2
    tr = _row_tile(r2, cols)
    nr = r2 // tr

    def body(place_ref, g_ref, t_ref, got_ref, *rest):
        own = g_ref[...].astype(F32) + t_ref[...].astype(F32)
        rest[-1][...] = ((own + got_ref[0].astype(F32)) + got_ref[1].astype(F32)) + got_ref[2].astype(F32)

    in_specs = [pl.BlockSpec((None, tr, cols), lambda i, pr: (pr[1], pr[0] * nr + i, 0)),
                pl.BlockSpec((None, tr, cols), lambda i, pr: (pr[1], i, 0)),
                pl.BlockSpec((3, tr, cols), lambda i, pr: (0, i, 0))]
    args = [place, grad, theirs, got]
    if buf is not None:
        in_specs.append(ANY)
        args.append(buf)
    grid_spec = pltpu.PrefetchScalarGridSpec(
        num_scalar_prefetch=1, grid=(nr,), in_specs=in_specs,
        out_specs=pl.BlockSpec((None, None, tr, cols), lambda i, pr: (layer, pr[0], i, 0)),
    )
    return pl.pallas_call(
        body, name=name, grid_spec=grid_spec, out_shape=jax.ShapeDtypeStruct((depth, 2, r2, cols), F32),
        input_output_aliases={} if buf is None else {4: 0}, compiler_params=_params(("parallel",), VMEM_BIG),
    )(*args)


def _sum_devices(parts, own, place, *, name):
    _, rows, cols = parts.shape
    tr = rows if N_DEV * rows * cols * 4 <= (16 << 20) else _pick(rows, (256, 128, 64, 32, 16, 8))

    def body(place_ref, p_ref, own_ref, out_ref):
        me = 2 * place_ref[1] + place_ref[0]
        acc = None
        for dev in range(N_DEV):
            term = jnp.where(me == dev, own_ref[...], p_ref[dev])
            acc = term if acc is None else acc + term
        out_ref[...] = acc

    grid_spec = pltpu.PrefetchScalarGridSpec(
        num_scalar_prefetch=1, grid=(rows // tr,),
        in_specs=[pl.BlockSpec((N_DEV, tr, cols), lambda i, pr: (0, i, 0)), pl.BlockSpec((tr, cols), lambda i, pr: (i, 0))],
        out_specs=pl.BlockSpec((tr, cols), lambda i, pr: (i, 0)),
    )
    return pl.pallas_call(
        body, name=name, grid_spec=grid_spec, out_shape=jax.ShapeDtypeStruct((rows, cols), F32),
        compiler_params=_params(("parallel",), VMEM_BIG),
    )(place, parts, own)


def _pack(parts):
    rows = []
    for t in parts:
        flat = t.reshape(-1, 128)
        pad = (-flat.shape[0]) % 8
        rows.append(jnp.pad(flat, ((0, pad), (0, 0))) if pad else flat)
    return jnp.concatenate(rows, axis=0)


def _unpack(pack, shapes):
    out, r0 = [], 0
    for shp in shapes:
        n = math.prod(shp) // 128
        out.append(pack[r0:r0 + n].reshape(shp))
        r0 += n + (-n) % 8
    return out


SMALL = ["attn_norm_g", "q_norm_g", "k_norm_g", "sgu_norm_g", "sgu_w", "sgu_b", "out_norm_a_g", "out_norm_b_g",
         "ffn_norm_g", "conv_b"]
BIG = ["w_in", "w_out", "w_up", "w_down"]
ORDER = ["attn_norm_g", "w_in", "q_norm_g", "k_norm_g", "sgu_norm_g", "sgu_w", "sgu_b", "out_norm_a_g", "out_norm_b_g",
         "w_out", "ffn_norm_g", "w_up", "conv_w", "conv_b", "w_down"]


def kernel(x, attn_norm_g, w_in, q_norm_g, k_norm_g, sgu_norm_g, sgu_w, sgu_b, out_norm_a_g, out_norm_b_g, w_out, ffn_norm_g, w_up, conv_w, conv_b, w_down, loss_target, m_attn_norm_g, m_w_in, m_q_norm_g, m_k_norm_g, m_sgu_norm_g, m_sgu_w, m_sgu_b, m_out_norm_a_g, m_out_norm_b_g, m_w_out, m_ffn_norm_g, m_w_up, m_conv_w, m_conv_b, m_w_down, v_attn_norm_g, v_w_in, v_q_norm_g, v_k_norm_g, v_sgu_norm_g, v_sgu_w, v_sgu_b, v_out_norm_a_g, v_out_norm_b_g, v_w_out, v_ffn_norm_g, v_w_up, v_conv_w, v_conv_b, v_w_down):
    W = dict(attn_norm_g=attn_norm_g, w_in=w_in, q_norm_g=q_norm_g, k_norm_g=k_norm_g, sgu_norm_g=sgu_norm_g, sgu_w=sgu_w,
             sgu_b=sgu_b, out_norm_a_g=out_norm_a_g, out_norm_b_g=out_norm_b_g, w_out=w_out, ffn_norm_g=ffn_norm_g, w_up=w_up,
             conv_w=conv_w, conv_b=conv_b, w_down=w_down)
    M = dict(attn_norm_g=m_attn_norm_g, w_in=m_w_in, q_norm_g=m_q_norm_g, k_norm_g=m_k_norm_g, sgu_norm_g=m_sgu_norm_g,
             sgu_w=m_sgu_w, sgu_b=m_sgu_b, out_norm_a_g=m_out_norm_a_g, out_norm_b_g=m_out_norm_b_g, w_out=m_w_out,
             ffn_norm_g=m_ffn_norm_g, w_up=m_w_up, conv_w=m_conv_w, conv_b=m_conv_b, w_down=m_w_down)
    V = dict(attn_norm_g=v_attn_norm_g, w_in=v_w_in, q_norm_g=v_q_norm_g, k_norm_g=v_k_norm_g, sgu_norm_g=v_sgu_norm_g,
             sgu_w=v_sgu_w, sgu_b=v_sgu_b, out_norm_a_g=v_out_norm_a_g, out_norm_b_g=v_out_norm_b_g, w_out=v_w_out,
             ffn_norm_g=v_ffn_norm_g, w_up=v_w_up, conv_w=v_conv_w, conv_b=v_conv_b, w_down=v_w_down)
    depth = w_in.shape[0]
    s, d = x.shape[1], x.shape[2]
    n_heads = out_norm_a_g.shape[1]
    core = lax.axis_index("c")
    chip = 2 * lax.axis_index("x") + lax.axis_index("y")
    place = jnp.stack([core, chip]).astype(jnp.int32)
    xs = x.reshape(s, d)

    f_local = conv_w.shape[2]
    taps = lax.dynamic_update_slice(jnp.zeros((N_CHIPS, 16, f_local), F32), conv_w.reshape(1, depth * 3, f_local),
                                    (chip, 0, 0))
    order = [(l, n) for l in range(depth) for n in BIG]
    first, token = _gather_start([_cast_into(w_in, 0, place, name="cast_w_in"), taps], name="gather_start_first")
    rest, token = _gather_start([_cast_into(W[n], l, place, dep=token, name=f"cast_{n}") for l, n in order[1:]],
                                name="gather_start_rest")
    states = dict(zip(order, [first[0]] + rest))

    states["taps"] = first[1]

    def landed(key, after, tag):
        buf = _gather_wait(states[key], after, name=f"gather_wait_{tag}")
        return _split_start([buf], _hand_over_copies, 3, sibling_only=True, name=f"hand_over_{tag}")

    def whole(state, after, tag):
        return _split_wait(state, _hand_over_copies, after, name=f"hand_over_wait_{tag}")[0]

    saved, full = [], []
    cur = xs
    for l in range(depth):
        gain = attn_norm_g + token[0, 0] if l == 0 else attn_norm_g
        h = _rmsnorm_fwd(cur, gain, l, name="attn_norm")
        if l == 0:
            ho_in, _ = landed((0, "w_in"), h, "w_in")
        w_in_l = whole(ho_in, h, "w_in")
        if l == 0:
            ho_taps, token = landed("taps", w_in_l, "taps")
        p = _mm(h, w_in_l, "nn", b_split=N_CHIPS, caps=(2048, 256, 2048), dep=token, name="proj_in")
        if l == 0:
            taps = whole(ho_taps, p, "taps")[:, :depth * 3].reshape(N_CHIPS, depth, 3, f_local)
            cw_full = jnp.transpose(taps, (1, 2, 0, 3)).reshape(depth, 3, N_CHIPS * f_local)
        mix, o_raw, lsum = _attn_fwd(p, q_norm_g, k_norm_g, out_norm_a_g, l, n_heads, name="attn_fwd")
        ho_out, token = landed((l, "w_out"), mix, "w_out")
        mix = _sgu_fwd(p, mix, sgu_w, sgu_b, sgu_norm_g + token[0, 0], out_norm_b_g, l, n_heads, name="sgu_fwd")
        w_out_l = whole(ho_out, mix, "w_out").reshape(-1, d)
        ho_up, token = landed((l, "w_up"), w_out_l, "w_up")
        x1 = _mm(mix, w_out_l, "nn", a_split=2, res=cur, caps=(2048, 512, 1024), dep=token, name="proj_out")
        h2 = _rmsnorm_fwd(x1, ffn_norm_g, l, name="ffn_norm")
        w_up_l = whole(ho_up, h2, "w_up")
        up_gate, up_val, act = _up_conv_fwd(h2, w_up_l, cw_full, conv_b, l, name="ffn_up_conv")
        ho_down, token = landed((l, "w_down"), act, "w_down")
        w_down_l = whole(ho_down, act, "w_down").reshape(-1, d)
        if l + 1 < depth:
            ho_in, token = landed((l + 1, "w_in"), w_down_l, "w_in")
        x2 = _mm(act, w_down_l, "nn", res=x1, caps=(1024, 512, 2816), dep=token, name="ffn_down")
        full.append(dict(w_in=w_in_l, w_out=w_out_l, w_up=w_up_l, w_down=w_down_l))
        saved.append(dict(x0=cur, h=h, p=p, o_raw=o_raw, lsum=lsum, mix=mix, x1=x1, h2=h2, up_gate=up_gate, up_val=up_val,
                          act=act))
        cur = x2

    dx, dxb, sq = _loss_head(cur, loss_target.reshape(s, d), name="loss_head")
    loss = lax.psum(sq[0, 0] * (0.5 / d), ("x", "y", "c"))

    small_grads = {n: [None] * depth for n in SMALL + ["conv_w"]}
    def pair_begin(n, grad):
        land = lax.empty((N_CHIPS, grad.shape[1] // 2, grad.shape[2]), grad.dtype)
        return _split_start([grad, land], _pair_copies, 1, sibling_only=True, name=f"pair_start_{n}")

    def chip_begin(n, state, after):
        grad, theirs = _split_wait(state, _pair_copies, after, name=f"pair_wait_{n}")
        state, tok = _chip_start(_pair_sum(grad, theirs, place, name=f"pair_sum_{n}"), name=f"chip_start_{n}")
        return (grad, theirs, state), tok

    pending = {}
    for l in reversed(range(depth)):
        fw, sv = full[l], saved[l]
        g_down = _mm(sv["act"], dxb, "tn", caps=(512, 2048, 2048), out_dtype=BF16, name="g_down")
        pair, tok = pair_begin("w_down", g_down.reshape(N_CHIPS, -1, d))
        dup, dcw, dcb = _conv_bwd(sv["up_gate"], sv["up_val"], dxb, fw["w_down"], cw_full, conv_b + tok[0, 0], l,
                                  name="conv_bwd")
        pending[(l, "w_down")], tok = chip_begin("w_down", pair, dup)
        g_up = _mm(sv["h2"], dup, "tn", b_split=2, o_split=N_CHIPS, caps=(2048, 256, 2048), out_dtype=BF16, dep=tok,
                   name="g_up")
        pair, tok = pair_begin("w_up", g_up)
        dh2 = _mm(dup, fw["w_up"], "nt", a_split=2, b_split=N_CHIPS, caps=(2048, 512, 2816), dep=tok, name="d_h2")
        pending[(l, "w_up")], tok = chip_begin("w_up", pair, dh2)
        dx1, dx1b, dg_ffn = _rmsnorm_bwd(sv["x1"], ffn_norm_g + tok[0, 0], l, dh2, dx, name="ffn_norm_bwd")
        dmix = _mm(dx1b, fw["w_out"], "nt", caps=(2048, 512, 2048), name="d_mix")
        g_out = _mm(sv["mix"], dx1b, "tn", a_split=2, caps=(512, 2048, 2048), out_dtype=BF16, name="g_out")
        pair, tok = pair_begin("w_out", g_out.reshape(N_CHIPS, -1, d))
        dqkv, dgq, dgk, dgoa = _attn_bwd(sv["p"], sv["o_raw"], sv["lsum"], dmix, q_norm_g + tok[0, 0], k_norm_g,
                                         out_norm_a_g, l, n_heads, name="attn_bwd")
        pending[(l, "w_out")], tok = chip_begin("w_out", pair, dqkv)
        duv, dsw, dsb, dgv, dgob = _sgu_bwd(sv["p"], dmix, sgu_w, sgu_b, sgu_norm_g + tok[0, 0], out_norm_b_g, l,
                                            n_heads, name="sgu_bwd")
        dp = jnp.concatenate([dqkv[0], dqkv[1], dqkv[2], duv[0], duv[1]], axis=1)
        g_in = _mm(sv["h"], dp, "tn", o_split=N_CHIPS, caps=(2048, 256, 2048), out_dtype=BF16, name="g_in")
        pair, tok = pair_begin("w_in", g_in)
        dh = _mm(dp, fw["w_in"], "nt", b_split=N_CHIPS, caps=(2048, 512, 1280), dep=tok, name="d_h")
        pending[(l, "w_in")], tok = chip_begin("w_in", pair, dh)
        dx, dxb, dg_attn = _rmsnorm_bwd(sv["x0"], attn_norm_g + tok[0, 0], l, dh, dx1, name="attn_norm_bwd")

        small_grads["attn_norm_g"][l] = dg_attn.reshape(d)
        small_grads["q_norm_g"][l] = jnp.sum(dgq, axis=(0, 1))
        small_grads["k_norm_g"][l] = jnp.sum(dgk, axis=(0, 1))
        small_grads["sgu_norm_g"][l] = dgv.reshape(-1, BLK)
        small_grads["sgu_w"][l] = dsw
        small_grads["sgu_b"][l] = dsb.reshape(-1, BLK)
        small_grads["out_norm_a_g"][l] = dgoa.reshape(-1, BLK)
        small_grads["out_norm_b_g"][l] = dgob.reshape(-1, BLK)
        small_grads["ffn_norm_g"][l] = dg_ffn.reshape(d)
        small_grads["conv_b"][l] = dcb[:, 0, :].reshape(-1)
        small_grads["conv_w"][l] = jnp.transpose(dcw[:, :3, :], (1, 0, 2)).reshape(3, -1)

    names = SMALL + ["conv_w"]
    pack = _pack([jnp.stack(small_grads[n]) for n in names])
    small, tok = _split_start([pack, lax.empty((N_DEV,) + pack.shape, F32)], _small_copies, N_DEV - 1, name="small_start")
    G, D_, NM, NV = {}, {}, {}, {}
    after, prev = tok, None
    for n in ("w_down", "w_up", "w_out", "w_in"):
        buf = None
        for l in reversed(range(depth)):
            grad, theirs, state = pending[(l, n)]
            got = _chip_wait(state, after, name=f"chip_wait_{n}")
            buf = _chip_sum(grad, theirs, got, place, buf, l, depth, name=f"chip_sum_{n}")
            after = buf
        share, tok = _split_start([buf], _share_copies, 1, sibling_only=True, name=f"share_start_{n}")
        if prev is not None:
            D_[prev], NM[prev], NV[prev], G[prev] = _adamw(W[prev], G[prev], M[prev], V[prev], dep=tok, copy_g=True,
                                                           name=f"adamw_{prev}")
            after = NV[prev]
        G[n] = _split_wait(share, _share_copies, after, name=f"share_wait_{n}")[0].reshape(W[n].shape)
        after, prev = G[n], n
    D_[prev], NM[prev], NV[prev], G[prev] = _adamw(W[prev], G[prev], M[prev], V[prev], copy_g=True, name=f"adamw_{prev}")

    pack, parts = _split_wait(small, _small_copies, NV[prev], name="small_wait")
    total = _sum_devices(parts, pack, place, name="sum_small")
    f_full = conv_b.shape[1]
    shapes = [W[n].shape for n in SMALL] + [(depth, 3, f_full)]
    for n, t in zip(names, _unpack(total, shapes)):
        G[n] = t
    G["conv_w"] = lax.dynamic_slice_in_dim(G["conv_w"], chip * f_local, f_local, axis=2)

    D_["conv_w"], NM["conv_w"], NV["conv_w"] = _adamw(conv_w, G["conv_w"], m_conv_w, v_conv_w, name="adamw_conv_w")
    small_shapes = [W[n].shape for n in SMALL]
    res = _adamw(_pack([W[n] for n in SMALL]), _pack([G[n] for n in SMALL]), _pack([M[n] for n in SMALL]),
                 _pack([V[n] for n in SMALL]), name="adamw_small")
    for dst, t in zip((D_, NM, NV), res):
        for n, u in zip(SMALL, _unpack(t, small_shapes)):
            dst[n] = u

    return (loss, dx.reshape(x.shape), *[G[n] for n in ORDER], *[D_[n] for n in ORDER], *[NM[n] for n in ORDER],
            *[NV[n] for n in ORDER])
```

```python
import functools
import math

import jax
import jax.numpy as jnp
from jax import lax
from jax.experimental import pallas as pl
from jax.experimental.pallas import tpu as pltpu

F32 = jnp.float32
BF16 = jnp.bfloat16
EPS = 1e-6
BLK = 128
N_CHIPS = 4
N_DEV = 8
ADAM_LR, ADAM_B1, ADAM_B2, ADAM_EPS, ADAM_WD, ADAM_STEP = 0.001, 0.9, 0.999, 1e-08, 0.01, 10
VMEM_BIG = 48 * 1024 * 1024
MESH = pl.DeviceIdType.MESH
ANY = pl.BlockSpec(memory_space=pl.ANY)


def _pick(dim, prefs):
    for t in prefs:
        if dim % t == 0:
            return t
    raise ValueError(f"no tile in {prefs} divides {dim}")


def _params(sem=None, vmem=None):
    return pltpu.CompilerParams(dimension_semantics=sem, vmem_limit_bytes=vmem)


def _ldims(arr, split):
    if split == 1:
        return arr.shape
    p, r, cs = arr.shape
    assert p == split
    return (r, p * cs)


def _spec(tr, tc, split, cols, rc):
    if split == 1:
        return pl.BlockSpec((tr, tc), lambda i, j, k: rc(i, j, k))
    per = (cols // split) // tc

    def imap(i, j, k):
        r, c = rc(i, j, k)
        return (c // per, r, c % per)

    return pl.BlockSpec((None, tr, tc), imap)


def _fit(unit, cap):
    return max(t for t in range(128, min(unit, cap) + 1, 128) if unit % t == 0)


def _mm(a, b, mode, *, name, caps, a_split=1, b_split=1, o_split=1, out_dtype=F32, res=None, dep=None):
    ar, ac = _ldims(a, a_split)
    br, bc = _ldims(b, b_split)
    if mode == "nn":
        m, k, n = ar, ac, bc
        assert br == k
        ku, nu, mu = math.gcd(k // a_split, k), math.gcd(n // b_split, n // o_split), m
    elif mode == "nt":
        m, k, n = ar, ac, br
        assert bc == k
        ku, nu, mu = math.gcd(k // a_split, k // b_split), n // o_split, m
    else:
        k, m, n = ar, ac, bc
        assert br == k
        ku, nu, mu = k, math.gcd(n // b_split, n // o_split), m // a_split
    tm, tn, tk = _fit(mu, caps[0]), _fit(nu, caps[1]), _fit(ku, caps[2])
    nk = k // tk
    if mode == "nn":
        a_spec = _spec(tm, tk, a_split, k, lambda i, j, kk: (i, kk))
        b_spec = _spec(tk, tn, b_split, n, lambda i, j, kk: (kk, j))
    elif mode == "nt":
        a_spec = _spec(tm, tk, a_split, k, lambda i, j, kk: (i, kk))
        b_spec = _spec(tn, tk, b_split, k, lambda i, j, kk: (j, kk))
    else:
        a_spec = _spec(tk, tm, a_split, m, lambda i, j, kk: (kk, i))
        b_spec = _spec(tk, tn, b_split, n, lambda i, j, kk: (kk, j))
    o_spec = _spec(tm, tn, o_split, n, lambda i, j, kk: (i, j))
    dims = {"nn": (((1,), (0,)), ((), ())), "nt": (((1,), (1,)), ((), ())), "tn": (((0,), (0,)), ((), ()))}[mode]

    def body(a_ref, b_ref, *rest):
        if dep is not None:
            rest = rest[1:]
        if res is None:
            o_ref, acc = rest
        else:
            r_ref, o_ref, acc = rest
        kk = pl.program_id(2)

        @pl.when(kk == 0)
        def _():
            acc[...] = jnp.zeros_like(acc)

        acc[...] += lax.dot_general(a_ref[...].astype(BF16), b_ref[...].astype(BF16), dims, preferred_element_type=F32)

        @pl.when(kk == nk - 1)
        def _():
            out = acc[...]
            if res is not None:
                out = out + r_ref[...]
            o_ref[...] = out.astype(o_ref.dtype)

    in_specs, args = [a_spec, b_spec], [a, b]
    if dep is not None:
        in_specs.append(ANY)
        args.append(dep)
    if res is not None:
        in_specs.append(pl.BlockSpec((tm, tn), lambda i, j, kk: (i, j)))
        args.append(res)
    out_shape = (m, n) if o_split == 1 else (o_split, m, n // o_split)
    return pl.pallas_call(
        body, name=name, grid=(m // tm, n // tn, nk), in_specs=in_specs, out_specs=o_spec,
        out_shape=jax.ShapeDtypeStruct(out_shape, out_dtype), scratch_shapes=[pltpu.VMEM((tm, tn), F32)],
        compiler_params=_params(("parallel", "parallel", "arbitrary"), VMEM_BIG),
    )(*args)


def _rstd(v):
    return lax.rsqrt(jnp.mean(v * v, axis=-1, keepdims=True) + EPS)


def _norm_bwd(v, r, gain, dout):
    a = dout * gain
    dv = r * (a - v * (r * r * jnp.mean(a * v, axis=-1, keepdims=True)))
    return dv, dout * v * r


def _rmsnorm_fwd(x, g, layer, *, name):
    s, d = x.shape
    tr = _pick(s, (256, 128))

    def body(x_ref, g_ref, o_ref):
        v = x_ref[...]
        o_ref[...] = (v * _rstd(v) * g_ref[...]).astype(o_ref.dtype)

    return pl.pallas_call(
        body, name=name, grid=(s // tr,),
        in_specs=[pl.BlockSpec((tr, d), lambda i: (i, 0)), pl.BlockSpec((None, 1, d), lambda i: (layer, 0, 0))],
        out_specs=pl.BlockSpec((tr, d), lambda i: (i, 0)), out_shape=jax.ShapeDtypeStruct((s, d), BF16),
        compiler_params=_params(("parallel",)),
    )(x, g.reshape(-1, 1, d))


def _rmsnorm_bwd(x, g, layer, dh, dres, *, name):
    s, d = x.shape
    tr = _pick(s, (512, 256, 128))

    def body(x_ref, g_ref, dh_ref, dres_ref, dx_ref, dxb_ref, dg_ref):
        v = x_ref[...]
        dv, dgr = _norm_bwd(v, _rstd(v), g_ref[...], dh_ref[...])
        dx = dres_ref[...] + dv
        dx_ref[...] = dx
        dxb_ref[...] = dx.astype(BF16)
        part = jnp.sum(dgr, axis=0, keepdims=True)

        @pl.when(pl.program_id(0) == 0)
        def _():
            dg_ref[...] = part

        @pl.when(pl.program_id(0) > 0)
        def _():
            dg_ref[...] += part

    row = pl.BlockSpec((tr, d), lambda i: (i, 0))
    one = pl.BlockSpec((1, d), lambda i: (0, 0))
    return pl.pallas_call(
        body, name=name, grid=(s // tr,), in_specs=[row, pl.BlockSpec((None, 1, d), lambda i: (layer, 0, 0)), row, row],
        out_specs=[row, row, one],
        out_shape=[jax.ShapeDtypeStruct((s, d), F32), jax.ShapeDtypeStruct((s, d), BF16), jax.ShapeDtypeStruct((1, d), F32)],
        compiler_params=_params(("arbitrary",), VMEM_BIG),
    )(x, g.reshape(-1, 1, d), dh, dres)


def _loss_head(y, target, *, name):
    s, d = y.shape
    tr = _pick(s, (256, 128))

    def body(y_ref, t_ref, dy_ref, dyb_ref, ls_ref):
        e = y_ref[...] - t_ref[...]
        dy = e * (1.0 / d)
        dy_ref[...] = dy
        dyb_ref[...] = dy.astype(BF16)
        part = jnp.full(ls_ref.shape, jnp.sum(e * e), F32)

        @pl.when(pl.program_id(0) == 0)
        def _():
            ls_ref[...] = part

        @pl.when(pl.program_id(0) > 0)
        def _():
            ls_ref[...] += part

    row = pl.BlockSpec((tr, d), lambda i: (i, 0))
    return pl.pallas_call(
        body, name=name, grid=(s // tr,), in_specs=[row, row], out_specs=[row, row, pl.BlockSpec((8, 128), lambda i: (0, 0))],
        out_shape=[jax.ShapeDtypeStruct((s, d), F32), jax.ShapeDtypeStruct((s, d), BF16), jax.ShapeDtypeStruct((8, 128), F32)],
        compiler_params=_params(("arbitrary",)),
    )(y, target)


def _iota2(axis):
    return lax.broadcasted_iota(jnp.int32, (BLK, BLK), axis)


def _tri_sum(v, tri):
    hi = v.astype(BF16)
    lo = (v - hi.astype(F32)).astype(BF16)
    return jnp.dot(hi, tri, preferred_element_type=F32) + jnp.dot(lo, tri, preferred_element_type=F32)


def _dot_nt(a, b):
    return lax.dot_general(a, b, (((1,), (1,)), ((), ())), preferred_element_type=F32)


def _dot_tn(a, b):
    return lax.dot_general(a, b, (((0,), (0,)), ((), ())), preferred_element_type=F32)


TQ_MAX = 1024
TQ_MAX_BWD = 1024
HP = 2
VMEM_ATTN_BWD = 56 * 1024 * 1024


def _lanes(hh):
    return slice(hh * BLK, (hh + 1) * BLK)


def _causal(n, diag):
    if not diag:
        return None
    return lax.broadcasted_iota(jnp.int32, (n, BLK), 1) < lax.broadcasted_iota(jnp.int32, (n, BLK), 0)


def _sb_sums(z, mask, rhs_gt):
    lb = jnp.minimum(z, 0.0) - jnp.log(1.0 + jnp.exp(-jnp.abs(z)))
    l1m = lb - z
    if mask is not None:
        l1m = jnp.where(mask, l1m, 0.0)
    return lb, _tri_sum(l1m, rhs_gt)


def _below(old, new, r0):
    return new if r0 == 0 else jnp.concatenate([old[:r0], new], axis=0)


def _attn_fwd(p, gq, gk, go, layer, n_heads, *, name):
    s = p.shape[0]
    tq = min(TQ_MAX, s)
    per = tq // BLK
    scale = BLK ** -0.5

    def body(q_ref, k_ref, v_ref, gq_ref, gk_ref, go_ref, att_ref, o_ref, l_ref, qn, kn, vb):
        for hh in range(HP):
            q = q_ref[:, _lanes(hh)]
            k = k_ref[:, _lanes(hh)]
            qn[:, _lanes(hh)] = (q * _rstd(q) * gq_ref[...] * scale).astype(BF16)
            kn[:, _lanes(hh)] = (k * _rstd(k) * gk_ref[...]).astype(BF16)
            vb[:, _lanes(hh)] = v_ref[:, _lanes(hh)].astype(BF16)
        rhs_gt = jnp.concatenate([(_iota2(0) > _iota2(1)).astype(BF16), jnp.ones((BLK, BLK), BF16)], axis=1)

        def step(q0, j, r0, diag, states):
            n = tq - r0
            rows = pl.ds(pl.multiple_of(q0 + r0, BLK), n)
            cols = pl.ds(pl.multiple_of(j * BLK, BLK), BLK)
            mask = _causal(n, diag)
            zs = [_dot_nt(qn[rows, _lanes(hh)], kn[cols, _lanes(hh)]) for hh in range(HP)]
            sums = [_sb_sums(z, mask, rhs_gt) for z in zs]
            new = []
            for hh in range(HP):
                acc, later = states[hh]
                lb, both = sums[hh]
                a = jnp.exp(lb + both[:, :BLK] + later[r0:])
                if diag:
                    a = jnp.where(mask, a, 0.0)
                acc_new = acc[r0:] + jnp.dot(a.astype(BF16), vb[cols, _lanes(hh)], preferred_element_type=F32)
                new.append((_below(acc, acc_new, r0), _below(later, later[r0:] + both[:, BLK:], r0)))
            return tuple(new)

        def q_block(i, _):
            q0 = i * tq
            zero = jnp.zeros((tq, BLK), F32)
            states = ((zero, zero),) * HP
            for jd in reversed(range(per)):
                states = step(q0, i * per + jd, jd * BLK, True, states)
            states = lax.fori_loop(0, i * per, lambda jj, st: step(q0, i * per - 1 - jj, 0, False, st), states)
            tile = pl.ds(pl.multiple_of(q0, tq), tq)
            for hh in range(HP):
                o, total = states[hh]
                o_ref[tile, _lanes(hh)] = o
                l_ref[hh, tile, :] = total
                att_ref[tile, _lanes(hh)] = (o * _rstd(o) * go_ref[hh]).astype(att_ref.dtype)
            return 0

        lax.fori_loop(0, s // tq, q_block, 0)

    assert n_heads % HP == 0
    groups = n_heads // HP

    def col(part):
        return pl.BlockSpec((s, HP * BLK), lambda g: (0, part * groups + g))

    gain = pl.BlockSpec((None, 1, BLK), lambda g: (layer, 0, 0))
    per_head = pl.BlockSpec((None, HP, 1, BLK), lambda g: (layer, g, 0, 0))
    return pl.pallas_call(
        body, name=name, grid=(groups,),
        in_specs=[col(0), col(1), col(2), gain, gain, per_head],
        out_specs=[pl.BlockSpec((None, s, HP * BLK), lambda g: (0, 0, g)), col(0), pl.BlockSpec((HP, s, BLK), lambda g: (g, 0, 0))],
        out_shape=[jax.ShapeDtypeStruct((2, s, n_heads * BLK), BF16), jax.ShapeDtypeStruct((s, n_heads * BLK), F32),
                   jax.ShapeDtypeStruct((n_heads, s, BLK), F32)],
        scratch_shapes=[pltpu.VMEM((s, HP * BLK), BF16)] * 3,
        compiler_params=_params(("parallel",), VMEM_BIG),
    )(p, p, p, gq.reshape(-1, 1, BLK), gk.reshape(-1, 1, BLK), go.reshape(-1, n_heads, 1, BLK))


def _attn_bwd(p, o_raw, lsum, dmix, gq, gk, go, layer, n_heads, *, name):
    s = p.shape[0]
    tq = min(TQ_MAX_BWD, s)
    per = tq // BLK
    scale = BLK ** -0.5

    def body(q_ref, k_ref, v_ref, o_ref, l_ref, da_ref, gq_ref, gk_ref, go_ref,
             dqkv_ref, dgq_ref, dgk_ref, dgo_ref, qn, kn, vb, dob, dqn, dkn, dvv):
        for hh in range(HP):
            q = q_ref[:, _lanes(hh)]
            k = k_ref[:, _lanes(hh)]
            qn[:, _lanes(hh)] = (q * _rstd(q) * gq_ref[...] * scale).astype(BF16)
            kn[:, _lanes(hh)] = (k * _rstd(k) * gk_ref[...]).astype(BF16)
            vb[:, _lanes(hh)] = v_ref[:, _lanes(hh)].astype(BF16)
            o = o_ref[:, _lanes(hh)]
            do, dgo_rows = _norm_bwd(o, _rstd(o), go_ref[hh], da_ref[:, _lanes(hh)])
            dob[:, _lanes(hh)] = do.astype(BF16)
            dgo_ref[hh] = jnp.sum(dgo_rows, axis=0, keepdims=True)
        dkn[...] = jnp.zeros_like(dkn)
        dvv[...] = jnp.zeros_like(dvv)
        ones = jnp.ones((BLK, BLK), BF16)
        rhs_gt = jnp.concatenate([(_iota2(0) > _iota2(1)).astype(BF16), ones], axis=1)
        rhs_lt = jnp.concatenate([(_iota2(0) < _iota2(1)).astype(BF16), ones], axis=1)

        def step(q0, j, r0, diag, states):
            n = tq - r0
            rows = pl.ds(pl.multiple_of(q0 + r0, BLK), n)
            cols = pl.ds(pl.multiple_of(j * BLK, BLK), BLK)
            mask = _causal(n, diag)
            zs = [_dot_nt(qn[rows, _lanes(hh)], kn[cols, _lanes(hh)]) for hh in range(HP)]
            das = [_dot_nt(dob[rows, _lanes(hh)], vb[cols, _lanes(hh)]) for hh in range(HP)]
            sums = [_sb_sums(z, mask, rhs_gt) for z in zs]
            mids = []
            for hh in range(HP):
                lb, both = sums[hh]
                upto = states[hh][0][r0:] + both[:, BLK:]
                a = jnp.exp(lb + both[:, :BLK] + (l_ref[hh, rows, :] - upto))
                if diag:
                    a = jnp.where(mask, a, 0.0)
                g = das[hh] * a
                mids.append((lb, upto, a, g, _tri_sum(g, rhs_lt)))
            new = []
            for hh in range(HP):
                seen, gsum, dq = states[hh]
                lb, upto, a, g, bothg = mids[hh]
                beta = jnp.exp(lb)
                dz = g * (1.0 - beta) - beta * (bothg[:, :BLK] + gsum[r0:])
                if diag:
                    dz = jnp.where(mask, dz, 0.0)
                dzs = dz.astype(BF16)
                dq_new = dq[r0:] + jnp.dot(dzs, kn[cols, _lanes(hh)], preferred_element_type=F32)
                dkn[cols, _lanes(hh)] += _dot_tn(dzs, qn[rows, _lanes(hh)])
                dvv[cols, _lanes(hh)] += _dot_tn(a.astype(BF16), dob[rows, _lanes(hh)])
                new.append((_below(seen, upto, r0), _below(gsum, gsum[r0:] + bothg[:, BLK:], r0), _below(dq, dq_new, r0)))
            return tuple(new)

        def q_block(i, _):
            q0 = i * tq
            zero = jnp.zeros((tq, BLK), F32)
            states = ((zero, zero, zero),) * HP
            states = lax.fori_loop(0, i * per, lambda j, st: step(q0, j, 0, False, st), states)
            for jd in range(per):
                states = step(q0, i * per + jd, jd * BLK, True, states)
            tile = pl.ds(pl.multiple_of(q0, tq), tq)
            for hh in range(HP):
                dqn[tile, _lanes(hh)] = states[hh][2]
            return 0

        lax.fori_loop(0, s // tq, q_block, 0)
        for hh in range(HP):
            q = q_ref[:, _lanes(hh)]
            k = k_ref[:, _lanes(hh)]
            dq_raw, dgq_rows = _norm_bwd(q, _rstd(q), gq_ref[...], dqn[:, _lanes(hh)] * scale)
            dk_raw, dgk_rows = _norm_bwd(k, _rstd(k), gk_ref[...], dkn[:, _lanes(hh)])
            dqkv_ref[0, :, _lanes(hh)] = dq_raw.astype(BF16)
            dqkv_ref[1, :, _lanes(hh)] = dk_raw.astype(BF16)
            dqkv_ref[2, :, _lanes(hh)] = dvv[:, _lanes(hh)].astype(BF16)
            dgq_ref[hh] = jnp.sum(dgq_rows, axis=0, keepdims=True)
            dgk_ref[hh] = jnp.sum(dgk_rows, axis=0, keepdims=True)

    assert n_heads % HP == 0
    groups = n_heads // HP

    once = pl.Buffered(1)

    def col(part):
        return pl.BlockSpec((s, HP * BLK), lambda g: (0, part * groups + g), pipeline_mode=once)

    gain = pl.BlockSpec((None, 1, BLK), lambda g: (layer, 0, 0))
    per_head = pl.BlockSpec((HP, 1, BLK), lambda g: (g, 0, 0))
    head_gain = jax.ShapeDtypeStruct((n_heads, 1, BLK), F32)
    return pl.pallas_call(
        body, name=name, grid=(groups,),
        in_specs=[col(0), col(1), col(2), col(0), pl.BlockSpec((HP, s, BLK), lambda g: (g, 0, 0), pipeline_mode=once), col(0),
                  gain, gain, pl.BlockSpec((None, HP, 1, BLK), lambda g: (layer, g, 0, 0))],
        out_specs=[pl.BlockSpec((3, s, HP * BLK), lambda g: (0, 0, g)), per_head, per_head, per_head],
        out_shape=[jax.ShapeDtypeStruct((3, s, n_heads * BLK), BF16), head_gain, head_gain, head_gain],
        scratch_shapes=[pltpu.VMEM((s, HP * BLK), BF16)] * 4 + [pltpu.VMEM((s, HP * BLK), F32)] * 3,
        compiler_params=_params(("parallel",), VMEM_ATTN_BWD),
    )(p, p, p, o_raw, lsum, dmix, gq.reshape(-1, 1, BLK), gk.reshape(-1, 1, BLK), go.reshape(-1, n_heads, 1, BLK))


SGU_TOGETHER = 4
_INV_SQRT2 = 0.7071067811865476
_INV_SQRT2PI = 0.3989422804014327


def _gelu(x):
    return 0.5 * x * (1.0 + lax.erf(x * _INV_SQRT2))


def _gelu_and_grad(x):
    cdf = 0.5 * (1.0 + lax.erf(x * _INV_SQRT2))
    return x * cdf, cdf + x * jnp.exp(-0.5 * x * x) * _INV_SQRT2PI


def _sgu_fwd(p, mix, w, b, gv, gout, layer, n_heads, *, name):
    s = p.shape[0]
    n_groups = w.shape[1]
    nb = s // BLK
    assert mix.shape == (2, s, n_groups * BLK)

    def body(u_ref, v_ref, w_ref, b_ref, gv_ref, go_ref, _mix_ref, out_ref):
        wt = jnp.where(_iota2(0) >= _iota2(1), w_ref[...], 0.0).astype(BF16)
        bias = b_ref[...]

        def chunks(i, _):
            rows = [pl.ds(pl.multiple_of((i * SGU_TOGETHER + k) * BLK, BLK), BLK) for k in range(SGU_TOGETHER)]
            us = [_gelu(u_ref[r, :]) for r in rows]
            vss = []
            for r in rows:
                vv = _gelu(v_ref[r, :])
                vss.append((vv * _rstd(vv) * gv_ref[...]).astype(BF16))
            mixed = [jnp.dot(wt, vs, preferred_element_type=F32) + bias for vs in vss]
            for r, u, m in zip(rows, us, mixed):
                gated = u * m
                out_ref[r, :] = (gated * _rstd(gated) * go_ref[...]).astype(out_ref.dtype)
            return 0

        assert nb % SGU_TOGETHER == 0
        lax.fori_loop(0, nb // SGU_TOGETHER, chunks, 0)

    def col(off):
        return pl.BlockSpec((s, BLK), lambda g: (0, off + g))

    per_group = pl.BlockSpec((None, None, 1, BLK), lambda g: (layer, g, 0, 0))
    return pl.pallas_call(
        body, name=name, grid=(n_groups,),
        in_specs=[col(3 * n_heads), col(3 * n_heads + n_groups), pl.BlockSpec((None, None, BLK, BLK), lambda g: (layer, g, 0, 0)),
                  pl.BlockSpec((None, None, BLK, 1), lambda g: (layer, g, 0, 0)), per_group, per_group, ANY],
        out_specs=pl.BlockSpec((None, s, BLK), lambda g: (1, 0, g)), out_shape=jax.ShapeDtypeStruct(mix.shape, mix.dtype),
        input_output_aliases={6: 0}, compiler_params=_params(("parallel",), VMEM_BIG),
    )(p, p, w, b.reshape(-1, n_groups, BLK, 1), gv.reshape(-1, n_groups, 1, BLK), gout.reshape(-1, n_groups, 1, BLK), mix)


def _sgu_bwd(p, dmix, w, b, gv, gout, layer, n_heads, *, name):
    s = p.shape[0]
    n_groups = w.shape[1]
    nb = s // BLK

    def body(u_ref, v_ref, ds_ref, w_ref, b_ref, gv_ref, go_ref, duv_ref, dw_ref, db_ref, dgv_ref, dgo_ref):
        lower = _iota2(0) >= _iota2(1)
        wt = jnp.where(lower, w_ref[...], 0.0).astype(BF16)
        bias = b_ref[...]

        def chunks(i, carry):
            dw, db, dgv, dgo = carry
            rows = [pl.ds(pl.multiple_of((i * SGU_TOGETHER + k) * BLK, BLK), BLK) for k in range(SGU_TOGETHER)]
            pre = []
            for r in rows:
                u, u_grad = _gelu_and_grad(u_ref[r, :])
                vv, vv_grad = _gelu_and_grad(v_ref[r, :])
                rv = _rstd(vv)
                pre.append((u, u_grad, vv, vv_grad, rv, (vv * rv * gv_ref[...]).astype(BF16)))
            mixed = [jnp.dot(wt, t[5], preferred_element_type=F32) + bias for t in pre]
            mid = []
            for r, t, m in zip(rows, pre, mixed):
                gated = t[0] * m
                dgated, dgo_rows = _norm_bwd(gated, _rstd(gated), go_ref[...], ds_ref[r, :])
                dmixed = dgated * t[0]
                duv_ref[0, r, :] = (dgated * m * t[1]).astype(BF16)
                dgo = dgo + jnp.sum(dgo_rows, axis=0, keepdims=True)
                db = db + jnp.sum(dmixed, axis=1, keepdims=True)
                mid.append(dmixed.astype(BF16))
            dvss = [_dot_tn(wt, dmb) for dmb in mid]
            for dmb, t in zip(mid, pre):
                dw = dw + _dot_nt(dmb, t[5])
            for r, t, dvs in zip(rows, pre, dvss):
                dvv, dgv_rows = _norm_bwd(t[2], t[4], gv_ref[...], dvs)
                duv_ref[1, r, :] = (dvv * t[3]).astype(BF16)
                dgv = dgv + jnp.sum(dgv_rows, axis=0, keepdims=True)
            return dw, db, dgv, dgo

        assert nb % SGU_TOGETHER == 0
        row0 = jnp.zeros((1, BLK), F32)
        dw, db, dgv, dgo = lax.fori_loop(0, nb // SGU_TOGETHER, chunks,
                                         (jnp.zeros((BLK, BLK), F32), jnp.zeros((BLK, 1), F32), row0, row0))
        dw_ref[...] = jnp.where(lower, dw, 0.0)
        db_ref[...] = db
        dgv_ref[...] = dgv
        dgo_ref[...] = dgo

    def col(off):
        return pl.BlockSpec((s, BLK), lambda g: (0, off + g))

    per_group = pl.BlockSpec((None, 1, BLK), lambda g: (g, 0, 0))
    square = pl.BlockSpec((None, BLK, BLK), lambda g: (g, 0, 0))
    column = pl.BlockSpec((None, BLK, 1), lambda g: (g, 0, 0))
    gain = jax.ShapeDtypeStruct((n_groups, 1, BLK), F32)
    return pl.pallas_call(
        body, name=name, grid=(n_groups,),
        in_specs=[col(3 * n_heads), col(3 * n_heads + n_groups), col(n_heads),
                  pl.BlockSpec((None, None, BLK, BLK), lambda g: (layer, g, 0, 0)),
                  pl.BlockSpec((None, None, BLK, 1), lambda g: (layer, g, 0, 0)),
                  pl.BlockSpec((None, None, 1, BLK), lambda g: (layer, g, 0, 0)),
                  pl.BlockSpec((None, None, 1, BLK), lambda g: (layer, g, 0, 0))],
        out_specs=[pl.BlockSpec((2, s, BLK), lambda g: (0, 0, g)), square, column, per_group, per_group],
        out_shape=[jax.ShapeDtypeStruct((2, s, n_groups * BLK), BF16), jax.ShapeDtypeStruct((n_groups, BLK, BLK), F32),
                   jax.ShapeDtypeStruct((n_groups, BLK, 1), F32), gain, gain],
        compiler_params=_params(("parallel",), VMEM_BIG),
    )(p, p, dmix, w, b.reshape(-1, n_groups, BLK, 1), gv.reshape(-1, n_groups, 1, BLK), gout.reshape(-1, n_groups, 1, BLK))


CONV_ROWS = 256
HALO = 8


def _shift_down(ref, r0, n, first):
    cur = ref[pl.ds(r0, n), :]
    prev = jnp.zeros((HALO, cur.shape[1]), F32) if first else ref[pl.ds(r0 - HALO, HALO), :]
    ext = jnp.concatenate([prev, cur], axis=0)
    return pltpu.roll(ext, 1, 0)[HALO:], pltpu.roll(ext, 2, 0)[HALO:], cur


def _shift_up(ref, r0, n, last):
    cur = ref[pl.ds(r0, n), :]
    nxt = jnp.zeros((HALO, cur.shape[1]), F32) if last else ref[pl.ds(r0 + n, HALO), :]
    ext = jnp.concatenate([cur, nxt], axis=0)
    return cur, pltpu.roll(ext, n + HALO - 1, 0)[:n], pltpu.roll(ext, n + HALO - 2, 0)[:n]


def _conv_rows(x1, x2, x0, w_ref, b_ref):
    return ((b_ref[...] + x2 * w_ref[0:1, :]) + x1 * w_ref[1:2, :]) + x0 * w_ref[2:3, :]


def _conv_specs(s, f, tc, layer):
    nf = f // tc
    gate = pl.BlockSpec((s, tc), lambda n: (0, n))
    wg = pl.BlockSpec((None, 3, tc), lambda n: (layer, 0, n))
    wv = pl.BlockSpec((None, 3, tc), lambda n: (layer, 0, nf + n))
    bg = pl.BlockSpec((None, 1, tc), lambda n: (layer, 0, n))
    bv = pl.BlockSpec((None, 1, tc), lambda n: (layer, 0, nf + n))
    return nf, gate, wg, wv, bg, bv


def _up_conv_fwd(h, w_up, cw, cb, layer, *, name):
    s, d = h.shape
    chips, _, per_chip = w_up.shape
    f = chips * per_chip // 2
    tc = _pick(math.gcd(f, per_chip), (256, 128))
    cr = min(CONV_ROWS, s)
    nf, gate, wg, wv, bg, bv = _conv_specs(s, f, tc, layer)
    per = per_chip // tc

    def body(h_ref, mg_ref, mv_ref, wg_ref, wv_ref, bg_ref, bv_ref, ug_ref, uv_ref, out_ref):
        hb = h_ref[...]
        ug_ref[...] = jnp.dot(hb, mg_ref[...], preferred_element_type=F32)
        uv_ref[...] = jnp.dot(hb, mv_ref[...], preferred_element_type=F32)
        for r0 in range(0, s, cr):
            gc = _conv_rows(*_shift_down(ug_ref, r0, cr, r0 == 0), wg_ref, bg_ref)
            vc = _conv_rows(*_shift_down(uv_ref, r0, cr, r0 == 0), wv_ref, bv_ref)
            out_ref[pl.ds(r0, cr), :] = (gc * jax.nn.sigmoid(gc) * vc).astype(out_ref.dtype)

    def cols(first):
        return pl.BlockSpec((None, d, tc), lambda n: ((first + n) // per, 0, (first + n) % per))

    half = jax.ShapeDtypeStruct((s, f), F32)
    return pl.pallas_call(
        body, name=name, grid=(nf,), in_specs=[pl.BlockSpec((s, d), lambda n: (0, 0)), cols(0), cols(nf), wg, wv, bg, bv],
        out_specs=[gate, gate, gate], out_shape=[half, half, jax.ShapeDtypeStruct((s, f), BF16)],
        compiler_params=_params(("arbitrary",), VMEM_BIG),
    )(h, w_up, w_up, cw, cw, cb.reshape(-1, 1, 2 * f), cb.reshape(-1, 1, 2 * f))


def _conv_bwd(up_gate, up_val, dy, w_down, cw, cb, layer, *, name):
    s, f = up_gate.shape
    d = dy.shape[1]
    f2 = 2 * f
    tc = _pick(f, (256, 128))
    cr = min(CONV_ROWS, s)
    nf, gate, wg, wv, bg, bv = _conv_specs(s, f, tc, layer)

    def body(g_ref, v_ref, dy_ref, wd_first, wd_next, wg_ref, wv_ref, bg_ref, bv_ref, dup_ref, dw_ref, db_ref,
             dgc, dvc, da_ref, da_next):
        @pl.when(pl.program_id(0) == 0)
        def _():
            da_ref[...] = _dot_nt(dy_ref[...], wd_first[...])

        da_next[...] = _dot_nt(dy_ref[...], wd_next[...])
        zero = jnp.zeros((1, tc), F32)
        sums = [[zero] * 4, [zero] * 4]
        for r0 in range(0, s, cr):
            rows = pl.ds(r0, cr)
            gx = _shift_down(g_ref, r0, cr, r0 == 0)
            vx = _shift_down(v_ref, r0, cr, r0 == 0)
            gc = _conv_rows(*gx, wg_ref, bg_ref)
            vc = _conv_rows(*vx, wv_ref, bv_ref)
            sig = jax.nn.sigmoid(gc)
            da = da_ref[rows, :]
            d_gate = da * vc * (sig * (1.0 + gc * (1.0 - sig)))
            d_val = da * (gc * sig)
            dgc[rows, :] = d_gate
            dvc[rows, :] = d_val
            for part, (dc, (x1, x2, x0)) in enumerate(((d_gate, gx), (d_val, vx))):
                for tap, xs in enumerate((x2, x1, x0)):
                    sums[part][tap] = sums[part][tap] + jnp.sum(dc * xs, axis=0, keepdims=True)
                sums[part][3] = sums[part][3] + jnp.sum(dc, axis=0, keepdims=True)
        dw_ref[...] = jnp.zeros_like(dw_ref)
        db_ref[...] = jnp.zeros_like(db_ref)
        for part, (dc_ref, w_ref) in enumerate(((dgc, wg_ref), (dvc, wv_ref))):
            for tap in range(3):
                dw_ref[part, tap:tap + 1, :] = sums[part][tap]
            db_ref[part, 0:1, :] = sums[part][3]
            for r0 in range(0, s, cr):
                d0, d1, d2 = _shift_up(dc_ref, r0, cr, r0 + cr == s)
                dup_ref[part, pl.ds(r0, cr), :] = ((d0 * w_ref[2:3, :] + d1 * w_ref[1:2, :]) + d2 * w_ref[0:1, :]).astype(BF16)
        da_ref[...] = da_next[...]

    small = pl.BlockSpec((2, 8, tc), lambda n: (0, 0, n))
    return pl.pallas_call(
        body, name=name, grid=(nf,),
        in_specs=[gate, gate, pl.BlockSpec((s, d), lambda n: (0, 0)), pl.BlockSpec((tc, d), lambda n: (0, 0)),
                  pl.BlockSpec((tc, d), lambda n: (jnp.minimum(n + 1, nf - 1), 0)), wg, wv, bg, bv],
        out_specs=[pl.BlockSpec((2, s, tc), lambda n: (0, 0, n)), small, small],
        out_shape=[jax.ShapeDtypeStruct((2, s, f), BF16), jax.ShapeDtypeStruct((2, 8, f), F32),
                   jax.ShapeDtypeStruct((2, 8, f), F32)],
        scratch_shapes=[pltpu.VMEM((s, tc), F32)] * 4,
        compiler_params=_params(("arbitrary",), VMEM_BIG),
    )(up_gate, up_val, dy, w_down, w_down, cw, cw, cb.reshape(-1, 1, f2), cb.reshape(-1, 1, f2))


def _adamw(w, g, m, v, *, name, dep=None, copy_g=False):
    shape = w.shape
    cols = shape[-1]
    rows = w.size // cols
    if rows * cols * 4 <= (2 << 20):
        tr = rows
    else:
        tr = next(t for t in (1024, 512, 256, 128, 64, 32, 16, 8) if rows % t == 0 and (t * cols * 4 <= (2 << 20) or t == 8))

    n_out = 4 if copy_g else 3

    def body(w_ref, g_ref, m_ref, v_ref, *rest):
        d_ref, nm_ref, nv_ref = rest[-n_out:][:3]
        gr = g_ref[...]
        if copy_g:
            rest[-1][...] = gr
        nm = ADAM_B1 * m_ref[...] + (1.0 - ADAM_B1) * gr
        nv = ADAM_B2 * v_ref[...] + (1.0 - ADAM_B2) * (gr * gr)
        m_hat = nm / (1.0 - ADAM_B1 ** ADAM_STEP)
        v_hat = nv / (1.0 - ADAM_B2 ** ADAM_STEP)
        d_ref[...] = -ADAM_LR * (m_hat / (jnp.sqrt(v_hat) + ADAM_EPS) + ADAM_WD * w_ref[...])
        nm_ref[...] = nm
        nv_ref[...] = nv

    blk = pl.BlockSpec((tr, cols), lambda i: (i, 0))
    out = jax.ShapeDtypeStruct((rows, cols), F32)
    res = pl.pallas_call(
        body, name=name, grid=(rows // tr,), in_specs=[blk] * 4 + ([] if dep is None else [ANY]), out_specs=[blk] * n_out,
        out_shape=[out] * n_out, compiler_params=_params(("parallel",), VMEM_BIG),
    )(*[t.reshape(rows, cols) for t in (w, g, m, v)], *([] if dep is None else [dep]))
    return [t.reshape(shape) for t in res]


def _place():
    x, y, c = lax.axis_index("x"), lax.axis_index("y"), lax.axis_index("c")
    others = [(1 - x, y), (x, 1 - y), (1 - x, 1 - y)]
    return x, y, c, others


def _remote(src, dst, send_sem, recv_sem, device):
    return pltpu.make_async_remote_copy(src_ref=src, dst_ref=dst, send_sem=send_sem, recv_sem=recv_sem, device_id=device,
                                        device_id_type=MESH)


def _cast_into(w, layer, place, *, name, dep=None):
    _, r, cols = w.shape
    tr = _row_tile(r, cols)

    def body(place_ref, w_ref, *rest):
        rest[-1][...] = w_ref[...].astype(BF16)

    in_specs, args = [pl.BlockSpec((None, tr, cols), lambda i, pr: (layer, i, 0))], [place, w]
    if dep is not None:
        in_specs.append(ANY)
        args.append(dep)
    grid_spec = pltpu.PrefetchScalarGridSpec(
        num_scalar_prefetch=1, grid=(r // tr,), in_specs=in_specs,
        out_specs=pl.BlockSpec((None, tr, cols), lambda i, pr: (pr[1], i, 0)),
    )
    return pl.pallas_call(
        body, name=name, grid_spec=grid_spec, out_shape=jax.ShapeDtypeStruct((N_CHIPS, r, cols), BF16),
        compiler_params=_params(("parallel",), VMEM_BIG),
    )(*args)


HBM = pl.BlockSpec(memory_space=pltpu.HBM)
SEM = pl.BlockSpec(memory_space=pltpu.SEMAPHORE)
EFFECT = pltpu.SideEffectType.DATAFLOW_SIDE_EFFECTING
TOKEN = jax.ShapeDtypeStruct((8, 128), F32)


def _in_hbm(t):
    return pltpu.with_memory_space_constraint(t, pltpu.HBM)


def _gather_copies(buf, send, recv):
    x, y, c, others = _place()
    half = buf.shape[1] // 2
    rows = pl.ds(c * half, half)
    return [_remote(buf.at[2 * x + y, rows], buf.at[2 * x + y, rows], send.at[j], recv.at[j], (px, py, c))
            for j, (px, py) in enumerate(others)]


def _gather_start(bufs, *, name):
    n = len(bufs)

    def body(*refs):
        ins, sends, recvs, token = refs[:n], refs[n:2 * n], refs[2 * n:3 * n], refs[4 * n]
        for a in range(n):
            for cp in _gather_copies(ins[a], sends[a], recvs[a]):
                cp.start()
        token[...] = jnp.zeros_like(token)

    sems = [pltpu.SemaphoreType.DMA((3,))] * (2 * n)
    res = pl.pallas_call(
        body, name=name, out_shape=sems + [pltpu.HBM(t.shape, t.dtype) for t in bufs] + [TOKEN],
        in_specs=[HBM] * n, out_specs=[SEM] * (2 * n) + [HBM] * n + [pl.BlockSpec(memory_space=pltpu.VMEM)],
        input_output_aliases={a: 2 * n + a for a in range(n)}, compiler_params=pltpu.CompilerParams(has_side_effects=EFFECT),
    )(*[_in_hbm(t) for t in bufs])
    return [(res[2 * n + a], res[a], res[n + a]) for a in range(n)], res[3 * n]


def _gather_wait(state, after, *, name):
    buf, send, recv = state

    def body(buf_ref, send_ref, recv_ref, after_ref, out_ref):
        for cp in _gather_copies(buf_ref, send_ref, recv_ref):
            cp.wait_send()
            cp.wait_recv()

    return pl.pallas_call(
        body, name=name, out_shape=pltpu.HBM(buf.shape, buf.dtype), in_specs=[HBM, SEM, SEM, ANY], out_specs=HBM,
        input_output_aliases={0: 0}, compiler_params=pltpu.CompilerParams(has_side_effects=EFFECT),
    )(buf, send, recv, after)


def _chip_copies(src, land, send, recv):
    _x, _y, c, others = _place()
    return [_remote(src.at[2 * px + py], land.at[j], send.at[j], recv.at[j], (px, py, c)) for j, (px, py) in enumerate(others)]


def _chip_start(partial, *, name):
    def body(src, land, send, recv, _src_thru, _land_thru, token):
        for cp in _chip_copies(src, land, send, recv):
            cp.start()
        token[...] = jnp.zeros_like(token)

    land_shape = (3,) + partial.shape[1:]
    sem = pltpu.SemaphoreType.DMA((3,))
    send, recv, src, land, token = pl.pallas_call(
        body, name=name, out_shape=[sem, sem, pltpu.HBM(partial.shape, partial.dtype), pltpu.HBM(land_shape, partial.dtype), TOKEN],
        in_specs=[HBM, HBM], out_specs=[SEM, SEM, HBM, HBM, pl.BlockSpec(memory_space=pltpu.VMEM)],
        input_output_aliases={0: 2, 1: 3}, compiler_params=pltpu.CompilerParams(has_side_effects=EFFECT),
    )(_in_hbm(partial), _in_hbm(lax.empty(land_shape, partial.dtype)))
    return (src, land, send, recv), token


def _chip_wait(state, after, *, name):
    src, land, send, recv = state

    def body(src_ref, land_ref, send_ref, recv_ref, after_ref, _src_out, _land_out):
        for cp in _chip_copies(src_ref, land_ref, send_ref, recv_ref):
            cp.wait_send()
            cp.wait_recv()

    return pl.pallas_call(
        body, name=name, out_shape=[pltpu.HBM(src.shape, src.dtype), pltpu.HBM(land.shape, land.dtype)],
        in_specs=[HBM, HBM, SEM, SEM, ANY], out_specs=[HBM, HBM], input_output_aliases={0: 0, 1: 1},
        compiler_params=pltpu.CompilerParams(has_side_effects=EFFECT),
    )(src, land, send, recv, after)[1]


SIBLING_ID = 1


def _split_start(bufs, copies, n_copies, *, name, sibling_only=False):
    n = len(bufs)

    def body(*refs):
        if sibling_only:
            x, y, c, _o = _place()
            barrier = pltpu.get_barrier_semaphore()
            pl.semaphore_signal(barrier, inc=1, device_id=(x, y, 1 - c), device_id_type=MESH)
            pl.semaphore_wait(barrier, 1)
        for cp in copies(refs[:n], refs[n], refs[n + 1]):
            cp.start()
        refs[-1][...] = jnp.zeros_like(refs[-1])

    sem = pltpu.SemaphoreType.DMA((n_copies,))
    res = pl.pallas_call(
        body, name=name, out_shape=[sem, sem] + [pltpu.HBM(t.shape, t.dtype) for t in bufs] + [TOKEN],
        in_specs=[HBM] * n, out_specs=[SEM, SEM] + [HBM] * n + [pl.BlockSpec(memory_space=pltpu.VMEM)],
        input_output_aliases={a: 2 + a for a in range(n)},
        compiler_params=pltpu.CompilerParams(has_side_effects=EFFECT, collective_id=SIBLING_ID if sibling_only else None),
    )(*[_in_hbm(t) for t in bufs])
    return (list(res[2:2 + n]), res[0], res[1]), res[-1]


def _split_wait(state, copies, after, *, name):
    bufs, send, recv = state
    n = len(bufs)

    def body(*refs):
        for cp in copies(refs[:n], refs[n], refs[n + 1]):
            cp.wait_send()
            cp.wait_recv()

    return list(pl.pallas_call(
        body, name=name, out_shape=[pltpu.HBM(t.shape, t.dtype) for t in bufs], in_specs=[HBM] * n + [SEM, SEM, ANY],
        out_specs=[HBM] * n, input_output_aliases={a: a for a in range(n)},
        compiler_params=pltpu.CompilerParams(has_side_effects=EFFECT),
    )(*bufs, send, recv, after))


def _hand_over_copies(refs, send, recv):
    x, y, c, others = _place()
    half = refs[0].shape[1] // 2
    got = [refs[0].at[2 * px + py, pl.ds(c * half, half)] for px, py in others]
    return [_remote(got[j], got[j], send.at[j], recv.at[j], (x, y, 1 - c)) for j in range(3)]


def _pair_copies(refs, send, recv):
    x, y, c, _o = _place()
    half = refs[0].shape[1] // 2
    return [_remote(refs[0].at[:, pl.ds((1 - c) * half, half), :], refs[1], send.at[0], recv.at[0], (x, y, 1 - c))]


def _share_copies(refs, send, recv):
    x, y, c, _o = _place()
    return [_remote(refs[0].at[:, c], refs[0].at[:, c], send.at[0], recv.at[0], (x, y, 1 - c))]


def _small_copies(refs, send, recv):
    x, y, c, others = _place()
    peers = [(x, y, 1 - c)] + [(px, py, pc) for px, py in others for pc in (c, 1 - c)]
    slot = refs[1].at[4 * x + 2 * y + c]
    return [_remote(refs[0], slot, send.at[k], recv.at[k], peer) for k, peer in enumerate(peers)]


def _row_tile(rows, cols):
    return max(t for t in range(16, rows + 1, 16) if rows % t == 0 and (t * cols * 4 <= (4 << 20) or t == 16))


def _pair_sum(grad, theirs, place, *, name):
    _, r, cols = grad.shape
    r2 = r // 2
    tr = _row_tile(r2, cols)
    nr = r2 // tr

    def body(place_ref, g_ref, t_ref, all_ref):
        all_ref[...] = (g_ref[...].astype(F32) + t_ref[...].astype(F32)).astype(all_ref.dtype)

    def other(k, pr):
        return k + (k >= pr[1]).astype(jnp.int32)

    grid_spec = pltpu.PrefetchScalarGridSpec(
        num_scalar_prefetch=1, grid=(N_CHIPS - 1, nr),
        in_specs=[pl.BlockSpec((None, tr, cols), lambda k, i, pr: (other(k, pr), pr[0] * nr + i, 0)),
                  pl.BlockSpec((None, tr, cols), lambda k, i, pr: (other(k, pr), i, 0))],
        out_specs=pl.BlockSpec((None, tr, cols), lambda k, i, pr: (other(k, pr), i, 0)),
    )
    return pl.pallas_call(
        body, name=name, grid_spec=grid_spec, out_shape=jax.ShapeDtypeStruct((N_CHIPS, r2, cols), BF16),
        compiler_params=_params(("parallel", "parallel"), VMEM_BIG),
    )(place, grad, theirs)


def _chip_sum(grad, theirs, got, place, buf, layer, depth, *, name):
    _, r, cols = grad.shape
    r2 = r // 2
    tr = _row_tile(r2, cols)
    nr = r2 // tr

    def body(place_ref, g_ref, t_ref, got_ref, *rest):
        own = g_ref[...].astype(F32) + t_ref[...].astype(F32)
        rest[-1][...] = ((own + got_ref[0].astype(F32)) + got_ref[1].astype(F32)) + got_ref[2].astype(F32)

    in_specs = [pl.BlockSpec((None, tr, cols), lambda i, pr: (pr[1], pr[0] * nr + i, 0)),
                pl.BlockSpec((None, tr, cols), lambda i, pr: (pr[1], i, 0)),
                pl.BlockSpec((3, tr, cols), lambda i, pr: (0, i, 0))]
    args = [place, grad, theirs, got]
    if buf is not None:
        in_specs.append(ANY)
        args.append(buf)
    grid_spec = pltpu.PrefetchScalarGridSpec(
        num_scalar_prefetch=1, grid=(nr,), in_specs=in_specs,
        out_specs=pl.BlockSpec((None, None, tr, cols), lambda i, pr: (layer, pr[0], i, 0)),
    )
    return pl.pallas_call(
        body, name=name, grid_spec=grid_spec, out_shape=jax.ShapeDtypeStruct((depth, 2, r2, cols), F32),
        input_output_aliases={} if buf is None else {4: 0}, compiler_params=_params(("parallel",), VMEM_BIG),
    )(*args)


def _sum_devices(parts, own, place, *, name):
    _, rows, cols = parts.shape
    tr = rows if N_DEV * rows * cols * 4 <= (16 << 20) else _pick(rows, (256, 128, 64, 32, 16, 8))

    def body(place_ref, p_ref, own_ref, out_ref):
        me = 2 * place_ref[1] + place_ref[0]
        acc = None
        for dev in range(N_DEV):
            term = jnp.where(me == dev, own_ref[...], p_ref[dev])
            acc = term if acc is None else acc + term
        out_ref[...] = acc

    grid_spec = pltpu.PrefetchScalarGridSpec(
        num_scalar_prefetch=1, grid=(rows // tr,),
        in_specs=[pl.BlockSpec((N_DEV, tr, cols), lambda i, pr: (0, i, 0)), pl.BlockSpec((tr, cols), lambda i, pr: (i, 0))],
        out_specs=pl.BlockSpec((tr, cols), lambda i, pr: (i, 0)),
    )
    return pl.pallas_call(
        body, name=name, grid_spec=grid_spec, out_shape=jax.ShapeDtypeStruct((rows, cols), F32),
        compiler_params=_params(("parallel",), VMEM_BIG),
    )(place, parts, own)


def _pack(parts):
    rows = []
    for t in parts:
        flat = t.reshape(-1, 128)
        pad = (-flat.shape[0]) % 8
        rows.append(jnp.pad(flat, ((0, pad), (0, 0))) if pad else flat)
    return jnp.concatenate(rows, axis=0)


def _unpack(pack, shapes):
    out, r0 = [], 0
    for shp in shapes:
        n = math.prod(shp) // 128
        out.append(pack[r0:r0 + n].reshape(shp))
        r0 += n + (-n) % 8
    return out


SMALL = ["attn_norm_g", "q_norm_g", "k_norm_g", "sgu_norm_g", "sgu_w", "sgu_b", "out_norm_a_g", "out_norm_b_g",
         "ffn_norm_g", "conv_b"]
BIG = ["w_in", "w_out", "w_up", "w_down"]
ORDER = ["attn_norm_g", "w_in", "q_norm_g", "k_norm_g", "sgu_norm_g", "sgu_w", "sgu_b", "out_norm_a_g", "out_norm_b_g",
         "w_out", "ffn_norm_g", "w_up", "conv_w", "conv_b", "w_down"]


def kernel(x, attn_norm_g, w_in, q_norm_g, k_norm_g, sgu_norm_g, sgu_w, sgu_b, out_norm_a_g, out_norm_b_g, w_out, ffn_norm_g, w_up, conv_w, conv_b, w_down, loss_target, m_attn_norm_g, m_w_in, m_q_norm_g, m_k_norm_g, m_sgu_norm_g, m_sgu_w, m_sgu_b, m_out_norm_a_g, m_out_norm_b_g, m_w_out, m_ffn_norm_g, m_w_up, m_conv_w, m_conv_b, m_w_down, v_attn_norm_g, v_w_in, v_q_norm_g, v_k_norm_g, v_sgu_norm_g, v_sgu_w, v_sgu_b, v_out_norm_a_g, v_out_norm_b_g, v_w_out, v_ffn_norm_g, v_w_up, v_conv_w, v_conv_b, v_w_down):
    W = dict(attn_norm_g=attn_norm_g, w_in=w_in, q_norm_g=q_norm_g, k_norm_g=k_norm_g, sgu_norm_g=sgu_norm_g, sgu_w=sgu_w,
             sgu_b=sgu_b, out_norm_a_g=out_norm_a_g, out_norm_b_g=out_norm_b_g, w_out=w_out, ffn_norm_g=ffn_norm_g, w_up=w_up,
             conv_w=conv_w, conv_b=conv_b, w_down=w_down)
    M = dict(attn_norm_g=m_attn_norm_g, w_in=m_w_in, q_norm_g=m_q_norm_g, k_norm_g=m_k_norm_g, sgu_norm_g=m_sgu_norm_g,
             sgu_w=m_sgu_w, sgu_b=m_sgu_b, out_norm_a_g=m_out_norm_a_g, out_norm_b_g=m_out_norm_b_g, w_out=m_w_out,
             ffn_norm_g=m_ffn_norm_g, w_up=m_w_up, conv_w=m_conv_w, conv_b=m_conv_b, w_down=m_w_down)
    V = dict(attn_norm_g=v_attn_norm_g, w_in=v_w_in, q_norm_g=v_q_norm_g, k_norm_g=v_k_norm_g, sgu_norm_g=v_sgu_norm_g,
             sgu_w=v_sgu_w, sgu_b=v_sgu_b, out_norm_a_g=v_out_norm_a_g, out_norm_b_g=v_out_norm_b_g, w_out=v_w_out,
             ffn_norm_g=v_ffn_norm_g, w_up=v_w_up, conv_w=v_conv_w, conv_b=v_conv_b, w_down=v_w_down)
    depth = w_in.shape[0]
    s, d = x.shape[1], x.shape[2]
    n_heads = out_norm_a_g.shape[1]
    core = lax.axis_index("c")
    chip = 2 * lax.axis_index("x") + lax.axis_index("y")
    place = jnp.stack([core, chip]).astype(jnp.int32)
    xs = x.reshape(s, d)

    f_local = conv_w.shape[2]
    taps = lax.dynamic_update_slice(jnp.zeros((N_CHIPS, 16, f_local), F32), conv_w.reshape(1, depth * 3, f_local),
                                    (chip, 0, 0))
    order = [(l, n) for l in range(depth) for n in BIG]
    first, token = _gather_start([_cast_into(w_in, 0, place, name="cast_w_in"), taps], name="gather_start_first")
    rest, token = _gather_start([_cast_into(W[n], l, place, dep=token, name=f"cast_{n}") for l, n in order[1:]],
                                name="gather_start_rest")
    states = dict(zip(order, [first[0]] + rest))

    states["taps"] = first[1]

    def landed(key, after, tag):
        buf = _gather_wait(states[key], after, name=f"gather_wait_{tag}")
        return _split_start([buf], _hand_over_copies, 3, sibling_only=True, name=f"hand_over_{tag}")

    def whole(state, after, tag):
        return _split_wait(state, _hand_over_copies, after, name=f"hand_over_wait_{tag}")[0]

    saved, full = [], []
    cur = xs
    for l in range(depth):
        gain = attn_norm_g + token[0, 0] if l == 0 else attn_norm_g
        h = _rmsnorm_fwd(cur, gain, l, name="attn_norm")
        if l == 0:
            ho_in, _ = landed((0, "w_in"), h, "w_in")
        w_in_l = whole(ho_in, h, "w_in")
        if l == 0:
            ho_taps, token = landed("taps", w_in_l, "taps")
        p = _mm(h, w_in_l, "nn", b_split=N_CHIPS, caps=(2048, 256, 2048), dep=token, name="proj_in")
        if l == 0:
            taps = whole(ho_taps, p, "taps")[:, :depth * 3].reshape(N_CHIPS, depth, 3, f_local)
            cw_full = jnp.transpose(taps, (1, 2, 0, 3)).reshape(depth, 3, N_CHIPS * f_local)
        mix, o_raw, lsum = _attn_fwd(p, q_norm_g, k_norm_g, out_norm_a_g, l, n_heads, name="attn_fwd")
        ho_out, token = landed((l, "w_out"), mix, "w_out")
        mix = _sgu_fwd(p, mix, sgu_w, sgu_b, sgu_norm_g + token[0, 0], out_norm_b_g, l, n_heads, name="sgu_fwd")
        w_out_l = whole(ho_out, mix, "w_out").reshape(-1, d)
        ho_up, token = landed((l, "w_up"), w_out_l, "w_up")
        x1 = _mm(mix, w_out_l, "nn", a_split=2, res=cur, caps=(2048, 512, 1024), dep=token, name="proj_out")
        h2 = _rmsnorm_fwd(x1, ffn_norm_g, l, name="ffn_norm")
        w_up_l = whole(ho_up, h2, "w_up")
        up_gate, up_val, act = _up_conv_fwd(h2, w_up_l, cw_full, conv_b, l, name="ffn_up_conv")
        ho_down, token = landed((l, "w_down"), act, "w_down")
        w_down_l = whole(ho_down, act, "w_down").reshape(-1, d)
        if l + 1 < depth:
            ho_in, token = landed((l + 1, "w_in"), w_down_l, "w_in")
        x2 = _mm(act, w_down_l, "nn", res=x1, caps=(1024, 512, 2816), dep=token, name="ffn_down")
        full.append(dict(w_in=w_in_l, w_out=w_out_l, w_up=w_up_l, w_down=w_down_l))
        saved.append(dict(x0=cur, h=h, p=p, o_raw=o_raw, lsum=lsum, mix=mix, x1=x1, h2=h2, up_gate=up_gate, up_val=up_val,
                          act=act))
        cur = x2

    dx, dxb, sq = _loss_head(cur, loss_target.reshape(s, d), name="loss_head")
    loss = lax.psum(sq[0, 0] * (0.5 / d), ("x", "y", "c"))

    small_grads = {n: [None] * depth for n in SMALL + ["conv_w"]}
    def pair_begin(n, grad):
        land = lax.empty((N_CHIPS, grad.shape[1] // 2, grad.shape[2]), grad.dtype)
        return _split_start([grad, land], _pair_copies, 1, sibling_only=True, name=f"pair_start_{n}")

    def chip_begin(n, state, after):
        grad, theirs = _split_wait(state, _pair_copies, after, name=f"pair_wait_{n}")
        state, tok = _chip_start(_pair_sum(grad, theirs, place, name=f"pair_sum_{n}"), name=f"chip_start_{n}")
        return (grad, theirs, state), tok

    pending = {}
    for l in reversed(range(depth)):
        fw, sv = full[l], saved[l]
        g_down = _mm(sv["act"], dxb, "tn", caps=(512, 2048, 2048), out_dtype=BF16, name="g_down")
        pair, tok = pair_begin("w_down", g_down.reshape(N_CHIPS, -1, d))
        dup, dcw, dcb = _conv_bwd(sv["up_gate"], sv["up_val"], dxb, fw["w_down"], cw_full, conv_b + tok[0, 0], l,
                                  name="conv_bwd")
        pending[(l, "w_down")], tok = chip_begin("w_down", pair, dup)
        g_up = _mm(sv["h2"], dup, "tn", b_split=2, o_split=N_CHIPS, caps=(2048, 256, 2048), out_dtype=BF16, dep=tok,
                   name="g_up")
        pair, tok = pair_begin("w_up", g_up)
        dh2 = _mm(dup, fw["w_up"], "nt", a_split=2, b_split=N_CHIPS, caps=(2048, 512, 2816), dep=tok, name="d_h2")
        pending[(l, "w_up")], tok = chip_begin("w_up", pair, dh2)
        dx1, dx1b, dg_ffn = _rmsnorm_bwd(sv["x1"], ffn_norm_g + tok[0, 0], l, dh2, dx, name="ffn_norm_bwd")
        dmix = _mm(dx1b, fw["w_out"], "nt", caps=(2048, 512, 2048), name="d_mix")
        g_out = _mm(sv["mix"], dx1b, "tn", a_split=2, caps=(512, 2048, 2048), out_dtype=BF16, name="g_out")
        pair, tok = pair_begin("w_out", g_out.reshape(N_CHIPS, -1, d))
        dqkv, dgq, dgk, dgoa = _attn_bwd(sv["p"], sv["o_raw"], sv["lsum"], dmix, q_norm_g + tok[0, 0], k_norm_g,
                                         out_norm_a_g, l, n_heads, name="attn_bwd")
        pending[(l, "w_out")], tok = chip_begin("w_out", pair, dqkv)
        duv, dsw, dsb, dgv, dgob = _sgu_bwd(sv["p"], dmix, sgu_w, sgu_b, sgu_norm_g + tok[0, 0], out_norm_b_g, l,
                                            n_heads, name="sgu_bwd")
        dp = jnp.concatenate([dqkv[0], dqkv[1], dqkv[2], duv[0], duv[1]], axis=1)
        g_in = _mm(sv["h"], dp, "tn", o_split=N_CHIPS, caps=(2048, 256, 2048), out_dtype=BF16, name="g_in")
        pair, tok = pair_begin("w_in", g_in)
        dh = _mm(dp, fw["w_in"], "nt", b_split=N_CHIPS, caps=(2048, 1024, 1280), dep=tok, name="d_h")
        pending[(l, "w_in")], tok = chip_begin("w_in", pair, dh)
        dx, dxb, dg_attn = _rmsnorm_bwd(sv["x0"], attn_norm_g + tok[0, 0], l, dh, dx1, name="attn_norm_bwd")

        small_grads["attn_norm_g"][l] = dg_attn.reshape(d)
        small_grads["q_norm_g"][l] = jnp.sum(dgq, axis=(0, 1))
        small_grads["k_norm_g"][l] = jnp.sum(dgk, axis=(0, 1))
        small_grads["sgu_norm_g"][l] = dgv.reshape(-1, BLK)
        small_grads["sgu_w"][l] = dsw
        small_grads["sgu_b"][l] = dsb.reshape(-1, BLK)
        small_grads["out_norm_a_g"][l] = dgoa.reshape(-1, BLK)
        small_grads["out_norm_b_g"][l] = dgob.reshape(-1, BLK)
        small_grads["ffn_norm_g"][l] = dg_ffn.reshape(d)
        small_grads["conv_b"][l] = dcb[:, 0, :].reshape(-1)
        small_grads["conv_w"][l] = jnp.transpose(dcw[:, :3, :], (1, 0, 2)).reshape(3, -1)

    names = SMALL + ["conv_w"]
    pack = _pack([jnp.stack(small_grads[n]) for n in names])
    small, tok = _split_start([pack, lax.empty((N_DEV,) + pack.shape, F32)], _small_copies, N_DEV - 1, name="small_start")
    G, D_, NM, NV = {}, {}, {}, {}
    after, prev = tok, None
    for n in ("w_down", "w_up", "w_out", "w_in"):
        buf = None
        for l in reversed(range(depth)):
            grad, theirs, state = pending[(l, n)]
            got = _chip_wait(state, after, name=f"chip_wait_{n}")
            buf = _chip_sum(grad, theirs, got, place, buf, l, depth, name=f"chip_sum_{n}")
            after = buf
        share, tok = _split_start([buf], _share_copies, 1, sibling_only=True, name=f"share_start_{n}")
        if prev is not None:
            D_[prev], NM[prev], NV[prev], G[prev] = _adamw(W[prev], G[prev], M[prev], V[prev], dep=tok, copy_g=True,
                                                           name=f"adamw_{prev}")
            after = NV[prev]
        G[n] = _split_wait(share, _share_copies, after, name=f"share_wait_{n}")[0].reshape(W[n].shape)
        after, prev = G[n], n
    D_[prev], NM[prev], NV[prev], G[prev] = _adamw(W[prev], G[prev], M[prev], V[prev], copy_g=True, name=f"adamw_{prev}")

    pack, parts = _split_wait(small, _small_copies, NV[prev], name="small_wait")
    total = _sum_devices(parts, pack, place, name="sum_small")
    f_full = conv_b.shape[1]
    shapes = [W[n].shape for n in SMALL] + [(depth, 3, f_full)]
    for n, t in zip(names, _unpack(total, shapes)):
        G[n] = t
    G["conv_w"] = lax.dynamic_slice_in_dim(G["conv_w"], chip * f_local, f_local, axis=2)

    D_["conv_w"], NM["conv_w"], NV["conv_w"] = _adamw(conv_w, G["conv_w"], m_conv_w, v_conv_w, name="adamw_conv_w")
    small_shapes = [W[n].shape for n in SMALL]
    res = _adamw(_pack([W[n] for n in SMALL]), _pack([G[n] for n in SMALL]), _pack([M[n] for n in SMALL]),
                 _pack([V[n] for n in SMALL]), name="adamw_small")
    for dst, t in zip((D_, NM, NV), res):
        for n, u in zip(SMALL, _unpack(t, small_shapes)):
            dst[n] = u

    return (loss, dx.reshape(x.shape), *[G[n] for n in ORDER], *[D_[n] for n in ORDER], *[NM[n] for n in ORDER],
            *[NV[n] for n in ORDER])
```
